```python
import jax, jax.numpy as jnp
from jax import lax
import numpy as np

D_MODEL = 2048
BATCH = 4
SEQ = 2048
DEPTH = 1

FOX_HEADS = 8
FOX_HEAD_DIM = 128
FOX_WIDTH = FOX_HEADS * FOX_HEAD_DIM
NSA_HEADS = 8
NSA_KV_GROUPS = 2
NSA_HPG = NSA_HEADS // NSA_KV_GROUPS
NSA_QK_DIM = 192
NSA_V_DIM = 128
NSA_WIDTH = NSA_HEADS * NSA_V_DIM
CMP_BLOCK = 32
CMP_STRIDE = 16
CMP_HIDDEN = 256
SEL_BLOCK = 64
SEL_TOPK = 16
SEL_LOCAL = 2
FORCE_SCORE = 1.0e4
WINDOW = 512
Q_BLOCK = 128
SEL_Q_CHUNK = 32
KV_K = NSA_KV_GROUPS * NSA_QK_DIM
KV_V = NSA_KV_GROUPS * NSA_V_DIM
D_FF = -(-(8 * D_MODEL) // (3 * 256)) * 256
RMS_EPS = 1e-6
IN_SPLITS = (FOX_WIDTH, FOX_WIDTH, FOX_WIDTH, FOX_HEADS,
             NSA_HEADS * NSA_QK_DIM, KV_K, KV_V, KV_K, KV_V, KV_K, KV_V,
             3 * NSA_HEADS, D_MODEL, D_MODEL)
D_IN = sum(IN_SPLITS)

kernel_name = "hybrid_fox_nsa_gated_parallel_block"


def _rms(x, gain):
    xf = x.astype(jnp.float32)
    y = xf * lax.rsqrt(jnp.mean(xf * xf, axis=-1, keepdims=True) + RMS_EPS)
    return (y * gain.astype(jnp.float32)).astype(x.dtype)


def _masked_softmax(s, mask):
    s = jnp.where(mask, s, -jnp.inf)
    m = jnp.max(s, axis=-1, keepdims=True)
    m = jnp.where(jnp.isfinite(m), m, 0.0)
    e = jnp.where(mask, jnp.exp(s - m), 0.0)
    return e / jnp.maximum(jnp.sum(e, axis=-1, keepdims=True), 1e-30)


def _alibi_slopes(n):
    return jnp.exp2(-8.0 * jnp.arange(1, n + 1, dtype=jnp.float32) / n)


def _fox_attention(q, k, v, log_f):
    B, T, H, Dh = q.shape
    nb = T // Q_BLOCK
    c = jnp.cumsum(log_f, axis=1).transpose(0, 2, 1)
    kh = k.transpose(0, 2, 1, 3)
    vh = v.transpose(0, 2, 1, 3)
    qb = q.reshape(B, nb, Q_BLOCK, H, Dh).transpose(1, 0, 3, 2, 4)
    cb = c.reshape(B, H, nb, Q_BLOCK).transpose(2, 0, 1, 3)
    scale = Dh ** -0.5
    kpos = jnp.arange(T)

    def block(args):
        qi, ci, i = args
        qpos = i * Q_BLOCK + jnp.arange(Q_BLOCK)
        s = jnp.einsum('bhqd,bhkd->bhqk', qi, kh, preferred_element_type=jnp.float32) * scale
        s = s + ci[..., :, None] - c[:, :, None, :]
        p = _masked_softmax(s, kpos[None, :] <= qpos[:, None])
        return jnp.einsum('bhqk,bhkd->bhqd', p.astype(vh.dtype), vh)

    o = lax.map(block, (qb, cb, jnp.arange(nb)))
    return o.transpose(1, 0, 3, 2, 4).reshape(B, T, H * Dh)


def _compress(z, pe, w1, w2):
    B, T, G, D = z.shape
    r = CMP_BLOCK // CMP_STRIDE
    ch = z.reshape(B, T // CMP_STRIDE, CMP_STRIDE, G, D)
    nc = T // CMP_STRIDE - r + 1
    blk = jnp.concatenate([ch[:, i:i + nc] for i in range(r)], axis=2)
    blk = blk + pe[None, None, :, None, :]
    flat = blk.transpose(0, 1, 3, 2, 4).reshape(B, nc, G, CMP_BLOCK * D)
    return jax.nn.silu(flat @ w1) @ w2


def _overlap_matrix(nc, ns):
    i = np.arange(nc)[:, None]
    j = np.arange(ns)[None, :]
    lo = np.maximum(i * CMP_STRIDE, j * SEL_BLOCK)
    hi = np.minimum(i * CMP_STRIDE + CMP_BLOCK, (j + 1) * SEL_BLOCK)
    return (np.maximum(hi - lo, 0) / CMP_STRIDE).astype(np.float32)


def _nsa_attention(q, kc_raw, vc_raw, ks_raw, vs_raw, kw_raw, vw_raw, gate_logits,
                   q_gain, kc_gain, ks_gain, kw_gain,
                   cmp_pe_k, cmp_w1_k, cmp_w2_k, cmp_pe_v, cmp_w1_v, cmp_w2_v):
    B, T, _ = q.shape
    G, HPG, Dk, Dv = NSA_KV_GROUPS, NSA_HPG, NSA_QK_DIM, NSA_V_DIM
    scale = Dk ** -0.5
    slopes = _alibi_slopes(NSA_HEADS).reshape(G, HPG)
    tpos = jnp.arange(T)
    qn = _rms(q.reshape(B, T, G, HPG, Dk), q_gain)
    qc = qn.transpose(0, 2, 3, 1, 4)

    kc = _rms(_compress(kc_raw.reshape(B, T, G, Dk), cmp_pe_k, cmp_w1_k, cmp_w2_k), kc_gain)
    vc = _compress(vc_raw.reshape(B, T, G, Dv), cmp_pe_v, cmp_w1_v, cmp_w2_v)
    nc = kc.shape[1]
    cend = jnp.arange(nc) * CMP_STRIDE + CMP_BLOCK - 1
    dist_c = tpos[:, None] - cend[None, :]
    s_c = jnp.einsum('btghd,bngd->bghtn', qn, kc, preferred_element_type=jnp.float32) * scale
    s_c = s_c - slopes[None, :, :, None, None] * dist_c.astype(jnp.float32)
    p_cmp = _masked_softmax(s_c, dist_c >= 0)
    o_cmp = jnp.einsum('bghtn,bngd->btghd', p_cmp.astype(vc.dtype), vc)

    ns = T // SEL_BLOCK
    n_sel = min(SEL_TOPK, ns)
    imp = jnp.einsum('bgtn,nj->bgtj', p_cmp.sum(axis=2), jnp.asarray(_overlap_matrix(nc, ns)))
    cur = tpos // SEL_BLOCK
    jidx = jnp.arange(ns)
    back = cur[:, None] - jidx[None, :]
    eligible = back >= 0
    forced = (jidx[None, :] == 0) | (eligible & (back < SEL_LOCAL))
    score = jnp.where(eligible, jnp.where(forced, FORCE_SCORE, imp), -1.0)
    top_score, idx = lax.top_k(score, n_sel)
    valid = top_score >= 0.0

    ks_b = _rms(ks_raw.reshape(B, T, G, Dk), ks_gain).transpose(0, 2, 1, 3).reshape(B, G, ns, SEL_BLOCK, Dk)
    vs_b = vs_raw.reshape(B, T, G, Dv).transpose(0, 2, 1, 3).reshape(B, G, ns, SEL_BLOCK, Dv)
    nq = T // SEL_Q_CHUNK
    q_ch = qc.reshape(B, G, HPG, nq, SEL_Q_CHUNK, Dk).transpose(3, 0, 1, 2, 4, 5)
    idx_ch = idx.reshape(B, G, nq, SEL_Q_CHUNK, n_sel).transpose(2, 0, 1, 3, 4)
    val_ch = valid.reshape(B, G, nq, SEL_Q_CHUNK, n_sel).transpose(2, 0, 1, 3, 4)
    bi = jnp.arange(B)[:, None, None, None]
    gi = jnp.arange(G)[None, :, None, None]
    in_blk = jnp.arange(SEL_BLOCK)

    def sel_chunk(args):
        qi, ii, vi, c = args
        kg = ks_b[bi, gi, ii]
        vg = vs_b[bi, gi, ii].reshape(B, G, SEL_Q_CHUNK, n_sel * SEL_BLOCK, Dv)
        qpos = c * SEL_Q_CHUNK + jnp.arange(SEL_Q_CHUNK)
        kpos = ii[..., None] * SEL_BLOCK + in_blk
        dist = qpos[None, None, :, None, None] - kpos
        mask = ((dist >= 0) & vi[..., None]).reshape(B, G, 1, SEL_Q_CHUNK, n_sel * SEL_BLOCK)
        s = jnp.einsum('bghqd,bgqskd->bghqsk', qi, kg, preferred_element_type=jnp.float32) * scale
        s = s - slopes[None, :, :, None, None, None] * dist[:, :, None].astype(jnp.float32)
        p = _masked_softmax(s.reshape(B, G, HPG, SEL_Q_CHUNK, n_sel * SEL_BLOCK), mask)
        return jnp.einsum('bghqk,bgqkd->bghqd', p.astype(vg.dtype), vg)

    o_slc = lax.map(sel_chunk, (q_ch, idx_ch, val_ch, jnp.arange(nq)))
    o_slc = o_slc.transpose(1, 0, 4, 2, 3, 5).reshape(B, T, G, HPG, Dv)

    nb = T // Q_BLOCK
    r = WINDOW // Q_BLOCK
    kw = _rms(kw_raw.reshape(B, T, G, Dk), kw_gain).transpose(0, 2, 1, 3)
    vw = vw_raw.reshape(B, T, G, Dv).transpose(0, 2, 1, 3)
    pad = ((0, 0), (0, 0), (WINDOW, 0), (0, 0))
    kwb = jnp.pad(kw, pad).reshape(B, G, nb + r, Q_BLOCK, Dk)
    vwb = jnp.pad(vw, pad).reshape(B, G, nb + r, Q_BLOCK, Dv)
    band_k = jnp.concatenate([kwb[:, :, i:i + nb] for i in range(r + 1)], axis=3)
    band_v = jnp.concatenate([vwb[:, :, i:i + nb] for i in range(r + 1)], axis=3)
    kb_len = (r + 1) * Q_BLOCK
    dist_w = jnp.arange(Q_BLOCK)[:, None] - jnp.arange(kb_len)[None, :] + WINDOW
    kpos_w = jnp.arange(nb)[:, None] * Q_BLOCK - WINDOW + jnp.arange(kb_len)[None, :]
    mask_w = (dist_w >= 0)[None] & (dist_w < WINDOW)[None] & (kpos_w >= 0)[:, None, :]
    q_w = qc.reshape(B, G, HPG, nb, Q_BLOCK, Dk)
    s_w = jnp.einsum('bghnqd,bgnkd->bghnqk', q_w, band_k, preferred_element_type=jnp.float32) * scale
    s_w = s_w - slopes[None, :, :, None, None, None] * dist_w.astype(jnp.float32)
    p_w = _masked_softmax(s_w, mask_w)
    o_win = jnp.einsum('bghnqk,bgnkd->bghnqd', p_w.astype(band_v.dtype), band_v)
    o_win = o_win.reshape(B, G, HPG, T, Dv).transpose(0, 3, 1, 2, 4)

    g = jax.nn.sigmoid(gate_logits.astype(jnp.float32)).reshape(B, T, G, HPG, 3)
    o = g[..., 0:1] * o_cmp + g[..., 1:2] * o_slc + g[..., 2:3] * o_win
    return o.reshape(B, T, NSA_WIDTH).astype(q.dtype)


def setup_inputs(seed: int = 0) -> dict:
    key = jax.random.key(seed)
    ks = jax.random.split(key, 24)
    f32 = jnp.float32
    L = DEPTH

    def nrm(k, shape, fan_in):
        return jax.random.normal(k, shape, f32) * fan_in ** -0.5

    def gain(k, shape):
        return 1.0 + 0.02 * jax.random.normal(k, shape, f32)

    return {
        "x": jax.random.normal(ks[0], (BATCH, SEQ, D_MODEL), f32),
        "norm_attn": gain(ks[1], (L, D_MODEL)),
        "w_in": nrm(ks[2], (L, D_MODEL, D_IN), D_MODEL),
        "fox_f_bias": 3.0 + 0.1 * jax.random.normal(ks[3], (L, FOX_HEADS), f32),
        "fox_q_gain": gain(ks[4], (L, FOX_HEAD_DIM)),
        "fox_k_gain": gain(ks[5], (L, FOX_HEAD_DIM)),
        "nsa_q_gain": gain(ks[6], (L, NSA_QK_DIM)),
        "nsa_kc_gain": gain(ks[7], (L, NSA_QK_DIM)),
        "nsa_ks_gain": gain(ks[8], (L, NSA_QK_DIM)),
        "nsa_kw_gain": gain(ks[9], (L, NSA_QK_DIM)),
        "cmp_pe_k": 0.02 * jax.random.normal(ks[10], (L, CMP_BLOCK, NSA_QK_DIM), f32),
        "cmp_w1_k": nrm(ks[11], (L, CMP_BLOCK * NSA_QK_DIM, CMP_HIDDEN), CMP_BLOCK * NSA_QK_DIM),
        "cmp_w2_k": nrm(ks[12], (L, CMP_HIDDEN, NSA_QK_DIM), CMP_HIDDEN),
        "cmp_pe_v": 0.02 * jax.random.normal(ks[13], (L, CMP_BLOCK, NSA_V_DIM), f32),
        "cmp_w1_v": nrm(ks[14], (L, CMP_BLOCK * NSA_V_DIM, CMP_HIDDEN), CMP_BLOCK * NSA_V_DIM),
        "cmp_w2_v": nrm(ks[15], (L, CMP_HIDDEN, NSA_V_DIM), CMP_HIDDEN),
        "w_up_fox": nrm(ks[16], (L, FOX_WIDTH, D_MODEL), FOX_WIDTH),
        "w_up_nsa": nrm(ks[17], (L, NSA_WIDTH, D_MODEL), NSA_WIDTH),
        "w_out": nrm(ks[18], (L, D_MODEL, D_MODEL), D_MODEL),
        "norm_ffn": gain(ks[19], (L, D_MODEL)),
        "w_ffn_gate": nrm(ks[20], (L, D_MODEL, D_FF), D_MODEL),
        "w_ffn_up": nrm(ks[21], (L, D_MODEL, D_FF), D_MODEL),
        "w_ffn_down": nrm(ks[22], (L, D_FF, D_MODEL), D_FF),
    }


def reference(x, norm_attn, w_in, fox_f_bias, fox_q_gain, fox_k_gain,
              nsa_q_gain, nsa_kc_gain, nsa_ks_gain, nsa_kw_gain,
              cmp_pe_k, cmp_w1_k, cmp_w2_k, cmp_pe_v, cmp_w1_v, cmp_w2_v,
              w_up_fox, w_up_nsa, w_out, norm_ffn, w_ffn_gate, w_ffn_up, w_ffn_down):
    B, T, _ = x.shape
    points = tuple(int(p) for p in np.cumsum(IN_SPLITS)[:-1])
    for l in range(DEPTH):
        xn = _rms(x, norm_attn[l])
        (fq, fk, fv, f_logit, nq, kc, vc, ksl, vsl, kw, vw,
         nsa_gate, gate_a, gate_b) = jnp.split(xn @ w_in[l], points, axis=-1)

        fq = _rms(fq.reshape(B, T, FOX_HEADS, FOX_HEAD_DIM), fox_q_gain[l])
        fk = _rms(fk.reshape(B, T, FOX_HEADS, FOX_HEAD_DIM), fox_k_gain[l])
        fv = fv.reshape(B, T, FOX_HEADS, FOX_HEAD_DIM)
        log_f = jax.nn.log_sigmoid(f_logit.astype(jnp.float32) + fox_f_bias[l].astype(jnp.float32))
        o_a = _fox_attention(fq, fk, fv, log_f)

        o_b = _nsa_attention(nq, kc, vc, ksl, vsl, kw, vw, nsa_gate,
                             nsa_q_gain[l], nsa_kc_gain[l], nsa_ks_gain[l], nsa_kw_gain[l],
                             cmp_pe_k[l], cmp_w1_k[l], cmp_w2_k[l],
                             cmp_pe_v[l], cmp_w1_v[l], cmp_w2_v[l])

        merged = (jax.nn.sigmoid(gate_a) * (o_a @ w_up_fox[l])
                  + jax.nn.sigmoid(gate_b) * (o_b @ w_up_nsa[l]))
        x = x + (merged @ w_out[l]).astype(x.dtype)

        hn = _rms(x, norm_ffn[l])
        x = x + ((jax.nn.silu(hn @ w_ffn_gate[l]) * (hn @ w_ffn_up[l])) @ w_ffn_down[l]).astype(x.dtype)
    return x
```

```python
import functools

import numpy as np
import jax
import jax.numpy as jnp
from jax import lax
from jax.experimental import pallas as pl
from jax.experimental.pallas import tpu as pltpu

F32 = jnp.float32
BF16 = jnp.bfloat16

D_MODEL = 2048
FOX_HEADS = 8
FOX_HEAD_DIM = 128
FOX_WIDTH = FOX_HEADS * FOX_HEAD_DIM
NSA_HEADS = 8
NSA_KV_GROUPS = 2
NSA_HPG = NSA_HEADS // NSA_KV_GROUPS
NSA_QK_DIM = 192
NSA_QK_PAD = 256
NSA_V_DIM = 128
NSA_WIDTH = NSA_HEADS * NSA_V_DIM
CMP_BLOCK = 32
CMP_STRIDE = 16
CMP_HIDDEN = 256
SEL_BLOCK = 64
SEL_TOPK = 16
SEL_LOCAL = 2
FORCE_SCORE = 1.0e4
WINDOW = 512
KV_K = NSA_KV_GROUPS * NSA_QK_DIM
KV_V = NSA_KV_GROUPS * NSA_V_DIM
D_FF = -(-(8 * D_MODEL) // (3 * 256)) * 256
RMS_EPS = 1e-6
IN_SPLITS = (FOX_WIDTH, FOX_WIDTH, FOX_WIDTH, FOX_HEADS,
             NSA_HEADS * NSA_QK_DIM, KV_K, KV_V, KV_K, KV_V, KV_K, KV_V,
             3 * NSA_HEADS, D_MODEL, D_MODEL)

LANES = 128
NEG = -1.0e30
VMEM_LIMIT = 56 * 1024 * 1024

_NT = (((1,), (1,)), ((), ()))


def _cparams(sem):
    return pltpu.CompilerParams(dimension_semantics=sem, vmem_limit_bytes=VMEM_LIMIT)


def _rms_kernel(x_ref, g_ref, o_ref):
    x = x_ref[...]
    ms = jnp.mean(x * x, axis=-1, keepdims=True)
    o_ref[...] = (x * lax.rsqrt(ms + RMS_EPS) * g_ref[...]).astype(o_ref.dtype)


def _rmsnorm(x, gain, tm=512):
    m, d = x.shape
    return pl.pallas_call(
        _rms_kernel,
        grid=(m // tm,),
        in_specs=[pl.BlockSpec((tm, d), lambda i: (i, 0)),
                  pl.BlockSpec((1, d), lambda i: (0, 0))],
        out_specs=pl.BlockSpec((tm, d), lambda i: (i, 0)),
        out_shape=jax.ShapeDtypeStruct((m, d), BF16),
        compiler_params=_cparams(("arbitrary",)),
        name="rmsnorm",
    )(x, gain.reshape(1, d))


def _mm_kernel(a_ref, w_ref, o_ref):
    o_ref[...] = jnp.dot(a_ref[...], w_ref[...],
                         preferred_element_type=F32).astype(o_ref.dtype)


def _matmul(a, w, out_dtype, tm, tn, name):
    m, k = a.shape
    n = w.shape[1]
    return pl.pallas_call(
        _mm_kernel,
        grid=(m // tm, n // tn),
        in_specs=[pl.BlockSpec((tm, k), lambda i, j: (i, 0)),
                  pl.BlockSpec((k, tn), lambda i, j: (0, j))],
        out_specs=pl.BlockSpec((tm, tn), lambda i, j: (i, j)),
        out_shape=jax.ShapeDtypeStruct((m, n), out_dtype),
        compiler_params=_cparams(("arbitrary", "arbitrary")),
        name=name,
    )(a, w)


def _mm_norm_kernel(a_ref, w_ref, gain_ref, flag_ref, o_ref, *, group, count):
    y = jnp.dot(a_ref[...], w_ref[...], preferred_element_type=F32)
    for c in range(y.shape[1] // group):
        sl = slice(c * group, (c + 1) * group)
        yc = y[:, sl]
        ss = jnp.sum(yc * yc, axis=-1, keepdims=True)
        rs = lax.rsqrt(ss * (1.0 / count) + RMS_EPS)
        scale = jnp.where(flag_ref[:, sl] > 0.0, rs, 1.0)
        o_ref[:, sl] = (yc * scale * gain_ref[:, sl]).astype(o_ref.dtype)


def _matmul_norm(a, w, gain, flag, group, count, tm, tn, name):
    m, k = a.shape
    n = w.shape[1]
    return pl.pallas_call(
        functools.partial(_mm_norm_kernel, group=group, count=count),
        grid=(m // tm, n // tn),
        in_specs=[pl.BlockSpec((tm, k), lambda i, j: (i, 0)),
                  pl.BlockSpec((k, tn), lambda i, j: (0, j)),
                  pl.BlockSpec((1, tn), lambda i, j: (0, j)),
                  pl.BlockSpec((1, tn), lambda i, j: (0, j))],
        out_specs=pl.BlockSpec((tm, tn), lambda i, j: (i, j)),
        out_shape=jax.ShapeDtypeStruct((m, n), BF16),
        compiler_params=_cparams(("arbitrary", "arbitrary")),
        name=name,
    )(a, w, gain.reshape(1, n), flag.reshape(1, n))


def _mm_res_kernel(a_ref, w_ref, r_ref, o_ref):
    o_ref[...] = r_ref[...] + jnp.dot(a_ref[...], w_ref[...], preferred_element_type=F32)


def _matmul_residual(a, w, res, tm, tn, name):
    m, k = a.shape
    n = w.shape[1]
    return pl.pallas_call(
        _mm_res_kernel,
        grid=(m // tm, n // tn),
        in_specs=[pl.BlockSpec((tm, k), lambda i, j: (i, 0)),
                  pl.BlockSpec((k, tn), lambda i, j: (0, j)),
                  pl.BlockSpec((tm, tn), lambda i, j: (i, j))],
        out_specs=pl.BlockSpec((tm, tn), lambda i, j: (i, j)),
        out_shape=jax.ShapeDtypeStruct((m, n), F32),
        compiler_params=_cparams(("arbitrary", "arbitrary")),
        name=name,
    )(a, w, res)


def _mm_acc_res_kernel(a_ref, w_ref, r_ref, o_ref, acc_ref):
    kk = pl.program_id(2)

    @pl.when(kk == 0)
    def _():
        acc_ref[...] = jnp.zeros_like(acc_ref)

    acc_ref[...] += jnp.dot(a_ref[...], w_ref[...], preferred_element_type=F32)

    @pl.when(kk == pl.num_programs(2) - 1)
    def _():
        o_ref[...] = r_ref[...] + acc_ref[...]


def _matmul_acc_residual(a, w, res, tm, tn, tk, name):
    m, k = a.shape
    n = w.shape[1]
    return pl.pallas_call(
        _mm_acc_res_kernel,
        grid=(m // tm, n // tn, k // tk),
        in_specs=[pl.BlockSpec((tm, tk), lambda i, j, kk: (i, kk)),
                  pl.BlockSpec((tk, tn), lambda i, j, kk: (kk, j)),
                  pl.BlockSpec((tm, tn), lambda i, j, kk: (i, j))],
        out_specs=pl.BlockSpec((tm, tn), lambda i, j, kk: (i, j)),
        out_shape=jax.ShapeDtypeStruct((m, n), F32),
        scratch_shapes=[pltpu.VMEM((tm, tn), F32)],
        compiler_params=_cparams(("arbitrary", "arbitrary", "arbitrary")),
        name=name,
    )(a, w, res)


def _merge_kernel(xn_ref, oa_ref, ob_ref, wga_ref, wgb_ref, wuf_ref, wun_ref, o_ref):
    xn = xn_ref[...]
    ga = jax.nn.sigmoid(jnp.dot(xn, wga_ref[...], preferred_element_type=F32))
    ua = jnp.dot(oa_ref[...], wuf_ref[...], preferred_element_type=F32)
    acc = ga * ua
    gb = jax.nn.sigmoid(jnp.dot(xn, wgb_ref[...], preferred_element_type=F32))
    ub = jnp.dot(ob_ref[...], wun_ref[...], preferred_element_type=F32)
    o_ref[...] = (acc + gb * ub).astype(o_ref.dtype)


def _merge(xn, oa, ob, wga, wgb, wuf, wun, tm=512, tn=512):
    m, d = xn.shape
    n = wga.shape[1]
    ka = oa.shape[1]
    kb = ob.shape[1]
    row = lambda i, j: (i, 0)
    col = lambda i, j: (0, j)
    return pl.pallas_call(
        _merge_kernel,
        grid=(m // tm, n // tn),
        in_specs=[pl.BlockSpec((tm, d), row), pl.BlockSpec((tm, ka), row),
                  pl.BlockSpec((tm, kb), row),
                  pl.BlockSpec((d, tn), col), pl.BlockSpec((d, tn), col),
                  pl.BlockSpec((ka, tn), col), pl.BlockSpec((kb, tn), col)],
        out_specs=pl.BlockSpec((tm, tn), lambda i, j: (i, j)),
        out_shape=jax.ShapeDtypeStruct((m, n), BF16),
        compiler_params=_cparams(("arbitrary", "arbitrary")),
        name="gated_merge",
    )(xn, oa, ob, wga, wgb, wuf, wun)


def _swiglu_kernel(a_ref, wg_ref, wu_ref, o_ref):
    a = a_ref[...]
    gt = jnp.dot(a, wg_ref[...], preferred_element_type=F32)
    up = jnp.dot(a, wu_ref[...], preferred_element_type=F32)
    o_ref[...] = (gt * jax.nn.sigmoid(gt) * up).astype(o_ref.dtype)


def _swiglu(a, wg, wu, tm=1024, tn=512):
    m, k = a.shape
    n = wg.shape[1]
    return pl.pallas_call(
        _swiglu_kernel,
        grid=(m // tm, n // tn),
        in_specs=[pl.BlockSpec((tm, k), lambda i, j: (i, 0)),
                  pl.BlockSpec((k, tn), lambda i, j: (0, j)),
                  pl.BlockSpec((k, tn), lambda i, j: (0, j))],
        out_specs=pl.BlockSpec((tm, tn), lambda i, j: (i, j)),
        out_shape=jax.ShapeDtypeStruct((m, n), BF16),
        compiler_params=_cparams(("arbitrary", "arbitrary")),
        name="swiglu_up",
    )(a, wg, wu)


def _split3(x):
    hi = x.astype(BF16)
    r1 = x - hi.astype(F32)
    mid = r1.astype(BF16)
    lo = (r1 - mid.astype(F32)).astype(BF16)
    return hi, mid, lo


def _decay_kernel(z_ref, b_ref, ccol_ref, crow_ref, *, blk):
    t = z_ref.shape[0]
    r = lax.broadcasted_iota(jnp.int32, (blk, blk), 0)
    c = lax.broadcasted_iota(jnp.int32, (blk, blk), 1)
    tri = jnp.where(r >= c, 1.0, 0.0).astype(BF16)
    carry = jnp.zeros((1, LANES), F32)
    for s in range(t // blk):
        z = z_ref[s * blk:(s + 1) * blk, :] + b_ref[...]
        logf = jnp.minimum(z, 0.0) - jnp.log1p(jnp.exp(-jnp.abs(z)))
        hi, mid, lo = _split3(logf)
        cb = (jnp.dot(tri, hi, preferred_element_type=F32)
              + jnp.dot(tri, mid, preferred_element_type=F32)
              + jnp.dot(tri, lo, preferred_element_type=F32)) + carry
        carry = cb[blk - 1:blk, :]
        ccol_ref[s * blk:(s + 1) * blk, :] = cb
        crow_ref[0, :, s * blk:(s + 1) * blk] = cb.T[:FOX_HEADS, :]


def _decay(p3, bias_row, batch, seq, blk=256):
    return pl.pallas_call(
        functools.partial(_decay_kernel, blk=blk),
        grid=(batch,),
        in_specs=[pl.BlockSpec((seq, LANES), lambda b: (b, 6)),
                  pl.BlockSpec((1, LANES), lambda b: (0, 0))],
        out_specs=[pl.BlockSpec((seq, LANES), lambda b: (b, 0)),
                   pl.BlockSpec((1, FOX_HEADS, seq), lambda b: (b, 0, 0))],
        out_shape=[jax.ShapeDtypeStruct((batch * seq, LANES), F32),
                   jax.ShapeDtypeStruct((batch, FOX_HEADS, seq), F32)],
        compiler_params=_cparams(("arbitrary",)),
        name="fox_decay_cumsum",
    )(p3, bias_row)


def _fox_kernel(q_ref, k_ref, v_ref, ccol_ref, crow_ref, o_ref, *, tq, tk):
    h = pl.program_id(1)
    i = pl.program_id(2)
    q = q_ref[...]
    cc = ccol_ref[...]
    lane = lax.broadcasted_iota(jnp.int32, cc.shape, 1)
    ci = jnp.sum(jnp.where(lane == h, cc, 0.0), axis=-1, keepdims=True)
    rc = (lax.broadcasted_iota(jnp.int32, (tq, tk), 0)
          - lax.broadcasted_iota(jnp.int32, (tq, tk), 1))

    def body(kt, carry):
        m, l, acc = carry
        k0 = pl.multiple_of(kt * tk, tk)
        k = k_ref[pl.ds(k0, tk), :]
        v = v_ref[pl.ds(k0, tk), :]
        cj = crow_ref[0, 0, :, pl.ds(k0, tk)]
        s = lax.dot_general(q, k, _NT, preferred_element_type=F32)
        s = s + ci - cj
        mask = (rc + (i * tq - k0)) >= 0
        sm = jnp.where(mask, s, NEG)
        m_new = jnp.maximum(m, jnp.max(sm, axis=-1, keepdims=True))
        p = jnp.where(mask, jnp.exp(sm - m_new), 0.0)
        alpha = jnp.exp(m - m_new)
        l = alpha * l + jnp.sum(p, axis=-1, keepdims=True)
        acc = alpha * acc + jnp.dot(p.astype(BF16), v, preferred_element_type=F32)
        return m_new, l, acc

    nkt = (i * tq + tq + tk - 1) // tk
    init = (jnp.full((tq, 1), NEG, F32), jnp.zeros((tq, 1), F32),
            jnp.zeros((tq, FOX_HEAD_DIM), F32))
    m, l, acc = lax.fori_loop(0, nkt, body, init)
    o_ref[...] = (acc / jnp.maximum(l, 1e-30)).astype(o_ref.dtype)


def _fox_attention(p1, ccol, crow4, batch, seq, tq=256, tk=256):
    nq = seq // tq
    h = FOX_HEADS
    return pl.pallas_call(
        functools.partial(_fox_kernel, tq=tq, tk=tk),
        grid=(batch, h, nq),
        in_specs=[pl.BlockSpec((tq, 128), lambda b, hh, i: (b * nq + i, hh)),
                  pl.BlockSpec((seq, 128), lambda b, hh, i: (b, h + hh)),
                  pl.BlockSpec((seq, 128), lambda b, hh, i: (b, 2 * h + hh)),
                  pl.BlockSpec((tq, LANES), lambda b, hh, i: (b * nq + i, 0)),
                  pl.BlockSpec((1, 1, 1, seq), lambda b, hh, i: (b, hh, 0, 0))],
        out_specs=pl.BlockSpec((tq, 128), lambda b, hh, i: (b * nq + i, hh)),
        out_shape=jax.ShapeDtypeStruct((batch * seq, FOX_WIDTH), BF16),
        compiler_params=_cparams(("arbitrary", "arbitrary", "arbitrary")),
        name="fox_attention",
    )(p1, p1, p1, ccol, crow4)


def _compress_one(z_refs, pe_ref, w1_ref, w2_ref, nblk):
    half = CMP_BLOCK // 2
    first = jnp.zeros((nblk, CMP_HIDDEN), F32)
    second = jnp.zeros((nblk, CMP_HIDDEN), F32)
    for p in range(half):
        rows = pl.ds(p, nblk, stride=CMP_STRIDE)
        zp = [z_ref[rows, :] for z_ref in z_refs]
        zp = zp[0] if len(zp) == 1 else jnp.concatenate(zp, axis=1)
        first += jnp.dot((zp + pe_ref[p:p + 1, :]).astype(BF16), w1_ref[p],
                         preferred_element_type=F32)
        second += jnp.dot((zp + pe_ref[half + p:half + p + 1, :]).astype(BF16),
                          w1_ref[half + p], preferred_element_type=F32)
    hid = first + pltpu.roll(second, nblk - 1, 0)
    act = (hid * jax.nn.sigmoid(hid)).astype(BF16)
    return jnp.dot(act, w2_ref[...], preferred_element_type=F32)


def _compress_kernel(zk0_ref, zk1_ref, zv_ref, pek_ref, w1k_ref, w2k_ref, gk_ref,
                     pev_ref, w1v_ref, w2v_ref, kc_ref, vc_ref, *, nblk):
    kc = _compress_one((zk0_ref, zk1_ref), pek_ref, w1k_ref, w2k_ref, nblk)
    ms = jnp.sum(kc * kc, axis=-1, keepdims=True) * (1.0 / NSA_QK_DIM)
    kc_ref[...] = (kc * lax.rsqrt(ms + RMS_EPS) * gk_ref[...]).astype(kc_ref.dtype)
    vc = _compress_one((zv_ref,), pev_ref, w1v_ref, w2v_ref, nblk)
    vc_ref[...] = vc.astype(vc_ref.dtype)


def _compress(p3, pek, w1k, w2k, gk, pev, w1v, w2v, batch, seq):
    g = NSA_KV_GROUPS
    nblk = seq // CMP_STRIDE
    full2 = lambda b, gg: (0, 0)
    full3 = lambda b, gg: (0, 0, 0)
    return pl.pallas_call(
        functools.partial(_compress_kernel, nblk=nblk),
        grid=(batch, g),
        in_specs=[pl.BlockSpec((seq, LANES), lambda b, gg: (b, 2 * gg)),
                  pl.BlockSpec((seq, LANES), lambda b, gg: (b, 2 * gg + 1)),
                  pl.BlockSpec((seq, NSA_V_DIM), lambda b, gg: (b, 4 + gg)),
                  pl.BlockSpec(pek.shape, full2), pl.BlockSpec(w1k.shape, full3),
                  pl.BlockSpec(w2k.shape, full2), pl.BlockSpec(gk.shape, full2),
                  pl.BlockSpec(pev.shape, full2), pl.BlockSpec(w1v.shape, full3),
                  pl.BlockSpec(w2v.shape, full2)],
        out_specs=[pl.BlockSpec((nblk, NSA_QK_PAD), lambda b, gg: (b * g + gg, 0)),
                   pl.BlockSpec((nblk, NSA_V_DIM), lambda b, gg: (b * g + gg, 0))],
        out_shape=[jax.ShapeDtypeStruct((batch * g * nblk, NSA_QK_PAD), BF16),
                   jax.ShapeDtypeStruct((batch * g * nblk, NSA_V_DIM), BF16)],
        compiler_params=_cparams(("arbitrary", "arbitrary")),
        name="nsa_compress",
    )(p3, p3, p3, pek, w1k, w2k, gk, pev, w1v, w2v)


def _nsa_kernel(q_ref, kc_ref, vc_ref, ks_ref, vs_ref, kw_ref, vw_ref, gl_ref,
                slope_ref, ov_ref, ex_ref, o_ref, selx_ref, *, tq, n_cmp):
    g = pl.program_id(1)
    i = pl.program_id(2)
    t0 = i * tq
    hpg = NSA_HPG
    dv = NSA_V_DIM
    qb = q_ref[...]
    q4 = jnp.concatenate(
        [qb[:, hh * NSA_QK_PAD:(hh + 1) * NSA_QK_PAD] for hh in range(hpg)], axis=0)
    slopes = [slope_ref[0, hh:hh + 1, :] for hh in range(hpg)]

    r_i = lax.broadcasted_iota(jnp.int32, (tq, LANES), 0)
    c_i = lax.broadcasted_iota(jnp.int32, (tq, LANES), 1)
    rc = r_i - c_i

    s_c = lax.dot_general(q4, kc_ref[...], _NT, preferred_element_type=F32)
    dist_c = (t0 + r_i) - (CMP_STRIDE * c_i + (CMP_BLOCK - 1))
    mask_c = jnp.where(c_i < n_cmp, dist_c, -1) >= 0
    dist_cf = dist_c.astype(F32)
    vc = vc_ref[...]
    o_cmp = []
    p_sum = jnp.zeros((tq, LANES), F32)
    for hh in range(hpg):
        sh = s_c[hh * tq:(hh + 1) * tq, :] - slopes[hh] * dist_cf
        sm = jnp.where(mask_c, sh, NEG)
        m = jnp.max(sm, axis=-1, keepdims=True)
        e = jnp.where(mask_c, jnp.exp(sm - m), 0.0)
        p = e / jnp.maximum(jnp.sum(e, axis=-1, keepdims=True), 1e-30)
        o_cmp.append(jnp.dot(p.astype(BF16), vc, preferred_element_type=F32))
        p_sum = p_sum + p

    ph = p_sum.astype(BF16)
    plo = (p_sum - ph.astype(F32)).astype(BF16)
    ov = ov_ref[...]
    imp = (jnp.dot(ph, ov, preferred_element_type=F32)
           + jnp.dot(plo, ov, preferred_element_type=F32))

    back = ((t0 + r_i) >> (SEL_BLOCK.bit_length() - 1)) - c_i
    elig = back >= 0
    forced = jnp.where(c_i == 0, 0, jnp.where(elig, back, SEL_LOCAL)) < SEL_LOCAL
    score = jnp.where(elig, jnp.where(forced, FORCE_SCORE, imp), -1.0)
    rank = jnp.zeros((tq, LANES), F32)
    for jp in range(selx_ref.shape[1] // SEL_BLOCK):
        col = score[:, jp:jp + 1]
        later = jnp.where(c_i > jp, 1.0, 0.0)
        rank = rank + jnp.where(col > score, 1.0, jnp.where(col == score, later, 0.0))
    sel = jnp.where(elig, jnp.where(rank < SEL_TOPK, 1.0, 0.0), 0.0)
    selx_ref[...] = jnp.dot(sel.astype(BF16), ex_ref[...], preferred_element_type=F32)

    def attend(k_ref, v_ref, lo, hi, mask_fn):
        def body(kt, carry):
            m, l, acc = carry
            k0 = pl.multiple_of(kt * LANES, LANES)
            k = k_ref[pl.ds(k0, LANES), :]
            v = v_ref[pl.ds(k0, LANES), :]
            s = lax.dot_general(q4, k, _NT, preferred_element_type=F32)
            dist = rc + (t0 - k0)
            mask = mask_fn(dist, k0)
            distf = dist.astype(F32)
            bias = jnp.concatenate(
                [jnp.where(mask, -slopes[hh] * distf, NEG) for hh in range(hpg)], axis=0)
            sm = s + bias
            m_new = jnp.maximum(m, jnp.max(sm, axis=-1, keepdims=True))
            p = jnp.where(sm > 0.5 * NEG, jnp.exp(sm - m_new), 0.0)
            alpha = jnp.exp(m - m_new)
            l = alpha * l + jnp.sum(p, axis=-1, keepdims=True)
            acc = alpha * acc + jnp.dot(p.astype(BF16), v, preferred_element_type=F32)
            return m_new, l, acc

        init = (jnp.full((hpg * tq, 1), NEG, F32), jnp.zeros((hpg * tq, 1), F32),
                jnp.zeros((hpg * tq, dv), F32))
        m, l, acc = lax.fori_loop(lo, hi, body, init)
        return acc / jnp.maximum(l, 1e-30)

    def mask_slc(dist, k0):
        return jnp.where(dist >= 0, selx_ref[:, pl.ds(k0, LANES)], 0.0) > 0.5

    def mask_win(dist, k0):
        return jnp.where(dist >= 0, dist, WINDOW) < WINDOW

    n_kt = (t0 + tq + LANES - 1) // LANES
    o_slc = attend(ks_ref, vs_ref, 0, n_kt, mask_slc)
    lo_w = jnp.maximum(t0 - (WINDOW - 1), 0) // LANES
    o_win = attend(kw_ref, vw_ref, lo_w, n_kt, mask_win)

    sig = jax.nn.sigmoid(gl_ref[...])
    for hh in range(hpg):
        base = FOX_HEADS + (g * hpg + hh) * 3
        gates = [jnp.sum(jnp.where(c_i == base + br, sig, 0.0), axis=-1, keepdims=True)
                 for br in range(3)]
        rows = slice(hh * tq, (hh + 1) * tq)
        out = gates[0] * o_cmp[hh] + gates[1] * o_slc[rows, :] + gates[2] * o_win[rows, :]
        o_ref[:, hh * dv:(hh + 1) * dv] = out.astype(o_ref.dtype)


def _nsa_attention(p1, p2, p3, kc, vc, slopes, ov, ex, batch, seq, tq=128):
    g = NSA_KV_GROUPS
    nq = seq // tq
    nblk = seq // CMP_STRIDE
    n_cmp = nblk - CMP_BLOCK // CMP_STRIDE + 1
    qw = NSA_HPG * NSA_QK_PAD
    kblk0 = NSA_HEADS * NSA_QK_PAD // NSA_QK_PAD
    vblk0 = 3 * FOX_WIDTH // NSA_V_DIM
    return pl.pallas_call(
        functools.partial(_nsa_kernel, tq=tq, n_cmp=n_cmp),
        grid=(batch, g, nq),
        in_specs=[
            pl.BlockSpec((tq, qw), lambda b, gg, i: (b * nq + i, gg)),
            pl.BlockSpec((nblk, NSA_QK_PAD), lambda b, gg, i: (b * g + gg, 0)),
            pl.BlockSpec((nblk, NSA_V_DIM), lambda b, gg, i: (b * g + gg, 0)),
            pl.BlockSpec((seq, NSA_QK_PAD), lambda b, gg, i: (b, kblk0 + gg)),
            pl.BlockSpec((seq, NSA_V_DIM), lambda b, gg, i: (b, vblk0 + gg)),
            pl.BlockSpec((seq, NSA_QK_PAD), lambda b, gg, i: (b, kblk0 + g + gg)),
            pl.BlockSpec((seq, NSA_V_DIM), lambda b, gg, i: (b, vblk0 + g + gg)),
            pl.BlockSpec((tq, LANES), lambda b, gg, i: (b * nq + i, 6)),
            pl.BlockSpec((1, 8, LANES), lambda b, gg, i: (gg, 0, 0)),
            pl.BlockSpec(ov.shape, lambda b, gg, i: (0, 0)),
            pl.BlockSpec(ex.shape, lambda b, gg, i: (0, 0)),
        ],
        out_specs=pl.BlockSpec((tq, NSA_HPG * NSA_V_DIM), lambda b, gg, i: (b * nq + i, gg)),
        out_shape=jax.ShapeDtypeStruct((batch * seq, NSA_WIDTH), BF16),
        scratch_shapes=[pltpu.VMEM((tq, seq), F32)],
        compiler_params=_cparams(("arbitrary", "arbitrary", "arbitrary")),
        name="nsa_attention",
    )(p2, kc, vc, p2, p1, p2, p1, p3, slopes, ov, ex)


def _pad_heads(w, heads):
    k = w.shape[0]
    w = w.reshape(k, heads, NSA_QK_DIM)
    w = jnp.pad(w, ((0, 0), (0, 0), (0, NSA_QK_PAD - NSA_QK_DIM)))
    return w.reshape(k, heads * NSA_QK_PAD)


def _pad_gain(gain, scale=1.0):
    return jnp.pad(gain * scale, (0, NSA_QK_PAD - NSA_QK_DIM))


def _overlap_matrix(nc, ns):
    i = np.arange(nc)[:, None]
    j = np.arange(ns)[None, :]
    lo = np.maximum(i * CMP_STRIDE, j * SEL_BLOCK)
    hi = np.minimum(i * CMP_STRIDE + CMP_BLOCK, (j + 1) * SEL_BLOCK)
    return (np.maximum(hi - lo, 0) / CMP_STRIDE).astype(np.float32)


def kernel(x, norm_attn, w_in, fox_f_bias, fox_q_gain, fox_k_gain,
           nsa_q_gain, nsa_kc_gain, nsa_ks_gain, nsa_kw_gain,
           cmp_pe_k, cmp_w1_k, cmp_w2_k, cmp_pe_v, cmp_w1_v, cmp_w2_v,
           w_up_fox, w_up_nsa, w_out, norm_ffn, w_ffn_gate, w_ffn_up, w_ffn_down):
    batch, seq, d = x.shape
    m = batch * seq
    depth = w_in.shape[0]
    pts = [0] + [int(p) for p in np.cumsum(IN_SPLITS)]
    nblk = seq // CMP_STRIDE
    n_cmp = nblk - CMP_BLOCK // CMP_STRIDE + 1
    ns = seq // SEL_BLOCK

    slopes_h = jnp.exp2(-8.0 * jnp.arange(1, NSA_HEADS + 1, dtype=F32) / NSA_HEADS)
    slopes = jnp.broadcast_to(
        jnp.pad(slopes_h.reshape(NSA_KV_GROUPS, NSA_HPG), ((0, 0), (0, 8 - NSA_HPG)))[:, :, None],
        (NSA_KV_GROUPS, 8, LANES))
    ov_np = np.zeros((nblk, LANES), np.float32)
    ov_np[:n_cmp, :ns] = _overlap_matrix(n_cmp, ns)
    ov = jnp.asarray(ov_np, BF16)
    ex_np = np.zeros((LANES, seq), np.float32)
    ex_np[np.arange(seq) // SEL_BLOCK, np.arange(seq)] = 1.0
    ex = jnp.asarray(ex_np, BF16)

    xf = x.reshape(m, d)
    for l in range(depth):
        w = w_in[l]
        seg = [w[:, pts[s]:pts[s + 1]] for s in range(len(IN_SPLITS))]
        (wfq, wfk, wfv, wfl, wnq, wkc, wvc, wks, wvs, wkw, wvw, wng, wga, wgb) = seg

        w1 = jnp.concatenate([wfq, wfk, wfv, wvs, wvw], axis=1).astype(BF16)
        n_raw = FOX_WIDTH + 2 * KV_V
        gain1 = jnp.concatenate([jnp.tile(fox_q_gain[l] * FOX_HEAD_DIM ** -0.5, FOX_HEADS),
                                 jnp.tile(fox_k_gain[l], FOX_HEADS),
                                 jnp.ones((n_raw,), F32)])
        flag1 = jnp.concatenate([jnp.ones((2 * FOX_WIDTH,), F32), jnp.zeros((n_raw,), F32)])
        w2 = jnp.concatenate([_pad_heads(wnq, NSA_HEADS), _pad_heads(wks, NSA_KV_GROUPS),
                              _pad_heads(wkw, NSA_KV_GROUPS)], axis=1).astype(BF16)
        gain2 = jnp.concatenate([jnp.tile(_pad_gain(nsa_q_gain[l], NSA_QK_DIM ** -0.5), NSA_HEADS),
                                 jnp.tile(_pad_gain(nsa_ks_gain[l]), NSA_KV_GROUPS),
                                 jnp.tile(_pad_gain(nsa_kw_gain[l]), NSA_KV_GROUPS)])
        flag2 = jnp.ones_like(gain2)
        n_small = FOX_HEADS + 3 * NSA_HEADS
        w3 = jnp.concatenate([_pad_heads(wkc, NSA_KV_GROUPS), wvc, wfl, wng,
                              jnp.zeros((d, LANES - n_small), F32)], axis=1).astype(BF16)

        xn = _rmsnorm(xf, norm_attn[l])
        p1 = _matmul_norm(xn, w1, gain1, flag1, 128, 128, 1024, 512, "proj_g128")
        p2 = _matmul_norm(xn, w2, gain2, flag2, NSA_QK_PAD, NSA_QK_DIM, 1024, 512, "proj_g256")
        p3 = _matmul(xn, w3, F32, 1024, w3.shape[1], "proj_f32")

        bias_row = jnp.pad(fox_f_bias[l], (0, LANES - FOX_HEADS)).reshape(1, LANES)
        ccol, crow = _decay(p3, bias_row, batch, seq)
        o_a = _fox_attention(p1, ccol, crow.reshape(batch, FOX_HEADS, 1, seq), batch, seq)

        pad_d = NSA_QK_PAD - NSA_QK_DIM
        pek = jnp.pad(cmp_pe_k[l], ((0, 0), (0, pad_d)))
        w1k = jnp.pad(cmp_w1_k[l].reshape(CMP_BLOCK, NSA_QK_DIM, CMP_HIDDEN),
                      ((0, 0), (0, pad_d), (0, 0))).astype(BF16)
        w2k = jnp.pad(cmp_w2_k[l], ((0, 0), (0, pad_d))).astype(BF16)
        gk = _pad_gain(nsa_kc_gain[l]).reshape(1, NSA_QK_PAD)
        w1v = cmp_w1_v[l].reshape(CMP_BLOCK, NSA_V_DIM, CMP_HIDDEN).astype(BF16)
        w2v = cmp_w2_v[l].astype(BF16)
        kc, vc = _compress(p3, pek, w1k, w2k, gk, cmp_pe_v[l], w1v, w2v, batch, seq)
        o_b = _nsa_attention(p1, p2, p3, kc, vc, slopes, ov, ex, batch, seq)

        merged = _merge(xn, o_a, o_b, wga.astype(BF16), wgb.astype(BF16),
                        w_up_fox[l].astype(BF16), w_up_nsa[l].astype(BF16))
        hres = _matmul_residual(merged, w_out[l].astype(BF16), xf, 1024, 512, "out_proj")

        hn = _rmsnorm(hres, norm_ffn[l])
        act = _swiglu(hn, w_ffn_gate[l].astype(BF16), w_ffn_up[l].astype(BF16))
        xf = _matmul_acc_residual(act, w_ffn_down[l].astype(BF16), hres, 1024, 1024, 512,
                                  "ffn_down")
    return xf.reshape(batch, seq, d)
```

```python
import functools

import numpy as np
import jax
import jax.numpy as jnp
from jax import lax
from jax.experimental import pallas as pl
from jax.experimental.pallas import tpu as pltpu

F32 = jnp.float32
BF16 = jnp.bfloat16

D_MODEL = 2048
FOX_HEADS = 8
FOX_HEAD_DIM = 128
FOX_WIDTH = FOX_HEADS * FOX_HEAD_DIM
NSA_HEADS = 8
NSA_KV_GROUPS = 2
NSA_HPG = NSA_HEADS // NSA_KV_GROUPS
NSA_QK_DIM = 192
NSA_QK_PAD = 256
NSA_V_DIM = 128
NSA_WIDTH = NSA_HEADS * NSA_V_DIM
CMP_BLOCK = 32
CMP_STRIDE = 16
CMP_HIDDEN = 256
SEL_BLOCK = 64
SEL_TOPK = 16
SEL_LOCAL = 2
FORCE_SCORE = 1.0e4
WINDOW = 512
KV_K = NSA_KV_GROUPS * NSA_QK_DIM
KV_V = NSA_KV_GROUPS * NSA_V_DIM
D_FF = -(-(8 * D_MODEL) // (3 * 256)) * 256
RMS_EPS = 1e-6
IN_SPLITS = (FOX_WIDTH, FOX_WIDTH, FOX_WIDTH, FOX_HEADS,
             NSA_HEADS * NSA_QK_DIM, KV_K, KV_V, KV_K, KV_V, KV_K, KV_V,
             3 * NSA_HEADS, D_MODEL, D_MODEL)

LANES = 128
NEG = -1.0e30
SLC_TILE = 512
VMEM_LIMIT = 56 * 1024 * 1024

_NT = (((1,), (1,)), ((), ()))


def _cparams(sem):
    return pltpu.CompilerParams(dimension_semantics=sem, vmem_limit_bytes=VMEM_LIMIT)


def _rms_kernel(x_ref, g_ref, o_ref):
    x = x_ref[...]
    ms = jnp.mean(x * x, axis=-1, keepdims=True)
    o_ref[...] = (x * lax.rsqrt(ms + RMS_EPS) * g_ref[...]).astype(o_ref.dtype)


def _rmsnorm(x, gain, tm=512):
    m, d = x.shape
    return pl.pallas_call(
        _rms_kernel,
        grid=(m // tm,),
        in_specs=[pl.BlockSpec((tm, d), lambda i: (i, 0)),
                  pl.BlockSpec((1, d), lambda i: (0, 0))],
        out_specs=pl.BlockSpec((tm, d), lambda i: (i, 0)),
        out_shape=jax.ShapeDtypeStruct((m, d), BF16),
        compiler_params=_cparams(("arbitrary",)),
        name="rmsnorm",
    )(x, gain.reshape(1, d))


def _mm_kernel(a_ref, w_ref, o_ref):
    o_ref[...] = jnp.dot(a_ref[...], w_ref[...],
                         preferred_element_type=F32).astype(o_ref.dtype)


def _matmul(a, w, out_dtype, tm, tn, name):
    m, k = a.shape
    n = w.shape[1]
    return pl.pallas_call(
        _mm_kernel,
        grid=(m // tm, n // tn),
        in_specs=[pl.BlockSpec((tm, k), lambda i, j: (i, 0)),
                  pl.BlockSpec((k, tn), lambda i, j: (0, j))],
        out_specs=pl.BlockSpec((tm, tn), lambda i, j: (i, j)),
        out_shape=jax.ShapeDtypeStruct((m, n), out_dtype),
        compiler_params=_cparams(("arbitrary", "arbitrary")),
        name=name,
    )(a, w)


def _mm_norm_kernel(a_ref, w_ref, gain_ref, flag_ref, o_ref, *, group, count):
    y = jnp.dot(a_ref[...], w_ref[...], preferred_element_type=F32)
    for c in range(y.shape[1] // group):
        sl = slice(c * group, (c + 1) * group)
        yc = y[:, sl]
        ss = jnp.sum(yc * yc, axis=-1, keepdims=True)
        rs = lax.rsqrt(ss * (1.0 / count) + RMS_EPS)
        scale = jnp.where(flag_ref[:, sl] > 0.0, rs, 1.0)
        o_ref[:, sl] = (yc * scale * gain_ref[:, sl]).astype(o_ref.dtype)


def _matmul_norm(a, w, gain, flag, group, count, tm, tn, name):
    m, k = a.shape
    n = w.shape[1]
    return pl.pallas_call(
        functools.partial(_mm_norm_kernel, group=group, count=count),
        grid=(m // tm, n // tn),
        in_specs=[pl.BlockSpec((tm, k), lambda i, j: (i, 0)),
                  pl.BlockSpec((k, tn), lambda i, j: (0, j)),
                  pl.BlockSpec((1, tn), lambda i, j: (0, j)),
                  pl.BlockSpec((1, tn), lambda i, j: (0, j))],
        out_specs=pl.BlockSpec((tm, tn), lambda i, j: (i, j)),
        out_shape=jax.ShapeDtypeStruct((m, n), BF16),
        compiler_params=_cparams(("arbitrary", "arbitrary")),
        name=name,
    )(a, w, gain.reshape(1, n), flag.reshape(1, n))


def _mm_res_kernel(a_ref, w_ref, r_ref, o_ref):
    o_ref[...] = r_ref[...] + jnp.dot(a_ref[...], w_ref[...], preferred_element_type=F32)


def _matmul_residual(a, w, res, tm, tn, name):
    m, k = a.shape
    n = w.shape[1]
    return pl.pallas_call(
        _mm_res_kernel,
        grid=(m // tm, n // tn),
        in_specs=[pl.BlockSpec((tm, k), lambda i, j: (i, 0)),
                  pl.BlockSpec((k, tn), lambda i, j: (0, j)),
                  pl.BlockSpec((tm, tn), lambda i, j: (i, j))],
        out_specs=pl.BlockSpec((tm, tn), lambda i, j: (i, j)),
        out_shape=jax.ShapeDtypeStruct((m, n), F32),
        compiler_params=_cparams(("arbitrary", "arbitrary")),
        name=name,
    )(a, w, res)


def _mm_acc_res_kernel(a_ref, w_ref, r_ref, o_ref, acc_ref):
    kk = pl.program_id(2)

    @pl.when(kk == 0)
    def _():
        acc_ref[...] = jnp.zeros_like(acc_ref)

    acc_ref[...] += jnp.dot(a_ref[...], w_ref[...], preferred_element_type=F32)

    @pl.when(kk == pl.num_programs(2) - 1)
    def _():
        o_ref[...] = r_ref[...] + acc_ref[...]


def _matmul_acc_residual(a, w, res, tm, tn, tk, name):
    m, k = a.shape
    n = w.shape[1]
    return pl.pallas_call(
        _mm_acc_res_kernel,
        grid=(m // tm, n // tn, k // tk),
        in_specs=[pl.BlockSpec((tm, tk), lambda i, j, kk: (i, kk)),
                  pl.BlockSpec((tk, tn), lambda i, j, kk: (kk, j)),
                  pl.BlockSpec((tm, tn), lambda i, j, kk: (i, j))],
        out_specs=pl.BlockSpec((tm, tn), lambda i, j, kk: (i, j)),
        out_shape=jax.ShapeDtypeStruct((m, n), F32),
        scratch_shapes=[pltpu.VMEM((tm, tn), F32)],
        compiler_params=_cparams(("arbitrary", "arbitrary", "arbitrary")),
        name=name,
    )(a, w, res)


def _merge_kernel(xn_ref, oa_ref, ob_ref, wga_ref, wgb_ref, wuf_ref, wun_ref, o_ref):
    xn = xn_ref[...]
    ga = jax.nn.sigmoid(jnp.dot(xn, wga_ref[...], preferred_element_type=F32))
    ua = jnp.dot(oa_ref[...], wuf_ref[...], preferred_element_type=F32)
    acc = ga * ua
    gb = jax.nn.sigmoid(jnp.dot(xn, wgb_ref[...], preferred_element_type=F32))
    ub = jnp.dot(ob_ref[...], wun_ref[...], preferred_element_type=F32)
    o_ref[...] = (acc + gb * ub).astype(o_ref.dtype)


def _merge(xn, oa, ob, wga, wgb, wuf, wun, tm=512, tn=512):
    m, d = xn.shape
    n = wga.shape[1]
    ka = oa.shape[1]
    kb = ob.shape[1]
    row = lambda i, j: (i, 0)
    col = lambda i, j: (0, j)
    return pl.pallas_call(
        _merge_kernel,
        grid=(m // tm, n // tn),
        in_specs=[pl.BlockSpec((tm, d), row), pl.BlockSpec((tm, ka), row),
                  pl.BlockSpec((tm, kb), row),
                  pl.BlockSpec((d, tn), col), pl.BlockSpec((d, tn), col),
                  pl.BlockSpec((ka, tn), col), pl.BlockSpec((kb, tn), col)],
        out_specs=pl.BlockSpec((tm, tn), lambda i, j: (i, j)),
        out_shape=jax.ShapeDtypeStruct((m, n), BF16),
        compiler_params=_cparams(("arbitrary", "arbitrary")),
        name="gated_merge",
    )(xn, oa, ob, wga, wgb, wuf, wun)


def _swiglu_kernel(a_ref, wg_ref, wu_ref, o_ref):
    a = a_ref[...]
    gt = jnp.dot(a, wg_ref[...], preferred_element_type=F32)
    up = jnp.dot(a, wu_ref[...], preferred_element_type=F32)
    o_ref[...] = (gt * jax.nn.sigmoid(gt) * up).astype(o_ref.dtype)


def _swiglu(a, wg, wu, tm=1024, tn=512):
    m, k = a.shape
    n = wg.shape[1]
    return pl.pallas_call(
        _swiglu_kernel,
        grid=(m // tm, n // tn),
        in_specs=[pl.BlockSpec((tm, k), lambda i, j: (i, 0)),
                  pl.BlockSpec((k, tn), lambda i, j: (0, j)),
                  pl.BlockSpec((k, tn), lambda i, j: (0, j))],
        out_specs=pl.BlockSpec((tm, tn), lambda i, j: (i, j)),
        out_shape=jax.ShapeDtypeStruct((m, n), BF16),
        compiler_params=_cparams(("arbitrary", "arbitrary")),
        name="swiglu_up",
    )(a, wg, wu)


def _split3(x):
    hi = x.astype(BF16)
    r1 = x - hi.astype(F32)
    mid = r1.astype(BF16)
    lo = (r1 - mid.astype(F32)).astype(BF16)
    return hi, mid, lo


def _decay_kernel(z_ref, b_ref, crep_ref, crow_ref, *, blk):
    t = z_ref.shape[0]
    r = lax.broadcasted_iota(jnp.int32, (blk, blk), 0)
    c = lax.broadcasted_iota(jnp.int32, (blk, blk), 1)
    tri = jnp.where(r >= c, 1.0, 0.0).astype(BF16)
    lane = lax.broadcasted_iota(jnp.int32, (blk, LANES), 1)
    carry = jnp.zeros((1, LANES), F32)
    for s in range(t // blk):
        rows = slice(s * blk, (s + 1) * blk)
        z = z_ref[rows, :] + b_ref[...]
        logf = jnp.minimum(z, 0.0) - jnp.log1p(jnp.exp(-jnp.abs(z)))
        hi, mid, lo = _split3(logf)
        cb = (jnp.dot(tri, hi, preferred_element_type=F32)
              + jnp.dot(tri, mid, preferred_element_type=F32)
              + jnp.dot(tri, lo, preferred_element_type=F32)) + carry
        carry = cb[blk - 1:blk, :]
        crow_ref[0, :, rows] = cb.T[:FOX_HEADS, :]
        for h in range(FOX_HEADS):
            col = jnp.sum(jnp.where(lane == h, cb, 0.0), axis=-1, keepdims=True)
            crep_ref[0, h, rows, :] = jnp.broadcast_to(col, (blk, LANES))


def _decay(p3, bias_row, batch, seq, blk=256):
    return pl.pallas_call(
        functools.partial(_decay_kernel, blk=blk),
        grid=(batch,),
        in_specs=[pl.BlockSpec((seq, LANES), lambda b: (b, 6)),
                  pl.BlockSpec((1, LANES), lambda b: (0, 0))],
        out_specs=[pl.BlockSpec((1, FOX_HEADS, seq, LANES), lambda b: (b, 0, 0, 0)),
                   pl.BlockSpec((1, FOX_HEADS, seq), lambda b: (b, 0, 0))],
        out_shape=[jax.ShapeDtypeStruct((batch, FOX_HEADS, seq, LANES), F32),
                   jax.ShapeDtypeStruct((batch, FOX_HEADS, seq), F32)],
        compiler_params=_cparams(("arbitrary",)),
        name="fox_decay_cumsum",
    )(p3, bias_row)


def _transpose_bf16(x):
    return x.astype(F32).T.astype(BF16)


def _fox_kernel(q_ref, k_ref, v_ref, crep_ref, crow_ref, o_ref, vt_ref, *, tq):
    i = pl.program_id(2)

    @pl.when(i == 0)
    def _():
        vt_ref[...] = _transpose_bf16(v_ref[...])

    qt = _transpose_bf16(q_ref[...])
    ci = crow_ref[0, 0]
    reps = tq // LANES

    def scores(k0):
        k = k_ref[pl.ds(k0, tq), :]
        cj = crep_ref[0, 0, pl.ds(k0, tq), :]
        st = jnp.dot(k, qt, preferred_element_type=F32)
        return st + ci - jnp.concatenate([cj] * reps, axis=1)

    k0 = pl.multiple_of(i * tq, tq)
    rk = lax.broadcasted_iota(jnp.int32, (tq, tq), 0)
    cq = lax.broadcasted_iota(jnp.int32, (tq, tq), 1)
    sm = jnp.where(rk <= cq, scores(k0), NEG)
    m = jnp.max(sm, axis=0, keepdims=True)
    p = jnp.exp(sm - m)
    l = jnp.sum(p, axis=0, keepdims=True)
    acc = jnp.dot(vt_ref[:, pl.ds(k0, tq)], p.astype(BF16), preferred_element_type=F32)

    def body(it, carry):
        m, l, acc = carry
        k0 = pl.multiple_of((i - 1 - it) * tq, tq)
        st = scores(k0)
        m_new = jnp.maximum(m, jnp.max(st, axis=0, keepdims=True))
        p = jnp.exp(st - m_new)
        alpha = jnp.exp(m - m_new)
        l = alpha * l + jnp.sum(p, axis=0, keepdims=True)
        acc = alpha * acc + jnp.dot(vt_ref[:, pl.ds(k0, tq)], p.astype(BF16),
                                    preferred_element_type=F32)
        return m_new, l, acc

    m, l, acc = lax.fori_loop(0, i, body, (m, l, acc))
    out = acc * (1.0 / jnp.maximum(l, 1e-30))
    o_ref[...] = out.T.astype(o_ref.dtype)


def _fox_attention(p1, crep, crow4, batch, seq, tq=512):
    nq = seq // tq
    h = FOX_HEADS
    return pl.pallas_call(
        functools.partial(_fox_kernel, tq=tq),
        grid=(batch, h, nq),
        in_specs=[pl.BlockSpec((tq, 128), lambda b, hh, i: (b * nq + i, hh)),
                  pl.BlockSpec((seq, 128), lambda b, hh, i: (b, h + hh)),
                  pl.BlockSpec((seq, 128), lambda b, hh, i: (b, 2 * h + hh)),
                  pl.BlockSpec((1, 1, seq, LANES), lambda b, hh, i: (b, hh, 0, 0)),
                  pl.BlockSpec((1, 1, 1, tq), lambda b, hh, i: (b, hh, 0, i))],
        out_specs=pl.BlockSpec((tq, 128), lambda b, hh, i: (b * nq + i, hh)),
        out_shape=jax.ShapeDtypeStruct((batch * seq, FOX_WIDTH), BF16),
        scratch_shapes=[pltpu.VMEM((FOX_HEAD_DIM, seq), BF16)],
        compiler_params=_cparams(("arbitrary", "arbitrary", "arbitrary")),
        name="fox_attention",
    )(p1, p1, p1, crep, crow4)


def _compress_one(z_refs, pe_ref, w1_ref, w2_ref, nblk):
    half = CMP_BLOCK // 2
    first = jnp.zeros((nblk, CMP_HIDDEN), F32)
    second = jnp.zeros((nblk, CMP_HIDDEN), F32)
    for p in range(half):
        rows = pl.ds(p, nblk, stride=CMP_STRIDE)
        zp = [z_ref[rows, :] for z_ref in z_refs]
        zp = zp[0] if len(zp) == 1 else jnp.concatenate(zp, axis=1)
        first += jnp.dot((zp + pe_ref[p:p + 1, :]).astype(BF16), w1_ref[p],
                         preferred_element_type=F32)
        second += jnp.dot((zp + pe_ref[half + p:half + p + 1, :]).astype(BF16),
                          w1_ref[half + p], preferred_element_type=F32)
    hid = first + pltpu.roll(second, nblk - 1, 0)
    act = (hid * jax.nn.sigmoid(hid)).astype(BF16)
    return jnp.dot(act, w2_ref[...], preferred_element_type=F32)


def _compress_kernel(zk0_ref, zk1_ref, zv_ref, pek_ref, w1k_ref, w2k_ref, gk_ref,
                     pev_ref, w1v_ref, w2v_ref, kc_ref, vc_ref, *, nblk):
    kc = _compress_one((zk0_ref, zk1_ref), pek_ref, w1k_ref, w2k_ref, nblk)
    ms = jnp.sum(kc * kc, axis=-1, keepdims=True) * (1.0 / NSA_QK_DIM)
    kc_ref[...] = (kc * lax.rsqrt(ms + RMS_EPS) * gk_ref[...]).astype(kc_ref.dtype)
    vc = _compress_one((zv_ref,), pev_ref, w1v_ref, w2v_ref, nblk)
    vc_ref[...] = vc.T.astype(vc_ref.dtype)


def _compress(p3, pek, w1k, w2k, gk, pev, w1v, w2v, batch, seq):
    g = NSA_KV_GROUPS
    nblk = seq // CMP_STRIDE
    full2 = lambda b, gg: (0, 0)
    full3 = lambda b, gg: (0, 0, 0)
    return pl.pallas_call(
        functools.partial(_compress_kernel, nblk=nblk),
        grid=(batch, g),
        in_specs=[pl.BlockSpec((seq, LANES), lambda b, gg: (b, 2 * gg)),
                  pl.BlockSpec((seq, LANES), lambda b, gg: (b, 2 * gg + 1)),
                  pl.BlockSpec((seq, NSA_V_DIM), lambda b, gg: (b, 4 + gg)),
                  pl.BlockSpec(pek.shape, full2), pl.BlockSpec(w1k.shape, full3),
                  pl.BlockSpec(w2k.shape, full2), pl.BlockSpec(gk.shape, full2),
                  pl.BlockSpec(pev.shape, full2), pl.BlockSpec(w1v.shape, full3),
                  pl.BlockSpec(w2v.shape, full2)],
        out_specs=[pl.BlockSpec((nblk, NSA_QK_PAD), lambda b, gg: (b * g + gg, 0)),
                   pl.BlockSpec((NSA_V_DIM, nblk), lambda b, gg: (b * g + gg, 0))],
        out_shape=[jax.ShapeDtypeStruct((batch * g * nblk, NSA_QK_PAD), BF16),
                   jax.ShapeDtypeStruct((batch * g * NSA_V_DIM, nblk), BF16)],
        compiler_params=_cparams(("arbitrary", "arbitrary")),
        name="nsa_compress",
    )(p3, p3, p3, pek, w1k, w2k, gk, pev, w1v, w2v)


def _nsa_kernel(q_ref, kc_ref, vct_ref, ks_ref, vs_ref, kw_ref, vw_ref, gl_ref,
                slope_ref, ovt_ref, ext_ref, o_ref,
                vst_ref, vwt_ref, selx_ref, glt_ref, *, tq, n_cmp, n_sel):
    g = pl.program_id(1)
    i = pl.program_id(2)
    t0 = i * tq
    hpg = NSA_HPG
    dv = NSA_V_DIM

    @pl.when(i == 0)
    def _():
        vst_ref[...] = _transpose_bf16(vs_ref[...])
        vwt_ref[...] = _transpose_bf16(vw_ref[...])

    qb = q_ref[...]
    q4t = jnp.concatenate(
        [_transpose_bf16(qb[:, hh * NSA_QK_PAD:(hh + 1) * NSA_QK_PAD]) for hh in range(hpg)],
        axis=1)
    slopes = [slope_ref[0, hh:hh + 1, :] for hh in range(hpg)]
    rk = lax.broadcasted_iota(jnp.int32, (LANES, tq), 0)
    cq = lax.broadcasted_iota(jnp.int32, (LANES, tq), 1)

    s_c = jnp.dot(kc_ref[...], q4t, preferred_element_type=F32)
    dist_c = (t0 + cq) - (CMP_STRIDE * rk + (CMP_BLOCK - 1))
    mask_c = jnp.where(rk < n_cmp, dist_c, -1) >= 0
    dist_cf = dist_c.astype(F32)
    probs = []
    p_sum = jnp.zeros((LANES, tq), F32)
    for hh in range(hpg):
        sh = s_c[:, hh * tq:(hh + 1) * tq] - slopes[hh] * dist_cf
        sm = jnp.where(mask_c, sh, NEG)
        m = jnp.max(sm, axis=0, keepdims=True)
        e = jnp.where(mask_c, jnp.exp(sm - m), 0.0)
        p = e * (1.0 / jnp.maximum(jnp.sum(e, axis=0, keepdims=True), 1e-30))
        probs.append(p)
        p_sum = p_sum + p
    o_cmp = jnp.dot(vct_ref[...], jnp.concatenate(probs, axis=1).astype(BF16),
                    preferred_element_type=F32)

    ph = p_sum.astype(BF16)
    plo = (p_sum - ph.astype(F32)).astype(BF16)
    ovt = ovt_ref[...]
    imp = (jnp.dot(ovt, ph, preferred_element_type=F32)
           + jnp.dot(ovt, plo, preferred_element_type=F32))[:n_sel, :]

    rj = lax.broadcasted_iota(jnp.int32, (n_sel, tq), 0)
    tcol = t0 + lax.broadcasted_iota(jnp.int32, (n_sel, tq), 1)
    back = (tcol >> (SEL_BLOCK.bit_length() - 1)) - rj
    elig = back >= 0
    forced = jnp.where(rj == 0, 0, jnp.where(elig, back, SEL_LOCAL)) < SEL_LOCAL
    score = jnp.where(elig, jnp.where(forced, FORCE_SCORE, imp), -1.0)
    rank = jnp.zeros((n_sel, tq), F32)
    for jp in range(n_sel):
        row = score[jp:jp + 1, :]
        later = jnp.where(rj > jp, 1.0, 0.0)
        rank = rank + jnp.where(row > score, 1.0, jnp.where(row == score, later, 0.0))
    sel = jnp.where(elig, jnp.where(rank < SEL_TOPK, 1.0, 0.0), 0.0)
    sel = jnp.concatenate([sel, jnp.zeros((LANES - n_sel, tq), F32)], axis=0).astype(BF16)
    selx_ref[...] = jnp.dot(ext_ref[...], sel, preferred_element_type=F32)

    def tile(k_ref, vt_ref, k0, rows, mask_fn):
        k = k_ref[pl.ds(k0, rows), :]
        st = jnp.dot(k, q4t, preferred_element_type=F32)
        dist = (lax.broadcasted_iota(jnp.int32, (rows, tq), 1)
                - lax.broadcasted_iota(jnp.int32, (rows, tq), 0)) + (t0 - k0)
        mask = mask_fn(dist, k0, rows)
        distf = dist.astype(F32)
        bias = jnp.concatenate(
            [jnp.where(mask, -slopes[hh] * distf, NEG) for hh in range(hpg)], axis=1)
        return st + bias, vt_ref[:, pl.ds(k0, rows)]

    def mask_slc(dist, k0, rows):
        return jnp.where(dist >= 0, selx_ref[pl.ds(k0, rows), :], 0.0) > 0.5

    def mask_win(dist, k0, rows):
        return jnp.where(dist >= 0, dist, WINDOW) < WINDOW

    n_full = t0 // SLC_TILE
    sm, vt = tile(ks_ref, vst_ref, pl.multiple_of(n_full * SLC_TILE, SLC_TILE), SLC_TILE,
                  mask_slc)
    m = jnp.max(sm, axis=0, keepdims=True)
    p = jnp.exp(sm - m)
    l = jnp.sum(p, axis=0, keepdims=True)
    acc = jnp.dot(vt, p.astype(BF16), preferred_element_type=F32)

    def body(it, carry):
        m, l, acc = carry
        k0 = pl.multiple_of((n_full - 1 - it) * SLC_TILE, SLC_TILE)
        sm, vt = tile(ks_ref, vst_ref, k0, SLC_TILE, mask_slc)
        m_new = jnp.maximum(m, jnp.max(sm, axis=0, keepdims=True))
        p = jnp.exp(sm - m_new)
        alpha = jnp.exp(m - m_new)
        l = alpha * l + jnp.sum(p, axis=0, keepdims=True)
        acc = alpha * acc + jnp.dot(vt, p.astype(BF16), preferred_element_type=F32)
        return m_new, l, acc

    m, l, acc = lax.fori_loop(0, n_full, body, (m, l, acc))
    o_slc = acc * (1.0 / jnp.maximum(l, 1e-30))

    k0 = pl.multiple_of(jnp.maximum(t0 - WINDOW, 0), LANES)
    sm, vt = tile(kw_ref, vwt_ref, k0, WINDOW + tq, mask_win)
    p = jnp.exp(sm - jnp.max(sm, axis=0, keepdims=True))
    o_win = (jnp.dot(vt, p.astype(BF16), preferred_element_type=F32)
             * (1.0 / jnp.maximum(jnp.sum(p, axis=0, keepdims=True), 1e-30)))

    glt_ref[...] = jax.nn.sigmoid(gl_ref[...]).T
    for hh in range(hpg):
        base = FOX_HEADS + (g * hpg + hh) * 3
        lanes = slice(hh * tq, (hh + 1) * tq)
        out = (glt_ref[pl.ds(base, 1), :] * o_cmp[:, lanes]
               + glt_ref[pl.ds(base + 1, 1), :] * o_slc[:, lanes]
               + glt_ref[pl.ds(base + 2, 1), :] * o_win[:, lanes])
        o_ref[:, hh * dv:(hh + 1) * dv] = out.T.astype(o_ref.dtype)


def _nsa_attention(p1, p2, p3, kc, vct, slopes, ovt, ext, batch, seq):
    tq = LANES
    g = NSA_KV_GROUPS
    nq = seq // tq
    nblk = seq // CMP_STRIDE
    n_cmp = nblk - CMP_BLOCK // CMP_STRIDE + 1
    qw = NSA_HPG * NSA_QK_PAD
    kblk0 = NSA_HEADS * NSA_QK_PAD // NSA_QK_PAD
    vblk0 = 3 * FOX_WIDTH // NSA_V_DIM
    return pl.pallas_call(
        functools.partial(_nsa_kernel, tq=tq, n_cmp=n_cmp, n_sel=seq // SEL_BLOCK),
        grid=(batch, g, nq),
        in_specs=[
            pl.BlockSpec((tq, qw), lambda b, gg, i: (b * nq + i, gg)),
            pl.BlockSpec((nblk, NSA_QK_PAD), lambda b, gg, i: (b * g + gg, 0)),
            pl.BlockSpec((NSA_V_DIM, nblk), lambda b, gg, i: (b * g + gg, 0)),
            pl.BlockSpec((seq, NSA_QK_PAD), lambda b, gg, i: (b, kblk0 + gg)),
            pl.BlockSpec((seq, NSA_V_DIM), lambda b, gg, i: (b, vblk0 + gg)),
            pl.BlockSpec((seq, NSA_QK_PAD), lambda b, gg, i: (b, kblk0 + g + gg)),
            pl.BlockSpec((seq, NSA_V_DIM), lambda b, gg, i: (b, vblk0 + g + gg)),
            pl.BlockSpec((tq, LANES), lambda b, gg, i: (b * nq + i, 6)),
            pl.BlockSpec((1, 8, LANES), lambda b, gg, i: (gg, 0, 0)),
            pl.BlockSpec(ovt.shape, lambda b, gg, i: (0, 0)),
            pl.BlockSpec(ext.shape, lambda b, gg, i: (0, 0)),
        ],
        out_specs=pl.BlockSpec((tq, NSA_HPG * NSA_V_DIM), lambda b, gg, i: (b * nq + i, gg)),
        out_shape=jax.ShapeDtypeStruct((batch * seq, NSA_WIDTH), BF16),
        scratch_shapes=[pltpu.VMEM((NSA_V_DIM, seq), BF16), pltpu.VMEM((NSA_V_DIM, seq), BF16),
                        pltpu.VMEM((seq, tq), F32), pltpu.VMEM((LANES, tq), F32)],
        compiler_params=_cparams(("arbitrary", "arbitrary", "arbitrary")),
        name="nsa_attention",
    )(p2, kc, vct, p2, p1, p2, p1, p3, slopes, ovt, ext)


def _pad_heads(w, heads):
    k = w.shape[0]
    w = w.reshape(k, heads, NSA_QK_DIM)
    w = jnp.pad(w, ((0, 0), (0, 0), (0, NSA_QK_PAD - NSA_QK_DIM)))
    return w.reshape(k, heads * NSA_QK_PAD)


def _pad_gain(gain, scale=1.0):
    return jnp.pad(gain * scale, (0, NSA_QK_PAD - NSA_QK_DIM))


def _overlap_matrix(nc, ns):
    i = np.arange(nc)[:, None]
    j = np.arange(ns)[None, :]
    lo = np.maximum(i * CMP_STRIDE, j * SEL_BLOCK)
    hi = np.minimum(i * CMP_STRIDE + CMP_BLOCK, (j + 1) * SEL_BLOCK)
    return (np.maximum(hi - lo, 0) / CMP_STRIDE).astype(np.float32)


def kernel(x, norm_attn, w_in, fox_f_bias, fox_q_gain, fox_k_gain,
           nsa_q_gain, nsa_kc_gain, nsa_ks_gain, nsa_kw_gain,
           cmp_pe_k, cmp_w1_k, cmp_w2_k, cmp_pe_v, cmp_w1_v, cmp_w2_v,
           w_up_fox, w_up_nsa, w_out, norm_ffn, w_ffn_gate, w_ffn_up, w_ffn_down):
    batch, seq, d = x.shape
    m = batch * seq
    depth = w_in.shape[0]
    pts = [0] + [int(p) for p in np.cumsum(IN_SPLITS)]
    nblk = seq // CMP_STRIDE
    n_cmp = nblk - CMP_BLOCK // CMP_STRIDE + 1
    ns = seq // SEL_BLOCK

    slopes_h = jnp.exp2(-8.0 * jnp.arange(1, NSA_HEADS + 1, dtype=F32) / NSA_HEADS)
    slopes = jnp.broadcast_to(
        jnp.pad(slopes_h.reshape(NSA_KV_GROUPS, NSA_HPG), ((0, 0), (0, 8 - NSA_HPG)))[:, :, None],
        (NSA_KV_GROUPS, 8, LANES))
    ovt_np = np.zeros((LANES, nblk), np.float32)
    ovt_np[:ns, :n_cmp] = _overlap_matrix(n_cmp, ns).T
    ovt = jnp.asarray(ovt_np, BF16)
    ext_np = np.zeros((seq, LANES), np.float32)
    ext_np[np.arange(seq), np.arange(seq) // SEL_BLOCK] = 1.0
    ext = jnp.asarray(ext_np, BF16)

    xf = x.reshape(m, d)
    for l in range(depth):
        w = w_in[l]
        seg = [w[:, pts[s]:pts[s + 1]] for s in range(len(IN_SPLITS))]
        (wfq, wfk, wfv, wfl, wnq, wkc, wvc, wks, wvs, wkw, wvw, wng, wga, wgb) = seg

        w1 = jnp.concatenate([wfq, wfk, wfv, wvs, wvw], axis=1).astype(BF16)
        n_raw = FOX_WIDTH + 2 * KV_V
        gain1 = jnp.concatenate([jnp.tile(fox_q_gain[l] * FOX_HEAD_DIM ** -0.5, FOX_HEADS),
                                 jnp.tile(fox_k_gain[l], FOX_HEADS),
                                 jnp.ones((n_raw,), F32)])
        flag1 = jnp.concatenate([jnp.ones((2 * FOX_WIDTH,), F32), jnp.zeros((n_raw,), F32)])
        w2 = jnp.concatenate([_pad_heads(wnq, NSA_HEADS), _pad_heads(wks, NSA_KV_GROUPS),
                              _pad_heads(wkw, NSA_KV_GROUPS)], axis=1).astype(BF16)
        gain2 = jnp.concatenate([jnp.tile(_pad_gain(nsa_q_gain[l], NSA_QK_DIM ** -0.5), NSA_HEADS),
                                 jnp.tile(_pad_gain(nsa_ks_gain[l]), NSA_KV_GROUPS),
                                 jnp.tile(_pad_gain(nsa_kw_gain[l]), NSA_KV_GROUPS)])
        flag2 = jnp.ones_like(gain2)
        n_small = FOX_HEADS + 3 * NSA_HEADS
        w3 = jnp.concatenate([_pad_heads(wkc, NSA_KV_GROUPS), wvc, wfl, wng,
                              jnp.zeros((d, LANES - n_small), F32)], axis=1).astype(BF16)

        xn = _rmsnorm(xf, norm_attn[l])
        p1 = _matmul_norm(xn, w1, gain1, flag1, 128, 128, 1024, 512, "proj_g128")
        p2 = _matmul_norm(xn, w2, gain2, flag2, NSA_QK_PAD, NSA_QK_DIM, 1024, 512, "proj_g256")
        p3 = _matmul(xn, w3, F32, 1024, w3.shape[1], "proj_f32")

        bias_row = jnp.pad(fox_f_bias[l], (0, LANES - FOX_HEADS)).reshape(1, LANES)
        crep, crow = _decay(p3, bias_row, batch, seq)
        o_a = _fox_attention(p1, crep, crow.reshape(batch, FOX_HEADS, 1, seq), batch, seq)

        pad_d = NSA_QK_PAD - NSA_QK_DIM
        pek = jnp.pad(cmp_pe_k[l], ((0, 0), (0, pad_d)))
        w1k = jnp.pad(cmp_w1_k[l].reshape(CMP_BLOCK, NSA_QK_DIM, CMP_HIDDEN),
                      ((0, 0), (0, pad_d), (0, 0))).astype(BF16)
        w2k = jnp.pad(cmp_w2_k[l], ((0, 0), (0, pad_d))).astype(BF16)
        gk = _pad_gain(nsa_kc_gain[l]).reshape(1, NSA_QK_PAD)
        w1v = cmp_w1_v[l].reshape(CMP_BLOCK, NSA_V_DIM, CMP_HIDDEN).astype(BF16)
        w2v = cmp_w2_v[l].astype(BF16)
        kc, vct = _compress(p3, pek, w1k, w2k, gk, cmp_pe_v[l], w1v, w2v, batch, seq)
        o_b = _nsa_attention(p1, p2, p3, kc, vct, slopes, ovt, ext, batch, seq)

        merged = _merge(xn, o_a, o_b, wga.astype(BF16), wgb.astype(BF16),
                        w_up_fox[l].astype(BF16), w_up_nsa[l].astype(BF16))
        hres = _matmul_residual(merged, w_out[l].astype(BF16), xf, 1024, 512, "out_proj")

        hn = _rmsnorm(hres, norm_ffn[l])
        act = _swiglu(hn, w_ffn_gate[l].astype(BF16), w_ffn_up[l].astype(BF16))
        xf = _matmul_acc_residual(act, w_ffn_down[l].astype(BF16), hres, 1024, 1024, 512,
                                  "ffn_down")
    return xf.reshape(batch, seq, d)
```

```python
import functools

import numpy as np
import jax
import jax.numpy as jnp
from jax import lax
from jax.experimental import pallas as pl
from jax.experimental.pallas import tpu as pltpu

F32 = jnp.float32
BF16 = jnp.bfloat16

D_MODEL = 2048
FOX_HEADS = 8
FOX_HEAD_DIM = 128
FOX_WIDTH = FOX_HEADS * FOX_HEAD_DIM
NSA_HEADS = 8
NSA_KV_GROUPS = 2
NSA_HPG = NSA_HEADS // NSA_KV_GROUPS
NSA_QK_DIM = 192
NSA_QK_PAD = 256
NSA_V_DIM = 128
NSA_WIDTH = NSA_HEADS * NSA_V_DIM
CMP_BLOCK = 32
CMP_STRIDE = 16
CMP_HIDDEN = 256
SEL_BLOCK = 64
SEL_TOPK = 16
SEL_LOCAL = 2
FORCE_SCORE = 1.0e4
WINDOW = 512
KV_K = NSA_KV_GROUPS * NSA_QK_DIM
KV_V = NSA_KV_GROUPS * NSA_V_DIM
D_FF = -(-(8 * D_MODEL) // (3 * 256)) * 256
RMS_EPS = 1e-6
IN_SPLITS = (FOX_WIDTH, FOX_WIDTH, FOX_WIDTH, FOX_HEADS,
             NSA_HEADS * NSA_QK_DIM, KV_K, KV_V, KV_K, KV_V, KV_K, KV_V,
             3 * NSA_HEADS, D_MODEL, D_MODEL)

LANES = 128
NEG = -1.0e30
SLC_TILE = 512
VMEM_LIMIT = 56 * 1024 * 1024

_NT = (((1,), (1,)), ((), ()))


def _cparams(sem):
    return pltpu.CompilerParams(dimension_semantics=sem, vmem_limit_bytes=VMEM_LIMIT)


def _rms_kernel(x_ref, g_ref, o_ref):
    x = x_ref[...]
    ms = jnp.mean(x * x, axis=-1, keepdims=True)
    o_ref[...] = (x * lax.rsqrt(ms + RMS_EPS) * g_ref[...]).astype(o_ref.dtype)


def _rmsnorm(x, gain, tm=512):
    m, d = x.shape
    return pl.pallas_call(
        _rms_kernel,
        grid=(m // tm,),
        in_specs=[pl.BlockSpec((tm, d), lambda i: (i, 0)),
                  pl.BlockSpec((1, d), lambda i: (0, 0))],
        out_specs=pl.BlockSpec((tm, d), lambda i: (i, 0)),
        out_shape=jax.ShapeDtypeStruct((m, d), BF16),
        compiler_params=_cparams(("arbitrary",)),
        name="rmsnorm",
    )(x, gain.reshape(1, d))


def _stage_weight(w_ref, wb_ref, pad_from=0, pad_to=0):
    if pad_from == pad_to:
        wb_ref[...] = w_ref[...].astype(BF16)
        return
    k = w_ref.shape[0]
    for h in range(w_ref.shape[1] // pad_from):
        wb_ref[:, h * pad_to:h * pad_to + pad_from] = (
            w_ref[:, h * pad_from:(h + 1) * pad_from].astype(BF16))
        wb_ref[:, h * pad_to + pad_from:(h + 1) * pad_to] = jnp.zeros((k, pad_to - pad_from), BF16)


def _first_m_step():
    return pl.program_id(1) == 0


def _proj_kernel(a_ref, w_ref, gain_ref, flag_ref, o_ref, wb_ref, *, pad_from, pad_to,
                 group, count):
    @pl.when(_first_m_step())
    def _():
        _stage_weight(w_ref, wb_ref, pad_from, pad_to)

    y = jnp.dot(a_ref[...], wb_ref[...], preferred_element_type=F32)
    if group == 0:
        o_ref[...] = y.astype(o_ref.dtype)
        return
    for c in range(y.shape[1] // group):
        sl = slice(c * group, (c + 1) * group)
        yc = y[:, sl]
        ss = jnp.sum(yc * yc, axis=-1, keepdims=True)
        rs = lax.rsqrt(ss * (1.0 / count) + RMS_EPS)
        scale = jnp.where(flag_ref[:, sl] > 0.0, rs, 1.0)
        o_ref[:, sl] = (yc * scale * gain_ref[:, sl]).astype(o_ref.dtype)


def _project(a, w, n_tiles, tn_in, col_block0, gain, flag, out_dtype, tm, name,
             pad_from=0, pad_to=0, group=0, count=1):
    m, k = a.shape
    tn_out = tn_in if pad_from == pad_to else tn_in // pad_from * pad_to
    n_out = n_tiles * tn_out
    if gain is None:
        gain = jnp.ones((n_out,), F32)
        flag = jnp.zeros((n_out,), F32)
    return pl.pallas_call(
        functools.partial(_proj_kernel, pad_from=pad_from, pad_to=pad_to, group=group,
                          count=count),
        grid=(n_tiles, m // tm),
        in_specs=[pl.BlockSpec((tm, k), lambda j, i: (i, 0)),
                  pl.BlockSpec((k, tn_in), lambda j, i: (0, col_block0 + j)),
                  pl.BlockSpec((1, tn_out), lambda j, i: (0, j)),
                  pl.BlockSpec((1, tn_out), lambda j, i: (0, j))],
        out_specs=pl.BlockSpec((tm, tn_out), lambda j, i: (i, j)),
        out_shape=jax.ShapeDtypeStruct((m, n_out), out_dtype),
        scratch_shapes=[pltpu.VMEM((k, tn_out), BF16)],
        compiler_params=_cparams(("arbitrary", "arbitrary")),
        name=name,
    )(a, w, gain.reshape(1, n_out), flag.reshape(1, n_out))


def _out_norm_kernel(a_ref, w_ref, r_ref, g_ref, h_ref, hn_ref, wb_ref):
    @pl.when(pl.program_id(0) == 0)
    def _():
        _stage_weight(w_ref, wb_ref)

    h = r_ref[...] + jnp.dot(a_ref[...], wb_ref[...], preferred_element_type=F32)
    h_ref[...] = h
    ms = jnp.mean(h * h, axis=-1, keepdims=True)
    hn_ref[...] = (h * lax.rsqrt(ms + RMS_EPS) * g_ref[...]).astype(hn_ref.dtype)


def _out_proj_norm(a, w, res, gain, tm=512):
    m, k = a.shape
    n = w.shape[1]
    full = lambda i: (0, 0)
    row = lambda i: (i, 0)
    return pl.pallas_call(
        _out_norm_kernel,
        grid=(m // tm,),
        in_specs=[pl.BlockSpec((tm, k), row),
                  pl.BlockSpec((k, n), full, pipeline_mode=pl.Buffered(1)),
                  pl.BlockSpec((tm, n), row), pl.BlockSpec((1, n), full)],
        out_specs=[pl.BlockSpec((tm, n), row), pl.BlockSpec((tm, n), row)],
        out_shape=[jax.ShapeDtypeStruct((m, n), F32), jax.ShapeDtypeStruct((m, n), BF16)],
        scratch_shapes=[pltpu.VMEM((k, n), BF16)],
        compiler_params=_cparams(("arbitrary",)),
        name="out_proj_norm",
    )(a, w, res, gain.reshape(1, n))


def _mm_res_kernel(a_ref, w_ref, r_ref, o_ref, wb_ref):
    @pl.when(_first_m_step())
    def _():
        _stage_weight(w_ref, wb_ref)

    o_ref[...] = r_ref[...] + jnp.dot(a_ref[...], wb_ref[...], preferred_element_type=F32)


def _matmul_residual(a, w, res, tm, tn, name):
    m, k = a.shape
    n = w.shape[1]
    return pl.pallas_call(
        _mm_res_kernel,
        grid=(n // tn, m // tm),
        in_specs=[pl.BlockSpec((tm, k), lambda j, i: (i, 0)),
                  pl.BlockSpec((k, tn), lambda j, i: (0, j)),
                  pl.BlockSpec((tm, tn), lambda j, i: (i, j))],
        out_specs=pl.BlockSpec((tm, tn), lambda j, i: (i, j)),
        out_shape=jax.ShapeDtypeStruct((m, n), F32),
        scratch_shapes=[pltpu.VMEM((k, tn), BF16)],
        compiler_params=_cparams(("arbitrary", "arbitrary")),
        name=name,
    )(a, w, res)


def _merge_kernel(xn_ref, oa_ref, ob_ref, wga_ref, wgb_ref, wuf_ref, wun_ref, o_ref,
                  bga_ref, bgb_ref, buf_ref, bun_ref):
    @pl.when(_first_m_step())
    def _():
        _stage_weight(wga_ref, bga_ref)
        _stage_weight(wgb_ref, bgb_ref)
        _stage_weight(wuf_ref, buf_ref)
        _stage_weight(wun_ref, bun_ref)

    xn = xn_ref[...]
    ga = jax.nn.sigmoid(jnp.dot(xn, bga_ref[...], preferred_element_type=F32))
    ua = jnp.dot(oa_ref[...], buf_ref[...], preferred_element_type=F32)
    acc = ga * ua
    gb = jax.nn.sigmoid(jnp.dot(xn, bgb_ref[...], preferred_element_type=F32))
    ub = jnp.dot(ob_ref[...], bun_ref[...], preferred_element_type=F32)
    o_ref[...] = (acc + gb * ub).astype(o_ref.dtype)


def _merge(xn, oa, ob, wga, wgb, wuf, wun, tm=512, tn=512):
    m, d = xn.shape
    n = wga.shape[1]
    ka = oa.shape[1]
    kb = ob.shape[1]
    row = lambda j, i: (i, 0)
    col = lambda j, i: (0, j)
    return pl.pallas_call(
        _merge_kernel,
        grid=(n // tn, m // tm),
        in_specs=[pl.BlockSpec((tm, d), row), pl.BlockSpec((tm, ka), row),
                  pl.BlockSpec((tm, kb), row),
                  pl.BlockSpec((d, tn), col), pl.BlockSpec((d, tn), col),
                  pl.BlockSpec((ka, tn), col), pl.BlockSpec((kb, tn), col)],
        out_specs=pl.BlockSpec((tm, tn), lambda j, i: (i, j)),
        out_shape=jax.ShapeDtypeStruct((m, n), BF16),
        scratch_shapes=[pltpu.VMEM((d, tn), BF16), pltpu.VMEM((d, tn), BF16),
                        pltpu.VMEM((ka, tn), BF16), pltpu.VMEM((kb, tn), BF16)],
        compiler_params=_cparams(("arbitrary", "arbitrary")),
        name="gated_merge",
    )(xn, oa, ob, wga, wgb, wuf, wun)


def _swiglu_kernel(a_ref, wg_ref, wu_ref, o_ref, bg_ref, bu_ref):
    @pl.when(_first_m_step())
    def _():
        _stage_weight(wg_ref, bg_ref)
        _stage_weight(wu_ref, bu_ref)

    a = a_ref[...]
    gt = jnp.dot(a, bg_ref[...], preferred_element_type=F32)
    up = jnp.dot(a, bu_ref[...], preferred_element_type=F32)
    o_ref[...] = (gt * jax.nn.sigmoid(gt) * up).astype(o_ref.dtype)


def _swiglu(a, wg, wu, tm=1024, tn=512):
    m, k = a.shape
    n = wg.shape[1]
    return pl.pallas_call(
        _swiglu_kernel,
        grid=(n // tn, m // tm),
        in_specs=[pl.BlockSpec((tm, k), lambda j, i: (i, 0)),
                  pl.BlockSpec((k, tn), lambda j, i: (0, j)),
                  pl.BlockSpec((k, tn), lambda j, i: (0, j))],
        out_specs=pl.BlockSpec((tm, tn), lambda j, i: (i, j)),
        out_shape=jax.ShapeDtypeStruct((m, n), BF16),
        scratch_shapes=[pltpu.VMEM((k, tn), BF16), pltpu.VMEM((k, tn), BF16)],
        compiler_params=_cparams(("arbitrary", "arbitrary")),
        name="swiglu_up",
    )(a, wg, wu)


def _split3(x):
    hi = x.astype(BF16)
    r1 = x - hi.astype(F32)
    mid = r1.astype(BF16)
    lo = (r1 - mid.astype(F32)).astype(BF16)
    return hi, mid, lo


def _decay_kernel(z_ref, b_ref, crep_ref, crow_ref, *, blk):
    t = z_ref.shape[0]
    r = lax.broadcasted_iota(jnp.int32, (blk, blk), 0)
    c = lax.broadcasted_iota(jnp.int32, (blk, blk), 1)
    tri = jnp.where(r >= c, 1.0, 0.0).astype(BF16)
    lane = lax.broadcasted_iota(jnp.int32, (blk, LANES), 1)
    carry = jnp.zeros((1, LANES), F32)
    for s in range(t // blk):
        rows = slice(s * blk, (s + 1) * blk)
        z = z_ref[rows, :] + b_ref[...]
        logf = jnp.minimum(z, 0.0) - jnp.log1p(jnp.exp(-jnp.abs(z)))
        hi, mid, lo = _split3(logf)
        cb = (jnp.dot(tri, hi, preferred_element_type=F32)
              + jnp.dot(tri, mid, preferred_element_type=F32)
              + jnp.dot(tri, lo, preferred_element_type=F32)) + carry
        carry = cb[blk - 1:blk, :]
        crow_ref[0, :, rows] = cb.T[:FOX_HEADS, :]
        for h in range(FOX_HEADS):
            col = jnp.sum(jnp.where(lane == h, cb, 0.0), axis=-1, keepdims=True)
            crep_ref[0, h, rows, :] = jnp.broadcast_to(col, (blk, LANES))


def _decay(p3, bias_row, batch, seq, blk=256):
    return pl.pallas_call(
        functools.partial(_decay_kernel, blk=blk),
        grid=(batch,),
        in_specs=[pl.BlockSpec((seq, LANES), lambda b: (b, 6)),
                  pl.BlockSpec((1, LANES), lambda b: (0, 0))],
        out_specs=[pl.BlockSpec((1, FOX_HEADS, seq, LANES), lambda b: (b, 0, 0, 0)),
                   pl.BlockSpec((1, FOX_HEADS, seq), lambda b: (b, 0, 0))],
        out_shape=[jax.ShapeDtypeStruct((batch, FOX_HEADS, seq, LANES), F32),
                   jax.ShapeDtypeStruct((batch, FOX_HEADS, seq), F32)],
        compiler_params=_cparams(("arbitrary",)),
        name="fox_decay_cumsum",
    )(p3, bias_row)


def _transpose_bf16(x):
    return x.astype(F32).T.astype(BF16)


def _fox_kernel(q_ref, k_ref, v_ref, crep_ref, crow_ref, o_ref, vt_ref, *, tq):
    i = pl.program_id(2)

    @pl.when(i == 0)
    def _():
        vt_ref[...] = _transpose_bf16(v_ref[...])

    qt = _transpose_bf16(q_ref[...])
    ci = crow_ref[0, 0]
    reps = tq // LANES

    def scores(k0):
        k = k_ref[pl.ds(k0, tq), :]
        cj = crep_ref[0, 0, pl.ds(k0, tq), :]
        st = jnp.dot(k, qt, preferred_element_type=F32)
        return st + ci - jnp.concatenate([cj] * reps, axis=1)

    k0 = pl.multiple_of(i * tq, tq)
    rk = lax.broadcasted_iota(jnp.int32, (tq, tq), 0)
    cq = lax.broadcasted_iota(jnp.int32, (tq, tq), 1)
    sm = jnp.where(rk <= cq, scores(k0), NEG)
    m = jnp.max(sm, axis=0, keepdims=True)
    p = jnp.exp(sm - m)
    l = jnp.sum(p, axis=0, keepdims=True)
    acc = jnp.dot(vt_ref[:, pl.ds(k0, tq)], p.astype(BF16), preferred_element_type=F32)

    def body(it, carry):
        m, l, acc = carry
        k0 = pl.multiple_of((i - 1 - it) * tq, tq)
        st = scores(k0)
        m_new = jnp.maximum(m, jnp.max(st, axis=0, keepdims=True))
        p = jnp.exp(st - m_new)
        alpha = jnp.exp(m - m_new)
        l = alpha * l + jnp.sum(p, axis=0, keepdims=True)
        acc = alpha * acc + jnp.dot(vt_ref[:, pl.ds(k0, tq)], p.astype(BF16),
                                    preferred_element_type=F32)
        return m_new, l, acc

    m, l, acc = lax.fori_loop(0, i, body, (m, l, acc))
    out = acc * (1.0 / jnp.maximum(l, 1e-30))
    o_ref[...] = out.T.astype(o_ref.dtype)


def _fox_attention(p1, crep, crow4, batch, seq, tq=512):
    nq = seq // tq
    h = FOX_HEADS
    return pl.pallas_call(
        functools.partial(_fox_kernel, tq=tq),
        grid=(batch, h, nq),
        in_specs=[pl.BlockSpec((tq, 128), lambda b, hh, i: (b * nq + i, hh)),
                  pl.BlockSpec((seq, 128), lambda b, hh, i: (b, h + hh)),
                  pl.BlockSpec((seq, 128), lambda b, hh, i: (b, 2 * h + hh)),
                  pl.BlockSpec((1, 1, seq, LANES), lambda b, hh, i: (b, hh, 0, 0)),
                  pl.BlockSpec((1, 1, 1, tq), lambda b, hh, i: (b, hh, 0, i))],
        out_specs=pl.BlockSpec((tq, 128), lambda b, hh, i: (b * nq + i, hh)),
        out_shape=jax.ShapeDtypeStruct((batch * seq, FOX_WIDTH), BF16),
        scratch_shapes=[pltpu.VMEM((FOX_HEAD_DIM, seq), BF16)],
        compiler_params=_cparams(("arbitrary", "arbitrary", "arbitrary")),
        name="fox_attention",
    )(p1, p1, p1, crep, crow4)


def _compress_one(z_refs, pe_ref, w1_ref, w2_ref, nblk):
    half = CMP_BLOCK // 2
    first = jnp.zeros((nblk, CMP_HIDDEN), F32)
    second = jnp.zeros((nblk, CMP_HIDDEN), F32)
    for p in range(half):
        rows = pl.ds(p, nblk, stride=CMP_STRIDE)
        zp = [z_ref[rows, :] for z_ref in z_refs]
        zp = zp[0] if len(zp) == 1 else jnp.concatenate(zp, axis=1)
        first += jnp.dot((zp + pe_ref[p:p + 1, :]).astype(BF16), w1_ref[p],
                         preferred_element_type=F32)
        second += jnp.dot((zp + pe_ref[half + p:half + p + 1, :]).astype(BF16),
                          w1_ref[half + p], preferred_element_type=F32)
    hid = first + pltpu.roll(second, nblk - 1, 0)
    act = (hid * jax.nn.sigmoid(hid)).astype(BF16)
    return jnp.dot(act, w2_ref[...], preferred_element_type=F32)


def _compress_kernel(zk0_ref, zk1_ref, zv_ref, pek_ref, w1k_ref, w2k_ref, gk_ref,
                     pev_ref, w1v_ref, w2v_ref, kc_ref, vc_ref, *, nblk):
    kc = _compress_one((zk0_ref, zk1_ref), pek_ref, w1k_ref, w2k_ref, nblk)
    ms = jnp.sum(kc * kc, axis=-1, keepdims=True) * (1.0 / NSA_QK_DIM)
    kc_ref[...] = (kc * lax.rsqrt(ms + RMS_EPS) * gk_ref[...]).astype(kc_ref.dtype)
    vc = _compress_one((zv_ref,), pev_ref, w1v_ref, w2v_ref, nblk)
    vc_ref[...] = vc.T.astype(vc_ref.dtype)


def _compress(p3, pek, w1k, w2k, gk, pev, w1v, w2v, batch, seq):
    g = NSA_KV_GROUPS
    nblk = seq // CMP_STRIDE
    full2 = lambda b, gg: (0, 0)
    full3 = lambda b, gg: (0, 0, 0)
    return pl.pallas_call(
        functools.partial(_compress_kernel, nblk=nblk),
        grid=(batch, g),
        in_specs=[pl.BlockSpec((seq, LANES), lambda b, gg: (b, 2 * gg)),
                  pl.BlockSpec((seq, LANES), lambda b, gg: (b, 2 * gg + 1)),
                  pl.BlockSpec((seq, NSA_V_DIM), lambda b, gg: (b, 4 + gg)),
                  pl.BlockSpec(pek.shape, full2), pl.BlockSpec(w1k.shape, full3),
                  pl.BlockSpec(w2k.shape, full2), pl.BlockSpec(gk.shape, full2),
                  pl.BlockSpec(pev.shape, full2), pl.BlockSpec(w1v.shape, full3),
                  pl.BlockSpec(w2v.shape, full2)],
        out_specs=[pl.BlockSpec((nblk, NSA_QK_PAD), lambda b, gg: (b * g + gg, 0)),
                   pl.BlockSpec((NSA_V_DIM, nblk), lambda b, gg: (b * g + gg, 0))],
        out_shape=[jax.ShapeDtypeStruct((batch * g * nblk, NSA_QK_PAD), BF16),
                   jax.ShapeDtypeStruct((batch * g * NSA_V_DIM, nblk), BF16)],
        compiler_params=_cparams(("arbitrary", "arbitrary")),
        name="nsa_compress",
    )(p3, p3, p3, pek, w1k, w2k, gk, pev, w1v, w2v)


def _nsa_kernel(q_ref, kc_ref, vct_ref, ks_ref, vs_ref, kw_ref, vw_ref, gl_ref,
                slope_ref, ovt_ref, ext_ref, o_ref,
                vst_ref, vwt_ref, selx_ref, glt_ref, *, tq, n_cmp, n_sel):
    g = pl.program_id(1)
    i = pl.program_id(2)
    t0 = i * tq
    hpg = NSA_HPG
    dv = NSA_V_DIM

    @pl.when(i == 0)
    def _():
        vst_ref[...] = _transpose_bf16(vs_ref[...])
        vwt_ref[...] = _transpose_bf16(vw_ref[...])

    qb = q_ref[...]
    q4t = jnp.concatenate(
        [_transpose_bf16(qb[:, hh * NSA_QK_PAD:(hh + 1) * NSA_QK_PAD]) for hh in range(hpg)],
        axis=1)
    slopes = [slope_ref[0, hh:hh + 1, :] for hh in range(hpg)]
    rk = lax.broadcasted_iota(jnp.int32, (LANES, tq), 0)
    cq = lax.broadcasted_iota(jnp.int32, (LANES, tq), 1)

    def tile(k_ref, vt_ref, k0, rows, mask_fn):
        k = k_ref[pl.ds(k0, rows), :]
        st = jnp.dot(k, q4t, preferred_element_type=F32)
        dist = (lax.broadcasted_iota(jnp.int32, (rows, tq), 1)
                - lax.broadcasted_iota(jnp.int32, (rows, tq), 0)) + (t0 - k0)
        mask = mask_fn(dist, k0, rows)
        distf = dist.astype(F32)
        bias = jnp.concatenate(
            [jnp.where(mask, -slopes[hh] * distf, NEG) for hh in range(hpg)], axis=1)
        return st + bias, vt_ref[:, pl.ds(k0, rows)]

    def mask_slc(dist, k0, rows):
        return jnp.where(dist >= 0, selx_ref[pl.ds(k0, rows), :], 0.0) > 0.5

    def mask_win(dist, k0, rows):
        return jnp.where(dist >= 0, dist, WINDOW) < WINDOW

    k0 = pl.multiple_of(jnp.maximum(t0 - WINDOW, 0), LANES)
    sm, vt = tile(kw_ref, vwt_ref, k0, WINDOW + tq, mask_win)
    p = jnp.exp(sm - jnp.max(sm, axis=0, keepdims=True))
    o_win = (jnp.dot(vt, p.astype(BF16), preferred_element_type=F32)
             * (1.0 / jnp.maximum(jnp.sum(p, axis=0, keepdims=True), 1e-30)))

    s_c = jnp.dot(kc_ref[...], q4t, preferred_element_type=F32)
    dist_c = (t0 + cq) - (CMP_STRIDE * rk + (CMP_BLOCK - 1))
    mask_c = jnp.where(rk < n_cmp, dist_c, -1) >= 0
    dist_cf = dist_c.astype(F32)
    probs = []
    p_sum = jnp.zeros((LANES, tq), F32)
    for hh in range(hpg):
        sh = s_c[:, hh * tq:(hh + 1) * tq] - slopes[hh] * dist_cf
        sm = jnp.where(mask_c, sh, NEG)
        m = jnp.max(sm, axis=0, keepdims=True)
        e = jnp.where(mask_c, jnp.exp(sm - m), 0.0)
        p = e * (1.0 / jnp.maximum(jnp.sum(e, axis=0, keepdims=True), 1e-30))
        probs.append(p)
        p_sum = p_sum + p
    o_cmp = jnp.dot(vct_ref[...], jnp.concatenate(probs, axis=1).astype(BF16),
                    preferred_element_type=F32)

    ph = p_sum.astype(BF16)
    plo = (p_sum - ph.astype(F32)).astype(BF16)
    ovt = ovt_ref[...]
    imp = (jnp.dot(ovt, ph, preferred_element_type=F32)
           + jnp.dot(ovt, plo, preferred_element_type=F32))[:n_sel, :]

    rj = lax.broadcasted_iota(jnp.int32, (n_sel, tq), 0)
    tcol = t0 + lax.broadcasted_iota(jnp.int32, (n_sel, tq), 1)
    back = (tcol >> (SEL_BLOCK.bit_length() - 1)) - rj
    elig = back >= 0
    forced = jnp.where(rj == 0, 0, jnp.where(elig, back, SEL_LOCAL)) < SEL_LOCAL
    score = jnp.where(elig, jnp.where(forced, FORCE_SCORE, imp), -1.0)
    rank = jnp.zeros((n_sel, tq), F32)
    for jp in range(n_sel):
        row = score[jp:jp + 1, :]
        later = jnp.where(rj > jp, 1.0, 0.0)
        rank = rank + jnp.where(row > score, 1.0, jnp.where(row == score, later, 0.0))
    sel = jnp.where(elig, jnp.where(rank < SEL_TOPK, 1.0, 0.0), 0.0)
    sel = jnp.concatenate([sel, jnp.zeros((LANES - n_sel, tq), F32)], axis=0).astype(BF16)
    selx_ref[...] = jnp.dot(ext_ref[...], sel, preferred_element_type=F32)

    n_full = t0 // SLC_TILE
    sm, vt = tile(ks_ref, vst_ref, pl.multiple_of(n_full * SLC_TILE, SLC_TILE), SLC_TILE,
                  mask_slc)
    m = jnp.max(sm, axis=0, keepdims=True)
    p = jnp.exp(sm - m)
    l = jnp.sum(p, axis=0, keepdims=True)
    acc = jnp.dot(vt, p.astype(BF16), preferred_element_type=F32)

    def body(it, carry):
        m, l, acc = carry
        k0 = pl.multiple_of((n_full - 1 - it) * SLC_TILE, SLC_TILE)
        sm, vt = tile(ks_ref, vst_ref, k0, SLC_TILE, mask_slc)
        m_new = jnp.maximum(m, jnp.max(sm, axis=0, keepdims=True))
        p = jnp.exp(sm - m_new)
        alpha = jnp.exp(m - m_new)
        l = alpha * l + jnp.sum(p, axis=0, keepdims=True)
        acc = alpha * acc + jnp.dot(vt, p.astype(BF16), preferred_element_type=F32)
        return m_new, l, acc

    m, l, acc = lax.fori_loop(0, n_full, body, (m, l, acc))
    o_slc = acc * (1.0 / jnp.maximum(l, 1e-30))

    glt_ref[...] = jax.nn.sigmoid(gl_ref[...]).T
    for hh in range(hpg):
        base = FOX_HEADS + (g * hpg + hh) * 3
        lanes = slice(hh * tq, (hh + 1) * tq)
        out = (glt_ref[pl.ds(base, 1), :] * o_cmp[:, lanes]
               + glt_ref[pl.ds(base + 1, 1), :] * o_slc[:, lanes]
               + glt_ref[pl.ds(base + 2, 1), :] * o_win[:, lanes])
        o_ref[:, hh * dv:(hh + 1) * dv] = out.T.astype(o_ref.dtype)


def _nsa_attention(pq, pk, pv, p3, kc, vct, slopes, ovt, ext, batch, seq):
    tq = LANES
    g = NSA_KV_GROUPS
    nq = seq // tq
    nblk = seq // CMP_STRIDE
    n_cmp = nblk - CMP_BLOCK // CMP_STRIDE + 1
    qw = NSA_HPG * NSA_QK_PAD
    return pl.pallas_call(
        functools.partial(_nsa_kernel, tq=tq, n_cmp=n_cmp, n_sel=seq // SEL_BLOCK),
        grid=(batch, g, nq),
        in_specs=[
            pl.BlockSpec((tq, qw), lambda b, gg, i: (b * nq + i, gg)),
            pl.BlockSpec((nblk, NSA_QK_PAD), lambda b, gg, i: (b * g + gg, 0)),
            pl.BlockSpec((NSA_V_DIM, nblk), lambda b, gg, i: (b * g + gg, 0)),
            pl.BlockSpec((seq, NSA_QK_PAD), lambda b, gg, i: (b, gg)),
            pl.BlockSpec((seq, NSA_V_DIM), lambda b, gg, i: (b, gg)),
            pl.BlockSpec((seq, NSA_QK_PAD), lambda b, gg, i: (b, g + gg)),
            pl.BlockSpec((seq, NSA_V_DIM), lambda b, gg, i: (b, g + gg)),
            pl.BlockSpec((tq, LANES), lambda b, gg, i: (b * nq + i, 6)),
            pl.BlockSpec((1, 8, LANES), lambda b, gg, i: (gg, 0, 0)),
            pl.BlockSpec(ovt.shape, lambda b, gg, i: (0, 0)),
            pl.BlockSpec(ext.shape, lambda b, gg, i: (0, 0)),
        ],
        out_specs=pl.BlockSpec((tq, NSA_HPG * NSA_V_DIM), lambda b, gg, i: (b * nq + i, gg)),
        out_shape=jax.ShapeDtypeStruct((batch * seq, NSA_WIDTH), BF16),
        scratch_shapes=[pltpu.VMEM((NSA_V_DIM, seq), BF16), pltpu.VMEM((NSA_V_DIM, seq), BF16),
                        pltpu.VMEM((seq, tq), F32), pltpu.VMEM((LANES, tq), F32)],
        compiler_params=_cparams(("arbitrary", "arbitrary", "arbitrary")),
        name="nsa_attention",
    )(pq, kc, vct, pk, pv, pk, pv, p3, slopes, ovt, ext)


def _pad_heads(w, heads):
    k = w.shape[0]
    w = w.reshape(k, heads, NSA_QK_DIM)
    w = jnp.pad(w, ((0, 0), (0, 0), (0, NSA_QK_PAD - NSA_QK_DIM)))
    return w.reshape(k, heads * NSA_QK_PAD)


def _pad_gain(gain, scale=1.0):
    return jnp.pad(gain * scale, (0, NSA_QK_PAD - NSA_QK_DIM))


def _overlap_matrix(nc, ns):
    i = np.arange(nc)[:, None]
    j = np.arange(ns)[None, :]
    lo = np.maximum(i * CMP_STRIDE, j * SEL_BLOCK)
    hi = np.minimum(i * CMP_STRIDE + CMP_BLOCK, (j + 1) * SEL_BLOCK)
    return (np.maximum(hi - lo, 0) / CMP_STRIDE).astype(np.float32)


def kernel(x, norm_attn, w_in, fox_f_bias, fox_q_gain, fox_k_gain,
           nsa_q_gain, nsa_kc_gain, nsa_ks_gain, nsa_kw_gain,
           cmp_pe_k, cmp_w1_k, cmp_w2_k, cmp_pe_v, cmp_w1_v, cmp_w2_v,
           w_up_fox, w_up_nsa, w_out, norm_ffn, w_ffn_gate, w_ffn_up, w_ffn_down):
    batch, seq, d = x.shape
    m = batch * seq
    depth = w_in.shape[0]
    pts = [0] + [int(p) for p in np.cumsum(IN_SPLITS)]
    nblk = seq // CMP_STRIDE
    n_cmp = nblk - CMP_BLOCK // CMP_STRIDE + 1
    ns = seq // SEL_BLOCK

    slopes_h = jnp.exp2(-8.0 * jnp.arange(1, NSA_HEADS + 1, dtype=F32) / NSA_HEADS)
    slopes = jnp.broadcast_to(
        jnp.pad(slopes_h.reshape(NSA_KV_GROUPS, NSA_HPG), ((0, 0), (0, 8 - NSA_HPG)))[:, :, None],
        (NSA_KV_GROUPS, 8, LANES))
    ovt_np = np.zeros((LANES, nblk), np.float32)
    ovt_np[:ns, :n_cmp] = _overlap_matrix(n_cmp, ns).T
    ovt = jnp.asarray(ovt_np, BF16)
    ext_np = np.zeros((seq, LANES), np.float32)
    ext_np[np.arange(seq), np.arange(seq) // SEL_BLOCK] = 1.0
    ext = jnp.asarray(ext_np, BF16)

    xf = x.reshape(m, d)
    for l in range(depth):
        w = w_in[l]
        seg = [w[:, pts[s]:pts[s + 1]] for s in range(len(IN_SPLITS))]
        (wfq, wfk, wfv, wfl, wnq, wkc, wvc, wks, wvs, wkw, wvw, wng, wga, wgb) = seg

        gain1 = jnp.concatenate([jnp.tile(fox_q_gain[l] * FOX_HEAD_DIM ** -0.5, FOX_HEADS),
                                 jnp.tile(fox_k_gain[l], FOX_HEADS),
                                 jnp.ones((FOX_WIDTH,), F32)])
        flag1 = jnp.concatenate([jnp.ones((2 * FOX_WIDTH,), F32), jnp.zeros((FOX_WIDTH,), F32)])
        gain_q = jnp.tile(_pad_gain(nsa_q_gain[l], NSA_QK_DIM ** -0.5), NSA_HEADS)
        gain_k = jnp.concatenate([jnp.tile(_pad_gain(nsa_ks_gain[l]), NSA_KV_GROUPS),
                                  jnp.tile(_pad_gain(nsa_kw_gain[l]), NSA_KV_GROUPS)])
        w_k = jnp.concatenate([wks, wkw], axis=1)
        w_v = jnp.concatenate([wvs, wvw], axis=1)
        n_small = FOX_HEADS + 3 * NSA_HEADS
        w3 = jnp.concatenate([_pad_heads(wkc, NSA_KV_GROUPS), wvc, wfl, wng,
                              jnp.zeros((d, LANES - n_small), F32)], axis=1)

        xn = _rmsnorm(xf, norm_attn[l])
        p1 = _project(xn, w, 3 * FOX_WIDTH // 512, 512, 0, gain1, flag1, BF16, 1024,
                      "proj_fox", group=128, count=128)
        pq = _project(xn, wnq, 2, NSA_HPG * NSA_QK_DIM, 0, gain_q, jnp.ones_like(gain_q), BF16,
                      1024, "proj_nsa_q", pad_from=NSA_QK_DIM, pad_to=NSA_QK_PAD,
                      group=NSA_QK_PAD, count=NSA_QK_DIM)
        pk = _project(xn, w_k, 1, 2 * KV_K, 0, gain_k, jnp.ones_like(gain_k), BF16,
                      1024, "proj_nsa_k", pad_from=NSA_QK_DIM, pad_to=NSA_QK_PAD,
                      group=NSA_QK_PAD, count=NSA_QK_DIM)
        pv = _project(xn, w_v, 1, 2 * KV_V, 0, None, None, BF16, 1024, "proj_nsa_v")
        p3 = _project(xn, w3, 1, w3.shape[1], 0, None, None, F32, 1024, "proj_f32")

        bias_row = jnp.pad(fox_f_bias[l], (0, LANES - FOX_HEADS)).reshape(1, LANES)
        crep, crow = _decay(p3, bias_row, batch, seq)
        o_a = _fox_attention(p1, crep, crow.reshape(batch, FOX_HEADS, 1, seq), batch, seq)

        pad_d = NSA_QK_PAD - NSA_QK_DIM
        pek = jnp.pad(cmp_pe_k[l], ((0, 0), (0, pad_d)))
        w1k = jnp.pad(cmp_w1_k[l].reshape(CMP_BLOCK, NSA_QK_DIM, CMP_HIDDEN),
                      ((0, 0), (0, pad_d), (0, 0))).astype(BF16)
        w2k = jnp.pad(cmp_w2_k[l], ((0, 0), (0, pad_d))).astype(BF16)
        gk = _pad_gain(nsa_kc_gain[l]).reshape(1, NSA_QK_PAD)
        w1v = cmp_w1_v[l].reshape(CMP_BLOCK, NSA_V_DIM, CMP_HIDDEN).astype(BF16)
        w2v = cmp_w2_v[l].astype(BF16)
        kc, vct = _compress(p3, pek, w1k, w2k, gk, cmp_pe_v[l], w1v, w2v, batch, seq)
        o_b = _nsa_attention(pq, pk, pv, p3, kc, vct, slopes, ovt, ext, batch, seq)

        merged = _merge(xn, o_a, o_b, wga, wgb, w_up_fox[l], w_up_nsa[l])
        hres, hn = _out_proj_norm(merged, w_out[l], xf, norm_ffn[l])

        act = _swiglu(hn, w_ffn_gate[l], w_ffn_up[l])
        xf = _matmul_residual(act, w_ffn_down[l], hres, 512, 512, "ffn_down")
    return xf.reshape(batch, seq, d)
```

```python
import functools

import numpy as np
import jax
import jax.numpy as jnp
from jax import lax
from jax.experimental import pallas as pl
from jax.experimental.pallas import tpu as pltpu

F32 = jnp.float32
BF16 = jnp.bfloat16

D_MODEL = 2048
FOX_HEADS = 8
FOX_HEAD_DIM = 128
FOX_WIDTH = FOX_HEADS * FOX_HEAD_DIM
NSA_HEADS = 8
NSA_KV_GROUPS = 2
NSA_HPG = NSA_HEADS // NSA_KV_GROUPS
NSA_QK_DIM = 192
NSA_QK_PAD = 256
NSA_V_DIM = 128
NSA_WIDTH = NSA_HEADS * NSA_V_DIM
CMP_BLOCK = 32
CMP_STRIDE = 16
CMP_HIDDEN = 256
SEL_BLOCK = 64
SEL_TOPK = 16
SEL_LOCAL = 2
FORCE_SCORE = 1.0e4
WINDOW = 512
KV_K = NSA_KV_GROUPS * NSA_QK_DIM
KV_V = NSA_KV_GROUPS * NSA_V_DIM
D_FF = -(-(8 * D_MODEL) // (3 * 256)) * 256
RMS_EPS = 1e-6
IN_SPLITS = (FOX_WIDTH, FOX_WIDTH, FOX_WIDTH, FOX_HEADS,
             NSA_HEADS * NSA_QK_DIM, KV_K, KV_V, KV_K, KV_V, KV_K, KV_V,
             3 * NSA_HEADS, D_MODEL, D_MODEL)

LANES = 128
NEG = -1.0e30
SLC_TILE = 512
VMEM_LIMIT = 56 * 1024 * 1024

_NT = (((1,), (1,)), ((), ()))


def _cparams(sem):
    return pltpu.CompilerParams(dimension_semantics=sem, vmem_limit_bytes=VMEM_LIMIT)


def _rms_kernel(x_ref, g_ref, o_ref):
    x = x_ref[...]
    ms = jnp.mean(x * x, axis=-1, keepdims=True)
    o_ref[...] = (x * lax.rsqrt(ms + RMS_EPS) * g_ref[...]).astype(o_ref.dtype)


def _rmsnorm(x, gain, tm=512):
    m, d = x.shape
    return pl.pallas_call(
        _rms_kernel,
        grid=(m // tm,),
        in_specs=[pl.BlockSpec((tm, d), lambda i: (i, 0)),
                  pl.BlockSpec((1, d), lambda i: (0, 0))],
        out_specs=pl.BlockSpec((tm, d), lambda i: (i, 0)),
        out_shape=jax.ShapeDtypeStruct((m, d), BF16),
        compiler_params=_cparams(("arbitrary",)),
        name="rmsnorm",
    )(x, gain.reshape(1, d))


STAGE_CHUNK = 512


def _stage_weight(w_ref, wb_ref, transposed=False, pad_from=0, pad_to=0):
    if not transposed:
        wb_ref[...] = w_ref[...].astype(BF16)
        return
    n_in, k = w_ref.shape
    for c in range(k // STAGE_CHUNK):
        cols = slice(c * STAGE_CHUNK, (c + 1) * STAGE_CHUNK)
        w = w_ref[:, cols]
        if pad_from != pad_to:
            zero = jnp.zeros((pad_to - pad_from, STAGE_CHUNK), F32)
            w = jnp.concatenate(
                [piece for h in range(n_in // pad_from)
                 for piece in (w[h * pad_from:(h + 1) * pad_from, :], zero)], axis=0)
        wb_ref[cols, :] = w.T.astype(BF16)


def _first_m_step():
    return pl.program_id(1) == 0


def _proj_kernel(a_ref, w_ref, gain_ref, flag_ref, o_ref, wb_ref, *, pad_from, pad_to,
                 group, count):
    @pl.when(_first_m_step())
    def _():
        _stage_weight(w_ref, wb_ref, True, pad_from, pad_to)

    y = jnp.dot(a_ref[...], wb_ref[...], preferred_element_type=F32)
    if group == 0:
        o_ref[...] = y.astype(o_ref.dtype)
        return
    for c in range(y.shape[1] // group):
        sl = slice(c * group, (c + 1) * group)
        yc = y[:, sl]
        ss = jnp.sum(yc * yc, axis=-1, keepdims=True)
        rs = lax.rsqrt(ss * (1.0 / count) + RMS_EPS)
        scale = jnp.where(flag_ref[:, sl] > 0.0, rs, 1.0)
        o_ref[:, sl] = (yc * scale * gain_ref[:, sl]).astype(o_ref.dtype)


def _project(a, wt, n_tiles, tn_in, gain, flag, out_dtype, tm, name,
             pad_from=0, pad_to=0, group=0, count=1):
    m, k = a.shape
    tn_out = tn_in if pad_from == pad_to else tn_in // pad_from * pad_to
    n_out = n_tiles * tn_out
    if gain is None:
        gain = jnp.ones((n_out,), F32)
        flag = jnp.zeros((n_out,), F32)
    return pl.pallas_call(
        functools.partial(_proj_kernel, pad_from=pad_from, pad_to=pad_to, group=group,
                          count=count),
        grid=(n_tiles, m // tm),
        in_specs=[pl.BlockSpec((tm, k), lambda j, i: (i, 0)),
                  pl.BlockSpec((tn_in, k), lambda j, i: (j, 0)),
                  pl.BlockSpec((1, tn_out), lambda j, i: (0, j)),
                  pl.BlockSpec((1, tn_out), lambda j, i: (0, j))],
        out_specs=pl.BlockSpec((tm, tn_out), lambda j, i: (i, j)),
        out_shape=jax.ShapeDtypeStruct((m, n_out), out_dtype),
        scratch_shapes=[pltpu.VMEM((k, tn_out), BF16)],
        compiler_params=_cparams(("arbitrary", "arbitrary")),
        name=name,
    )(a, wt, gain.reshape(1, n_out), flag.reshape(1, n_out))


def _out_norm_kernel(a_ref, w_ref, r_ref, g_ref, h_ref, hn_ref, wb_ref):
    @pl.when(pl.program_id(0) == 0)
    def _():
        _stage_weight(w_ref, wb_ref)

    h = r_ref[...] + jnp.dot(a_ref[...], wb_ref[...], preferred_element_type=F32)
    h_ref[...] = h
    ms = jnp.mean(h * h, axis=-1, keepdims=True)
    hn_ref[...] = (h * lax.rsqrt(ms + RMS_EPS) * g_ref[...]).astype(hn_ref.dtype)


def _out_proj_norm(a, w, res, gain, tm=512):
    m, k = a.shape
    n = w.shape[1]
    full = lambda i: (0, 0)
    row = lambda i: (i, 0)
    return pl.pallas_call(
        _out_norm_kernel,
        grid=(m // tm,),
        in_specs=[pl.BlockSpec((tm, k), row),
                  pl.BlockSpec((k, n), full, pipeline_mode=pl.Buffered(1)),
                  pl.BlockSpec((tm, n), row), pl.BlockSpec((1, n), full)],
        out_specs=[pl.BlockSpec((tm, n), row), pl.BlockSpec((tm, n), row)],
        out_shape=[jax.ShapeDtypeStruct((m, n), F32), jax.ShapeDtypeStruct((m, n), BF16)],
        scratch_shapes=[pltpu.VMEM((k, n), BF16)],
        compiler_params=_cparams(("arbitrary",)),
        name="out_proj_norm",
    )(a, w, res, gain.reshape(1, n))


def _mm_res_kernel(a_ref, w_ref, r_ref, o_ref, wb_ref):
    @pl.when(_first_m_step())
    def _():
        _stage_weight(w_ref, wb_ref)

    o_ref[...] = r_ref[...] + jnp.dot(a_ref[...], wb_ref[...], preferred_element_type=F32)


def _matmul_residual(a, w, res, tm, tn, name):
    m, k = a.shape
    n = w.shape[1]
    return pl.pallas_call(
        _mm_res_kernel,
        grid=(n // tn, m // tm),
        in_specs=[pl.BlockSpec((tm, k), lambda j, i: (i, 0)),
                  pl.BlockSpec((k, tn), lambda j, i: (0, j)),
                  pl.BlockSpec((tm, tn), lambda j, i: (i, j))],
        out_specs=pl.BlockSpec((tm, tn), lambda j, i: (i, j)),
        out_shape=jax.ShapeDtypeStruct((m, n), F32),
        scratch_shapes=[pltpu.VMEM((k, tn), BF16)],
        compiler_params=_cparams(("arbitrary", "arbitrary")),
        name=name,
    )(a, w, res)


def _merge_kernel(xn_ref, oa_ref, ob_ref, wga_ref, wgb_ref, wuf_ref, wun_ref, o_ref,
                  bga_ref, bgb_ref, buf_ref, bun_ref):
    @pl.when(_first_m_step())
    def _():
        _stage_weight(wga_ref, bga_ref, True)
        _stage_weight(wgb_ref, bgb_ref, True)
        _stage_weight(wuf_ref, buf_ref)
        _stage_weight(wun_ref, bun_ref)

    xn = xn_ref[...]
    ga = jax.nn.sigmoid(jnp.dot(xn, bga_ref[...], preferred_element_type=F32))
    ua = jnp.dot(oa_ref[...], buf_ref[...], preferred_element_type=F32)
    acc = ga * ua
    gb = jax.nn.sigmoid(jnp.dot(xn, bgb_ref[...], preferred_element_type=F32))
    ub = jnp.dot(ob_ref[...], bun_ref[...], preferred_element_type=F32)
    o_ref[...] = (acc + gb * ub).astype(o_ref.dtype)


def _merge(xn, oa, ob, wga_t, wgb_t, wuf, wun, tm=512, tn=512):
    m, d = xn.shape
    n = wga_t.shape[0]
    ka = oa.shape[1]
    kb = ob.shape[1]
    row = lambda j, i: (i, 0)
    col = lambda j, i: (0, j)
    return pl.pallas_call(
        _merge_kernel,
        grid=(n // tn, m // tm),
        in_specs=[pl.BlockSpec((tm, d), row), pl.BlockSpec((tm, ka), row),
                  pl.BlockSpec((tm, kb), row),
                  pl.BlockSpec((tn, d), lambda j, i: (j, 0)),
                  pl.BlockSpec((tn, d), lambda j, i: (j, 0)),
                  pl.BlockSpec((ka, tn), col), pl.BlockSpec((kb, tn), col)],
        out_specs=pl.BlockSpec((tm, tn), lambda j, i: (i, j)),
        out_shape=jax.ShapeDtypeStruct((m, n), BF16),
        scratch_shapes=[pltpu.VMEM((d, tn), BF16), pltpu.VMEM((d, tn), BF16),
                        pltpu.VMEM((ka, tn), BF16), pltpu.VMEM((kb, tn), BF16)],
        compiler_params=_cparams(("arbitrary", "arbitrary")),
        name="gated_merge",
    )(xn, oa, ob, wga_t, wgb_t, wuf, wun)


def _swiglu_kernel(a_ref, wg_ref, wu_ref, o_ref, bg_ref, bu_ref):
    @pl.when(_first_m_step())
    def _():
        _stage_weight(wg_ref, bg_ref)
        _stage_weight(wu_ref, bu_ref)

    a = a_ref[...]
    gt = jnp.dot(a, bg_ref[...], preferred_element_type=F32)
    up = jnp.dot(a, bu_ref[...], preferred_element_type=F32)
    o_ref[...] = (gt * jax.nn.sigmoid(gt) * up).astype(o_ref.dtype)


def _swiglu(a, wg, wu, tm=1024, tn=512):
    m, k = a.shape
    n = wg.shape[1]
    return pl.pallas_call(
        _swiglu_kernel,
        grid=(n // tn, m // tm),
        in_specs=[pl.BlockSpec((tm, k), lambda j, i: (i, 0)),
                  pl.BlockSpec((k, tn), lambda j, i: (0, j)),
                  pl.BlockSpec((k, tn), lambda j, i: (0, j))],
        out_specs=pl.BlockSpec((tm, tn), lambda j, i: (i, j)),
        out_shape=jax.ShapeDtypeStruct((m, n), BF16),
        scratch_shapes=[pltpu.VMEM((k, tn), BF16), pltpu.VMEM((k, tn), BF16)],
        compiler_params=_cparams(("arbitrary", "arbitrary")),
        name="swiglu_up",
    )(a, wg, wu)


def _split3(x):
    hi = x.astype(BF16)
    r1 = x - hi.astype(F32)
    mid = r1.astype(BF16)
    lo = (r1 - mid.astype(F32)).astype(BF16)
    return hi, mid, lo


def _decay_kernel(z_ref, b_ref, crep_ref, crow_ref, *, blk):
    t = z_ref.shape[0]
    r = lax.broadcasted_iota(jnp.int32, (blk, blk), 0)
    c = lax.broadcasted_iota(jnp.int32, (blk, blk), 1)
    tri = jnp.where(r >= c, 1.0, 0.0).astype(BF16)
    lane = lax.broadcasted_iota(jnp.int32, (blk, LANES), 1)
    carry = jnp.zeros((1, LANES), F32)
    for s in range(t // blk):
        rows = slice(s * blk, (s + 1) * blk)
        z = z_ref[rows, :] + b_ref[...]
        logf = jnp.minimum(z, 0.0) - jnp.log1p(jnp.exp(-jnp.abs(z)))
        hi, mid, lo = _split3(logf)
        cb = (jnp.dot(tri, hi, preferred_element_type=F32)
              + jnp.dot(tri, mid, preferred_element_type=F32)
              + jnp.dot(tri, lo, preferred_element_type=F32)) + carry
        carry = cb[blk - 1:blk, :]
        crow_ref[0, :, rows] = cb.T[:FOX_HEADS, :]
        for h in range(FOX_HEADS):
            col = jnp.sum(jnp.where(lane == h, cb, 0.0), axis=-1, keepdims=True)
            crep_ref[0, h, rows, :] = jnp.broadcast_to(col, (blk, LANES))


def _decay(p3, bias_row, batch, seq, blk=256):
    return pl.pallas_call(
        functools.partial(_decay_kernel, blk=blk),
        grid=(batch,),
        in_specs=[pl.BlockSpec((seq, LANES), lambda b: (b, 6)),
                  pl.BlockSpec((1, LANES), lambda b: (0, 0))],
        out_specs=[pl.BlockSpec((1, FOX_HEADS, seq, LANES), lambda b: (b, 0, 0, 0)),
                   pl.BlockSpec((1, FOX_HEADS, seq), lambda b: (b, 0, 0))],
        out_shape=[jax.ShapeDtypeStruct((batch, FOX_HEADS, seq, LANES), F32),
                   jax.ShapeDtypeStruct((batch, FOX_HEADS, seq), F32)],
        compiler_params=_cparams(("arbitrary",)),
        name="fox_decay_cumsum",
    )(p3, bias_row)


def _transpose_bf16(x):
    return x.astype(F32).T.astype(BF16)


def _fox_kernel(q_ref, k_ref, v_ref, crep_ref, crow_ref, o_ref, vt_ref, *, tq):
    i = pl.program_id(2)

    @pl.when(i == 0)
    def _():
        vt_ref[...] = _transpose_bf16(v_ref[...])

    qt = _transpose_bf16(q_ref[...])
    ci = crow_ref[0, 0]
    reps = tq // LANES

    def scores(k0):
        k = k_ref[pl.ds(k0, tq), :]
        cj = crep_ref[0, 0, pl.ds(k0, tq), :]
        st = jnp.dot(k, qt, preferred_element_type=F32)
        return st + ci - jnp.concatenate([cj] * reps, axis=1)

    k0 = pl.multiple_of(i * tq, tq)
    rk = lax.broadcasted_iota(jnp.int32, (tq, tq), 0)
    cq = lax.broadcasted_iota(jnp.int32, (tq, tq), 1)
    sm = jnp.where(rk <= cq, scores(k0), NEG)
    m = jnp.max(sm, axis=0, keepdims=True)
    p = jnp.exp(sm - m)
    l = jnp.sum(p, axis=0, keepdims=True)
    acc = jnp.dot(vt_ref[:, pl.ds(k0, tq)], p.astype(BF16), preferred_element_type=F32)

    def body(it, carry):
        m, l, acc = carry
        k0 = pl.multiple_of((i - 1 - it) * tq, tq)
        st = scores(k0)
        m_new = jnp.maximum(m, jnp.max(st, axis=0, keepdims=True))
        p = jnp.exp(st - m_new)
        alpha = jnp.exp(m - m_new)
        l = alpha * l + jnp.sum(p, axis=0, keepdims=True)
        acc = alpha * acc + jnp.dot(vt_ref[:, pl.ds(k0, tq)], p.astype(BF16),
                                    preferred_element_type=F32)
        return m_new, l, acc

    m, l, acc = lax.fori_loop(0, i, body, (m, l, acc))
    out = acc * (1.0 / jnp.maximum(l, 1e-30))
    o_ref[...] = out.T.astype(o_ref.dtype)


def _fox_attention(p1, crep, crow4, batch, seq, tq=512):
    nq = seq // tq
    h = FOX_HEADS
    return pl.pallas_call(
        functools.partial(_fox_kernel, tq=tq),
        grid=(batch, h, nq),
        in_specs=[pl.BlockSpec((tq, 128), lambda b, hh, i: (b * nq + i, hh)),
                  pl.BlockSpec((seq, 128), lambda b, hh, i: (b, h + hh)),
                  pl.BlockSpec((seq, 128), lambda b, hh, i: (b, 2 * h + hh)),
                  pl.BlockSpec((1, 1, seq, LANES), lambda b, hh, i: (b, hh, 0, 0)),
                  pl.BlockSpec((1, 1, 1, tq), lambda b, hh, i: (b, hh, 0, i))],
        out_specs=pl.BlockSpec((tq, 128), lambda b, hh, i: (b * nq + i, hh)),
        out_shape=jax.ShapeDtypeStruct((batch * seq, FOX_WIDTH), BF16),
        scratch_shapes=[pltpu.VMEM((FOX_HEAD_DIM, seq), BF16)],
        compiler_params=_cparams(("arbitrary", "arbitrary", "arbitrary")),
        name="fox_attention",
    )(p1, p1, p1, crep, crow4)


def _compress_one(z_refs, pe_ref, w1_ref, w2_ref, nblk):
    half = CMP_BLOCK // 2
    first = jnp.zeros((nblk, CMP_HIDDEN), F32)
    second = jnp.zeros((nblk, CMP_HIDDEN), F32)
    for p in range(half):
        rows = pl.ds(p, nblk, stride=CMP_STRIDE)
        zp = [z_ref[rows, :] for z_ref in z_refs]
        zp = zp[0] if len(zp) == 1 else jnp.concatenate(zp, axis=1)
        first += jnp.dot((zp + pe_ref[p:p + 1, :]).astype(BF16), w1_ref[p],
                         preferred_element_type=F32)
        second += jnp.dot((zp + pe_ref[half + p:half + p + 1, :]).astype(BF16),
                          w1_ref[half + p], preferred_element_type=F32)
    hid = first + pltpu.roll(second, nblk - 1, 0)
    act = (hid * jax.nn.sigmoid(hid)).astype(BF16)
    return jnp.dot(act, w2_ref[...], preferred_element_type=F32)


def _compress_kernel(zk0_ref, zk1_ref, zv_ref, pek_ref, w1k_ref, w2k_ref, gk_ref,
                     pev_ref, w1v_ref, w2v_ref, kc_ref, vc_ref, *, nblk):
    kc = _compress_one((zk0_ref, zk1_ref), pek_ref, w1k_ref, w2k_ref, nblk)
    ms = jnp.sum(kc * kc, axis=-1, keepdims=True) * (1.0 / NSA_QK_DIM)
    kc_ref[...] = (kc * lax.rsqrt(ms + RMS_EPS) * gk_ref[...]).astype(kc_ref.dtype)
    vc = _compress_one((zv_ref,), pev_ref, w1v_ref, w2v_ref, nblk)
    vc_ref[...] = vc.T.astype(vc_ref.dtype)


def _compress(p3, pek, w1k, w2k, gk, pev, w1v, w2v, batch, seq):
    g = NSA_KV_GROUPS
    nblk = seq // CMP_STRIDE
    full2 = lambda b, gg: (0, 0)
    full3 = lambda b, gg: (0, 0, 0)
    return pl.pallas_call(
        functools.partial(_compress_kernel, nblk=nblk),
        grid=(batch, g),
        in_specs=[pl.BlockSpec((seq, LANES), lambda b, gg: (b, 2 * gg)),
                  pl.BlockSpec((seq, LANES), lambda b, gg: (b, 2 * gg + 1)),
                  pl.BlockSpec((seq, NSA_V_DIM), lambda b, gg: (b, 4 + gg)),
                  pl.BlockSpec(pek.shape, full2), pl.BlockSpec(w1k.shape, full3),
                  pl.BlockSpec(w2k.shape, full2), pl.BlockSpec(gk.shape, full2),
                  pl.BlockSpec(pev.shape, full2), pl.BlockSpec(w1v.shape, full3),
                  pl.BlockSpec(w2v.shape, full2)],
        out_specs=[pl.BlockSpec((nblk, NSA_QK_PAD), lambda b, gg: (b * g + gg, 0)),
                   pl.BlockSpec((NSA_V_DIM, nblk), lambda b, gg: (b * g + gg, 0))],
        out_shape=[jax.ShapeDtypeStruct((batch * g * nblk, NSA_QK_PAD), BF16),
                   jax.ShapeDtypeStruct((batch * g * NSA_V_DIM, nblk), BF16)],
        compiler_params=_cparams(("arbitrary", "arbitrary")),
        name="nsa_compress",
    )(p3, p3, p3, pek, w1k, w2k, gk, pev, w1v, w2v)


def _nsa_kernel(q_ref, kc_ref, vct_ref, ks_ref, vs_ref, kw_ref, vw_ref, gl_ref,
                slope_ref, ovt_ref, ext_ref, o_ref,
                vst_ref, vwt_ref, selx_ref, glt_ref, *, tq, n_cmp, n_sel):
    g = pl.program_id(1)
    i = pl.program_id(2)
    t0 = i * tq
    hpg = NSA_HPG
    dv = NSA_V_DIM

    @pl.when(i == 0)
    def _():
        vst_ref[...] = _transpose_bf16(vs_ref[...])
        vwt_ref[...] = _transpose_bf16(vw_ref[...])

    qb = q_ref[...]
    q4t = jnp.concatenate(
        [_transpose_bf16(qb[:, hh * NSA_QK_PAD:(hh + 1) * NSA_QK_PAD]) for hh in range(hpg)],
        axis=1)
    slopes = [slope_ref[0, hh:hh + 1, :] for hh in range(hpg)]
    rk = lax.broadcasted_iota(jnp.int32, (LANES, tq), 0)
    cq = lax.broadcasted_iota(jnp.int32, (LANES, tq), 1)

    def tile(k_ref, vt_ref, k0, rows, mask_fn):
        k = k_ref[pl.ds(k0, rows), :]
        st = jnp.dot(k, q4t, preferred_element_type=F32)
        dist = (lax.broadcasted_iota(jnp.int32, (rows, tq), 1)
                - lax.broadcasted_iota(jnp.int32, (rows, tq), 0)) + (t0 - k0)
        mask = mask_fn(dist, k0, rows)
        distf = dist.astype(F32)
        bias = jnp.concatenate(
            [jnp.where(mask, -slopes[hh] * distf, NEG) for hh in range(hpg)], axis=1)
        return st + bias, vt_ref[:, pl.ds(k0, rows)]

    def mask_slc(dist, k0, rows):
        return jnp.where(dist >= 0, selx_ref[pl.ds(k0, rows), :], 0.0) > 0.5

    def mask_win(dist, k0, rows):
        return jnp.where(dist >= 0, dist, WINDOW) < WINDOW

    k0 = pl.multiple_of(jnp.maximum(t0 - WINDOW, 0), LANES)
    sm, vt = tile(kw_ref, vwt_ref, k0, WINDOW + tq, mask_win)
    p = jnp.exp(sm - jnp.max(sm, axis=0, keepdims=True))
    o_win = (jnp.dot(vt, p.astype(BF16), preferred_element_type=F32)
             * (1.0 / jnp.maximum(jnp.sum(p, axis=0, keepdims=True), 1e-30)))

    s_c = jnp.dot(kc_ref[...], q4t, preferred_element_type=F32)
    dist_c = (t0 + cq) - (CMP_STRIDE * rk + (CMP_BLOCK - 1))
    mask_c = jnp.where(rk < n_cmp, dist_c, -1) >= 0
    dist_cf = dist_c.astype(F32)
    probs = []
    p_sum = jnp.zeros((LANES, tq), F32)
    for hh in range(hpg):
        sh = s_c[:, hh * tq:(hh + 1) * tq] - slopes[hh] * dist_cf
        sm = jnp.where(mask_c, sh, NEG)
        m = jnp.max(sm, axis=0, keepdims=True)
        e = jnp.where(mask_c, jnp.exp(sm - m), 0.0)
        p = e * (1.0 / jnp.maximum(jnp.sum(e, axis=0, keepdims=True), 1e-30))
        probs.append(p)
        p_sum = p_sum + p
    o_cmp = jnp.dot(vct_ref[...], jnp.concatenate(probs, axis=1).astype(BF16),
                    preferred_element_type=F32)

    ph = p_sum.astype(BF16)
    plo = (p_sum - ph.astype(F32)).astype(BF16)
    ovt = ovt_ref[...]
    imp = (jnp.dot(ovt, ph, preferred_element_type=F32)
           + jnp.dot(ovt, plo, preferred_element_type=F32))[:n_sel, :]

    rj = lax.broadcasted_iota(jnp.int32, (n_sel, tq), 0)
    tcol = t0 + lax.broadcasted_iota(jnp.int32, (n_sel, tq), 1)
    back = (tcol >> (SEL_BLOCK.bit_length() - 1)) - rj
    elig = back >= 0
    forced = jnp.where(rj == 0, 0, jnp.where(elig, back, SEL_LOCAL)) < SEL_LOCAL
    score = jnp.where(elig, jnp.where(forced, FORCE_SCORE, imp), -1.0)
    rank = jnp.zeros((n_sel, tq), F32)
    for jp in range(n_sel):
        row = score[jp:jp + 1, :]
        later = jnp.where(rj > jp, 1.0, 0.0)
        rank = rank + jnp.where(row > score, 1.0, jnp.where(row == score, later, 0.0))
    sel = jnp.where(elig, jnp.where(rank < SEL_TOPK, 1.0, 0.0), 0.0)
    sel = jnp.concatenate([sel, jnp.zeros((LANES - n_sel, tq), F32)], axis=0).astype(BF16)
    selx_ref[...] = jnp.dot(ext_ref[...], sel, preferred_element_type=F32)

    n_full = t0 // SLC_TILE
    sm, vt = tile(ks_ref, vst_ref, pl.multiple_of(n_full * SLC_TILE, SLC_TILE), SLC_TILE,
                  mask_slc)
    m = jnp.max(sm, axis=0, keepdims=True)
    p = jnp.exp(sm - m)
    l = jnp.sum(p, axis=0, keepdims=True)
    acc = jnp.dot(vt, p.astype(BF16), preferred_element_type=F32)

    def body(it, carry):
        m, l, acc = carry
        k0 = pl.multiple_of((n_full - 1 - it) * SLC_TILE, SLC_TILE)
        sm, vt = tile(ks_ref, vst_ref, k0, SLC_TILE, mask_slc)
        m_new = jnp.maximum(m, jnp.max(sm, axis=0, keepdims=True))
        p = jnp.exp(sm - m_new)
        alpha = jnp.exp(m - m_new)
        l = alpha * l + jnp.sum(p, axis=0, keepdims=True)
        acc = alpha * acc + jnp.dot(vt, p.astype(BF16), preferred_element_type=F32)
        return m_new, l, acc

    m, l, acc = lax.fori_loop(0, n_full, body, (m, l, acc))
    o_slc = acc * (1.0 / jnp.maximum(l, 1e-30))

    glt_ref[...] = jax.nn.sigmoid(gl_ref[...]).T
    for hh in range(hpg):
        base = FOX_HEADS + (g * hpg + hh) * 3
        lanes = slice(hh * tq, (hh + 1) * tq)
        out = (glt_ref[pl.ds(base, 1), :] * o_cmp[:, lanes]
               + glt_ref[pl.ds(base + 1, 1), :] * o_slc[:, lanes]
               + glt_ref[pl.ds(base + 2, 1), :] * o_win[:, lanes])
        o_ref[:, hh * dv:(hh + 1) * dv] = out.T.astype(o_ref.dtype)


def _nsa_attention(pq, pk, pv, p3, kc, vct, slopes, ovt, ext, batch, seq):
    tq = LANES
    g = NSA_KV_GROUPS
    nq = seq // tq
    nblk = seq // CMP_STRIDE
    n_cmp = nblk - CMP_BLOCK // CMP_STRIDE + 1
    qw = NSA_HPG * NSA_QK_PAD
    return pl.pallas_call(
        functools.partial(_nsa_kernel, tq=tq, n_cmp=n_cmp, n_sel=seq // SEL_BLOCK),
        grid=(batch, g, nq),
        in_specs=[
            pl.BlockSpec((tq, qw), lambda b, gg, i: (b * nq + i, gg)),
            pl.BlockSpec((nblk, NSA_QK_PAD), lambda b, gg, i: (b * g + gg, 0)),
            pl.BlockSpec((NSA_V_DIM, nblk), lambda b, gg, i: (b * g + gg, 0)),
            pl.BlockSpec((seq, NSA_QK_PAD), lambda b, gg, i: (b, gg)),
            pl.BlockSpec((seq, NSA_V_DIM), lambda b, gg, i: (b, gg)),
            pl.BlockSpec((seq, NSA_QK_PAD), lambda b, gg, i: (b, g + gg)),
            pl.BlockSpec((seq, NSA_V_DIM), lambda b, gg, i: (b, g + gg)),
            pl.BlockSpec((tq, LANES), lambda b, gg, i: (b * nq + i, 6)),
            pl.BlockSpec((1, 8, LANES), lambda b, gg, i: (gg, 0, 0)),
            pl.BlockSpec(ovt.shape, lambda b, gg, i: (0, 0)),
            pl.BlockSpec(ext.shape, lambda b, gg, i: (0, 0)),
        ],
        out_specs=pl.BlockSpec((tq, NSA_HPG * NSA_V_DIM), lambda b, gg, i: (b * nq + i, gg)),
        out_shape=jax.ShapeDtypeStruct((batch * seq, NSA_WIDTH), BF16),
        scratch_shapes=[pltpu.VMEM((NSA_V_DIM, seq), BF16), pltpu.VMEM((NSA_V_DIM, seq), BF16),
                        pltpu.VMEM((seq, tq), F32), pltpu.VMEM((LANES, tq), F32)],
        compiler_params=_cparams(("arbitrary", "arbitrary", "arbitrary")),
        name="nsa_attention",
    )(pq, kc, vct, pk, pv, pk, pv, p3, slopes, ovt, ext)


def _pad_head_rows(wt, heads):
    k = wt.shape[1]
    wt = wt.reshape(heads, NSA_QK_DIM, k)
    wt = jnp.pad(wt, ((0, 0), (0, NSA_QK_PAD - NSA_QK_DIM), (0, 0)))
    return wt.reshape(heads * NSA_QK_PAD, k)


def _pad_gain(gain, scale=1.0):
    return jnp.pad(gain * scale, (0, NSA_QK_PAD - NSA_QK_DIM))


def _overlap_matrix(nc, ns):
    i = np.arange(nc)[:, None]
    j = np.arange(ns)[None, :]
    lo = np.maximum(i * CMP_STRIDE, j * SEL_BLOCK)
    hi = np.minimum(i * CMP_STRIDE + CMP_BLOCK, (j + 1) * SEL_BLOCK)
    return (np.maximum(hi - lo, 0) / CMP_STRIDE).astype(np.float32)


def kernel(x, norm_attn, w_in, fox_f_bias, fox_q_gain, fox_k_gain,
           nsa_q_gain, nsa_kc_gain, nsa_ks_gain, nsa_kw_gain,
           cmp_pe_k, cmp_w1_k, cmp_w2_k, cmp_pe_v, cmp_w1_v, cmp_w2_v,
           w_up_fox, w_up_nsa, w_out, norm_ffn, w_ffn_gate, w_ffn_up, w_ffn_down):
    batch, seq, d = x.shape
    m = batch * seq
    depth = w_in.shape[0]
    pts = [0] + [int(p) for p in np.cumsum(IN_SPLITS)]
    nblk = seq // CMP_STRIDE
    n_cmp = nblk - CMP_BLOCK // CMP_STRIDE + 1
    ns = seq // SEL_BLOCK

    slopes_h = jnp.exp2(-8.0 * jnp.arange(1, NSA_HEADS + 1, dtype=F32) / NSA_HEADS)
    slopes = jnp.broadcast_to(
        jnp.pad(slopes_h.reshape(NSA_KV_GROUPS, NSA_HPG), ((0, 0), (0, 8 - NSA_HPG)))[:, :, None],
        (NSA_KV_GROUPS, 8, LANES))
    ovt_np = np.zeros((LANES, nblk), np.float32)
    ovt_np[:ns, :n_cmp] = _overlap_matrix(n_cmp, ns).T
    ovt = jnp.asarray(ovt_np, BF16)
    ext_np = np.zeros((seq, LANES), np.float32)
    ext_np[np.arange(seq), np.arange(seq) // SEL_BLOCK] = 1.0
    ext = jnp.asarray(ext_np, BF16)

    w_in_t = jnp.swapaxes(w_in, 1, 2)

    xf = x.reshape(m, d)
    for l in range(depth):
        wt = w_in_t[l]
        seg = [wt[pts[s]:pts[s + 1]] for s in range(len(IN_SPLITS))]
        (_, _, _, wfl, wnq, wkc, wvc, wks, wvs, wkw, wvw, wng, wga, wgb) = seg

        gain1 = jnp.concatenate([jnp.tile(fox_q_gain[l] * FOX_HEAD_DIM ** -0.5, FOX_HEADS),
                                 jnp.tile(fox_k_gain[l], FOX_HEADS),
                                 jnp.ones((FOX_WIDTH,), F32)])
        flag1 = jnp.concatenate([jnp.ones((2 * FOX_WIDTH,), F32), jnp.zeros((FOX_WIDTH,), F32)])
        gain_q = jnp.tile(_pad_gain(nsa_q_gain[l], NSA_QK_DIM ** -0.5), NSA_HEADS)
        gain_k = jnp.concatenate([jnp.tile(_pad_gain(nsa_ks_gain[l]), NSA_KV_GROUPS),
                                  jnp.tile(_pad_gain(nsa_kw_gain[l]), NSA_KV_GROUPS)])
        w_k = jnp.concatenate([wks, wkw], axis=0)
        w_v = jnp.concatenate([wvs, wvw], axis=0)
        n_small = FOX_HEADS + 3 * NSA_HEADS
        w3 = jnp.concatenate([_pad_head_rows(wkc, NSA_KV_GROUPS), wvc, wfl, wng,
                              jnp.zeros((LANES - n_small, d), F32)], axis=0)

        xn = _rmsnorm(xf, norm_attn[l])
        p1 = _project(xn, wt, 3 * FOX_WIDTH // 512, 512, gain1, flag1, BF16, 1024,
                      "proj_fox", group=128, count=128)
        pq = _project(xn, wnq, 2, NSA_HPG * NSA_QK_DIM, gain_q, jnp.ones_like(gain_q), BF16,
                      1024, "proj_nsa_q", pad_from=NSA_QK_DIM, pad_to=NSA_QK_PAD,
                      group=NSA_QK_PAD, count=NSA_QK_DIM)
        pk = _project(xn, w_k, 1, 2 * KV_K, gain_k, jnp.ones_like(gain_k), BF16,
                      1024, "proj_nsa_k", pad_from=NSA_QK_DIM, pad_to=NSA_QK_PAD,
                      group=NSA_QK_PAD, count=NSA_QK_DIM)
        pv = _project(xn, w_v, 1, 2 * KV_V, None, None, BF16, 1024, "proj_nsa_v")
        p3 = _project(xn, w3, 1, w3.shape[0], None, None, F32, 1024, "proj_f32")

        bias_row = jnp.pad(fox_f_bias[l], (0, LANES - FOX_HEADS)).reshape(1, LANES)
        crep, crow = _decay(p3, bias_row, batch, seq)
        o_a = _fox_attention(p1, crep, crow.reshape(batch, FOX_HEADS, 1, seq), batch, seq)

        pad_d = NSA_QK_PAD - NSA_QK_DIM
        pek = jnp.pad(cmp_pe_k[l], ((0, 0), (0, pad_d)))
        w1k = jnp.pad(cmp_w1_k[l].reshape(CMP_BLOCK, NSA_QK_DIM, CMP_HIDDEN),
                      ((0, 0), (0, pad_d), (0, 0))).astype(BF16)
        w2k = jnp.pad(cmp_w2_k[l], ((0, 0), (0, pad_d))).astype(BF16)
        gk = _pad_gain(nsa_kc_gain[l]).reshape(1, NSA_QK_PAD)
        w1v = cmp_w1_v[l].reshape(CMP_BLOCK, NSA_V_DIM, CMP_HIDDEN).astype(BF16)
        w2v = cmp_w2_v[l].astype(BF16)
        kc, vct = _compress(p3, pek, w1k, w2k, gk, cmp_pe_v[l], w1v, w2v, batch, seq)
        o_b = _nsa_attention(pq, pk, pv, p3, kc, vct, slopes, ovt, ext, batch, seq)

        merged = _merge(xn, o_a, o_b, wga, wgb, w_up_fox[l], w_up_nsa[l])
        hres, hn = _out_proj_norm(merged, w_out[l], xf, norm_ffn[l])

        act = _swiglu(hn, w_ffn_gate[l], w_ffn_up[l])
        xf = _matmul_residual(act, w_ffn_down[l], hres, 512, 512, "ffn_down")
    return xf.reshape(batch, seq, d)
```

```python
import functools

import numpy as np
import jax
import jax.numpy as jnp
from jax import lax
from jax.experimental import pallas as pl
from jax.experimental.pallas import tpu as pltpu

F32 = jnp.float32
BF16 = jnp.bfloat16

D_MODEL = 2048
FOX_HEADS = 8
FOX_HEAD_DIM = 128
FOX_WIDTH = FOX_HEADS * FOX_HEAD_DIM
NSA_HEADS = 8
NSA_KV_GROUPS = 2
NSA_HPG = NSA_HEADS // NSA_KV_GROUPS
NSA_QK_DIM = 192
NSA_QK_PAD = 256
NSA_V_DIM = 128
NSA_WIDTH = NSA_HEADS * NSA_V_DIM
CMP_BLOCK = 32
CMP_STRIDE = 16
CMP_HIDDEN = 256
SEL_BLOCK = 64
SEL_TOPK = 16
SEL_LOCAL = 2
FORCE_SCORE = 1.0e4
WINDOW = 512
KV_K = NSA_KV_GROUPS * NSA_QK_DIM
KV_V = NSA_KV_GROUPS * NSA_V_DIM
D_FF = -(-(8 * D_MODEL) // (3 * 256)) * 256
RMS_EPS = 1e-6
IN_SPLITS = (FOX_WIDTH, FOX_WIDTH, FOX_WIDTH, FOX_HEADS,
             NSA_HEADS * NSA_QK_DIM, KV_K, KV_V, KV_K, KV_V, KV_K, KV_V,
             3 * NSA_HEADS, D_MODEL, D_MODEL)

LANES = 128
BF16_SUBLANES = 16
NEG = -1.0e30
LOG2E = 1.4426950408889634
SLC_TILE = 512
VMEM_LIMIT = 56 * 1024 * 1024


def _cparams(sem):
    return pltpu.CompilerParams(dimension_semantics=sem, vmem_limit_bytes=VMEM_LIMIT)


def _rms_kernel(x_ref, g_ref, o_ref):
    x = x_ref[...]
    ms = jnp.mean(x * x, axis=-1, keepdims=True)
    o_ref[...] = (x * lax.rsqrt(ms + RMS_EPS) * g_ref[...]).astype(o_ref.dtype)


def _rmsnorm(x, gain, tm=512):
    m, d = x.shape
    return pl.pallas_call(
        _rms_kernel,
        grid=(m // tm,),
        in_specs=[pl.BlockSpec((tm, d), lambda i: (i, 0)),
                  pl.BlockSpec((1, d), lambda i: (0, 0))],
        out_specs=pl.BlockSpec((tm, d), lambda i: (i, 0)),
        out_shape=jax.ShapeDtypeStruct((m, d), BF16),
        compiler_params=_cparams(("arbitrary",)),
        name="rmsnorm",
    )(x, gain.reshape(1, d))


STAGE_CHUNK = 512


def _stage_weight(w_ref, wb_ref, transposed=False, pad_from=0, pad_to=0):
    if not transposed:
        wb_ref[...] = w_ref[...].astype(BF16)
        return
    n_in, k = w_ref.shape
    for c in range(k // STAGE_CHUNK):
        cols = slice(c * STAGE_CHUNK, (c + 1) * STAGE_CHUNK)
        w = w_ref[:, cols]
        if pad_from != pad_to:
            zero = jnp.zeros((pad_to - pad_from, STAGE_CHUNK), F32)
            w = jnp.concatenate(
                [piece for h in range(n_in // pad_from)
                 for piece in (w[h * pad_from:(h + 1) * pad_from, :], zero)], axis=0)
        wb_ref[cols, :] = w.T.astype(BF16)


def _first_m_step():
    return pl.program_id(1) == 0


def _proj_kernel(a_ref, w_ref, gain_ref, flag_ref, o_ref, wb_ref, *, pad_from, pad_to,
                 group, count):
    @pl.when(_first_m_step())
    def _():
        _stage_weight(w_ref, wb_ref, True, pad_from, pad_to)

    y = jnp.dot(a_ref[...], wb_ref[...], preferred_element_type=F32)
    if group == 0:
        o_ref[...] = y.astype(o_ref.dtype)
        return
    for c in range(y.shape[1] // group):
        sl = slice(c * group, (c + 1) * group)
        yc = y[:, sl]
        ss = jnp.sum(yc * yc, axis=-1, keepdims=True)
        rs = lax.rsqrt(ss * (1.0 / count) + RMS_EPS)
        scale = jnp.where(flag_ref[:, sl] > 0.0, rs, 1.0)
        o_ref[:, sl] = (yc * scale * gain_ref[:, sl]).astype(o_ref.dtype)


def _project(a, wt, n_tiles, tn_in, gain, flag, out_dtype, tm, name,
             pad_from=0, pad_to=0, group=0, count=1):
    m, k = a.shape
    tn_out = tn_in if pad_from == pad_to else tn_in // pad_from * pad_to
    n_out = n_tiles * tn_out
    if gain is None:
        gain = jnp.ones((n_out,), F32)
        flag = jnp.zeros((n_out,), F32)
    return pl.pallas_call(
        functools.partial(_proj_kernel, pad_from=pad_from, pad_to=pad_to, group=group,
                          count=count),
        grid=(n_tiles, m // tm),
        in_specs=[pl.BlockSpec((tm, k), lambda j, i: (i, 0)),
                  pl.BlockSpec((tn_in, k), lambda j, i: (j, 0)),
                  pl.BlockSpec((1, tn_out), lambda j, i: (0, j)),
                  pl.BlockSpec((1, tn_out), lambda j, i: (0, j))],
        out_specs=pl.BlockSpec((tm, tn_out), lambda j, i: (i, j)),
        out_shape=jax.ShapeDtypeStruct((m, n_out), out_dtype),
        scratch_shapes=[pltpu.VMEM((k, tn_out), BF16)],
        compiler_params=_cparams(("arbitrary", "arbitrary")),
        name=name,
    )(a, wt, gain.reshape(1, n_out), flag.reshape(1, n_out))


def _out_norm_kernel(a_ref, w_ref, r_ref, g_ref, h_ref, hn_ref, wb_ref):
    @pl.when(pl.program_id(0) == 0)
    def _():
        _stage_weight(w_ref, wb_ref)

    h = r_ref[...] + jnp.dot(a_ref[...], wb_ref[...], preferred_element_type=F32)
    h_ref[...] = h
    ms = jnp.mean(h * h, axis=-1, keepdims=True)
    hn_ref[...] = (h * lax.rsqrt(ms + RMS_EPS) * g_ref[...]).astype(hn_ref.dtype)


def _out_proj_norm(a, w, res, gain, tm=512):
    m, k = a.shape
    n = w.shape[1]
    full = lambda i: (0, 0)
    row = lambda i: (i, 0)
    return pl.pallas_call(
        _out_norm_kernel,
        grid=(m // tm,),
        in_specs=[pl.BlockSpec((tm, k), row),
                  pl.BlockSpec((k, n), full, pipeline_mode=pl.Buffered(1)),
                  pl.BlockSpec((tm, n), row), pl.BlockSpec((1, n), full)],
        out_specs=[pl.BlockSpec((tm, n), row), pl.BlockSpec((tm, n), row)],
        out_shape=[jax.ShapeDtypeStruct((m, n), F32), jax.ShapeDtypeStruct((m, n), BF16)],
        scratch_shapes=[pltpu.VMEM((k, n), BF16)],
        compiler_params=_cparams(("arbitrary",)),
        name="out_proj_norm",
    )(a, w, res, gain.reshape(1, n))


def _mm_res_kernel(a_ref, w_ref, r_ref, o_ref, wb_ref):
    @pl.when(_first_m_step())
    def _():
        _stage_weight(w_ref, wb_ref)

    o_ref[...] = r_ref[...] + jnp.dot(a_ref[...], wb_ref[...], preferred_element_type=F32)


def _matmul_residual(a, w, res, tm, tn, name):
    m, k = a.shape
    n = w.shape[1]
    return pl.pallas_call(
        _mm_res_kernel,
        grid=(n // tn, m // tm),
        in_specs=[pl.BlockSpec((tm, k), lambda j, i: (i, 0)),
                  pl.BlockSpec((k, tn), lambda j, i: (0, j)),
                  pl.BlockSpec((tm, tn), lambda j, i: (i, j))],
        out_specs=pl.BlockSpec((tm, tn), lambda j, i: (i, j)),
        out_shape=jax.ShapeDtypeStruct((m, n), F32),
        scratch_shapes=[pltpu.VMEM((k, tn), BF16)],
        compiler_params=_cparams(("arbitrary", "arbitrary")),
        name=name,
    )(a, w, res)


def _merge_kernel(xn_ref, oa_ref, ob_ref, wga_ref, wgb_ref, wuf_ref, wun_ref, o_ref,
                  bga_ref, bgb_ref, buf_ref, bun_ref):
    @pl.when(_first_m_step())
    def _():
        _stage_weight(wga_ref, bga_ref, True)
        _stage_weight(wgb_ref, bgb_ref, True)
        _stage_weight(wuf_ref, buf_ref)
        _stage_weight(wun_ref, bun_ref)

    xn = xn_ref[...]
    ga = jax.nn.sigmoid(jnp.dot(xn, bga_ref[...], preferred_element_type=F32))
    ua = jnp.dot(oa_ref[...], buf_ref[...], preferred_element_type=F32)
    acc = ga * ua
    gb = jax.nn.sigmoid(jnp.dot(xn, bgb_ref[...], preferred_element_type=F32))
    ub = jnp.dot(ob_ref[...], bun_ref[...], preferred_element_type=F32)
    o_ref[...] = (acc + gb * ub).astype(o_ref.dtype)


def _merge(xn, oa, ob, wga_t, wgb_t, wuf, wun, tm=512, tn=512):
    m, d = xn.shape
    n = wga_t.shape[0]
    ka = oa.shape[1]
    kb = ob.shape[1]
    row = lambda j, i: (i, 0)
    col = lambda j, i: (0, j)
    return pl.pallas_call(
        _merge_kernel,
        grid=(n // tn, m // tm),
        in_specs=[pl.BlockSpec((tm, d), row), pl.BlockSpec((tm, ka), row),
                  pl.BlockSpec((tm, kb), row),
                  pl.BlockSpec((tn, d), lambda j, i: (j, 0)),
                  pl.BlockSpec((tn, d), lambda j, i: (j, 0)),
                  pl.BlockSpec((ka, tn), col), pl.BlockSpec((kb, tn), col)],
        out_specs=pl.BlockSpec((tm, tn), lambda j, i: (i, j)),
        out_shape=jax.ShapeDtypeStruct((m, n), BF16),
        scratch_shapes=[pltpu.VMEM((d, tn), BF16), pltpu.VMEM((d, tn), BF16),
                        pltpu.VMEM((ka, tn), BF16), pltpu.VMEM((kb, tn), BF16)],
        compiler_params=_cparams(("arbitrary", "arbitrary")),
        name="gated_merge",
    )(xn, oa, ob, wga_t, wgb_t, wuf, wun)


def _swiglu_kernel(a_ref, wg_ref, wu_ref, o_ref, bg_ref, bu_ref):
    @pl.when(_first_m_step())
    def _():
        _stage_weight(wg_ref, bg_ref)
        _stage_weight(wu_ref, bu_ref)

    a = a_ref[...]
    gt = jnp.dot(a, bg_ref[...], preferred_element_type=F32)
    up = jnp.dot(a, bu_ref[...], preferred_element_type=F32)
    o_ref[...] = (gt * jax.nn.sigmoid(gt) * up).astype(o_ref.dtype)


def _swiglu(a, wg, wu, tm=1024, tn=512):
    m, k = a.shape
    n = wg.shape[1]
    return pl.pallas_call(
        _swiglu_kernel,
        grid=(n // tn, m // tm),
        in_specs=[pl.BlockSpec((tm, k), lambda j, i: (i, 0)),
                  pl.BlockSpec((k, tn), lambda j, i: (0, j)),
                  pl.BlockSpec((k, tn), lambda j, i: (0, j))],
        out_specs=pl.BlockSpec((tm, tn), lambda j, i: (i, j)),
        out_shape=jax.ShapeDtypeStruct((m, n), BF16),
        scratch_shapes=[pltpu.VMEM((k, tn), BF16), pltpu.VMEM((k, tn), BF16)],
        compiler_params=_cparams(("arbitrary", "arbitrary")),
        name="swiglu_up",
    )(a, wg, wu)


def _split3(x):
    hi = x.astype(BF16)
    r1 = x - hi.astype(F32)
    mid = r1.astype(BF16)
    lo = (r1 - mid.astype(F32)).astype(BF16)
    return hi, mid, lo


def _decay_kernel(z_ref, b_ref, crep_ref, crow_ref, *, blk):
    t = z_ref.shape[0]
    r = lax.broadcasted_iota(jnp.int32, (blk, blk), 0)
    c = lax.broadcasted_iota(jnp.int32, (blk, blk), 1)
    tri = jnp.where(r >= c, 1.0, 0.0).astype(BF16)
    lane = lax.broadcasted_iota(jnp.int32, (blk, LANES), 1)
    carry = jnp.zeros((1, LANES), F32)
    for s in range(t // blk):
        rows = slice(s * blk, (s + 1) * blk)
        z = z_ref[rows, :] + b_ref[...]
        logf = (jnp.minimum(z, 0.0) - jnp.log1p(jnp.exp(-jnp.abs(z)))) * LOG2E
        hi, mid, lo = _split3(logf)
        cb = (jnp.dot(tri, hi, preferred_element_type=F32)
              + jnp.dot(tri, mid, preferred_element_type=F32)
              + jnp.dot(tri, lo, preferred_element_type=F32)) + carry
        carry = cb[blk - 1:blk, :]
        crow_ref[0, :, rows] = cb.T[:FOX_HEADS, :]
        for h in range(FOX_HEADS):
            col = jnp.sum(jnp.where(lane == h, cb, 0.0), axis=-1, keepdims=True)
            crep_ref[0, h, rows, :] = jnp.broadcast_to(col, (blk, LANES))


def _decay(p3, bias_row, batch, seq, blk=256):
    return pl.pallas_call(
        functools.partial(_decay_kernel, blk=blk),
        grid=(batch,),
        in_specs=[pl.BlockSpec((seq, LANES), lambda b: (b, 6)),
                  pl.BlockSpec((1, LANES), lambda b: (0, 0))],
        out_specs=[pl.BlockSpec((1, FOX_HEADS, seq, LANES), lambda b: (b, 0, 0, 0)),
                   pl.BlockSpec((1, FOX_HEADS, seq), lambda b: (b, 0, 0))],
        out_shape=[jax.ShapeDtypeStruct((batch, FOX_HEADS, seq, LANES), F32),
                   jax.ShapeDtypeStruct((batch, FOX_HEADS, seq), F32)],
        compiler_params=_cparams(("arbitrary",)),
        name="fox_decay_cumsum",
    )(p3, bias_row)


def _transpose_bf16(x):
    return x.astype(F32).T.astype(BF16)


def _stage_values_t(v_ref, vt_ref):
    d = v_ref.shape[1]
    vt_ref[0:d, :] = _transpose_bf16(v_ref[...])
    vt_ref[d:, :] = jnp.ones((vt_ref.shape[0] - d, vt_ref.shape[1]), BF16)


def _normalised(acc, d):
    return acc[:d, :] * (1.0 / jnp.maximum(acc[d:d + 1, :], 1e-30))


def _fox_kernel(q_ref, k_ref, v_ref, crep_ref, crow_ref, o_ref, vt_ref, *, tq, nq):
    i = pl.program_id(2)
    dh = FOX_HEAD_DIM

    @pl.when(i == 0)
    def _():
        _stage_values_t(v_ref, vt_ref)

    qt = _transpose_bf16(q_ref[...])
    ci = crow_ref[0, 0]
    reps = tq // LANES
    rk = lax.broadcasted_iota(jnp.int32, (tq, tq), 0)
    cq = lax.broadcasted_iota(jnp.int32, (tq, tq), 1)

    def scores(k0):
        cj = crep_ref[0, 0, k0:k0 + tq, :]
        st = jnp.dot(k_ref[k0:k0 + tq, :], qt, preferred_element_type=F32)
        return st + ci - jnp.concatenate([cj] * reps, axis=1)

    def variant(n):
        def run():
            starts = [t * tq for t in range(n, -1, -1)]
            tiles = [jnp.where(rk <= cq, scores(starts[0]), NEG)]
            tiles += [scores(k0) for k0 in starts[1:]]
            m = jnp.max(tiles[0], axis=0, keepdims=True)
            acc = jnp.dot(vt_ref[:, starts[0]:starts[0] + tq],
                          jnp.exp2(tiles[0] - m).astype(BF16), preferred_element_type=F32)
            for st, k0 in zip(tiles[1:], starts[1:]):
                m_new = jnp.maximum(m, jnp.max(st, axis=0, keepdims=True))
                acc = jnp.exp2(m - m_new) * acc + jnp.dot(
                    vt_ref[:, k0:k0 + tq], jnp.exp2(st - m_new).astype(BF16),
                    preferred_element_type=F32)
                m = m_new
            o_ref[...] = _normalised(acc, dh).T.astype(o_ref.dtype)
        return run

    lax.switch(i, [variant(n) for n in range(nq)])


def _fox_attention(p1, crep, crow4, batch, seq, tq=512):
    nq = seq // tq
    h = FOX_HEADS
    return pl.pallas_call(
        functools.partial(_fox_kernel, tq=tq, nq=nq),
        grid=(batch, h, nq),
        in_specs=[pl.BlockSpec((tq, 128), lambda b, hh, i: (b * nq + i, hh)),
                  pl.BlockSpec((seq, 128), lambda b, hh, i: (b, h + hh)),
                  pl.BlockSpec((seq, 128), lambda b, hh, i: (b, 2 * h + hh)),
                  pl.BlockSpec((1, 1, seq, LANES), lambda b, hh, i: (b, hh, 0, 0)),
                  pl.BlockSpec((1, 1, 1, tq), lambda b, hh, i: (b, hh, 0, i))],
        out_specs=pl.BlockSpec((tq, 128), lambda b, hh, i: (b * nq + i, hh)),
        out_shape=jax.ShapeDtypeStruct((batch * seq, FOX_WIDTH), BF16),
        scratch_shapes=[pltpu.VMEM((FOX_HEAD_DIM + BF16_SUBLANES, seq), BF16)],
        compiler_params=_cparams(("arbitrary", "arbitrary", "arbitrary")),
        name="fox_attention",
    )(p1, p1, p1, crep, crow4)


def _compress_one(z_refs, pe_ref, w1_ref, w2_ref, nblk):
    half = CMP_BLOCK // 2
    first = jnp.zeros((nblk, CMP_HIDDEN), F32)
    second = jnp.zeros((nblk, CMP_HIDDEN), F32)
    for p in range(half):
        rows = pl.ds(p, nblk, stride=CMP_STRIDE)
        zp = [z_ref[rows, :] for z_ref in z_refs]
        zp = zp[0] if len(zp) == 1 else jnp.concatenate(zp, axis=1)
        first += jnp.dot((zp + pe_ref[p:p + 1, :]).astype(BF16), w1_ref[p],
                         preferred_element_type=F32)
        second += jnp.dot((zp + pe_ref[half + p:half + p + 1, :]).astype(BF16),
                          w1_ref[half + p], preferred_element_type=F32)
    hid = first + pltpu.roll(second, nblk - 1, 0)
    act = (hid * jax.nn.sigmoid(hid)).astype(BF16)
    return jnp.dot(act, w2_ref[...], preferred_element_type=F32)


def _compress_kernel(zk0_ref, zk1_ref, zv_ref, pek_ref, w1k_ref, w2k_ref, gk_ref,
                     pev_ref, w1v_ref, w2v_ref, kc_ref, vc_ref, *, nblk):
    kc = _compress_one((zk0_ref, zk1_ref), pek_ref, w1k_ref, w2k_ref, nblk)
    ms = jnp.sum(kc * kc, axis=-1, keepdims=True) * (1.0 / NSA_QK_DIM)
    kc_ref[...] = (kc * lax.rsqrt(ms + RMS_EPS) * gk_ref[...]).astype(kc_ref.dtype)
    vc = _compress_one((zv_ref,), pev_ref, w1v_ref, w2v_ref, nblk)
    vc_ref[...] = vc.T.astype(vc_ref.dtype)


def _compress(p3, pek, w1k, w2k, gk, pev, w1v, w2v, batch, seq):
    g = NSA_KV_GROUPS
    nblk = seq // CMP_STRIDE
    full2 = lambda b, gg: (0, 0)
    full3 = lambda b, gg: (0, 0, 0)
    return pl.pallas_call(
        functools.partial(_compress_kernel, nblk=nblk),
        grid=(batch, g),
        in_specs=[pl.BlockSpec((seq, LANES), lambda b, gg: (b, 2 * gg)),
                  pl.BlockSpec((seq, LANES), lambda b, gg: (b, 2 * gg + 1)),
                  pl.BlockSpec((seq, NSA_V_DIM), lambda b, gg: (b, 4 + gg)),
                  pl.BlockSpec(pek.shape, full2), pl.BlockSpec(w1k.shape, full3),
                  pl.BlockSpec(w2k.shape, full2), pl.BlockSpec(gk.shape, full2),
                  pl.BlockSpec(pev.shape, full2), pl.BlockSpec(w1v.shape, full3),
                  pl.BlockSpec(w2v.shape, full2)],
        out_specs=[pl.BlockSpec((nblk, NSA_QK_PAD), lambda b, gg: (b * g + gg, 0)),
                   pl.BlockSpec((NSA_V_DIM, nblk), lambda b, gg: (b * g + gg, 0))],
        out_shape=[jax.ShapeDtypeStruct((batch * g * nblk, NSA_QK_PAD), BF16),
                   jax.ShapeDtypeStruct((batch * g * NSA_V_DIM, nblk), BF16)],
        compiler_params=_cparams(("arbitrary", "arbitrary")),
        name="nsa_compress",
    )(p3, p3, p3, pek, w1k, w2k, gk, pev, w1v, w2v)


def _q_heads_t(q_ref):
    qb = q_ref[...]
    return jnp.concatenate(
        [_transpose_bf16(qb[:, hh * NSA_QK_PAD:(hh + 1) * NSA_QK_PAD]) for hh in range(NSA_HPG)],
        axis=1)


def _gate_rows(glt_ref, g, hh):
    base = FOX_HEADS + (g * NSA_HPG + hh) * 3
    return [glt_ref[pl.ds(base + br, 1), :] for br in range(3)]


def _nsa_select_kernel(q_ref, kc_ref, vct_ref, gl_ref, slope_ref, ovt_ref,
                       sel_ref, ocmp_ref, glt_ref, *, tq, n_cmp, n_sel):
    g = pl.program_id(1)
    t0 = pl.program_id(2) * tq
    hpg = NSA_HPG
    dv = NSA_V_DIM
    q4t = _q_heads_t(q_ref)
    slopes = [slope_ref[0, hh:hh + 1, :] for hh in range(hpg)]
    nrep = tq // LANES
    rk = lax.broadcasted_iota(jnp.int32, (LANES, tq), 0)
    cq = lax.broadcasted_iota(jnp.int32, (LANES, tq), 1)

    s_c = jnp.dot(kc_ref[...], q4t, preferred_element_type=F32)
    dist_c = (t0 + cq) - (CMP_STRIDE * rk + (CMP_BLOCK - 1))
    mask_c = jnp.where(rk < n_cmp, dist_c, -1) >= 0
    dist_cf = dist_c.astype(F32)
    probs = []
    p_sum = jnp.zeros((LANES, tq), F32)
    for hh in range(hpg):
        slope = jnp.concatenate([slopes[hh]] * nrep, axis=1)
        sh = s_c[:, hh * tq:(hh + 1) * tq] - slope * dist_cf
        sm = jnp.where(mask_c, sh, NEG)
        m = jnp.max(sm, axis=0, keepdims=True)
        e = jnp.where(mask_c, jnp.exp2(sm - m), 0.0)
        p = e * (1.0 / jnp.maximum(jnp.sum(e, axis=0, keepdims=True), 1e-30))
        probs.append(p)
        p_sum = p_sum + p
    o_cmp = jnp.dot(vct_ref[...], jnp.concatenate(probs, axis=1).astype(BF16),
                    preferred_element_type=F32)

    ph = p_sum.astype(BF16)
    plo = (p_sum - ph.astype(F32)).astype(BF16)
    ovt = ovt_ref[...]
    imp = (jnp.dot(ovt, ph, preferred_element_type=F32)
           + jnp.dot(ovt, plo, preferred_element_type=F32))[:n_sel, :]

    rj = lax.broadcasted_iota(jnp.int32, (n_sel, tq), 0)
    tcol = t0 + lax.broadcasted_iota(jnp.int32, (n_sel, tq), 1)
    back = (tcol >> (SEL_BLOCK.bit_length() - 1)) - rj
    elig = back >= 0
    forced = jnp.where(rj == 0, 0, jnp.where(elig, back, SEL_LOCAL)) < SEL_LOCAL
    score = jnp.where(elig, jnp.where(forced, FORCE_SCORE, imp), -1.0)
    rank = jnp.zeros((n_sel, tq), F32)
    for jp in range(n_sel):
        row = score[jp:jp + 1, :]
        later = jnp.where(rj > jp, 1.0, 0.0)
        rank = rank + jnp.where(row > score, 1.0, jnp.where(row == score, later, 0.0))
    sel = jnp.where(elig, jnp.where(rank < SEL_TOPK, 1.0, 0.0), 0.0)
    sel_ref[...] = jnp.concatenate(
        [sel, jnp.zeros((LANES - n_sel, tq), F32)], axis=0).astype(sel_ref.dtype)

    glt_ref[...] = jax.nn.sigmoid(gl_ref[...]).T
    for hh in range(hpg):
        gate = _gate_rows(glt_ref, g, hh)[0]
        ocmp_ref[:, hh * dv:(hh + 1) * dv] = (gate * o_cmp[:, hh * tq:(hh + 1) * tq]).T


def _nsa_select(pq, p3, kc, vct, slopes, ovt, batch, seq, tq=512):
    g = NSA_KV_GROUPS
    nq = seq // tq
    nblk = seq // CMP_STRIDE
    n_cmp = nblk - CMP_BLOCK // CMP_STRIDE + 1
    return pl.pallas_call(
        functools.partial(_nsa_select_kernel, tq=tq, n_cmp=n_cmp, n_sel=seq // SEL_BLOCK),
        grid=(batch, g, nq),
        in_specs=[
            pl.BlockSpec((tq, NSA_HPG * NSA_QK_PAD), lambda b, gg, i: (b * nq + i, gg)),
            pl.BlockSpec((nblk, NSA_QK_PAD), lambda b, gg, i: (b * g + gg, 0)),
            pl.BlockSpec((NSA_V_DIM, nblk), lambda b, gg, i: (b * g + gg, 0)),
            pl.BlockSpec((tq, LANES), lambda b, gg, i: (b * nq + i, 6)),
            pl.BlockSpec((1, 8, LANES), lambda b, gg, i: (gg, 0, 0)),
            pl.BlockSpec(ovt.shape, lambda b, gg, i: (0, 0)),
        ],
        out_specs=[pl.BlockSpec((LANES, tq), lambda b, gg, i: (b * g + gg, i)),
                   pl.BlockSpec((tq, NSA_HPG * NSA_V_DIM), lambda b, gg, i: (b * nq + i, gg))],
        out_shape=[jax.ShapeDtypeStruct((batch * g * LANES, seq), BF16),
                   jax.ShapeDtypeStruct((batch * seq, NSA_WIDTH), F32)],
        scratch_shapes=[pltpu.VMEM((LANES, tq), F32)],
        compiler_params=_cparams(("arbitrary", "arbitrary", "arbitrary")),
        name="nsa_select",
    )(pq, kc, vct, p3, slopes, ovt)


def _nsa_attend_kernel(q_ref, ks_ref, vs_ref, kw_ref, vw_ref, gl_ref, slope_ref, sel_ref,
                       ext_ref, ocmp_ref, o_ref, vst_ref, vwt_ref, glt_ref, *, tq, n_var):
    g = pl.program_id(1)
    i = pl.program_id(2)
    t0 = i * tq
    hpg = NSA_HPG
    dv = NSA_V_DIM

    @pl.when(i == 0)
    def _():
        _stage_values_t(vs_ref, vst_ref)
        _stage_values_t(vw_ref, vwt_ref)

    q4t = _q_heads_t(q_ref)
    slopes = [slope_ref[0, hh:hh + 1, :] for hh in range(hpg)]
    sel = sel_ref[...]
    glt_ref[...] = jax.nn.sigmoid(gl_ref[...]).T

    def scores(k, k0, rows, mask_fn):
        st = jnp.dot(k, q4t, preferred_element_type=F32)
        dist = (lax.broadcasted_iota(jnp.int32, (rows, tq), 1)
                - lax.broadcasted_iota(jnp.int32, (rows, tq), 0)) + (t0 - k0)
        mask = mask_fn(dist)
        distf = dist.astype(F32)
        return st + jnp.concatenate(
            [jnp.where(mask, -slopes[hh] * distf, NEG) for hh in range(hpg)], axis=1)

    def variant(n):
        def run():
            kw0 = pl.multiple_of(jnp.maximum(t0 - WINDOW, 0), LANES)
            sw = scores(kw_ref[pl.ds(kw0, WINDOW + tq), :], kw0, WINDOW + tq,
                        lambda dist: jnp.where(dist >= 0, dist, WINDOW) < WINDOW)
            pw = jnp.exp2(sw - jnp.max(sw, axis=0, keepdims=True))
            o_win = _normalised(jnp.dot(vwt_ref[:, pl.ds(kw0, WINDOW + tq)], pw.astype(BF16),
                                        preferred_element_type=F32), dv)

            starts = [t * SLC_TILE for t in range(n, -1, -1)]
            tiles = []
            for k0 in starts:
                selx = jnp.dot(ext_ref[k0:k0 + SLC_TILE, :], sel,
                               preferred_element_type=F32)
                tiles.append(scores(
                    ks_ref[k0:k0 + SLC_TILE, :], k0, SLC_TILE,
                    lambda dist, selx=selx: jnp.where(dist >= 0, selx, 0.0) > 0.5))
            m = jnp.max(tiles[0], axis=0, keepdims=True)
            acc = jnp.dot(vst_ref[:, starts[0]:starts[0] + SLC_TILE],
                          jnp.exp2(tiles[0] - m).astype(BF16), preferred_element_type=F32)
            for st, k0 in zip(tiles[1:], starts[1:]):
                m_new = jnp.maximum(m, jnp.max(st, axis=0, keepdims=True))
                acc = jnp.exp2(m - m_new) * acc + jnp.dot(
                    vst_ref[:, k0:k0 + SLC_TILE], jnp.exp2(st - m_new).astype(BF16),
                    preferred_element_type=F32)
                m = m_new
            o_slc = _normalised(acc, dv)

            for hh in range(hpg):
                _, g_slc, g_win = _gate_rows(glt_ref, g, hh)
                lanes = slice(hh * tq, (hh + 1) * tq)
                out = g_slc * o_slc[:, lanes] + g_win * o_win[:, lanes]
                cols = slice(hh * dv, (hh + 1) * dv)
                o_ref[:, cols] = (ocmp_ref[:, cols] + out.T).astype(o_ref.dtype)
        return run

    lax.switch(t0 // SLC_TILE, [variant(n) for n in range(n_var)])


def _nsa_attend(pq, pk, pv, p3, sel, ocmp, slopes, ext, batch, seq):
    tq = LANES
    g = NSA_KV_GROUPS
    nq = seq // tq
    qw = NSA_HPG * NSA_QK_PAD
    ow = NSA_HPG * NSA_V_DIM
    vrows = NSA_V_DIM + BF16_SUBLANES
    return pl.pallas_call(
        functools.partial(_nsa_attend_kernel, tq=tq, n_var=seq // SLC_TILE),
        grid=(batch, g, nq),
        in_specs=[
            pl.BlockSpec((tq, qw), lambda b, gg, i: (b * nq + i, gg)),
            pl.BlockSpec((seq, NSA_QK_PAD), lambda b, gg, i: (b, gg)),
            pl.BlockSpec((seq, NSA_V_DIM), lambda b, gg, i: (b, gg)),
            pl.BlockSpec((seq, NSA_QK_PAD), lambda b, gg, i: (b, g + gg)),
            pl.BlockSpec((seq, NSA_V_DIM), lambda b, gg, i: (b, g + gg)),
            pl.BlockSpec((tq, LANES), lambda b, gg, i: (b * nq + i, 6)),
            pl.BlockSpec((1, 8, LANES), lambda b, gg, i: (gg, 0, 0)),
            pl.BlockSpec((LANES, tq), lambda b, gg, i: (b * g + gg, i)),
            pl.BlockSpec(ext.shape, lambda b, gg, i: (0, 0)),
            pl.BlockSpec((tq, ow), lambda b, gg, i: (b * nq + i, gg)),
        ],
        out_specs=pl.BlockSpec((tq, ow), lambda b, gg, i: (b * nq + i, gg)),
        out_shape=jax.ShapeDtypeStruct((batch * seq, NSA_WIDTH), BF16),
        scratch_shapes=[pltpu.VMEM((vrows, seq), BF16), pltpu.VMEM((vrows, seq), BF16),
                        pltpu.VMEM((LANES, tq), F32)],
        compiler_params=_cparams(("arbitrary", "arbitrary", "arbitrary")),
        name="nsa_attend",
    )(pq, pk, pv, pk, pv, p3, slopes, sel, ext, ocmp)


def _pad_head_rows(wt, heads):
    k = wt.shape[1]
    wt = wt.reshape(heads, NSA_QK_DIM, k)
    wt = jnp.pad(wt, ((0, 0), (0, NSA_QK_PAD - NSA_QK_DIM), (0, 0)))
    return wt.reshape(heads * NSA_QK_PAD, k)


def _pad_gain(gain, scale=1.0):
    return jnp.pad(gain * scale, (0, NSA_QK_PAD - NSA_QK_DIM))


def _overlap_matrix(nc, ns):
    i = np.arange(nc)[:, None]
    j = np.arange(ns)[None, :]
    lo = np.maximum(i * CMP_STRIDE, j * SEL_BLOCK)
    hi = np.minimum(i * CMP_STRIDE + CMP_BLOCK, (j + 1) * SEL_BLOCK)
    return (np.maximum(hi - lo, 0) / CMP_STRIDE).astype(np.float32)


def kernel(x, norm_attn, w_in, fox_f_bias, fox_q_gain, fox_k_gain,
           nsa_q_gain, nsa_kc_gain, nsa_ks_gain, nsa_kw_gain,
           cmp_pe_k, cmp_w1_k, cmp_w2_k, cmp_pe_v, cmp_w1_v, cmp_w2_v,
           w_up_fox, w_up_nsa, w_out, norm_ffn, w_ffn_gate, w_ffn_up, w_ffn_down):
    batch, seq, d = x.shape
    m = batch * seq
    depth = w_in.shape[0]
    pts = [0] + [int(p) for p in np.cumsum(IN_SPLITS)]
    nblk = seq // CMP_STRIDE
    n_cmp = nblk - CMP_BLOCK // CMP_STRIDE + 1
    ns = seq // SEL_BLOCK

    slopes_h = jnp.exp2(-8.0 * jnp.arange(1, NSA_HEADS + 1, dtype=F32) / NSA_HEADS) * LOG2E
    slopes = jnp.broadcast_to(
        jnp.pad(slopes_h.reshape(NSA_KV_GROUPS, NSA_HPG), ((0, 0), (0, 8 - NSA_HPG)))[:, :, None],
        (NSA_KV_GROUPS, 8, LANES))
    ovt_np = np.zeros((LANES, nblk), np.float32)
    ovt_np[:ns, :n_cmp] = _overlap_matrix(n_cmp, ns).T
    ovt = jnp.asarray(ovt_np, BF16)
    ext_np = np.zeros((seq, LANES), np.float32)
    ext_np[np.arange(seq), np.arange(seq) // SEL_BLOCK] = 1.0
    ext = jnp.asarray(ext_np, BF16)

    w_in_t = jnp.swapaxes(w_in, 1, 2)

    xf = x.reshape(m, d)
    for l in range(depth):
        wt = w_in_t[l]
        seg = [wt[pts[s]:pts[s + 1]] for s in range(len(IN_SPLITS))]
        (_, _, _, wfl, wnq, wkc, wvc, wks, wvs, wkw, wvw, wng, wga, wgb) = seg

        gain1 = jnp.concatenate([jnp.tile(fox_q_gain[l] * (FOX_HEAD_DIM ** -0.5 * LOG2E), FOX_HEADS),
                                 jnp.tile(fox_k_gain[l], FOX_HEADS),
                                 jnp.ones((FOX_WIDTH,), F32)])
        flag1 = jnp.concatenate([jnp.ones((2 * FOX_WIDTH,), F32), jnp.zeros((FOX_WIDTH,), F32)])
        gain_q = jnp.tile(_pad_gain(nsa_q_gain[l], NSA_QK_DIM ** -0.5 * LOG2E), NSA_HEADS)
        gain_k = jnp.concatenate([jnp.tile(_pad_gain(nsa_ks_gain[l]), NSA_KV_GROUPS),
                                  jnp.tile(_pad_gain(nsa_kw_gain[l]), NSA_KV_GROUPS)])
        w_k = jnp.concatenate([wks, wkw], axis=0)
        w_v = jnp.concatenate([wvs, wvw], axis=0)
        n_small = FOX_HEADS + 3 * NSA_HEADS
        w3 = jnp.concatenate([_pad_head_rows(wkc, NSA_KV_GROUPS), wvc, wfl, wng,
                              jnp.zeros((LANES - n_small, d), F32)], axis=0)

        xn = _rmsnorm(xf, norm_attn[l])
        p1 = _project(xn, wt, 3 * FOX_WIDTH // 512, 512, gain1, flag1, BF16, 1024,
                      "proj_fox", group=128, count=128)
        pq = _project(xn, wnq, 2, NSA_HPG * NSA_QK_DIM, gain_q, jnp.ones_like(gain_q), BF16,
                      1024, "proj_nsa_q", pad_from=NSA_QK_DIM, pad_to=NSA_QK_PAD,
                      group=NSA_QK_PAD, count=NSA_QK_DIM)
        pk = _project(xn, w_k, 1, 2 * KV_K, gain_k, jnp.ones_like(gain_k), BF16,
                      1024, "proj_nsa_k", pad_from=NSA_QK_DIM, pad_to=NSA_QK_PAD,
                      group=NSA_QK_PAD, count=NSA_QK_DIM)
        pv = _project(xn, w_v, 1, 2 * KV_V, None, None, BF16, 1024, "proj_nsa_v")
        p3 = _project(xn, w3, 1, w3.shape[0], None, None, F32, 1024, "proj_f32")

        bias_row = jnp.pad(fox_f_bias[l], (0, LANES - FOX_HEADS)).reshape(1, LANES)
        crep, crow = _decay(p3, bias_row, batch, seq)
        o_a = _fox_attention(p1, crep, crow.reshape(batch, FOX_HEADS, 1, seq), batch, seq)

        pad_d = NSA_QK_PAD - NSA_QK_DIM
        pek = jnp.pad(cmp_pe_k[l], ((0, 0), (0, pad_d)))
        w1k = jnp.pad(cmp_w1_k[l].reshape(CMP_BLOCK, NSA_QK_DIM, CMP_HIDDEN),
                      ((0, 0), (0, pad_d), (0, 0))).astype(BF16)
        w2k = jnp.pad(cmp_w2_k[l], ((0, 0), (0, pad_d))).astype(BF16)
        gk = _pad_gain(nsa_kc_gain[l]).reshape(1, NSA_QK_PAD)
        w1v = cmp_w1_v[l].reshape(CMP_BLOCK, NSA_V_DIM, CMP_HIDDEN).astype(BF16)
        w2v = cmp_w2_v[l].astype(BF16)
        kc, vct = _compress(p3, pek, w1k, w2k, gk, cmp_pe_v[l], w1v, w2v, batch, seq)
        sel, ocmp = _nsa_select(pq, p3, kc, vct, slopes, ovt, batch, seq)
        o_b = _nsa_attend(pq, pk, pv, p3, sel, ocmp, slopes, ext, batch, seq)

        merged = _merge(xn, o_a, o_b, wga, wgb, w_up_fox[l], w_up_nsa[l])
        hres, hn = _out_proj_norm(merged, w_out[l], xf, norm_ffn[l])

        act = _swiglu(hn, w_ffn_gate[l], w_ffn_up[l])
        xf = _matmul_residual(act, w_ffn_down[l], hres, 512, 512, "ffn_down")
    return xf.reshape(batch, seq, d)
```

```python
import functools

import numpy as np
import jax
import jax.numpy as jnp
from jax import lax
from jax.experimental import pallas as pl
from jax.experimental.pallas import tpu as pltpu

F32 = jnp.float32
BF16 = jnp.bfloat16

D_MODEL = 2048
FOX_HEADS = 8
FOX_HEAD_DIM = 128
FOX_WIDTH = FOX_HEADS * FOX_HEAD_DIM
NSA_HEADS = 8
NSA_KV_GROUPS = 2
NSA_HPG = NSA_HEADS // NSA_KV_GROUPS
NSA_QK_DIM = 192
NSA_QK_PAD = 256
NSA_V_DIM = 128
NSA_WIDTH = NSA_HEADS * NSA_V_DIM
CMP_BLOCK = 32
CMP_STRIDE = 16
CMP_HIDDEN = 256
SEL_BLOCK = 64
SEL_TOPK = 16
SEL_LOCAL = 2
FORCE_SCORE = 1.0e4
WINDOW = 512
KV_K = NSA_KV_GROUPS * NSA_QK_DIM
KV_V = NSA_KV_GROUPS * NSA_V_DIM
D_FF = -(-(8 * D_MODEL) // (3 * 256)) * 256
RMS_EPS = 1e-6
IN_SPLITS = (FOX_WIDTH, FOX_WIDTH, FOX_WIDTH, FOX_HEADS,
             NSA_HEADS * NSA_QK_DIM, KV_K, KV_V, KV_K, KV_V, KV_K, KV_V,
             3 * NSA_HEADS, D_MODEL, D_MODEL)

LANES = 128
SUBLANES = 8
BF16_SUBLANES = 16
NEG = -1.0e30
LOG2E = 1.4426950408889634
SLC_TILE = 512
VMEM_LIMIT = 56 * 1024 * 1024


def _cparams(sem):
    return pltpu.CompilerParams(dimension_semantics=sem, vmem_limit_bytes=VMEM_LIMIT)


def _rms_kernel(x_ref, g_ref, o_ref):
    x = x_ref[...]
    ms = jnp.mean(x * x, axis=-1, keepdims=True)
    o_ref[...] = (x * lax.rsqrt(ms + RMS_EPS) * g_ref[...]).astype(o_ref.dtype)


def _rmsnorm(x, gain, tm=512):
    m, d = x.shape
    return pl.pallas_call(
        _rms_kernel,
        grid=(m // tm,),
        in_specs=[pl.BlockSpec((tm, d), lambda i: (i, 0)),
                  pl.BlockSpec((1, d), lambda i: (0, 0))],
        out_specs=pl.BlockSpec((tm, d), lambda i: (i, 0)),
        out_shape=jax.ShapeDtypeStruct((m, d), BF16),
        compiler_params=_cparams(("arbitrary",)),
        name="rmsnorm",
    )(x, gain.reshape(1, d))


STAGE_CHUNK = 512


def _stage_weight(w_ref, wb_ref, transposed=False, pad_from=0, pad_to=0):
    if not transposed:
        wb_ref[...] = w_ref[...].astype(BF16)
        return
    n_in, k = w_ref.shape
    for c in range(k // STAGE_CHUNK):
        cols = slice(c * STAGE_CHUNK, (c + 1) * STAGE_CHUNK)
        w = w_ref[:, cols]
        if pad_from != pad_to:
            zero = jnp.zeros((pad_to - pad_from, STAGE_CHUNK), F32)
            w = jnp.concatenate(
                [piece for h in range(n_in // pad_from)
                 for piece in (w[h * pad_from:(h + 1) * pad_from, :], zero)], axis=0)
        wb_ref[cols, :] = w.T.astype(BF16)


def _first_m_step():
    return pl.program_id(1) == 0


def _proj_kernel(a_ref, w_ref, gain_ref, flag_ref, o_ref, wb_ref, *, pad_from, pad_to,
                 group, count):
    @pl.when(_first_m_step())
    def _():
        _stage_weight(w_ref, wb_ref, True, pad_from, pad_to)

    y = jnp.dot(a_ref[...], wb_ref[...], preferred_element_type=F32)
    if group == 0:
        o_ref[...] = y.astype(o_ref.dtype)
        return
    for c in range(y.shape[1] // group):
        sl = slice(c * group, (c + 1) * group)
        yc = y[:, sl]
        ss = jnp.sum(yc * yc, axis=-1, keepdims=True)
        rs = lax.rsqrt(ss * (1.0 / count) + RMS_EPS)
        scale = jnp.where(flag_ref[:, sl] > 0.0, rs, 1.0)
        o_ref[:, sl] = (yc * scale * gain_ref[:, sl]).astype(o_ref.dtype)


def _row_window(starts, rows, k):
    def index(j, i):
        start = starts[-1]
        for t in range(len(starts) - 2, -1, -1):
            start = jnp.where(j == t, starts[t], start)
        return pl.multiple_of(start, SUBLANES), 0
    assert all(s % SUBLANES == 0 for s in starts)
    return pl.BlockSpec((pl.Element(rows), pl.Element(k)), index)


def _project(a, wt, starts, tn_in, gain, flag, out_dtype, tm, name,
             pad_from=0, pad_to=0, group=0, count=1):
    m, k = a.shape
    n_tiles = len(starts)
    tn_out = tn_in if pad_from == pad_to else tn_in // pad_from * pad_to
    n_out = n_tiles * tn_out
    if gain is None:
        gain = jnp.ones((n_out,), F32)
        flag = jnp.zeros((n_out,), F32)
    return pl.pallas_call(
        functools.partial(_proj_kernel, pad_from=pad_from, pad_to=pad_to, group=group,
                          count=count),
        grid=(n_tiles, m // tm),
        in_specs=[pl.BlockSpec((tm, k), lambda j, i: (i, 0)),
                  _row_window(starts, tn_in, k),
                  pl.BlockSpec((1, tn_out), lambda j, i: (0, j)),
                  pl.BlockSpec((1, tn_out), lambda j, i: (0, j))],
        out_specs=pl.BlockSpec((tm, tn_out), lambda j, i: (i, j)),
        out_shape=jax.ShapeDtypeStruct((m, n_out), out_dtype),
        scratch_shapes=[pltpu.VMEM((k, tn_out), BF16)],
        compiler_params=_cparams(("arbitrary", "arbitrary")),
        name=name,
    )(a, wt, gain.reshape(1, n_out), flag.reshape(1, n_out))


def _out_norm_kernel(a_ref, w_ref, r_ref, g_ref, h_ref, hn_ref, wb_ref):
    @pl.when(pl.program_id(0) == 0)
    def _():
        _stage_weight(w_ref, wb_ref)

    h = r_ref[...] + jnp.dot(a_ref[...], wb_ref[...], preferred_element_type=F32)
    h_ref[...] = h
    ms = jnp.mean(h * h, axis=-1, keepdims=True)
    hn_ref[...] = (h * lax.rsqrt(ms + RMS_EPS) * g_ref[...]).astype(hn_ref.dtype)


def _out_proj_norm(a, w, res, gain, tm=512):
    m, k = a.shape
    n = w.shape[1]
    full = lambda i: (0, 0)
    row = lambda i: (i, 0)
    return pl.pallas_call(
        _out_norm_kernel,
        grid=(m // tm,),
        in_specs=[pl.BlockSpec((tm, k), row),
                  pl.BlockSpec((k, n), full, pipeline_mode=pl.Buffered(1)),
                  pl.BlockSpec((tm, n), row), pl.BlockSpec((1, n), full)],
        out_specs=[pl.BlockSpec((tm, n), row), pl.BlockSpec((tm, n), row)],
        out_shape=[jax.ShapeDtypeStruct((m, n), F32), jax.ShapeDtypeStruct((m, n), BF16)],
        scratch_shapes=[pltpu.VMEM((k, n), BF16)],
        compiler_params=_cparams(("arbitrary",)),
        name="out_proj_norm",
    )(a, w, res, gain.reshape(1, n))


def _mm_res_kernel(a_ref, w_ref, r_ref, o_ref, wb_ref):
    @pl.when(_first_m_step())
    def _():
        _stage_weight(w_ref, wb_ref)

    o_ref[...] = r_ref[...] + jnp.dot(a_ref[...], wb_ref[...], preferred_element_type=F32)


def _matmul_residual(a, w, res, tm, tn, name):
    m, k = a.shape
    n = w.shape[1]
    return pl.pallas_call(
        _mm_res_kernel,
        grid=(n // tn, m // tm),
        in_specs=[pl.BlockSpec((tm, k), lambda j, i: (i, 0)),
                  pl.BlockSpec((k, tn), lambda j, i: (0, j)),
                  pl.BlockSpec((tm, tn), lambda j, i: (i, j))],
        out_specs=pl.BlockSpec((tm, tn), lambda j, i: (i, j)),
        out_shape=jax.ShapeDtypeStruct((m, n), F32),
        scratch_shapes=[pltpu.VMEM((k, tn), BF16)],
        compiler_params=_cparams(("arbitrary", "arbitrary")),
        name=name,
    )(a, w, res)


def _merge_kernel(xn_ref, oa_ref, ob_ref, wga_ref, wgb_ref, wuf_ref, wun_ref, o_ref,
                  bga_ref, bgb_ref, buf_ref, bun_ref):
    @pl.when(_first_m_step())
    def _():
        _stage_weight(wga_ref, bga_ref, True)
        _stage_weight(wgb_ref, bgb_ref, True)
        _stage_weight(wuf_ref, buf_ref)
        _stage_weight(wun_ref, bun_ref)

    xn = xn_ref[...]
    ga = jax.nn.sigmoid(jnp.dot(xn, bga_ref[...], preferred_element_type=F32))
    ua = jnp.dot(oa_ref[...], buf_ref[...], preferred_element_type=F32)
    acc = ga * ua
    gb = jax.nn.sigmoid(jnp.dot(xn, bgb_ref[...], preferred_element_type=F32))
    ub = jnp.dot(ob_ref[...], bun_ref[...], preferred_element_type=F32)
    o_ref[...] = (acc + gb * ub).astype(o_ref.dtype)


def _merge(xn, oa, ob, wt, row_a, row_b, wuf, wun, tm=512, tn=512):
    m, d = xn.shape
    n = wuf.shape[1]
    ka = oa.shape[1]
    kb = ob.shape[1]
    row = lambda j, i: (i, 0)
    col = lambda j, i: (0, j)
    return pl.pallas_call(
        _merge_kernel,
        grid=(n // tn, m // tm),
        in_specs=[pl.BlockSpec((tm, d), row), pl.BlockSpec((tm, ka), row),
                  pl.BlockSpec((tm, kb), row),
                  _row_window([row_a + t * tn for t in range(n // tn)], tn, d),
                  _row_window([row_b + t * tn for t in range(n // tn)], tn, d),
                  pl.BlockSpec((ka, tn), col), pl.BlockSpec((kb, tn), col)],
        out_specs=pl.BlockSpec((tm, tn), lambda j, i: (i, j)),
        out_shape=jax.ShapeDtypeStruct((m, n), BF16),
        scratch_shapes=[pltpu.VMEM((d, tn), BF16), pltpu.VMEM((d, tn), BF16),
                        pltpu.VMEM((ka, tn), BF16), pltpu.VMEM((kb, tn), BF16)],
        compiler_params=_cparams(("arbitrary", "arbitrary")),
        name="gated_merge",
    )(xn, oa, ob, wt, wt, wuf, wun)


def _swiglu_kernel(a_ref, wg_ref, wu_ref, o_ref, bg_ref, bu_ref):
    @pl.when(_first_m_step())
    def _():
        _stage_weight(wg_ref, bg_ref)
        _stage_weight(wu_ref, bu_ref)

    a = a_ref[...]
    gt = jnp.dot(a, bg_ref[...], preferred_element_type=F32)
    up = jnp.dot(a, bu_ref[...], preferred_element_type=F32)
    o_ref[...] = (gt * jax.nn.sigmoid(gt) * up).astype(o_ref.dtype)


def _swiglu(a, wg, wu, tm=1024, tn=512):
    m, k = a.shape
    n = wg.shape[1]
    return pl.pallas_call(
        _swiglu_kernel,
        grid=(n // tn, m // tm),
        in_specs=[pl.BlockSpec((tm, k), lambda j, i: (i, 0)),
                  pl.BlockSpec((k, tn), lambda j, i: (0, j)),
                  pl.BlockSpec((k, tn), lambda j, i: (0, j))],
        out_specs=pl.BlockSpec((tm, tn), lambda j, i: (i, j)),
        out_shape=jax.ShapeDtypeStruct((m, n), BF16),
        scratch_shapes=[pltpu.VMEM((k, tn), BF16), pltpu.VMEM((k, tn), BF16)],
        compiler_params=_cparams(("arbitrary", "arbitrary")),
        name="swiglu_up",
    )(a, wg, wu)


def _split3(x):
    hi = x.astype(BF16)
    r1 = x - hi.astype(F32)
    mid = r1.astype(BF16)
    lo = (r1 - mid.astype(F32)).astype(BF16)
    return hi, mid, lo


def _decay_kernel(z_ref, b_ref, crep_ref, crow_ref, *, blk):
    t = z_ref.shape[0]
    r = lax.broadcasted_iota(jnp.int32, (blk, blk), 0)
    c = lax.broadcasted_iota(jnp.int32, (blk, blk), 1)
    tri = jnp.where(r >= c, 1.0, 0.0).astype(BF16)
    lane = lax.broadcasted_iota(jnp.int32, (blk, LANES), 1)
    carry = jnp.zeros((1, LANES), F32)
    for s in range(t // blk):
        rows = slice(s * blk, (s + 1) * blk)
        z = z_ref[rows, :] + b_ref[...]
        logf = (jnp.minimum(z, 0.0) - jnp.log1p(jnp.exp(-jnp.abs(z)))) * LOG2E
        hi, mid, lo = _split3(logf)
        cb = (jnp.dot(tri, hi, preferred_element_type=F32)
              + jnp.dot(tri, mid, preferred_element_type=F32)
              + jnp.dot(tri, lo, preferred_element_type=F32)) + carry
        carry = cb[blk - 1:blk, :]
        crow_ref[0, :, rows] = cb.T[:FOX_HEADS, :]
        for h in range(FOX_HEADS):
            col = jnp.sum(jnp.where(lane == h, cb, 0.0), axis=-1, keepdims=True)
            crep_ref[0, h, rows, :] = jnp.broadcast_to(col, (blk, LANES))


def _decay(p3, bias_row, batch, seq, blk=256):
    return pl.pallas_call(
        functools.partial(_decay_kernel, blk=blk),
        grid=(batch,),
        in_specs=[pl.BlockSpec((seq, LANES), lambda b: (b, 6)),
                  pl.BlockSpec((1, LANES), lambda b: (0, 0))],
        out_specs=[pl.BlockSpec((1, FOX_HEADS, seq, LANES), lambda b: (b, 0, 0, 0)),
                   pl.BlockSpec((1, FOX_HEADS, seq), lambda b: (b, 0, 0))],
        out_shape=[jax.ShapeDtypeStruct((batch, FOX_HEADS, seq, LANES), F32),
                   jax.ShapeDtypeStruct((batch, FOX_HEADS, seq), F32)],
        compiler_params=_cparams(("arbitrary",)),
        name="fox_decay_cumsum",
    )(p3, bias_row)


def _transpose_bf16(x):
    return x.astype(F32).T.astype(BF16)


def _stage_values_t(v_ref, vt_ref):
    d = v_ref.shape[1]
    vt_ref[0:d, :] = _transpose_bf16(v_ref[...])
    vt_ref[d:, :] = jnp.ones((vt_ref.shape[0] - d, vt_ref.shape[1]), BF16)


def _normalised(acc, d):
    return acc[:d, :] * (1.0 / jnp.maximum(acc[d:d + 1, :], 1e-30))


def _fox_kernel(q_ref, k_ref, v_ref, crep_ref, crow_ref, o_ref, vt_ref, *, tq, nq):
    i = pl.program_id(2)
    dh = FOX_HEAD_DIM

    @pl.when(i == 0)
    def _():
        _stage_values_t(v_ref, vt_ref)

    qt = _transpose_bf16(q_ref[...])
    ci = crow_ref[0, 0]
    reps = tq // LANES
    rk = lax.broadcasted_iota(jnp.int32, (tq, tq), 0)
    cq = lax.broadcasted_iota(jnp.int32, (tq, tq), 1)

    def scores(k0):
        cj = crep_ref[0, 0, k0:k0 + tq, :]
        st = jnp.dot(k_ref[k0:k0 + tq, :], qt, preferred_element_type=F32)
        return st + ci - jnp.concatenate([cj] * reps, axis=1)

    def variant(n):
        def run():
            starts = [t * tq for t in range(n, -1, -1)]
            tiles = [jnp.where(rk <= cq, scores(starts[0]), NEG)]
            tiles += [scores(k0) for k0 in starts[1:]]
            m = jnp.max(tiles[0], axis=0, keepdims=True)
            acc = jnp.dot(vt_ref[:, starts[0]:starts[0] + tq],
                          jnp.exp2(tiles[0] - m).astype(BF16), preferred_element_type=F32)
            for st, k0 in zip(tiles[1:], starts[1:]):
                m_new = jnp.maximum(m, jnp.max(st, axis=0, keepdims=True))
                acc = jnp.exp2(m - m_new) * acc + jnp.dot(
                    vt_ref[:, k0:k0 + tq], jnp.exp2(st - m_new).astype(BF16),
                    preferred_element_type=F32)
                m = m_new
            o_ref[...] = _normalised(acc, dh).T.astype(o_ref.dtype)
        return run

    lax.switch(i, [variant(n) for n in range(nq)])


def _fox_attention(p1, crep, crow4, batch, seq, tq=512):
    nq = seq // tq
    h = FOX_HEADS
    return pl.pallas_call(
        functools.partial(_fox_kernel, tq=tq, nq=nq),
        grid=(batch, h, nq),
        in_specs=[pl.BlockSpec((tq, 128), lambda b, hh, i: (b * nq + i, hh)),
                  pl.BlockSpec((seq, 128), lambda b, hh, i: (b, h + hh)),
                  pl.BlockSpec((seq, 128), lambda b, hh, i: (b, 2 * h + hh)),
                  pl.BlockSpec((1, 1, seq, LANES), lambda b, hh, i: (b, hh, 0, 0)),
                  pl.BlockSpec((1, 1, 1, tq), lambda b, hh, i: (b, hh, 0, i))],
        out_specs=pl.BlockSpec((tq, 128), lambda b, hh, i: (b * nq + i, hh)),
        out_shape=jax.ShapeDtypeStruct((batch * seq, FOX_WIDTH), BF16),
        scratch_shapes=[pltpu.VMEM((FOX_HEAD_DIM + BF16_SUBLANES, seq), BF16)],
        compiler_params=_cparams(("arbitrary", "arbitrary", "arbitrary")),
        name="fox_attention",
    )(p1, p1, p1, crep, crow4)


def _compress_one(z_refs, pe_ref, w1_ref, w2_ref, nblk):
    half = CMP_BLOCK // 2
    first = jnp.zeros((nblk, CMP_HIDDEN), F32)
    second = jnp.zeros((nblk, CMP_HIDDEN), F32)
    for p in range(half):
        rows = pl.ds(p, nblk, stride=CMP_STRIDE)
        zp = [z_ref[rows, :] for z_ref in z_refs]
        zp = zp[0] if len(zp) == 1 else jnp.concatenate(zp, axis=1)
        first += jnp.dot((zp + pe_ref[p:p + 1, :]).astype(BF16), w1_ref[p],
                         preferred_element_type=F32)
        second += jnp.dot((zp + pe_ref[half + p:half + p + 1, :]).astype(BF16),
                          w1_ref[half + p], preferred_element_type=F32)
    hid = first + pltpu.roll(second, nblk - 1, 0)
    act = (hid * jax.nn.sigmoid(hid)).astype(BF16)
    return jnp.dot(act, w2_ref[...], preferred_element_type=F32)


def _compress_kernel(zk0_ref, zk1_ref, zv_ref, pek_ref, w1k_ref, w2k_ref, gk_ref,
                     pev_ref, w1v_ref, w2v_ref, kc_ref, vc_ref, *, nblk):
    kc = _compress_one((zk0_ref, zk1_ref), pek_ref, w1k_ref, w2k_ref, nblk)
    ms = jnp.sum(kc * kc, axis=-1, keepdims=True) * (1.0 / NSA_QK_DIM)
    kc_ref[...] = (kc * lax.rsqrt(ms + RMS_EPS) * gk_ref[...]).astype(kc_ref.dtype)
    vc = _compress_one((zv_ref,), pev_ref, w1v_ref, w2v_ref, nblk)
    vc_ref[...] = vc.T.astype(vc_ref.dtype)


def _compress(p3, pek, w1k, w2k, gk, pev, w1v, w2v, batch, seq):
    g = NSA_KV_GROUPS
    nblk = seq // CMP_STRIDE
    full2 = lambda b, gg: (0, 0)
    full3 = lambda b, gg: (0, 0, 0)
    return pl.pallas_call(
        functools.partial(_compress_kernel, nblk=nblk),
        grid=(batch, g),
        in_specs=[pl.BlockSpec((seq, LANES), lambda b, gg: (b, 2 * gg)),
                  pl.BlockSpec((seq, LANES), lambda b, gg: (b, 2 * gg + 1)),
                  pl.BlockSpec((seq, NSA_V_DIM), lambda b, gg: (b, 4 + gg)),
                  pl.BlockSpec(pek.shape, full2), pl.BlockSpec(w1k.shape, full3),
                  pl.BlockSpec(w2k.shape, full2), pl.BlockSpec(gk.shape, full2),
                  pl.BlockSpec(pev.shape, full2), pl.BlockSpec(w1v.shape, full3),
                  pl.BlockSpec(w2v.shape, full2)],
        out_specs=[pl.BlockSpec((nblk, NSA_QK_PAD), lambda b, gg: (b * g + gg, 0)),
                   pl.BlockSpec((NSA_V_DIM, nblk), lambda b, gg: (b * g + gg, 0))],
        out_shape=[jax.ShapeDtypeStruct((batch * g * nblk, NSA_QK_PAD), BF16),
                   jax.ShapeDtypeStruct((batch * g * NSA_V_DIM, nblk), BF16)],
        compiler_params=_cparams(("arbitrary", "arbitrary")),
        name="nsa_compress",
    )(p3, p3, p3, pek, w1k, w2k, gk, pev, w1v, w2v)


def _q_heads_t(q_ref):
    qb = q_ref[...]
    return jnp.concatenate(
        [_transpose_bf16(qb[:, hh * NSA_QK_PAD:(hh + 1) * NSA_QK_PAD]) for hh in range(NSA_HPG)],
        axis=1)


def _gate_rows(glt_ref, g, hh):
    base = FOX_HEADS + (g * NSA_HPG + hh) * 3
    return [glt_ref[pl.ds(base + br, 1), :] for br in range(3)]


def _nsa_select_kernel(q_ref, kc_ref, vct_ref, gl_ref, slope_ref, ovt_ref,
                       sel_ref, ocmp_ref, glt_ref, *, tq, n_cmp, n_sel):
    g = pl.program_id(1)
    t0 = pl.program_id(2) * tq
    hpg = NSA_HPG
    dv = NSA_V_DIM
    q4t = _q_heads_t(q_ref)
    slopes = [slope_ref[0, hh:hh + 1, :] for hh in range(hpg)]
    nrep = tq // LANES
    rk = lax.broadcasted_iota(jnp.int32, (LANES, tq), 0)
    cq = lax.broadcasted_iota(jnp.int32, (LANES, tq), 1)

    s_c = jnp.dot(kc_ref[...], q4t, preferred_element_type=F32)
    dist_c = (t0 + cq) - (CMP_STRIDE * rk + (CMP_BLOCK - 1))
    mask_c = jnp.where(rk < n_cmp, dist_c, -1) >= 0
    dist_cf = dist_c.astype(F32)
    probs = []
    p_sum = jnp.zeros((LANES, tq), F32)
    for hh in range(hpg):
        slope = jnp.concatenate([slopes[hh]] * nrep, axis=1)
        sh = s_c[:, hh * tq:(hh + 1) * tq] - slope * dist_cf
        sm = jnp.where(mask_c, sh, NEG)
        m = jnp.max(sm, axis=0, keepdims=True)
        e = jnp.where(mask_c, jnp.exp2(sm - m), 0.0)
        p = e * (1.0 / jnp.maximum(jnp.sum(e, axis=0, keepdims=True), 1e-30))
        probs.append(p)
        p_sum = p_sum + p
    o_cmp = jnp.dot(vct_ref[...], jnp.concatenate(probs, axis=1).astype(BF16),
                    preferred_element_type=F32)

    ph = p_sum.astype(BF16)
    plo = (p_sum - ph.astype(F32)).astype(BF16)
    ovt = ovt_ref[...]
    imp = (jnp.dot(ovt, ph, preferred_element_type=F32)
           + jnp.dot(ovt, plo, preferred_element_type=F32))[:n_sel, :]

    rj = lax.broadcasted_iota(jnp.int32, (n_sel, tq), 0)
    tcol = t0 + lax.broadcasted_iota(jnp.int32, (n_sel, tq), 1)
    back = (tcol >> (SEL_BLOCK.bit_length() - 1)) - rj
    elig = back >= 0
    forced = jnp.where(rj == 0, 0, jnp.where(elig, back, SEL_LOCAL)) < SEL_LOCAL
    score = jnp.where(elig, jnp.where(forced, FORCE_SCORE, imp), -1.0)
    rank = jnp.zeros((n_sel, tq), F32)
    for jp in range(n_sel):
        row = score[jp:jp + 1, :]
        later = jnp.where(rj > jp, 1.0, 0.0)
        rank = rank + jnp.where(row > score, 1.0, jnp.where(row == score, later, 0.0))
    sel = jnp.where(elig, jnp.where(rank < SEL_TOPK, 1.0, 0.0), 0.0)
    sel_ref[...] = jnp.concatenate(
        [sel, jnp.zeros((LANES - n_sel, tq), F32)], axis=0).astype(sel_ref.dtype)

    glt_ref[...] = jax.nn.sigmoid(gl_ref[...]).T
    for hh in range(hpg):
        gate = _gate_rows(glt_ref, g, hh)[0]
        ocmp_ref[:, hh * dv:(hh + 1) * dv] = (gate * o_cmp[:, hh * tq:(hh + 1) * tq]).T


def _nsa_select(pq, p3, kc, vct, slopes, ovt, batch, seq, tq=512):
    g = NSA_KV_GROUPS
    nq = seq // tq
    nblk = seq // CMP_STRIDE
    n_cmp = nblk - CMP_BLOCK // CMP_STRIDE + 1
    return pl.pallas_call(
        functools.partial(_nsa_select_kernel, tq=tq, n_cmp=n_cmp, n_sel=seq // SEL_BLOCK),
        grid=(batch, g, nq),
        in_specs=[
            pl.BlockSpec((tq, NSA_HPG * NSA_QK_PAD), lambda b, gg, i: (b * nq + i, gg)),
            pl.BlockSpec((nblk, NSA_QK_PAD), lambda b, gg, i: (b * g + gg, 0)),
            pl.BlockSpec((NSA_V_DIM, nblk), lambda b, gg, i: (b * g + gg, 0)),
            pl.BlockSpec((tq, LANES), lambda b, gg, i: (b * nq + i, 6)),
            pl.BlockSpec((1, 8, LANES), lambda b, gg, i: (gg, 0, 0)),
            pl.BlockSpec(ovt.shape, lambda b, gg, i: (0, 0)),
        ],
        out_specs=[pl.BlockSpec((LANES, tq), lambda b, gg, i: (b * g + gg, i)),
                   pl.BlockSpec((tq, NSA_HPG * NSA_V_DIM), lambda b, gg, i: (b * nq + i, gg))],
        out_shape=[jax.ShapeDtypeStruct((batch * g * LANES, seq), BF16),
                   jax.ShapeDtypeStruct((batch * seq, NSA_WIDTH), F32)],
        scratch_shapes=[pltpu.VMEM((LANES, tq), F32)],
        compiler_params=_cparams(("arbitrary", "arbitrary", "arbitrary")),
        name="nsa_select",
    )(pq, kc, vct, p3, slopes, ovt)


def _nsa_attend_kernel(q_ref, ks_ref, vs_ref, kw_ref, vw_ref, gl_ref, slope_ref, sel_ref,
                       ext_ref, ocmp_ref, o_ref, vst_ref, vwt_ref, glt_ref, *, tq, n_var):
    g = pl.program_id(1)
    i = pl.program_id(2)
    t0 = i * tq
    hpg = NSA_HPG
    dv = NSA_V_DIM

    @pl.when(i == 0)
    def _():
        _stage_values_t(vs_ref, vst_ref)
        _stage_values_t(vw_ref, vwt_ref)

    q4t = _q_heads_t(q_ref)
    slopes = [slope_ref[0, hh:hh + 1, :] for hh in range(hpg)]
    sel = sel_ref[...]
    glt_ref[...] = jax.nn.sigmoid(gl_ref[...]).T

    def scores(k, k0, rows, mask_fn):
        st = jnp.dot(k, q4t, preferred_element_type=F32)
        dist = (lax.broadcasted_iota(jnp.int32, (rows, tq), 1)
                - lax.broadcasted_iota(jnp.int32, (rows, tq), 0)) + (t0 - k0)
        mask = mask_fn(dist)
        distf = dist.astype(F32)
        return st + jnp.concatenate(
            [jnp.where(mask, -slopes[hh] * distf, NEG) for hh in range(hpg)], axis=1)

    def variant(n):
        def run():
            kw0 = pl.multiple_of(jnp.maximum(t0 - WINDOW, 0), LANES)
            sw = scores(kw_ref[pl.ds(kw0, WINDOW + tq), :], kw0, WINDOW + tq,
                        lambda dist: jnp.where(dist >= 0, dist, WINDOW) < WINDOW)
            pw = jnp.exp2(sw - jnp.max(sw, axis=0, keepdims=True))
            o_win = _normalised(jnp.dot(vwt_ref[:, pl.ds(kw0, WINDOW + tq)], pw.astype(BF16),
                                        preferred_element_type=F32), dv)

            starts = [t * SLC_TILE for t in range(n, -1, -1)]
            tiles = []
            for k0 in starts:
                selx = jnp.dot(ext_ref[k0:k0 + SLC_TILE, :], sel,
                               preferred_element_type=F32)
                tiles.append(scores(
                    ks_ref[k0:k0 + SLC_TILE, :], k0, SLC_TILE,
                    lambda dist, selx=selx: jnp.where(dist >= 0, selx, 0.0) > 0.5))
            m = jnp.max(tiles[0], axis=0, keepdims=True)
            acc = jnp.dot(vst_ref[:, starts[0]:starts[0] + SLC_TILE],
                          jnp.exp2(tiles[0] - m).astype(BF16), preferred_element_type=F32)
            for st, k0 in zip(tiles[1:], starts[1:]):
                m_new = jnp.maximum(m, jnp.max(st, axis=0, keepdims=True))
                acc = jnp.exp2(m - m_new) * acc + jnp.dot(
                    vst_ref[:, k0:k0 + SLC_TILE], jnp.exp2(st - m_new).astype(BF16),
                    preferred_element_type=F32)
                m = m_new
            o_slc = _normalised(acc, dv)

            for hh in range(hpg):
                _, g_slc, g_win = _gate_rows(glt_ref, g, hh)
                lanes = slice(hh * tq, (hh + 1) * tq)
                out = g_slc * o_slc[:, lanes] + g_win * o_win[:, lanes]
                cols = slice(hh * dv, (hh + 1) * dv)
                o_ref[:, cols] = (ocmp_ref[:, cols] + out.T).astype(o_ref.dtype)
        return run

    lax.switch(t0 // SLC_TILE, [variant(n) for n in range(n_var)])


def _nsa_attend(pq, pk, pv, p3, sel, ocmp, slopes, ext, batch, seq):
    tq = LANES
    g = NSA_KV_GROUPS
    nq = seq // tq
    qw = NSA_HPG * NSA_QK_PAD
    ow = NSA_HPG * NSA_V_DIM
    vrows = NSA_V_DIM + BF16_SUBLANES
    return pl.pallas_call(
        functools.partial(_nsa_attend_kernel, tq=tq, n_var=seq // SLC_TILE),
        grid=(batch, g, nq),
        in_specs=[
            pl.BlockSpec((tq, qw), lambda b, gg, i: (b * nq + i, gg)),
            pl.BlockSpec((seq, NSA_QK_PAD), lambda b, gg, i: (b, gg)),
            pl.BlockSpec((seq, NSA_V_DIM), lambda b, gg, i: (b, gg)),
            pl.BlockSpec((seq, NSA_QK_PAD), lambda b, gg, i: (b, g + gg)),
            pl.BlockSpec((seq, NSA_V_DIM), lambda b, gg, i: (b, g + gg)),
            pl.BlockSpec((tq, LANES), lambda b, gg, i: (b * nq + i, 6)),
            pl.BlockSpec((1, 8, LANES), lambda b, gg, i: (gg, 0, 0)),
            pl.BlockSpec((LANES, tq), lambda b, gg, i: (b * g + gg, i)),
            pl.BlockSpec(ext.shape, lambda b, gg, i: (0, 0)),
            pl.BlockSpec((tq, ow), lambda b, gg, i: (b * nq + i, gg)),
        ],
        out_specs=pl.BlockSpec((tq, ow), lambda b, gg, i: (b * nq + i, gg)),
        out_shape=jax.ShapeDtypeStruct((batch * seq, NSA_WIDTH), BF16),
        scratch_shapes=[pltpu.VMEM((vrows, seq), BF16), pltpu.VMEM((vrows, seq), BF16),
                        pltpu.VMEM((LANES, tq), F32)],
        compiler_params=_cparams(("arbitrary", "arbitrary", "arbitrary")),
        name="nsa_attend",
    )(pq, pk, pv, pk, pv, p3, slopes, sel, ext, ocmp)


def _pad_head_rows(wt, heads):
    k = wt.shape[1]
    wt = wt.reshape(heads, NSA_QK_DIM, k)
    wt = jnp.pad(wt, ((0, 0), (0, NSA_QK_PAD - NSA_QK_DIM), (0, 0)))
    return wt.reshape(heads * NSA_QK_PAD, k)


def _pad_gain(gain, scale=1.0):
    return jnp.pad(gain * scale, (0, NSA_QK_PAD - NSA_QK_DIM))


def _overlap_matrix(nc, ns):
    i = np.arange(nc)[:, None]
    j = np.arange(ns)[None, :]
    lo = np.maximum(i * CMP_STRIDE, j * SEL_BLOCK)
    hi = np.minimum(i * CMP_STRIDE + CMP_BLOCK, (j + 1) * SEL_BLOCK)
    return (np.maximum(hi - lo, 0) / CMP_STRIDE).astype(np.float32)


def kernel(x, norm_attn, w_in, fox_f_bias, fox_q_gain, fox_k_gain,
           nsa_q_gain, nsa_kc_gain, nsa_ks_gain, nsa_kw_gain,
           cmp_pe_k, cmp_w1_k, cmp_w2_k, cmp_pe_v, cmp_w1_v, cmp_w2_v,
           w_up_fox, w_up_nsa, w_out, norm_ffn, w_ffn_gate, w_ffn_up, w_ffn_down):
    batch, seq, d = x.shape
    m = batch * seq
    depth = w_in.shape[0]
    pts = [0] + [int(p) for p in np.cumsum(IN_SPLITS)]
    nblk = seq // CMP_STRIDE
    n_cmp = nblk - CMP_BLOCK // CMP_STRIDE + 1
    ns = seq // SEL_BLOCK

    slopes_h = jnp.exp2(-8.0 * jnp.arange(1, NSA_HEADS + 1, dtype=F32) / NSA_HEADS) * LOG2E
    slopes = jnp.broadcast_to(
        jnp.pad(slopes_h.reshape(NSA_KV_GROUPS, NSA_HPG), ((0, 0), (0, 8 - NSA_HPG)))[:, :, None],
        (NSA_KV_GROUPS, 8, LANES))
    ovt_np = np.zeros((LANES, nblk), np.float32)
    ovt_np[:ns, :n_cmp] = _overlap_matrix(n_cmp, ns).T
    ovt = jnp.asarray(ovt_np, BF16)
    ext_np = np.zeros((seq, LANES), np.float32)
    ext_np[np.arange(seq), np.arange(seq) // SEL_BLOCK] = 1.0
    ext = jnp.asarray(ext_np, BF16)

    w_in_t = jnp.swapaxes(w_in, 1, 2)

    xf = x.reshape(m, d)
    for l in range(depth):
        wt = w_in_t[l]
        row = dict(zip(("fq", "fk", "fv", "fl", "nq", "kc", "vc", "ks", "vs", "kw", "vw", "ng",
                        "ga", "gb"), pts))

        gain1 = jnp.concatenate([jnp.tile(fox_q_gain[l] * (FOX_HEAD_DIM ** -0.5 * LOG2E), FOX_HEADS),
                                 jnp.tile(fox_k_gain[l], FOX_HEADS),
                                 jnp.ones((FOX_WIDTH,), F32)])
        flag1 = jnp.concatenate([jnp.ones((2 * FOX_WIDTH,), F32), jnp.zeros((FOX_WIDTH,), F32)])
        gain_q = jnp.tile(_pad_gain(nsa_q_gain[l], NSA_QK_DIM ** -0.5 * LOG2E), NSA_HEADS)
        gain_k = jnp.concatenate([jnp.tile(_pad_gain(nsa_ks_gain[l]), NSA_KV_GROUPS),
                                  jnp.tile(_pad_gain(nsa_kw_gain[l]), NSA_KV_GROUPS)])
        n_small = FOX_HEADS + 3 * NSA_HEADS
        w3 = jnp.concatenate([_pad_head_rows(wt[row["kc"]:row["vc"]], NSA_KV_GROUPS),
                              wt[row["vc"]:row["ks"]], wt[row["fl"]:row["nq"]],
                              wt[row["ng"]:row["ga"]],
                              jnp.zeros((LANES - n_small, d), F32)], axis=0)

        xn = _rmsnorm(xf, norm_attn[l])
        q_tile = NSA_HPG * NSA_QK_DIM
        p1 = _project(xn, wt, [t * 512 for t in range(3 * FOX_WIDTH // 512)], 512, gain1, flag1,
                      BF16, 1024, "proj_fox", group=128, count=128)
        pq = _project(xn, wt, [row["nq"], row["nq"] + q_tile], q_tile, gain_q,
                      jnp.ones_like(gain_q), BF16, 1024, "proj_nsa_q", pad_from=NSA_QK_DIM,
                      pad_to=NSA_QK_PAD, group=NSA_QK_PAD, count=NSA_QK_DIM)
        pk = _project(xn, wt, [row["ks"], row["kw"]], KV_K, gain_k, jnp.ones_like(gain_k), BF16,
                      1024, "proj_nsa_k", pad_from=NSA_QK_DIM, pad_to=NSA_QK_PAD,
                      group=NSA_QK_PAD, count=NSA_QK_DIM)
        pv = _project(xn, wt, [row["vs"], row["vw"]], KV_V, None, None, BF16, 1024, "proj_nsa_v")
        p3 = _project(xn, w3, [0], w3.shape[0], None, None, F32, 1024, "proj_f32")

        bias_row = jnp.pad(fox_f_bias[l], (0, LANES - FOX_HEADS)).reshape(1, LANES)
        crep, crow = _decay(p3, bias_row, batch, seq)
        o_a = _fox_attention(p1, crep, crow.reshape(batch, FOX_HEADS, 1, seq), batch, seq)

        pad_d = NSA_QK_PAD - NSA_QK_DIM
        pek = jnp.pad(cmp_pe_k[l], ((0, 0), (0, pad_d)))
        w1k = jnp.pad(cmp_w1_k[l].reshape(CMP_BLOCK, NSA_QK_DIM, CMP_HIDDEN),
                      ((0, 0), (0, pad_d), (0, 0))).astype(BF16)
        w2k = jnp.pad(cmp_w2_k[l], ((0, 0), (0, pad_d))).astype(BF16)
        gk = _pad_gain(nsa_kc_gain[l]).reshape(1, NSA_QK_PAD)
        w1v = cmp_w1_v[l].reshape(CMP_BLOCK, NSA_V_DIM, CMP_HIDDEN).astype(BF16)
        w2v = cmp_w2_v[l].astype(BF16)
        kc, vct = _compress(p3, pek, w1k, w2k, gk, cmp_pe_v[l], w1v, w2v, batch, seq)
        sel, ocmp = _nsa_select(pq, p3, kc, vct, slopes, ovt, batch, seq)
        o_b = _nsa_attend(pq, pk, pv, p3, sel, ocmp, slopes, ext, batch, seq)

        merged = _merge(xn, o_a, o_b, wt, row["ga"], row["gb"], w_up_fox[l], w_up_nsa[l])
        hres, hn = _out_proj_norm(merged, w_out[l], xf, norm_ffn[l])

        act = _swiglu(hn, w_ffn_gate[l], w_ffn_up[l])
        xf = _matmul_residual(act, w_ffn_down[l], hres, 512, 512, "ffn_down")
    return xf.reshape(batch, seq, d)
```

```python
import functools

import numpy as np
import jax
import jax.numpy as jnp
from jax import lax
from jax.experimental import pallas as pl
from jax.experimental.pallas import tpu as pltpu

F32 = jnp.float32
BF16 = jnp.bfloat16

D_MODEL = 2048
FOX_HEADS = 8
FOX_HEAD_DIM = 128
FOX_WIDTH = FOX_HEADS * FOX_HEAD_DIM
NSA_HEADS = 8
NSA_KV_GROUPS = 2
NSA_HPG = NSA_HEADS // NSA_KV_GROUPS
NSA_QK_DIM = 192
NSA_QK_PAD = 256
NSA_V_DIM = 128
NSA_WIDTH = NSA_HEADS * NSA_V_DIM
CMP_BLOCK = 32
CMP_STRIDE = 16
CMP_HIDDEN = 256
SEL_BLOCK = 64
SEL_TOPK = 16
SEL_LOCAL = 2
FORCE_SCORE = 1.0e4
WINDOW = 512
KV_K = NSA_KV_GROUPS * NSA_QK_DIM
KV_V = NSA_KV_GROUPS * NSA_V_DIM
D_FF = -(-(8 * D_MODEL) // (3 * 256)) * 256
RMS_EPS = 1e-6
IN_SPLITS = (FOX_WIDTH, FOX_WIDTH, FOX_WIDTH, FOX_HEADS,
             NSA_HEADS * NSA_QK_DIM, KV_K, KV_V, KV_K, KV_V, KV_K, KV_V,
             3 * NSA_HEADS, D_MODEL, D_MODEL)

LANES = 128
SUBLANES = 8
BF16_SUBLANES = 16
NEG = -1.0e30
LOG2E = 1.4426950408889634
SLC_TILE = 512
VMEM_LIMIT = 56 * 1024 * 1024


def _cparams(sem):
    return pltpu.CompilerParams(dimension_semantics=sem, vmem_limit_bytes=VMEM_LIMIT)


def _rms_kernel(x_ref, g_ref, o_ref):
    x = x_ref[...]
    ms = jnp.mean(x * x, axis=-1, keepdims=True)
    o_ref[...] = (x * lax.rsqrt(ms + RMS_EPS) * g_ref[...]).astype(o_ref.dtype)


def _rmsnorm(x, gain, tm=512):
    m, d = x.shape
    return pl.pallas_call(
        _rms_kernel,
        grid=(m // tm,),
        in_specs=[pl.BlockSpec((tm, d), lambda i: (i, 0)),
                  pl.BlockSpec((1, d), lambda i: (0, 0))],
        out_specs=pl.BlockSpec((tm, d), lambda i: (i, 0)),
        out_shape=jax.ShapeDtypeStruct((m, d), BF16),
        compiler_params=_cparams(("arbitrary",)),
        name="rmsnorm",
    )(x, gain.reshape(1, d))


STAGE_CHUNK = 512


def _stage_weight(w_ref, wb_ref, transposed=False, pad_from=0, pad_to=0):
    if not transposed:
        wb_ref[...] = w_ref[...].astype(BF16)
        return
    n_in, k = w_ref.shape
    for c in range(k // STAGE_CHUNK):
        cols = slice(c * STAGE_CHUNK, (c + 1) * STAGE_CHUNK)
        w = w_ref[:, cols]
        if pad_from != pad_to:
            zero = jnp.zeros((pad_to - pad_from, STAGE_CHUNK), F32)
            w = jnp.concatenate(
                [piece for h in range(n_in // pad_from)
                 for piece in (w[h * pad_from:(h + 1) * pad_from, :], zero)], axis=0)
        wb_ref[cols, :] = w.T.astype(BF16)


def _first_m_step():
    return pl.program_id(1) == 0


def _proj_kernel(a_ref, w_ref, gain_ref, flag_ref, o_ref, wb_ref, *, pad_from, pad_to,
                 group, count):
    @pl.when(_first_m_step())
    def _():
        _stage_weight(w_ref, wb_ref, True, pad_from, pad_to)

    y = jnp.dot(a_ref[...], wb_ref[...], preferred_element_type=F32)
    if group == 0:
        o_ref[...] = y.astype(o_ref.dtype)
        return
    for c in range(y.shape[1] // group):
        sl = slice(c * group, (c + 1) * group)
        yc = y[:, sl]
        ss = jnp.sum(yc * yc, axis=-1, keepdims=True)
        rs = lax.rsqrt(ss * (1.0 / count) + RMS_EPS)
        scale = jnp.where(flag_ref[:, sl] > 0.0, rs, 1.0)
        o_ref[:, sl] = (yc * scale * gain_ref[:, sl]).astype(o_ref.dtype)


def _row_window(starts, rows, k):
    def index(j, i):
        start = starts[-1]
        for t in range(len(starts) - 2, -1, -1):
            start = jnp.where(j == t, starts[t], start)
        return pl.multiple_of(start, SUBLANES), 0
    assert all(s % SUBLANES == 0 for s in starts)
    return pl.BlockSpec((pl.Element(rows), pl.Element(k)), index)


def _project(a, wt, starts, tn_in, gain, flag, out_dtype, tm, name,
             pad_from=0, pad_to=0, group=0, count=1):
    m, k = a.shape
    n_tiles = len(starts)
    tn_out = tn_in if pad_from == pad_to else tn_in // pad_from * pad_to
    n_out = n_tiles * tn_out
    if gain is None:
        gain = jnp.ones((n_out,), F32)
        flag = jnp.zeros((n_out,), F32)
    return pl.pallas_call(
        functools.partial(_proj_kernel, pad_from=pad_from, pad_to=pad_to, group=group,
                          count=count),
        grid=(n_tiles, m // tm),
        in_specs=[pl.BlockSpec((tm, k), lambda j, i: (i, 0)),
                  _row_window(starts, tn_in, k),
                  pl.BlockSpec((1, tn_out), lambda j, i: (0, j)),
                  pl.BlockSpec((1, tn_out), lambda j, i: (0, j))],
        out_specs=pl.BlockSpec((tm, tn_out), lambda j, i: (i, j)),
        out_shape=jax.ShapeDtypeStruct((m, n_out), out_dtype),
        scratch_shapes=[pltpu.VMEM((k, tn_out), BF16)],
        compiler_params=_cparams(("arbitrary", "arbitrary")),
        name=name,
    )(a, wt, gain.reshape(1, n_out), flag.reshape(1, n_out))


def _out_norm_kernel(a_ref, w_ref, r_ref, g_ref, h_ref, hn_ref, wb_ref):
    @pl.when(pl.program_id(0) == 0)
    def _():
        _stage_weight(w_ref, wb_ref)

    h = r_ref[...] + jnp.dot(a_ref[...], wb_ref[...], preferred_element_type=F32)
    h_ref[...] = h
    ms = jnp.mean(h * h, axis=-1, keepdims=True)
    hn_ref[...] = (h * lax.rsqrt(ms + RMS_EPS) * g_ref[...]).astype(hn_ref.dtype)


def _out_proj_norm(a, w, res, gain, tm=512):
    m, k = a.shape
    n = w.shape[1]
    full = lambda i: (0, 0)
    row = lambda i: (i, 0)
    return pl.pallas_call(
        _out_norm_kernel,
        grid=(m // tm,),
        in_specs=[pl.BlockSpec((tm, k), row),
                  pl.BlockSpec((k, n), full, pipeline_mode=pl.Buffered(1)),
                  pl.BlockSpec((tm, n), row), pl.BlockSpec((1, n), full)],
        out_specs=[pl.BlockSpec((tm, n), row), pl.BlockSpec((tm, n), row)],
        out_shape=[jax.ShapeDtypeStruct((m, n), F32), jax.ShapeDtypeStruct((m, n), BF16)],
        scratch_shapes=[pltpu.VMEM((k, n), BF16)],
        compiler_params=_cparams(("arbitrary",)),
        name="out_proj_norm",
    )(a, w, res, gain.reshape(1, n))


def _mm_res_kernel(a_ref, w_ref, r_ref, o_ref, wb_ref):
    @pl.when(_first_m_step())
    def _():
        _stage_weight(w_ref, wb_ref)

    o_ref[...] = r_ref[...] + jnp.dot(a_ref[...], wb_ref[...], preferred_element_type=F32)


def _matmul_residual(a, w, res, tm, tn, name):
    m, k = a.shape
    n = w.shape[1]
    return pl.pallas_call(
        _mm_res_kernel,
        grid=(n // tn, m // tm),
        in_specs=[pl.BlockSpec((tm, k), lambda j, i: (i, 0)),
                  pl.BlockSpec((k, tn), lambda j, i: (0, j)),
                  pl.BlockSpec((tm, tn), lambda j, i: (i, j))],
        out_specs=pl.BlockSpec((tm, tn), lambda j, i: (i, j)),
        out_shape=jax.ShapeDtypeStruct((m, n), F32),
        scratch_shapes=[pltpu.VMEM((k, tn), BF16)],
        compiler_params=_cparams(("arbitrary", "arbitrary")),
        name=name,
    )(a, w, res)


def _merge_kernel(xn_ref, oa_ref, ob_ref, wga_ref, wgb_ref, wuf_ref, wun_ref, o_ref,
                  bga_ref, bgb_ref, buf_ref, bun_ref):
    @pl.when(_first_m_step())
    def _():
        _stage_weight(wga_ref, bga_ref, True)
        _stage_weight(wgb_ref, bgb_ref, True)
        _stage_weight(wuf_ref, buf_ref)
        _stage_weight(wun_ref, bun_ref)

    xn = xn_ref[...]
    ga = jax.nn.sigmoid(jnp.dot(xn, bga_ref[...], preferred_element_type=F32))
    ua = jnp.dot(oa_ref[...], buf_ref[...], preferred_element_type=F32)
    acc = ga * ua
    gb = jax.nn.sigmoid(jnp.dot(xn, bgb_ref[...], preferred_element_type=F32))
    ub = jnp.dot(ob_ref[...], bun_ref[...], preferred_element_type=F32)
    o_ref[...] = (acc + gb * ub).astype(o_ref.dtype)


def _merge(xn, oa, ob, wt, row_a, row_b, wuf, wun, tm=512, tn=512):
    m, d = xn.shape
    n = wuf.shape[1]
    ka = oa.shape[1]
    kb = ob.shape[1]
    row = lambda j, i: (i, 0)
    col = lambda j, i: (0, j)
    return pl.pallas_call(
        _merge_kernel,
        grid=(n // tn, m // tm),
        in_specs=[pl.BlockSpec((tm, d), row), pl.BlockSpec((tm, ka), row),
                  pl.BlockSpec((tm, kb), row),
                  _row_window([row_a + t * tn for t in range(n // tn)], tn, d),
                  _row_window([row_b + t * tn for t in range(n // tn)], tn, d),
                  pl.BlockSpec((ka, tn), col), pl.BlockSpec((kb, tn), col)],
        out_specs=pl.BlockSpec((tm, tn), lambda j, i: (i, j)),
        out_shape=jax.ShapeDtypeStruct((m, n), BF16),
        scratch_shapes=[pltpu.VMEM((d, tn), BF16), pltpu.VMEM((d, tn), BF16),
                        pltpu.VMEM((ka, tn), BF16), pltpu.VMEM((kb, tn), BF16)],
        compiler_params=_cparams(("arbitrary", "arbitrary")),
        name="gated_merge",
    )(xn, oa, ob, wt, wt, wuf, wun)


def _swiglu_kernel(a_ref, wg_ref, wu_ref, o_ref, bg_ref, bu_ref):
    @pl.when(_first_m_step())
    def _():
        _stage_weight(wg_ref, bg_ref)
        _stage_weight(wu_ref, bu_ref)

    a = a_ref[...]
    gt = jnp.dot(a, bg_ref[...], preferred_element_type=F32)
    up = jnp.dot(a, bu_ref[...], preferred_element_type=F32)
    o_ref[...] = (gt * jax.nn.sigmoid(gt) * up).astype(o_ref.dtype)


def _swiglu(a, wg, wu, tm=1024, tn=512):
    m, k = a.shape
    n = wg.shape[1]
    return pl.pallas_call(
        _swiglu_kernel,
        grid=(n // tn, m // tm),
        in_specs=[pl.BlockSpec((tm, k), lambda j, i: (i, 0)),
                  pl.BlockSpec((k, tn), lambda j, i: (0, j)),
                  pl.BlockSpec((k, tn), lambda j, i: (0, j))],
        out_specs=pl.BlockSpec((tm, tn), lambda j, i: (i, j)),
        out_shape=jax.ShapeDtypeStruct((m, n), BF16),
        scratch_shapes=[pltpu.VMEM((k, tn), BF16), pltpu.VMEM((k, tn), BF16)],
        compiler_params=_cparams(("arbitrary", "arbitrary")),
        name="swiglu_up",
    )(a, wg, wu)


def _split3(x):
    hi = x.astype(BF16)
    r1 = x - hi.astype(F32)
    mid = r1.astype(BF16)
    lo = (r1 - mid.astype(F32)).astype(BF16)
    return hi, mid, lo


def _decay_kernel(z_ref, b_ref, crep_ref, crow_ref, *, blk):
    t = z_ref.shape[0]
    r = lax.broadcasted_iota(jnp.int32, (blk, blk), 0)
    c = lax.broadcasted_iota(jnp.int32, (blk, blk), 1)
    tri = jnp.where(r >= c, 1.0, 0.0).astype(BF16)
    lane = lax.broadcasted_iota(jnp.int32, (blk, LANES), 1)
    carry = jnp.zeros((1, LANES), F32)
    for s in range(t // blk):
        rows = slice(s * blk, (s + 1) * blk)
        z = z_ref[rows, :] + b_ref[...]
        logf = (jnp.minimum(z, 0.0) - jnp.log1p(jnp.exp(-jnp.abs(z)))) * LOG2E
        hi, mid, lo = _split3(logf)
        cb = (jnp.dot(tri, hi, preferred_element_type=F32)
              + jnp.dot(tri, mid, preferred_element_type=F32)
              + jnp.dot(tri, lo, preferred_element_type=F32)) + carry
        carry = cb[blk - 1:blk, :]
        crow_ref[0, :, rows] = cb.T[:FOX_HEADS, :]
        for h in range(FOX_HEADS):
            col = jnp.sum(jnp.where(lane == h, cb, 0.0), axis=-1, keepdims=True)
            crep_ref[0, h, rows, :] = jnp.broadcast_to(col, (blk, LANES))


def _decay(p3, bias_row, batch, seq, blk=256):
    return pl.pallas_call(
        functools.partial(_decay_kernel, blk=blk),
        grid=(batch,),
        in_specs=[pl.BlockSpec((seq, LANES), lambda b: (b, 6)),
                  pl.BlockSpec((1, LANES), lambda b: (0, 0))],
        out_specs=[pl.BlockSpec((1, FOX_HEADS, seq, LANES), lambda b: (b, 0, 0, 0)),
                   pl.BlockSpec((1, FOX_HEADS, seq), lambda b: (b, 0, 0))],
        out_shape=[jax.ShapeDtypeStruct((batch, FOX_HEADS, seq, LANES), F32),
                   jax.ShapeDtypeStruct((batch, FOX_HEADS, seq), F32)],
        compiler_params=_cparams(("arbitrary",)),
        name="fox_decay_cumsum",
    )(p3, bias_row)


def _transpose_bf16(x):
    return x.astype(F32).T.astype(BF16)


def _normalised(acc, d):
    return acc[:d, :] * (1.0 / jnp.maximum(acc[d:d + 1, :], 1e-30))


def _fox_kernel(q_ref, k_ref, v_ref, crep_ref, crow_ref, o_ref, vt_ref, *, tq, nq, nh):
    i = pl.program_id(2)
    dh = FOX_HEAD_DIM

    @pl.when(i == 0)
    def _():
        for h in range(nh):
            vt_ref[h, 0:dh, :] = _transpose_bf16(v_ref[:, h * dh:(h + 1) * dh])
            vt_ref[h, dh:, :] = jnp.ones((vt_ref.shape[1] - dh, vt_ref.shape[2]), BF16)

    qts = [_transpose_bf16(q_ref[:, h * dh:(h + 1) * dh]) for h in range(nh)]
    cis = [crow_ref[0, h] for h in range(nh)]
    reps = tq // LANES
    rk = lax.broadcasted_iota(jnp.int32, (tq, tq), 0)
    cq = lax.broadcasted_iota(jnp.int32, (tq, tq), 1)

    def scores(h, k0):
        cj = crep_ref[0, h, k0:k0 + tq, :]
        st = jnp.dot(k_ref[k0:k0 + tq, h * dh:(h + 1) * dh], qts[h],
                     preferred_element_type=F32)
        return st + cis[h] - jnp.concatenate([cj] * reps, axis=1)

    def variant(n):
        def run():
            starts = [t * tq for t in range(n, -1, -1)]
            tiles = [[jnp.where(rk <= cq, scores(h, starts[0]), NEG)]
                     + [scores(h, k0) for k0 in starts[1:]] for h in range(nh)]
            ms = [jnp.max(tiles[h][0], axis=0, keepdims=True) for h in range(nh)]
            accs = [jnp.dot(vt_ref[h, :, starts[0]:starts[0] + tq],
                            jnp.exp2(tiles[h][0] - ms[h]).astype(BF16),
                            preferred_element_type=F32) for h in range(nh)]
            for t in range(1, n + 1):
                k0 = starts[t]
                for h in range(nh):
                    st = tiles[h][t]
                    m_new = jnp.maximum(ms[h], jnp.max(st, axis=0, keepdims=True))
                    accs[h] = jnp.exp2(ms[h] - m_new) * accs[h] + jnp.dot(
                        vt_ref[h, :, k0:k0 + tq], jnp.exp2(st - m_new).astype(BF16),
                        preferred_element_type=F32)
                    ms[h] = m_new
            for h in range(nh):
                o_ref[:, h * dh:(h + 1) * dh] = _normalised(accs[h], dh).T.astype(o_ref.dtype)
        return run

    lax.switch(i, [variant(n) for n in range(nq)])


def _fox_attention(p1, crep, crow4, batch, seq, tq=512, nh=4):
    nq = seq // tq
    hg = FOX_HEADS // nh
    w = nh * FOX_HEAD_DIM
    return pl.pallas_call(
        functools.partial(_fox_kernel, tq=tq, nq=nq, nh=nh),
        grid=(batch, hg, nq),
        in_specs=[pl.BlockSpec((tq, w), lambda b, hh, i: (b * nq + i, hh)),
                  pl.BlockSpec((seq, w), lambda b, hh, i: (b, hg + hh)),
                  pl.BlockSpec((seq, w), lambda b, hh, i: (b, 2 * hg + hh)),
                  pl.BlockSpec((1, nh, seq, LANES), lambda b, hh, i: (b, hh, 0, 0)),
                  pl.BlockSpec((1, nh, 1, tq), lambda b, hh, i: (b, hh, 0, i))],
        out_specs=pl.BlockSpec((tq, w), lambda b, hh, i: (b * nq + i, hh)),
        out_shape=jax.ShapeDtypeStruct((batch * seq, FOX_WIDTH), BF16),
        scratch_shapes=[pltpu.VMEM((nh, FOX_HEAD_DIM + BF16_SUBLANES, seq), BF16)],
        compiler_params=_cparams(("arbitrary", "arbitrary", "arbitrary")),
        name="fox_attention",
    )(p1, p1, p1, crep, crow4)


def _compress_one(z_refs, pe_ref, w1_ref, w2_ref, nblk):
    half = CMP_BLOCK // 2
    first = jnp.zeros((nblk, CMP_HIDDEN), F32)
    second = jnp.zeros((nblk, CMP_HIDDEN), F32)
    for p in range(half):
        rows = pl.ds(p, nblk, stride=CMP_STRIDE)
        zp = [z_ref[rows, :] for z_ref in z_refs]
        zp = zp[0] if len(zp) == 1 else jnp.concatenate(zp, axis=1)
        first += jnp.dot((zp + pe_ref[p:p + 1, :]).astype(BF16), w1_ref[p],
                         preferred_element_type=F32)
        second += jnp.dot((zp + pe_ref[half + p:half + p + 1, :]).astype(BF16),
                          w1_ref[half + p], preferred_element_type=F32)
    hid = first + pltpu.roll(second, nblk - 1, 0)
    act = (hid * jax.nn.sigmoid(hid)).astype(BF16)
    return jnp.dot(act, w2_ref[...], preferred_element_type=F32)


def _compress_kernel(zk0_ref, zk1_ref, zv_ref, pek_ref, w1k_ref, w2k_ref, gk_ref,
                     pev_ref, w1v_ref, w2v_ref, kc_ref, vc_ref, *, nblk):
    kc = _compress_one((zk0_ref, zk1_ref), pek_ref, w1k_ref, w2k_ref, nblk)
    ms = jnp.sum(kc * kc, axis=-1, keepdims=True) * (1.0 / NSA_QK_DIM)
    kc_ref[...] = (kc * lax.rsqrt(ms + RMS_EPS) * gk_ref[...]).astype(kc_ref.dtype)
    vc = _compress_one((zv_ref,), pev_ref, w1v_ref, w2v_ref, nblk)
    vc_ref[...] = vc.T.astype(vc_ref.dtype)


def _compress(p3, pek, w1k, w2k, gk, pev, w1v, w2v, batch, seq):
    g = NSA_KV_GROUPS
    nblk = seq // CMP_STRIDE
    full2 = lambda b, gg: (0, 0)
    full3 = lambda b, gg: (0, 0, 0)
    return pl.pallas_call(
        functools.partial(_compress_kernel, nblk=nblk),
        grid=(batch, g),
        in_specs=[pl.BlockSpec((seq, LANES), lambda b, gg: (b, 2 * gg)),
                  pl.BlockSpec((seq, LANES), lambda b, gg: (b, 2 * gg + 1)),
                  pl.BlockSpec((seq, NSA_V_DIM), lambda b, gg: (b, 4 + gg)),
                  pl.BlockSpec(pek.shape, full2), pl.BlockSpec(w1k.shape, full3),
                  pl.BlockSpec(w2k.shape, full2), pl.BlockSpec(gk.shape, full2),
                  pl.BlockSpec(pev.shape, full2), pl.BlockSpec(w1v.shape, full3),
                  pl.BlockSpec(w2v.shape, full2)],
        out_specs=[pl.BlockSpec((nblk, NSA_QK_PAD), lambda b, gg: (b * g + gg, 0)),
                   pl.BlockSpec((NSA_V_DIM, nblk), lambda b, gg: (b * g + gg, 0))],
        out_shape=[jax.ShapeDtypeStruct((batch * g * nblk, NSA_QK_PAD), BF16),
                   jax.ShapeDtypeStruct((batch * g * NSA_V_DIM, nblk), BF16)],
        compiler_params=_cparams(("arbitrary", "arbitrary")),
        name="nsa_compress",
    )(p3, p3, p3, pek, w1k, w2k, gk, pev, w1v, w2v)


def _q_heads_t(q_ref):
    qb = q_ref[...]
    return jnp.concatenate(
        [_transpose_bf16(qb[:, hh * NSA_QK_PAD:(hh + 1) * NSA_QK_PAD]) for hh in range(NSA_HPG)],
        axis=1)


def _gate_rows(glt_ref, g, hh):
    base = FOX_HEADS + (g * NSA_HPG + hh) * 3
    return [glt_ref[pl.ds(base + br, 1), :] for br in range(3)]


def _nsa_select_kernel(q_ref, kc_ref, vct_ref, gl_ref, slope_ref, ovt_ref,
                       sel_ref, ocmp_ref, glt_ref, *, tq, n_cmp, n_sel):
    g = pl.program_id(1)
    t0 = pl.program_id(2) * tq
    hpg = NSA_HPG
    dv = NSA_V_DIM
    q4t = _q_heads_t(q_ref)
    slopes = [slope_ref[0, hh:hh + 1, :] for hh in range(hpg)]
    nrep = tq // LANES
    rk = lax.broadcasted_iota(jnp.int32, (LANES, tq), 0)
    cq = lax.broadcasted_iota(jnp.int32, (LANES, tq), 1)

    s_c = jnp.dot(kc_ref[...], q4t, preferred_element_type=F32)
    dist_c = (t0 + cq) - (CMP_STRIDE * rk + (CMP_BLOCK - 1))
    mask_c = jnp.where(rk < n_cmp, dist_c, -1) >= 0
    dist_cf = dist_c.astype(F32)
    probs = []
    p_sum = jnp.zeros((LANES, tq), F32)
    for hh in range(hpg):
        slope = jnp.concatenate([slopes[hh]] * nrep, axis=1)
        sh = s_c[:, hh * tq:(hh + 1) * tq] - slope * dist_cf
        sm = jnp.where(mask_c, sh, NEG)
        m = jnp.max(sm, axis=0, keepdims=True)
        e = jnp.where(mask_c, jnp.exp2(sm - m), 0.0)
        p = e * (1.0 / jnp.maximum(jnp.sum(e, axis=0, keepdims=True), 1e-30))
        probs.append(p)
        p_sum = p_sum + p
    o_cmp = jnp.dot(vct_ref[...], jnp.concatenate(probs, axis=1).astype(BF16),
                    preferred_element_type=F32)

    ph = p_sum.astype(BF16)
    plo = (p_sum - ph.astype(F32)).astype(BF16)
    ovt = ovt_ref[...]
    imp = (jnp.dot(ovt, ph, preferred_element_type=F32)
           + jnp.dot(ovt, plo, preferred_element_type=F32))[:n_sel, :]

    rj = lax.broadcasted_iota(jnp.int32, (n_sel, tq), 0)
    tcol = t0 + lax.broadcasted_iota(jnp.int32, (n_sel, tq), 1)
    back = (tcol >> (SEL_BLOCK.bit_length() - 1)) - rj
    elig = back >= 0
    forced = jnp.where(rj == 0, 0, jnp.where(elig, back, SEL_LOCAL)) < SEL_LOCAL
    score = jnp.where(elig, jnp.where(forced, FORCE_SCORE, imp), -1.0)
    rank = jnp.zeros((n_sel, tq), F32)
    for jp in range(n_sel):
        row = score[jp:jp + 1, :]
        later = jnp.where(rj > jp, 1.0, 0.0)
        rank = rank + jnp.where(row > score, 1.0, jnp.where(row == score, later, 0.0))
    sel = jnp.where(elig, jnp.where(rank < SEL_TOPK, 1.0, 0.0), 0.0)
    sel_ref[...] = jnp.concatenate(
        [sel, jnp.zeros((LANES - n_sel, tq), F32)], axis=0).astype(sel_ref.dtype)

    glt_ref[...] = jax.nn.sigmoid(gl_ref[...]).T
    for hh in range(hpg):
        gate = _gate_rows(glt_ref, g, hh)[0]
        ocmp_ref[:, hh * dv:(hh + 1) * dv] = (gate * o_cmp[:, hh * tq:(hh + 1) * tq]).T


def _nsa_select(pq, p3, kc, vct, slopes, ovt, batch, seq, tq=512):
    g = NSA_KV_GROUPS
    nq = seq // tq
    nblk = seq // CMP_STRIDE
    n_cmp = nblk - CMP_BLOCK // CMP_STRIDE + 1
    return pl.pallas_call(
        functools.partial(_nsa_select_kernel, tq=tq, n_cmp=n_cmp, n_sel=seq // SEL_BLOCK),
        grid=(batch, g, nq),
        in_specs=[
            pl.BlockSpec((tq, NSA_HPG * NSA_QK_PAD), lambda b, gg, i: (b * nq + i, gg)),
            pl.BlockSpec((nblk, NSA_QK_PAD), lambda b, gg, i: (b * g + gg, 0)),
            pl.BlockSpec((NSA_V_DIM, nblk), lambda b, gg, i: (b * g + gg, 0)),
            pl.BlockSpec((tq, LANES), lambda b, gg, i: (b * nq + i, 6)),
            pl.BlockSpec((1, 8, LANES), lambda b, gg, i: (gg, 0, 0)),
            pl.BlockSpec(ovt.shape, lambda b, gg, i: (0, 0)),
        ],
        out_specs=[pl.BlockSpec((LANES, tq), lambda b, gg, i: (b * g + gg, i)),
                   pl.BlockSpec((tq, NSA_HPG * NSA_V_DIM), lambda b, gg, i: (b * nq + i, gg))],
        out_shape=[jax.ShapeDtypeStruct((batch * g * LANES, seq), BF16),
                   jax.ShapeDtypeStruct((batch * seq, NSA_WIDTH), F32)],
        scratch_shapes=[pltpu.VMEM((LANES, tq), F32)],
        compiler_params=_cparams(("arbitrary", "arbitrary", "arbitrary")),
        name="nsa_select",
    )(pq, kc, vct, p3, slopes, ovt)


def _nsa_attend_kernel(q_ref, ks_ref, vs_ref, kw_ref, vw_ref, gl_ref, slope_ref, sel_ref,
                       ext_ref, ocmp_ref, o_ref, vst_ref, vwt_ref, glt_ref, *, tq, n_var):
    i = pl.program_id(1)
    t0 = i * tq
    hpg = NSA_HPG
    ng = NSA_KV_GROUPS
    dv = NSA_V_DIM
    dk = NSA_QK_PAD

    @pl.when(i == 0)
    def _():
        ones = jnp.ones((vst_ref.shape[1] - dv, vst_ref.shape[2]), BF16)
        for gg in range(ng):
            vst_ref[gg, 0:dv, :] = _transpose_bf16(vs_ref[:, gg * dv:(gg + 1) * dv])
            vst_ref[gg, dv:, :] = ones
            vwt_ref[gg, 0:dv, :] = _transpose_bf16(vw_ref[:, gg * dv:(gg + 1) * dv])
            vwt_ref[gg, dv:, :] = ones

    qb = q_ref[...]
    q4t = [jnp.concatenate(
        [_transpose_bf16(qb[:, (gg * hpg + hh) * dk:(gg * hpg + hh + 1) * dk])
         for hh in range(hpg)], axis=1) for gg in range(ng)]
    slopes = [[slope_ref[gg, hh:hh + 1, :] for hh in range(hpg)] for gg in range(ng)]
    sels = [sel_ref[gg * LANES:(gg + 1) * LANES, :] for gg in range(ng)]
    glt_ref[...] = jax.nn.sigmoid(gl_ref[...]).T

    def scores(gg, k, k0, rows, mask_fn):
        st = jnp.dot(k, q4t[gg], preferred_element_type=F32)
        dist = (lax.broadcasted_iota(jnp.int32, (rows, tq), 1)
                - lax.broadcasted_iota(jnp.int32, (rows, tq), 0)) + (t0 - k0)
        mask = mask_fn(dist)
        distf = dist.astype(F32)
        return st + jnp.concatenate(
            [jnp.where(mask, -slopes[gg][hh] * distf, NEG) for hh in range(hpg)], axis=1)

    def variant(n):
        def run():
            kw0 = pl.multiple_of(jnp.maximum(t0 - WINDOW, 0), LANES)
            wrows = WINDOW + tq
            o_win = []
            for gg in range(ng):
                sw = scores(gg, kw_ref[pl.ds(kw0, wrows), gg * dk:(gg + 1) * dk], kw0, wrows,
                            lambda dist: jnp.where(dist >= 0, dist, WINDOW) < WINDOW)
                pw = jnp.exp2(sw - jnp.max(sw, axis=0, keepdims=True))
                o_win.append(_normalised(
                    jnp.dot(vwt_ref[gg, :, pl.ds(kw0, wrows)], pw.astype(BF16),
                            preferred_element_type=F32), dv))

            starts = [t * SLC_TILE for t in range(n, -1, -1)]
            tiles = [[] for _ in range(ng)]
            for k0 in starts:
                for gg in range(ng):
                    selx = jnp.dot(ext_ref[k0:k0 + SLC_TILE, :], sels[gg],
                                   preferred_element_type=F32)
                    tiles[gg].append(scores(
                        gg, ks_ref[k0:k0 + SLC_TILE, gg * dk:(gg + 1) * dk], k0, SLC_TILE,
                        lambda dist, selx=selx: jnp.where(dist >= 0, selx, 0.0) > 0.5))
            ms = [jnp.max(tiles[gg][0], axis=0, keepdims=True) for gg in range(ng)]
            accs = [jnp.dot(vst_ref[gg, :, starts[0]:starts[0] + SLC_TILE],
                            jnp.exp2(tiles[gg][0] - ms[gg]).astype(BF16),
                            preferred_element_type=F32) for gg in range(ng)]
            for t in range(1, n + 1):
                k0 = starts[t]
                for gg in range(ng):
                    st = tiles[gg][t]
                    m_new = jnp.maximum(ms[gg], jnp.max(st, axis=0, keepdims=True))
                    accs[gg] = jnp.exp2(ms[gg] - m_new) * accs[gg] + jnp.dot(
                        vst_ref[gg, :, k0:k0 + SLC_TILE], jnp.exp2(st - m_new).astype(BF16),
                        preferred_element_type=F32)
                    ms[gg] = m_new

            for gg in range(ng):
                o_slc = _normalised(accs[gg], dv)
                for hh in range(hpg):
                    _, g_slc, g_win = _gate_rows(glt_ref, gg, hh)
                    lanes = slice(hh * tq, (hh + 1) * tq)
                    out = g_slc * o_slc[:, lanes] + g_win * o_win[gg][:, lanes]
                    cols = slice((gg * hpg + hh) * dv, (gg * hpg + hh + 1) * dv)
                    o_ref[:, cols] = (ocmp_ref[:, cols] + out.T).astype(o_ref.dtype)
        return run

    lax.switch(t0 // SLC_TILE, [variant(n) for n in range(n_var)])


def _nsa_attend(pq, pk, pv, p3, sel, ocmp, slopes, ext, batch, seq):
    tq = LANES
    g = NSA_KV_GROUPS
    nq = seq // tq
    vrows = NSA_V_DIM + BF16_SUBLANES
    return pl.pallas_call(
        functools.partial(_nsa_attend_kernel, tq=tq, n_var=seq // SLC_TILE),
        grid=(batch, nq),
        in_specs=[
            pl.BlockSpec((tq, NSA_HEADS * NSA_QK_PAD), lambda b, i: (b * nq + i, 0)),
            pl.BlockSpec((seq, g * NSA_QK_PAD), lambda b, i: (b, 0)),
            pl.BlockSpec((seq, g * NSA_V_DIM), lambda b, i: (b, 0)),
            pl.BlockSpec((seq, g * NSA_QK_PAD), lambda b, i: (b, 1)),
            pl.BlockSpec((seq, g * NSA_V_DIM), lambda b, i: (b, 1)),
            pl.BlockSpec((tq, LANES), lambda b, i: (b * nq + i, 6)),
            pl.BlockSpec(slopes.shape, lambda b, i: (0, 0, 0)),
            pl.BlockSpec((g * LANES, tq), lambda b, i: (b, i)),
            pl.BlockSpec(ext.shape, lambda b, i: (0, 0)),
            pl.BlockSpec((tq, NSA_WIDTH), lambda b, i: (b * nq + i, 0)),
        ],
        out_specs=pl.BlockSpec((tq, NSA_WIDTH), lambda b, i: (b * nq + i, 0)),
        out_shape=jax.ShapeDtypeStruct((batch * seq, NSA_WIDTH), BF16),
        scratch_shapes=[pltpu.VMEM((g, vrows, seq), BF16), pltpu.VMEM((g, vrows, seq), BF16),
                        pltpu.VMEM((LANES, tq), F32)],
        compiler_params=_cparams(("arbitrary", "arbitrary")),
        name="nsa_attend",
    )(pq, pk, pv, pk, pv, p3, slopes, sel, ext, ocmp)


def _pad_head_rows(wt, heads):
    k = wt.shape[1]
    wt = wt.reshape(heads, NSA_QK_DIM, k)
    wt = jnp.pad(wt, ((0, 0), (0, NSA_QK_PAD - NSA_QK_DIM), (0, 0)))
    return wt.reshape(heads * NSA_QK_PAD, k)


def _pad_gain(gain, scale=1.0):
    return jnp.pad(gain * scale, (0, NSA_QK_PAD - NSA_QK_DIM))


def _overlap_matrix(nc, ns):
    i = np.arange(nc)[:, None]
    j = np.arange(ns)[None, :]
    lo = np.maximum(i * CMP_STRIDE, j * SEL_BLOCK)
    hi = np.minimum(i * CMP_STRIDE + CMP_BLOCK, (j + 1) * SEL_BLOCK)
    return (np.maximum(hi - lo, 0) / CMP_STRIDE).astype(np.float32)


def kernel(x, norm_attn, w_in, fox_f_bias, fox_q_gain, fox_k_gain,
           nsa_q_gain, nsa_kc_gain, nsa_ks_gain, nsa_kw_gain,
           cmp_pe_k, cmp_w1_k, cmp_w2_k, cmp_pe_v, cmp_w1_v, cmp_w2_v,
           w_up_fox, w_up_nsa, w_out, norm_ffn, w_ffn_gate, w_ffn_up, w_ffn_down):
    batch, seq, d = x.shape
    m = batch * seq
    depth = w_in.shape[0]
    pts = [0] + [int(p) for p in np.cumsum(IN_SPLITS)]
    nblk = seq // CMP_STRIDE
    n_cmp = nblk - CMP_BLOCK // CMP_STRIDE + 1
    ns = seq // SEL_BLOCK

    slopes_h = jnp.exp2(-8.0 * jnp.arange(1, NSA_HEADS + 1, dtype=F32) / NSA_HEADS) * LOG2E
    slopes = jnp.broadcast_to(
        jnp.pad(slopes_h.reshape(NSA_KV_GROUPS, NSA_HPG), ((0, 0), (0, 8 - NSA_HPG)))[:, :, None],
        (NSA_KV_GROUPS, 8, LANES))
    ovt_np = np.zeros((LANES, nblk), np.float32)
    ovt_np[:ns, :n_cmp] = _overlap_matrix(n_cmp, ns).T
    ovt = jnp.asarray(ovt_np, BF16)
    ext_np = np.zeros((seq, LANES), np.float32)
    ext_np[np.arange(seq), np.arange(seq) // SEL_BLOCK] = 1.0
    ext = jnp.asarray(ext_np, BF16)

    w_in_t = jnp.swapaxes(w_in, 1, 2)

    xf = x.reshape(m, d)
    for l in range(depth):
        wt = w_in_t[l]
        row = dict(zip(("fq", "fk", "fv", "fl", "nq", "kc", "vc", "ks", "vs", "kw", "vw", "ng",
                        "ga", "gb"), pts))

        gain1 = jnp.concatenate([jnp.tile(fox_q_gain[l] * (FOX_HEAD_DIM ** -0.5 * LOG2E), FOX_HEADS),
                                 jnp.tile(fox_k_gain[l], FOX_HEADS),
                                 jnp.ones((FOX_WIDTH,), F32)])
        flag1 = jnp.concatenate([jnp.ones((2 * FOX_WIDTH,), F32), jnp.zeros((FOX_WIDTH,), F32)])
        gain_q = jnp.tile(_pad_gain(nsa_q_gain[l], NSA_QK_DIM ** -0.5 * LOG2E), NSA_HEADS)
        gain_k = jnp.concatenate([jnp.tile(_pad_gain(nsa_ks_gain[l]), NSA_KV_GROUPS),
                                  jnp.tile(_pad_gain(nsa_kw_gain[l]), NSA_KV_GROUPS)])
        n_small = FOX_HEADS + 3 * NSA_HEADS
        w3 = jnp.concatenate([_pad_head_rows(wt[row["kc"]:row["vc"]], NSA_KV_GROUPS),
                              wt[row["vc"]:row["ks"]], wt[row["fl"]:row["nq"]],
                              wt[row["ng"]:row["ga"]],
                              jnp.zeros((LANES - n_small, d), F32)], axis=0)

        xn = _rmsnorm(xf, norm_attn[l])
        q_tile = NSA_HPG * NSA_QK_DIM
        p1 = _project(xn, wt, [t * 512 for t in range(3 * FOX_WIDTH // 512)], 512, gain1, flag1,
                      BF16, 1024, "proj_fox", group=128, count=128)
        pq = _project(xn, wt, [row["nq"], row["nq"] + q_tile], q_tile, gain_q,
                      jnp.ones_like(gain_q), BF16, 1024, "proj_nsa_q", pad_from=NSA_QK_DIM,
                      pad_to=NSA_QK_PAD, group=NSA_QK_PAD, count=NSA_QK_DIM)
        pk = _project(xn, wt, [row["ks"], row["kw"]], KV_K, gain_k, jnp.ones_like(gain_k), BF16,
                      1024, "proj_nsa_k", pad_from=NSA_QK_DIM, pad_to=NSA_QK_PAD,
                      group=NSA_QK_PAD, count=NSA_QK_DIM)
        pv = _project(xn, wt, [row["vs"], row["vw"]], KV_V, None, None, BF16, 1024, "proj_nsa_v")
        p3 = _project(xn, w3, [0], w3.shape[0], None, None, F32, 1024, "proj_f32")

        bias_row = jnp.pad(fox_f_bias[l], (0, LANES - FOX_HEADS)).reshape(1, LANES)
        crep, crow = _decay(p3, bias_row, batch, seq)
        o_a = _fox_attention(p1, crep, crow.reshape(batch, FOX_HEADS, 1, seq), batch, seq)

        pad_d = NSA_QK_PAD - NSA_QK_DIM
        pek = jnp.pad(cmp_pe_k[l], ((0, 0), (0, pad_d)))
        w1k = jnp.pad(cmp_w1_k[l].reshape(CMP_BLOCK, NSA_QK_DIM, CMP_HIDDEN),
                      ((0, 0), (0, pad_d), (0, 0))).astype(BF16)
        w2k = jnp.pad(cmp_w2_k[l], ((0, 0), (0, pad_d))).astype(BF16)
        gk = _pad_gain(nsa_kc_gain[l]).reshape(1, NSA_QK_PAD)
        w1v = cmp_w1_v[l].reshape(CMP_BLOCK, NSA_V_DIM, CMP_HIDDEN).astype(BF16)
        w2v = cmp_w2_v[l].astype(BF16)
        kc, vct = _compress(p3, pek, w1k, w2k, gk, cmp_pe_v[l], w1v, w2v, batch, seq)
        sel, ocmp = _nsa_select(pq, p3, kc, vct, slopes, ovt, batch, seq)
        o_b = _nsa_attend(pq, pk, pv, p3, sel, ocmp, slopes, ext, batch, seq)

        merged = _merge(xn, o_a, o_b, wt, row["ga"], row["gb"], w_up_fox[l], w_up_nsa[l])
        hres, hn = _out_proj_norm(merged, w_out[l], xf, norm_ffn[l])

        act = _swiglu(hn, w_ffn_gate[l], w_ffn_up[l])
        xf = _matmul_residual(act, w_ffn_down[l], hres, 512, 512, "ffn_down")
    return xf.reshape(batch, seq, d)
```

```python
import functools

import numpy as np
import jax
import jax.numpy as jnp
from jax import lax
from jax.experimental import pallas as pl
from jax.experimental.pallas import tpu as pltpu

F32 = jnp.float32
BF16 = jnp.bfloat16

D_MODEL = 2048
FOX_HEADS = 8
FOX_HEAD_DIM = 128
FOX_WIDTH = FOX_HEADS * FOX_HEAD_DIM
NSA_HEADS = 8
NSA_KV_GROUPS = 2
NSA_HPG = NSA_HEADS // NSA_KV_GROUPS
NSA_QK_DIM = 192
NSA_QK_PAD = 256
NSA_V_DIM = 128
NSA_WIDTH = NSA_HEADS * NSA_V_DIM
CMP_BLOCK = 32
CMP_STRIDE = 16
CMP_HIDDEN = 256
SEL_BLOCK = 64
SEL_TOPK = 16
SEL_LOCAL = 2
FORCE_SCORE = 1.0e4
WINDOW = 512
KV_K = NSA_KV_GROUPS * NSA_QK_DIM
KV_V = NSA_KV_GROUPS * NSA_V_DIM
D_FF = -(-(8 * D_MODEL) // (3 * 256)) * 256
RMS_EPS = 1e-6
IN_SPLITS = (FOX_WIDTH, FOX_WIDTH, FOX_WIDTH, FOX_HEADS,
             NSA_HEADS * NSA_QK_DIM, KV_K, KV_V, KV_K, KV_V, KV_K, KV_V,
             3 * NSA_HEADS, D_MODEL, D_MODEL)

LANES = 128
SUBLANES = 8
BF16_SUBLANES = 16
NEG = -1.0e30
LOG2E = 1.4426950408889634
SLC_TILE = 512
VMEM_LIMIT = 56 * 1024 * 1024


def _cparams(sem):
    return pltpu.CompilerParams(dimension_semantics=sem, vmem_limit_bytes=VMEM_LIMIT)


def _rms_kernel(x_ref, g_ref, o_ref):
    x = x_ref[...]
    ms = jnp.mean(x * x, axis=-1, keepdims=True)
    o_ref[...] = (x * lax.rsqrt(ms + RMS_EPS) * g_ref[...]).astype(o_ref.dtype)


def _rmsnorm(x, gain, tm=512):
    m, d = x.shape
    return pl.pallas_call(
        _rms_kernel,
        grid=(m // tm,),
        in_specs=[pl.BlockSpec((tm, d), lambda i: (i, 0)),
                  pl.BlockSpec((1, d), lambda i: (0, 0))],
        out_specs=pl.BlockSpec((tm, d), lambda i: (i, 0)),
        out_shape=jax.ShapeDtypeStruct((m, d), BF16),
        compiler_params=_cparams(("arbitrary",)),
        name="rmsnorm",
    )(x, gain.reshape(1, d))


STAGE_CHUNK = 512
EPILOGUE_SPLIT = 4


def _stage_weight(w_ref, wb_ref, transposed=False, pad_from=0, pad_to=0):
    if not transposed:
        wb_ref[...] = w_ref[...].astype(BF16)
        return
    n_in, k = w_ref.shape
    for c in range(k // STAGE_CHUNK):
        cols = slice(c * STAGE_CHUNK, (c + 1) * STAGE_CHUNK)
        w = w_ref[:, cols]
        if pad_from != pad_to:
            zero = jnp.zeros((pad_to - pad_from, STAGE_CHUNK), F32)
            w = jnp.concatenate(
                [piece for h in range(n_in // pad_from)
                 for piece in (w[h * pad_from:(h + 1) * pad_from, :], zero)], axis=0)
        wb_ref[cols, :] = w.T.astype(BF16)


def _first_m_step():
    return pl.program_id(1) == 0


def _proj_kernel(a_ref, w_ref, gain_ref, flag_ref, o_ref, wb_ref, *, pad_from, pad_to,
                 group, count):
    @pl.when(_first_m_step())
    def _():
        _stage_weight(w_ref, wb_ref, True, pad_from, pad_to)

    if group == 0:
        o_ref[...] = jnp.dot(a_ref[...], wb_ref[...],
                             preferred_element_type=F32).astype(o_ref.dtype)
        return
    rows = a_ref.shape[0] // EPILOGUE_SPLIT
    ys = [jnp.dot(a_ref[r * rows:(r + 1) * rows, :], wb_ref[...], preferred_element_type=F32)
          for r in range(EPILOGUE_SPLIT)]
    for r, y in enumerate(ys):
        for c in range(y.shape[1] // group):
            sl = slice(c * group, (c + 1) * group)
            yc = y[:, sl]
            ss = jnp.sum(yc * yc, axis=-1, keepdims=True)
            rs = lax.rsqrt(ss * (1.0 / count) + RMS_EPS)
            scale = jnp.where(flag_ref[:, sl] > 0.0, rs, 1.0)
            o_ref[r * rows:(r + 1) * rows, sl] = (
                yc * scale * gain_ref[:, sl]).astype(o_ref.dtype)


def _row_window(starts, rows, k):
    def index(j, i):
        start = starts[-1]
        for t in range(len(starts) - 2, -1, -1):
            start = jnp.where(j == t, starts[t], start)
        return pl.multiple_of(start, SUBLANES), 0
    assert all(s % SUBLANES == 0 for s in starts)
    return pl.BlockSpec((pl.Element(rows), pl.Element(k)), index)


def _project(a, wt, starts, tn_in, gain, flag, out_dtype, tm, name,
             pad_from=0, pad_to=0, group=0, count=1):
    m, k = a.shape
    n_tiles = len(starts)
    tn_out = tn_in if pad_from == pad_to else tn_in // pad_from * pad_to
    n_out = n_tiles * tn_out
    if gain is None:
        gain = jnp.ones((n_out,), F32)
        flag = jnp.zeros((n_out,), F32)
    return pl.pallas_call(
        functools.partial(_proj_kernel, pad_from=pad_from, pad_to=pad_to, group=group,
                          count=count),
        grid=(n_tiles, m // tm),
        in_specs=[pl.BlockSpec((tm, k), lambda j, i: (i, 0)),
                  _row_window(starts, tn_in, k),
                  pl.BlockSpec((1, tn_out), lambda j, i: (0, j)),
                  pl.BlockSpec((1, tn_out), lambda j, i: (0, j))],
        out_specs=pl.BlockSpec((tm, tn_out), lambda j, i: (i, j)),
        out_shape=jax.ShapeDtypeStruct((m, n_out), out_dtype),
        scratch_shapes=[pltpu.VMEM((k, tn_out), BF16)],
        compiler_params=_cparams(("arbitrary", "arbitrary")),
        name=name,
    )(a, wt, gain.reshape(1, n_out), flag.reshape(1, n_out))


def _out_norm_kernel(a_ref, w_ref, r_ref, g_ref, h_ref, hn_ref, wb_ref):
    @pl.when(pl.program_id(0) == 0)
    def _():
        _stage_weight(w_ref, wb_ref)

    h = r_ref[...] + jnp.dot(a_ref[...], wb_ref[...], preferred_element_type=F32)
    h_ref[...] = h
    ms = jnp.mean(h * h, axis=-1, keepdims=True)
    hn_ref[...] = (h * lax.rsqrt(ms + RMS_EPS) * g_ref[...]).astype(hn_ref.dtype)


def _out_proj_norm(a, w, res, gain, tm=512):
    m, k = a.shape
    n = w.shape[1]
    full = lambda i: (0, 0)
    row = lambda i: (i, 0)
    return pl.pallas_call(
        _out_norm_kernel,
        grid=(m // tm,),
        in_specs=[pl.BlockSpec((tm, k), row),
                  pl.BlockSpec((k, n), full, pipeline_mode=pl.Buffered(1)),
                  pl.BlockSpec((tm, n), row), pl.BlockSpec((1, n), full)],
        out_specs=[pl.BlockSpec((tm, n), row), pl.BlockSpec((tm, n), row)],
        out_shape=[jax.ShapeDtypeStruct((m, n), F32), jax.ShapeDtypeStruct((m, n), BF16)],
        scratch_shapes=[pltpu.VMEM((k, n), BF16)],
        compiler_params=_cparams(("arbitrary",)),
        name="out_proj_norm",
    )(a, w, res, gain.reshape(1, n))


def _mm_res_kernel(a_ref, w_ref, r_ref, o_ref, wb_ref):
    @pl.when(_first_m_step())
    def _():
        _stage_weight(w_ref, wb_ref)

    o_ref[...] = r_ref[...] + jnp.dot(a_ref[...], wb_ref[...], preferred_element_type=F32)


def _matmul_residual(a, w, res, tm, tn, name):
    m, k = a.shape
    n = w.shape[1]
    return pl.pallas_call(
        _mm_res_kernel,
        grid=(n // tn, m // tm),
        in_specs=[pl.BlockSpec((tm, k), lambda j, i: (i, 0)),
                  pl.BlockSpec((k, tn), lambda j, i: (0, j)),
                  pl.BlockSpec((tm, tn), lambda j, i: (i, j))],
        out_specs=pl.BlockSpec((tm, tn), lambda j, i: (i, j)),
        out_shape=jax.ShapeDtypeStruct((m, n), F32),
        scratch_shapes=[pltpu.VMEM((k, tn), BF16)],
        compiler_params=_cparams(("arbitrary", "arbitrary")),
        name=name,
    )(a, w, res)


def _merge_kernel(xn_ref, oa_ref, ob_ref, wga_ref, wgb_ref, wuf_ref, wun_ref, o_ref,
                  bga_ref, bgb_ref, buf_ref, bun_ref):
    @pl.when(_first_m_step())
    def _():
        _stage_weight(wga_ref, bga_ref, True)
        _stage_weight(wgb_ref, bgb_ref, True)
        _stage_weight(wuf_ref, buf_ref)
        _stage_weight(wun_ref, bun_ref)

    xn = xn_ref[...]
    ga = jax.nn.sigmoid(jnp.dot(xn, bga_ref[...], preferred_element_type=F32))
    ua = jnp.dot(oa_ref[...], buf_ref[...], preferred_element_type=F32)
    acc = ga * ua
    gb = jax.nn.sigmoid(jnp.dot(xn, bgb_ref[...], preferred_element_type=F32))
    ub = jnp.dot(ob_ref[...], bun_ref[...], preferred_element_type=F32)
    o_ref[...] = (acc + gb * ub).astype(o_ref.dtype)


def _merge(xn, oa, ob, wt, row_a, row_b, wuf, wun, tm=512, tn=512):
    m, d = xn.shape
    n = wuf.shape[1]
    ka = oa.shape[1]
    kb = ob.shape[1]
    row = lambda j, i: (i, 0)
    col = lambda j, i: (0, j)
    return pl.pallas_call(
        _merge_kernel,
        grid=(n // tn, m // tm),
        in_specs=[pl.BlockSpec((tm, d), row), pl.BlockSpec((tm, ka), row),
                  pl.BlockSpec((tm, kb), row),
                  _row_window([row_a + t * tn for t in range(n // tn)], tn, d),
                  _row_window([row_b + t * tn for t in range(n // tn)], tn, d),
                  pl.BlockSpec((ka, tn), col), pl.BlockSpec((kb, tn), col)],
        out_specs=pl.BlockSpec((tm, tn), lambda j, i: (i, j)),
        out_shape=jax.ShapeDtypeStruct((m, n), BF16),
        scratch_shapes=[pltpu.VMEM((d, tn), BF16), pltpu.VMEM((d, tn), BF16),
                        pltpu.VMEM((ka, tn), BF16), pltpu.VMEM((kb, tn), BF16)],
        compiler_params=_cparams(("arbitrary", "arbitrary")),
        name="gated_merge",
    )(xn, oa, ob, wt, wt, wuf, wun)


def _swiglu_kernel(a_ref, wg_ref, wu_ref, o_ref, bg_ref, bu_ref):
    @pl.when(_first_m_step())
    def _():
        _stage_weight(wg_ref, bg_ref)
        _stage_weight(wu_ref, bu_ref)

    a = a_ref[...]
    gt = jnp.dot(a, bg_ref[...], preferred_element_type=F32)
    up = jnp.dot(a, bu_ref[...], preferred_element_type=F32)
    o_ref[...] = (gt * jax.nn.sigmoid(gt) * up).astype(o_ref.dtype)


def _swiglu(a, wg, wu, tm=1024, tn=512):
    m, k = a.shape
    n = wg.shape[1]
    return pl.pallas_call(
        _swiglu_kernel,
        grid=(n // tn, m // tm),
        in_specs=[pl.BlockSpec((tm, k), lambda j, i: (i, 0)),
                  pl.BlockSpec((k, tn), lambda j, i: (0, j)),
                  pl.BlockSpec((k, tn), lambda j, i: (0, j))],
        out_specs=pl.BlockSpec((tm, tn), lambda j, i: (i, j)),
        out_shape=jax.ShapeDtypeStruct((m, n), BF16),
        scratch_shapes=[pltpu.VMEM((k, tn), BF16), pltpu.VMEM((k, tn), BF16)],
        compiler_params=_cparams(("arbitrary", "arbitrary")),
        name="swiglu_up",
    )(a, wg, wu)


def _split3(x):
    hi = x.astype(BF16)
    r1 = x - hi.astype(F32)
    mid = r1.astype(BF16)
    lo = (r1 - mid.astype(F32)).astype(BF16)
    return hi, mid, lo


def _decay_kernel(z_ref, b_ref, crep_ref, crow_ref, *, blk):
    t = z_ref.shape[0]
    r = lax.broadcasted_iota(jnp.int32, (blk, blk), 0)
    c = lax.broadcasted_iota(jnp.int32, (blk, blk), 1)
    tri = jnp.where(r >= c, 1.0, 0.0).astype(BF16)
    lane = lax.broadcasted_iota(jnp.int32, (blk, LANES), 1)
    carry = jnp.zeros((1, LANES), F32)
    for s in range(t // blk):
        rows = slice(s * blk, (s + 1) * blk)
        z = z_ref[rows, :] + b_ref[...]
        logf = (jnp.minimum(z, 0.0) - jnp.log1p(jnp.exp(-jnp.abs(z)))) * LOG2E
        hi, mid, lo = _split3(logf)
        cb = (jnp.dot(tri, hi, preferred_element_type=F32)
              + jnp.dot(tri, mid, preferred_element_type=F32)
              + jnp.dot(tri, lo, preferred_element_type=F32)) + carry
        carry = cb[blk - 1:blk, :]
        crow_ref[0, :, rows] = cb.T[:FOX_HEADS, :]
        for h in range(FOX_HEADS):
            col = jnp.sum(jnp.where(lane == h, cb, 0.0), axis=-1, keepdims=True)
            crep_ref[0, h, rows, :] = jnp.broadcast_to(col, (blk, LANES))


def _decay(p3, bias_row, batch, seq, blk=256):
    return pl.pallas_call(
        functools.partial(_decay_kernel, blk=blk),
        grid=(batch,),
        in_specs=[pl.BlockSpec((seq, LANES), lambda b: (b, 6)),
                  pl.BlockSpec((1, LANES), lambda b: (0, 0))],
        out_specs=[pl.BlockSpec((1, FOX_HEADS, seq, LANES), lambda b: (b, 0, 0, 0)),
                   pl.BlockSpec((1, FOX_HEADS, seq), lambda b: (b, 0, 0))],
        out_shape=[jax.ShapeDtypeStruct((batch, FOX_HEADS, seq, LANES), F32),
                   jax.ShapeDtypeStruct((batch, FOX_HEADS, seq), F32)],
        compiler_params=_cparams(("arbitrary",)),
        name="fox_decay_cumsum",
    )(p3, bias_row)


def _transpose_bf16(x):
    return x.astype(F32).T.astype(BF16)


def _normalised(acc, d):
    return acc[:d, :] * (1.0 / jnp.maximum(acc[d:d + 1, :], 1e-30))


def _fox_kernel(q_ref, k_ref, v_ref, crep_ref, crow_ref, o_ref, vt_ref, *, tq, nq, nh):
    i = pl.program_id(2)
    dh = FOX_HEAD_DIM

    @pl.when(i == 0)
    def _():
        for h in range(nh):
            vt_ref[h, 0:dh, :] = _transpose_bf16(v_ref[:, h * dh:(h + 1) * dh])
            vt_ref[h, dh:, :] = jnp.ones((vt_ref.shape[1] - dh, vt_ref.shape[2]), BF16)

    qts = [_transpose_bf16(q_ref[:, h * dh:(h + 1) * dh]) for h in range(nh)]
    cis = [crow_ref[0, h] for h in range(nh)]
    reps = tq // LANES
    rk = lax.broadcasted_iota(jnp.int32, (tq, tq), 0)
    cq = lax.broadcasted_iota(jnp.int32, (tq, tq), 1)

    def scores(h, k0):
        cj = crep_ref[0, h, k0:k0 + tq, :]
        st = jnp.dot(k_ref[k0:k0 + tq, h * dh:(h + 1) * dh], qts[h],
                     preferred_element_type=F32)
        return st + cis[h] - jnp.concatenate([cj] * reps, axis=1)

    def variant(n):
        def run():
            starts = [t * tq for t in range(n, -1, -1)]
            tiles = [[jnp.where(rk <= cq, scores(h, starts[0]), NEG)]
                     + [scores(h, k0) for k0 in starts[1:]] for h in range(nh)]
            ms = [jnp.max(tiles[h][0], axis=0, keepdims=True) for h in range(nh)]
            accs = [jnp.dot(vt_ref[h, :, starts[0]:starts[0] + tq],
                            jnp.exp2(tiles[h][0] - ms[h]).astype(BF16),
                            preferred_element_type=F32) for h in range(nh)]
            for t in range(1, n + 1):
                k0 = starts[t]
                for h in range(nh):
                    st = tiles[h][t]
                    m_new = jnp.maximum(ms[h], jnp.max(st, axis=0, keepdims=True))
                    accs[h] = jnp.exp2(ms[h] - m_new) * accs[h] + jnp.dot(
                        vt_ref[h, :, k0:k0 + tq], jnp.exp2(st - m_new).astype(BF16),
                        preferred_element_type=F32)
                    ms[h] = m_new
            for h in range(nh):
                o_ref[:, h * dh:(h + 1) * dh] = _normalised(accs[h], dh).T.astype(o_ref.dtype)
        return run

    lax.switch(i, [variant(n) for n in range(nq)])


def _fox_attention(p1, crep, crow4, batch, seq, tq=512, nh=4):
    nq = seq // tq
    hg = FOX_HEADS // nh
    w = nh * FOX_HEAD_DIM
    return pl.pallas_call(
        functools.partial(_fox_kernel, tq=tq, nq=nq, nh=nh),
        grid=(batch, hg, nq),
        in_specs=[pl.BlockSpec((tq, w), lambda b, hh, i: (b * nq + i, hh)),
                  pl.BlockSpec((seq, w), lambda b, hh, i: (b, hg + hh)),
                  pl.BlockSpec((seq, w), lambda b, hh, i: (b, 2 * hg + hh)),
                  pl.BlockSpec((1, nh, seq, LANES), lambda b, hh, i: (b, hh, 0, 0)),
                  pl.BlockSpec((1, nh, 1, tq), lambda b, hh, i: (b, hh, 0, i))],
        out_specs=pl.BlockSpec((tq, w), lambda b, hh, i: (b * nq + i, hh)),
        out_shape=jax.ShapeDtypeStruct((batch * seq, FOX_WIDTH), BF16),
        scratch_shapes=[pltpu.VMEM((nh, FOX_HEAD_DIM + BF16_SUBLANES, seq), BF16)],
        compiler_params=_cparams(("arbitrary", "arbitrary", "arbitrary")),
        name="fox_attention",
    )(p1, p1, p1, crep, crow4)


def _compress_one(z_refs, pe_ref, w1_ref, w2_ref, nblk):
    half = CMP_BLOCK // 2
    first = jnp.zeros((nblk, CMP_HIDDEN), F32)
    second = jnp.zeros((nblk, CMP_HIDDEN), F32)
    for p in range(half):
        rows = pl.ds(p, nblk, stride=CMP_STRIDE)
        zp = [z_ref[rows, :] for z_ref in z_refs]
        zp = zp[0] if len(zp) == 1 else jnp.concatenate(zp, axis=1)
        first += jnp.dot((zp + pe_ref[p:p + 1, :]).astype(BF16), w1_ref[p],
                         preferred_element_type=F32)
        second += jnp.dot((zp + pe_ref[half + p:half + p + 1, :]).astype(BF16),
                          w1_ref[half + p], preferred_element_type=F32)
    hid = first + pltpu.roll(second, nblk - 1, 0)
    act = (hid * jax.nn.sigmoid(hid)).astype(BF16)
    return jnp.dot(act, w2_ref[...], preferred_element_type=F32)


def _compress_kernel(zk0_ref, zk1_ref, zv_ref, pek_ref, w1k_ref, w2k_ref, gk_ref,
                     pev_ref, w1v_ref, w2v_ref, kc_ref, vc_ref, *, nblk):
    kc = _compress_one((zk0_ref, zk1_ref), pek_ref, w1k_ref, w2k_ref, nblk)
    ms = jnp.sum(kc * kc, axis=-1, keepdims=True) * (1.0 / NSA_QK_DIM)
    kc_ref[...] = (kc * lax.rsqrt(ms + RMS_EPS) * gk_ref[...]).astype(kc_ref.dtype)
    vc = _compress_one((zv_ref,), pev_ref, w1v_ref, w2v_ref, nblk)
    vc_ref[...] = vc.T.astype(vc_ref.dtype)


def _compress(p3, pek, w1k, w2k, gk, pev, w1v, w2v, batch, seq):
    g = NSA_KV_GROUPS
    nblk = seq // CMP_STRIDE
    full2 = lambda b, gg: (0, 0)
    full3 = lambda b, gg: (0, 0, 0)
    return pl.pallas_call(
        functools.partial(_compress_kernel, nblk=nblk),
        grid=(batch, g),
        in_specs=[pl.BlockSpec((seq, LANES), lambda b, gg: (b, 2 * gg)),
                  pl.BlockSpec((seq, LANES), lambda b, gg: (b, 2 * gg + 1)),
                  pl.BlockSpec((seq, NSA_V_DIM), lambda b, gg: (b, 4 + gg)),
                  pl.BlockSpec(pek.shape, full2), pl.BlockSpec(w1k.shape, full3),
                  pl.BlockSpec(w2k.shape, full2), pl.BlockSpec(gk.shape, full2),
                  pl.BlockSpec(pev.shape, full2), pl.BlockSpec(w1v.shape, full3),
                  pl.BlockSpec(w2v.shape, full2)],
        out_specs=[pl.BlockSpec((nblk, NSA_QK_PAD), lambda b, gg: (b * g + gg, 0)),
                   pl.BlockSpec((NSA_V_DIM, nblk), lambda b, gg: (b * g + gg, 0))],
        out_shape=[jax.ShapeDtypeStruct((batch * g * nblk, NSA_QK_PAD), BF16),
                   jax.ShapeDtypeStruct((batch * g * NSA_V_DIM, nblk), BF16)],
        compiler_params=_cparams(("arbitrary", "arbitrary")),
        name="nsa_compress",
    )(p3, p3, p3, pek, w1k, w2k, gk, pev, w1v, w2v)


def _q_heads_t(q_ref):
    qb = q_ref[...]
    return jnp.concatenate(
        [_transpose_bf16(qb[:, hh * NSA_QK_PAD:(hh + 1) * NSA_QK_PAD]) for hh in range(NSA_HPG)],
        axis=1)


def _gate_rows(glt_ref, g, hh):
    base = FOX_HEADS + (g * NSA_HPG + hh) * 3
    return [glt_ref[pl.ds(base + br, 1), :] for br in range(3)]


def _nsa_select_kernel(q_ref, kc_ref, vct_ref, gl_ref, slope_ref, ovt_ref,
                       sel_ref, ocmp_ref, glt_ref, *, tq, n_cmp, n_sel):
    g = pl.program_id(1)
    t0 = pl.program_id(2) * tq
    hpg = NSA_HPG
    dv = NSA_V_DIM
    q4t = _q_heads_t(q_ref)
    slopes = [slope_ref[0, hh:hh + 1, :] for hh in range(hpg)]
    nrep = tq // LANES
    rk = lax.broadcasted_iota(jnp.int32, (LANES, tq), 0)
    cq = lax.broadcasted_iota(jnp.int32, (LANES, tq), 1)

    s_c = jnp.dot(kc_ref[...], q4t, preferred_element_type=F32)
    dist_c = (t0 + cq) - (CMP_STRIDE * rk + (CMP_BLOCK - 1))
    mask_c = jnp.where(rk < n_cmp, dist_c, -1) >= 0
    dist_cf = dist_c.astype(F32)
    probs = []
    p_sum = jnp.zeros((LANES, tq), F32)
    for hh in range(hpg):
        slope = jnp.concatenate([slopes[hh]] * nrep, axis=1)
        sh = s_c[:, hh * tq:(hh + 1) * tq] - slope * dist_cf
        sm = jnp.where(mask_c, sh, NEG)
        m = jnp.max(sm, axis=0, keepdims=True)
        e = jnp.where(mask_c, jnp.exp2(sm - m), 0.0)
        p = e * (1.0 / jnp.maximum(jnp.sum(e, axis=0, keepdims=True), 1e-30))
        probs.append(p)
        p_sum = p_sum + p
    o_cmp = jnp.dot(vct_ref[...], jnp.concatenate(probs, axis=1).astype(BF16),
                    preferred_element_type=F32)

    ph = p_sum.astype(BF16)
    plo = (p_sum - ph.astype(F32)).astype(BF16)
    ovt = ovt_ref[...]
    imp = (jnp.dot(ovt, ph, preferred_element_type=F32)
           + jnp.dot(ovt, plo, preferred_element_type=F32))[:n_sel, :]

    rj = lax.broadcasted_iota(jnp.int32, (n_sel, tq), 0)
    tcol = t0 + lax.broadcasted_iota(jnp.int32, (n_sel, tq), 1)
    back = (tcol >> (SEL_BLOCK.bit_length() - 1)) - rj
    elig = back >= 0
    forced = jnp.where(rj == 0, 0, jnp.where(elig, back, SEL_LOCAL)) < SEL_LOCAL
    score = jnp.where(elig, jnp.where(forced, FORCE_SCORE, imp), -1.0)
    rank = jnp.zeros((n_sel, tq), F32)
    for jp in range(n_sel):
        row = score[jp:jp + 1, :]
        later = jnp.where(rj > jp, 1.0, 0.0)
        rank = rank + jnp.where(row > score, 1.0, jnp.where(row == score, later, 0.0))
    sel = jnp.where(elig, jnp.where(rank < SEL_TOPK, 1.0, 0.0), 0.0)
    sel_ref[...] = jnp.concatenate(
        [sel, jnp.zeros((LANES - n_sel, tq), F32)], axis=0).astype(sel_ref.dtype)

    glt_ref[...] = jax.nn.sigmoid(gl_ref[...]).T
    for hh in range(hpg):
        gate = _gate_rows(glt_ref, g, hh)[0]
        ocmp_ref[:, hh * dv:(hh + 1) * dv] = (gate * o_cmp[:, hh * tq:(hh + 1) * tq]).T


def _nsa_select(pq, p3, kc, vct, slopes, ovt, batch, seq, tq=512):
    g = NSA_KV_GROUPS
    nq = seq // tq
    nblk = seq // CMP_STRIDE
    n_cmp = nblk - CMP_BLOCK // CMP_STRIDE + 1
    return pl.pallas_call(
        functools.partial(_nsa_select_kernel, tq=tq, n_cmp=n_cmp, n_sel=seq // SEL_BLOCK),
        grid=(batch, g, nq),
        in_specs=[
            pl.BlockSpec((tq, NSA_HPG * NSA_QK_PAD), lambda b, gg, i: (b * nq + i, gg)),
            pl.BlockSpec((nblk, NSA_QK_PAD), lambda b, gg, i: (b * g + gg, 0)),
            pl.BlockSpec((NSA_V_DIM, nblk), lambda b, gg, i: (b * g + gg, 0)),
            pl.BlockSpec((tq, LANES), lambda b, gg, i: (b * nq + i, 6)),
            pl.BlockSpec((1, 8, LANES), lambda b, gg, i: (gg, 0, 0)),
            pl.BlockSpec(ovt.shape, lambda b, gg, i: (0, 0)),
        ],
        out_specs=[pl.BlockSpec((LANES, tq), lambda b, gg, i: (b * g + gg, i)),
                   pl.BlockSpec((tq, NSA_HPG * NSA_V_DIM), lambda b, gg, i: (b * nq + i, gg))],
        out_shape=[jax.ShapeDtypeStruct((batch * g * LANES, seq), BF16),
                   jax.ShapeDtypeStruct((batch * seq, NSA_WIDTH), F32)],
        scratch_shapes=[pltpu.VMEM((LANES, tq), F32)],
        compiler_params=_cparams(("arbitrary", "arbitrary", "arbitrary")),
        name="nsa_select",
    )(pq, kc, vct, p3, slopes, ovt)


def _nsa_attend_kernel(q_ref, ks_ref, vs_ref, kw_ref, vw_ref, gl_ref, slope_ref, sel_ref,
                       ext_ref, ocmp_ref, o_ref, vst_ref, vwt_ref, glt_ref, *, tq, n_var):
    i = pl.program_id(1)
    t0 = i * tq
    hpg = NSA_HPG
    ng = NSA_KV_GROUPS
    dv = NSA_V_DIM
    dk = NSA_QK_PAD

    @pl.when(i == 0)
    def _():
        ones = jnp.ones((vst_ref.shape[1] - dv, vst_ref.shape[2]), BF16)
        for gg in range(ng):
            vst_ref[gg, 0:dv, :] = _transpose_bf16(vs_ref[:, gg * dv:(gg + 1) * dv])
            vst_ref[gg, dv:, :] = ones
            vwt_ref[gg, 0:dv, :] = _transpose_bf16(vw_ref[:, gg * dv:(gg + 1) * dv])
            vwt_ref[gg, dv:, :] = ones

    qb = q_ref[...]
    q4t = [jnp.concatenate(
        [_transpose_bf16(qb[:, (gg * hpg + hh) * dk:(gg * hpg + hh + 1) * dk])
         for hh in range(hpg)], axis=1) for gg in range(ng)]
    slopes = [[slope_ref[gg, hh:hh + 1, :] for hh in range(hpg)] for gg in range(ng)]
    sels = [sel_ref[gg * LANES:(gg + 1) * LANES, :] for gg in range(ng)]
    glt_ref[...] = jax.nn.sigmoid(gl_ref[...]).T

    def scores(gg, k, k0, rows, mask_fn):
        st = jnp.dot(k, q4t[gg], preferred_element_type=F32)
        dist = (lax.broadcasted_iota(jnp.int32, (rows, tq), 1)
                - lax.broadcasted_iota(jnp.int32, (rows, tq), 0)) + (t0 - k0)
        mask = mask_fn(dist)
        distf = dist.astype(F32)
        return st + jnp.concatenate(
            [jnp.where(mask, -slopes[gg][hh] * distf, NEG) for hh in range(hpg)], axis=1)

    def variant(n):
        def run():
            kw0 = pl.multiple_of(jnp.maximum(t0 - WINDOW, 0), LANES)
            wrows = WINDOW + tq
            o_win = []
            for gg in range(ng):
                sw = scores(gg, kw_ref[pl.ds(kw0, wrows), gg * dk:(gg + 1) * dk], kw0, wrows,
                            lambda dist: jnp.where(dist >= 0, dist, WINDOW) < WINDOW)
                pw = jnp.exp2(sw - jnp.max(sw, axis=0, keepdims=True))
                o_win.append(_normalised(
                    jnp.dot(vwt_ref[gg, :, pl.ds(kw0, wrows)], pw.astype(BF16),
                            preferred_element_type=F32), dv))

            starts = [t * SLC_TILE for t in range(n, -1, -1)]
            tiles = [[] for _ in range(ng)]
            for k0 in starts:
                for gg in range(ng):
                    selx = jnp.dot(ext_ref[k0:k0 + SLC_TILE, :], sels[gg],
                                   preferred_element_type=F32)
                    tiles[gg].append(scores(
                        gg, ks_ref[k0:k0 + SLC_TILE, gg * dk:(gg + 1) * dk], k0, SLC_TILE,
                        lambda dist, selx=selx: jnp.where(dist >= 0, selx, 0.0) > 0.5))
            ms = [jnp.max(tiles[gg][0], axis=0, keepdims=True) for gg in range(ng)]
            accs = [jnp.dot(vst_ref[gg, :, starts[0]:starts[0] + SLC_TILE],
                            jnp.exp2(tiles[gg][0] - ms[gg]).astype(BF16),
                            preferred_element_type=F32) for gg in range(ng)]
            for t in range(1, n + 1):
                k0 = starts[t]
                for gg in range(ng):
                    st = tiles[gg][t]
                    m_new = jnp.maximum(ms[gg], jnp.max(st, axis=0, keepdims=True))
                    accs[gg] = jnp.exp2(ms[gg] - m_new) * accs[gg] + jnp.dot(
                        vst_ref[gg, :, k0:k0 + SLC_TILE], jnp.exp2(st - m_new).astype(BF16),
                        preferred_element_type=F32)
                    ms[gg] = m_new

            for gg in range(ng):
                o_slc = _normalised(accs[gg], dv)
                for hh in range(hpg):
                    _, g_slc, g_win = _gate_rows(glt_ref, gg, hh)
                    lanes = slice(hh * tq, (hh + 1) * tq)
                    out = g_slc * o_slc[:, lanes] + g_win * o_win[gg][:, lanes]
                    cols = slice((gg * hpg + hh) * dv, (gg * hpg + hh + 1) * dv)
                    o_ref[:, cols] = (ocmp_ref[:, cols] + out.T).astype(o_ref.dtype)
        return run

    lax.switch(t0 // SLC_TILE, [variant(n) for n in range(n_var)])


def _nsa_attend(pq, pk, pv, p3, sel, ocmp, slopes, ext, batch, seq):
    tq = LANES
    g = NSA_KV_GROUPS
    nq = seq // tq
    vrows = NSA_V_DIM + BF16_SUBLANES
    return pl.pallas_call(
        functools.partial(_nsa_attend_kernel, tq=tq, n_var=seq // SLC_TILE),
        grid=(batch, nq),
        in_specs=[
            pl.BlockSpec((tq, NSA_HEADS * NSA_QK_PAD), lambda b, i: (b * nq + i, 0)),
            pl.BlockSpec((seq, g * NSA_QK_PAD), lambda b, i: (b, 0)),
            pl.BlockSpec((seq, g * NSA_V_DIM), lambda b, i: (b, 0)),
            pl.BlockSpec((seq, g * NSA_QK_PAD), lambda b, i: (b, 1)),
            pl.BlockSpec((seq, g * NSA_V_DIM), lambda b, i: (b, 1)),
            pl.BlockSpec((tq, LANES), lambda b, i: (b * nq + i, 6)),
            pl.BlockSpec(slopes.shape, lambda b, i: (0, 0, 0)),
            pl.BlockSpec((g * LANES, tq), lambda b, i: (b, i)),
            pl.BlockSpec(ext.shape, lambda b, i: (0, 0)),
            pl.BlockSpec((tq, NSA_WIDTH), lambda b, i: (b * nq + i, 0)),
        ],
        out_specs=pl.BlockSpec((tq, NSA_WIDTH), lambda b, i: (b * nq + i, 0)),
        out_shape=jax.ShapeDtypeStruct((batch * seq, NSA_WIDTH), BF16),
        scratch_shapes=[pltpu.VMEM((g, vrows, seq), BF16), pltpu.VMEM((g, vrows, seq), BF16),
                        pltpu.VMEM((LANES, tq), F32)],
        compiler_params=_cparams(("arbitrary", "arbitrary")),
        name="nsa_attend",
    )(pq, pk, pv, pk, pv, p3, slopes, sel, ext, ocmp)


def _pad_head_rows(wt, heads):
    k = wt.shape[1]
    wt = wt.reshape(heads, NSA_QK_DIM, k)
    wt = jnp.pad(wt, ((0, 0), (0, NSA_QK_PAD - NSA_QK_DIM), (0, 0)))
    return wt.reshape(heads * NSA_QK_PAD, k)


def _pad_gain(gain, scale=1.0):
    return jnp.pad(gain * scale, (0, NSA_QK_PAD - NSA_QK_DIM))


def _overlap_matrix(nc, ns):
    i = np.arange(nc)[:, None]
    j = np.arange(ns)[None, :]
    lo = np.maximum(i * CMP_STRIDE, j * SEL_BLOCK)
    hi = np.minimum(i * CMP_STRIDE + CMP_BLOCK, (j + 1) * SEL_BLOCK)
    return (np.maximum(hi - lo, 0) / CMP_STRIDE).astype(np.float32)


def kernel(x, norm_attn, w_in, fox_f_bias, fox_q_gain, fox_k_gain,
           nsa_q_gain, nsa_kc_gain, nsa_ks_gain, nsa_kw_gain,
           cmp_pe_k, cmp_w1_k, cmp_w2_k, cmp_pe_v, cmp_w1_v, cmp_w2_v,
           w_up_fox, w_up_nsa, w_out, norm_ffn, w_ffn_gate, w_ffn_up, w_ffn_down):
    batch, seq, d = x.shape
    m = batch * seq
    depth = w_in.shape[0]
    pts = [0] + [int(p) for p in np.cumsum(IN_SPLITS)]
    nblk = seq // CMP_STRIDE
    n_cmp = nblk - CMP_BLOCK // CMP_STRIDE + 1
    ns = seq // SEL_BLOCK

    slopes_h = jnp.exp2(-8.0 * jnp.arange(1, NSA_HEADS + 1, dtype=F32) / NSA_HEADS) * LOG2E
    slopes = jnp.broadcast_to(
        jnp.pad(slopes_h.reshape(NSA_KV_GROUPS, NSA_HPG), ((0, 0), (0, 8 - NSA_HPG)))[:, :, None],
        (NSA_KV_GROUPS, 8, LANES))
    ovt_np = np.zeros((LANES, nblk), np.float32)
    ovt_np[:ns, :n_cmp] = _overlap_matrix(n_cmp, ns).T
    ovt = jnp.asarray(ovt_np, BF16)
    ext_np = np.zeros((seq, LANES), np.float32)
    ext_np[np.arange(seq), np.arange(seq) // SEL_BLOCK] = 1.0
    ext = jnp.asarray(ext_np, BF16)

    w_in_t = jnp.swapaxes(w_in, 1, 2)

    xf = x.reshape(m, d)
    for l in range(depth):
        wt = w_in_t[l]
        row = dict(zip(("fq", "fk", "fv", "fl", "nq", "kc", "vc", "ks", "vs", "kw", "vw", "ng",
                        "ga", "gb"), pts))

        gain1 = jnp.concatenate([jnp.tile(fox_q_gain[l] * (FOX_HEAD_DIM ** -0.5 * LOG2E), FOX_HEADS),
                                 jnp.tile(fox_k_gain[l], FOX_HEADS),
                                 jnp.ones((FOX_WIDTH,), F32)])
        flag1 = jnp.concatenate([jnp.ones((2 * FOX_WIDTH,), F32), jnp.zeros((FOX_WIDTH,), F32)])
        gain_q = jnp.tile(_pad_gain(nsa_q_gain[l], NSA_QK_DIM ** -0.5 * LOG2E), NSA_HEADS)
        gain_k = jnp.concatenate([jnp.tile(_pad_gain(nsa_ks_gain[l]), NSA_KV_GROUPS),
                                  jnp.tile(_pad_gain(nsa_kw_gain[l]), NSA_KV_GROUPS)])
        n_small = FOX_HEADS + 3 * NSA_HEADS
        w3 = jnp.concatenate([_pad_head_rows(wt[row["kc"]:row["vc"]], NSA_KV_GROUPS),
                              wt[row["vc"]:row["ks"]], wt[row["fl"]:row["nq"]],
                              wt[row["ng"]:row["ga"]],
                              jnp.zeros((LANES - n_small, d), F32)], axis=0)

        xn = _rmsnorm(xf, norm_attn[l])
        q_tile = NSA_HPG * NSA_QK_DIM
        p1 = _project(xn, wt, [t * 1024 for t in range(3 * FOX_WIDTH // 1024)], 1024, gain1, flag1,
                      BF16, 1024, "proj_fox", group=128, count=128)
        pq = _project(xn, wt, [row["nq"], row["nq"] + q_tile], q_tile, gain_q,
                      jnp.ones_like(gain_q), BF16, 1024, "proj_nsa_q", pad_from=NSA_QK_DIM,
                      pad_to=NSA_QK_PAD, group=NSA_QK_PAD, count=NSA_QK_DIM)
        pk = _project(xn, wt, [row["ks"], row["kw"]], KV_K, gain_k, jnp.ones_like(gain_k), BF16,
                      1024, "proj_nsa_k", pad_from=NSA_QK_DIM, pad_to=NSA_QK_PAD,
                      group=NSA_QK_PAD, count=NSA_QK_DIM)
        pv = _project(xn, wt, [row["vs"], row["vw"]], KV_V, None, None, BF16, 1024, "proj_nsa_v")
        p3 = _project(xn, w3, [0], w3.shape[0], None, None, F32, 1024, "proj_f32")

        bias_row = jnp.pad(fox_f_bias[l], (0, LANES - FOX_HEADS)).reshape(1, LANES)
        crep, crow = _decay(p3, bias_row, batch, seq)
        o_a = _fox_attention(p1, crep, crow.reshape(batch, FOX_HEADS, 1, seq), batch, seq)

        pad_d = NSA_QK_PAD - NSA_QK_DIM
        pek = jnp.pad(cmp_pe_k[l], ((0, 0), (0, pad_d)))
        w1k = jnp.pad(cmp_w1_k[l].reshape(CMP_BLOCK, NSA_QK_DIM, CMP_HIDDEN),
                      ((0, 0), (0, pad_d), (0, 0))).astype(BF16)
        w2k = jnp.pad(cmp_w2_k[l], ((0, 0), (0, pad_d))).astype(BF16)
        gk = _pad_gain(nsa_kc_gain[l]).reshape(1, NSA_QK_PAD)
        w1v = cmp_w1_v[l].reshape(CMP_BLOCK, NSA_V_DIM, CMP_HIDDEN).astype(BF16)
        w2v = cmp_w2_v[l].astype(BF16)
        kc, vct = _compress(p3, pek, w1k, w2k, gk, cmp_pe_v[l], w1v, w2v, batch, seq)
        sel, ocmp = _nsa_select(pq, p3, kc, vct, slopes, ovt, batch, seq)
        o_b = _nsa_attend(pq, pk, pv, p3, sel, ocmp, slopes, ext, batch, seq)

        merged = _merge(xn, o_a, o_b, wt, row["ga"], row["gb"], w_up_fox[l], w_up_nsa[l])
        hres, hn = _out_proj_norm(merged, w_out[l], xf, norm_ffn[l])

        act = _swiglu(hn, w_ffn_gate[l], w_ffn_up[l])
        xf = _matmul_residual(act, w_ffn_down[l], hres, 512, 512, "ffn_down")
    return xf.reshape(batch, seq, d)
```

```python
import functools

import numpy as np
import jax
import jax.numpy as jnp
from jax import lax
from jax.experimental import pallas as pl
from jax.experimental.pallas import tpu as pltpu

F32 = jnp.float32
BF16 = jnp.bfloat16

D_MODEL = 2048
FOX_HEADS = 8
FOX_HEAD_DIM = 128
FOX_WIDTH = FOX_HEADS * FOX_HEAD_DIM
NSA_HEADS = 8
NSA_KV_GROUPS = 2
NSA_HPG = NSA_HEADS // NSA_KV_GROUPS
NSA_QK_DIM = 192
NSA_QK_PAD = 256
NSA_V_DIM = 128
NSA_WIDTH = NSA_HEADS * NSA_V_DIM
CMP_BLOCK = 32
CMP_STRIDE = 16
CMP_HIDDEN = 256
SEL_BLOCK = 64
SEL_TOPK = 16
SEL_LOCAL = 2
FORCE_SCORE = 1.0e4
WINDOW = 512
KV_K = NSA_KV_GROUPS * NSA_QK_DIM
KV_V = NSA_KV_GROUPS * NSA_V_DIM
D_FF = -(-(8 * D_MODEL) // (3 * 256)) * 256
RMS_EPS = 1e-6
IN_SPLITS = (FOX_WIDTH, FOX_WIDTH, FOX_WIDTH, FOX_HEADS,
             NSA_HEADS * NSA_QK_DIM, KV_K, KV_V, KV_K, KV_V, KV_K, KV_V,
             3 * NSA_HEADS, D_MODEL, D_MODEL)

LANES = 128
SUBLANES = 8
BF16_SUBLANES = 16
NEG = -1.0e30
LOG2E = 1.4426950408889634
SLC_TILE = 512
VMEM_LIMIT = 56 * 1024 * 1024


def _cparams(sem):
    return pltpu.CompilerParams(dimension_semantics=sem, vmem_limit_bytes=VMEM_LIMIT)


def _rms_kernel(x_ref, g_ref, o_ref):
    x = x_ref[...]
    ms = jnp.mean(x * x, axis=-1, keepdims=True)
    o_ref[...] = (x * lax.rsqrt(ms + RMS_EPS) * g_ref[...]).astype(o_ref.dtype)


def _rmsnorm(x, gain, tm=512):
    m, d = x.shape
    return pl.pallas_call(
        _rms_kernel,
        grid=(m // tm,),
        in_specs=[pl.BlockSpec((tm, d), lambda i: (i, 0)),
                  pl.BlockSpec((1, d), lambda i: (0, 0))],
        out_specs=pl.BlockSpec((tm, d), lambda i: (i, 0)),
        out_shape=jax.ShapeDtypeStruct((m, d), BF16),
        compiler_params=_cparams(("arbitrary",)),
        name="rmsnorm",
    )(x, gain.reshape(1, d))


STAGE_CHUNK = 512
EPILOGUE_SPLIT = 4


def _stage_weight(w_ref, wb_ref, transposed=False, pad_from=0, pad_to=0):
    if not transposed:
        wb_ref[...] = w_ref[...].astype(BF16)
        return
    n_in, k = w_ref.shape
    for c in range(k // STAGE_CHUNK):
        cols = slice(c * STAGE_CHUNK, (c + 1) * STAGE_CHUNK)
        w = w_ref[:, cols]
        if pad_from != pad_to:
            zero = jnp.zeros((pad_to - pad_from, STAGE_CHUNK), F32)
            w = jnp.concatenate(
                [piece for h in range(n_in // pad_from)
                 for piece in (w[h * pad_from:(h + 1) * pad_from, :], zero)], axis=0)
        wb_ref[cols, :] = w.T.astype(BF16)


def _first_m_step():
    return pl.program_id(1) == 0


AUG_LANE = NSA_QK_DIM - LANES


def _key_aug(pos):
    lane = lax.broadcasted_iota(jnp.int32, pos.shape, 1)
    hi = (pos >> 8).astype(F32)
    lo = (pos & 255).astype(F32)
    return jnp.where(lane < AUG_LANE, 0.0,
                     jnp.where(lane < AUG_LANE + 3, hi,
                               jnp.where(lane < AUG_LANE + 6, lo,
                                         jnp.where(lane < AUG_LANE + 9, 1.0, 0.0))))


def _query_aug(pos, const_row, slope_row):
    u = pos.astype(F32) * slope_row
    hi = u.astype(BF16).astype(F32)
    r1 = u - hi
    mid = r1.astype(BF16).astype(F32)
    lane = lax.broadcasted_iota(jnp.int32, pos.shape, 1)
    return const_row + jnp.where(lane == AUG_LANE + 6, hi,
                                 jnp.where(lane == AUG_LANE + 7, mid, r1 - mid))


def _proj_kernel(*refs, pad_from, pad_to, group, count, aug, seq):
    if aug == "query":
        a_ref, w_ref, gain_ref, flag_ref, aug_ref, o_ref, wb_ref = refs
    else:
        a_ref, w_ref, gain_ref, flag_ref, o_ref, wb_ref = refs

    @pl.when(_first_m_step())
    def _():
        _stage_weight(w_ref, wb_ref, True, pad_from, pad_to)

    if group == 0:
        o_ref[...] = jnp.dot(a_ref[...], wb_ref[...],
                             preferred_element_type=F32).astype(o_ref.dtype)
        return
    tm = a_ref.shape[0]
    rows = tm // EPILOGUE_SPLIT
    ys = [jnp.dot(a_ref[r * rows:(r + 1) * rows, :], wb_ref[...], preferred_element_type=F32)
          for r in range(EPILOGUE_SPLIT)]
    for r, y in enumerate(ys):
        if aug:
            pos = ((pl.program_id(1) * tm) % seq + r * rows
                   + lax.broadcasted_iota(jnp.int32, (rows, LANES), 0))
        for c in range(y.shape[1] // group):
            sl = slice(c * group, (c + 1) * group)
            yc = y[:, sl]
            ss = jnp.sum(yc * yc, axis=-1, keepdims=True)
            rs = lax.rsqrt(ss * (1.0 / count) + RMS_EPS)
            scale = jnp.where(flag_ref[:, sl] > 0.0, rs, 1.0)
            out = yc * scale * gain_ref[:, sl]
            rsl = slice(r * rows, (r + 1) * rows)
            if not aug:
                o_ref[rsl, sl] = out.astype(o_ref.dtype)
                continue
            up = slice((c + 1) * group - LANES, (c + 1) * group)
            extra = (_key_aug(pos) if aug == "key"
                     else _query_aug(pos, aug_ref[0:1, up], aug_ref[1:2, up]))
            o_ref[rsl, c * group:(c + 1) * group - LANES] = out[:, :group - LANES].astype(o_ref.dtype)
            o_ref[rsl, up] = (out[:, group - LANES:] + extra).astype(o_ref.dtype)


def _row_window(starts, rows, k):
    def index(j, i):
        start = starts[-1]
        for t in range(len(starts) - 2, -1, -1):
            start = jnp.where(j == t, starts[t], start)
        return pl.multiple_of(start, SUBLANES), 0
    assert all(s % SUBLANES == 0 for s in starts)
    return pl.BlockSpec((pl.Element(rows), pl.Element(k)), index)


def _project(a, wt, starts, tn_in, gain, flag, out_dtype, tm, name,
             pad_from=0, pad_to=0, group=0, count=1, aug="", seq=0, aug_table=None):
    m, k = a.shape
    n_tiles = len(starts)
    tn_out = tn_in if pad_from == pad_to else tn_in // pad_from * pad_to
    n_out = n_tiles * tn_out
    if gain is None:
        gain = jnp.ones((n_out,), F32)
        flag = jnp.zeros((n_out,), F32)
    assert not aug or seq % tm == 0
    operands = [a, wt, gain.reshape(1, n_out), flag.reshape(1, n_out)]
    in_specs = [pl.BlockSpec((tm, k), lambda j, i: (i, 0)),
                _row_window(starts, tn_in, k),
                pl.BlockSpec((1, tn_out), lambda j, i: (0, j)),
                pl.BlockSpec((1, tn_out), lambda j, i: (0, j))]
    if aug == "query":
        operands.append(aug_table)
        in_specs.append(pl.BlockSpec((2, tn_out), lambda j, i: (0, j)))
    return pl.pallas_call(
        functools.partial(_proj_kernel, pad_from=pad_from, pad_to=pad_to, group=group,
                          count=count, aug=aug, seq=seq),
        grid=(n_tiles, m // tm),
        in_specs=in_specs,
        out_specs=pl.BlockSpec((tm, tn_out), lambda j, i: (i, j)),
        out_shape=jax.ShapeDtypeStruct((m, n_out), out_dtype),
        scratch_shapes=[pltpu.VMEM((k, tn_out), BF16)],
        compiler_params=_cparams(("arbitrary", "arbitrary")),
        name=name,
    )(*operands)


def _out_norm_kernel(a_ref, w_ref, r_ref, g_ref, h_ref, hn_ref, wb_ref):
    @pl.when(pl.program_id(0) == 0)
    def _():
        _stage_weight(w_ref, wb_ref)

    h = r_ref[...] + jnp.dot(a_ref[...], wb_ref[...], preferred_element_type=F32)
    h_ref[...] = h
    ms = jnp.mean(h * h, axis=-1, keepdims=True)
    hn_ref[...] = (h * lax.rsqrt(ms + RMS_EPS) * g_ref[...]).astype(hn_ref.dtype)


def _out_proj_norm(a, w, res, gain, tm=512):
    m, k = a.shape
    n = w.shape[1]
    full = lambda i: (0, 0)
    row = lambda i: (i, 0)
    return pl.pallas_call(
        _out_norm_kernel,
        grid=(m // tm,),
        in_specs=[pl.BlockSpec((tm, k), row),
                  pl.BlockSpec((k, n), full, pipeline_mode=pl.Buffered(1)),
                  pl.BlockSpec((tm, n), row), pl.BlockSpec((1, n), full)],
        out_specs=[pl.BlockSpec((tm, n), row), pl.BlockSpec((tm, n), row)],
        out_shape=[jax.ShapeDtypeStruct((m, n), F32), jax.ShapeDtypeStruct((m, n), BF16)],
        scratch_shapes=[pltpu.VMEM((k, n), BF16)],
        compiler_params=_cparams(("arbitrary",)),
        name="out_proj_norm",
    )(a, w, res, gain.reshape(1, n))


def _mm_res_kernel(a_ref, w_ref, r_ref, o_ref, wb_ref):
    @pl.when(_first_m_step())
    def _():
        _stage_weight(w_ref, wb_ref)

    o_ref[...] = r_ref[...] + jnp.dot(a_ref[...], wb_ref[...], preferred_element_type=F32)


def _matmul_residual(a, w, res, tm, tn, name):
    m, k = a.shape
    n = w.shape[1]
    return pl.pallas_call(
        _mm_res_kernel,
        grid=(n // tn, m // tm),
        in_specs=[pl.BlockSpec((tm, k), lambda j, i: (i, 0)),
                  pl.BlockSpec((k, tn), lambda j, i: (0, j)),
                  pl.BlockSpec((tm, tn), lambda j, i: (i, j))],
        out_specs=pl.BlockSpec((tm, tn), lambda j, i: (i, j)),
        out_shape=jax.ShapeDtypeStruct((m, n), F32),
        scratch_shapes=[pltpu.VMEM((k, tn), BF16)],
        compiler_params=_cparams(("arbitrary", "arbitrary")),
        name=name,
    )(a, w, res)


def _merge_kernel(xn_ref, oa_ref, ob_ref, wga_ref, wgb_ref, wuf_ref, wun_ref, o_ref,
                  bga_ref, bgb_ref, buf_ref, bun_ref):
    @pl.when(_first_m_step())
    def _():
        _stage_weight(wga_ref, bga_ref, True)
        _stage_weight(wgb_ref, bgb_ref, True)
        _stage_weight(wuf_ref, buf_ref)
        _stage_weight(wun_ref, bun_ref)

    xn = xn_ref[...]
    ga = jax.nn.sigmoid(jnp.dot(xn, bga_ref[...], preferred_element_type=F32))
    ua = jnp.dot(oa_ref[...], buf_ref[...], preferred_element_type=F32)
    acc = ga * ua
    gb = jax.nn.sigmoid(jnp.dot(xn, bgb_ref[...], preferred_element_type=F32))
    ub = jnp.dot(ob_ref[...], bun_ref[...], preferred_element_type=F32)
    o_ref[...] = (acc + gb * ub).astype(o_ref.dtype)


def _merge(xn, oa, ob, wt, row_a, row_b, wuf, wun, tm=512, tn=512):
    m, d = xn.shape
    n = wuf.shape[1]
    ka = oa.shape[1]
    kb = ob.shape[1]
    row = lambda j, i: (i, 0)
    col = lambda j, i: (0, j)
    return pl.pallas_call(
        _merge_kernel,
        grid=(n // tn, m // tm),
        in_specs=[pl.BlockSpec((tm, d), row), pl.BlockSpec((tm, ka), row),
                  pl.BlockSpec((tm, kb), row),
                  _row_window([row_a + t * tn for t in range(n // tn)], tn, d),
                  _row_window([row_b + t * tn for t in range(n // tn)], tn, d),
                  pl.BlockSpec((ka, tn), col), pl.BlockSpec((kb, tn), col)],
        out_specs=pl.BlockSpec((tm, tn), lambda j, i: (i, j)),
        out_shape=jax.ShapeDtypeStruct((m, n), BF16),
        scratch_shapes=[pltpu.VMEM((d, tn), BF16), pltpu.VMEM((d, tn), BF16),
                        pltpu.VMEM((ka, tn), BF16), pltpu.VMEM((kb, tn), BF16)],
        compiler_params=_cparams(("arbitrary", "arbitrary")),
        name="gated_merge",
    )(xn, oa, ob, wt, wt, wuf, wun)


def _swiglu_kernel(a_ref, wg_ref, wu_ref, o_ref, bg_ref, bu_ref):
    @pl.when(_first_m_step())
    def _():
        _stage_weight(wg_ref, bg_ref)
        _stage_weight(wu_ref, bu_ref)

    a = a_ref[...]
    gt = jnp.dot(a, bg_ref[...], preferred_element_type=F32)
    up = jnp.dot(a, bu_ref[...], preferred_element_type=F32)
    o_ref[...] = (gt * jax.nn.sigmoid(gt) * up).astype(o_ref.dtype)


def _swiglu(a, wg, wu, tm=1024, tn=512):
    m, k = a.shape
    n = wg.shape[1]
    return pl.pallas_call(
        _swiglu_kernel,
        grid=(n // tn, m // tm),
        in_specs=[pl.BlockSpec((tm, k), lambda j, i: (i, 0)),
                  pl.BlockSpec((k, tn), lambda j, i: (0, j)),
                  pl.BlockSpec((k, tn), lambda j, i: (0, j))],
        out_specs=pl.BlockSpec((tm, tn), lambda j, i: (i, j)),
        out_shape=jax.ShapeDtypeStruct((m, n), BF16),
        scratch_shapes=[pltpu.VMEM((k, tn), BF16), pltpu.VMEM((k, tn), BF16)],
        compiler_params=_cparams(("arbitrary", "arbitrary")),
        name="swiglu_up",
    )(a, wg, wu)


def _split3(x):
    hi = x.astype(BF16)
    r1 = x - hi.astype(F32)
    mid = r1.astype(BF16)
    lo = (r1 - mid.astype(F32)).astype(BF16)
    return hi, mid, lo


def _decay_kernel(z_ref, b_ref, crep_ref, crow_ref, *, blk):
    t = z_ref.shape[0]
    r = lax.broadcasted_iota(jnp.int32, (blk, blk), 0)
    c = lax.broadcasted_iota(jnp.int32, (blk, blk), 1)
    tri = jnp.where(r >= c, 1.0, 0.0).astype(BF16)
    lane = lax.broadcasted_iota(jnp.int32, (blk, LANES), 1)
    carry = jnp.zeros((1, LANES), F32)
    for s in range(t // blk):
        rows = slice(s * blk, (s + 1) * blk)
        z = z_ref[rows, :] + b_ref[...]
        logf = (jnp.minimum(z, 0.0) - jnp.log1p(jnp.exp(-jnp.abs(z)))) * LOG2E
        hi, mid, lo = _split3(logf)
        cb = (jnp.dot(tri, hi, preferred_element_type=F32)
              + jnp.dot(tri, mid, preferred_element_type=F32)
              + jnp.dot(tri, lo, preferred_element_type=F32)) + carry
        carry = cb[blk - 1:blk, :]
        crow_ref[0, :, rows] = cb.T[:FOX_HEADS, :]
        for h in range(FOX_HEADS):
            col = jnp.sum(jnp.where(lane == h, cb, 0.0), axis=-1, keepdims=True)
            crep_ref[0, h, rows, :] = jnp.broadcast_to(col, (blk, LANES))


def _decay(p3, bias_row, batch, seq, blk=256):
    return pl.pallas_call(
        functools.partial(_decay_kernel, blk=blk),
        grid=(batch,),
        in_specs=[pl.BlockSpec((seq, LANES), lambda b: (b, 6)),
                  pl.BlockSpec((1, LANES), lambda b: (0, 0))],
        out_specs=[pl.BlockSpec((1, FOX_HEADS, seq, LANES), lambda b: (b, 0, 0, 0)),
                   pl.BlockSpec((1, FOX_HEADS, seq), lambda b: (b, 0, 0))],
        out_shape=[jax.ShapeDtypeStruct((batch, FOX_HEADS, seq, LANES), F32),
                   jax.ShapeDtypeStruct((batch, FOX_HEADS, seq), F32)],
        compiler_params=_cparams(("arbitrary",)),
        name="fox_decay_cumsum",
    )(p3, bias_row)


def _transpose_bf16(x):
    return x.astype(F32).T.astype(BF16)


def _normalised(acc, d):
    return acc[:d, :] * (1.0 / jnp.maximum(acc[d:d + 1, :], 1e-30))


def _fox_kernel(q_ref, k_ref, v_ref, crep_ref, crow_ref, o_ref, vt_ref, *, tq, nq, nh):
    i = pl.program_id(2)
    dh = FOX_HEAD_DIM

    @pl.when(i == 0)
    def _():
        for h in range(nh):
            vt_ref[h, 0:dh, :] = _transpose_bf16(v_ref[:, h * dh:(h + 1) * dh])
            vt_ref[h, dh:, :] = jnp.ones((vt_ref.shape[1] - dh, vt_ref.shape[2]), BF16)

    qts = [_transpose_bf16(q_ref[:, h * dh:(h + 1) * dh]) for h in range(nh)]
    cis = [crow_ref[0, h] for h in range(nh)]
    reps = tq // LANES
    rk = lax.broadcasted_iota(jnp.int32, (tq, tq), 0)
    cq = lax.broadcasted_iota(jnp.int32, (tq, tq), 1)

    def scores(h, k0):
        cj = crep_ref[0, h, k0:k0 + tq, :]
        st = jnp.dot(k_ref[k0:k0 + tq, h * dh:(h + 1) * dh], qts[h],
                     preferred_element_type=F32)
        return st + cis[h] - jnp.concatenate([cj] * reps, axis=1)

    def variant(n):
        def run():
            starts = [t * tq for t in range(n, -1, -1)]
            tiles = [[jnp.where(rk <= cq, scores(h, starts[0]), NEG)]
                     + [scores(h, k0) for k0 in starts[1:]] for h in range(nh)]
            ms = [jnp.max(tiles[h][0], axis=0, keepdims=True) for h in range(nh)]
            accs = [jnp.dot(vt_ref[h, :, starts[0]:starts[0] + tq],
                            jnp.exp2(tiles[h][0] - ms[h]).astype(BF16),
                            preferred_element_type=F32) for h in range(nh)]
            for t in range(1, n + 1):
                k0 = starts[t]
                for h in range(nh):
                    st = tiles[h][t]
                    m_new = jnp.maximum(ms[h], jnp.max(st, axis=0, keepdims=True))
                    accs[h] = jnp.exp2(ms[h] - m_new) * accs[h] + jnp.dot(
                        vt_ref[h, :, k0:k0 + tq], jnp.exp2(st - m_new).astype(BF16),
                        preferred_element_type=F32)
                    ms[h] = m_new
            for h in range(nh):
                o_ref[:, h * dh:(h + 1) * dh] = _normalised(accs[h], dh).T.astype(o_ref.dtype)
        return run

    lax.switch(i, [variant(n) for n in range(nq)])


def _fox_attention(p1, crep, crow4, batch, seq, tq=512, nh=4):
    nq = seq // tq
    hg = FOX_HEADS // nh
    w = nh * FOX_HEAD_DIM
    return pl.pallas_call(
        functools.partial(_fox_kernel, tq=tq, nq=nq, nh=nh),
        grid=(batch, hg, nq),
        in_specs=[pl.BlockSpec((tq, w), lambda b, hh, i: (b * nq + i, hh)),
                  pl.BlockSpec((seq, w), lambda b, hh, i: (b, hg + hh)),
                  pl.BlockSpec((seq, w), lambda b, hh, i: (b, 2 * hg + hh)),
                  pl.BlockSpec((1, nh, seq, LANES), lambda b, hh, i: (b, hh, 0, 0)),
                  pl.BlockSpec((1, nh, 1, tq), lambda b, hh, i: (b, hh, 0, i))],
        out_specs=pl.BlockSpec((tq, w), lambda b, hh, i: (b * nq + i, hh)),
        out_shape=jax.ShapeDtypeStruct((batch * seq, FOX_WIDTH), BF16),
        scratch_shapes=[pltpu.VMEM((nh, FOX_HEAD_DIM + BF16_SUBLANES, seq), BF16)],
        compiler_params=_cparams(("arbitrary", "arbitrary", "arbitrary")),
        name="fox_attention",
    )(p1, p1, p1, crep, crow4)


def _compress_one(z_refs, pe_ref, w1_ref, w2_ref, nblk):
    half = CMP_BLOCK // 2
    first = jnp.zeros((nblk, CMP_HIDDEN), F32)
    second = jnp.zeros((nblk, CMP_HIDDEN), F32)
    for p in range(half):
        rows = pl.ds(p, nblk, stride=CMP_STRIDE)
        zp = [z_ref[rows, :] for z_ref in z_refs]
        zp = zp[0] if len(zp) == 1 else jnp.concatenate(zp, axis=1)
        first += jnp.dot((zp + pe_ref[p:p + 1, :]).astype(BF16), w1_ref[p],
                         preferred_element_type=F32)
        second += jnp.dot((zp + pe_ref[half + p:half + p + 1, :]).astype(BF16),
                          w1_ref[half + p], preferred_element_type=F32)
    hid = first + pltpu.roll(second, nblk - 1, 0)
    act = (hid * jax.nn.sigmoid(hid)).astype(BF16)
    return jnp.dot(act, w2_ref[...], preferred_element_type=F32)


def _compress_kernel(zk0_ref, zk1_ref, zv_ref, pek_ref, w1k_ref, w2k_ref, gk_ref,
                     pev_ref, w1v_ref, w2v_ref, kc_ref, vc_ref, *, nblk):
    kc = _compress_one((zk0_ref, zk1_ref), pek_ref, w1k_ref, w2k_ref, nblk)
    ms = jnp.sum(kc * kc, axis=-1, keepdims=True) * (1.0 / NSA_QK_DIM)
    kc = kc * lax.rsqrt(ms + RMS_EPS) * gk_ref[...]
    pos = CMP_STRIDE * lax.broadcasted_iota(jnp.int32, (nblk, LANES), 0) + (CMP_BLOCK - 1)
    up = NSA_QK_PAD - LANES
    kc_ref[:, :up] = kc[:, :up].astype(kc_ref.dtype)
    kc_ref[:, up:] = (kc[:, up:] + _key_aug(pos)).astype(kc_ref.dtype)
    vc = _compress_one((zv_ref,), pev_ref, w1v_ref, w2v_ref, nblk)
    vc_ref[...] = vc.T.astype(vc_ref.dtype)


def _compress(p3, pek, w1k, w2k, gk, pev, w1v, w2v, batch, seq):
    g = NSA_KV_GROUPS
    nblk = seq // CMP_STRIDE
    full2 = lambda b, gg: (0, 0)
    full3 = lambda b, gg: (0, 0, 0)
    return pl.pallas_call(
        functools.partial(_compress_kernel, nblk=nblk),
        grid=(batch, g),
        in_specs=[pl.BlockSpec((seq, LANES), lambda b, gg: (b, 2 * gg)),
                  pl.BlockSpec((seq, LANES), lambda b, gg: (b, 2 * gg + 1)),
                  pl.BlockSpec((seq, NSA_V_DIM), lambda b, gg: (b, 4 + gg)),
                  pl.BlockSpec(pek.shape, full2), pl.BlockSpec(w1k.shape, full3),
                  pl.BlockSpec(w2k.shape, full2), pl.BlockSpec(gk.shape, full2),
                  pl.BlockSpec(pev.shape, full2), pl.BlockSpec(w1v.shape, full3),
                  pl.BlockSpec(w2v.shape, full2)],
        out_specs=[pl.BlockSpec((nblk, NSA_QK_PAD), lambda b, gg: (b * g + gg, 0)),
                   pl.BlockSpec((NSA_V_DIM, nblk), lambda b, gg: (b * g + gg, 0))],
        out_shape=[jax.ShapeDtypeStruct((batch * g * nblk, NSA_QK_PAD), BF16),
                   jax.ShapeDtypeStruct((batch * g * NSA_V_DIM, nblk), BF16)],
        compiler_params=_cparams(("arbitrary", "arbitrary")),
        name="nsa_compress",
    )(p3, p3, p3, pek, w1k, w2k, gk, pev, w1v, w2v)


def _q_heads_t(q_ref):
    qb = q_ref[...]
    return jnp.concatenate(
        [_transpose_bf16(qb[:, hh * NSA_QK_PAD:(hh + 1) * NSA_QK_PAD]) for hh in range(NSA_HPG)],
        axis=1)


def _gate_rows(glt_ref, g, hh):
    base = FOX_HEADS + (g * NSA_HPG + hh) * 3
    return [glt_ref[pl.ds(base + br, 1), :] for br in range(3)]


def _nsa_select_kernel(q_ref, kc_ref, vct_ref, gl_ref, ovt_ref,
                       sel_ref, ocmp_ref, glt_ref, *, tq, n_cmp, n_sel):
    g = pl.program_id(1)
    t0 = pl.program_id(2) * tq
    hpg = NSA_HPG
    dv = NSA_V_DIM
    q4t = _q_heads_t(q_ref)
    rk = lax.broadcasted_iota(jnp.int32, (LANES, tq), 0)
    cq = lax.broadcasted_iota(jnp.int32, (LANES, tq), 1)

    s_c = jnp.dot(kc_ref[...], q4t, preferred_element_type=F32)
    dist_c = (t0 + cq) - (CMP_STRIDE * rk + (CMP_BLOCK - 1))
    mask_c = jnp.where(rk < n_cmp, dist_c, -1) >= 0
    probs = []
    p_sum = jnp.zeros((LANES, tq), F32)
    for hh in range(hpg):
        sm = jnp.where(mask_c, s_c[:, hh * tq:(hh + 1) * tq], NEG)
        m = jnp.max(sm, axis=0, keepdims=True)
        e = jnp.where(mask_c, jnp.exp2(sm - m), 0.0)
        p = e * (1.0 / jnp.maximum(jnp.sum(e, axis=0, keepdims=True), 1e-30))
        probs.append(p)
        p_sum = p_sum + p
    o_cmp = jnp.dot(vct_ref[...], jnp.concatenate(probs, axis=1).astype(BF16),
                    preferred_element_type=F32)

    ph = p_sum.astype(BF16)
    plo = (p_sum - ph.astype(F32)).astype(BF16)
    ovt = ovt_ref[...]
    imp = (jnp.dot(ovt, ph, preferred_element_type=F32)
           + jnp.dot(ovt, plo, preferred_element_type=F32))[:n_sel, :]

    rj = lax.broadcasted_iota(jnp.int32, (n_sel, tq), 0)
    tcol = t0 + lax.broadcasted_iota(jnp.int32, (n_sel, tq), 1)
    back = (tcol >> (SEL_BLOCK.bit_length() - 1)) - rj
    elig = back >= 0
    forced = jnp.where(rj == 0, 0, jnp.where(elig, back, SEL_LOCAL)) < SEL_LOCAL
    score = jnp.where(elig, jnp.where(forced, FORCE_SCORE, imp), -1.0)
    rank = jnp.zeros((n_sel, tq), F32)
    for jp in range(n_sel):
        row = score[jp:jp + 1, :]
        later = jnp.where(rj > jp, 1.0, 0.0)
        rank = rank + jnp.where(row > score, 1.0, jnp.where(row == score, later, 0.0))
    sel = jnp.where(elig, jnp.where(rank < SEL_TOPK, 1.0, 0.0), 0.0)
    sel_ref[...] = jnp.concatenate(
        [sel, jnp.zeros((LANES - n_sel, tq), F32)], axis=0).astype(sel_ref.dtype)

    glt_ref[...] = jax.nn.sigmoid(gl_ref[...]).T
    for hh in range(hpg):
        gate = _gate_rows(glt_ref, g, hh)[0]
        ocmp_ref[:, hh * dv:(hh + 1) * dv] = (gate * o_cmp[:, hh * tq:(hh + 1) * tq]).T


def _nsa_select(pq, p3, kc, vct, ovt, batch, seq, tq=512):
    g = NSA_KV_GROUPS
    nq = seq // tq
    nblk = seq // CMP_STRIDE
    n_cmp = nblk - CMP_BLOCK // CMP_STRIDE + 1
    return pl.pallas_call(
        functools.partial(_nsa_select_kernel, tq=tq, n_cmp=n_cmp, n_sel=seq // SEL_BLOCK),
        grid=(batch, g, nq),
        in_specs=[
            pl.BlockSpec((tq, NSA_HPG * NSA_QK_PAD), lambda b, gg, i: (b * nq + i, gg)),
            pl.BlockSpec((nblk, NSA_QK_PAD), lambda b, gg, i: (b * g + gg, 0)),
            pl.BlockSpec((NSA_V_DIM, nblk), lambda b, gg, i: (b * g + gg, 0)),
            pl.BlockSpec((tq, LANES), lambda b, gg, i: (b * nq + i, 6)),
            pl.BlockSpec(ovt.shape, lambda b, gg, i: (0, 0)),
        ],
        out_specs=[pl.BlockSpec((LANES, tq), lambda b, gg, i: (b * g + gg, i)),
                   pl.BlockSpec((tq, NSA_HPG * NSA_V_DIM), lambda b, gg, i: (b * nq + i, gg))],
        out_shape=[jax.ShapeDtypeStruct((batch * g * LANES, seq), BF16),
                   jax.ShapeDtypeStruct((batch * seq, NSA_WIDTH), F32)],
        scratch_shapes=[pltpu.VMEM((LANES, tq), F32)],
        compiler_params=_cparams(("arbitrary", "arbitrary", "arbitrary")),
        name="nsa_select",
    )(pq, kc, vct, p3, ovt)


def _nsa_attend_kernel(q_ref, ks_ref, vs_ref, kw_ref, vw_ref, gl_ref, sel_ref,
                       ext_ref, ocmp_ref, o_ref, vst_ref, vwt_ref, glt_ref, *, tq, n_var):
    i = pl.program_id(1)
    t0 = i * tq
    hpg = NSA_HPG
    ng = NSA_KV_GROUPS
    dv = NSA_V_DIM
    dk = NSA_QK_PAD

    @pl.when(i == 0)
    def _():
        ones = jnp.ones((vst_ref.shape[1] - dv, vst_ref.shape[2]), BF16)
        for gg in range(ng):
            vst_ref[gg, 0:dv, :] = _transpose_bf16(vs_ref[:, gg * dv:(gg + 1) * dv])
            vst_ref[gg, dv:, :] = ones
            vwt_ref[gg, 0:dv, :] = _transpose_bf16(vw_ref[:, gg * dv:(gg + 1) * dv])
            vwt_ref[gg, dv:, :] = ones

    qb = q_ref[...]
    q4t = [jnp.concatenate(
        [_transpose_bf16(qb[:, (gg * hpg + hh) * dk:(gg * hpg + hh + 1) * dk])
         for hh in range(hpg)], axis=1) for gg in range(ng)]
    sels = [sel_ref[gg * LANES:(gg + 1) * LANES, :] for gg in range(ng)]
    glt_ref[...] = jax.nn.sigmoid(gl_ref[...]).T

    def scores(gg, k, k0, rows, mask_fn):
        st = jnp.dot(k, q4t[gg], preferred_element_type=F32)
        dist = (lax.broadcasted_iota(jnp.int32, (rows, tq), 1)
                - lax.broadcasted_iota(jnp.int32, (rows, tq), 0)) + (t0 - k0)
        return st + jnp.concatenate([jnp.where(mask_fn(dist), 0.0, NEG)] * hpg, axis=1)

    def variant(n):
        def run():
            kw0 = pl.multiple_of(jnp.maximum(t0 - WINDOW, 0), LANES)
            wrows = WINDOW + tq
            o_win = []
            for gg in range(ng):
                sw = scores(gg, kw_ref[pl.ds(kw0, wrows), gg * dk:(gg + 1) * dk], kw0, wrows,
                            lambda dist: jnp.where(dist >= 0, dist, WINDOW) < WINDOW)
                pw = jnp.exp2(sw - jnp.max(sw, axis=0, keepdims=True))
                o_win.append(_normalised(
                    jnp.dot(vwt_ref[gg, :, pl.ds(kw0, wrows)], pw.astype(BF16),
                            preferred_element_type=F32), dv))

            starts = [t * SLC_TILE for t in range(n, -1, -1)]
            tiles = [[] for _ in range(ng)]
            for k0 in starts:
                for gg in range(ng):
                    selx = jnp.dot(ext_ref[k0:k0 + SLC_TILE, :], sels[gg],
                                   preferred_element_type=F32)
                    tiles[gg].append(scores(
                        gg, ks_ref[k0:k0 + SLC_TILE, gg * dk:(gg + 1) * dk], k0, SLC_TILE,
                        lambda dist, selx=selx: jnp.where(dist >= 0, selx, 0.0) > 0.5))
            ms = [jnp.max(tiles[gg][0], axis=0, keepdims=True) for gg in range(ng)]
            accs = [jnp.dot(vst_ref[gg, :, starts[0]:starts[0] + SLC_TILE],
                            jnp.exp2(tiles[gg][0] - ms[gg]).astype(BF16),
                            preferred_element_type=F32) for gg in range(ng)]
            for t in range(1, n + 1):
                k0 = starts[t]
                for gg in range(ng):
                    st = tiles[gg][t]
                    m_new = jnp.maximum(ms[gg], jnp.max(st, axis=0, keepdims=True))
                    accs[gg] = jnp.exp2(ms[gg] - m_new) * accs[gg] + jnp.dot(
                        vst_ref[gg, :, k0:k0 + SLC_TILE], jnp.exp2(st - m_new).astype(BF16),
                        preferred_element_type=F32)
                    ms[gg] = m_new

            for gg in range(ng):
                o_slc = _normalised(accs[gg], dv)
                for hh in range(hpg):
                    _, g_slc, g_win = _gate_rows(glt_ref, gg, hh)
                    lanes = slice(hh * tq, (hh + 1) * tq)
                    out = g_slc * o_slc[:, lanes] + g_win * o_win[gg][:, lanes]
                    cols = slice((gg * hpg + hh) * dv, (gg * hpg + hh + 1) * dv)
                    o_ref[:, cols] = (ocmp_ref[:, cols] + out.T).astype(o_ref.dtype)
        return run

    lax.switch(t0 // SLC_TILE, [variant(n) for n in range(n_var)])


def _nsa_attend(pq, pk, pv, p3, sel, ocmp, ext, batch, seq):
    tq = LANES
    g = NSA_KV_GROUPS
    nq = seq // tq
    vrows = NSA_V_DIM + BF16_SUBLANES
    return pl.pallas_call(
        functools.partial(_nsa_attend_kernel, tq=tq, n_var=seq // SLC_TILE),
        grid=(batch, nq),
        in_specs=[
            pl.BlockSpec((tq, NSA_HEADS * NSA_QK_PAD), lambda b, i: (b * nq + i, 0)),
            pl.BlockSpec((seq, g * NSA_QK_PAD), lambda b, i: (b, 0)),
            pl.BlockSpec((seq, g * NSA_V_DIM), lambda b, i: (b, 0)),
            pl.BlockSpec((seq, g * NSA_QK_PAD), lambda b, i: (b, 1)),
            pl.BlockSpec((seq, g * NSA_V_DIM), lambda b, i: (b, 1)),
            pl.BlockSpec((tq, LANES), lambda b, i: (b * nq + i, 6)),
            pl.BlockSpec((g * LANES, tq), lambda b, i: (b, i)),
            pl.BlockSpec(ext.shape, lambda b, i: (0, 0)),
            pl.BlockSpec((tq, NSA_WIDTH), lambda b, i: (b * nq + i, 0)),
        ],
        out_specs=pl.BlockSpec((tq, NSA_WIDTH), lambda b, i: (b * nq + i, 0)),
        out_shape=jax.ShapeDtypeStruct((batch * seq, NSA_WIDTH), BF16),
        scratch_shapes=[pltpu.VMEM((g, vrows, seq), BF16), pltpu.VMEM((g, vrows, seq), BF16),
                        pltpu.VMEM((LANES, tq), F32)],
        compiler_params=_cparams(("arbitrary", "arbitrary")),
        name="nsa_attend",
    )(pq, pk, pv, pk, pv, p3, sel, ext, ocmp)


def _pad_head_rows(wt, heads):
    k = wt.shape[1]
    wt = wt.reshape(heads, NSA_QK_DIM, k)
    wt = jnp.pad(wt, ((0, 0), (0, NSA_QK_PAD - NSA_QK_DIM), (0, 0)))
    return wt.reshape(heads * NSA_QK_PAD, k)


def _pad_gain(gain, scale=1.0):
    return jnp.pad(gain * scale, (0, NSA_QK_PAD - NSA_QK_DIM))


def _overlap_matrix(nc, ns):
    i = np.arange(nc)[:, None]
    j = np.arange(ns)[None, :]
    lo = np.maximum(i * CMP_STRIDE, j * SEL_BLOCK)
    hi = np.minimum(i * CMP_STRIDE + CMP_BLOCK, (j + 1) * SEL_BLOCK)
    return (np.maximum(hi - lo, 0) / CMP_STRIDE).astype(np.float32)


def kernel(x, norm_attn, w_in, fox_f_bias, fox_q_gain, fox_k_gain,
           nsa_q_gain, nsa_kc_gain, nsa_ks_gain, nsa_kw_gain,
           cmp_pe_k, cmp_w1_k, cmp_w2_k, cmp_pe_v, cmp_w1_v, cmp_w2_v,
           w_up_fox, w_up_nsa, w_out, norm_ffn, w_ffn_gate, w_ffn_up, w_ffn_down):
    batch, seq, d = x.shape
    m = batch * seq
    depth = w_in.shape[0]
    pts = [0] + [int(p) for p in np.cumsum(IN_SPLITS)]
    nblk = seq // CMP_STRIDE
    n_cmp = nblk - CMP_BLOCK // CMP_STRIDE + 1
    ns = seq // SEL_BLOCK

    slope = jnp.exp2(-8.0 * jnp.arange(1, NSA_HEADS + 1, dtype=F32) / NSA_HEADS) * LOG2E
    s1, s2, s3 = [p.astype(F32) for p in _split3(slope)]
    q_const = jnp.stack([256.0 * s1, 256.0 * s2, 256.0 * s3, s1, s2, s3], axis=1)
    lane0 = NSA_QK_DIM
    aug_q = jnp.zeros((2, NSA_HEADS, NSA_QK_PAD), F32)
    aug_q = aug_q.at[0, :, lane0:lane0 + 6].set(q_const)
    aug_q = aug_q.at[1, :, lane0 + 6:lane0 + 9].set(-slope[:, None])
    aug_q = aug_q.reshape(2, NSA_HEADS * NSA_QK_PAD)
    ovt_np = np.zeros((LANES, nblk), np.float32)
    ovt_np[:ns, :n_cmp] = _overlap_matrix(n_cmp, ns).T
    ovt = jnp.asarray(ovt_np, BF16)
    ext_np = np.zeros((seq, LANES), np.float32)
    ext_np[np.arange(seq), np.arange(seq) // SEL_BLOCK] = 1.0
    ext = jnp.asarray(ext_np, BF16)

    w_in_t = jnp.swapaxes(w_in, 1, 2)

    xf = x.reshape(m, d)
    for l in range(depth):
        wt = w_in_t[l]
        row = dict(zip(("fq", "fk", "fv", "fl", "nq", "kc", "vc", "ks", "vs", "kw", "vw", "ng",
                        "ga", "gb"), pts))

        gain1 = jnp.concatenate([jnp.tile(fox_q_gain[l] * (FOX_HEAD_DIM ** -0.5 * LOG2E), FOX_HEADS),
                                 jnp.tile(fox_k_gain[l], FOX_HEADS),
                                 jnp.ones((FOX_WIDTH,), F32)])
        flag1 = jnp.concatenate([jnp.ones((2 * FOX_WIDTH,), F32), jnp.zeros((FOX_WIDTH,), F32)])
        gain_q = jnp.tile(_pad_gain(nsa_q_gain[l], NSA_QK_DIM ** -0.5 * LOG2E), NSA_HEADS)
        gain_k = jnp.concatenate([jnp.tile(_pad_gain(nsa_ks_gain[l]), NSA_KV_GROUPS),
                                  jnp.tile(_pad_gain(nsa_kw_gain[l]), NSA_KV_GROUPS)])
        n_small = FOX_HEADS + 3 * NSA_HEADS
        w3 = jnp.concatenate([_pad_head_rows(wt[row["kc"]:row["vc"]], NSA_KV_GROUPS),
                              wt[row["vc"]:row["ks"]], wt[row["fl"]:row["nq"]],
                              wt[row["ng"]:row["ga"]],
                              jnp.zeros((LANES - n_small, d), F32)], axis=0)

        xn = _rmsnorm(xf, norm_attn[l])
        q_tile = NSA_HPG * NSA_QK_DIM
        p1 = _project(xn, wt, [t * 1024 for t in range(3 * FOX_WIDTH // 1024)], 1024, gain1, flag1,
                      BF16, 1024, "proj_fox", group=128, count=128)
        pq = _project(xn, wt, [row["nq"], row["nq"] + q_tile], q_tile, gain_q,
                      jnp.ones_like(gain_q), BF16, 1024, "proj_nsa_q", pad_from=NSA_QK_DIM,
                      pad_to=NSA_QK_PAD, group=NSA_QK_PAD, count=NSA_QK_DIM,
                      aug="query", seq=seq, aug_table=aug_q)
        pk = _project(xn, wt, [row["ks"], row["kw"]], KV_K, gain_k, jnp.ones_like(gain_k), BF16,
                      1024, "proj_nsa_k", pad_from=NSA_QK_DIM, pad_to=NSA_QK_PAD,
                      group=NSA_QK_PAD, count=NSA_QK_DIM, aug="key", seq=seq)
        pv = _project(xn, wt, [row["vs"], row["vw"]], KV_V, None, None, BF16, 1024, "proj_nsa_v")
        p3 = _project(xn, w3, [0], w3.shape[0], None, None, F32, 1024, "proj_f32")

        bias_row = jnp.pad(fox_f_bias[l], (0, LANES - FOX_HEADS)).reshape(1, LANES)
        crep, crow = _decay(p3, bias_row, batch, seq)
        o_a = _fox_attention(p1, crep, crow.reshape(batch, FOX_HEADS, 1, seq), batch, seq)

        pad_d = NSA_QK_PAD - NSA_QK_DIM
        pek = jnp.pad(cmp_pe_k[l], ((0, 0), (0, pad_d)))
        w1k = jnp.pad(cmp_w1_k[l].reshape(CMP_BLOCK, NSA_QK_DIM, CMP_HIDDEN),
                      ((0, 0), (0, pad_d), (0, 0))).astype(BF16)
        w2k = jnp.pad(cmp_w2_k[l], ((0, 0), (0, pad_d))).astype(BF16)
        gk = _pad_gain(nsa_kc_gain[l]).reshape(1, NSA_QK_PAD)
        w1v = cmp_w1_v[l].reshape(CMP_BLOCK, NSA_V_DIM, CMP_HIDDEN).astype(BF16)
        w2v = cmp_w2_v[l].astype(BF16)
        kc, vct = _compress(p3, pek, w1k, w2k, gk, cmp_pe_v[l], w1v, w2v, batch, seq)
        sel, ocmp = _nsa_select(pq, p3, kc, vct, ovt, batch, seq)
        o_b = _nsa_attend(pq, pk, pv, p3, sel, ocmp, ext, batch, seq)

        merged = _merge(xn, o_a, o_b, wt, row["ga"], row["gb"], w_up_fox[l], w_up_nsa[l])
        hres, hn = _out_proj_norm(merged, w_out[l], xf, norm_ffn[l])

        act = _swiglu(hn, w_ffn_gate[l], w_ffn_up[l])
        xf = _matmul_residual(act, w_ffn_down[l], hres, 512, 512, "ffn_down")
    return xf.reshape(batch, seq, d)
```

```python
import functools

import numpy as np
import jax
import jax.numpy as jnp
from jax import lax
from jax.experimental import pallas as pl
from jax.experimental.pallas import tpu as pltpu

F32 = jnp.float32
BF16 = jnp.bfloat16

D_MODEL = 2048
FOX_HEADS = 8
FOX_HEAD_DIM = 128
FOX_WIDTH = FOX_HEADS * FOX_HEAD_DIM
NSA_HEADS = 8
NSA_KV_GROUPS = 2
NSA_HPG = NSA_HEADS // NSA_KV_GROUPS
NSA_QK_DIM = 192
NSA_QK_PAD = 256
NSA_V_DIM = 128
NSA_WIDTH = NSA_HEADS * NSA_V_DIM
CMP_BLOCK = 32
CMP_STRIDE = 16
CMP_HIDDEN = 256
SEL_BLOCK = 64
SEL_TOPK = 16
SEL_LOCAL = 2
FORCE_SCORE = 1.0e4
WINDOW = 512
KV_K = NSA_KV_GROUPS * NSA_QK_DIM
KV_V = NSA_KV_GROUPS * NSA_V_DIM
D_FF = -(-(8 * D_MODEL) // (3 * 256)) * 256
RMS_EPS = 1e-6
IN_SPLITS = (FOX_WIDTH, FOX_WIDTH, FOX_WIDTH, FOX_HEADS,
             NSA_HEADS * NSA_QK_DIM, KV_K, KV_V, KV_K, KV_V, KV_K, KV_V,
             3 * NSA_HEADS, D_MODEL, D_MODEL)

LANES = 128
SUBLANES = 8
BF16_SUBLANES = 16
NEG = -1.0e30
LOG2E = 1.4426950408889634
SLC_TILE = 512
VMEM_LIMIT = 56 * 1024 * 1024


def _cparams(sem):
    return pltpu.CompilerParams(dimension_semantics=sem, vmem_limit_bytes=VMEM_LIMIT)


def _rms_kernel(x_ref, g_ref, o_ref):
    x = x_ref[...]
    ms = jnp.mean(x * x, axis=-1, keepdims=True)
    o_ref[...] = (x * lax.rsqrt(ms + RMS_EPS) * g_ref[...]).astype(o_ref.dtype)


def _rmsnorm(x, gain, tm=512):
    m, d = x.shape
    return pl.pallas_call(
        _rms_kernel,
        grid=(m // tm,),
        in_specs=[pl.BlockSpec((tm, d), lambda i: (i, 0)),
                  pl.BlockSpec((1, d), lambda i: (0, 0))],
        out_specs=pl.BlockSpec((tm, d), lambda i: (i, 0)),
        out_shape=jax.ShapeDtypeStruct((m, d), BF16),
        compiler_params=_cparams(("arbitrary",)),
        name="rmsnorm",
    )(x, gain.reshape(1, d))


STAGE_CHUNK = 512
EPILOGUE_SPLIT = 4


def _stage_weight(w_ref, wb_ref, transposed=False, pad_from=0, pad_to=0):
    if not transposed:
        wb_ref[...] = w_ref[...].astype(BF16)
        return
    n_in, k = w_ref.shape
    for c in range(k // STAGE_CHUNK):
        cols = slice(c * STAGE_CHUNK, (c + 1) * STAGE_CHUNK)
        w = w_ref[:, cols]
        if pad_from != pad_to:
            zero = jnp.zeros((pad_to - pad_from, STAGE_CHUNK), F32)
            w = jnp.concatenate(
                [piece for h in range(n_in // pad_from)
                 for piece in (w[h * pad_from:(h + 1) * pad_from, :], zero)], axis=0)
        wb_ref[cols, :] = w.T.astype(BF16)


def _first_m_step():
    return pl.program_id(1) == 0


AUG_LANE = NSA_QK_DIM - LANES


def _key_aug(pos):
    lane = lax.broadcasted_iota(jnp.int32, pos.shape, 1)
    hi = (pos >> 8).astype(F32)
    lo = (pos & 255).astype(F32)
    return jnp.where(lane < AUG_LANE, 0.0,
                     jnp.where(lane < AUG_LANE + 3, hi,
                               jnp.where(lane < AUG_LANE + 6, lo,
                                         jnp.where(lane < AUG_LANE + 9, 1.0, 0.0))))


def _query_aug(pos, const_row, slope_row):
    u = pos.astype(F32) * slope_row
    hi = u.astype(BF16).astype(F32)
    r1 = u - hi
    mid = r1.astype(BF16).astype(F32)
    lane = lax.broadcasted_iota(jnp.int32, pos.shape, 1)
    return const_row + jnp.where(lane == AUG_LANE + 6, hi,
                                 jnp.where(lane == AUG_LANE + 7, mid, r1 - mid))


def _proj_kernel(*refs, pad_from, pad_to, group, count, aug, seq):
    if aug == "query":
        a_ref, w_ref, gain_ref, flag_ref, aug_ref, o_ref, wb_ref = refs
    else:
        a_ref, w_ref, gain_ref, flag_ref, o_ref, wb_ref = refs

    @pl.when(_first_m_step())
    def _():
        _stage_weight(w_ref, wb_ref, True, pad_from, pad_to)

    if group == 0:
        o_ref[...] = jnp.dot(a_ref[...], wb_ref[...],
                             preferred_element_type=F32).astype(o_ref.dtype)
        return
    tm = a_ref.shape[0]
    rows = tm // EPILOGUE_SPLIT
    ys = [jnp.dot(a_ref[r * rows:(r + 1) * rows, :], wb_ref[...], preferred_element_type=F32)
          for r in range(EPILOGUE_SPLIT)]
    for r, y in enumerate(ys):
        if aug:
            pos = ((pl.program_id(1) * tm) % seq + r * rows
                   + lax.broadcasted_iota(jnp.int32, (rows, LANES), 0))
        for c in range(y.shape[1] // group):
            sl = slice(c * group, (c + 1) * group)
            yc = y[:, sl]
            ss = jnp.sum(yc * yc, axis=-1, keepdims=True)
            rs = lax.rsqrt(ss * (1.0 / count) + RMS_EPS)
            scale = jnp.where(flag_ref[:, sl] > 0.0, rs, 1.0)
            out = yc * scale * gain_ref[:, sl]
            rsl = slice(r * rows, (r + 1) * rows)
            if not aug:
                o_ref[rsl, sl] = out.astype(o_ref.dtype)
                continue
            up = slice((c + 1) * group - LANES, (c + 1) * group)
            extra = (_key_aug(pos) if aug == "key"
                     else _query_aug(pos, aug_ref[0:1, up], aug_ref[1:2, up]))
            o_ref[rsl, c * group:(c + 1) * group - LANES] = out[:, :group - LANES].astype(o_ref.dtype)
            o_ref[rsl, up] = (out[:, group - LANES:] + extra).astype(o_ref.dtype)


def _row_window(starts, rows, k):
    def index(j, i):
        start = starts[-1]
        for t in range(len(starts) - 2, -1, -1):
            start = jnp.where(j == t, starts[t], start)
        return pl.multiple_of(start, SUBLANES), 0
    assert all(s % SUBLANES == 0 for s in starts)
    return pl.BlockSpec((pl.Element(rows), pl.Element(k)), index)


def _project(a, wt, starts, tn_in, gain, flag, out_dtype, tm, name,
             pad_from=0, pad_to=0, group=0, count=1, aug="", seq=0, aug_table=None):
    m, k = a.shape
    n_tiles = len(starts)
    tn_out = tn_in if pad_from == pad_to else tn_in // pad_from * pad_to
    n_out = n_tiles * tn_out
    if gain is None:
        gain = jnp.ones((n_out,), F32)
        flag = jnp.zeros((n_out,), F32)
    assert not aug or seq % tm == 0
    operands = [a, wt, gain.reshape(1, n_out), flag.reshape(1, n_out)]
    in_specs = [pl.BlockSpec((tm, k), lambda j, i: (i, 0)),
                _row_window(starts, tn_in, k),
                pl.BlockSpec((1, tn_out), lambda j, i: (0, j)),
                pl.BlockSpec((1, tn_out), lambda j, i: (0, j))]
    if aug == "query":
        operands.append(aug_table)
        in_specs.append(pl.BlockSpec((2, tn_out), lambda j, i: (0, j)))
    return pl.pallas_call(
        functools.partial(_proj_kernel, pad_from=pad_from, pad_to=pad_to, group=group,
                          count=count, aug=aug, seq=seq),
        grid=(n_tiles, m // tm),
        in_specs=in_specs,
        out_specs=pl.BlockSpec((tm, tn_out), lambda j, i: (i, j)),
        out_shape=jax.ShapeDtypeStruct((m, n_out), out_dtype),
        scratch_shapes=[pltpu.VMEM((k, tn_out), BF16)],
        compiler_params=_cparams(("arbitrary", "arbitrary")),
        name=name,
    )(*operands)


def _out_norm_kernel(a_ref, w_ref, r_ref, g_ref, h_ref, hn_ref, wb_ref):
    @pl.when(pl.program_id(0) == 0)
    def _():
        _stage_weight(w_ref, wb_ref)

    h = r_ref[...] + jnp.dot(a_ref[...], wb_ref[...], preferred_element_type=F32)
    h_ref[...] = h
    ms = jnp.mean(h * h, axis=-1, keepdims=True)
    hn_ref[...] = (h * lax.rsqrt(ms + RMS_EPS) * g_ref[...]).astype(hn_ref.dtype)


def _out_proj_norm(a, w, res, gain, tm=512):
    m, k = a.shape
    n = w.shape[1]
    full = lambda i: (0, 0)
    row = lambda i: (i, 0)
    return pl.pallas_call(
        _out_norm_kernel,
        grid=(m // tm,),
        in_specs=[pl.BlockSpec((tm, k), row),
                  pl.BlockSpec((k, n), full, pipeline_mode=pl.Buffered(1)),
                  pl.BlockSpec((tm, n), row), pl.BlockSpec((1, n), full)],
        out_specs=[pl.BlockSpec((tm, n), row), pl.BlockSpec((tm, n), row)],
        out_shape=[jax.ShapeDtypeStruct((m, n), F32), jax.ShapeDtypeStruct((m, n), BF16)],
        scratch_shapes=[pltpu.VMEM((k, n), BF16)],
        compiler_params=_cparams(("arbitrary",)),
        name="out_proj_norm",
    )(a, w, res, gain.reshape(1, n))


def _mm_res_kernel(a_ref, w_ref, r_ref, o_ref, wb_ref):
    @pl.when(_first_m_step())
    def _():
        _stage_weight(w_ref, wb_ref)

    o_ref[...] = r_ref[...] + jnp.dot(a_ref[...], wb_ref[...], preferred_element_type=F32)


def _matmul_residual(a, w, res, tm, tn, name):
    m, k = a.shape
    n = w.shape[1]
    return pl.pallas_call(
        _mm_res_kernel,
        grid=(n // tn, m // tm),
        in_specs=[pl.BlockSpec((tm, k), lambda j, i: (i, 0)),
                  pl.BlockSpec((k, tn), lambda j, i: (0, j)),
                  pl.BlockSpec((tm, tn), lambda j, i: (i, j))],
        out_specs=pl.BlockSpec((tm, tn), lambda j, i: (i, j)),
        out_shape=jax.ShapeDtypeStruct((m, n), F32),
        scratch_shapes=[pltpu.VMEM((k, tn), BF16)],
        compiler_params=_cparams(("arbitrary", "arbitrary")),
        name=name,
    )(a, w, res)


def _merge_kernel(xn_ref, oa_ref, ob_ref, wga_ref, wgb_ref, wuf_ref, wun_ref, o_ref,
                  bga_ref, bgb_ref, buf_ref, bun_ref):
    @pl.when(_first_m_step())
    def _():
        _stage_weight(wga_ref, bga_ref, True)
        _stage_weight(wgb_ref, bgb_ref, True)
        _stage_weight(wuf_ref, buf_ref)
        _stage_weight(wun_ref, bun_ref)

    xn = xn_ref[...]
    ga = jax.nn.sigmoid(jnp.dot(xn, bga_ref[...], preferred_element_type=F32))
    ua = jnp.dot(oa_ref[...], buf_ref[...], preferred_element_type=F32)
    acc = ga * ua
    gb = jax.nn.sigmoid(jnp.dot(xn, bgb_ref[...], preferred_element_type=F32))
    ub = jnp.dot(ob_ref[...], bun_ref[...], preferred_element_type=F32)
    o_ref[...] = (acc + gb * ub).astype(o_ref.dtype)


def _merge(xn, oa, ob, wt, row_a, row_b, wuf, wun, tm=512, tn=512):
    m, d = xn.shape
    n = wuf.shape[1]
    ka = oa.shape[1]
    kb = ob.shape[1]
    row = lambda j, i: (i, 0)
    col = lambda j, i: (0, j)
    return pl.pallas_call(
        _merge_kernel,
        grid=(n // tn, m // tm),
        in_specs=[pl.BlockSpec((tm, d), row), pl.BlockSpec((tm, ka), row),
                  pl.BlockSpec((tm, kb), row),
                  _row_window([row_a + t * tn for t in range(n // tn)], tn, d),
                  _row_window([row_b + t * tn for t in range(n // tn)], tn, d),
                  pl.BlockSpec((ka, tn), col), pl.BlockSpec((kb, tn), col)],
        out_specs=pl.BlockSpec((tm, tn), lambda j, i: (i, j)),
        out_shape=jax.ShapeDtypeStruct((m, n), BF16),
        scratch_shapes=[pltpu.VMEM((d, tn), BF16), pltpu.VMEM((d, tn), BF16),
                        pltpu.VMEM((ka, tn), BF16), pltpu.VMEM((kb, tn), BF16)],
        compiler_params=_cparams(("arbitrary", "arbitrary")),
        name="gated_merge",
    )(xn, oa, ob, wt, wt, wuf, wun)


def _swiglu_kernel(a_ref, wg_ref, wu_ref, o_ref, bg_ref, bu_ref):
    @pl.when(_first_m_step())
    def _():
        _stage_weight(wg_ref, bg_ref)
        _stage_weight(wu_ref, bu_ref)

    a = a_ref[...]
    gt = jnp.dot(a, bg_ref[...], preferred_element_type=F32)
    up = jnp.dot(a, bu_ref[...], preferred_element_type=F32)
    o_ref[...] = (gt * jax.nn.sigmoid(gt) * up).astype(o_ref.dtype)


def _swiglu(a, wg, wu, tm=1024, tn=512):
    m, k = a.shape
    n = wg.shape[1]
    return pl.pallas_call(
        _swiglu_kernel,
        grid=(n // tn, m // tm),
        in_specs=[pl.BlockSpec((tm, k), lambda j, i: (i, 0)),
                  pl.BlockSpec((k, tn), lambda j, i: (0, j)),
                  pl.BlockSpec((k, tn), lambda j, i: (0, j))],
        out_specs=pl.BlockSpec((tm, tn), lambda j, i: (i, j)),
        out_shape=jax.ShapeDtypeStruct((m, n), BF16),
        scratch_shapes=[pltpu.VMEM((k, tn), BF16), pltpu.VMEM((k, tn), BF16)],
        compiler_params=_cparams(("arbitrary", "arbitrary")),
        name="swiglu_up",
    )(a, wg, wu)


def _split3(x):
    hi = x.astype(BF16)
    r1 = x - hi.astype(F32)
    mid = r1.astype(BF16)
    lo = (r1 - mid.astype(F32)).astype(BF16)
    return hi, mid, lo


def _decay_kernel(z_ref, b_ref, crep_ref, crow_ref, *, blk):
    t = z_ref.shape[0]
    r = lax.broadcasted_iota(jnp.int32, (blk, blk), 0)
    c = lax.broadcasted_iota(jnp.int32, (blk, blk), 1)
    tri = jnp.where(r >= c, 1.0, 0.0).astype(BF16)
    lane = lax.broadcasted_iota(jnp.int32, (blk, LANES), 1)
    carry = jnp.zeros((1, LANES), F32)
    for s in range(t // blk):
        rows = slice(s * blk, (s + 1) * blk)
        z = z_ref[rows, :] + b_ref[...]
        logf = (jnp.minimum(z, 0.0) - jnp.log1p(jnp.exp(-jnp.abs(z)))) * LOG2E
        hi, mid, lo = _split3(logf)
        cb = (jnp.dot(tri, hi, preferred_element_type=F32)
              + jnp.dot(tri, mid, preferred_element_type=F32)
              + jnp.dot(tri, lo, preferred_element_type=F32)) + carry
        carry = cb[blk - 1:blk, :]
        crow_ref[0, :, rows] = cb.T[:FOX_HEADS, :]
        for h in range(FOX_HEADS):
            col = jnp.sum(jnp.where(lane == h, cb, 0.0), axis=-1, keepdims=True)
            crep_ref[0, h, rows, :] = jnp.broadcast_to(col, (blk, LANES))


def _decay(p3, bias_row, batch, seq, blk=256):
    return pl.pallas_call(
        functools.partial(_decay_kernel, blk=blk),
        grid=(batch,),
        in_specs=[pl.BlockSpec((seq, LANES), lambda b: (b, 6)),
                  pl.BlockSpec((1, LANES), lambda b: (0, 0))],
        out_specs=[pl.BlockSpec((1, FOX_HEADS, seq, LANES), lambda b: (b, 0, 0, 0)),
                   pl.BlockSpec((1, FOX_HEADS, seq), lambda b: (b, 0, 0))],
        out_shape=[jax.ShapeDtypeStruct((batch, FOX_HEADS, seq, LANES), F32),
                   jax.ShapeDtypeStruct((batch, FOX_HEADS, seq), F32)],
        compiler_params=_cparams(("arbitrary",)),
        name="fox_decay_cumsum",
    )(p3, bias_row)


def _transpose_bf16(x):
    return x.astype(F32).T.astype(BF16)


def _normalised(acc, d):
    return acc[:d, :] * (1.0 / jnp.maximum(acc[d:d + 1, :], 1e-30))


def _fox_kernel(q_ref, k_ref, v_ref, crep_ref, crow_ref, o_ref, vt_ref, *, tq, nq, nh):
    i = pl.program_id(2)
    dh = FOX_HEAD_DIM

    @pl.when(i == 0)
    def _():
        for h in range(nh):
            vt_ref[h, 0:dh, :] = _transpose_bf16(v_ref[:, h * dh:(h + 1) * dh])
            vt_ref[h, dh:, :] = jnp.ones((vt_ref.shape[1] - dh, vt_ref.shape[2]), BF16)

    qts = [_transpose_bf16(q_ref[:, h * dh:(h + 1) * dh]) for h in range(nh)]
    cis = [crow_ref[0, h] for h in range(nh)]
    reps = tq // LANES
    rk = lax.broadcasted_iota(jnp.int32, (tq, tq), 0)
    cq = lax.broadcasted_iota(jnp.int32, (tq, tq), 1)

    def scores(h, k0):
        cj = crep_ref[0, h, k0:k0 + tq, :]
        st = jnp.dot(k_ref[k0:k0 + tq, h * dh:(h + 1) * dh], qts[h],
                     preferred_element_type=F32)
        return st + cis[h] - jnp.concatenate([cj] * reps, axis=1)

    def variant(n):
        def run():
            starts = [t * tq for t in range(n, -1, -1)]
            tiles = [[jnp.where(rk <= cq, scores(h, starts[0]), NEG)]
                     + [scores(h, k0) for k0 in starts[1:]] for h in range(nh)]
            ms = [jnp.max(tiles[h][0], axis=0, keepdims=True) for h in range(nh)]
            accs = [jnp.dot(vt_ref[h, :, starts[0]:starts[0] + tq],
                            jnp.exp2(tiles[h][0] - ms[h]).astype(BF16),
                            preferred_element_type=F32) for h in range(nh)]
            for t in range(1, n + 1):
                k0 = starts[t]
                for h in range(nh):
                    st = tiles[h][t]
                    m_new = jnp.maximum(ms[h], jnp.max(st, axis=0, keepdims=True))
                    accs[h] = jnp.exp2(ms[h] - m_new) * accs[h] + jnp.dot(
                        vt_ref[h, :, k0:k0 + tq], jnp.exp2(st - m_new).astype(BF16),
                        preferred_element_type=F32)
                    ms[h] = m_new
            for h in range(nh):
                o_ref[:, h * dh:(h + 1) * dh] = _normalised(accs[h], dh).T.astype(o_ref.dtype)
        return run

    lax.switch(i, [variant(n) for n in range(nq)])


def _fox_attention(p1, crep, crow4, batch, seq, tq=512, nh=4):
    nq = seq // tq
    hg = FOX_HEADS // nh
    w = nh * FOX_HEAD_DIM
    return pl.pallas_call(
        functools.partial(_fox_kernel, tq=tq, nq=nq, nh=nh),
        grid=(batch, hg, nq),
        in_specs=[pl.BlockSpec((tq, w), lambda b, hh, i: (b * nq + i, hh)),
                  pl.BlockSpec((seq, w), lambda b, hh, i: (b, hg + hh)),
                  pl.BlockSpec((seq, w), lambda b, hh, i: (b, 2 * hg + hh)),
                  pl.BlockSpec((1, nh, seq, LANES), lambda b, hh, i: (b, hh, 0, 0)),
                  pl.BlockSpec((1, nh, 1, tq), lambda b, hh, i: (b, hh, 0, i))],
        out_specs=pl.BlockSpec((tq, w), lambda b, hh, i: (b * nq + i, hh)),
        out_shape=jax.ShapeDtypeStruct((batch * seq, FOX_WIDTH), BF16),
        scratch_shapes=[pltpu.VMEM((nh, FOX_HEAD_DIM + BF16_SUBLANES, seq), BF16)],
        compiler_params=_cparams(("arbitrary", "arbitrary", "arbitrary")),
        name="fox_attention",
    )(p1, p1, p1, crep, crow4)


def _compress_one(z_refs, pe_ref, w1_ref, w2_ref, nblk):
    half = CMP_BLOCK // 2
    first = jnp.zeros((nblk, CMP_HIDDEN), F32)
    second = jnp.zeros((nblk, CMP_HIDDEN), F32)
    for p in range(half):
        rows = pl.ds(p, nblk, stride=CMP_STRIDE)
        zp = [z_ref[rows, :] for z_ref in z_refs]
        zp = zp[0] if len(zp) == 1 else jnp.concatenate(zp, axis=1)
        first += jnp.dot((zp + pe_ref[p:p + 1, :]).astype(BF16), w1_ref[p],
                         preferred_element_type=F32)
        second += jnp.dot((zp + pe_ref[half + p:half + p + 1, :]).astype(BF16),
                          w1_ref[half + p], preferred_element_type=F32)
    hid = first + pltpu.roll(second, nblk - 1, 0)
    act = (hid * jax.nn.sigmoid(hid)).astype(BF16)
    return jnp.dot(act, w2_ref[...], preferred_element_type=F32)


def _compress_kernel(zk0_ref, zk1_ref, zv_ref, pek_ref, w1k_ref, w2k_ref, gk_ref,
                     pev_ref, w1v_ref, w2v_ref, kc_ref, vc_ref, *, nblk):
    kc = _compress_one((zk0_ref, zk1_ref), pek_ref, w1k_ref, w2k_ref, nblk)
    ms = jnp.sum(kc * kc, axis=-1, keepdims=True) * (1.0 / NSA_QK_DIM)
    kc = kc * lax.rsqrt(ms + RMS_EPS) * gk_ref[...]
    pos = CMP_STRIDE * lax.broadcasted_iota(jnp.int32, (nblk, LANES), 0) + (CMP_BLOCK - 1)
    up = NSA_QK_PAD - LANES
    kc_ref[:, :up] = kc[:, :up].astype(kc_ref.dtype)
    kc_ref[:, up:] = (kc[:, up:] + _key_aug(pos)).astype(kc_ref.dtype)
    vc = _compress_one((zv_ref,), pev_ref, w1v_ref, w2v_ref, nblk)
    vc_ref[...] = vc.T.astype(vc_ref.dtype)


def _compress(p3, pek, w1k, w2k, gk, pev, w1v, w2v, batch, seq):
    g = NSA_KV_GROUPS
    nblk = seq // CMP_STRIDE
    full2 = lambda b, gg: (0, 0)
    full3 = lambda b, gg: (0, 0, 0)
    return pl.pallas_call(
        functools.partial(_compress_kernel, nblk=nblk),
        grid=(batch, g),
        in_specs=[pl.BlockSpec((seq, LANES), lambda b, gg: (b, 2 * gg)),
                  pl.BlockSpec((seq, LANES), lambda b, gg: (b, 2 * gg + 1)),
                  pl.BlockSpec((seq, NSA_V_DIM), lambda b, gg: (b, 4 + gg)),
                  pl.BlockSpec(pek.shape, full2), pl.BlockSpec(w1k.shape, full3),
                  pl.BlockSpec(w2k.shape, full2), pl.BlockSpec(gk.shape, full2),
                  pl.BlockSpec(pev.shape, full2), pl.BlockSpec(w1v.shape, full3),
                  pl.BlockSpec(w2v.shape, full2)],
        out_specs=[pl.BlockSpec((nblk, NSA_QK_PAD), lambda b, gg: (b * g + gg, 0)),
                   pl.BlockSpec((NSA_V_DIM, nblk), lambda b, gg: (b * g + gg, 0))],
        out_shape=[jax.ShapeDtypeStruct((batch * g * nblk, NSA_QK_PAD), BF16),
                   jax.ShapeDtypeStruct((batch * g * NSA_V_DIM, nblk), BF16)],
        compiler_params=_cparams(("arbitrary", "arbitrary")),
        name="nsa_compress",
    )(p3, p3, p3, pek, w1k, w2k, gk, pev, w1v, w2v)


def _q_heads_t(q_ref):
    qb = q_ref[...]
    return jnp.concatenate(
        [_transpose_bf16(qb[:, hh * NSA_QK_PAD:(hh + 1) * NSA_QK_PAD]) for hh in range(NSA_HPG)],
        axis=1)


def _gate_rows(glt_ref, g, hh):
    base = FOX_HEADS + (g * NSA_HPG + hh) * 3
    return [glt_ref[pl.ds(base + br, 1), :] for br in range(3)]


def _nsa_select_kernel(q_ref, kc_ref, vct_ref, gl_ref, ovt_ref,
                       sel_ref, ocmp_ref, glt_ref, *, tq, n_cmp, n_sel):
    g = pl.program_id(1)
    t0 = pl.program_id(2) * tq
    hpg = NSA_HPG
    dv = NSA_V_DIM
    q4t = _q_heads_t(q_ref)
    rk = lax.broadcasted_iota(jnp.int32, (LANES, tq), 0)
    cq = lax.broadcasted_iota(jnp.int32, (LANES, tq), 1)

    s_c = jnp.dot(kc_ref[...], q4t, preferred_element_type=F32)
    dist_c = (t0 + cq) - (CMP_STRIDE * rk + (CMP_BLOCK - 1))
    mask_c = jnp.where(rk < n_cmp, dist_c, -1) >= 0
    probs = []
    p_sum = jnp.zeros((LANES, tq), F32)
    for hh in range(hpg):
        sm = jnp.where(mask_c, s_c[:, hh * tq:(hh + 1) * tq], NEG)
        m = jnp.max(sm, axis=0, keepdims=True)
        e = jnp.where(mask_c, jnp.exp2(sm - m), 0.0)
        p = e * (1.0 / jnp.maximum(jnp.sum(e, axis=0, keepdims=True), 1e-30))
        probs.append(p)
        p_sum = p_sum + p
    o_cmp = jnp.dot(vct_ref[...], jnp.concatenate(probs, axis=1).astype(BF16),
                    preferred_element_type=F32)

    ph = p_sum.astype(BF16)
    plo = (p_sum - ph.astype(F32)).astype(BF16)
    ovt = ovt_ref[...]
    imp = (jnp.dot(ovt, ph, preferred_element_type=F32)
           + jnp.dot(ovt, plo, preferred_element_type=F32))[:n_sel, :]

    rj = lax.broadcasted_iota(jnp.int32, (n_sel, tq), 0)
    tcol = t0 + lax.broadcasted_iota(jnp.int32, (n_sel, tq), 1)
    back = (tcol >> (SEL_BLOCK.bit_length() - 1)) - rj
    elig = back >= 0
    forced = jnp.where(rj == 0, 0, jnp.where(elig, back, SEL_LOCAL)) < SEL_LOCAL
    score = jnp.where(elig, jnp.where(forced, FORCE_SCORE, imp), -1.0)
    rank = jnp.zeros((n_sel, tq), F32)
    for jp in range(n_sel):
        row = score[jp:jp + 1, :]
        later = jnp.where(rj > jp, 1.0, 0.0)
        rank = rank + jnp.where(row > score, 1.0, jnp.where(row == score, later, 0.0))
    sel_ref[0:n_sel, :] = jnp.where(elig, jnp.where(rank < SEL_TOPK, 0.0, NEG), NEG)
    sel_ref[n_sel:, :] = jnp.full((LANES - n_sel, tq), NEG, F32)

    glt_ref[...] = jax.nn.sigmoid(gl_ref[...]).T
    for hh in range(hpg):
        gate = _gate_rows(glt_ref, g, hh)[0]
        ocmp_ref[:, hh * dv:(hh + 1) * dv] = (gate * o_cmp[:, hh * tq:(hh + 1) * tq]).T


def _nsa_select(pq, p3, kc, vct, ovt, batch, seq, tq=512):
    g = NSA_KV_GROUPS
    nq = seq // tq
    nblk = seq // CMP_STRIDE
    n_cmp = nblk - CMP_BLOCK // CMP_STRIDE + 1
    return pl.pallas_call(
        functools.partial(_nsa_select_kernel, tq=tq, n_cmp=n_cmp, n_sel=seq // SEL_BLOCK),
        grid=(batch, g, nq),
        in_specs=[
            pl.BlockSpec((tq, NSA_HPG * NSA_QK_PAD), lambda b, gg, i: (b * nq + i, gg)),
            pl.BlockSpec((nblk, NSA_QK_PAD), lambda b, gg, i: (b * g + gg, 0)),
            pl.BlockSpec((NSA_V_DIM, nblk), lambda b, gg, i: (b * g + gg, 0)),
            pl.BlockSpec((tq, LANES), lambda b, gg, i: (b * nq + i, 6)),
            pl.BlockSpec(ovt.shape, lambda b, gg, i: (0, 0)),
        ],
        out_specs=[pl.BlockSpec((LANES, tq), lambda b, gg, i: (b * g + gg, i)),
                   pl.BlockSpec((tq, NSA_HPG * NSA_V_DIM), lambda b, gg, i: (b * nq + i, gg))],
        out_shape=[jax.ShapeDtypeStruct((batch * g * LANES, seq), F32),
                   jax.ShapeDtypeStruct((batch * seq, NSA_WIDTH), F32)],
        scratch_shapes=[pltpu.VMEM((LANES, tq), F32)],
        compiler_params=_cparams(("arbitrary", "arbitrary", "arbitrary")),
        name="nsa_select",
    )(pq, kc, vct, p3, ovt)


def _nsa_attend_kernel(q_ref, ks_ref, vs_ref, kw_ref, vw_ref, gl_ref, sel_ref,
                       ocmp_ref, o_ref, vst_ref, vwt_ref, glt_ref, *, tq, n_var):
    i = pl.program_id(1)
    t0 = i * tq
    hpg = NSA_HPG
    ng = NSA_KV_GROUPS
    dv = NSA_V_DIM
    dk = NSA_QK_PAD

    @pl.when(i == 0)
    def _():
        ones = jnp.ones((vst_ref.shape[1] - dv, vst_ref.shape[2]), BF16)
        for gg in range(ng):
            vst_ref[gg, 0:dv, :] = _transpose_bf16(vs_ref[:, gg * dv:(gg + 1) * dv])
            vst_ref[gg, dv:, :] = ones
            vwt_ref[gg, 0:dv, :] = _transpose_bf16(vw_ref[:, gg * dv:(gg + 1) * dv])
            vwt_ref[gg, dv:, :] = ones

    qb = q_ref[...]
    q4t = [jnp.concatenate(
        [_transpose_bf16(qb[:, (gg * hpg + hh) * dk:(gg * hpg + hh + 1) * dk])
         for hh in range(hpg)], axis=1) for gg in range(ng)]
    glt_ref[...] = jax.nn.sigmoid(gl_ref[...]).T

    def distance(k0, rows):
        return (lax.broadcasted_iota(jnp.int32, (rows, tq), 1)
                - lax.broadcasted_iota(jnp.int32, (rows, tq), 0)) + (t0 - k0)

    def scores(gg, k, mask_bias):
        st = jnp.dot(k, q4t[gg], preferred_element_type=F32)
        return st + jnp.concatenate([mask_bias] * hpg, axis=1)

    def selection_bias(gg, k0):
        j0 = gg * LANES + k0 // SEL_BLOCK
        return jnp.concatenate(
            [jnp.broadcast_to(sel_ref[j0 + j:j0 + j + 1, :], (SEL_BLOCK, tq))
             for j in range(SLC_TILE // SEL_BLOCK)], axis=0)

    def variant(n):
        def run():
            kw0 = pl.multiple_of(jnp.maximum(t0 - WINDOW, 0), LANES)
            wrows = WINDOW + tq
            dist_w = distance(kw0, wrows)
            bias_w = jnp.where(jnp.where(dist_w >= 0, dist_w, WINDOW) < WINDOW, 0.0, NEG)
            o_win = []
            for gg in range(ng):
                sw = scores(gg, kw_ref[pl.ds(kw0, wrows), gg * dk:(gg + 1) * dk], bias_w)
                pw = jnp.exp2(sw - jnp.max(sw, axis=0, keepdims=True))
                o_win.append(_normalised(
                    jnp.dot(vwt_ref[gg, :, pl.ds(kw0, wrows)], pw.astype(BF16),
                            preferred_element_type=F32), dv))

            starts = [t * SLC_TILE for t in range(n, -1, -1)]
            causal = distance(starts[0], SLC_TILE) >= 0
            tiles = [[] for _ in range(ng)]
            for t, k0 in enumerate(starts):
                for gg in range(ng):
                    bias = selection_bias(gg, k0)
                    if t == 0:
                        bias = jnp.where(causal, bias, NEG)
                    tiles[gg].append(scores(
                        gg, ks_ref[k0:k0 + SLC_TILE, gg * dk:(gg + 1) * dk], bias))
            ms = [jnp.max(tiles[gg][0], axis=0, keepdims=True) for gg in range(ng)]
            accs = [jnp.dot(vst_ref[gg, :, starts[0]:starts[0] + SLC_TILE],
                            jnp.exp2(tiles[gg][0] - ms[gg]).astype(BF16),
                            preferred_element_type=F32) for gg in range(ng)]
            for t in range(1, n + 1):
                k0 = starts[t]
                for gg in range(ng):
                    st = tiles[gg][t]
                    m_new = jnp.maximum(ms[gg], jnp.max(st, axis=0, keepdims=True))
                    accs[gg] = jnp.exp2(ms[gg] - m_new) * accs[gg] + jnp.dot(
                        vst_ref[gg, :, k0:k0 + SLC_TILE], jnp.exp2(st - m_new).astype(BF16),
                        preferred_element_type=F32)
                    ms[gg] = m_new

            for gg in range(ng):
                o_slc = _normalised(accs[gg], dv)
                for hh in range(hpg):
                    _, g_slc, g_win = _gate_rows(glt_ref, gg, hh)
                    lanes = slice(hh * tq, (hh + 1) * tq)
                    out = g_slc * o_slc[:, lanes] + g_win * o_win[gg][:, lanes]
                    cols = slice((gg * hpg + hh) * dv, (gg * hpg + hh + 1) * dv)
                    o_ref[:, cols] = (ocmp_ref[:, cols] + out.T).astype(o_ref.dtype)
        return run

    lax.switch(t0 // SLC_TILE, [variant(n) for n in range(n_var)])


def _nsa_attend(pq, pk, pv, p3, sel, ocmp, batch, seq):
    tq = LANES
    g = NSA_KV_GROUPS
    nq = seq // tq
    vrows = NSA_V_DIM + BF16_SUBLANES
    return pl.pallas_call(
        functools.partial(_nsa_attend_kernel, tq=tq, n_var=seq // SLC_TILE),
        grid=(batch, nq),
        in_specs=[
            pl.BlockSpec((tq, NSA_HEADS * NSA_QK_PAD), lambda b, i: (b * nq + i, 0)),
            pl.BlockSpec((seq, g * NSA_QK_PAD), lambda b, i: (b, 0)),
            pl.BlockSpec((seq, g * NSA_V_DIM), lambda b, i: (b, 0)),
            pl.BlockSpec((seq, g * NSA_QK_PAD), lambda b, i: (b, 1)),
            pl.BlockSpec((seq, g * NSA_V_DIM), lambda b, i: (b, 1)),
            pl.BlockSpec((tq, LANES), lambda b, i: (b * nq + i, 6)),
            pl.BlockSpec((g * LANES, tq), lambda b, i: (b, i)),
            pl.BlockSpec((tq, NSA_WIDTH), lambda b, i: (b * nq + i, 0)),
        ],
        out_specs=pl.BlockSpec((tq, NSA_WIDTH), lambda b, i: (b * nq + i, 0)),
        out_shape=jax.ShapeDtypeStruct((batch * seq, NSA_WIDTH), BF16),
        scratch_shapes=[pltpu.VMEM((g, vrows, seq), BF16), pltpu.VMEM((g, vrows, seq), BF16),
                        pltpu.VMEM((LANES, tq), F32)],
        compiler_params=_cparams(("arbitrary", "arbitrary")),
        name="nsa_attend",
    )(pq, pk, pv, pk, pv, p3, sel, ocmp)


def _pad_head_rows(wt, heads):
    k = wt.shape[1]
    wt = wt.reshape(heads, NSA_QK_DIM, k)
    wt = jnp.pad(wt, ((0, 0), (0, NSA_QK_PAD - NSA_QK_DIM), (0, 0)))
    return wt.reshape(heads * NSA_QK_PAD, k)


def _pad_gain(gain, scale=1.0):
    return jnp.pad(gain * scale, (0, NSA_QK_PAD - NSA_QK_DIM))


def _overlap_matrix(nc, ns):
    i = np.arange(nc)[:, None]
    j = np.arange(ns)[None, :]
    lo = np.maximum(i * CMP_STRIDE, j * SEL_BLOCK)
    hi = np.minimum(i * CMP_STRIDE + CMP_BLOCK, (j + 1) * SEL_BLOCK)
    return (np.maximum(hi - lo, 0) / CMP_STRIDE).astype(np.float32)


def kernel(x, norm_attn, w_in, fox_f_bias, fox_q_gain, fox_k_gain,
           nsa_q_gain, nsa_kc_gain, nsa_ks_gain, nsa_kw_gain,
           cmp_pe_k, cmp_w1_k, cmp_w2_k, cmp_pe_v, cmp_w1_v, cmp_w2_v,
           w_up_fox, w_up_nsa, w_out, norm_ffn, w_ffn_gate, w_ffn_up, w_ffn_down):
    batch, seq, d = x.shape
    m = batch * seq
    depth = w_in.shape[0]
    pts = [0] + [int(p) for p in np.cumsum(IN_SPLITS)]
    nblk = seq // CMP_STRIDE
    n_cmp = nblk - CMP_BLOCK // CMP_STRIDE + 1
    ns = seq // SEL_BLOCK

    slope = jnp.exp2(-8.0 * jnp.arange(1, NSA_HEADS + 1, dtype=F32) / NSA_HEADS) * LOG2E
    s1, s2, s3 = [p.astype(F32) for p in _split3(slope)]
    q_const = jnp.stack([256.0 * s1, 256.0 * s2, 256.0 * s3, s1, s2, s3], axis=1)
    lane0 = NSA_QK_DIM
    aug_q = jnp.zeros((2, NSA_HEADS, NSA_QK_PAD), F32)
    aug_q = aug_q.at[0, :, lane0:lane0 + 6].set(q_const)
    aug_q = aug_q.at[1, :, lane0 + 6:lane0 + 9].set(-slope[:, None])
    aug_q = aug_q.reshape(2, NSA_HEADS * NSA_QK_PAD)
    ovt_np = np.zeros((LANES, nblk), np.float32)
    ovt_np[:ns, :n_cmp] = _overlap_matrix(n_cmp, ns).T
    ovt = jnp.asarray(ovt_np, BF16)

    w_in_t = jnp.swapaxes(w_in, 1, 2)

    xf = x.reshape(m, d)
    for l in range(depth):
        wt = w_in_t[l]
        row = dict(zip(("fq", "fk", "fv", "fl", "nq", "kc", "vc", "ks", "vs", "kw", "vw", "ng",
                        "ga", "gb"), pts))

        gain1 = jnp.concatenate([jnp.tile(fox_q_gain[l] * (FOX_HEAD_DIM ** -0.5 * LOG2E), FOX_HEADS),
                                 jnp.tile(fox_k_gain[l], FOX_HEADS),
                                 jnp.ones((FOX_WIDTH,), F32)])
        flag1 = jnp.concatenate([jnp.ones((2 * FOX_WIDTH,), F32), jnp.zeros((FOX_WIDTH,), F32)])
        gain_q = jnp.tile(_pad_gain(nsa_q_gain[l], NSA_QK_DIM ** -0.5 * LOG2E), NSA_HEADS)
        gain_k = jnp.concatenate([jnp.tile(_pad_gain(nsa_ks_gain[l]), NSA_KV_GROUPS),
                                  jnp.tile(_pad_gain(nsa_kw_gain[l]), NSA_KV_GROUPS)])
        n_small = FOX_HEADS + 3 * NSA_HEADS
        w3 = jnp.concatenate([_pad_head_rows(wt[row["kc"]:row["vc"]], NSA_KV_GROUPS),
                              wt[row["vc"]:row["ks"]], wt[row["fl"]:row["nq"]],
                              wt[row["ng"]:row["ga"]],
                              jnp.zeros((LANES - n_small, d), F32)], axis=0)

        xn = _rmsnorm(xf, norm_attn[l])
        q_tile = NSA_HPG * NSA_QK_DIM
        p1 = _project(xn, wt, [t * 1024 for t in range(3 * FOX_WIDTH // 1024)], 1024, gain1, flag1,
                      BF16, 1024, "proj_fox", group=128, count=128)
        pq = _project(xn, wt, [row["nq"], row["nq"] + q_tile], q_tile, gain_q,
                      jnp.ones_like(gain_q), BF16, 1024, "proj_nsa_q", pad_from=NSA_QK_DIM,
                      pad_to=NSA_QK_PAD, group=NSA_QK_PAD, count=NSA_QK_DIM,
                      aug="query", seq=seq, aug_table=aug_q)
        pk = _project(xn, wt, [row["ks"], row["kw"]], KV_K, gain_k, jnp.ones_like(gain_k), BF16,
                      1024, "proj_nsa_k", pad_from=NSA_QK_DIM, pad_to=NSA_QK_PAD,
                      group=NSA_QK_PAD, count=NSA_QK_DIM, aug="key", seq=seq)
        pv = _project(xn, wt, [row["vs"], row["vw"]], KV_V, None, None, BF16, 1024, "proj_nsa_v")
        p3 = _project(xn, w3, [0], w3.shape[0], None, None, F32, 1024, "proj_f32")

        bias_row = jnp.pad(fox_f_bias[l], (0, LANES - FOX_HEADS)).reshape(1, LANES)
        crep, crow = _decay(p3, bias_row, batch, seq)
        o_a = _fox_attention(p1, crep, crow.reshape(batch, FOX_HEADS, 1, seq), batch, seq)

        pad_d = NSA_QK_PAD - NSA_QK_DIM
        pek = jnp.pad(cmp_pe_k[l], ((0, 0), (0, pad_d)))
        w1k = jnp.pad(cmp_w1_k[l].reshape(CMP_BLOCK, NSA_QK_DIM, CMP_HIDDEN),
                      ((0, 0), (0, pad_d), (0, 0))).astype(BF16)
        w2k = jnp.pad(cmp_w2_k[l], ((0, 0), (0, pad_d))).astype(BF16)
        gk = _pad_gain(nsa_kc_gain[l]).reshape(1, NSA_QK_PAD)
        w1v = cmp_w1_v[l].reshape(CMP_BLOCK, NSA_V_DIM, CMP_HIDDEN).astype(BF16)
        w2v = cmp_w2_v[l].astype(BF16)
        kc, vct = _compress(p3, pek, w1k, w2k, gk, cmp_pe_v[l], w1v, w2v, batch, seq)
        sel, ocmp = _nsa_select(pq, p3, kc, vct, ovt, batch, seq)
        o_b = _nsa_attend(pq, pk, pv, p3, sel, ocmp, batch, seq)

        merged = _merge(xn, o_a, o_b, wt, row["ga"], row["gb"], w_up_fox[l], w_up_nsa[l])
        hres, hn = _out_proj_norm(merged, w_out[l], xf, norm_ffn[l])

        act = _swiglu(hn, w_ffn_gate[l], w_ffn_up[l])
        xf = _matmul_residual(act, w_ffn_down[l], hres, 512, 512, "ffn_down")
    return xf.reshape(batch, seq, d)
```

```python
import functools

import numpy as np
import jax
import jax.numpy as jnp
from jax import lax
from jax.experimental import pallas as pl
from jax.experimental.pallas import tpu as pltpu

F32 = jnp.float32
BF16 = jnp.bfloat16

D_MODEL = 2048
FOX_HEADS = 8
FOX_HEAD_DIM = 128
FOX_WIDTH = FOX_HEADS * FOX_HEAD_DIM
NSA_HEADS = 8
NSA_KV_GROUPS = 2
NSA_HPG = NSA_HEADS // NSA_KV_GROUPS
NSA_QK_DIM = 192
NSA_QK_PAD = 256
NSA_V_DIM = 128
NSA_WIDTH = NSA_HEADS * NSA_V_DIM
CMP_BLOCK = 32
CMP_STRIDE = 16
CMP_HIDDEN = 256
SEL_BLOCK = 64
SEL_TOPK = 16
SEL_LOCAL = 2
FORCE_SCORE = 1.0e4
WINDOW = 512
KV_K = NSA_KV_GROUPS * NSA_QK_DIM
KV_V = NSA_KV_GROUPS * NSA_V_DIM
D_FF = -(-(8 * D_MODEL) // (3 * 256)) * 256
RMS_EPS = 1e-6
IN_SPLITS = (FOX_WIDTH, FOX_WIDTH, FOX_WIDTH, FOX_HEADS,
             NSA_HEADS * NSA_QK_DIM, KV_K, KV_V, KV_K, KV_V, KV_K, KV_V,
             3 * NSA_HEADS, D_MODEL, D_MODEL)

LANES = 128
SUBLANES = 8
BF16_SUBLANES = 16
NEG = -1.0e30
LOG2E = 1.4426950408889634
SLC_TILE = 512
VMEM_LIMIT = 56 * 1024 * 1024


def _cparams(sem):
    return pltpu.CompilerParams(dimension_semantics=sem, vmem_limit_bytes=VMEM_LIMIT)


def _rms_kernel(x_ref, g_ref, o_ref):
    x = x_ref[...]
    ms = jnp.mean(x * x, axis=-1, keepdims=True)
    o_ref[...] = (x * lax.rsqrt(ms + RMS_EPS) * g_ref[...]).astype(o_ref.dtype)


def _rmsnorm(x, gain, tm=512):
    m, d = x.shape
    return pl.pallas_call(
        _rms_kernel,
        grid=(m // tm,),
        in_specs=[pl.BlockSpec((tm, d), lambda i: (i, 0)),
                  pl.BlockSpec((1, d), lambda i: (0, 0))],
        out_specs=pl.BlockSpec((tm, d), lambda i: (i, 0)),
        out_shape=jax.ShapeDtypeStruct((m, d), BF16),
        compiler_params=_cparams(("arbitrary",)),
        name="rmsnorm",
    )(x, gain.reshape(1, d))


STAGE_CHUNK = 512
EPILOGUE_SPLIT = 4


def _stage_weight(w_ref, wb_ref, transposed=False, pad_from=0, pad_to=0):
    if not transposed:
        wb_ref[...] = w_ref[...].astype(BF16)
        return
    n_in, k = w_ref.shape
    for c in range(k // STAGE_CHUNK):
        cols = slice(c * STAGE_CHUNK, (c + 1) * STAGE_CHUNK)
        w = w_ref[:, cols]
        if pad_from != pad_to:
            zero = jnp.zeros((pad_to - pad_from, STAGE_CHUNK), F32)
            w = jnp.concatenate(
                [piece for h in range(n_in // pad_from)
                 for piece in (w[h * pad_from:(h + 1) * pad_from, :], zero)], axis=0)
        wb_ref[cols, :] = w.T.astype(BF16)


def _first_m_step():
    return pl.program_id(1) == 0


AUG_LANE = NSA_QK_DIM - LANES


def _key_aug(pos):
    lane = lax.broadcasted_iota(jnp.int32, pos.shape, 1)
    hi = (pos >> 8).astype(F32)
    lo = (pos & 255).astype(F32)
    return jnp.where(lane < AUG_LANE, 0.0,
                     jnp.where(lane < AUG_LANE + 3, hi,
                               jnp.where(lane < AUG_LANE + 6, lo,
                                         jnp.where(lane < AUG_LANE + 9, 1.0, 0.0))))


def _query_aug(pos, const_row, slope_row):
    u = pos.astype(F32) * slope_row
    hi = u.astype(BF16).astype(F32)
    r1 = u - hi
    mid = r1.astype(BF16).astype(F32)
    lane = lax.broadcasted_iota(jnp.int32, pos.shape, 1)
    return const_row + jnp.where(lane == AUG_LANE + 6, hi,
                                 jnp.where(lane == AUG_LANE + 7, mid, r1 - mid))


def _proj_kernel(*refs, pad_from, pad_to, group, count, aug, seq):
    if aug == "query":
        a_ref, w_ref, gain_ref, flag_ref, aug_ref, o_ref, wb_ref = refs
    else:
        a_ref, w_ref, gain_ref, flag_ref, o_ref, wb_ref = refs

    @pl.when(_first_m_step())
    def _():
        _stage_weight(w_ref, wb_ref, True, pad_from, pad_to)

    if group == 0:
        o_ref[...] = jnp.dot(a_ref[...], wb_ref[...],
                             preferred_element_type=F32).astype(o_ref.dtype)
        return
    tm = a_ref.shape[0]
    rows = tm // EPILOGUE_SPLIT
    ys = [jnp.dot(a_ref[r * rows:(r + 1) * rows, :], wb_ref[...], preferred_element_type=F32)
          for r in range(EPILOGUE_SPLIT)]
    for r, y in enumerate(ys):
        if aug:
            pos = ((pl.program_id(1) * tm) % seq + r * rows
                   + lax.broadcasted_iota(jnp.int32, (rows, LANES), 0))
        for c in range(y.shape[1] // group):
            sl = slice(c * group, (c + 1) * group)
            yc = y[:, sl]
            ss = jnp.sum(yc * yc, axis=-1, keepdims=True)
            rs = lax.rsqrt(ss * (1.0 / count) + RMS_EPS)
            scale = jnp.where(flag_ref[:, sl] > 0.0, rs, 1.0)
            out = yc * scale * gain_ref[:, sl]
            rsl = slice(r * rows, (r + 1) * rows)
            if not aug:
                o_ref[rsl, sl] = out.astype(o_ref.dtype)
                continue
            up = slice((c + 1) * group - LANES, (c + 1) * group)
            extra = (_key_aug(pos) if aug == "key"
                     else _query_aug(pos, aug_ref[0:1, up], aug_ref[1:2, up]))
            o_ref[rsl, c * group:(c + 1) * group - LANES] = out[:, :group - LANES].astype(o_ref.dtype)
            o_ref[rsl, up] = (out[:, group - LANES:] + extra).astype(o_ref.dtype)


def _row_window(starts, rows, k):
    def index(j, i):
        start = starts[-1]
        for t in range(len(starts) - 2, -1, -1):
            start = jnp.where(j == t, starts[t], start)
        return pl.multiple_of(start, SUBLANES), 0
    assert all(s % SUBLANES == 0 for s in starts)
    return pl.BlockSpec((pl.Element(rows), pl.Element(k)), index)


def _project(a, wt, starts, tn_in, gain, flag, out_dtype, tm, name,
             pad_from=0, pad_to=0, group=0, count=1, aug="", seq=0, aug_table=None):
    m, k = a.shape
    n_tiles = len(starts)
    tn_out = tn_in if pad_from == pad_to else tn_in // pad_from * pad_to
    n_out = n_tiles * tn_out
    if gain is None:
        gain = jnp.ones((n_out,), F32)
        flag = jnp.zeros((n_out,), F32)
    assert not aug or seq % tm == 0
    operands = [a, wt, gain.reshape(1, n_out), flag.reshape(1, n_out)]
    in_specs = [pl.BlockSpec((tm, k), lambda j, i: (i, 0)),
                _row_window(starts, tn_in, k),
                pl.BlockSpec((1, tn_out), lambda j, i: (0, j)),
                pl.BlockSpec((1, tn_out), lambda j, i: (0, j))]
    if aug == "query":
        operands.append(aug_table)
        in_specs.append(pl.BlockSpec((2, tn_out), lambda j, i: (0, j)))
    return pl.pallas_call(
        functools.partial(_proj_kernel, pad_from=pad_from, pad_to=pad_to, group=group,
                          count=count, aug=aug, seq=seq),
        grid=(n_tiles, m // tm),
        in_specs=in_specs,
        out_specs=pl.BlockSpec((tm, tn_out), lambda j, i: (i, j)),
        out_shape=jax.ShapeDtypeStruct((m, n_out), out_dtype),
        scratch_shapes=[pltpu.VMEM((k, tn_out), BF16)],
        compiler_params=_cparams(("arbitrary", "arbitrary")),
        name=name,
    )(*operands)


def _out_norm_kernel(a_ref, w_ref, r_ref, g_ref, h_ref, hn_ref, wb_ref):
    @pl.when(pl.program_id(0) == 0)
    def _():
        _stage_weight(w_ref, wb_ref)

    h = r_ref[...] + jnp.dot(a_ref[...], wb_ref[...], preferred_element_type=F32)
    h_ref[...] = h
    ms = jnp.mean(h * h, axis=-1, keepdims=True)
    hn_ref[...] = (h * lax.rsqrt(ms + RMS_EPS) * g_ref[...]).astype(hn_ref.dtype)


def _out_proj_norm(a, w, res, gain, tm=512):
    m, k = a.shape
    n = w.shape[1]
    full = lambda i: (0, 0)
    row = lambda i: (i, 0)
    return pl.pallas_call(
        _out_norm_kernel,
        grid=(m // tm,),
        in_specs=[pl.BlockSpec((tm, k), row),
                  pl.BlockSpec((k, n), full, pipeline_mode=pl.Buffered(1)),
                  pl.BlockSpec((tm, n), row), pl.BlockSpec((1, n), full)],
        out_specs=[pl.BlockSpec((tm, n), row), pl.BlockSpec((tm, n), row)],
        out_shape=[jax.ShapeDtypeStruct((m, n), F32), jax.ShapeDtypeStruct((m, n), BF16)],
        scratch_shapes=[pltpu.VMEM((k, n), BF16)],
        compiler_params=_cparams(("arbitrary",)),
        name="out_proj_norm",
    )(a, w, res, gain.reshape(1, n))


def _mm_res_kernel(a_ref, w_ref, r_ref, o_ref, wb_ref):
    @pl.when(_first_m_step())
    def _():
        _stage_weight(w_ref, wb_ref)

    o_ref[...] = r_ref[...] + jnp.dot(a_ref[...], wb_ref[...], preferred_element_type=F32)


def _matmul_residual(a, w, res, tm, tn, name):
    m, k = a.shape
    n = w.shape[1]
    return pl.pallas_call(
        _mm_res_kernel,
        grid=(n // tn, m // tm),
        in_specs=[pl.BlockSpec((tm, k), lambda j, i: (i, 0)),
                  pl.BlockSpec((k, tn), lambda j, i: (0, j)),
                  pl.BlockSpec((tm, tn), lambda j, i: (i, j))],
        out_specs=pl.BlockSpec((tm, tn), lambda j, i: (i, j)),
        out_shape=jax.ShapeDtypeStruct((m, n), F32),
        scratch_shapes=[pltpu.VMEM((k, tn), BF16)],
        compiler_params=_cparams(("arbitrary", "arbitrary")),
        name=name,
    )(a, w, res)


def _merge_kernel(xn_ref, oa_ref, ob_ref, wga_ref, wgb_ref, wuf_ref, wun_ref, o_ref,
                  bga_ref, bgb_ref, buf_ref, bun_ref):
    @pl.when(_first_m_step())
    def _():
        _stage_weight(wga_ref, bga_ref, True)
        _stage_weight(wgb_ref, bgb_ref, True)
        _stage_weight(wuf_ref, buf_ref)
        _stage_weight(wun_ref, bun_ref)

    xn = xn_ref[...]
    ga = jax.nn.sigmoid(jnp.dot(xn, bga_ref[...], preferred_element_type=F32))
    ua = jnp.dot(oa_ref[...], buf_ref[...], preferred_element_type=F32)
    acc = ga * ua
    gb = jax.nn.sigmoid(jnp.dot(xn, bgb_ref[...], preferred_element_type=F32))
    ub = jnp.dot(ob_ref[...], bun_ref[...], preferred_element_type=F32)
    o_ref[...] = (acc + gb * ub).astype(o_ref.dtype)


def _merge(xn, oa, ob, wt, row_a, row_b, wuf, wun, tm=512, tn=512):
    m, d = xn.shape
    n = wuf.shape[1]
    ka = oa.shape[1]
    kb = ob.shape[1]
    row = lambda j, i: (i, 0)
    col = lambda j, i: (0, j)
    return pl.pallas_call(
        _merge_kernel,
        grid=(n // tn, m // tm),
        in_specs=[pl.BlockSpec((tm, d), row), pl.BlockSpec((tm, ka), row),
                  pl.BlockSpec((tm, kb), row),
                  _row_window([row_a + t * tn for t in range(n // tn)], tn, d),
                  _row_window([row_b + t * tn for t in range(n // tn)], tn, d),
                  pl.BlockSpec((ka, tn), col), pl.BlockSpec((kb, tn), col)],
        out_specs=pl.BlockSpec((tm, tn), lambda j, i: (i, j)),
        out_shape=jax.ShapeDtypeStruct((m, n), BF16),
        scratch_shapes=[pltpu.VMEM((d, tn), BF16), pltpu.VMEM((d, tn), BF16),
                        pltpu.VMEM((ka, tn), BF16), pltpu.VMEM((kb, tn), BF16)],
        compiler_params=_cparams(("arbitrary", "arbitrary")),
        name="gated_merge",
    )(xn, oa, ob, wt, wt, wuf, wun)


def _swiglu_kernel(a_ref, wg_ref, wu_ref, o_ref, bg_ref, bu_ref):
    @pl.when(_first_m_step())
    def _():
        _stage_weight(wg_ref, bg_ref)
        _stage_weight(wu_ref, bu_ref)

    a = a_ref[...]
    gt = jnp.dot(a, bg_ref[...], preferred_element_type=F32)
    up = jnp.dot(a, bu_ref[...], preferred_element_type=F32)
    o_ref[...] = (gt * jax.nn.sigmoid(gt) * up).astype(o_ref.dtype)


def _swiglu(a, wg, wu, tm=1024, tn=512):
    m, k = a.shape
    n = wg.shape[1]
    return pl.pallas_call(
        _swiglu_kernel,
        grid=(n // tn, m // tm),
        in_specs=[pl.BlockSpec((tm, k), lambda j, i: (i, 0)),
                  pl.BlockSpec((k, tn), lambda j, i: (0, j)),
                  pl.BlockSpec((k, tn), lambda j, i: (0, j))],
        out_specs=pl.BlockSpec((tm, tn), lambda j, i: (i, j)),
        out_shape=jax.ShapeDtypeStruct((m, n), BF16),
        scratch_shapes=[pltpu.VMEM((k, tn), BF16), pltpu.VMEM((k, tn), BF16)],
        compiler_params=_cparams(("arbitrary", "arbitrary")),
        name="swiglu_up",
    )(a, wg, wu)


def _split3(x):
    hi = x.astype(BF16)
    r1 = x - hi.astype(F32)
    mid = r1.astype(BF16)
    lo = (r1 - mid.astype(F32)).astype(BF16)
    return hi, mid, lo


def _decay_kernel(z_ref, b_ref, crep_ref, crow_ref, *, blk):
    t = z_ref.shape[0]
    r = lax.broadcasted_iota(jnp.int32, (blk, blk), 0)
    c = lax.broadcasted_iota(jnp.int32, (blk, blk), 1)
    tri = jnp.where(r >= c, 1.0, 0.0).astype(BF16)
    lane = lax.broadcasted_iota(jnp.int32, (blk, LANES), 1)
    carry = jnp.zeros((1, LANES), F32)
    for s in range(t // blk):
        rows = slice(s * blk, (s + 1) * blk)
        z = z_ref[rows, :] + b_ref[...]
        logf = (jnp.minimum(z, 0.0) - jnp.log1p(jnp.exp(-jnp.abs(z)))) * LOG2E
        hi, mid, lo = _split3(logf)
        cb = (jnp.dot(tri, hi, preferred_element_type=F32)
              + jnp.dot(tri, mid, preferred_element_type=F32)
              + jnp.dot(tri, lo, preferred_element_type=F32)) + carry
        carry = cb[blk - 1:blk, :]
        crow_ref[0, :, rows] = cb.T[:FOX_HEADS, :]
        for h in range(FOX_HEADS):
            col = jnp.sum(jnp.where(lane == h, cb, 0.0), axis=-1, keepdims=True)
            crep_ref[0, h, rows, :] = jnp.broadcast_to(col, (blk, LANES))


def _decay(p3, bias_row, batch, seq, blk=256):
    return pl.pallas_call(
        functools.partial(_decay_kernel, blk=blk),
        grid=(batch,),
        in_specs=[pl.BlockSpec((seq, LANES), lambda b: (b, 6)),
                  pl.BlockSpec((1, LANES), lambda b: (0, 0))],
        out_specs=[pl.BlockSpec((1, FOX_HEADS, seq, LANES), lambda b: (b, 0, 0, 0)),
                   pl.BlockSpec((1, FOX_HEADS, seq), lambda b: (b, 0, 0))],
        out_shape=[jax.ShapeDtypeStruct((batch, FOX_HEADS, seq, LANES), F32),
                   jax.ShapeDtypeStruct((batch, FOX_HEADS, seq), F32)],
        compiler_params=_cparams(("arbitrary",)),
        name="fox_decay_cumsum",
    )(p3, bias_row)


def _transpose_bf16(x):
    return x.astype(F32).T.astype(BF16)


def _normalised(acc, d):
    return acc[:d, :] * (1.0 / jnp.maximum(acc[d:d + 1, :], 1e-30))


def _fox_kernel(q_ref, k_ref, v_ref, crep_ref, crow_ref, o_ref, vt_ref, *, tq, nq, nh):
    i = pl.program_id(2)
    dh = FOX_HEAD_DIM

    @pl.when(i == 0)
    def _():
        for h in range(nh):
            vt_ref[h, 0:dh, :] = _transpose_bf16(v_ref[:, h * dh:(h + 1) * dh])
            vt_ref[h, dh:, :] = jnp.ones((vt_ref.shape[1] - dh, vt_ref.shape[2]), BF16)

    qts = [_transpose_bf16(q_ref[:, h * dh:(h + 1) * dh]) for h in range(nh)]
    cis = [crow_ref[0, h] for h in range(nh)]
    reps = tq // LANES
    rk = lax.broadcasted_iota(jnp.int32, (tq, tq), 0)
    cq = lax.broadcasted_iota(jnp.int32, (tq, tq), 1)

    def scores(h, k0):
        cj = crep_ref[0, h, k0:k0 + tq, :]
        st = jnp.dot(k_ref[k0:k0 + tq, h * dh:(h + 1) * dh], qts[h],
                     preferred_element_type=F32)
        return st + cis[h] - jnp.concatenate([cj] * reps, axis=1)

    def variant(n):
        def run():
            starts = [t * tq for t in range(n, -1, -1)]
            tiles = [[jnp.where(rk <= cq, scores(h, starts[0]), NEG)]
                     + [scores(h, k0) for k0 in starts[1:]] for h in range(nh)]
            ms = [jnp.max(tiles[h][0], axis=0, keepdims=True) for h in range(nh)]
            accs = [jnp.dot(vt_ref[h, :, starts[0]:starts[0] + tq],
                            jnp.exp2(tiles[h][0] - ms[h]).astype(BF16),
                            preferred_element_type=F32) for h in range(nh)]
            for t in range(1, n + 1):
                k0 = starts[t]
                for h in range(nh):
                    st = tiles[h][t]
                    m_new = jnp.maximum(ms[h], jnp.max(st, axis=0, keepdims=True))
                    accs[h] = jnp.exp2(ms[h] - m_new) * accs[h] + jnp.dot(
                        vt_ref[h, :, k0:k0 + tq], jnp.exp2(st - m_new).astype(BF16),
                        preferred_element_type=F32)
                    ms[h] = m_new
            for h in range(nh):
                o_ref[:, h * dh:(h + 1) * dh] = _normalised(accs[h], dh).T.astype(o_ref.dtype)
        return run

    lax.switch(i, [variant(n) for n in range(nq)])


def _fox_attention(p1, crep, crow4, batch, seq, tq=512, nh=4):
    nq = seq // tq
    hg = FOX_HEADS // nh
    w = nh * FOX_HEAD_DIM
    return pl.pallas_call(
        functools.partial(_fox_kernel, tq=tq, nq=nq, nh=nh),
        grid=(batch, hg, nq),
        in_specs=[pl.BlockSpec((tq, w), lambda b, hh, i: (b * nq + i, hh)),
                  pl.BlockSpec((seq, w), lambda b, hh, i: (b, hg + hh)),
                  pl.BlockSpec((seq, w), lambda b, hh, i: (b, 2 * hg + hh)),
                  pl.BlockSpec((1, nh, seq, LANES), lambda b, hh, i: (b, hh, 0, 0)),
                  pl.BlockSpec((1, nh, 1, tq), lambda b, hh, i: (b, hh, 0, i))],
        out_specs=pl.BlockSpec((tq, w), lambda b, hh, i: (b * nq + i, hh)),
        out_shape=jax.ShapeDtypeStruct((batch * seq, FOX_WIDTH), BF16),
        scratch_shapes=[pltpu.VMEM((nh, FOX_HEAD_DIM + BF16_SUBLANES, seq), BF16)],
        compiler_params=_cparams(("arbitrary", "arbitrary", "arbitrary")),
        name="fox_attention",
    )(p1, p1, p1, crep, crow4)


def _compress_one(z_refs, pe_ref, w1_ref, w2_ref, nblk):
    half = CMP_BLOCK // 2
    first = jnp.zeros((nblk, CMP_HIDDEN), F32)
    second = jnp.zeros((nblk, CMP_HIDDEN), F32)
    for p in range(half):
        rows = pl.ds(p, nblk, stride=CMP_STRIDE)
        zp = [z_ref[rows, :] for z_ref in z_refs]
        zp = zp[0] if len(zp) == 1 else jnp.concatenate(zp, axis=1)
        first += jnp.dot((zp + pe_ref[p:p + 1, :]).astype(BF16), w1_ref[p],
                         preferred_element_type=F32)
        second += jnp.dot((zp + pe_ref[half + p:half + p + 1, :]).astype(BF16),
                          w1_ref[half + p], preferred_element_type=F32)
    hid = first + pltpu.roll(second, nblk - 1, 0)
    act = (hid * jax.nn.sigmoid(hid)).astype(BF16)
    return jnp.dot(act, w2_ref[...], preferred_element_type=F32)


def _compress_kernel(zk0_ref, zk1_ref, zv_ref, pek_ref, w1k_ref, w2k_ref, gk_ref,
                     pev_ref, w1v_ref, w2v_ref, kc_ref, vc_ref, *, nblk):
    kc = _compress_one((zk0_ref, zk1_ref), pek_ref, w1k_ref, w2k_ref, nblk)
    ms = jnp.sum(kc * kc, axis=-1, keepdims=True) * (1.0 / NSA_QK_DIM)
    kc = kc * lax.rsqrt(ms + RMS_EPS) * gk_ref[...]
    pos = CMP_STRIDE * lax.broadcasted_iota(jnp.int32, (nblk, LANES), 0) + (CMP_BLOCK - 1)
    up = NSA_QK_PAD - LANES
    kc_ref[:, :up] = kc[:, :up].astype(kc_ref.dtype)
    kc_ref[:, up:] = (kc[:, up:] + _key_aug(pos)).astype(kc_ref.dtype)
    vc = _compress_one((zv_ref,), pev_ref, w1v_ref, w2v_ref, nblk)
    vc_ref[...] = vc.T.astype(vc_ref.dtype)


def _compress(p3, pek, w1k, w2k, gk, pev, w1v, w2v, batch, seq):
    g = NSA_KV_GROUPS
    nblk = seq // CMP_STRIDE
    full2 = lambda b, gg: (0, 0)
    full3 = lambda b, gg: (0, 0, 0)
    return pl.pallas_call(
        functools.partial(_compress_kernel, nblk=nblk),
        grid=(batch, g),
        in_specs=[pl.BlockSpec((seq, LANES), lambda b, gg: (b, 2 * gg)),
                  pl.BlockSpec((seq, LANES), lambda b, gg: (b, 2 * gg + 1)),
                  pl.BlockSpec((seq, NSA_V_DIM), lambda b, gg: (b, 4 + gg)),
                  pl.BlockSpec(pek.shape, full2), pl.BlockSpec(w1k.shape, full3),
                  pl.BlockSpec(w2k.shape, full2), pl.BlockSpec(gk.shape, full2),
                  pl.BlockSpec(pev.shape, full2), pl.BlockSpec(w1v.shape, full3),
                  pl.BlockSpec(w2v.shape, full2)],
        out_specs=[pl.BlockSpec((nblk, NSA_QK_PAD), lambda b, gg: (b * g + gg, 0)),
                   pl.BlockSpec((NSA_V_DIM, nblk), lambda b, gg: (b * g + gg, 0))],
        out_shape=[jax.ShapeDtypeStruct((batch * g * nblk, NSA_QK_PAD), BF16),
                   jax.ShapeDtypeStruct((batch * g * NSA_V_DIM, nblk), BF16)],
        compiler_params=_cparams(("arbitrary", "arbitrary")),
        name="nsa_compress",
    )(p3, p3, p3, pek, w1k, w2k, gk, pev, w1v, w2v)


def _q_heads_t(q_ref):
    qb = q_ref[...]
    return jnp.concatenate(
        [_transpose_bf16(qb[:, hh * NSA_QK_PAD:(hh + 1) * NSA_QK_PAD]) for hh in range(NSA_HPG)],
        axis=1)


def _gate_rows(glt_ref, g, hh):
    base = FOX_HEADS + (g * NSA_HPG + hh) * 3
    return [glt_ref[pl.ds(base + br, 1), :] for br in range(3)]


def _nsa_select_kernel(q_ref, kc_ref, vct_ref, gl_ref, ovt_ref,
                       sel_ref, ocmp_ref, glt_ref, *, tq, n_cmp, n_sel):
    g = pl.program_id(1)
    t0 = pl.program_id(2) * tq
    hpg = NSA_HPG
    dv = NSA_V_DIM
    q4t = _q_heads_t(q_ref)
    rk = lax.broadcasted_iota(jnp.int32, (LANES, tq), 0)
    cq = lax.broadcasted_iota(jnp.int32, (LANES, tq), 1)

    s_c = jnp.dot(kc_ref[...], q4t, preferred_element_type=F32)
    dist_c = (t0 + cq) - (CMP_STRIDE * rk + (CMP_BLOCK - 1))
    mask_c = jnp.where(rk < n_cmp, dist_c, -1) >= 0
    probs = []
    p_sum = jnp.zeros((LANES, tq), F32)
    for hh in range(hpg):
        sm = jnp.where(mask_c, s_c[:, hh * tq:(hh + 1) * tq], NEG)
        m = jnp.max(sm, axis=0, keepdims=True)
        e = jnp.where(mask_c, jnp.exp2(sm - m), 0.0)
        p = e * (1.0 / jnp.maximum(jnp.sum(e, axis=0, keepdims=True), 1e-30))
        probs.append(p)
        p_sum = p_sum + p
    o_cmp = jnp.dot(vct_ref[...], jnp.concatenate(probs, axis=1).astype(BF16),
                    preferred_element_type=F32)

    ph = p_sum.astype(BF16)
    plo = (p_sum - ph.astype(F32)).astype(BF16)
    ovt = ovt_ref[...]
    imp = (jnp.dot(ovt, ph, preferred_element_type=F32)
           + jnp.dot(ovt, plo, preferred_element_type=F32))[:n_sel, :]

    rj = lax.broadcasted_iota(jnp.int32, (n_sel, tq), 0)
    tcol = t0 + lax.broadcasted_iota(jnp.int32, (n_sel, tq), 1)
    back = (tcol >> (SEL_BLOCK.bit_length() - 1)) - rj
    elig = back >= 0
    forced = jnp.where(rj == 0, 0, jnp.where(elig, back, SEL_LOCAL)) < SEL_LOCAL
    score = jnp.where(elig, jnp.where(forced, FORCE_SCORE, imp), -1.0)
    rank = jnp.zeros((n_sel, tq), F32)
    for jp in range(n_sel):
        row = score[jp:jp + 1, :]
        later = jnp.where(rj > jp, 1.0, 0.0)
        rank = rank + jnp.where(row > score, 1.0, jnp.where(row == score, later, 0.0))
    sel_ref[0:n_sel, :] = jnp.where(elig, jnp.where(rank < SEL_TOPK, 0.0, NEG), NEG)
    sel_ref[n_sel:, :] = jnp.full((LANES - n_sel, tq), NEG, F32)

    glt_ref[...] = jax.nn.sigmoid(gl_ref[...]).T
    for hh in range(hpg):
        gate = _gate_rows(glt_ref, g, hh)[0]
        ocmp_ref[:, hh * dv:(hh + 1) * dv] = (gate * o_cmp[:, hh * tq:(hh + 1) * tq]).T


def _nsa_select(pq, p3, kc, vct, ovt, batch, seq, tq=512):
    g = NSA_KV_GROUPS
    nq = seq // tq
    nblk = seq // CMP_STRIDE
    n_cmp = nblk - CMP_BLOCK // CMP_STRIDE + 1
    return pl.pallas_call(
        functools.partial(_nsa_select_kernel, tq=tq, n_cmp=n_cmp, n_sel=seq // SEL_BLOCK),
        grid=(batch, g, nq),
        in_specs=[
            pl.BlockSpec((tq, NSA_HPG * NSA_QK_PAD), lambda b, gg, i: (b * nq + i, gg)),
            pl.BlockSpec((nblk, NSA_QK_PAD), lambda b, gg, i: (b * g + gg, 0)),
            pl.BlockSpec((NSA_V_DIM, nblk), lambda b, gg, i: (b * g + gg, 0)),
            pl.BlockSpec((tq, LANES), lambda b, gg, i: (b * nq + i, 6)),
            pl.BlockSpec(ovt.shape, lambda b, gg, i: (0, 0)),
        ],
        out_specs=[pl.BlockSpec((LANES, tq), lambda b, gg, i: (b * g + gg, i)),
                   pl.BlockSpec((tq, NSA_HPG * NSA_V_DIM), lambda b, gg, i: (b * nq + i, gg))],
        out_shape=[jax.ShapeDtypeStruct((batch * g * LANES, seq), F32),
                   jax.ShapeDtypeStruct((batch * seq, NSA_WIDTH), F32)],
        scratch_shapes=[pltpu.VMEM((LANES, tq), F32)],
        compiler_params=_cparams(("arbitrary", "arbitrary", "arbitrary")),
        name="nsa_select",
    )(pq, kc, vct, p3, ovt)


def _nsa_attend_kernel(q_ref, ks_ref, vs_ref, kw_ref, vw_ref, gl_ref, sel_ref,
                       ocmp_ref, o_ref, vst_ref, vwt_ref, glt_ref, *, tq, n_var):
    i = pl.program_id(1)
    t0 = i * tq
    hpg = NSA_HPG
    ng = NSA_KV_GROUPS
    dv = NSA_V_DIM
    dk = NSA_QK_PAD

    @pl.when(i == 0)
    def _():
        ones = jnp.ones((vst_ref.shape[1] - dv, vst_ref.shape[2]), BF16)
        for gg in range(ng):
            vst_ref[gg, 0:dv, :] = _transpose_bf16(vs_ref[:, gg * dv:(gg + 1) * dv])
            vst_ref[gg, dv:, :] = ones
            vwt_ref[gg, 0:dv, :] = _transpose_bf16(vw_ref[:, gg * dv:(gg + 1) * dv])
            vwt_ref[gg, dv:, :] = ones

    qb = q_ref[...]
    q4t = [jnp.concatenate(
        [_transpose_bf16(qb[:, (gg * hpg + hh) * dk:(gg * hpg + hh + 1) * dk])
         for hh in range(hpg)], axis=1) for gg in range(ng)]
    glt_ref[...] = jax.nn.sigmoid(gl_ref[...]).T

    def distance(k0, rows):
        return (lax.broadcasted_iota(jnp.int32, (rows, tq), 1)
                - lax.broadcasted_iota(jnp.int32, (rows, tq), 0)) + (t0 - k0)

    def scores(gg, k, mask_bias):
        st = jnp.dot(k, q4t[gg], preferred_element_type=F32)
        return st + jnp.concatenate([mask_bias] * hpg, axis=1)

    def selection_bias(gg, k0):
        j0 = gg * LANES + k0 // SEL_BLOCK
        return jnp.concatenate(
            [jnp.broadcast_to(sel_ref[j0 + j:j0 + j + 1, :], (SEL_BLOCK, tq))
             for j in range(SLC_TILE // SEL_BLOCK)], axis=0)

    def variant(n):
        def run():
            kw0 = pl.multiple_of(jnp.maximum(t0 - WINDOW, 0), LANES)
            wrows = WINDOW + tq
            dist_w = distance(kw0, wrows)
            bias_w = jnp.where(jnp.where(dist_w >= 0, dist_w, WINDOW) < WINDOW, 0.0, NEG)
            o_win = []
            for gg in range(ng):
                sw = scores(gg, kw_ref[pl.ds(kw0, wrows), gg * dk:(gg + 1) * dk], bias_w)
                pw = jnp.exp2(sw - jnp.max(sw, axis=0, keepdims=True))
                o_win.append(_normalised(
                    jnp.dot(vwt_ref[gg, :, pl.ds(kw0, wrows)], pw.astype(BF16),
                            preferred_element_type=F32), dv))

            starts = [t * SLC_TILE for t in range(n, -1, -1)]
            causal = distance(starts[0], SLC_TILE) >= 0
            tiles = [[] for _ in range(ng)]
            for t, k0 in enumerate(starts):
                for gg in range(ng):
                    bias = selection_bias(gg, k0)
                    if t == 0:
                        bias = jnp.where(causal, bias, NEG)
                    tiles[gg].append(scores(
                        gg, ks_ref[k0:k0 + SLC_TILE, gg * dk:(gg + 1) * dk], bias))
            ms = [jnp.max(tiles[gg][0], axis=0, keepdims=True) for gg in range(ng)]
            accs = [jnp.dot(vst_ref[gg, :, starts[0]:starts[0] + SLC_TILE],
                            jnp.exp2(tiles[gg][0] - ms[gg]).astype(BF16),
                            preferred_element_type=F32) for gg in range(ng)]
            for t in range(1, n + 1):
                k0 = starts[t]
                for gg in range(ng):
                    st = tiles[gg][t]
                    m_new = jnp.maximum(ms[gg], jnp.max(st, axis=0, keepdims=True))
                    accs[gg] = jnp.exp2(ms[gg] - m_new) * accs[gg] + jnp.dot(
                        vst_ref[gg, :, k0:k0 + SLC_TILE], jnp.exp2(st - m_new).astype(BF16),
                        preferred_element_type=F32)
                    ms[gg] = m_new

            for gg in range(ng):
                o_slc = _normalised(accs[gg], dv)
                for hh in range(hpg):
                    _, g_slc, g_win = _gate_rows(glt_ref, gg, hh)
                    lanes = slice(hh * tq, (hh + 1) * tq)
                    out = g_slc * o_slc[:, lanes] + g_win * o_win[gg][:, lanes]
                    cols = slice((gg * hpg + hh) * dv, (gg * hpg + hh + 1) * dv)
                    o_ref[:, cols] = (ocmp_ref[:, cols] + out.T).astype(o_ref.dtype)
        return run

    lax.switch(t0 // SLC_TILE, [variant(n) for n in range(n_var)])


def _nsa_attend(pq, pk, pv, p3, sel, ocmp, batch, seq):
    tq = LANES
    g = NSA_KV_GROUPS
    nq = seq // tq
    vrows = NSA_V_DIM + BF16_SUBLANES
    return pl.pallas_call(
        functools.partial(_nsa_attend_kernel, tq=tq, n_var=seq // SLC_TILE),
        grid=(batch, nq),
        in_specs=[
            pl.BlockSpec((tq, NSA_HEADS * NSA_QK_PAD), lambda b, i: (b * nq + i, 0)),
            pl.BlockSpec((seq, g * NSA_QK_PAD), lambda b, i: (b, 0)),
            pl.BlockSpec((seq, g * NSA_V_DIM), lambda b, i: (b, 0)),
            pl.BlockSpec((seq, g * NSA_QK_PAD), lambda b, i: (b, 1)),
            pl.BlockSpec((seq, g * NSA_V_DIM), lambda b, i: (b, 1)),
            pl.BlockSpec((tq, LANES), lambda b, i: (b * nq + i, 6)),
            pl.BlockSpec((g * LANES, tq), lambda b, i: (b, i)),
            pl.BlockSpec((tq, NSA_WIDTH), lambda b, i: (b * nq + i, 0)),
        ],
        out_specs=pl.BlockSpec((tq, NSA_WIDTH), lambda b, i: (b * nq + i, 0)),
        out_shape=jax.ShapeDtypeStruct((batch * seq, NSA_WIDTH), BF16),
        scratch_shapes=[pltpu.VMEM((g, vrows, seq), BF16), pltpu.VMEM((g, vrows, seq), BF16),
                        pltpu.VMEM((LANES, tq), F32)],
        compiler_params=_cparams(("arbitrary", "arbitrary")),
        name="nsa_attend",
    )(pq, pk, pv, pk, pv, p3, sel, ocmp)


def _pad_head_rows(wt, heads):
    k = wt.shape[1]
    wt = wt.reshape(heads, NSA_QK_DIM, k)
    wt = jnp.pad(wt, ((0, 0), (0, NSA_QK_PAD - NSA_QK_DIM), (0, 0)))
    return wt.reshape(heads * NSA_QK_PAD, k)


def _pad_gain(gain, scale=1.0):
    return jnp.pad(gain * scale, (0, NSA_QK_PAD - NSA_QK_DIM))


def _overlap_matrix(nc, ns):
    i = np.arange(nc)[:, None]
    j = np.arange(ns)[None, :]
    lo = np.maximum(i * CMP_STRIDE, j * SEL_BLOCK)
    hi = np.minimum(i * CMP_STRIDE + CMP_BLOCK, (j + 1) * SEL_BLOCK)
    return (np.maximum(hi - lo, 0) / CMP_STRIDE).astype(np.float32)


def kernel(x, norm_attn, w_in, fox_f_bias, fox_q_gain, fox_k_gain,
           nsa_q_gain, nsa_kc_gain, nsa_ks_gain, nsa_kw_gain,
           cmp_pe_k, cmp_w1_k, cmp_w2_k, cmp_pe_v, cmp_w1_v, cmp_w2_v,
           w_up_fox, w_up_nsa, w_out, norm_ffn, w_ffn_gate, w_ffn_up, w_ffn_down):
    batch, seq, d = x.shape
    m = batch * seq
    depth = w_in.shape[0]
    pts = [0] + [int(p) for p in np.cumsum(IN_SPLITS)]
    nblk = seq // CMP_STRIDE
    n_cmp = nblk - CMP_BLOCK // CMP_STRIDE + 1
    ns = seq // SEL_BLOCK

    slope = jnp.exp2(-8.0 * jnp.arange(1, NSA_HEADS + 1, dtype=F32) / NSA_HEADS) * LOG2E
    s1, s2, s3 = [p.astype(F32) for p in _split3(slope)]
    q_const = jnp.stack([256.0 * s1, 256.0 * s2, 256.0 * s3, s1, s2, s3], axis=1)
    lane0 = NSA_QK_DIM
    aug_q = jnp.zeros((2, NSA_HEADS, NSA_QK_PAD), F32)
    aug_q = aug_q.at[0, :, lane0:lane0 + 6].set(q_const)
    aug_q = aug_q.at[1, :, lane0 + 6:lane0 + 9].set(-slope[:, None])
    aug_q = aug_q.reshape(2, NSA_HEADS * NSA_QK_PAD)
    ovt_np = np.zeros((LANES, nblk), np.float32)
    ovt_np[:ns, :n_cmp] = _overlap_matrix(n_cmp, ns).T
    ovt = jnp.asarray(ovt_np, BF16)

    w_in_t = jnp.swapaxes(w_in, 1, 2)

    xf = x.reshape(m, d)
    for l in range(depth):
        wt = w_in_t[l]
        row = dict(zip(("fq", "fk", "fv", "fl", "nq", "kc", "vc", "ks", "vs", "kw", "vw", "ng",
                        "ga", "gb"), pts))

        gain1 = jnp.concatenate([jnp.tile(fox_q_gain[l] * (FOX_HEAD_DIM ** -0.5 * LOG2E), FOX_HEADS),
                                 jnp.tile(fox_k_gain[l], FOX_HEADS),
                                 jnp.ones((FOX_WIDTH,), F32)])
        flag1 = jnp.concatenate([jnp.ones((2 * FOX_WIDTH,), F32), jnp.zeros((FOX_WIDTH,), F32)])
        gain_q = jnp.tile(_pad_gain(nsa_q_gain[l], NSA_QK_DIM ** -0.5 * LOG2E), NSA_HEADS)
        gain_k = jnp.concatenate([jnp.tile(_pad_gain(nsa_ks_gain[l]), NSA_KV_GROUPS),
                                  jnp.tile(_pad_gain(nsa_kw_gain[l]), NSA_KV_GROUPS)])
        n_small = FOX_HEADS + 3 * NSA_HEADS
        w3 = jnp.concatenate([_pad_head_rows(wt[row["kc"]:row["vc"]], NSA_KV_GROUPS),
                              wt[row["vc"]:row["ks"]], wt[row["fl"]:row["nq"]],
                              wt[row["ng"]:row["ga"]],
                              jnp.zeros((LANES - n_small, d), F32)], axis=0)

        xn = _rmsnorm(xf, norm_attn[l])
        q_tile = NSA_HPG * NSA_QK_DIM
        p1 = _project(xn, wt, [t * 1024 for t in range(3 * FOX_WIDTH // 1024)], 1024, gain1, flag1,
                      BF16, 1024, "proj_fox", group=128, count=128)
        pq = _project(xn, wt, [row["nq"], row["nq"] + q_tile], q_tile, gain_q,
                      jnp.ones_like(gain_q), BF16, 1024, "proj_nsa_q", pad_from=NSA_QK_DIM,
                      pad_to=NSA_QK_PAD, group=NSA_QK_PAD, count=NSA_QK_DIM,
                      aug="query", seq=seq, aug_table=aug_q)
        w_k = jnp.concatenate([wt[row["ks"]:row["vs"]], wt[row["kw"]:row["vw"]]], axis=0)
        w_v = jnp.concatenate([wt[row["vs"]:row["kw"]], wt[row["vw"]:row["ng"]]], axis=0)
        pk = _project(xn, w_k, [0], 2 * KV_K, gain_k, jnp.ones_like(gain_k), BF16,
                      1024, "proj_nsa_k", pad_from=NSA_QK_DIM, pad_to=NSA_QK_PAD,
                      group=NSA_QK_PAD, count=NSA_QK_DIM, aug="key", seq=seq)
        pv = _project(xn, w_v, [0], 2 * KV_V, None, None, BF16, 1024, "proj_nsa_v")
        p3 = _project(xn, w3, [0], w3.shape[0], None, None, F32, 1024, "proj_f32")

        bias_row = jnp.pad(fox_f_bias[l], (0, LANES - FOX_HEADS)).reshape(1, LANES)
        crep, crow = _decay(p3, bias_row, batch, seq)
        o_a = _fox_attention(p1, crep, crow.reshape(batch, FOX_HEADS, 1, seq), batch, seq)

        pad_d = NSA_QK_PAD - NSA_QK_DIM
        pek = jnp.pad(cmp_pe_k[l], ((0, 0), (0, pad_d)))
        w1k = jnp.pad(cmp_w1_k[l].reshape(CMP_BLOCK, NSA_QK_DIM, CMP_HIDDEN),
                      ((0, 0), (0, pad_d), (0, 0))).astype(BF16)
        w2k = jnp.pad(cmp_w2_k[l], ((0, 0), (0, pad_d))).astype(BF16)
        gk = _pad_gain(nsa_kc_gain[l]).reshape(1, NSA_QK_PAD)
        w1v = cmp_w1_v[l].reshape(CMP_BLOCK, NSA_V_DIM, CMP_HIDDEN).astype(BF16)
        w2v = cmp_w2_v[l].astype(BF16)
        kc, vct = _compress(p3, pek, w1k, w2k, gk, cmp_pe_v[l], w1v, w2v, batch, seq)
        sel, ocmp = _nsa_select(pq, p3, kc, vct, ovt, batch, seq)
        o_b = _nsa_attend(pq, pk, pv, p3, sel, ocmp, batch, seq)

        merged = _merge(xn, o_a, o_b, wt, row["ga"], row["gb"], w_up_fox[l], w_up_nsa[l])
        hres, hn = _out_proj_norm(merged, w_out[l], xf, norm_ffn[l])

        act = _swiglu(hn, w_ffn_gate[l], w_ffn_up[l])
        xf = _matmul_residual(act, w_ffn_down[l], hres, 512, 512, "ffn_down")
    return xf.reshape(batch, seq, d)
```

```python
import functools

import numpy as np
import jax
import jax.numpy as jnp
from jax import lax
from jax.experimental import pallas as pl
from jax.experimental.pallas import tpu as pltpu

F32 = jnp.float32
BF16 = jnp.bfloat16

D_MODEL = 2048
FOX_HEADS = 8
FOX_HEAD_DIM = 128
FOX_WIDTH = FOX_HEADS * FOX_HEAD_DIM
NSA_HEADS = 8
NSA_KV_GROUPS = 2
NSA_HPG = NSA_HEADS // NSA_KV_GROUPS
NSA_QK_DIM = 192
NSA_QK_PAD = 256
NSA_V_DIM = 128
NSA_WIDTH = NSA_HEADS * NSA_V_DIM
CMP_BLOCK = 32
CMP_STRIDE = 16
CMP_HIDDEN = 256
SEL_BLOCK = 64
SEL_TOPK = 16
SEL_LOCAL = 2
FORCE_SCORE = 1.0e4
WINDOW = 512
KV_K = NSA_KV_GROUPS * NSA_QK_DIM
KV_V = NSA_KV_GROUPS * NSA_V_DIM
D_FF = -(-(8 * D_MODEL) // (3 * 256)) * 256
RMS_EPS = 1e-6
IN_SPLITS = (FOX_WIDTH, FOX_WIDTH, FOX_WIDTH, FOX_HEADS,
             NSA_HEADS * NSA_QK_DIM, KV_K, KV_V, KV_K, KV_V, KV_K, KV_V,
             3 * NSA_HEADS, D_MODEL, D_MODEL)

LANES = 128
SUBLANES = 8
BF16_SUBLANES = 16
NEG = -1.0e30
LOG2E = 1.4426950408889634
SLC_TILE = 512
VMEM_LIMIT = 56 * 1024 * 1024


def _cparams(sem):
    return pltpu.CompilerParams(dimension_semantics=sem, vmem_limit_bytes=VMEM_LIMIT)


def _rms_kernel(x_ref, g_ref, o_ref):
    x = x_ref[...]
    ms = jnp.mean(x * x, axis=-1, keepdims=True)
    o_ref[...] = (x * lax.rsqrt(ms + RMS_EPS) * g_ref[...]).astype(o_ref.dtype)


def _rmsnorm(x, gain, tm=512):
    m, d = x.shape
    return pl.pallas_call(
        _rms_kernel,
        grid=(m // tm,),
        in_specs=[pl.BlockSpec((tm, d), lambda i: (i, 0)),
                  pl.BlockSpec((1, d), lambda i: (0, 0))],
        out_specs=pl.BlockSpec((tm, d), lambda i: (i, 0)),
        out_shape=jax.ShapeDtypeStruct((m, d), BF16),
        compiler_params=_cparams(("arbitrary",)),
        name="rmsnorm",
    )(x, gain.reshape(1, d))


STAGE_CHUNK = 512
EPILOGUE_SPLIT = 4


def _stage_weight(w_ref, wb_ref, transposed=False, pad_from=0, pad_to=0):
    if not transposed:
        wb_ref[...] = w_ref[...].astype(BF16)
        return
    n_in, k = w_ref.shape
    for c in range(k // STAGE_CHUNK):
        cols = slice(c * STAGE_CHUNK, (c + 1) * STAGE_CHUNK)
        w = w_ref[:, cols]
        if pad_from != pad_to:
            zero = jnp.zeros((pad_to - pad_from, STAGE_CHUNK), F32)
            w = jnp.concatenate(
                [piece for h in range(n_in // pad_from)
                 for piece in (w[h * pad_from:(h + 1) * pad_from, :], zero)], axis=0)
        wb_ref[cols, :] = w.T.astype(BF16)


def _first_m_step():
    return pl.program_id(1) == 0


AUG_LANE = NSA_QK_DIM - LANES


def _key_aug(pos):
    lane = lax.broadcasted_iota(jnp.int32, pos.shape, 1)
    hi = (pos >> 8).astype(F32)
    lo = (pos & 255).astype(F32)
    return jnp.where(lane < AUG_LANE, 0.0,
                     jnp.where(lane < AUG_LANE + 3, hi,
                               jnp.where(lane < AUG_LANE + 6, lo,
                                         jnp.where(lane < AUG_LANE + 9, 1.0, 0.0))))


def _query_aug(pos, const_row, slope_row):
    u = pos.astype(F32) * slope_row
    hi = u.astype(BF16).astype(F32)
    r1 = u - hi
    mid = r1.astype(BF16).astype(F32)
    lane = lax.broadcasted_iota(jnp.int32, pos.shape, 1)
    return const_row + jnp.where(lane == AUG_LANE + 6, hi,
                                 jnp.where(lane == AUG_LANE + 7, mid, r1 - mid))


def _proj_kernel(*refs, pad_from, pad_to, group, count, aug, seq):
    if aug == "query":
        a_ref, w_ref, gain_ref, flag_ref, aug_ref, o_ref, wb_ref = refs
    else:
        a_ref, w_ref, gain_ref, flag_ref, o_ref, wb_ref = refs

    @pl.when(_first_m_step())
    def _():
        _stage_weight(w_ref, wb_ref, True, pad_from, pad_to)

    if group == 0:
        o_ref[...] = jnp.dot(a_ref[...], wb_ref[...],
                             preferred_element_type=F32).astype(o_ref.dtype)
        return
    tm = a_ref.shape[0]
    rows = tm // EPILOGUE_SPLIT
    ys = [jnp.dot(a_ref[r * rows:(r + 1) * rows, :], wb_ref[...], preferred_element_type=F32)
          for r in range(EPILOGUE_SPLIT)]
    for r, y in enumerate(ys):
        if aug:
            pos = ((pl.program_id(1) * tm) % seq + r * rows
                   + lax.broadcasted_iota(jnp.int32, (rows, LANES), 0))
        for c in range(y.shape[1] // group):
            sl = slice(c * group, (c + 1) * group)
            yc = y[:, sl]
            ss = jnp.sum(yc * yc, axis=-1, keepdims=True)
            rs = lax.rsqrt(ss * (1.0 / count) + RMS_EPS)
            scale = jnp.where(flag_ref[:, sl] > 0.0, rs, 1.0)
            out = yc * scale * gain_ref[:, sl]
            rsl = slice(r * rows, (r + 1) * rows)
            if not aug:
                o_ref[rsl, sl] = out.astype(o_ref.dtype)
                continue
            up = slice((c + 1) * group - LANES, (c + 1) * group)
            extra = (_key_aug(pos) if aug == "key"
                     else _query_aug(pos, aug_ref[0:1, up], aug_ref[1:2, up]))
            o_ref[rsl, c * group:(c + 1) * group - LANES] = out[:, :group - LANES].astype(o_ref.dtype)
            o_ref[rsl, up] = (out[:, group - LANES:] + extra).astype(o_ref.dtype)


def _row_window(starts, rows, k):
    def index(j, i):
        start = starts[-1]
        for t in range(len(starts) - 2, -1, -1):
            start = jnp.where(j == t, starts[t], start)
        return pl.multiple_of(start, SUBLANES), 0
    assert all(s % SUBLANES == 0 for s in starts)
    return pl.BlockSpec((pl.Element(rows), pl.Element(k)), index)


def _project(a, wt, starts, tn_in, gain, flag, out_dtype, tm, name,
             pad_from=0, pad_to=0, group=0, count=1, aug="", seq=0, aug_table=None):
    m, k = a.shape
    n_tiles = len(starts)
    tn_out = tn_in if pad_from == pad_to else tn_in // pad_from * pad_to
    n_out = n_tiles * tn_out
    if gain is None:
        gain = jnp.ones((n_out,), F32)
        flag = jnp.zeros((n_out,), F32)
    assert not aug or seq % tm == 0
    operands = [a, wt, gain.reshape(1, n_out), flag.reshape(1, n_out)]
    in_specs = [pl.BlockSpec((tm, k), lambda j, i: (i, 0)),
                _row_window(starts, tn_in, k),
                pl.BlockSpec((1, tn_out), lambda j, i: (0, j)),
                pl.BlockSpec((1, tn_out), lambda j, i: (0, j))]
    if aug == "query":
        operands.append(aug_table)
        in_specs.append(pl.BlockSpec((2, tn_out), lambda j, i: (0, j)))
    return pl.pallas_call(
        functools.partial(_proj_kernel, pad_from=pad_from, pad_to=pad_to, group=group,
                          count=count, aug=aug, seq=seq),
        grid=(n_tiles, m // tm),
        in_specs=in_specs,
        out_specs=pl.BlockSpec((tm, tn_out), lambda j, i: (i, j)),
        out_shape=jax.ShapeDtypeStruct((m, n_out), out_dtype),
        scratch_shapes=[pltpu.VMEM((k, tn_out), BF16)],
        compiler_params=_cparams(("arbitrary", "arbitrary")),
        name=name,
    )(*operands)


def _out_norm_kernel(a_ref, w_ref, r_ref, g_ref, h_ref, hn_ref, wb_ref):
    @pl.when(pl.program_id(0) == 0)
    def _():
        _stage_weight(w_ref, wb_ref)

    h = r_ref[...] + jnp.dot(a_ref[...], wb_ref[...], preferred_element_type=F32)
    h_ref[...] = h
    ms = jnp.mean(h * h, axis=-1, keepdims=True)
    hn_ref[...] = (h * lax.rsqrt(ms + RMS_EPS) * g_ref[...]).astype(hn_ref.dtype)


def _out_proj_norm(a, w, res, gain, tm=512):
    m, k = a.shape
    n = w.shape[1]
    full = lambda i: (0, 0)
    row = lambda i: (i, 0)
    return pl.pallas_call(
        _out_norm_kernel,
        grid=(m // tm,),
        in_specs=[pl.BlockSpec((tm, k), row),
                  pl.BlockSpec((k, n), full, pipeline_mode=pl.Buffered(1)),
                  pl.BlockSpec((tm, n), row), pl.BlockSpec((1, n), full)],
        out_specs=[pl.BlockSpec((tm, n), row), pl.BlockSpec((tm, n), row)],
        out_shape=[jax.ShapeDtypeStruct((m, n), F32), jax.ShapeDtypeStruct((m, n), BF16)],
        scratch_shapes=[pltpu.VMEM((k, n), BF16)],
        compiler_params=_cparams(("arbitrary",)),
        name="out_proj_norm",
    )(a, w, res, gain.reshape(1, n))


def _mm_res_kernel(a_ref, w_ref, r_ref, o_ref, wb_ref):
    @pl.when(_first_m_step())
    def _():
        _stage_weight(w_ref, wb_ref)

    o_ref[...] = r_ref[...] + jnp.dot(a_ref[...], wb_ref[...], preferred_element_type=F32)


def _matmul_residual(a, w, res, tm, tn, name):
    m, k = a.shape
    n = w.shape[1]
    return pl.pallas_call(
        _mm_res_kernel,
        grid=(n // tn, m // tm),
        in_specs=[pl.BlockSpec((tm, k), lambda j, i: (i, 0)),
                  pl.BlockSpec((k, tn), lambda j, i: (0, j)),
                  pl.BlockSpec((tm, tn), lambda j, i: (i, j))],
        out_specs=pl.BlockSpec((tm, tn), lambda j, i: (i, j)),
        out_shape=jax.ShapeDtypeStruct((m, n), F32),
        scratch_shapes=[pltpu.VMEM((k, tn), BF16)],
        compiler_params=_cparams(("arbitrary", "arbitrary")),
        name=name,
    )(a, w, res)


def _merge_kernel(xn_ref, oa_ref, ob_ref, wga_ref, wgb_ref, wuf_ref, wun_ref, o_ref,
                  bga_ref, bgb_ref, buf_ref, bun_ref):
    @pl.when(_first_m_step())
    def _():
        _stage_weight(wga_ref, bga_ref, True)
        _stage_weight(wgb_ref, bgb_ref, True)
        _stage_weight(wuf_ref, buf_ref)
        _stage_weight(wun_ref, bun_ref)

    xn = xn_ref[...]
    ga = jax.nn.sigmoid(jnp.dot(xn, bga_ref[...], preferred_element_type=F32))
    ua = jnp.dot(oa_ref[...], buf_ref[...], preferred_element_type=F32)
    acc = ga * ua
    gb = jax.nn.sigmoid(jnp.dot(xn, bgb_ref[...], preferred_element_type=F32))
    ub = jnp.dot(ob_ref[...], bun_ref[...], preferred_element_type=F32)
    o_ref[...] = (acc + gb * ub).astype(o_ref.dtype)


def _merge(xn, oa, ob, wt, row_a, row_b, wuf, wun, tm=512, tn=512):
    m, d = xn.shape
    n = wuf.shape[1]
    ka = oa.shape[1]
    kb = ob.shape[1]
    row = lambda j, i: (i, 0)
    col = lambda j, i: (0, j)
    return pl.pallas_call(
        _merge_kernel,
        grid=(n // tn, m // tm),
        in_specs=[pl.BlockSpec((tm, d), row), pl.BlockSpec((tm, ka), row),
                  pl.BlockSpec((tm, kb), row),
                  _row_window([row_a + t * tn for t in range(n // tn)], tn, d),
                  _row_window([row_b + t * tn for t in range(n // tn)], tn, d),
                  pl.BlockSpec((ka, tn), col), pl.BlockSpec((kb, tn), col)],
        out_specs=pl.BlockSpec((tm, tn), lambda j, i: (i, j)),
        out_shape=jax.ShapeDtypeStruct((m, n), BF16),
        scratch_shapes=[pltpu.VMEM((d, tn), BF16), pltpu.VMEM((d, tn), BF16),
                        pltpu.VMEM((ka, tn), BF16), pltpu.VMEM((kb, tn), BF16)],
        compiler_params=_cparams(("arbitrary", "arbitrary")),
        name="gated_merge",
    )(xn, oa, ob, wt, wt, wuf, wun)


def _swiglu_kernel(a_ref, wg_ref, wu_ref, o_ref, bg_ref, bu_ref):
    @pl.when(_first_m_step())
    def _():
        _stage_weight(wg_ref, bg_ref)
        _stage_weight(wu_ref, bu_ref)

    a = a_ref[...]
    gt = jnp.dot(a, bg_ref[...], preferred_element_type=F32)
    up = jnp.dot(a, bu_ref[...], preferred_element_type=F32)
    o_ref[...] = (gt * jax.nn.sigmoid(gt) * up).astype(o_ref.dtype)


def _swiglu(a, wg, wu, tm=1024, tn=512):
    m, k = a.shape
    n = wg.shape[1]
    return pl.pallas_call(
        _swiglu_kernel,
        grid=(n // tn, m // tm),
        in_specs=[pl.BlockSpec((tm, k), lambda j, i: (i, 0)),
                  pl.BlockSpec((k, tn), lambda j, i: (0, j)),
                  pl.BlockSpec((k, tn), lambda j, i: (0, j))],
        out_specs=pl.BlockSpec((tm, tn), lambda j, i: (i, j)),
        out_shape=jax.ShapeDtypeStruct((m, n), BF16),
        scratch_shapes=[pltpu.VMEM((k, tn), BF16), pltpu.VMEM((k, tn), BF16)],
        compiler_params=_cparams(("arbitrary", "arbitrary")),
        name="swiglu_up",
    )(a, wg, wu)


def _split3(x):
    hi = x.astype(BF16)
    r1 = x - hi.astype(F32)
    mid = r1.astype(BF16)
    lo = (r1 - mid.astype(F32)).astype(BF16)
    return hi, mid, lo


def _decay_kernel(z_ref, b_ref, ccol_ref, crow_ref, *, blk):
    t = z_ref.shape[0]
    r = lax.broadcasted_iota(jnp.int32, (blk, blk), 0)
    c = lax.broadcasted_iota(jnp.int32, (blk, blk), 1)
    tri = jnp.where(r >= c, 1.0, 0.0).astype(BF16)
    carry = jnp.zeros((1, LANES), F32)
    for s in range(t // blk):
        rows = slice(s * blk, (s + 1) * blk)
        z = z_ref[rows, :] + b_ref[...]
        logf = (jnp.minimum(z, 0.0) - jnp.log1p(jnp.exp(-jnp.abs(z)))) * LOG2E
        hi, mid, lo = _split3(logf)
        cb = (jnp.dot(tri, hi, preferred_element_type=F32)
              + jnp.dot(tri, mid, preferred_element_type=F32)
              + jnp.dot(tri, lo, preferred_element_type=F32)) + carry
        carry = cb[blk - 1:blk, :]
        ccol_ref[rows, :] = cb
        crow_ref[0, :, rows] = cb.T[:FOX_HEADS, :]


def _decay(p3, bias_row, batch, seq, blk=256):
    return pl.pallas_call(
        functools.partial(_decay_kernel, blk=blk),
        grid=(batch,),
        in_specs=[pl.BlockSpec((seq, LANES), lambda b: (b, 6)),
                  pl.BlockSpec((1, LANES), lambda b: (0, 0))],
        out_specs=[pl.BlockSpec((seq, LANES), lambda b: (b, 0)),
                   pl.BlockSpec((1, FOX_HEADS, seq), lambda b: (b, 0, 0))],
        out_shape=[jax.ShapeDtypeStruct((batch * seq, LANES), F32),
                   jax.ShapeDtypeStruct((batch, FOX_HEADS, seq), F32)],
        compiler_params=_cparams(("arbitrary",)),
        name="fox_decay_cumsum",
    )(p3, bias_row)


def _transpose_bf16(x):
    return x.astype(F32).T.astype(BF16)


def _normalised(acc, d):
    return acc[:d, :] * (1.0 / jnp.maximum(acc[d:d + 1, :], 1e-30))


def _fox_kernel(q_ref, k_ref, v_ref, ccol_ref, crow_ref, o_ref, vt_ref, ka_ref, *, tq, nq, nh):
    hh = pl.program_id(1)
    i = pl.program_id(2)
    dh = FOX_HEAD_DIM

    def pieces(x, lane, first):
        hi = x.astype(BF16).astype(F32)
        r1 = x - hi
        mid = r1.astype(BF16).astype(F32)
        return jnp.where(lane == first, hi,
                         jnp.where(lane == first + 1, mid,
                                   jnp.where(lane == first + 2, r1 - mid, 0.0)))

    @pl.when(i == 0)
    def _():
        cc = ccol_ref[...]
        lane = lax.broadcasted_iota(jnp.int32, cc.shape, 1)
        for h in range(nh):
            vt_ref[h, 0:dh, :] = _transpose_bf16(v_ref[:, h * dh:(h + 1) * dh])
            vt_ref[h, dh:, :] = jnp.ones((vt_ref.shape[1] - dh, vt_ref.shape[2]), BF16)
            cj = jnp.sum(jnp.where(lane == hh * nh + h, cc, 0.0), axis=-1, keepdims=True)
            extra = pieces(-cj, lane, 0) + jnp.where(lane < 3, 0.0, jnp.where(lane < 6, 1.0, 0.0))
            ka_ref[h, :, 0:dh] = k_ref[:, h * dh:(h + 1) * dh]
            ka_ref[h, :, dh:] = extra.astype(BF16)

    row = lax.broadcasted_iota(jnp.int32, (dh, tq), 0)
    qts = []
    for h in range(nh):
        ci = crow_ref[0, h]
        extra = pieces(ci, row, 3) + jnp.where(row < 3, 1.0, 0.0)
        qts.append(jnp.concatenate(
            [_transpose_bf16(q_ref[:, h * dh:(h + 1) * dh]), extra.astype(BF16)], axis=0))
    rk = lax.broadcasted_iota(jnp.int32, (tq, tq), 0)
    cq = lax.broadcasted_iota(jnp.int32, (tq, tq), 1)

    def scores(h, k0):
        return jnp.dot(ka_ref[h, k0:k0 + tq, :], qts[h], preferred_element_type=F32)

    def variant(n):
        def run():
            starts = [t * tq for t in range(n, -1, -1)]
            tiles = [[jnp.where(rk <= cq, scores(h, starts[0]), NEG)]
                     + [scores(h, k0) for k0 in starts[1:]] for h in range(nh)]
            ms = [jnp.max(tiles[h][0], axis=0, keepdims=True) for h in range(nh)]
            accs = [jnp.dot(vt_ref[h, :, starts[0]:starts[0] + tq],
                            jnp.exp2(tiles[h][0] - ms[h]).astype(BF16),
                            preferred_element_type=F32) for h in range(nh)]
            for t in range(1, n + 1):
                k0 = starts[t]
                for h in range(nh):
                    st = tiles[h][t]
                    m_new = jnp.maximum(ms[h], jnp.max(st, axis=0, keepdims=True))
                    accs[h] = jnp.exp2(ms[h] - m_new) * accs[h] + jnp.dot(
                        vt_ref[h, :, k0:k0 + tq], jnp.exp2(st - m_new).astype(BF16),
                        preferred_element_type=F32)
                    ms[h] = m_new
            for h in range(nh):
                o_ref[:, h * dh:(h + 1) * dh] = _normalised(accs[h], dh).T.astype(o_ref.dtype)
        return run

    lax.switch(i, [variant(n) for n in range(nq)])


def _fox_attention(p1, ccol, crow4, batch, seq, tq=512, nh=4):
    nq = seq // tq
    hg = FOX_HEADS // nh
    w = nh * FOX_HEAD_DIM
    return pl.pallas_call(
        functools.partial(_fox_kernel, tq=tq, nq=nq, nh=nh),
        grid=(batch, hg, nq),
        in_specs=[pl.BlockSpec((tq, w), lambda b, hh, i: (b * nq + i, hh)),
                  pl.BlockSpec((seq, w), lambda b, hh, i: (b, hg + hh)),
                  pl.BlockSpec((seq, w), lambda b, hh, i: (b, 2 * hg + hh)),
                  pl.BlockSpec((seq, LANES), lambda b, hh, i: (b, 0)),
                  pl.BlockSpec((1, nh, 1, tq), lambda b, hh, i: (b, hh, 0, i))],
        out_specs=pl.BlockSpec((tq, w), lambda b, hh, i: (b * nq + i, hh)),
        out_shape=jax.ShapeDtypeStruct((batch * seq, FOX_WIDTH), BF16),
        scratch_shapes=[pltpu.VMEM((nh, FOX_HEAD_DIM + BF16_SUBLANES, seq), BF16),
                        pltpu.VMEM((nh, seq, 2 * FOX_HEAD_DIM), BF16)],
        compiler_params=_cparams(("arbitrary", "arbitrary", "arbitrary")),
        name="fox_attention",
    )(p1, p1, p1, ccol, crow4)


def _compress_one(z_refs, pe_ref, w1_ref, w2_ref, nblk):
    half = CMP_BLOCK // 2
    first = jnp.zeros((nblk, CMP_HIDDEN), F32)
    second = jnp.zeros((nblk, CMP_HIDDEN), F32)
    for p in range(half):
        rows = pl.ds(p, nblk, stride=CMP_STRIDE)
        zp = [z_ref[rows, :] for z_ref in z_refs]
        zp = zp[0] if len(zp) == 1 else jnp.concatenate(zp, axis=1)
        first += jnp.dot((zp + pe_ref[p:p + 1, :]).astype(BF16), w1_ref[p],
                         preferred_element_type=F32)
        second += jnp.dot((zp + pe_ref[half + p:half + p + 1, :]).astype(BF16),
                          w1_ref[half + p], preferred_element_type=F32)
    hid = first + pltpu.roll(second, nblk - 1, 0)
    act = (hid * jax.nn.sigmoid(hid)).astype(BF16)
    return jnp.dot(act, w2_ref[...], preferred_element_type=F32)


def _compress_kernel(zk0_ref, zk1_ref, zv_ref, pek_ref, w1k_ref, w2k_ref, gk_ref,
                     pev_ref, w1v_ref, w2v_ref, kc_ref, vc_ref, *, nblk):
    kc = _compress_one((zk0_ref, zk1_ref), pek_ref, w1k_ref, w2k_ref, nblk)
    ms = jnp.sum(kc * kc, axis=-1, keepdims=True) * (1.0 / NSA_QK_DIM)
    kc = kc * lax.rsqrt(ms + RMS_EPS) * gk_ref[...]
    pos = CMP_STRIDE * lax.broadcasted_iota(jnp.int32, (nblk, LANES), 0) + (CMP_BLOCK - 1)
    up = NSA_QK_PAD - LANES
    kc_ref[:, :up] = kc[:, :up].astype(kc_ref.dtype)
    kc_ref[:, up:] = (kc[:, up:] + _key_aug(pos)).astype(kc_ref.dtype)
    vc = _compress_one((zv_ref,), pev_ref, w1v_ref, w2v_ref, nblk)
    vc_ref[...] = vc.T.astype(vc_ref.dtype)


def _compress(p3, pek, w1k, w2k, gk, pev, w1v, w2v, batch, seq):
    g = NSA_KV_GROUPS
    nblk = seq // CMP_STRIDE
    full2 = lambda b, gg: (0, 0)
    full3 = lambda b, gg: (0, 0, 0)
    return pl.pallas_call(
        functools.partial(_compress_kernel, nblk=nblk),
        grid=(batch, g),
        in_specs=[pl.BlockSpec((seq, LANES), lambda b, gg: (b, 2 * gg)),
                  pl.BlockSpec((seq, LANES), lambda b, gg: (b, 2 * gg + 1)),
                  pl.BlockSpec((seq, NSA_V_DIM), lambda b, gg: (b, 4 + gg)),
                  pl.BlockSpec(pek.shape, full2), pl.BlockSpec(w1k.shape, full3),
                  pl.BlockSpec(w2k.shape, full2), pl.BlockSpec(gk.shape, full2),
                  pl.BlockSpec(pev.shape, full2), pl.BlockSpec(w1v.shape, full3),
                  pl.BlockSpec(w2v.shape, full2)],
        out_specs=[pl.BlockSpec((nblk, NSA_QK_PAD), lambda b, gg: (b * g + gg, 0)),
                   pl.BlockSpec((NSA_V_DIM, nblk), lambda b, gg: (b * g + gg, 0))],
        out_shape=[jax.ShapeDtypeStruct((batch * g * nblk, NSA_QK_PAD), BF16),
                   jax.ShapeDtypeStruct((batch * g * NSA_V_DIM, nblk), BF16)],
        compiler_params=_cparams(("arbitrary", "arbitrary")),
        name="nsa_compress",
    )(p3, p3, p3, pek, w1k, w2k, gk, pev, w1v, w2v)


def _q_heads_t(q_ref):
    qb = q_ref[...]
    return jnp.concatenate(
        [_transpose_bf16(qb[:, hh * NSA_QK_PAD:(hh + 1) * NSA_QK_PAD]) for hh in range(NSA_HPG)],
        axis=1)


def _gate_rows(glt_ref, g, hh):
    base = FOX_HEADS + (g * NSA_HPG + hh) * 3
    return [glt_ref[pl.ds(base + br, 1), :] for br in range(3)]


def _nsa_select_kernel(q_ref, kc_ref, vct_ref, gl_ref, ovt_ref,
                       sel_ref, ocmp_ref, glt_ref, *, tq, n_cmp, n_sel):
    g = pl.program_id(1)
    t0 = pl.program_id(2) * tq
    hpg = NSA_HPG
    dv = NSA_V_DIM
    q4t = _q_heads_t(q_ref)
    rk = lax.broadcasted_iota(jnp.int32, (LANES, tq), 0)
    cq = lax.broadcasted_iota(jnp.int32, (LANES, tq), 1)

    s_c = jnp.dot(kc_ref[...], q4t, preferred_element_type=F32)
    dist_c = (t0 + cq) - (CMP_STRIDE * rk + (CMP_BLOCK - 1))
    mask_c = jnp.where(rk < n_cmp, dist_c, -1) >= 0
    probs = []
    p_sum = jnp.zeros((LANES, tq), F32)
    for hh in range(hpg):
        sm = jnp.where(mask_c, s_c[:, hh * tq:(hh + 1) * tq], NEG)
        m = jnp.max(sm, axis=0, keepdims=True)
        e = jnp.where(mask_c, jnp.exp2(sm - m), 0.0)
        p = e * (1.0 / jnp.maximum(jnp.sum(e, axis=0, keepdims=True), 1e-30))
        probs.append(p)
        p_sum = p_sum + p
    o_cmp = jnp.dot(vct_ref[...], jnp.concatenate(probs, axis=1).astype(BF16),
                    preferred_element_type=F32)

    ph = p_sum.astype(BF16)
    plo = (p_sum - ph.astype(F32)).astype(BF16)
    ovt = ovt_ref[...]
    imp = (jnp.dot(ovt, ph, preferred_element_type=F32)
           + jnp.dot(ovt, plo, preferred_element_type=F32))[:n_sel, :]

    rj = lax.broadcasted_iota(jnp.int32, (n_sel, tq), 0)
    tcol = t0 + lax.broadcasted_iota(jnp.int32, (n_sel, tq), 1)
    back = (tcol >> (SEL_BLOCK.bit_length() - 1)) - rj
    elig = back >= 0
    forced = jnp.where(rj == 0, 0, jnp.where(elig, back, SEL_LOCAL)) < SEL_LOCAL
    score = jnp.where(elig, jnp.where(forced, FORCE_SCORE, imp), -1.0)
    rank = jnp.zeros((n_sel, tq), F32)
    for jp in range(n_sel):
        row = score[jp:jp + 1, :]
        later = jnp.where(rj > jp, 1.0, 0.0)
        rank = rank + jnp.where(row > score, 1.0, jnp.where(row == score, later, 0.0))
    sel_ref[0:n_sel, :] = jnp.where(elig, jnp.where(rank < SEL_TOPK, 0.0, NEG), NEG)
    sel_ref[n_sel:, :] = jnp.full((LANES - n_sel, tq), NEG, F32)

    glt_ref[...] = jax.nn.sigmoid(gl_ref[...]).T
    for hh in range(hpg):
        gate = _gate_rows(glt_ref, g, hh)[0]
        ocmp_ref[:, hh * dv:(hh + 1) * dv] = (gate * o_cmp[:, hh * tq:(hh + 1) * tq]).T


def _nsa_select(pq, p3, kc, vct, ovt, batch, seq, tq=512):
    g = NSA_KV_GROUPS
    nq = seq // tq
    nblk = seq // CMP_STRIDE
    n_cmp = nblk - CMP_BLOCK // CMP_STRIDE + 1
    return pl.pallas_call(
        functools.partial(_nsa_select_kernel, tq=tq, n_cmp=n_cmp, n_sel=seq // SEL_BLOCK),
        grid=(batch, g, nq),
        in_specs=[
            pl.BlockSpec((tq, NSA_HPG * NSA_QK_PAD), lambda b, gg, i: (b * nq + i, gg)),
            pl.BlockSpec((nblk, NSA_QK_PAD), lambda b, gg, i: (b * g + gg, 0)),
            pl.BlockSpec((NSA_V_DIM, nblk), lambda b, gg, i: (b * g + gg, 0)),
            pl.BlockSpec((tq, LANES), lambda b, gg, i: (b * nq + i, 6)),
            pl.BlockSpec(ovt.shape, lambda b, gg, i: (0, 0)),
        ],
        out_specs=[pl.BlockSpec((LANES, tq), lambda b, gg, i: (b * g + gg, i)),
                   pl.BlockSpec((tq, NSA_HPG * NSA_V_DIM), lambda b, gg, i: (b * nq + i, gg))],
        out_shape=[jax.ShapeDtypeStruct((batch * g * LANES, seq), F32),
                   jax.ShapeDtypeStruct((batch * seq, NSA_WIDTH), F32)],
        scratch_shapes=[pltpu.VMEM((LANES, tq), F32)],
        compiler_params=_cparams(("arbitrary", "arbitrary", "arbitrary")),
        name="nsa_select",
    )(pq, kc, vct, p3, ovt)


def _nsa_attend_kernel(q_ref, ks_ref, vs_ref, kw_ref, vw_ref, gl_ref, sel_ref,
                       ocmp_ref, o_ref, vst_ref, vwt_ref, glt_ref, *, tq, n_var):
    i = pl.program_id(1)
    t0 = i * tq
    hpg = NSA_HPG
    ng = NSA_KV_GROUPS
    dv = NSA_V_DIM
    dk = NSA_QK_PAD

    @pl.when(i == 0)
    def _():
        ones = jnp.ones((vst_ref.shape[1] - dv, vst_ref.shape[2]), BF16)
        for gg in range(ng):
            vst_ref[gg, 0:dv, :] = _transpose_bf16(vs_ref[:, gg * dv:(gg + 1) * dv])
            vst_ref[gg, dv:, :] = ones
            vwt_ref[gg, 0:dv, :] = _transpose_bf16(vw_ref[:, gg * dv:(gg + 1) * dv])
            vwt_ref[gg, dv:, :] = ones

    qb = q_ref[...]
    q4t = [jnp.concatenate(
        [_transpose_bf16(qb[:, (gg * hpg + hh) * dk:(gg * hpg + hh + 1) * dk])
         for hh in range(hpg)], axis=1) for gg in range(ng)]
    glt_ref[...] = jax.nn.sigmoid(gl_ref[...]).T

    def distance(k0, rows):
        return (lax.broadcasted_iota(jnp.int32, (rows, tq), 1)
                - lax.broadcasted_iota(jnp.int32, (rows, tq), 0)) + (t0 - k0)

    def scores(gg, k, mask_bias):
        st = jnp.dot(k, q4t[gg], preferred_element_type=F32)
        return st + jnp.concatenate([mask_bias] * hpg, axis=1)

    def selection_bias(gg, k0):
        j0 = gg * LANES + k0 // SEL_BLOCK
        return jnp.concatenate(
            [jnp.broadcast_to(sel_ref[j0 + j:j0 + j + 1, :], (SEL_BLOCK, tq))
             for j in range(SLC_TILE // SEL_BLOCK)], axis=0)

    def variant(n):
        def run():
            kw0 = pl.multiple_of(jnp.maximum(t0 - WINDOW, 0), LANES)
            wrows = WINDOW + tq
            dist_w = distance(kw0, wrows)
            bias_w = jnp.where(jnp.where(dist_w >= 0, dist_w, WINDOW) < WINDOW, 0.0, NEG)
            o_win = []
            for gg in range(ng):
                sw = scores(gg, kw_ref[pl.ds(kw0, wrows), gg * dk:(gg + 1) * dk], bias_w)
                pw = jnp.exp2(sw - jnp.max(sw, axis=0, keepdims=True))
                o_win.append(_normalised(
                    jnp.dot(vwt_ref[gg, :, pl.ds(kw0, wrows)], pw.astype(BF16),
                            preferred_element_type=F32), dv))

            starts = [t * SLC_TILE for t in range(n, -1, -1)]
            causal = distance(starts[0], SLC_TILE) >= 0
            tiles = [[] for _ in range(ng)]
            for t, k0 in enumerate(starts):
                for gg in range(ng):
                    bias = selection_bias(gg, k0)
                    if t == 0:
                        bias = jnp.where(causal, bias, NEG)
                    tiles[gg].append(scores(
                        gg, ks_ref[k0:k0 + SLC_TILE, gg * dk:(gg + 1) * dk], bias))
            ms = [jnp.max(tiles[gg][0], axis=0, keepdims=True) for gg in range(ng)]
            accs = [jnp.dot(vst_ref[gg, :, starts[0]:starts[0] + SLC_TILE],
                            jnp.exp2(tiles[gg][0] - ms[gg]).astype(BF16),
                            preferred_element_type=F32) for gg in range(ng)]
            for t in range(1, n + 1):
                k0 = starts[t]
                for gg in range(ng):
                    st = tiles[gg][t]
                    m_new = jnp.maximum(ms[gg], jnp.max(st, axis=0, keepdims=True))
                    accs[gg] = jnp.exp2(ms[gg] - m_new) * accs[gg] + jnp.dot(
                        vst_ref[gg, :, k0:k0 + SLC_TILE], jnp.exp2(st - m_new).astype(BF16),
                        preferred_element_type=F32)
                    ms[gg] = m_new

            for gg in range(ng):
                o_slc = _normalised(accs[gg], dv)
                for hh in range(hpg):
                    _, g_slc, g_win = _gate_rows(glt_ref, gg, hh)
                    lanes = slice(hh * tq, (hh + 1) * tq)
                    out = g_slc * o_slc[:, lanes] + g_win * o_win[gg][:, lanes]
                    cols = slice((gg * hpg + hh) * dv, (gg * hpg + hh + 1) * dv)
                    o_ref[:, cols] = (ocmp_ref[:, cols] + out.T).astype(o_ref.dtype)
        return run

    lax.switch(t0 // SLC_TILE, [variant(n) for n in range(n_var)])


def _nsa_attend(pq, pk, pv, p3, sel, ocmp, batch, seq):
    tq = LANES
    g = NSA_KV_GROUPS
    nq = seq // tq
    vrows = NSA_V_DIM + BF16_SUBLANES
    return pl.pallas_call(
        functools.partial(_nsa_attend_kernel, tq=tq, n_var=seq // SLC_TILE),
        grid=(batch, nq),
        in_specs=[
            pl.BlockSpec((tq, NSA_HEADS * NSA_QK_PAD), lambda b, i: (b * nq + i, 0)),
            pl.BlockSpec((seq, g * NSA_QK_PAD), lambda b, i: (b, 0)),
            pl.BlockSpec((seq, g * NSA_V_DIM), lambda b, i: (b, 0)),
            pl.BlockSpec((seq, g * NSA_QK_PAD), lambda b, i: (b, 1)),
            pl.BlockSpec((seq, g * NSA_V_DIM), lambda b, i: (b, 1)),
            pl.BlockSpec((tq, LANES), lambda b, i: (b * nq + i, 6)),
            pl.BlockSpec((g * LANES, tq), lambda b, i: (b, i)),
            pl.BlockSpec((tq, NSA_WIDTH), lambda b, i: (b * nq + i, 0)),
        ],
        out_specs=pl.BlockSpec((tq, NSA_WIDTH), lambda b, i: (b * nq + i, 0)),
        out_shape=jax.ShapeDtypeStruct((batch * seq, NSA_WIDTH), BF16),
        scratch_shapes=[pltpu.VMEM((g, vrows, seq), BF16), pltpu.VMEM((g, vrows, seq), BF16),
                        pltpu.VMEM((LANES, tq), F32)],
        compiler_params=_cparams(("arbitrary", "arbitrary")),
        name="nsa_attend",
    )(pq, pk, pv, pk, pv, p3, sel, ocmp)


def _pad_head_rows(wt, heads):
    k = wt.shape[1]
    wt = wt.reshape(heads, NSA_QK_DIM, k)
    wt = jnp.pad(wt, ((0, 0), (0, NSA_QK_PAD - NSA_QK_DIM), (0, 0)))
    return wt.reshape(heads * NSA_QK_PAD, k)


def _pad_gain(gain, scale=1.0):
    return jnp.pad(gain * scale, (0, NSA_QK_PAD - NSA_QK_DIM))


def _overlap_matrix(nc, ns):
    i = np.arange(nc)[:, None]
    j = np.arange(ns)[None, :]
    lo = np.maximum(i * CMP_STRIDE, j * SEL_BLOCK)
    hi = np.minimum(i * CMP_STRIDE + CMP_BLOCK, (j + 1) * SEL_BLOCK)
    return (np.maximum(hi - lo, 0) / CMP_STRIDE).astype(np.float32)


def kernel(x, norm_attn, w_in, fox_f_bias, fox_q_gain, fox_k_gain,
           nsa_q_gain, nsa_kc_gain, nsa_ks_gain, nsa_kw_gain,
           cmp_pe_k, cmp_w1_k, cmp_w2_k, cmp_pe_v, cmp_w1_v, cmp_w2_v,
           w_up_fox, w_up_nsa, w_out, norm_ffn, w_ffn_gate, w_ffn_up, w_ffn_down):
    batch, seq, d = x.shape
    m = batch * seq
    depth = w_in.shape[0]
    pts = [0] + [int(p) for p in np.cumsum(IN_SPLITS)]
    nblk = seq // CMP_STRIDE
    n_cmp = nblk - CMP_BLOCK // CMP_STRIDE + 1
    ns = seq // SEL_BLOCK

    slope = jnp.exp2(-8.0 * jnp.arange(1, NSA_HEADS + 1, dtype=F32) / NSA_HEADS) * LOG2E
    s1, s2, s3 = [p.astype(F32) for p in _split3(slope)]
    q_const = jnp.stack([256.0 * s1, 256.0 * s2, 256.0 * s3, s1, s2, s3], axis=1)
    lane0 = NSA_QK_DIM
    aug_q = jnp.zeros((2, NSA_HEADS, NSA_QK_PAD), F32)
    aug_q = aug_q.at[0, :, lane0:lane0 + 6].set(q_const)
    aug_q = aug_q.at[1, :, lane0 + 6:lane0 + 9].set(-slope[:, None])
    aug_q = aug_q.reshape(2, NSA_HEADS * NSA_QK_PAD)
    ovt_np = np.zeros((LANES, nblk), np.float32)
    ovt_np[:ns, :n_cmp] = _overlap_matrix(n_cmp, ns).T
    ovt = jnp.asarray(ovt_np, BF16)

    w_in_t = jnp.swapaxes(w_in, 1, 2)

    xf = x.reshape(m, d)
    for l in range(depth):
        wt = w_in_t[l]
        row = dict(zip(("fq", "fk", "fv", "fl", "nq", "kc", "vc", "ks", "vs", "kw", "vw", "ng",
                        "ga", "gb"), pts))

        gain1 = jnp.concatenate([jnp.tile(fox_q_gain[l] * (FOX_HEAD_DIM ** -0.5 * LOG2E), FOX_HEADS),
                                 jnp.tile(fox_k_gain[l], FOX_HEADS),
                                 jnp.ones((FOX_WIDTH,), F32)])
        flag1 = jnp.concatenate([jnp.ones((2 * FOX_WIDTH,), F32), jnp.zeros((FOX_WIDTH,), F32)])
        gain_q = jnp.tile(_pad_gain(nsa_q_gain[l], NSA_QK_DIM ** -0.5 * LOG2E), NSA_HEADS)
        gain_k = jnp.concatenate([jnp.tile(_pad_gain(nsa_ks_gain[l]), NSA_KV_GROUPS),
                                  jnp.tile(_pad_gain(nsa_kw_gain[l]), NSA_KV_GROUPS)])
        n_small = FOX_HEADS + 3 * NSA_HEADS
        w3 = jnp.concatenate([_pad_head_rows(wt[row["kc"]:row["vc"]], NSA_KV_GROUPS),
                              wt[row["vc"]:row["ks"]], wt[row["fl"]:row["nq"]],
                              wt[row["ng"]:row["ga"]],
                              jnp.zeros((LANES - n_small, d), F32)], axis=0)

        xn = _rmsnorm(xf, norm_attn[l])
        q_tile = NSA_HPG * NSA_QK_DIM
        p1 = _project(xn, wt, [t * 1024 for t in range(3 * FOX_WIDTH // 1024)], 1024, gain1, flag1,
                      BF16, 1024, "proj_fox", group=128, count=128)
        pq = _project(xn, wt, [row["nq"], row["nq"] + q_tile], q_tile, gain_q,
                      jnp.ones_like(gain_q), BF16, 1024, "proj_nsa_q", pad_from=NSA_QK_DIM,
                      pad_to=NSA_QK_PAD, group=NSA_QK_PAD, count=NSA_QK_DIM,
                      aug="query", seq=seq, aug_table=aug_q)
        w_k = jnp.concatenate([wt[row["ks"]:row["vs"]], wt[row["kw"]:row["vw"]]], axis=0)
        w_v = jnp.concatenate([wt[row["vs"]:row["kw"]], wt[row["vw"]:row["ng"]]], axis=0)
        pk = _project(xn, w_k, [0], 2 * KV_K, gain_k, jnp.ones_like(gain_k), BF16,
                      1024, "proj_nsa_k", pad_from=NSA_QK_DIM, pad_to=NSA_QK_PAD,
                      group=NSA_QK_PAD, count=NSA_QK_DIM, aug="key", seq=seq)
        pv = _project(xn, w_v, [0], 2 * KV_V, None, None, BF16, 1024, "proj_nsa_v")
        p3 = _project(xn, w3, [0], w3.shape[0], None, None, F32, 1024, "proj_f32")

        bias_row = jnp.pad(fox_f_bias[l], (0, LANES - FOX_HEADS)).reshape(1, LANES)
        ccol, crow = _decay(p3, bias_row, batch, seq)
        o_a = _fox_attention(p1, ccol, crow.reshape(batch, FOX_HEADS, 1, seq), batch, seq)

        pad_d = NSA_QK_PAD - NSA_QK_DIM
        pek = jnp.pad(cmp_pe_k[l], ((0, 0), (0, pad_d)))
        w1k = jnp.pad(cmp_w1_k[l].reshape(CMP_BLOCK, NSA_QK_DIM, CMP_HIDDEN),
                      ((0, 0), (0, pad_d), (0, 0))).astype(BF16)
        w2k = jnp.pad(cmp_w2_k[l], ((0, 0), (0, pad_d))).astype(BF16)
        gk = _pad_gain(nsa_kc_gain[l]).reshape(1, NSA_QK_PAD)
        w1v = cmp_w1_v[l].reshape(CMP_BLOCK, NSA_V_DIM, CMP_HIDDEN).astype(BF16)
        w2v = cmp_w2_v[l].astype(BF16)
        kc, vct = _compress(p3, pek, w1k, w2k, gk, cmp_pe_v[l], w1v, w2v, batch, seq)
        sel, ocmp = _nsa_select(pq, p3, kc, vct, ovt, batch, seq)
        o_b = _nsa_attend(pq, pk, pv, p3, sel, ocmp, batch, seq)

        merged = _merge(xn, o_a, o_b, wt, row["ga"], row["gb"], w_up_fox[l], w_up_nsa[l])
        hres, hn = _out_proj_norm(merged, w_out[l], xf, norm_ffn[l])

        act = _swiglu(hn, w_ffn_gate[l], w_ffn_up[l])
        xf = _matmul_residual(act, w_ffn_down[l], hres, 512, 512, "ffn_down")
    return xf.reshape(batch, seq, d)
```

```python
import functools

import numpy as np
import jax
import jax.numpy as jnp
from jax import lax
from jax.experimental import pallas as pl
from jax.experimental.pallas import tpu as pltpu

F32 = jnp.float32
BF16 = jnp.bfloat16

D_MODEL = 2048
FOX_HEADS = 8
FOX_HEAD_DIM = 128
FOX_WIDTH = FOX_HEADS * FOX_HEAD_DIM
NSA_HEADS = 8
NSA_KV_GROUPS = 2
NSA_HPG = NSA_HEADS // NSA_KV_GROUPS
NSA_QK_DIM = 192
NSA_QK_PAD = 256
NSA_V_DIM = 128
NSA_WIDTH = NSA_HEADS * NSA_V_DIM
CMP_BLOCK = 32
CMP_STRIDE = 16
CMP_HIDDEN = 256
SEL_BLOCK = 64
SEL_TOPK = 16
SEL_LOCAL = 2
FORCE_SCORE = 1.0e4
WINDOW = 512
KV_K = NSA_KV_GROUPS * NSA_QK_DIM
KV_V = NSA_KV_GROUPS * NSA_V_DIM
D_FF = -(-(8 * D_MODEL) // (3 * 256)) * 256
RMS_EPS = 1e-6
IN_SPLITS = (FOX_WIDTH, FOX_WIDTH, FOX_WIDTH, FOX_HEADS,
             NSA_HEADS * NSA_QK_DIM, KV_K, KV_V, KV_K, KV_V, KV_K, KV_V,
             3 * NSA_HEADS, D_MODEL, D_MODEL)

LANES = 128
SUBLANES = 8
BF16_SUBLANES = 16
NEG = -1.0e30
LOG2E = 1.4426950408889634
SLC_TILE = 512
VMEM_LIMIT = 56 * 1024 * 1024


def _cparams(sem):
    return pltpu.CompilerParams(dimension_semantics=sem, vmem_limit_bytes=VMEM_LIMIT)


def _rms_kernel(x_ref, g_ref, o_ref):
    x = x_ref[...]
    ms = jnp.mean(x * x, axis=-1, keepdims=True)
    o_ref[...] = (x * lax.rsqrt(ms + RMS_EPS) * g_ref[...]).astype(o_ref.dtype)


def _rmsnorm(x, gain, tm=512):
    m, d = x.shape
    return pl.pallas_call(
        _rms_kernel,
        grid=(m // tm,),
        in_specs=[pl.BlockSpec((tm, d), lambda i: (i, 0)),
                  pl.BlockSpec((1, d), lambda i: (0, 0))],
        out_specs=pl.BlockSpec((tm, d), lambda i: (i, 0)),
        out_shape=jax.ShapeDtypeStruct((m, d), BF16),
        compiler_params=_cparams(("arbitrary",)),
        name="rmsnorm",
    )(x, gain.reshape(1, d))


STAGE_CHUNK = 512
EPILOGUE_SPLIT = 4


def _stage_weight(w_ref, wb_ref, transposed=False, pad_from=0, pad_to=0):
    if not transposed:
        wb_ref[...] = w_ref[...].astype(BF16)
        return
    n_in, k = w_ref.shape
    for c in range(k // STAGE_CHUNK):
        cols = slice(c * STAGE_CHUNK, (c + 1) * STAGE_CHUNK)
        w = w_ref[:, cols]
        if pad_from != pad_to:
            zero = jnp.zeros((pad_to - pad_from, STAGE_CHUNK), F32)
            w = jnp.concatenate(
                [piece for h in range(n_in // pad_from)
                 for piece in (w[h * pad_from:(h + 1) * pad_from, :], zero)], axis=0)
        wb_ref[cols, :] = w.T.astype(BF16)


def _first_m_step():
    return pl.program_id(1) == 0


AUG_LANE = NSA_QK_DIM - LANES


def _key_aug(pos):
    lane = lax.broadcasted_iota(jnp.int32, pos.shape, 1)
    hi = (pos >> 8).astype(F32)
    lo = (pos & 255).astype(F32)
    return jnp.where(lane < AUG_LANE, 0.0,
                     jnp.where(lane < AUG_LANE + 3, hi,
                               jnp.where(lane < AUG_LANE + 6, lo,
                                         jnp.where(lane < AUG_LANE + 9, 1.0, 0.0))))


def _query_aug(pos, const_row, slope_row):
    u = pos.astype(F32) * slope_row
    hi = u.astype(BF16).astype(F32)
    r1 = u - hi
    mid = r1.astype(BF16).astype(F32)
    lane = lax.broadcasted_iota(jnp.int32, pos.shape, 1)
    return const_row + jnp.where(lane == AUG_LANE + 6, hi,
                                 jnp.where(lane == AUG_LANE + 7, mid, r1 - mid))


def _proj_kernel(*refs, pad_from, pad_to, group, count, aug, seq):
    if aug == "query":
        a_ref, w_ref, gain_ref, flag_ref, aug_ref, o_ref, wb_ref = refs
    else:
        a_ref, w_ref, gain_ref, flag_ref, o_ref, wb_ref = refs

    @pl.when(_first_m_step())
    def _():
        _stage_weight(w_ref, wb_ref, True, pad_from, pad_to)

    if group == 0:
        o_ref[...] = jnp.dot(a_ref[...], wb_ref[...],
                             preferred_element_type=F32).astype(o_ref.dtype)
        return
    tm = a_ref.shape[0]
    rows = tm // EPILOGUE_SPLIT
    ys = [jnp.dot(a_ref[r * rows:(r + 1) * rows, :], wb_ref[...], preferred_element_type=F32)
          for r in range(EPILOGUE_SPLIT)]
    for r, y in enumerate(ys):
        if aug:
            pos = ((pl.program_id(1) * tm) % seq + r * rows
                   + lax.broadcasted_iota(jnp.int32, (rows, LANES), 0))
        for c in range(y.shape[1] // group):
            sl = slice(c * group, (c + 1) * group)
            yc = y[:, sl]
            ss = jnp.sum(yc * yc, axis=-1, keepdims=True)
            rs = lax.rsqrt(ss * (1.0 / count) + RMS_EPS)
            scale = jnp.where(flag_ref[:, sl] > 0.0, rs, 1.0)
            out = yc * scale * gain_ref[:, sl]
            rsl = slice(r * rows, (r + 1) * rows)
            if not aug:
                o_ref[rsl, sl] = out.astype(o_ref.dtype)
                continue
            up = slice((c + 1) * group - LANES, (c + 1) * group)
            extra = (_key_aug(pos) if aug == "key"
                     else _query_aug(pos, aug_ref[0:1, up], aug_ref[1:2, up]))
            o_ref[rsl, c * group:(c + 1) * group - LANES] = out[:, :group - LANES].astype(o_ref.dtype)
            o_ref[rsl, up] = (out[:, group - LANES:] + extra).astype(o_ref.dtype)


def _row_window(starts, rows, k):
    def index(j, i):
        start = starts[-1]
        for t in range(len(starts) - 2, -1, -1):
            start = jnp.where(j == t, starts[t], start)
        return pl.multiple_of(start, SUBLANES), 0
    assert all(s % SUBLANES == 0 for s in starts)
    return pl.BlockSpec((pl.Element(rows), pl.Element(k)), index)


def _project(a, wt, starts, tn_in, gain, flag, out_dtype, tm, name,
             pad_from=0, pad_to=0, group=0, count=1, aug="", seq=0, aug_table=None):
    m, k = a.shape
    n_tiles = len(starts)
    tn_out = tn_in if pad_from == pad_to else tn_in // pad_from * pad_to
    n_out = n_tiles * tn_out
    if gain is None:
        gain = jnp.ones((n_out,), F32)
        flag = jnp.zeros((n_out,), F32)
    assert not aug or seq % tm == 0
    operands = [a, wt, gain.reshape(1, n_out), flag.reshape(1, n_out)]
    in_specs = [pl.BlockSpec((tm, k), lambda j, i: (i, 0)),
                _row_window(starts, tn_in, k),
                pl.BlockSpec((1, tn_out), lambda j, i: (0, j)),
                pl.BlockSpec((1, tn_out), lambda j, i: (0, j))]
    if aug == "query":
        operands.append(aug_table)
        in_specs.append(pl.BlockSpec((2, tn_out), lambda j, i: (0, j)))
    return pl.pallas_call(
        functools.partial(_proj_kernel, pad_from=pad_from, pad_to=pad_to, group=group,
                          count=count, aug=aug, seq=seq),
        grid=(n_tiles, m // tm),
        in_specs=in_specs,
        out_specs=pl.BlockSpec((tm, tn_out), lambda j, i: (i, j)),
        out_shape=jax.ShapeDtypeStruct((m, n_out), out_dtype),
        scratch_shapes=[pltpu.VMEM((k, tn_out), BF16)],
        compiler_params=_cparams(("arbitrary", "arbitrary")),
        name=name,
    )(*operands)


def _out_norm_kernel(a_ref, w_ref, r_ref, g_ref, h_ref, hn_ref, wb_ref):
    @pl.when(pl.program_id(0) == 0)
    def _():
        _stage_weight(w_ref, wb_ref)

    h = r_ref[...] + jnp.dot(a_ref[...], wb_ref[...], preferred_element_type=F32)
    h_ref[...] = h
    ms = jnp.mean(h * h, axis=-1, keepdims=True)
    hn_ref[...] = (h * lax.rsqrt(ms + RMS_EPS) * g_ref[...]).astype(hn_ref.dtype)


def _out_proj_norm(a, w, res, gain, tm=512):
    m, k = a.shape
    n = w.shape[1]
    full = lambda i: (0, 0)
    row = lambda i: (i, 0)
    return pl.pallas_call(
        _out_norm_kernel,
        grid=(m // tm,),
        in_specs=[pl.BlockSpec((tm, k), row),
                  pl.BlockSpec((k, n), full, pipeline_mode=pl.Buffered(1)),
                  pl.BlockSpec((tm, n), row), pl.BlockSpec((1, n), full)],
        out_specs=[pl.BlockSpec((tm, n), row), pl.BlockSpec((tm, n), row)],
        out_shape=[jax.ShapeDtypeStruct((m, n), F32), jax.ShapeDtypeStruct((m, n), BF16)],
        scratch_shapes=[pltpu.VMEM((k, n), BF16)],
        compiler_params=_cparams(("arbitrary",)),
        name="out_proj_norm",
    )(a, w, res, gain.reshape(1, n))


def _mm_res_kernel(a_ref, w_ref, r_ref, o_ref, wb_ref):
    @pl.when(_first_m_step())
    def _():
        _stage_weight(w_ref, wb_ref)

    o_ref[...] = r_ref[...] + jnp.dot(a_ref[...], wb_ref[...], preferred_element_type=F32)


def _matmul_residual(a, w, res, tm, tn, name):
    m, k = a.shape
    n = w.shape[1]
    return pl.pallas_call(
        _mm_res_kernel,
        grid=(n // tn, m // tm),
        in_specs=[pl.BlockSpec((tm, k), lambda j, i: (i, 0)),
                  pl.BlockSpec((k, tn), lambda j, i: (0, j), pipeline_mode=pl.Buffered(1)),
                  pl.BlockSpec((tm, tn), lambda j, i: (i, j))],
        out_specs=pl.BlockSpec((tm, tn), lambda j, i: (i, j)),
        out_shape=jax.ShapeDtypeStruct((m, n), F32),
        scratch_shapes=[pltpu.VMEM((k, tn), BF16)],
        compiler_params=_cparams(("arbitrary", "arbitrary")),
        name=name,
    )(a, w, res)


def _merge_kernel(xn_ref, oa_ref, ob_ref, wga_ref, wgb_ref, wuf_ref, wun_ref, o_ref,
                  bga_ref, bgb_ref, buf_ref, bun_ref):
    @pl.when(_first_m_step())
    def _():
        _stage_weight(wga_ref, bga_ref, True)
        _stage_weight(wgb_ref, bgb_ref, True)
        _stage_weight(wuf_ref, buf_ref)
        _stage_weight(wun_ref, bun_ref)

    xn = xn_ref[...]
    ga = jax.nn.sigmoid(jnp.dot(xn, bga_ref[...], preferred_element_type=F32))
    ua = jnp.dot(oa_ref[...], buf_ref[...], preferred_element_type=F32)
    acc = ga * ua
    gb = jax.nn.sigmoid(jnp.dot(xn, bgb_ref[...], preferred_element_type=F32))
    ub = jnp.dot(ob_ref[...], bun_ref[...], preferred_element_type=F32)
    o_ref[...] = (acc + gb * ub).astype(o_ref.dtype)


def _merge(xn, oa, ob, wt, row_a, row_b, wuf, wun, tm=512, tn=512):
    m, d = xn.shape
    n = wuf.shape[1]
    ka = oa.shape[1]
    kb = ob.shape[1]
    row = lambda j, i: (i, 0)
    col = lambda j, i: (0, j)
    return pl.pallas_call(
        _merge_kernel,
        grid=(n // tn, m // tm),
        in_specs=[pl.BlockSpec((tm, d), row), pl.BlockSpec((tm, ka), row),
                  pl.BlockSpec((tm, kb), row),
                  _row_window([row_a + t * tn for t in range(n // tn)], tn, d),
                  _row_window([row_b + t * tn for t in range(n // tn)], tn, d),
                  pl.BlockSpec((ka, tn), col), pl.BlockSpec((kb, tn), col)],
        out_specs=pl.BlockSpec((tm, tn), lambda j, i: (i, j)),
        out_shape=jax.ShapeDtypeStruct((m, n), BF16),
        scratch_shapes=[pltpu.VMEM((d, tn), BF16), pltpu.VMEM((d, tn), BF16),
                        pltpu.VMEM((ka, tn), BF16), pltpu.VMEM((kb, tn), BF16)],
        compiler_params=_cparams(("arbitrary", "arbitrary")),
        name="gated_merge",
    )(xn, oa, ob, wt, wt, wuf, wun)


def _swiglu_kernel(a_ref, wg_ref, wu_ref, o_ref, bg_ref, bu_ref):
    @pl.when(_first_m_step())
    def _():
        _stage_weight(wg_ref, bg_ref)
        _stage_weight(wu_ref, bu_ref)

    a = a_ref[...]
    gt = jnp.dot(a, bg_ref[...], preferred_element_type=F32)
    up = jnp.dot(a, bu_ref[...], preferred_element_type=F32)
    o_ref[...] = (gt * jax.nn.sigmoid(gt) * up).astype(o_ref.dtype)


def _swiglu(a, wg, wu, tm=1024, tn=512):
    m, k = a.shape
    n = wg.shape[1]
    return pl.pallas_call(
        _swiglu_kernel,
        grid=(n // tn, m // tm),
        in_specs=[pl.BlockSpec((tm, k), lambda j, i: (i, 0)),
                  pl.BlockSpec((k, tn), lambda j, i: (0, j)),
                  pl.BlockSpec((k, tn), lambda j, i: (0, j))],
        out_specs=pl.BlockSpec((tm, tn), lambda j, i: (i, j)),
        out_shape=jax.ShapeDtypeStruct((m, n), BF16),
        scratch_shapes=[pltpu.VMEM((k, tn), BF16), pltpu.VMEM((k, tn), BF16)],
        compiler_params=_cparams(("arbitrary", "arbitrary")),
        name="swiglu_up",
    )(a, wg, wu)


def _split3(x):
    hi = x.astype(BF16)
    r1 = x - hi.astype(F32)
    mid = r1.astype(BF16)
    lo = (r1 - mid.astype(F32)).astype(BF16)
    return hi, mid, lo


def _decay_kernel(z_ref, b_ref, ccol_ref, crow_ref, *, blk):
    t = z_ref.shape[0]
    r = lax.broadcasted_iota(jnp.int32, (blk, blk), 0)
    c = lax.broadcasted_iota(jnp.int32, (blk, blk), 1)
    tri = jnp.where(r >= c, 1.0, 0.0).astype(BF16)
    carry = jnp.zeros((1, LANES), F32)
    for s in range(t // blk):
        rows = slice(s * blk, (s + 1) * blk)
        z = z_ref[rows, :] + b_ref[...]
        logf = (jnp.minimum(z, 0.0) - jnp.log1p(jnp.exp(-jnp.abs(z)))) * LOG2E
        hi, mid, lo = _split3(logf)
        cb = (jnp.dot(tri, hi, preferred_element_type=F32)
              + jnp.dot(tri, mid, preferred_element_type=F32)
              + jnp.dot(tri, lo, preferred_element_type=F32)) + carry
        carry = cb[blk - 1:blk, :]
        ccol_ref[rows, :] = cb
        crow_ref[0, :, rows] = cb.T[:FOX_HEADS, :]


def _decay(p3, bias_row, batch, seq, blk=256):
    return pl.pallas_call(
        functools.partial(_decay_kernel, blk=blk),
        grid=(batch,),
        in_specs=[pl.BlockSpec((seq, LANES), lambda b: (b, 6)),
                  pl.BlockSpec((1, LANES), lambda b: (0, 0))],
        out_specs=[pl.BlockSpec((seq, LANES), lambda b: (b, 0)),
                   pl.BlockSpec((1, FOX_HEADS, seq), lambda b: (b, 0, 0))],
        out_shape=[jax.ShapeDtypeStruct((batch * seq, LANES), F32),
                   jax.ShapeDtypeStruct((batch, FOX_HEADS, seq), F32)],
        compiler_params=_cparams(("arbitrary",)),
        name="fox_decay_cumsum",
    )(p3, bias_row)


def _transpose_bf16(x):
    return x.astype(F32).T.astype(BF16)


def _normalised(acc, d):
    return acc[:d, :] * (1.0 / jnp.maximum(acc[d:d + 1, :], 1e-30))


def _fox_kernel(q_ref, k_ref, v_ref, ccol_ref, crow_ref, o_ref, vt_ref, ka_ref, *, tq, nq, nh):
    hh = pl.program_id(1)
    i = pl.program_id(2)
    dh = FOX_HEAD_DIM

    def pieces(x, lane, first):
        hi = x.astype(BF16).astype(F32)
        r1 = x - hi
        mid = r1.astype(BF16).astype(F32)
        return jnp.where(lane == first, hi,
                         jnp.where(lane == first + 1, mid,
                                   jnp.where(lane == first + 2, r1 - mid, 0.0)))

    @pl.when(i == 0)
    def _():
        cc = ccol_ref[...]
        lane = lax.broadcasted_iota(jnp.int32, cc.shape, 1)
        for h in range(nh):
            vt_ref[h, 0:dh, :] = _transpose_bf16(v_ref[:, h * dh:(h + 1) * dh])
            vt_ref[h, dh:, :] = jnp.ones((vt_ref.shape[1] - dh, vt_ref.shape[2]), BF16)
            cj = jnp.sum(jnp.where(lane == hh * nh + h, cc, 0.0), axis=-1, keepdims=True)
            extra = pieces(-cj, lane, 0) + jnp.where(lane < 3, 0.0, jnp.where(lane < 6, 1.0, 0.0))
            ka_ref[h, :, 0:dh] = k_ref[:, h * dh:(h + 1) * dh]
            ka_ref[h, :, dh:] = extra.astype(BF16)

    row = lax.broadcasted_iota(jnp.int32, (dh, tq), 0)
    qts = []
    for h in range(nh):
        ci = crow_ref[0, h]
        extra = pieces(ci, row, 3) + jnp.where(row < 3, 1.0, 0.0)
        qts.append(jnp.concatenate(
            [_transpose_bf16(q_ref[:, h * dh:(h + 1) * dh]), extra.astype(BF16)], axis=0))
    rk = lax.broadcasted_iota(jnp.int32, (tq, tq), 0)
    cq = lax.broadcasted_iota(jnp.int32, (tq, tq), 1)

    def scores(h, k0):
        return jnp.dot(ka_ref[h, k0:k0 + tq, :], qts[h], preferred_element_type=F32)

    def variant(n):
        def run():
            starts = [t * tq for t in range(n, -1, -1)]
            tiles = [[jnp.where(rk <= cq, scores(h, starts[0]), NEG)]
                     + [scores(h, k0) for k0 in starts[1:]] for h in range(nh)]
            ms = [jnp.max(tiles[h][0], axis=0, keepdims=True) for h in range(nh)]
            accs = [jnp.dot(vt_ref[h, :, starts[0]:starts[0] + tq],
                            jnp.exp2(tiles[h][0] - ms[h]).astype(BF16),
                            preferred_element_type=F32) for h in range(nh)]
            for t in range(1, n + 1):
                k0 = starts[t]
                for h in range(nh):
                    st = tiles[h][t]
                    m_new = jnp.maximum(ms[h], jnp.max(st, axis=0, keepdims=True))
                    accs[h] = jnp.exp2(ms[h] - m_new) * accs[h] + jnp.dot(
                        vt_ref[h, :, k0:k0 + tq], jnp.exp2(st - m_new).astype(BF16),
                        preferred_element_type=F32)
                    ms[h] = m_new
            for h in range(nh):
                o_ref[:, h * dh:(h + 1) * dh] = _normalised(accs[h], dh).T.astype(o_ref.dtype)
        return run

    lax.switch(i, [variant(n) for n in range(nq)])


def _fox_attention(p1, ccol, crow4, batch, seq, tq=512, nh=4):
    nq = seq // tq
    hg = FOX_HEADS // nh
    w = nh * FOX_HEAD_DIM
    return pl.pallas_call(
        functools.partial(_fox_kernel, tq=tq, nq=nq, nh=nh),
        grid=(batch, hg, nq),
        in_specs=[pl.BlockSpec((tq, w), lambda b, hh, i: (b * nq + i, hh)),
                  pl.BlockSpec((seq, w), lambda b, hh, i: (b, hg + hh)),
                  pl.BlockSpec((seq, w), lambda b, hh, i: (b, 2 * hg + hh)),
                  pl.BlockSpec((seq, LANES), lambda b, hh, i: (b, 0)),
                  pl.BlockSpec((1, nh, 1, tq), lambda b, hh, i: (b, hh, 0, i))],
        out_specs=pl.BlockSpec((tq, w), lambda b, hh, i: (b * nq + i, hh)),
        out_shape=jax.ShapeDtypeStruct((batch * seq, FOX_WIDTH), BF16),
        scratch_shapes=[pltpu.VMEM((nh, FOX_HEAD_DIM + BF16_SUBLANES, seq), BF16),
                        pltpu.VMEM((nh, seq, 2 * FOX_HEAD_DIM), BF16)],
        compiler_params=_cparams(("arbitrary", "arbitrary", "arbitrary")),
        name="fox_attention",
    )(p1, p1, p1, ccol, crow4)


def _compress_one(z_refs, pe_ref, w1_ref, w2_ref, nblk):
    half = CMP_BLOCK // 2
    first = jnp.zeros((nblk, CMP_HIDDEN), F32)
    second = jnp.zeros((nblk, CMP_HIDDEN), F32)
    for p in range(half):
        rows = pl.ds(p, nblk, stride=CMP_STRIDE)
        zp = [z_ref[rows, :] for z_ref in z_refs]
        zp = zp[0] if len(zp) == 1 else jnp.concatenate(zp, axis=1)
        first += jnp.dot((zp + pe_ref[p:p + 1, :]).astype(BF16), w1_ref[p],
                         preferred_element_type=F32)
        second += jnp.dot((zp + pe_ref[half + p:half + p + 1, :]).astype(BF16),
                          w1_ref[half + p], preferred_element_type=F32)
    hid = first + pltpu.roll(second, nblk - 1, 0)
    act = (hid * jax.nn.sigmoid(hid)).astype(BF16)
    return jnp.dot(act, w2_ref[...], preferred_element_type=F32)


def _compress_kernel(zk0_ref, zk1_ref, zv_ref, pek_ref, w1k_ref, w2k_ref, gk_ref,
                     pev_ref, w1v_ref, w2v_ref, kc_ref, vc_ref, *, nblk):
    kc = _compress_one((zk0_ref, zk1_ref), pek_ref, w1k_ref, w2k_ref, nblk)
    ms = jnp.sum(kc * kc, axis=-1, keepdims=True) * (1.0 / NSA_QK_DIM)
    kc = kc * lax.rsqrt(ms + RMS_EPS) * gk_ref[...]
    pos = CMP_STRIDE * lax.broadcasted_iota(jnp.int32, (nblk, LANES), 0) + (CMP_BLOCK - 1)
    up = NSA_QK_PAD - LANES
    kc_ref[:, :up] = kc[:, :up].astype(kc_ref.dtype)
    kc_ref[:, up:] = (kc[:, up:] + _key_aug(pos)).astype(kc_ref.dtype)
    vc = _compress_one((zv_ref,), pev_ref, w1v_ref, w2v_ref, nblk)
    vc_ref[...] = vc.T.astype(vc_ref.dtype)


def _compress(p3, pek, w1k, w2k, gk, pev, w1v, w2v, batch, seq):
    g = NSA_KV_GROUPS
    nblk = seq // CMP_STRIDE
    full2 = lambda b, gg: (0, 0)
    full3 = lambda b, gg: (0, 0, 0)
    return pl.pallas_call(
        functools.partial(_compress_kernel, nblk=nblk),
        grid=(batch, g),
        in_specs=[pl.BlockSpec((seq, LANES), lambda b, gg: (b, 2 * gg)),
                  pl.BlockSpec((seq, LANES), lambda b, gg: (b, 2 * gg + 1)),
                  pl.BlockSpec((seq, NSA_V_DIM), lambda b, gg: (b, 4 + gg)),
                  pl.BlockSpec(pek.shape, full2), pl.BlockSpec(w1k.shape, full3),
                  pl.BlockSpec(w2k.shape, full2), pl.BlockSpec(gk.shape, full2),
                  pl.BlockSpec(pev.shape, full2), pl.BlockSpec(w1v.shape, full3),
                  pl.BlockSpec(w2v.shape, full2)],
        out_specs=[pl.BlockSpec((nblk, NSA_QK_PAD), lambda b, gg: (b * g + gg, 0)),
                   pl.BlockSpec((NSA_V_DIM, nblk), lambda b, gg: (b * g + gg, 0))],
        out_shape=[jax.ShapeDtypeStruct((batch * g * nblk, NSA_QK_PAD), BF16),
                   jax.ShapeDtypeStruct((batch * g * NSA_V_DIM, nblk), BF16)],
        compiler_params=_cparams(("arbitrary", "arbitrary")),
        name="nsa_compress",
    )(p3, p3, p3, pek, w1k, w2k, gk, pev, w1v, w2v)


def _q_heads_t(q_ref):
    qb = q_ref[...]
    return jnp.concatenate(
        [_transpose_bf16(qb[:, hh * NSA_QK_PAD:(hh + 1) * NSA_QK_PAD]) for hh in range(NSA_HPG)],
        axis=1)


def _gate_rows(glt_ref, g, hh):
    base = FOX_HEADS + (g * NSA_HPG + hh) * 3
    return [glt_ref[pl.ds(base + br, 1), :] for br in range(3)]


def _nsa_select_kernel(q_ref, kc_ref, vct_ref, gl_ref, ovt_ref,
                       sel_ref, ocmp_ref, glt_ref, *, tq, n_cmp, n_sel):
    g = pl.program_id(1)
    t0 = pl.program_id(2) * tq
    hpg = NSA_HPG
    dv = NSA_V_DIM
    q4t = _q_heads_t(q_ref)
    rk = lax.broadcasted_iota(jnp.int32, (LANES, tq), 0)
    cq = lax.broadcasted_iota(jnp.int32, (LANES, tq), 1)

    s_c = jnp.dot(kc_ref[...], q4t, preferred_element_type=F32)
    dist_c = (t0 + cq) - (CMP_STRIDE * rk + (CMP_BLOCK - 1))
    mask_c = jnp.where(rk < n_cmp, dist_c, -1) >= 0
    probs = []
    p_sum = jnp.zeros((LANES, tq), F32)
    for hh in range(hpg):
        sm = jnp.where(mask_c, s_c[:, hh * tq:(hh + 1) * tq], NEG)
        m = jnp.max(sm, axis=0, keepdims=True)
        e = jnp.where(mask_c, jnp.exp2(sm - m), 0.0)
        p = e * (1.0 / jnp.maximum(jnp.sum(e, axis=0, keepdims=True), 1e-30))
        probs.append(p)
        p_sum = p_sum + p
    o_cmp = jnp.dot(vct_ref[...], jnp.concatenate(probs, axis=1).astype(BF16),
                    preferred_element_type=F32)

    ph = p_sum.astype(BF16)
    plo = (p_sum - ph.astype(F32)).astype(BF16)
    ovt = ovt_ref[...]
    imp = (jnp.dot(ovt, ph, preferred_element_type=F32)
           + jnp.dot(ovt, plo, preferred_element_type=F32))[:n_sel, :]

    rj = lax.broadcasted_iota(jnp.int32, (n_sel, tq), 0)
    tcol = t0 + lax.broadcasted_iota(jnp.int32, (n_sel, tq), 1)
    back = (tcol >> (SEL_BLOCK.bit_length() - 1)) - rj
    elig = back >= 0
    forced = jnp.where(rj == 0, 0, jnp.where(elig, back, SEL_LOCAL)) < SEL_LOCAL
    score = jnp.where(elig, jnp.where(forced, FORCE_SCORE, imp), -1.0)
    rank = jnp.zeros((n_sel, tq), F32)
    for jp in range(n_sel):
        row = score[jp:jp + 1, :]
        later = jnp.where(rj > jp, 1.0, 0.0)
        rank = rank + jnp.where(row > score, 1.0, jnp.where(row == score, later, 0.0))
    sel_ref[0:n_sel, :] = jnp.where(elig, jnp.where(rank < SEL_TOPK, 0.0, NEG), NEG)
    sel_ref[n_sel:, :] = jnp.full((LANES - n_sel, tq), NEG, F32)

    glt_ref[...] = jax.nn.sigmoid(gl_ref[...]).T
    for hh in range(hpg):
        gate = _gate_rows(glt_ref, g, hh)[0]
        ocmp_ref[:, hh * dv:(hh + 1) * dv] = (gate * o_cmp[:, hh * tq:(hh + 1) * tq]).T


def _nsa_select(pq, p3, kc, vct, ovt, batch, seq, tq=512):
    g = NSA_KV_GROUPS
    nq = seq // tq
    nblk = seq // CMP_STRIDE
    n_cmp = nblk - CMP_BLOCK // CMP_STRIDE + 1
    return pl.pallas_call(
        functools.partial(_nsa_select_kernel, tq=tq, n_cmp=n_cmp, n_sel=seq // SEL_BLOCK),
        grid=(batch, g, nq),
        in_specs=[
            pl.BlockSpec((tq, NSA_HPG * NSA_QK_PAD), lambda b, gg, i: (b * nq + i, gg)),
            pl.BlockSpec((nblk, NSA_QK_PAD), lambda b, gg, i: (b * g + gg, 0)),
            pl.BlockSpec((NSA_V_DIM, nblk), lambda b, gg, i: (b * g + gg, 0)),
            pl.BlockSpec((tq, LANES), lambda b, gg, i: (b * nq + i, 6)),
            pl.BlockSpec(ovt.shape, lambda b, gg, i: (0, 0)),
        ],
        out_specs=[pl.BlockSpec((LANES, tq), lambda b, gg, i: (b * g + gg, i)),
                   pl.BlockSpec((tq, NSA_HPG * NSA_V_DIM), lambda b, gg, i: (b * nq + i, gg))],
        out_shape=[jax.ShapeDtypeStruct((batch * g * LANES, seq), F32),
                   jax.ShapeDtypeStruct((batch * seq, NSA_WIDTH), F32)],
        scratch_shapes=[pltpu.VMEM((LANES, tq), F32)],
        compiler_params=_cparams(("arbitrary", "arbitrary", "arbitrary")),
        name="nsa_select",
    )(pq, kc, vct, p3, ovt)


def _nsa_attend_kernel(q_ref, ks_ref, vs_ref, kw_ref, vw_ref, gl_ref, sel_ref,
                       ocmp_ref, o_ref, vst_ref, vwt_ref, glt_ref, *, tq, n_var):
    i = pl.program_id(1)
    t0 = i * tq
    hpg = NSA_HPG
    ng = NSA_KV_GROUPS
    dv = NSA_V_DIM
    dk = NSA_QK_PAD

    @pl.when(i == 0)
    def _():
        ones = jnp.ones((vst_ref.shape[1] - dv, vst_ref.shape[2]), BF16)
        for gg in range(ng):
            vst_ref[gg, 0:dv, :] = _transpose_bf16(vs_ref[:, gg * dv:(gg + 1) * dv])
            vst_ref[gg, dv:, :] = ones
            vwt_ref[gg, 0:dv, :] = _transpose_bf16(vw_ref[:, gg * dv:(gg + 1) * dv])
            vwt_ref[gg, dv:, :] = ones

    qb = q_ref[...]
    q4t = [jnp.concatenate(
        [_transpose_bf16(qb[:, (gg * hpg + hh) * dk:(gg * hpg + hh + 1) * dk])
         for hh in range(hpg)], axis=1) for gg in range(ng)]
    glt_ref[...] = jax.nn.sigmoid(gl_ref[...]).T

    def distance(k0, rows):
        return (lax.broadcasted_iota(jnp.int32, (rows, tq), 1)
                - lax.broadcasted_iota(jnp.int32, (rows, tq), 0)) + (t0 - k0)

    def scores(gg, k, mask_bias):
        st = jnp.dot(k, q4t[gg], preferred_element_type=F32)
        return st + jnp.concatenate([mask_bias] * hpg, axis=1)

    def selection_bias(gg, k0):
        j0 = gg * LANES + k0 // SEL_BLOCK
        return jnp.concatenate(
            [jnp.broadcast_to(sel_ref[j0 + j:j0 + j + 1, :], (SEL_BLOCK, tq))
             for j in range(SLC_TILE // SEL_BLOCK)], axis=0)

    def variant(n):
        def run():
            kw0 = pl.multiple_of(jnp.maximum(t0 - WINDOW, 0), LANES)
            wrows = WINDOW + tq
            dist_w = distance(kw0, wrows)
            bias_w = jnp.where(jnp.where(dist_w >= 0, dist_w, WINDOW) < WINDOW, 0.0, NEG)
            o_win = []
            for gg in range(ng):
                sw = scores(gg, kw_ref[pl.ds(kw0, wrows), gg * dk:(gg + 1) * dk], bias_w)
                pw = jnp.exp2(sw - jnp.max(sw, axis=0, keepdims=True))
                o_win.append(_normalised(
                    jnp.dot(vwt_ref[gg, :, pl.ds(kw0, wrows)], pw.astype(BF16),
                            preferred_element_type=F32), dv))

            starts = [t * SLC_TILE for t in range(n, -1, -1)]
            causal = distance(starts[0], SLC_TILE) >= 0
            tiles = [[] for _ in range(ng)]
            for t, k0 in enumerate(starts):
                for gg in range(ng):
                    bias = selection_bias(gg, k0)
                    if t == 0:
                        bias = jnp.where(causal, bias, NEG)
                    tiles[gg].append(scores(
                        gg, ks_ref[k0:k0 + SLC_TILE, gg * dk:(gg + 1) * dk], bias))
            ms = [jnp.max(tiles[gg][0], axis=0, keepdims=True) for gg in range(ng)]
            accs = [jnp.dot(vst_ref[gg, :, starts[0]:starts[0] + SLC_TILE],
                            jnp.exp2(tiles[gg][0] - ms[gg]).astype(BF16),
                            preferred_element_type=F32) for gg in range(ng)]
            for t in range(1, n + 1):
                k0 = starts[t]
                for gg in range(ng):
                    st = tiles[gg][t]
                    m_new = jnp.maximum(ms[gg], jnp.max(st, axis=0, keepdims=True))
                    accs[gg] = jnp.exp2(ms[gg] - m_new) * accs[gg] + jnp.dot(
                        vst_ref[gg, :, k0:k0 + SLC_TILE], jnp.exp2(st - m_new).astype(BF16),
                        preferred_element_type=F32)
                    ms[gg] = m_new

            for gg in range(ng):
                o_slc = _normalised(accs[gg], dv)
                for hh in range(hpg):
                    _, g_slc, g_win = _gate_rows(glt_ref, gg, hh)
                    lanes = slice(hh * tq, (hh + 1) * tq)
                    out = g_slc * o_slc[:, lanes] + g_win * o_win[gg][:, lanes]
                    cols = slice((gg * hpg + hh) * dv, (gg * hpg + hh + 1) * dv)
                    o_ref[:, cols] = (ocmp_ref[:, cols] + out.T).astype(o_ref.dtype)
        return run

    lax.switch(t0 // SLC_TILE, [variant(n) for n in range(n_var)])


def _nsa_attend(pq, pk, pv, p3, sel, ocmp, batch, seq):
    tq = LANES
    g = NSA_KV_GROUPS
    nq = seq // tq
    vrows = NSA_V_DIM + BF16_SUBLANES
    return pl.pallas_call(
        functools.partial(_nsa_attend_kernel, tq=tq, n_var=seq // SLC_TILE),
        grid=(batch, nq),
        in_specs=[
            pl.BlockSpec((tq, NSA_HEADS * NSA_QK_PAD), lambda b, i: (b * nq + i, 0)),
            pl.BlockSpec((seq, g * NSA_QK_PAD), lambda b, i: (b, 0)),
            pl.BlockSpec((seq, g * NSA_V_DIM), lambda b, i: (b, 0)),
            pl.BlockSpec((seq, g * NSA_QK_PAD), lambda b, i: (b, 1)),
            pl.BlockSpec((seq, g * NSA_V_DIM), lambda b, i: (b, 1)),
            pl.BlockSpec((tq, LANES), lambda b, i: (b * nq + i, 6)),
            pl.BlockSpec((g * LANES, tq), lambda b, i: (b, i)),
            pl.BlockSpec((tq, NSA_WIDTH), lambda b, i: (b * nq + i, 0)),
        ],
        out_specs=pl.BlockSpec((tq, NSA_WIDTH), lambda b, i: (b * nq + i, 0)),
        out_shape=jax.ShapeDtypeStruct((batch * seq, NSA_WIDTH), BF16),
        scratch_shapes=[pltpu.VMEM((g, vrows, seq), BF16), pltpu.VMEM((g, vrows, seq), BF16),
                        pltpu.VMEM((LANES, tq), F32)],
        compiler_params=_cparams(("arbitrary", "arbitrary")),
        name="nsa_attend",
    )(pq, pk, pv, pk, pv, p3, sel, ocmp)


def _pad_head_rows(wt, heads):
    k = wt.shape[1]
    wt = wt.reshape(heads, NSA_QK_DIM, k)
    wt = jnp.pad(wt, ((0, 0), (0, NSA_QK_PAD - NSA_QK_DIM), (0, 0)))
    return wt.reshape(heads * NSA_QK_PAD, k)


def _pad_gain(gain, scale=1.0):
    return jnp.pad(gain * scale, (0, NSA_QK_PAD - NSA_QK_DIM))


def _overlap_matrix(nc, ns):
    i = np.arange(nc)[:, None]
    j = np.arange(ns)[None, :]
    lo = np.maximum(i * CMP_STRIDE, j * SEL_BLOCK)
    hi = np.minimum(i * CMP_STRIDE + CMP_BLOCK, (j + 1) * SEL_BLOCK)
    return (np.maximum(hi - lo, 0) / CMP_STRIDE).astype(np.float32)


def kernel(x, norm_attn, w_in, fox_f_bias, fox_q_gain, fox_k_gain,
           nsa_q_gain, nsa_kc_gain, nsa_ks_gain, nsa_kw_gain,
           cmp_pe_k, cmp_w1_k, cmp_w2_k, cmp_pe_v, cmp_w1_v, cmp_w2_v,
           w_up_fox, w_up_nsa, w_out, norm_ffn, w_ffn_gate, w_ffn_up, w_ffn_down):
    batch, seq, d = x.shape
    m = batch * seq
    depth = w_in.shape[0]
    pts = [0] + [int(p) for p in np.cumsum(IN_SPLITS)]
    nblk = seq // CMP_STRIDE
    n_cmp = nblk - CMP_BLOCK // CMP_STRIDE + 1
    ns = seq // SEL_BLOCK

    slope = jnp.exp2(-8.0 * jnp.arange(1, NSA_HEADS + 1, dtype=F32) / NSA_HEADS) * LOG2E
    s1, s2, s3 = [p.astype(F32) for p in _split3(slope)]
    q_const = jnp.stack([256.0 * s1, 256.0 * s2, 256.0 * s3, s1, s2, s3], axis=1)
    lane0 = NSA_QK_DIM
    aug_q = jnp.zeros((2, NSA_HEADS, NSA_QK_PAD), F32)
    aug_q = aug_q.at[0, :, lane0:lane0 + 6].set(q_const)
    aug_q = aug_q.at[1, :, lane0 + 6:lane0 + 9].set(-slope[:, None])
    aug_q = aug_q.reshape(2, NSA_HEADS * NSA_QK_PAD)
    ovt_np = np.zeros((LANES, nblk), np.float32)
    ovt_np[:ns, :n_cmp] = _overlap_matrix(n_cmp, ns).T
    ovt = jnp.asarray(ovt_np, BF16)

    w_in_t = jnp.swapaxes(w_in, 1, 2)

    xf = x.reshape(m, d)
    for l in range(depth):
        wt = w_in_t[l]
        row = dict(zip(("fq", "fk", "fv", "fl", "nq", "kc", "vc", "ks", "vs", "kw", "vw", "ng",
                        "ga", "gb"), pts))

        gain1 = jnp.concatenate([jnp.tile(fox_q_gain[l] * (FOX_HEAD_DIM ** -0.5 * LOG2E), FOX_HEADS),
                                 jnp.tile(fox_k_gain[l], FOX_HEADS),
                                 jnp.ones((FOX_WIDTH,), F32)])
        flag1 = jnp.concatenate([jnp.ones((2 * FOX_WIDTH,), F32), jnp.zeros((FOX_WIDTH,), F32)])
        gain_q = jnp.tile(_pad_gain(nsa_q_gain[l], NSA_QK_DIM ** -0.5 * LOG2E), NSA_HEADS)
        gain_k = jnp.concatenate([jnp.tile(_pad_gain(nsa_ks_gain[l]), NSA_KV_GROUPS),
                                  jnp.tile(_pad_gain(nsa_kw_gain[l]), NSA_KV_GROUPS)])
        n_small = FOX_HEADS + 3 * NSA_HEADS
        w3 = jnp.concatenate([_pad_head_rows(wt[row["kc"]:row["vc"]], NSA_KV_GROUPS),
                              wt[row["vc"]:row["ks"]], wt[row["fl"]:row["nq"]],
                              wt[row["ng"]:row["ga"]],
                              jnp.zeros((LANES - n_small, d), F32)], axis=0)

        xn = _rmsnorm(xf, norm_attn[l])
        q_tile = NSA_HPG * NSA_QK_DIM
        p1 = _project(xn, wt, [t * 1024 for t in range(3 * FOX_WIDTH // 1024)], 1024, gain1, flag1,
                      BF16, 1024, "proj_fox", group=128, count=128)
        pq = _project(xn, wt, [row["nq"], row["nq"] + q_tile], q_tile, gain_q,
                      jnp.ones_like(gain_q), BF16, 1024, "proj_nsa_q", pad_from=NSA_QK_DIM,
                      pad_to=NSA_QK_PAD, group=NSA_QK_PAD, count=NSA_QK_DIM,
                      aug="query", seq=seq, aug_table=aug_q)
        w_k = jnp.concatenate([wt[row["ks"]:row["vs"]], wt[row["kw"]:row["vw"]]], axis=0)
        w_v = jnp.concatenate([wt[row["vs"]:row["kw"]], wt[row["vw"]:row["ng"]]], axis=0)
        pk = _project(xn, w_k, [0], 2 * KV_K, gain_k, jnp.ones_like(gain_k), BF16,
                      1024, "proj_nsa_k", pad_from=NSA_QK_DIM, pad_to=NSA_QK_PAD,
                      group=NSA_QK_PAD, count=NSA_QK_DIM, aug="key", seq=seq)
        pv = _project(xn, w_v, [0], 2 * KV_V, None, None, BF16, 1024, "proj_nsa_v")
        p3 = _project(xn, w3, [0], w3.shape[0], None, None, F32, 1024, "proj_f32")

        bias_row = jnp.pad(fox_f_bias[l], (0, LANES - FOX_HEADS)).reshape(1, LANES)
        ccol, crow = _decay(p3, bias_row, batch, seq)
        o_a = _fox_attention(p1, ccol, crow.reshape(batch, FOX_HEADS, 1, seq), batch, seq)

        pad_d = NSA_QK_PAD - NSA_QK_DIM
        pek = jnp.pad(cmp_pe_k[l], ((0, 0), (0, pad_d)))
        w1k = jnp.pad(cmp_w1_k[l].reshape(CMP_BLOCK, NSA_QK_DIM, CMP_HIDDEN),
                      ((0, 0), (0, pad_d), (0, 0))).astype(BF16)
        w2k = jnp.pad(cmp_w2_k[l], ((0, 0), (0, pad_d))).astype(BF16)
        gk = _pad_gain(nsa_kc_gain[l]).reshape(1, NSA_QK_PAD)
        w1v = cmp_w1_v[l].reshape(CMP_BLOCK, NSA_V_DIM, CMP_HIDDEN).astype(BF16)
        w2v = cmp_w2_v[l].astype(BF16)
        kc, vct = _compress(p3, pek, w1k, w2k, gk, cmp_pe_v[l], w1v, w2v, batch, seq)
        sel, ocmp = _nsa_select(pq, p3, kc, vct, ovt, batch, seq)
        o_b = _nsa_attend(pq, pk, pv, p3, sel, ocmp, batch, seq)

        merged = _merge(xn, o_a, o_b, wt, row["ga"], row["gb"], w_up_fox[l], w_up_nsa[l])
        hres, hn = _out_proj_norm(merged, w_out[l], xf, norm_ffn[l])

        act = _swiglu(hn, w_ffn_gate[l], w_ffn_up[l])
        xf = _matmul_residual(act, w_ffn_down[l], hres, 256, 1024, "ffn_down")
    return xf.reshape(batch, seq, d)
```

```python
import functools

import numpy as np
import jax
import jax.numpy as jnp
from jax import lax
from jax.experimental import pallas as pl
from jax.experimental.pallas import tpu as pltpu

F32 = jnp.float32
BF16 = jnp.bfloat16

D_MODEL = 2048
FOX_HEADS = 8
FOX_HEAD_DIM = 128
FOX_WIDTH = FOX_HEADS * FOX_HEAD_DIM
NSA_HEADS = 8
NSA_KV_GROUPS = 2
NSA_HPG = NSA_HEADS // NSA_KV_GROUPS
NSA_QK_DIM = 192
NSA_QK_PAD = 256
NSA_V_DIM = 128
NSA_WIDTH = NSA_HEADS * NSA_V_DIM
CMP_BLOCK = 32
CMP_STRIDE = 16
CMP_HIDDEN = 256
SEL_BLOCK = 64
SEL_TOPK = 16
SEL_LOCAL = 2
FORCE_SCORE = 1.0e4
WINDOW = 512
KV_K = NSA_KV_GROUPS * NSA_QK_DIM
KV_V = NSA_KV_GROUPS * NSA_V_DIM
D_FF = -(-(8 * D_MODEL) // (3 * 256)) * 256
RMS_EPS = 1e-6
IN_SPLITS = (FOX_WIDTH, FOX_WIDTH, FOX_WIDTH, FOX_HEADS,
             NSA_HEADS * NSA_QK_DIM, KV_K, KV_V, KV_K, KV_V, KV_K, KV_V,
             3 * NSA_HEADS, D_MODEL, D_MODEL)

LANES = 128
SUBLANES = 8
BF16_SUBLANES = 16
NEG = -1.0e30
LOG2E = 1.4426950408889634
SLC_TILE = 512
VMEM_LIMIT = 56 * 1024 * 1024


def _cparams(sem):
    return pltpu.CompilerParams(dimension_semantics=sem, vmem_limit_bytes=VMEM_LIMIT)


def _rms_kernel(x_ref, g_ref, o_ref):
    x = x_ref[...]
    ms = jnp.mean(x * x, axis=-1, keepdims=True)
    o_ref[...] = (x * lax.rsqrt(ms + RMS_EPS) * g_ref[...]).astype(o_ref.dtype)


def _rmsnorm(x, gain, tm=512):
    m, d = x.shape
    return pl.pallas_call(
        _rms_kernel,
        grid=(m // tm,),
        in_specs=[pl.BlockSpec((tm, d), lambda i: (i, 0)),
                  pl.BlockSpec((1, d), lambda i: (0, 0))],
        out_specs=pl.BlockSpec((tm, d), lambda i: (i, 0)),
        out_shape=jax.ShapeDtypeStruct((m, d), BF16),
        compiler_params=_cparams(("arbitrary",)),
        name="rmsnorm",
    )(x, gain.reshape(1, d))


STAGE_CHUNK = 512
EPILOGUE_SPLIT = 4


def _stage_weight(w_ref, wb_ref, transposed=False, pad_from=0, pad_to=0):
    if not transposed:
        wb_ref[...] = w_ref[...].astype(BF16)
        return
    n_in, k = w_ref.shape
    for c in range(k // STAGE_CHUNK):
        cols = slice(c * STAGE_CHUNK, (c + 1) * STAGE_CHUNK)
        w = w_ref[:, cols]
        if pad_from != pad_to:
            zero = jnp.zeros((pad_to - pad_from, STAGE_CHUNK), F32)
            w = jnp.concatenate(
                [piece for h in range(n_in // pad_from)
                 for piece in (w[h * pad_from:(h + 1) * pad_from, :], zero)], axis=0)
        wb_ref[cols, :] = w.T.astype(BF16)


def _first_m_step():
    return pl.program_id(1) == 0


AUG_LANE = NSA_QK_DIM - LANES


def _key_aug(pos):
    lane = lax.broadcasted_iota(jnp.int32, pos.shape, 1)
    hi = (pos >> 8).astype(F32)
    lo = (pos & 255).astype(F32)
    return jnp.where(lane < AUG_LANE, 0.0,
                     jnp.where(lane < AUG_LANE + 3, hi,
                               jnp.where(lane < AUG_LANE + 6, lo,
                                         jnp.where(lane < AUG_LANE + 9, 1.0, 0.0))))


def _query_aug(pos, const_row, slope_row):
    u = pos.astype(F32) * slope_row
    hi = u.astype(BF16).astype(F32)
    r1 = u - hi
    mid = r1.astype(BF16).astype(F32)
    lane = lax.broadcasted_iota(jnp.int32, pos.shape, 1)
    return const_row + jnp.where(lane == AUG_LANE + 6, hi,
                                 jnp.where(lane == AUG_LANE + 7, mid, r1 - mid))


def _proj_kernel(*refs, pad_from, pad_to, group, count, aug, seq):
    if aug == "query":
        a_ref, w_ref, gain_ref, flag_ref, aug_ref, o_ref, wb_ref = refs
    else:
        a_ref, w_ref, gain_ref, flag_ref, o_ref, wb_ref = refs

    @pl.when(_first_m_step())
    def _():
        _stage_weight(w_ref, wb_ref, True, pad_from, pad_to)

    if group == 0:
        o_ref[...] = jnp.dot(a_ref[...], wb_ref[...],
                             preferred_element_type=F32).astype(o_ref.dtype)
        return
    tm = a_ref.shape[0]
    rows = tm // EPILOGUE_SPLIT
    ys = [jnp.dot(a_ref[r * rows:(r + 1) * rows, :], wb_ref[...], preferred_element_type=F32)
          for r in range(EPILOGUE_SPLIT)]
    for r, y in enumerate(ys):
        if aug:
            pos = ((pl.program_id(1) * tm) % seq + r * rows
                   + lax.broadcasted_iota(jnp.int32, (rows, LANES), 0))
        for c in range(y.shape[1] // group):
            sl = slice(c * group, (c + 1) * group)
            yc = y[:, sl]
            ss = jnp.sum(yc * yc, axis=-1, keepdims=True)
            rs = lax.rsqrt(ss * (1.0 / count) + RMS_EPS)
            scale = jnp.where(flag_ref[:, sl] > 0.0, rs, 1.0)
            out = yc * scale * gain_ref[:, sl]
            rsl = slice(r * rows, (r + 1) * rows)
            if not aug:
                o_ref[rsl, sl] = out.astype(o_ref.dtype)
                continue
            up = slice((c + 1) * group - LANES, (c + 1) * group)
            extra = (_key_aug(pos) if aug == "key"
                     else _query_aug(pos, aug_ref[0:1, up], aug_ref[1:2, up]))
            o_ref[rsl, c * group:(c + 1) * group - LANES] = out[:, :group - LANES].astype(o_ref.dtype)
            o_ref[rsl, up] = (out[:, group - LANES:] + extra).astype(o_ref.dtype)


def _row_window(starts, rows, k):
    def index(j, i):
        start = starts[-1]
        for t in range(len(starts) - 2, -1, -1):
            start = jnp.where(j == t, starts[t], start)
        return pl.multiple_of(start, SUBLANES), 0
    assert all(s % SUBLANES == 0 for s in starts)
    return pl.BlockSpec((pl.Element(rows), pl.Element(k)), index)


def _project(a, wt, starts, tn_in, gain, flag, out_dtype, tm, name,
             pad_from=0, pad_to=0, group=0, count=1, aug="", seq=0, aug_table=None):
    m, k = a.shape
    n_tiles = len(starts)
    tn_out = tn_in if pad_from == pad_to else tn_in // pad_from * pad_to
    n_out = n_tiles * tn_out
    if gain is None:
        gain = jnp.ones((n_out,), F32)
        flag = jnp.zeros((n_out,), F32)
    assert not aug or seq % tm == 0
    operands = [a, wt, gain.reshape(1, n_out), flag.reshape(1, n_out)]
    in_specs = [pl.BlockSpec((tm, k), lambda j, i: (i, 0)),
                _row_window(starts, tn_in, k),
                pl.BlockSpec((1, tn_out), lambda j, i: (0, j)),
                pl.BlockSpec((1, tn_out), lambda j, i: (0, j))]
    if aug == "query":
        operands.append(aug_table)
        in_specs.append(pl.BlockSpec((2, tn_out), lambda j, i: (0, j)))
    return pl.pallas_call(
        functools.partial(_proj_kernel, pad_from=pad_from, pad_to=pad_to, group=group,
                          count=count, aug=aug, seq=seq),
        grid=(n_tiles, m // tm),
        in_specs=in_specs,
        out_specs=pl.BlockSpec((tm, tn_out), lambda j, i: (i, j)),
        out_shape=jax.ShapeDtypeStruct((m, n_out), out_dtype),
        scratch_shapes=[pltpu.VMEM((k, tn_out), BF16)],
        compiler_params=_cparams(("arbitrary", "arbitrary")),
        name=name,
    )(*operands)


def _out_norm_kernel(a_ref, w_ref, r_ref, g_ref, h_ref, hn_ref, wb_ref):
    @pl.when(pl.program_id(0) == 0)
    def _():
        _stage_weight(w_ref, wb_ref)

    h = r_ref[...] + jnp.dot(a_ref[...], wb_ref[...], preferred_element_type=F32)
    h_ref[...] = h
    ms = jnp.mean(h * h, axis=-1, keepdims=True)
    hn_ref[...] = (h * lax.rsqrt(ms + RMS_EPS) * g_ref[...]).astype(hn_ref.dtype)


def _out_proj_norm(a, w, res, gain, tm=512):
    m, k = a.shape
    n = w.shape[1]
    full = lambda i: (0, 0)
    row = lambda i: (i, 0)
    return pl.pallas_call(
        _out_norm_kernel,
        grid=(m // tm,),
        in_specs=[pl.BlockSpec((tm, k), row),
                  pl.BlockSpec((k, n), full, pipeline_mode=pl.Buffered(1)),
                  pl.BlockSpec((tm, n), row), pl.BlockSpec((1, n), full)],
        out_specs=[pl.BlockSpec((tm, n), row), pl.BlockSpec((tm, n), row)],
        out_shape=[jax.ShapeDtypeStruct((m, n), F32), jax.ShapeDtypeStruct((m, n), BF16)],
        scratch_shapes=[pltpu.VMEM((k, n), BF16)],
        compiler_params=_cparams(("arbitrary",)),
        name="out_proj_norm",
    )(a, w, res, gain.reshape(1, n))


def _mm_res_kernel(a_ref, w_ref, r_ref, o_ref, wb_ref):
    @pl.when(_first_m_step())
    def _():
        _stage_weight(w_ref, wb_ref)

    o_ref[...] = r_ref[...] + jnp.dot(a_ref[...], wb_ref[...], preferred_element_type=F32)


def _matmul_residual(a, w, res, tm, tn, name):
    m, k = a.shape
    n = w.shape[1]
    return pl.pallas_call(
        _mm_res_kernel,
        grid=(n // tn, m // tm),
        in_specs=[pl.BlockSpec((tm, k), lambda j, i: (i, 0)),
                  pl.BlockSpec((k, tn), lambda j, i: (0, j)),
                  pl.BlockSpec((tm, tn), lambda j, i: (i, j))],
        out_specs=pl.BlockSpec((tm, tn), lambda j, i: (i, j)),
        out_shape=jax.ShapeDtypeStruct((m, n), F32),
        scratch_shapes=[pltpu.VMEM((k, tn), BF16)],
        compiler_params=_cparams(("arbitrary", "arbitrary")),
        name=name,
    )(a, w, res)


def _merge_kernel(xn_ref, oa_ref, ob_ref, wga_ref, wgb_ref, wuf_ref, wun_ref, o_ref,
                  bga_ref, bgb_ref, buf_ref, bun_ref):
    @pl.when(_first_m_step())
    def _():
        _stage_weight(wga_ref, bga_ref, True)
        _stage_weight(wgb_ref, bgb_ref, True)
        _stage_weight(wuf_ref, buf_ref)
        _stage_weight(wun_ref, bun_ref)

    xn = xn_ref[...]
    ga = jax.nn.sigmoid(jnp.dot(xn, bga_ref[...], preferred_element_type=F32))
    ua = jnp.dot(oa_ref[...], buf_ref[...], preferred_element_type=F32)
    acc = ga * ua
    gb = jax.nn.sigmoid(jnp.dot(xn, bgb_ref[...], preferred_element_type=F32))
    ub = jnp.dot(ob_ref[...], bun_ref[...], preferred_element_type=F32)
    o_ref[...] = (acc + gb * ub).astype(o_ref.dtype)


def _merge(xn, oa, ob, wt, row_a, row_b, wuf, wun, tm=512, tn=512):
    m, d = xn.shape
    n = wuf.shape[1]
    ka = oa.shape[1]
    kb = ob.shape[1]
    row = lambda j, i: (i, 0)
    col = lambda j, i: (0, j)
    return pl.pallas_call(
        _merge_kernel,
        grid=(n // tn, m // tm),
        in_specs=[pl.BlockSpec((tm, d), row), pl.BlockSpec((tm, ka), row),
                  pl.BlockSpec((tm, kb), row),
                  _row_window([row_a + t * tn for t in range(n // tn)], tn, d),
                  _row_window([row_b + t * tn for t in range(n // tn)], tn, d),
                  pl.BlockSpec((ka, tn), col), pl.BlockSpec((kb, tn), col)],
        out_specs=pl.BlockSpec((tm, tn), lambda j, i: (i, j)),
        out_shape=jax.ShapeDtypeStruct((m, n), BF16),
        scratch_shapes=[pltpu.VMEM((d, tn), BF16), pltpu.VMEM((d, tn), BF16),
                        pltpu.VMEM((ka, tn), BF16), pltpu.VMEM((kb, tn), BF16)],
        compiler_params=_cparams(("arbitrary", "arbitrary")),
        name="gated_merge",
    )(xn, oa, ob, wt, wt, wuf, wun)


def _swiglu_kernel(a_ref, wg_ref, wu_ref, o_ref, bg_ref, bu_ref):
    @pl.when(_first_m_step())
    def _():
        _stage_weight(wg_ref, bg_ref)
        _stage_weight(wu_ref, bu_ref)

    a = a_ref[...]
    gt = jnp.dot(a, bg_ref[...], preferred_element_type=F32)
    up = jnp.dot(a, bu_ref[...], preferred_element_type=F32)
    o_ref[...] = (gt * jax.nn.sigmoid(gt) * up).astype(o_ref.dtype)


def _swiglu(a, wg, wu, tm=1024, tn=512):
    m, k = a.shape
    n = wg.shape[1]
    return pl.pallas_call(
        _swiglu_kernel,
        grid=(n // tn, m // tm),
        in_specs=[pl.BlockSpec((tm, k), lambda j, i: (i, 0)),
                  pl.BlockSpec((k, tn), lambda j, i: (0, j)),
                  pl.BlockSpec((k, tn), lambda j, i: (0, j))],
        out_specs=pl.BlockSpec((tm, tn), lambda j, i: (i, j)),
        out_shape=jax.ShapeDtypeStruct((m, n), BF16),
        scratch_shapes=[pltpu.VMEM((k, tn), BF16), pltpu.VMEM((k, tn), BF16)],
        compiler_params=_cparams(("arbitrary", "arbitrary")),
        name="swiglu_up",
    )(a, wg, wu)


def _split3(x):
    hi = x.astype(BF16)
    r1 = x - hi.astype(F32)
    mid = r1.astype(BF16)
    lo = (r1 - mid.astype(F32)).astype(BF16)
    return hi, mid, lo


def _decay_kernel(z_ref, b_ref, ccol_ref, crow_ref, *, blk):
    t = z_ref.shape[0]
    r = lax.broadcasted_iota(jnp.int32, (blk, blk), 0)
    c = lax.broadcasted_iota(jnp.int32, (blk, blk), 1)
    tri = jnp.where(r >= c, 1.0, 0.0).astype(BF16)
    carry = jnp.zeros((1, LANES), F32)
    for s in range(t // blk):
        rows = slice(s * blk, (s + 1) * blk)
        z = z_ref[rows, :] + b_ref[...]
        logf = (jnp.minimum(z, 0.0) - jnp.log1p(jnp.exp(-jnp.abs(z)))) * LOG2E
        hi, mid, lo = _split3(logf)
        cb = (jnp.dot(tri, hi, preferred_element_type=F32)
              + jnp.dot(tri, mid, preferred_element_type=F32)
              + jnp.dot(tri, lo, preferred_element_type=F32)) + carry
        carry = cb[blk - 1:blk, :]
        ccol_ref[rows, :] = cb
        crow_ref[0, :, rows] = cb.T[:FOX_HEADS, :]


def _decay(p3, bias_row, batch, seq, blk=256):
    return pl.pallas_call(
        functools.partial(_decay_kernel, blk=blk),
        grid=(batch,),
        in_specs=[pl.BlockSpec((seq, LANES), lambda b: (b, 6)),
                  pl.BlockSpec((1, LANES), lambda b: (0, 0))],
        out_specs=[pl.BlockSpec((seq, LANES), lambda b: (b, 0)),
                   pl.BlockSpec((1, FOX_HEADS, seq), lambda b: (b, 0, 0))],
        out_shape=[jax.ShapeDtypeStruct((batch * seq, LANES), F32),
                   jax.ShapeDtypeStruct((batch, FOX_HEADS, seq), F32)],
        compiler_params=_cparams(("arbitrary",)),
        name="fox_decay_cumsum",
    )(p3, bias_row)


def _transpose_bf16(x):
    return x.astype(F32).T.astype(BF16)


def _normalised(acc, d):
    return acc[:d, :] * (1.0 / jnp.maximum(acc[d:d + 1, :], 1e-30))


def _fox_kernel(q_ref, k_ref, v_ref, ccol_ref, crow_ref, o_ref, vt_ref, ka_ref, *, tq, nq, nh):
    hh = pl.program_id(1)
    i = pl.program_id(2)
    dh = FOX_HEAD_DIM

    def pieces(x, lane, first):
        hi = x.astype(BF16).astype(F32)
        r1 = x - hi
        mid = r1.astype(BF16).astype(F32)
        return jnp.where(lane == first, hi,
                         jnp.where(lane == first + 1, mid,
                                   jnp.where(lane == first + 2, r1 - mid, 0.0)))

    @pl.when(i == 0)
    def _():
        cc = ccol_ref[...]
        lane = lax.broadcasted_iota(jnp.int32, cc.shape, 1)
        for h in range(nh):
            vt_ref[h, 0:dh, :] = _transpose_bf16(v_ref[:, h * dh:(h + 1) * dh])
            vt_ref[h, dh:, :] = jnp.ones((vt_ref.shape[1] - dh, vt_ref.shape[2]), BF16)
            cj = jnp.sum(jnp.where(lane == hh * nh + h, cc, 0.0), axis=-1, keepdims=True)
            extra = pieces(-cj, lane, 0) + jnp.where(lane < 3, 0.0, jnp.where(lane < 6, 1.0, 0.0))
            ka_ref[h, :, 0:dh] = k_ref[:, h * dh:(h + 1) * dh]
            ka_ref[h, :, dh:] = extra.astype(BF16)

    row = lax.broadcasted_iota(jnp.int32, (dh, tq), 0)
    qts = []
    for h in range(nh):
        ci = crow_ref[0, h]
        extra = pieces(ci, row, 3) + jnp.where(row < 3, 1.0, 0.0)
        qts.append(jnp.concatenate(
            [_transpose_bf16(q_ref[:, h * dh:(h + 1) * dh]), extra.astype(BF16)], axis=0))
    rk = lax.broadcasted_iota(jnp.int32, (tq, tq), 0)
    cq = lax.broadcasted_iota(jnp.int32, (tq, tq), 1)

    def scores(h, k0):
        return jnp.dot(ka_ref[h, k0:k0 + tq, :], qts[h], preferred_element_type=F32)

    def variant(n):
        def run():
            starts = [t * tq for t in range(n, -1, -1)]
            tiles = [[jnp.where(rk <= cq, scores(h, starts[0]), NEG)]
                     + [scores(h, k0) for k0 in starts[1:]] for h in range(nh)]
            ms = [jnp.max(tiles[h][0], axis=0, keepdims=True) for h in range(nh)]
            accs = [jnp.dot(vt_ref[h, :, starts[0]:starts[0] + tq],
                            jnp.exp2(tiles[h][0] - ms[h]).astype(BF16),
                            preferred_element_type=F32) for h in range(nh)]
            for t in range(1, n + 1):
                k0 = starts[t]
                for h in range(nh):
                    st = tiles[h][t]
                    m_new = jnp.maximum(ms[h], jnp.max(st, axis=0, keepdims=True))
                    accs[h] = jnp.exp2(ms[h] - m_new) * accs[h] + jnp.dot(
                        vt_ref[h, :, k0:k0 + tq], jnp.exp2(st - m_new).astype(BF16),
                        preferred_element_type=F32)
                    ms[h] = m_new
            for h in range(nh):
                o_ref[:, h * dh:(h + 1) * dh] = _normalised(accs[h], dh).T.astype(o_ref.dtype)
        return run

    lax.switch(i, [variant(n) for n in range(nq)])


def _fox_attention(p1, ccol, crow4, batch, seq, tq=512, nh=4):
    nq = seq // tq
    hg = FOX_HEADS // nh
    w = nh * FOX_HEAD_DIM
    return pl.pallas_call(
        functools.partial(_fox_kernel, tq=tq, nq=nq, nh=nh),
        grid=(batch, hg, nq),
        in_specs=[pl.BlockSpec((tq, w), lambda b, hh, i: (b * nq + i, hh)),
                  pl.BlockSpec((seq, w), lambda b, hh, i: (b, hg + hh)),
                  pl.BlockSpec((seq, w), lambda b, hh, i: (b, 2 * hg + hh)),
                  pl.BlockSpec((seq, LANES), lambda b, hh, i: (b, 0)),
                  pl.BlockSpec((1, nh, 1, tq), lambda b, hh, i: (b, hh, 0, i))],
        out_specs=pl.BlockSpec((tq, w), lambda b, hh, i: (b * nq + i, hh)),
        out_shape=jax.ShapeDtypeStruct((batch * seq, FOX_WIDTH), BF16),
        scratch_shapes=[pltpu.VMEM((nh, FOX_HEAD_DIM + BF16_SUBLANES, seq), BF16),
                        pltpu.VMEM((nh, seq, 2 * FOX_HEAD_DIM), BF16)],
        compiler_params=_cparams(("arbitrary", "arbitrary", "arbitrary")),
        name="fox_attention",
    )(p1, p1, p1, ccol, crow4)


def _compress_one(z_refs, pe_ref, w1_ref, w2_ref, nblk):
    half = CMP_BLOCK // 2
    first = jnp.zeros((nblk, CMP_HIDDEN), F32)
    second = jnp.zeros((nblk, CMP_HIDDEN), F32)
    for p in range(half):
        rows = pl.ds(p, nblk, stride=CMP_STRIDE)
        zp = [z_ref[rows, :] for z_ref in z_refs]
        zp = zp[0] if len(zp) == 1 else jnp.concatenate(zp, axis=1)
        first += jnp.dot((zp + pe_ref[p:p + 1, :]).astype(BF16), w1_ref[p],
                         preferred_element_type=F32)
        second += jnp.dot((zp + pe_ref[half + p:half + p + 1, :]).astype(BF16),
                          w1_ref[half + p], preferred_element_type=F32)
    hid = first + pltpu.roll(second, nblk - 1, 0)
    act = (hid * jax.nn.sigmoid(hid)).astype(BF16)
    return jnp.dot(act, w2_ref[...], preferred_element_type=F32)


def _compress_kernel(zk0_ref, zk1_ref, zv_ref, pek_ref, w1k_ref, w2k_ref, gk_ref,
                     pev_ref, w1v_ref, w2v_ref, kc_ref, vc_ref, *, nblk):
    kc = _compress_one((zk0_ref, zk1_ref), pek_ref, w1k_ref, w2k_ref, nblk)
    ms = jnp.sum(kc * kc, axis=-1, keepdims=True) * (1.0 / NSA_QK_DIM)
    kc = kc * lax.rsqrt(ms + RMS_EPS) * gk_ref[...]
    pos = CMP_STRIDE * lax.broadcasted_iota(jnp.int32, (nblk, LANES), 0) + (CMP_BLOCK - 1)
    up = NSA_QK_PAD - LANES
    kc_ref[:, :up] = kc[:, :up].astype(kc_ref.dtype)
    kc_ref[:, up:] = (kc[:, up:] + _key_aug(pos)).astype(kc_ref.dtype)
    vc = _compress_one((zv_ref,), pev_ref, w1v_ref, w2v_ref, nblk)
    vc_ref[...] = vc.T.astype(vc_ref.dtype)


def _compress(p3, pek, w1k, w2k, gk, pev, w1v, w2v, batch, seq):
    g = NSA_KV_GROUPS
    nblk = seq // CMP_STRIDE
    full2 = lambda b, gg: (0, 0)
    full3 = lambda b, gg: (0, 0, 0)
    return pl.pallas_call(
        functools.partial(_compress_kernel, nblk=nblk),
        grid=(batch, g),
        in_specs=[pl.BlockSpec((seq, LANES), lambda b, gg: (b, 2 * gg)),
                  pl.BlockSpec((seq, LANES), lambda b, gg: (b, 2 * gg + 1)),
                  pl.BlockSpec((seq, NSA_V_DIM), lambda b, gg: (b, 4 + gg)),
                  pl.BlockSpec(pek.shape, full2), pl.BlockSpec(w1k.shape, full3),
                  pl.BlockSpec(w2k.shape, full2), pl.BlockSpec(gk.shape, full2),
                  pl.BlockSpec(pev.shape, full2), pl.BlockSpec(w1v.shape, full3),
                  pl.BlockSpec(w2v.shape, full2)],
        out_specs=[pl.BlockSpec((nblk, NSA_QK_PAD), lambda b, gg: (b * g + gg, 0)),
                   pl.BlockSpec((NSA_V_DIM, nblk), lambda b, gg: (b * g + gg, 0))],
        out_shape=[jax.ShapeDtypeStruct((batch * g * nblk, NSA_QK_PAD), BF16),
                   jax.ShapeDtypeStruct((batch * g * NSA_V_DIM, nblk), BF16)],
        compiler_params=_cparams(("arbitrary", "arbitrary")),
        name="nsa_compress",
    )(p3, p3, p3, pek, w1k, w2k, gk, pev, w1v, w2v)


def _q_heads_t(q_ref):
    qb = q_ref[...]
    return jnp.concatenate(
        [_transpose_bf16(qb[:, hh * NSA_QK_PAD:(hh + 1) * NSA_QK_PAD]) for hh in range(NSA_HPG)],
        axis=1)


def _gate_rows(glt_ref, g, hh):
    base = FOX_HEADS + (g * NSA_HPG + hh) * 3
    return [glt_ref[pl.ds(base + br, 1), :] for br in range(3)]


def _nsa_select_kernel(q_ref, kc_ref, vct_ref, gl_ref, ovt_ref,
                       sel_ref, ocmp_ref, glt_ref, *, tq, n_cmp, n_sel):
    g = pl.program_id(1)
    t0 = pl.program_id(2) * tq
    hpg = NSA_HPG
    dv = NSA_V_DIM
    q4t = _q_heads_t(q_ref)
    rk = lax.broadcasted_iota(jnp.int32, (LANES, tq), 0)
    cq = lax.broadcasted_iota(jnp.int32, (LANES, tq), 1)

    s_c = jnp.dot(kc_ref[...], q4t, preferred_element_type=F32)
    dist_c = (t0 + cq) - (CMP_STRIDE * rk + (CMP_BLOCK - 1))
    mask_c = jnp.where(rk < n_cmp, dist_c, -1) >= 0
    probs = []
    p_sum = jnp.zeros((LANES, tq), F32)
    for hh in range(hpg):
        sm = jnp.where(mask_c, s_c[:, hh * tq:(hh + 1) * tq], NEG)
        m = jnp.max(sm, axis=0, keepdims=True)
        e = jnp.where(mask_c, jnp.exp2(sm - m), 0.0)
        p = e * (1.0 / jnp.maximum(jnp.sum(e, axis=0, keepdims=True), 1e-30))
        probs.append(p)
        p_sum = p_sum + p
    o_cmp = jnp.dot(vct_ref[...], jnp.concatenate(probs, axis=1).astype(BF16),
                    preferred_element_type=F32)

    ph = p_sum.astype(BF16)
    plo = (p_sum - ph.astype(F32)).astype(BF16)
    ovt = ovt_ref[...]
    imp = (jnp.dot(ovt, ph, preferred_element_type=F32)
           + jnp.dot(ovt, plo, preferred_element_type=F32))[:n_sel, :]

    rj = lax.broadcasted_iota(jnp.int32, (n_sel, tq), 0)
    tcol = t0 + lax.broadcasted_iota(jnp.int32, (n_sel, tq), 1)
    back = (tcol >> (SEL_BLOCK.bit_length() - 1)) - rj
    elig = back >= 0
    forced = jnp.where(rj == 0, 0, jnp.where(elig, back, SEL_LOCAL)) < SEL_LOCAL
    score = jnp.where(elig, jnp.where(forced, FORCE_SCORE, imp), -1.0)
    rank = jnp.zeros((n_sel, tq), F32)
    for jp in range(n_sel):
        row = score[jp:jp + 1, :]
        later = jnp.where(rj > jp, 1.0, 0.0)
        rank = rank + jnp.where(row > score, 1.0, jnp.where(row == score, later, 0.0))
    sel_ref[0:n_sel, :] = jnp.where(elig, jnp.where(rank < SEL_TOPK, 0.0, NEG), NEG)
    sel_ref[n_sel:, :] = jnp.full((LANES - n_sel, tq), NEG, F32)

    glt_ref[...] = jax.nn.sigmoid(gl_ref[...]).T
    for hh in range(hpg):
        gate = _gate_rows(glt_ref, g, hh)[0]
        ocmp_ref[:, hh * dv:(hh + 1) * dv] = (gate * o_cmp[:, hh * tq:(hh + 1) * tq]).T


def _nsa_select(pq, p3, kc, vct, ovt, batch, seq, tq=512):
    g = NSA_KV_GROUPS
    nq = seq // tq
    nblk = seq // CMP_STRIDE
    n_cmp = nblk - CMP_BLOCK // CMP_STRIDE + 1
    return pl.pallas_call(
        functools.partial(_nsa_select_kernel, tq=tq, n_cmp=n_cmp, n_sel=seq // SEL_BLOCK),
        grid=(batch, g, nq),
        in_specs=[
            pl.BlockSpec((tq, NSA_HPG * NSA_QK_PAD), lambda b, gg, i: (b * nq + i, gg)),
            pl.BlockSpec((nblk, NSA_QK_PAD), lambda b, gg, i: (b * g + gg, 0)),
            pl.BlockSpec((NSA_V_DIM, nblk), lambda b, gg, i: (b * g + gg, 0)),
            pl.BlockSpec((tq, LANES), lambda b, gg, i: (b * nq + i, 6)),
            pl.BlockSpec(ovt.shape, lambda b, gg, i: (0, 0)),
        ],
        out_specs=[pl.BlockSpec((LANES, tq), lambda b, gg, i: (b * g + gg, i)),
                   pl.BlockSpec((tq, NSA_HPG * NSA_V_DIM), lambda b, gg, i: (b * nq + i, gg))],
        out_shape=[jax.ShapeDtypeStruct((batch * g * LANES, seq), F32),
                   jax.ShapeDtypeStruct((batch * seq, NSA_WIDTH), F32)],
        scratch_shapes=[pltpu.VMEM((LANES, tq), F32)],
        compiler_params=_cparams(("arbitrary", "arbitrary", "arbitrary")),
        name="nsa_select",
    )(pq, kc, vct, p3, ovt)


def _nsa_attend_kernel(q_ref, ks_ref, vs_ref, kw_ref, vw_ref, gl_ref, sel_ref,
                       ocmp_ref, o_ref, vst_ref, vwt_ref, glt_ref, *, tq, n_var):
    i = pl.program_id(1)
    t0 = i * tq
    hpg = NSA_HPG
    ng = NSA_KV_GROUPS
    dv = NSA_V_DIM
    dk = NSA_QK_PAD

    @pl.when(i == 0)
    def _():
        ones = jnp.ones((vst_ref.shape[1] - dv, vst_ref.shape[2]), BF16)
        for gg in range(ng):
            vst_ref[gg, 0:dv, :] = _transpose_bf16(vs_ref[:, gg * dv:(gg + 1) * dv])
            vst_ref[gg, dv:, :] = ones
            vwt_ref[gg, 0:dv, :] = _transpose_bf16(vw_ref[:, gg * dv:(gg + 1) * dv])
            vwt_ref[gg, dv:, :] = ones

    qb = q_ref[...]
    q4t = [jnp.concatenate(
        [_transpose_bf16(qb[:, (gg * hpg + hh) * dk:(gg * hpg + hh + 1) * dk])
         for hh in range(hpg)], axis=1) for gg in range(ng)]
    glt_ref[...] = jax.nn.sigmoid(gl_ref[...]).T

    def distance(k0, rows):
        return (lax.broadcasted_iota(jnp.int32, (rows, tq), 1)
                - lax.broadcasted_iota(jnp.int32, (rows, tq), 0)) + (t0 - k0)

    def scores(gg, k, mask_bias):
        st = jnp.dot(k, q4t[gg], preferred_element_type=F32)
        return st + jnp.concatenate([mask_bias] * hpg, axis=1)

    def selection_bias(gg, k0):
        j0 = gg * LANES + k0 // SEL_BLOCK
        return jnp.concatenate(
            [jnp.broadcast_to(sel_ref[j0 + j:j0 + j + 1, :], (SEL_BLOCK, tq))
             for j in range(SLC_TILE // SEL_BLOCK)], axis=0)

    def variant(n):
        def run():
            kw0 = pl.multiple_of(jnp.maximum(t0 - WINDOW, 0), LANES)
            wrows = WINDOW + tq
            dist_w = distance(kw0, wrows)
            bias_w = jnp.where(jnp.where(dist_w >= 0, dist_w, WINDOW) < WINDOW, 0.0, NEG)
            o_win = []
            for gg in range(ng):
                sw = scores(gg, kw_ref[pl.ds(kw0, wrows), gg * dk:(gg + 1) * dk], bias_w)
                pw = jnp.exp2(sw - jnp.max(sw, axis=0, keepdims=True))
                o_win.append(_normalised(
                    jnp.dot(vwt_ref[gg, :, pl.ds(kw0, wrows)], pw.astype(BF16),
                            preferred_element_type=F32), dv))

            starts = [t * SLC_TILE for t in range(n, -1, -1)]
            causal = distance(starts[0], SLC_TILE) >= 0
            tiles = [[] for _ in range(ng)]
            for t, k0 in enumerate(starts):
                for gg in range(ng):
                    bias = selection_bias(gg, k0)
                    if t == 0:
                        bias = jnp.where(causal, bias, NEG)
                    tiles[gg].append(scores(
                        gg, ks_ref[k0:k0 + SLC_TILE, gg * dk:(gg + 1) * dk], bias))
            ms = [jnp.max(tiles[gg][0], axis=0, keepdims=True) for gg in range(ng)]
            accs = [jnp.dot(vst_ref[gg, :, starts[0]:starts[0] + SLC_TILE],
                            jnp.exp2(tiles[gg][0] - ms[gg]).astype(BF16),
                            preferred_element_type=F32) for gg in range(ng)]
            for t in range(1, n + 1):
                k0 = starts[t]
                for gg in range(ng):
                    st = tiles[gg][t]
                    m_new = jnp.maximum(ms[gg], jnp.max(st, axis=0, keepdims=True))
                    accs[gg] = jnp.exp2(ms[gg] - m_new) * accs[gg] + jnp.dot(
                        vst_ref[gg, :, k0:k0 + SLC_TILE], jnp.exp2(st - m_new).astype(BF16),
                        preferred_element_type=F32)
                    ms[gg] = m_new

            for gg in range(ng):
                o_slc = _normalised(accs[gg], dv)
                for hh in range(hpg):
                    _, g_slc, g_win = _gate_rows(glt_ref, gg, hh)
                    lanes = slice(hh * tq, (hh + 1) * tq)
                    out = g_slc * o_slc[:, lanes] + g_win * o_win[gg][:, lanes]
                    cols = slice((gg * hpg + hh) * dv, (gg * hpg + hh + 1) * dv)
                    o_ref[:, cols] = (ocmp_ref[:, cols] + out.T).astype(o_ref.dtype)
        return run

    lax.switch(t0 // SLC_TILE, [variant(n) for n in range(n_var)])


def _nsa_attend(pq, pk, pv, p3, sel, ocmp, batch, seq):
    tq = 2 * LANES
    g = NSA_KV_GROUPS
    nq = seq // tq
    vrows = NSA_V_DIM + BF16_SUBLANES
    return pl.pallas_call(
        functools.partial(_nsa_attend_kernel, tq=tq, n_var=seq // SLC_TILE),
        grid=(batch, nq),
        in_specs=[
            pl.BlockSpec((tq, NSA_HEADS * NSA_QK_PAD), lambda b, i: (b * nq + i, 0)),
            pl.BlockSpec((seq, g * NSA_QK_PAD), lambda b, i: (b, 0)),
            pl.BlockSpec((seq, g * NSA_V_DIM), lambda b, i: (b, 0)),
            pl.BlockSpec((seq, g * NSA_QK_PAD), lambda b, i: (b, 1)),
            pl.BlockSpec((seq, g * NSA_V_DIM), lambda b, i: (b, 1)),
            pl.BlockSpec((tq, LANES), lambda b, i: (b * nq + i, 6)),
            pl.BlockSpec((g * LANES, tq), lambda b, i: (b, i)),
            pl.BlockSpec((tq, NSA_WIDTH), lambda b, i: (b * nq + i, 0)),
        ],
        out_specs=pl.BlockSpec((tq, NSA_WIDTH), lambda b, i: (b * nq + i, 0)),
        out_shape=jax.ShapeDtypeStruct((batch * seq, NSA_WIDTH), BF16),
        scratch_shapes=[pltpu.VMEM((g, vrows, seq), BF16), pltpu.VMEM((g, vrows, seq), BF16),
                        pltpu.VMEM((LANES, tq), F32)],
        compiler_params=_cparams(("arbitrary", "arbitrary")),
        name="nsa_attend",
    )(pq, pk, pv, pk, pv, p3, sel, ocmp)


def _pad_head_rows(wt, heads):
    k = wt.shape[1]
    wt = wt.reshape(heads, NSA_QK_DIM, k)
    wt = jnp.pad(wt, ((0, 0), (0, NSA_QK_PAD - NSA_QK_DIM), (0, 0)))
    return wt.reshape(heads * NSA_QK_PAD, k)


def _pad_gain(gain, scale=1.0):
    return jnp.pad(gain * scale, (0, NSA_QK_PAD - NSA_QK_DIM))


def _overlap_matrix(nc, ns):
    i = np.arange(nc)[:, None]
    j = np.arange(ns)[None, :]
    lo = np.maximum(i * CMP_STRIDE, j * SEL_BLOCK)
    hi = np.minimum(i * CMP_STRIDE + CMP_BLOCK, (j + 1) * SEL_BLOCK)
    return (np.maximum(hi - lo, 0) / CMP_STRIDE).astype(np.float32)


def kernel(x, norm_attn, w_in, fox_f_bias, fox_q_gain, fox_k_gain,
           nsa_q_gain, nsa_kc_gain, nsa_ks_gain, nsa_kw_gain,
           cmp_pe_k, cmp_w1_k, cmp_w2_k, cmp_pe_v, cmp_w1_v, cmp_w2_v,
           w_up_fox, w_up_nsa, w_out, norm_ffn, w_ffn_gate, w_ffn_up, w_ffn_down):
    batch, seq, d = x.shape
    m = batch * seq
    depth = w_in.shape[0]
    pts = [0] + [int(p) for p in np.cumsum(IN_SPLITS)]
    nblk = seq // CMP_STRIDE
    n_cmp = nblk - CMP_BLOCK // CMP_STRIDE + 1
    ns = seq // SEL_BLOCK

    slope = jnp.exp2(-8.0 * jnp.arange(1, NSA_HEADS + 1, dtype=F32) / NSA_HEADS) * LOG2E
    s1, s2, s3 = [p.astype(F32) for p in _split3(slope)]
    q_const = jnp.stack([256.0 * s1, 256.0 * s2, 256.0 * s3, s1, s2, s3], axis=1)
    lane0 = NSA_QK_DIM
    aug_q = jnp.zeros((2, NSA_HEADS, NSA_QK_PAD), F32)
    aug_q = aug_q.at[0, :, lane0:lane0 + 6].set(q_const)
    aug_q = aug_q.at[1, :, lane0 + 6:lane0 + 9].set(-slope[:, None])
    aug_q = aug_q.reshape(2, NSA_HEADS * NSA_QK_PAD)
    ovt_np = np.zeros((LANES, nblk), np.float32)
    ovt_np[:ns, :n_cmp] = _overlap_matrix(n_cmp, ns).T
    ovt = jnp.asarray(ovt_np, BF16)

    w_in_t = jnp.swapaxes(w_in, 1, 2)

    xf = x.reshape(m, d)
    for l in range(depth):
        wt = w_in_t[l]
        row = dict(zip(("fq", "fk", "fv", "fl", "nq", "kc", "vc", "ks", "vs", "kw", "vw", "ng",
                        "ga", "gb"), pts))

        gain1 = jnp.concatenate([jnp.tile(fox_q_gain[l] * (FOX_HEAD_DIM ** -0.5 * LOG2E), FOX_HEADS),
                                 jnp.tile(fox_k_gain[l], FOX_HEADS),
                                 jnp.ones((FOX_WIDTH,), F32)])
        flag1 = jnp.concatenate([jnp.ones((2 * FOX_WIDTH,), F32), jnp.zeros((FOX_WIDTH,), F32)])
        gain_q = jnp.tile(_pad_gain(nsa_q_gain[l], NSA_QK_DIM ** -0.5 * LOG2E), NSA_HEADS)
        gain_k = jnp.concatenate([jnp.tile(_pad_gain(nsa_ks_gain[l]), NSA_KV_GROUPS),
                                  jnp.tile(_pad_gain(nsa_kw_gain[l]), NSA_KV_GROUPS)])
        n_small = FOX_HEADS + 3 * NSA_HEADS
        w3 = jnp.concatenate([_pad_head_rows(wt[row["kc"]:row["vc"]], NSA_KV_GROUPS),
                              wt[row["vc"]:row["ks"]], wt[row["fl"]:row["nq"]],
                              wt[row["ng"]:row["ga"]],
                              jnp.zeros((LANES - n_small, d), F32)], axis=0)

        xn = _rmsnorm(xf, norm_attn[l])
        q_tile = NSA_HPG * NSA_QK_DIM
        p1 = _project(xn, wt, [t * 1024 for t in range(3 * FOX_WIDTH // 1024)], 1024, gain1, flag1,
                      BF16, 1024, "proj_fox", group=128, count=128)
        pq = _project(xn, wt, [row["nq"], row["nq"] + q_tile], q_tile, gain_q,
                      jnp.ones_like(gain_q), BF16, 1024, "proj_nsa_q", pad_from=NSA_QK_DIM,
                      pad_to=NSA_QK_PAD, group=NSA_QK_PAD, count=NSA_QK_DIM,
                      aug="query", seq=seq, aug_table=aug_q)
        w_k = jnp.concatenate([wt[row["ks"]:row["vs"]], wt[row["kw"]:row["vw"]]], axis=0)
        w_v = jnp.concatenate([wt[row["vs"]:row["kw"]], wt[row["vw"]:row["ng"]]], axis=0)
        pk = _project(xn, w_k, [0], 2 * KV_K, gain_k, jnp.ones_like(gain_k), BF16,
                      1024, "proj_nsa_k", pad_from=NSA_QK_DIM, pad_to=NSA_QK_PAD,
                      group=NSA_QK_PAD, count=NSA_QK_DIM, aug="key", seq=seq)
        pv = _project(xn, w_v, [0], 2 * KV_V, None, None, BF16, 1024, "proj_nsa_v")
        p3 = _project(xn, w3, [0], w3.shape[0], None, None, F32, 1024, "proj_f32")

        bias_row = jnp.pad(fox_f_bias[l], (0, LANES - FOX_HEADS)).reshape(1, LANES)
        ccol, crow = _decay(p3, bias_row, batch, seq)
        o_a = _fox_attention(p1, ccol, crow.reshape(batch, FOX_HEADS, 1, seq), batch, seq)

        pad_d = NSA_QK_PAD - NSA_QK_DIM
        pek = jnp.pad(cmp_pe_k[l], ((0, 0), (0, pad_d)))
        w1k = jnp.pad(cmp_w1_k[l].reshape(CMP_BLOCK, NSA_QK_DIM, CMP_HIDDEN),
                      ((0, 0), (0, pad_d), (0, 0))).astype(BF16)
        w2k = jnp.pad(cmp_w2_k[l], ((0, 0), (0, pad_d))).astype(BF16)
        gk = _pad_gain(nsa_kc_gain[l]).reshape(1, NSA_QK_PAD)
        w1v = cmp_w1_v[l].reshape(CMP_BLOCK, NSA_V_DIM, CMP_HIDDEN).astype(BF16)
        w2v = cmp_w2_v[l].astype(BF16)
        kc, vct = _compress(p3, pek, w1k, w2k, gk, cmp_pe_v[l], w1v, w2v, batch, seq)
        sel, ocmp = _nsa_select(pq, p3, kc, vct, ovt, batch, seq)
        o_b = _nsa_attend(pq, pk, pv, p3, sel, ocmp, batch, seq)

        merged = _merge(xn, o_a, o_b, wt, row["ga"], row["gb"], w_up_fox[l], w_up_nsa[l])
        hres, hn = _out_proj_norm(merged, w_out[l], xf, norm_ffn[l])

        act = _swiglu(hn, w_ffn_gate[l], w_ffn_up[l])
        xf = _matmul_residual(act, w_ffn_down[l], hres, 512, 512, "ffn_down")
    return xf.reshape(batch, seq, d)
```

```python
import functools

import numpy as np
import jax
import jax.numpy as jnp
from jax import lax
from jax.experimental import pallas as pl
from jax.experimental.pallas import tpu as pltpu

F32 = jnp.float32
BF16 = jnp.bfloat16

D_MODEL = 2048
FOX_HEADS = 8
FOX_HEAD_DIM = 128
FOX_WIDTH = FOX_HEADS * FOX_HEAD_DIM
NSA_HEADS = 8
NSA_KV_GROUPS = 2
NSA_HPG = NSA_HEADS // NSA_KV_GROUPS
NSA_QK_DIM = 192
NSA_QK_PAD = 256
NSA_V_DIM = 128
NSA_WIDTH = NSA_HEADS * NSA_V_DIM
CMP_BLOCK = 32
CMP_STRIDE = 16
CMP_HIDDEN = 256
SEL_BLOCK = 64
SEL_TOPK = 16
SEL_LOCAL = 2
FORCE_SCORE = 1.0e4
WINDOW = 512
KV_K = NSA_KV_GROUPS * NSA_QK_DIM
KV_V = NSA_KV_GROUPS * NSA_V_DIM
D_FF = -(-(8 * D_MODEL) // (3 * 256)) * 256
RMS_EPS = 1e-6
IN_SPLITS = (FOX_WIDTH, FOX_WIDTH, FOX_WIDTH, FOX_HEADS,
             NSA_HEADS * NSA_QK_DIM, KV_K, KV_V, KV_K, KV_V, KV_K, KV_V,
             3 * NSA_HEADS, D_MODEL, D_MODEL)

LANES = 128
SUBLANES = 8
BF16_SUBLANES = 16
NEG = -1.0e30
LOG2E = 1.4426950408889634
SLC_TILE = 512
VMEM_LIMIT = 56 * 1024 * 1024

RMS_TM = 512
PROJ_TM = 1024
FOX_PROJ_TN = 1024
MERGE_TM, MERGE_TN = 512, 512
OUT_PROJ_TM = 512
SWIGLU_TM, SWIGLU_TN = 1024, 512
FFN_DOWN_TM, FFN_DOWN_TN = 512, 512
FOX_TQ = 512
FOX_HEADS_PER_STEP = 4
NSA_SELECT_TQ = 512
NSA_ATTEND_TQ = 256


def _cparams(sem):
    return pltpu.CompilerParams(dimension_semantics=sem, vmem_limit_bytes=VMEM_LIMIT)


def _rms_kernel(x_ref, g_ref, o_ref):
    x = x_ref[...]
    ms = jnp.mean(x * x, axis=-1, keepdims=True)
    o_ref[...] = (x * lax.rsqrt(ms + RMS_EPS) * g_ref[...]).astype(o_ref.dtype)


def _rmsnorm(x, gain, tm=RMS_TM):
    m, d = x.shape
    return pl.pallas_call(
        _rms_kernel,
        grid=(m // tm,),
        in_specs=[pl.BlockSpec((tm, d), lambda i: (i, 0)),
                  pl.BlockSpec((1, d), lambda i: (0, 0))],
        out_specs=pl.BlockSpec((tm, d), lambda i: (i, 0)),
        out_shape=jax.ShapeDtypeStruct((m, d), BF16),
        compiler_params=_cparams(("arbitrary",)),
        name="rmsnorm",
    )(x, gain.reshape(1, d))


STAGE_CHUNK = 512
EPILOGUE_SPLIT = 4


def _stage_weight(w_ref, wb_ref, transposed=False, pad_from=0, pad_to=0):
    if not transposed:
        wb_ref[...] = w_ref[...].astype(BF16)
        return
    n_in, k = w_ref.shape
    for c in range(k // STAGE_CHUNK):
        cols = slice(c * STAGE_CHUNK, (c + 1) * STAGE_CHUNK)
        w = w_ref[:, cols]
        if pad_from != pad_to:
            zero = jnp.zeros((pad_to - pad_from, STAGE_CHUNK), F32)
            w = jnp.concatenate(
                [piece for h in range(n_in // pad_from)
                 for piece in (w[h * pad_from:(h + 1) * pad_from, :], zero)], axis=0)
        wb_ref[cols, :] = w.T.astype(BF16)


def _first_m_step():
    return pl.program_id(1) == 0


AUG_LANE = NSA_QK_DIM - LANES


def _key_aug(pos):
    lane = lax.broadcasted_iota(jnp.int32, pos.shape, 1)
    hi = (pos >> 8).astype(F32)
    lo = (pos & 255).astype(F32)
    return jnp.where(lane < AUG_LANE, 0.0,
                     jnp.where(lane < AUG_LANE + 3, hi,
                               jnp.where(lane < AUG_LANE + 6, lo,
                                         jnp.where(lane < AUG_LANE + 9, 1.0, 0.0))))


def _query_aug(pos, const_row, slope_row):
    u = pos.astype(F32) * slope_row
    hi = u.astype(BF16).astype(F32)
    r1 = u - hi
    mid = r1.astype(BF16).astype(F32)
    lane = lax.broadcasted_iota(jnp.int32, pos.shape, 1)
    return const_row + jnp.where(lane == AUG_LANE + 6, hi,
                                 jnp.where(lane == AUG_LANE + 7, mid, r1 - mid))


def _proj_kernel(*refs, pad_from, pad_to, group, count, aug, seq):
    if aug == "query":
        a_ref, w_ref, gain_ref, flag_ref, aug_ref, o_ref, wb_ref = refs
    else:
        a_ref, w_ref, gain_ref, flag_ref, o_ref, wb_ref = refs

    @pl.when(_first_m_step())
    def _():
        _stage_weight(w_ref, wb_ref, True, pad_from, pad_to)

    if group == 0:
        o_ref[...] = jnp.dot(a_ref[...], wb_ref[...],
                             preferred_element_type=F32).astype(o_ref.dtype)
        return
    tm = a_ref.shape[0]
    rows = tm // EPILOGUE_SPLIT
    ys = [jnp.dot(a_ref[r * rows:(r + 1) * rows, :], wb_ref[...], preferred_element_type=F32)
          for r in range(EPILOGUE_SPLIT)]
    for r, y in enumerate(ys):
        if aug:
            pos = ((pl.program_id(1) * tm) % seq + r * rows
                   + lax.broadcasted_iota(jnp.int32, (rows, LANES), 0))
        for c in range(y.shape[1] // group):
            sl = slice(c * group, (c + 1) * group)
            yc = y[:, sl]
            ss = jnp.sum(yc * yc, axis=-1, keepdims=True)
            rs = lax.rsqrt(ss * (1.0 / count) + RMS_EPS)
            scale = jnp.where(flag_ref[:, sl] > 0.0, rs, 1.0)
            out = yc * scale * gain_ref[:, sl]
            rsl = slice(r * rows, (r + 1) * rows)
            if not aug:
                o_ref[rsl, sl] = out.astype(o_ref.dtype)
                continue
            up = slice((c + 1) * group - LANES, (c + 1) * group)
            extra = (_key_aug(pos) if aug == "key"
                     else _query_aug(pos, aug_ref[0:1, up], aug_ref[1:2, up]))
            o_ref[rsl, c * group:(c + 1) * group - LANES] = out[:, :group - LANES].astype(o_ref.dtype)
            o_ref[rsl, up] = (out[:, group - LANES:] + extra).astype(o_ref.dtype)


def _row_window(starts, rows, k):
    def index(j, i):
        start = starts[-1]
        for t in range(len(starts) - 2, -1, -1):
            start = jnp.where(j == t, starts[t], start)
        return pl.multiple_of(start, SUBLANES), 0
    assert all(s % SUBLANES == 0 for s in starts)
    return pl.BlockSpec((pl.Element(rows), pl.Element(k)), index)


def _project(a, wt, starts, tn_in, gain, flag, out_dtype, tm, name,
             pad_from=0, pad_to=0, group=0, count=1, aug="", seq=0, aug_table=None):
    m, k = a.shape
    n_tiles = len(starts)
    tn_out = tn_in if pad_from == pad_to else tn_in // pad_from * pad_to
    n_out = n_tiles * tn_out
    if gain is None:
        gain = jnp.ones((n_out,), F32)
        flag = jnp.zeros((n_out,), F32)
    assert not aug or seq % tm == 0
    operands = [a, wt, gain.reshape(1, n_out), flag.reshape(1, n_out)]
    in_specs = [pl.BlockSpec((tm, k), lambda j, i: (i, 0)),
                _row_window(starts, tn_in, k),
                pl.BlockSpec((1, tn_out), lambda j, i: (0, j)),
                pl.BlockSpec((1, tn_out), lambda j, i: (0, j))]
    if aug == "query":
        operands.append(aug_table)
        in_specs.append(pl.BlockSpec((2, tn_out), lambda j, i: (0, j)))
    return pl.pallas_call(
        functools.partial(_proj_kernel, pad_from=pad_from, pad_to=pad_to, group=group,
                          count=count, aug=aug, seq=seq),
        grid=(n_tiles, m // tm),
        in_specs=in_specs,
        out_specs=pl.BlockSpec((tm, tn_out), lambda j, i: (i, j)),
        out_shape=jax.ShapeDtypeStruct((m, n_out), out_dtype),
        scratch_shapes=[pltpu.VMEM((k, tn_out), BF16)],
        compiler_params=_cparams(("arbitrary", "arbitrary")),
        name=name,
    )(*operands)


def _out_norm_kernel(a_ref, w_ref, r_ref, g_ref, h_ref, hn_ref, wb_ref):
    @pl.when(pl.program_id(0) == 0)
    def _():
        _stage_weight(w_ref, wb_ref)

    h = r_ref[...] + jnp.dot(a_ref[...], wb_ref[...], preferred_element_type=F32)
    h_ref[...] = h
    ms = jnp.mean(h * h, axis=-1, keepdims=True)
    hn_ref[...] = (h * lax.rsqrt(ms + RMS_EPS) * g_ref[...]).astype(hn_ref.dtype)


def _out_proj_norm(a, w, res, gain, tm=OUT_PROJ_TM):
    m, k = a.shape
    n = w.shape[1]
    full = lambda i: (0, 0)
    row = lambda i: (i, 0)
    return pl.pallas_call(
        _out_norm_kernel,
        grid=(m // tm,),
        in_specs=[pl.BlockSpec((tm, k), row),
                  pl.BlockSpec((k, n), full, pipeline_mode=pl.Buffered(1)),
                  pl.BlockSpec((tm, n), row), pl.BlockSpec((1, n), full)],
        out_specs=[pl.BlockSpec((tm, n), row), pl.BlockSpec((tm, n), row)],
        out_shape=[jax.ShapeDtypeStruct((m, n), F32), jax.ShapeDtypeStruct((m, n), BF16)],
        scratch_shapes=[pltpu.VMEM((k, n), BF16)],
        compiler_params=_cparams(("arbitrary",)),
        name="out_proj_norm",
    )(a, w, res, gain.reshape(1, n))


def _mm_res_kernel(a_ref, w_ref, r_ref, o_ref, wb_ref):
    @pl.when(_first_m_step())
    def _():
        _stage_weight(w_ref, wb_ref)

    o_ref[...] = r_ref[...] + jnp.dot(a_ref[...], wb_ref[...], preferred_element_type=F32)


def _matmul_residual(a, w, res, tm, tn, name):
    m, k = a.shape
    n = w.shape[1]
    return pl.pallas_call(
        _mm_res_kernel,
        grid=(n // tn, m // tm),
        in_specs=[pl.BlockSpec((tm, k), lambda j, i: (i, 0)),
                  pl.BlockSpec((k, tn), lambda j, i: (0, j)),
                  pl.BlockSpec((tm, tn), lambda j, i: (i, j))],
        out_specs=pl.BlockSpec((tm, tn), lambda j, i: (i, j)),
        out_shape=jax.ShapeDtypeStruct((m, n), F32),
        scratch_shapes=[pltpu.VMEM((k, tn), BF16)],
        compiler_params=_cparams(("arbitrary", "arbitrary")),
        name=name,
    )(a, w, res)


def _merge_kernel(xn_ref, oa_ref, ob_ref, wga_ref, wgb_ref, wuf_ref, wun_ref, o_ref,
                  bga_ref, bgb_ref, buf_ref, bun_ref):
    @pl.when(_first_m_step())
    def _():
        _stage_weight(wga_ref, bga_ref, True)
        _stage_weight(wgb_ref, bgb_ref, True)
        _stage_weight(wuf_ref, buf_ref)
        _stage_weight(wun_ref, bun_ref)

    xn = xn_ref[...]
    ga = jax.nn.sigmoid(jnp.dot(xn, bga_ref[...], preferred_element_type=F32))
    ua = jnp.dot(oa_ref[...], buf_ref[...], preferred_element_type=F32)
    acc = ga * ua
    gb = jax.nn.sigmoid(jnp.dot(xn, bgb_ref[...], preferred_element_type=F32))
    ub = jnp.dot(ob_ref[...], bun_ref[...], preferred_element_type=F32)
    o_ref[...] = (acc + gb * ub).astype(o_ref.dtype)


def _merge(xn, oa, ob, wt, row_a, row_b, wuf, wun, tm=MERGE_TM, tn=MERGE_TN):
    m, d = xn.shape
    n = wuf.shape[1]
    ka = oa.shape[1]
    kb = ob.shape[1]
    row = lambda j, i: (i, 0)
    col = lambda j, i: (0, j)
    return pl.pallas_call(
        _merge_kernel,
        grid=(n // tn, m // tm),
        in_specs=[pl.BlockSpec((tm, d), row), pl.BlockSpec((tm, ka), row),
                  pl.BlockSpec((tm, kb), row),
                  _row_window([row_a + t * tn for t in range(n // tn)], tn, d),
                  _row_window([row_b + t * tn for t in range(n // tn)], tn, d),
                  pl.BlockSpec((ka, tn), col), pl.BlockSpec((kb, tn), col)],
        out_specs=pl.BlockSpec((tm, tn), lambda j, i: (i, j)),
        out_shape=jax.ShapeDtypeStruct((m, n), BF16),
        scratch_shapes=[pltpu.VMEM((d, tn), BF16), pltpu.VMEM((d, tn), BF16),
                        pltpu.VMEM((ka, tn), BF16), pltpu.VMEM((kb, tn), BF16)],
        compiler_params=_cparams(("arbitrary", "arbitrary")),
        name="gated_merge",
    )(xn, oa, ob, wt, wt, wuf, wun)


def _swiglu_kernel(a_ref, wg_ref, wu_ref, o_ref, bg_ref, bu_ref):
    @pl.when(_first_m_step())
    def _():
        _stage_weight(wg_ref, bg_ref)
        _stage_weight(wu_ref, bu_ref)

    a = a_ref[...]
    gt = jnp.dot(a, bg_ref[...], preferred_element_type=F32)
    up = jnp.dot(a, bu_ref[...], preferred_element_type=F32)
    o_ref[...] = (gt * jax.nn.sigmoid(gt) * up).astype(o_ref.dtype)


def _swiglu(a, wg, wu, tm=SWIGLU_TM, tn=SWIGLU_TN):
    m, k = a.shape
    n = wg.shape[1]
    return pl.pallas_call(
        _swiglu_kernel,
        grid=(n // tn, m // tm),
        in_specs=[pl.BlockSpec((tm, k), lambda j, i: (i, 0)),
                  pl.BlockSpec((k, tn), lambda j, i: (0, j)),
                  pl.BlockSpec((k, tn), lambda j, i: (0, j))],
        out_specs=pl.BlockSpec((tm, tn), lambda j, i: (i, j)),
        out_shape=jax.ShapeDtypeStruct((m, n), BF16),
        scratch_shapes=[pltpu.VMEM((k, tn), BF16), pltpu.VMEM((k, tn), BF16)],
        compiler_params=_cparams(("arbitrary", "arbitrary")),
        name="swiglu_up",
    )(a, wg, wu)


def _split3(x):
    hi = x.astype(BF16)
    r1 = x - hi.astype(F32)
    mid = r1.astype(BF16)
    lo = (r1 - mid.astype(F32)).astype(BF16)
    return hi, mid, lo


def _decay_kernel(z_ref, b_ref, ccol_ref, crow_ref, *, blk):
    t = z_ref.shape[0]
    r = lax.broadcasted_iota(jnp.int32, (blk, blk), 0)
    c = lax.broadcasted_iota(jnp.int32, (blk, blk), 1)
    tri = jnp.where(r >= c, 1.0, 0.0).astype(BF16)
    carry = jnp.zeros((1, LANES), F32)
    for s in range(t // blk):
        rows = slice(s * blk, (s + 1) * blk)
        z = z_ref[rows, :] + b_ref[...]
        logf = (jnp.minimum(z, 0.0) - jnp.log1p(jnp.exp(-jnp.abs(z)))) * LOG2E
        hi, mid, lo = _split3(logf)
        cb = (jnp.dot(tri, hi, preferred_element_type=F32)
              + jnp.dot(tri, mid, preferred_element_type=F32)
              + jnp.dot(tri, lo, preferred_element_type=F32)) + carry
        carry = cb[blk - 1:blk, :]
        ccol_ref[rows, :] = cb
        crow_ref[0, :, rows] = cb.T[:FOX_HEADS, :]


def _decay(p3, bias_row, batch, seq, blk=256):
    return pl.pallas_call(
        functools.partial(_decay_kernel, blk=blk),
        grid=(batch,),
        in_specs=[pl.BlockSpec((seq, LANES), lambda b: (b, 6)),
                  pl.BlockSpec((1, LANES), lambda b: (0, 0))],
        out_specs=[pl.BlockSpec((seq, LANES), lambda b: (b, 0)),
                   pl.BlockSpec((1, FOX_HEADS, seq), lambda b: (b, 0, 0))],
        out_shape=[jax.ShapeDtypeStruct((batch * seq, LANES), F32),
                   jax.ShapeDtypeStruct((batch, FOX_HEADS, seq), F32)],
        compiler_params=_cparams(("arbitrary",)),
        name="fox_decay_cumsum",
    )(p3, bias_row)


def _transpose_bf16(x):
    return x.astype(F32).T.astype(BF16)


def _normalised(acc, d):
    return acc[:d, :] * (1.0 / jnp.maximum(acc[d:d + 1, :], 1e-30))


def _fox_kernel(q_ref, k_ref, v_ref, ccol_ref, crow_ref, o_ref, vt_ref, ka_ref, *, tq, nq, nh):
    hh = pl.program_id(1)
    i = pl.program_id(2)
    dh = FOX_HEAD_DIM

    def pieces(x, lane, first):
        hi = x.astype(BF16).astype(F32)
        r1 = x - hi
        mid = r1.astype(BF16).astype(F32)
        return jnp.where(lane == first, hi,
                         jnp.where(lane == first + 1, mid,
                                   jnp.where(lane == first + 2, r1 - mid, 0.0)))

    @pl.when(i == 0)
    def _():
        cc = ccol_ref[...]
        lane = lax.broadcasted_iota(jnp.int32, cc.shape, 1)
        for h in range(nh):
            vt_ref[h, 0:dh, :] = _transpose_bf16(v_ref[:, h * dh:(h + 1) * dh])
            vt_ref[h, dh:, :] = jnp.ones((vt_ref.shape[1] - dh, vt_ref.shape[2]), BF16)
            cj = jnp.sum(jnp.where(lane == hh * nh + h, cc, 0.0), axis=-1, keepdims=True)
            extra = pieces(-cj, lane, 0) + jnp.where(lane < 3, 0.0, jnp.where(lane < 6, 1.0, 0.0))
            ka_ref[h, :, 0:dh] = k_ref[:, h * dh:(h + 1) * dh]
            ka_ref[h, :, dh:] = extra.astype(BF16)

    row = lax.broadcasted_iota(jnp.int32, (dh, tq), 0)
    qts = []
    for h in range(nh):
        ci = crow_ref[0, h]
        extra = pieces(ci, row, 3) + jnp.where(row < 3, 1.0, 0.0)
        qts.append(jnp.concatenate(
            [_transpose_bf16(q_ref[:, h * dh:(h + 1) * dh]), extra.astype(BF16)], axis=0))
    rk = lax.broadcasted_iota(jnp.int32, (tq, tq), 0)
    cq = lax.broadcasted_iota(jnp.int32, (tq, tq), 1)

    def scores(h, k0):
        return jnp.dot(ka_ref[h, k0:k0 + tq, :], qts[h], preferred_element_type=F32)

    def variant(n):
        def run():
            starts = [t * tq for t in range(n, -1, -1)]
            tiles = [[jnp.where(rk <= cq, scores(h, starts[0]), NEG)]
                     + [scores(h, k0) for k0 in starts[1:]] for h in range(nh)]
            ms = [jnp.max(tiles[h][0], axis=0, keepdims=True) for h in range(nh)]
            accs = [jnp.dot(vt_ref[h, :, starts[0]:starts[0] + tq],
                            jnp.exp2(tiles[h][0] - ms[h]).astype(BF16),
                            preferred_element_type=F32) for h in range(nh)]
            for t in range(1, n + 1):
                k0 = starts[t]
                for h in range(nh):
                    st = tiles[h][t]
                    m_new = jnp.maximum(ms[h], jnp.max(st, axis=0, keepdims=True))
                    accs[h] = jnp.exp2(ms[h] - m_new) * accs[h] + jnp.dot(
                        vt_ref[h, :, k0:k0 + tq], jnp.exp2(st - m_new).astype(BF16),
                        preferred_element_type=F32)
                    ms[h] = m_new
            for h in range(nh):
                o_ref[:, h * dh:(h + 1) * dh] = _normalised(accs[h], dh).T.astype(o_ref.dtype)
        return run

    lax.switch(i, [variant(n) for n in range(nq)])


def _fox_attention(p1, ccol, crow4, batch, seq, tq=FOX_TQ, nh=FOX_HEADS_PER_STEP):
    nq = seq // tq
    hg = FOX_HEADS // nh
    w = nh * FOX_HEAD_DIM
    return pl.pallas_call(
        functools.partial(_fox_kernel, tq=tq, nq=nq, nh=nh),
        grid=(batch, hg, nq),
        in_specs=[pl.BlockSpec((tq, w), lambda b, hh, i: (b * nq + i, hh)),
                  pl.BlockSpec((seq, w), lambda b, hh, i: (b, hg + hh)),
                  pl.BlockSpec((seq, w), lambda b, hh, i: (b, 2 * hg + hh)),
                  pl.BlockSpec((seq, LANES), lambda b, hh, i: (b, 0)),
                  pl.BlockSpec((1, nh, 1, tq), lambda b, hh, i: (b, hh, 0, i))],
        out_specs=pl.BlockSpec((tq, w), lambda b, hh, i: (b * nq + i, hh)),
        out_shape=jax.ShapeDtypeStruct((batch * seq, FOX_WIDTH), BF16),
        scratch_shapes=[pltpu.VMEM((nh, FOX_HEAD_DIM + BF16_SUBLANES, seq), BF16),
                        pltpu.VMEM((nh, seq, 2 * FOX_HEAD_DIM), BF16)],
        compiler_params=_cparams(("arbitrary", "arbitrary", "arbitrary")),
        name="fox_attention",
    )(p1, p1, p1, ccol, crow4)


def _compress_one(z_refs, pe_ref, w1_ref, w2_ref, nblk):
    half = CMP_BLOCK // 2
    first = jnp.zeros((nblk, CMP_HIDDEN), F32)
    second = jnp.zeros((nblk, CMP_HIDDEN), F32)
    for p in range(half):
        rows = pl.ds(p, nblk, stride=CMP_STRIDE)
        zp = [z_ref[rows, :] for z_ref in z_refs]
        zp = zp[0] if len(zp) == 1 else jnp.concatenate(zp, axis=1)
        first += jnp.dot((zp + pe_ref[p:p + 1, :]).astype(BF16), w1_ref[p],
                         preferred_element_type=F32)
        second += jnp.dot((zp + pe_ref[half + p:half + p + 1, :]).astype(BF16),
                          w1_ref[half + p], preferred_element_type=F32)
    hid = first + pltpu.roll(second, nblk - 1, 0)
    act = (hid * jax.nn.sigmoid(hid)).astype(BF16)
    return jnp.dot(act, w2_ref[...], preferred_element_type=F32)


def _compress_kernel(zk0_ref, zk1_ref, zv_ref, pek_ref, w1k_ref, w2k_ref, gk_ref,
                     pev_ref, w1v_ref, w2v_ref, kc_ref, vc_ref, *, nblk):
    kc = _compress_one((zk0_ref, zk1_ref), pek_ref, w1k_ref, w2k_ref, nblk)
    ms = jnp.sum(kc * kc, axis=-1, keepdims=True) * (1.0 / NSA_QK_DIM)
    kc = kc * lax.rsqrt(ms + RMS_EPS) * gk_ref[...]
    pos = CMP_STRIDE * lax.broadcasted_iota(jnp.int32, (nblk, LANES), 0) + (CMP_BLOCK - 1)
    up = NSA_QK_PAD - LANES
    kc_ref[:, :up] = kc[:, :up].astype(kc_ref.dtype)
    kc_ref[:, up:] = (kc[:, up:] + _key_aug(pos)).astype(kc_ref.dtype)
    vc = _compress_one((zv_ref,), pev_ref, w1v_ref, w2v_ref, nblk)
    vc_ref[...] = vc.T.astype(vc_ref.dtype)


def _compress(p3, pek, w1k, w2k, gk, pev, w1v, w2v, batch, seq):
    g = NSA_KV_GROUPS
    nblk = seq // CMP_STRIDE
    full2 = lambda b, gg: (0, 0)
    full3 = lambda b, gg: (0, 0, 0)
    return pl.pallas_call(
        functools.partial(_compress_kernel, nblk=nblk),
        grid=(batch, g),
        in_specs=[pl.BlockSpec((seq, LANES), lambda b, gg: (b, 2 * gg)),
                  pl.BlockSpec((seq, LANES), lambda b, gg: (b, 2 * gg + 1)),
                  pl.BlockSpec((seq, NSA_V_DIM), lambda b, gg: (b, 4 + gg)),
                  pl.BlockSpec(pek.shape, full2), pl.BlockSpec(w1k.shape, full3),
                  pl.BlockSpec(w2k.shape, full2), pl.BlockSpec(gk.shape, full2),
                  pl.BlockSpec(pev.shape, full2), pl.BlockSpec(w1v.shape, full3),
                  pl.BlockSpec(w2v.shape, full2)],
        out_specs=[pl.BlockSpec((nblk, NSA_QK_PAD), lambda b, gg: (b * g + gg, 0)),
                   pl.BlockSpec((NSA_V_DIM, nblk), lambda b, gg: (b * g + gg, 0))],
        out_shape=[jax.ShapeDtypeStruct((batch * g * nblk, NSA_QK_PAD), BF16),
                   jax.ShapeDtypeStruct((batch * g * NSA_V_DIM, nblk), BF16)],
        compiler_params=_cparams(("arbitrary", "arbitrary")),
        name="nsa_compress",
    )(p3, p3, p3, pek, w1k, w2k, gk, pev, w1v, w2v)


def _q_heads_t(q_ref):
    qb = q_ref[...]
    return jnp.concatenate(
        [_transpose_bf16(qb[:, hh * NSA_QK_PAD:(hh + 1) * NSA_QK_PAD]) for hh in range(NSA_HPG)],
        axis=1)


def _gate_rows(glt_ref, g, hh):
    base = FOX_HEADS + (g * NSA_HPG + hh) * 3
    return [glt_ref[pl.ds(base + br, 1), :] for br in range(3)]


def _nsa_select_kernel(q_ref, kc_ref, vct_ref, gl_ref, ovt_ref,
                       sel_ref, ocmp_ref, glt_ref, *, tq, n_cmp, n_sel):
    g = pl.program_id(1)
    t0 = pl.program_id(2) * tq
    hpg = NSA_HPG
    dv = NSA_V_DIM
    q4t = _q_heads_t(q_ref)
    rk = lax.broadcasted_iota(jnp.int32, (LANES, tq), 0)
    cq = lax.broadcasted_iota(jnp.int32, (LANES, tq), 1)

    s_c = jnp.dot(kc_ref[...], q4t, preferred_element_type=F32)
    dist_c = (t0 + cq) - (CMP_STRIDE * rk + (CMP_BLOCK - 1))
    mask_c = jnp.where(rk < n_cmp, dist_c, -1) >= 0
    probs = []
    p_sum = jnp.zeros((LANES, tq), F32)
    for hh in range(hpg):
        sm = jnp.where(mask_c, s_c[:, hh * tq:(hh + 1) * tq], NEG)
        m = jnp.max(sm, axis=0, keepdims=True)
        e = jnp.where(mask_c, jnp.exp2(sm - m), 0.0)
        p = e * (1.0 / jnp.maximum(jnp.sum(e, axis=0, keepdims=True), 1e-30))
        probs.append(p)
        p_sum = p_sum + p
    o_cmp = jnp.dot(vct_ref[...], jnp.concatenate(probs, axis=1).astype(BF16),
                    preferred_element_type=F32)

    ph = p_sum.astype(BF16)
    plo = (p_sum - ph.astype(F32)).astype(BF16)
    ovt = ovt_ref[...]
    imp = (jnp.dot(ovt, ph, preferred_element_type=F32)
           + jnp.dot(ovt, plo, preferred_element_type=F32))[:n_sel, :]

    rj = lax.broadcasted_iota(jnp.int32, (n_sel, tq), 0)
    tcol = t0 + lax.broadcasted_iota(jnp.int32, (n_sel, tq), 1)
    back = (tcol >> (SEL_BLOCK.bit_length() - 1)) - rj
    elig = back >= 0
    forced = jnp.where(rj == 0, 0, jnp.where(elig, back, SEL_LOCAL)) < SEL_LOCAL
    score = jnp.where(elig, jnp.where(forced, FORCE_SCORE, imp), -1.0)
    rank = jnp.zeros((n_sel, tq), F32)
    for jp in range(n_sel):
        row = score[jp:jp + 1, :]
        later = jnp.where(rj > jp, 1.0, 0.0)
        rank = rank + jnp.where(row > score, 1.0, jnp.where(row == score, later, 0.0))
    sel_ref[0:n_sel, :] = jnp.where(elig, jnp.where(rank < SEL_TOPK, 0.0, NEG), NEG)
    sel_ref[n_sel:, :] = jnp.full((LANES - n_sel, tq), NEG, F32)

    glt_ref[...] = jax.nn.sigmoid(gl_ref[...]).T
    for hh in range(hpg):
        gate = _gate_rows(glt_ref, g, hh)[0]
        ocmp_ref[:, hh * dv:(hh + 1) * dv] = (gate * o_cmp[:, hh * tq:(hh + 1) * tq]).T


def _nsa_select(pq, p3, kc, vct, ovt, batch, seq, tq=NSA_SELECT_TQ):
    g = NSA_KV_GROUPS
    nq = seq // tq
    nblk = seq // CMP_STRIDE
    n_cmp = nblk - CMP_BLOCK // CMP_STRIDE + 1
    return pl.pallas_call(
        functools.partial(_nsa_select_kernel, tq=tq, n_cmp=n_cmp, n_sel=seq // SEL_BLOCK),
        grid=(batch, g, nq),
        in_specs=[
            pl.BlockSpec((tq, NSA_HPG * NSA_QK_PAD), lambda b, gg, i: (b * nq + i, gg)),
            pl.BlockSpec((nblk, NSA_QK_PAD), lambda b, gg, i: (b * g + gg, 0)),
            pl.BlockSpec((NSA_V_DIM, nblk), lambda b, gg, i: (b * g + gg, 0)),
            pl.BlockSpec((tq, LANES), lambda b, gg, i: (b * nq + i, 6)),
            pl.BlockSpec(ovt.shape, lambda b, gg, i: (0, 0)),
        ],
        out_specs=[pl.BlockSpec((LANES, tq), lambda b, gg, i: (b * g + gg, i)),
                   pl.BlockSpec((tq, NSA_HPG * NSA_V_DIM), lambda b, gg, i: (b * nq + i, gg))],
        out_shape=[jax.ShapeDtypeStruct((batch * g * LANES, seq), F32),
                   jax.ShapeDtypeStruct((batch * seq, NSA_WIDTH), F32)],
        scratch_shapes=[pltpu.VMEM((LANES, tq), F32)],
        compiler_params=_cparams(("arbitrary", "arbitrary", "arbitrary")),
        name="nsa_select",
    )(pq, kc, vct, p3, ovt)


def _nsa_attend_kernel(q_ref, ks_ref, vs_ref, kw_ref, vw_ref, gl_ref, sel_ref,
                       ocmp_ref, o_ref, vst_ref, vwt_ref, glt_ref, *, tq, n_var):
    i = pl.program_id(1)
    t0 = i * tq
    hpg = NSA_HPG
    ng = NSA_KV_GROUPS
    dv = NSA_V_DIM
    dk = NSA_QK_PAD

    @pl.when(i == 0)
    def _():
        ones = jnp.ones((vst_ref.shape[1] - dv, vst_ref.shape[2]), BF16)
        for gg in range(ng):
            vst_ref[gg, 0:dv, :] = _transpose_bf16(vs_ref[:, gg * dv:(gg + 1) * dv])
            vst_ref[gg, dv:, :] = ones
            vwt_ref[gg, 0:dv, :] = _transpose_bf16(vw_ref[:, gg * dv:(gg + 1) * dv])
            vwt_ref[gg, dv:, :] = ones

    qb = q_ref[...]
    q4t = [jnp.concatenate(
        [_transpose_bf16(qb[:, (gg * hpg + hh) * dk:(gg * hpg + hh + 1) * dk])
         for hh in range(hpg)], axis=1) for gg in range(ng)]
    glt_ref[...] = jax.nn.sigmoid(gl_ref[...]).T

    def distance(k0, rows):
        return (lax.broadcasted_iota(jnp.int32, (rows, tq), 1)
                - lax.broadcasted_iota(jnp.int32, (rows, tq), 0)) + (t0 - k0)

    def scores(gg, k, mask_bias):
        st = jnp.dot(k, q4t[gg], preferred_element_type=F32)
        return st + jnp.concatenate([mask_bias] * hpg, axis=1)

    def selection_bias(gg, k0):
        j0 = gg * LANES + k0 // SEL_BLOCK
        return jnp.concatenate(
            [jnp.broadcast_to(sel_ref[j0 + j:j0 + j + 1, :], (SEL_BLOCK, tq))
             for j in range(SLC_TILE // SEL_BLOCK)], axis=0)

    def variant(n):
        def run():
            kw0 = pl.multiple_of(jnp.maximum(t0 - WINDOW, 0), LANES)
            wrows = WINDOW + tq
            dist_w = distance(kw0, wrows)
            bias_w = jnp.where(jnp.where(dist_w >= 0, dist_w, WINDOW) < WINDOW, 0.0, NEG)
            o_win = []
            for gg in range(ng):
                sw = scores(gg, kw_ref[pl.ds(kw0, wrows), gg * dk:(gg + 1) * dk], bias_w)
                pw = jnp.exp2(sw - jnp.max(sw, axis=0, keepdims=True))
                o_win.append(_normalised(
                    jnp.dot(vwt_ref[gg, :, pl.ds(kw0, wrows)], pw.astype(BF16),
                            preferred_element_type=F32), dv))

            starts = [t * SLC_TILE for t in range(n, -1, -1)]
            causal = distance(starts[0], SLC_TILE) >= 0
            tiles = [[] for _ in range(ng)]
            for t, k0 in enumerate(starts):
                for gg in range(ng):
                    bias = selection_bias(gg, k0)
                    if t == 0:
                        bias = jnp.where(causal, bias, NEG)
                    tiles[gg].append(scores(
                        gg, ks_ref[k0:k0 + SLC_TILE, gg * dk:(gg + 1) * dk], bias))
            ms = [jnp.max(tiles[gg][0], axis=0, keepdims=True) for gg in range(ng)]
            accs = [jnp.dot(vst_ref[gg, :, starts[0]:starts[0] + SLC_TILE],
                            jnp.exp2(tiles[gg][0] - ms[gg]).astype(BF16),
                            preferred_element_type=F32) for gg in range(ng)]
            for t in range(1, n + 1):
                k0 = starts[t]
                for gg in range(ng):
                    st = tiles[gg][t]
                    m_new = jnp.maximum(ms[gg], jnp.max(st, axis=0, keepdims=True))
                    accs[gg] = jnp.exp2(ms[gg] - m_new) * accs[gg] + jnp.dot(
                        vst_ref[gg, :, k0:k0 + SLC_TILE], jnp.exp2(st - m_new).astype(BF16),
                        preferred_element_type=F32)
                    ms[gg] = m_new

            for gg in range(ng):
                o_slc = _normalised(accs[gg], dv)
                for hh in range(hpg):
                    _, g_slc, g_win = _gate_rows(glt_ref, gg, hh)
                    lanes = slice(hh * tq, (hh + 1) * tq)
                    out = g_slc * o_slc[:, lanes] + g_win * o_win[gg][:, lanes]
                    cols = slice((gg * hpg + hh) * dv, (gg * hpg + hh + 1) * dv)
                    o_ref[:, cols] = (ocmp_ref[:, cols] + out.T).astype(o_ref.dtype)
        return run

    lax.switch(t0 // SLC_TILE, [variant(n) for n in range(n_var)])


def _nsa_attend(pq, pk, pv, p3, sel, ocmp, batch, seq):
    tq = NSA_ATTEND_TQ
    g = NSA_KV_GROUPS
    nq = seq // tq
    vrows = NSA_V_DIM + BF16_SUBLANES
    return pl.pallas_call(
        functools.partial(_nsa_attend_kernel, tq=tq, n_var=seq // SLC_TILE),
        grid=(batch, nq),
        in_specs=[
            pl.BlockSpec((tq, NSA_HEADS * NSA_QK_PAD), lambda b, i: (b * nq + i, 0)),
            pl.BlockSpec((seq, g * NSA_QK_PAD), lambda b, i: (b, 0)),
            pl.BlockSpec((seq, g * NSA_V_DIM), lambda b, i: (b, 0)),
            pl.BlockSpec((seq, g * NSA_QK_PAD), lambda b, i: (b, 1)),
            pl.BlockSpec((seq, g * NSA_V_DIM), lambda b, i: (b, 1)),
            pl.BlockSpec((tq, LANES), lambda b, i: (b * nq + i, 6)),
            pl.BlockSpec((g * LANES, tq), lambda b, i: (b, i)),
            pl.BlockSpec((tq, NSA_WIDTH), lambda b, i: (b * nq + i, 0)),
        ],
        out_specs=pl.BlockSpec((tq, NSA_WIDTH), lambda b, i: (b * nq + i, 0)),
        out_shape=jax.ShapeDtypeStruct((batch * seq, NSA_WIDTH), BF16),
        scratch_shapes=[pltpu.VMEM((g, vrows, seq), BF16), pltpu.VMEM((g, vrows, seq), BF16),
                        pltpu.VMEM((LANES, tq), F32)],
        compiler_params=_cparams(("arbitrary", "arbitrary")),
        name="nsa_attend",
    )(pq, pk, pv, pk, pv, p3, sel, ocmp)


def _pad_head_rows(wt, heads):
    k = wt.shape[1]
    wt = wt.reshape(heads, NSA_QK_DIM, k)
    wt = jnp.pad(wt, ((0, 0), (0, NSA_QK_PAD - NSA_QK_DIM), (0, 0)))
    return wt.reshape(heads * NSA_QK_PAD, k)


def _pad_gain(gain, scale=1.0):
    return jnp.pad(gain * scale, (0, NSA_QK_PAD - NSA_QK_DIM))


def _overlap_matrix(nc, ns):
    i = np.arange(nc)[:, None]
    j = np.arange(ns)[None, :]
    lo = np.maximum(i * CMP_STRIDE, j * SEL_BLOCK)
    hi = np.minimum(i * CMP_STRIDE + CMP_BLOCK, (j + 1) * SEL_BLOCK)
    return (np.maximum(hi - lo, 0) / CMP_STRIDE).astype(np.float32)


def kernel(x, norm_attn, w_in, fox_f_bias, fox_q_gain, fox_k_gain,
           nsa_q_gain, nsa_kc_gain, nsa_ks_gain, nsa_kw_gain,
           cmp_pe_k, cmp_w1_k, cmp_w2_k, cmp_pe_v, cmp_w1_v, cmp_w2_v,
           w_up_fox, w_up_nsa, w_out, norm_ffn, w_ffn_gate, w_ffn_up, w_ffn_down):
    batch, seq, d = x.shape
    m = batch * seq
    depth = w_in.shape[0]
    pts = [0] + [int(p) for p in np.cumsum(IN_SPLITS)]
    nblk = seq // CMP_STRIDE
    n_cmp = nblk - CMP_BLOCK // CMP_STRIDE + 1
    ns = seq // SEL_BLOCK

    slope = jnp.exp2(-8.0 * jnp.arange(1, NSA_HEADS + 1, dtype=F32) / NSA_HEADS) * LOG2E
    s1, s2, s3 = [p.astype(F32) for p in _split3(slope)]
    q_const = jnp.stack([256.0 * s1, 256.0 * s2, 256.0 * s3, s1, s2, s3], axis=1)
    lane0 = NSA_QK_DIM
    aug_q = jnp.zeros((2, NSA_HEADS, NSA_QK_PAD), F32)
    aug_q = aug_q.at[0, :, lane0:lane0 + 6].set(q_const)
    aug_q = aug_q.at[1, :, lane0 + 6:lane0 + 9].set(-slope[:, None])
    aug_q = aug_q.reshape(2, NSA_HEADS * NSA_QK_PAD)
    ovt_np = np.zeros((LANES, nblk), np.float32)
    ovt_np[:ns, :n_cmp] = _overlap_matrix(n_cmp, ns).T
    ovt = jnp.asarray(ovt_np, BF16)

    w_in_t = jnp.swapaxes(w_in, 1, 2)

    xf = x.reshape(m, d)
    for l in range(depth):
        wt = w_in_t[l]
        row = dict(zip(("fq", "fk", "fv", "fl", "nq", "kc", "vc", "ks", "vs", "kw", "vw", "ng",
                        "ga", "gb"), pts))

        gain1 = jnp.concatenate([jnp.tile(fox_q_gain[l] * (FOX_HEAD_DIM ** -0.5 * LOG2E), FOX_HEADS),
                                 jnp.tile(fox_k_gain[l], FOX_HEADS),
                                 jnp.ones((FOX_WIDTH,), F32)])
        flag1 = jnp.concatenate([jnp.ones((2 * FOX_WIDTH,), F32), jnp.zeros((FOX_WIDTH,), F32)])
        gain_q = jnp.tile(_pad_gain(nsa_q_gain[l], NSA_QK_DIM ** -0.5 * LOG2E), NSA_HEADS)
        gain_k = jnp.concatenate([jnp.tile(_pad_gain(nsa_ks_gain[l]), NSA_KV_GROUPS),
                                  jnp.tile(_pad_gain(nsa_kw_gain[l]), NSA_KV_GROUPS)])
        n_small = FOX_HEADS + 3 * NSA_HEADS
        w3 = jnp.concatenate([_pad_head_rows(wt[row["kc"]:row["vc"]], NSA_KV_GROUPS),
                              wt[row["vc"]:row["ks"]], wt[row["fl"]:row["nq"]],
                              wt[row["ng"]:row["ga"]],
                              jnp.zeros((LANES - n_small, d), F32)], axis=0)

        xn = _rmsnorm(xf, norm_attn[l])
        q_tile = NSA_HPG * NSA_QK_DIM
        p1 = _project(xn, wt, list(range(0, 3 * FOX_WIDTH, FOX_PROJ_TN)), FOX_PROJ_TN, gain1,
                      flag1, BF16, PROJ_TM, "proj_fox", group=FOX_HEAD_DIM, count=FOX_HEAD_DIM)
        pq = _project(xn, wt, [row["nq"], row["nq"] + q_tile], q_tile, gain_q,
                      jnp.ones_like(gain_q), BF16, PROJ_TM, "proj_nsa_q", pad_from=NSA_QK_DIM,
                      pad_to=NSA_QK_PAD, group=NSA_QK_PAD, count=NSA_QK_DIM,
                      aug="query", seq=seq, aug_table=aug_q)
        w_k = jnp.concatenate([wt[row["ks"]:row["vs"]], wt[row["kw"]:row["vw"]]], axis=0)
        w_v = jnp.concatenate([wt[row["vs"]:row["kw"]], wt[row["vw"]:row["ng"]]], axis=0)
        pk = _project(xn, w_k, [0], 2 * KV_K, gain_k, jnp.ones_like(gain_k), BF16,
                      PROJ_TM, "proj_nsa_k", pad_from=NSA_QK_DIM, pad_to=NSA_QK_PAD,
                      group=NSA_QK_PAD, count=NSA_QK_DIM, aug="key", seq=seq)
        pv = _project(xn, w_v, [0], 2 * KV_V, None, None, BF16, PROJ_TM, "proj_nsa_v")
        p3 = _project(xn, w3, [0], w3.shape[0], None, None, F32, PROJ_TM, "proj_f32")

        bias_row = jnp.pad(fox_f_bias[l], (0, LANES - FOX_HEADS)).reshape(1, LANES)
        ccol, crow = _decay(p3, bias_row, batch, seq)
        o_a = _fox_attention(p1, ccol, crow.reshape(batch, FOX_HEADS, 1, seq), batch, seq)

        pad_d = NSA_QK_PAD - NSA_QK_DIM
        pek = jnp.pad(cmp_pe_k[l], ((0, 0), (0, pad_d)))
        w1k = jnp.pad(cmp_w1_k[l].reshape(CMP_BLOCK, NSA_QK_DIM, CMP_HIDDEN),
                      ((0, 0), (0, pad_d), (0, 0))).astype(BF16)
        w2k = jnp.pad(cmp_w2_k[l], ((0, 0), (0, pad_d))).astype(BF16)
        gk = _pad_gain(nsa_kc_gain[l]).reshape(1, NSA_QK_PAD)
        w1v = cmp_w1_v[l].reshape(CMP_BLOCK, NSA_V_DIM, CMP_HIDDEN).astype(BF16)
        w2v = cmp_w2_v[l].astype(BF16)
        kc, vct = _compress(p3, pek, w1k, w2k, gk, cmp_pe_v[l], w1v, w2v, batch, seq)
        sel, ocmp = _nsa_select(pq, p3, kc, vct, ovt, batch, seq)
        o_b = _nsa_attend(pq, pk, pv, p3, sel, ocmp, batch, seq)

        merged = _merge(xn, o_a, o_b, wt, row["ga"], row["gb"], w_up_fox[l], w_up_nsa[l])
        hres, hn = _out_proj_norm(merged, w_out[l], xf, norm_ffn[l])

        act = _swiglu(hn, w_ffn_gate[l], w_ffn_up[l])
        xf = _matmul_residual(act, w_ffn_down[l], hres, FFN_DOWN_TM, FFN_DOWN_TN, "ffn_down")
    return xf.reshape(batch, seq, d)
```

```python
import functools

import numpy as np
import jax
import jax.numpy as jnp
from jax import lax
from jax.experimental import pallas as pl
from jax.experimental.pallas import tpu as pltpu

F32 = jnp.float32
BF16 = jnp.bfloat16

D_MODEL = 2048
FOX_HEADS = 8
FOX_HEAD_DIM = 128
FOX_WIDTH = FOX_HEADS * FOX_HEAD_DIM
NSA_HEADS = 8
NSA_KV_GROUPS = 2
NSA_HPG = NSA_HEADS // NSA_KV_GROUPS
NSA_QK_DIM = 192
NSA_QK_PAD = 256
NSA_V_DIM = 128
NSA_WIDTH = NSA_HEADS * NSA_V_DIM
CMP_BLOCK = 32
CMP_STRIDE = 16
CMP_HIDDEN = 256
SEL_BLOCK = 64
SEL_TOPK = 16
SEL_LOCAL = 2
FORCE_SCORE = 1.0e4
WINDOW = 512
KV_K = NSA_KV_GROUPS * NSA_QK_DIM
KV_V = NSA_KV_GROUPS * NSA_V_DIM
D_FF = -(-(8 * D_MODEL) // (3 * 256)) * 256
RMS_EPS = 1e-6
IN_SPLITS = (FOX_WIDTH, FOX_WIDTH, FOX_WIDTH, FOX_HEADS,
             NSA_HEADS * NSA_QK_DIM, KV_K, KV_V, KV_K, KV_V, KV_K, KV_V,
             3 * NSA_HEADS, D_MODEL, D_MODEL)

LANES = 128
SUBLANES = 8
BF16_SUBLANES = 16
NEG = -1.0e30
LOG2E = 1.4426950408889634
SLC_TILE = 512
VMEM_LIMIT = 56 * 1024 * 1024

RMS_TM = 512
PROJ_TM = 1024
FOX_PROJ_TN = 1024
MERGE_TM, MERGE_TN = 512, 512
OUT_PROJ_TM = 512
SWIGLU_TM, SWIGLU_TN = 1024, 512
FFN_DOWN_TM, FFN_DOWN_TN = 512, 512
FOX_TQ = 512
FOX_HEADS_PER_STEP = 4
NSA_SELECT_TQ = 512
NSA_ATTEND_TQ = 256


def _cparams(sem):
    return pltpu.CompilerParams(dimension_semantics=sem, vmem_limit_bytes=VMEM_LIMIT)


def _rms_kernel(x_ref, g_ref, o_ref):
    x = x_ref[...]
    ms = jnp.mean(x * x, axis=-1, keepdims=True)
    o_ref[...] = (x * lax.rsqrt(ms + RMS_EPS) * g_ref[...]).astype(o_ref.dtype)


def _rmsnorm(x, gain, tm=RMS_TM):
    m, d = x.shape
    return pl.pallas_call(
        _rms_kernel,
        grid=(m // tm,),
        in_specs=[pl.BlockSpec((tm, d), lambda i: (i, 0)),
                  pl.BlockSpec((1, d), lambda i: (0, 0))],
        out_specs=pl.BlockSpec((tm, d), lambda i: (i, 0)),
        out_shape=jax.ShapeDtypeStruct((m, d), BF16),
        compiler_params=_cparams(("arbitrary",)),
        name="rmsnorm",
    )(x, gain.reshape(1, d))


STAGE_CHUNK = 512
EPILOGUE_SPLIT = 4


def _stage_weight(w_ref, wb_ref, transposed=False, pad_from=0, pad_to=0):
    if not transposed:
        wb_ref[...] = w_ref[...].astype(BF16)
        return
    n_in, k = w_ref.shape
    for c in range(k // STAGE_CHUNK):
        cols = slice(c * STAGE_CHUNK, (c + 1) * STAGE_CHUNK)
        w = w_ref[:, cols]
        if pad_from != pad_to:
            zero = jnp.zeros((pad_to - pad_from, STAGE_CHUNK), F32)
            w = jnp.concatenate(
                [piece for h in range(n_in // pad_from)
                 for piece in (w[h * pad_from:(h + 1) * pad_from, :], zero)], axis=0)
        wb_ref[cols, :] = w.T.astype(BF16)


def _first_m_step():
    return pl.program_id(1) == 0


AUG_LANE = NSA_QK_DIM - LANES


def _key_aug(pos):
    lane = lax.broadcasted_iota(jnp.int32, pos.shape, 1)
    hi = (pos >> 8).astype(F32)
    lo = (pos & 255).astype(F32)
    return jnp.where(lane < AUG_LANE, 0.0,
                     jnp.where(lane < AUG_LANE + 3, hi,
                               jnp.where(lane < AUG_LANE + 6, lo,
                                         jnp.where(lane < AUG_LANE + 9, 1.0, 0.0))))


def _query_aug(pos, const_row, slope_row):
    u = pos.astype(F32) * slope_row
    hi = u.astype(BF16).astype(F32)
    r1 = u - hi
    mid = r1.astype(BF16).astype(F32)
    lane = lax.broadcasted_iota(jnp.int32, pos.shape, 1)
    return const_row + jnp.where(lane == AUG_LANE + 6, hi,
                                 jnp.where(lane == AUG_LANE + 7, mid, r1 - mid))


def _proj_kernel(*refs, pad_from, pad_to, group, count, aug, seq):
    if aug == "query":
        a_ref, w_ref, gain_ref, flag_ref, aug_ref, o_ref, wb_ref = refs
    else:
        a_ref, w_ref, gain_ref, flag_ref, o_ref, wb_ref = refs

    @pl.when(_first_m_step())
    def _():
        _stage_weight(w_ref, wb_ref, True, pad_from, pad_to)

    if group == 0:
        o_ref[...] = jnp.dot(a_ref[...], wb_ref[...],
                             preferred_element_type=F32).astype(o_ref.dtype)
        return
    tm = a_ref.shape[0]
    rows = tm // EPILOGUE_SPLIT
    ys = [jnp.dot(a_ref[r * rows:(r + 1) * rows, :], wb_ref[...], preferred_element_type=F32)
          for r in range(EPILOGUE_SPLIT)]
    for r, y in enumerate(ys):
        if aug:
            pos = ((pl.program_id(1) * tm) % seq + r * rows
                   + lax.broadcasted_iota(jnp.int32, (rows, LANES), 0))
        for c in range(y.shape[1] // group):
            sl = slice(c * group, (c + 1) * group)
            yc = y[:, sl]
            ss = jnp.sum(yc * yc, axis=-1, keepdims=True)
            rs = lax.rsqrt(ss * (1.0 / count) + RMS_EPS)
            scale = jnp.where(flag_ref[:, sl] > 0.0, rs, 1.0)
            out = yc * scale * gain_ref[:, sl]
            rsl = slice(r * rows, (r + 1) * rows)
            if not aug:
                o_ref[rsl, sl] = out.astype(o_ref.dtype)
                continue
            up = slice((c + 1) * group - LANES, (c + 1) * group)
            extra = (_key_aug(pos) if aug == "key"
                     else _query_aug(pos, aug_ref[0:1, up], aug_ref[1:2, up]))
            o_ref[rsl, c * group:(c + 1) * group - LANES] = out[:, :group - LANES].astype(o_ref.dtype)
            o_ref[rsl, up] = (out[:, group - LANES:] + extra).astype(o_ref.dtype)


def _row_window(starts, rows, k):
    def index(j, i):
        start = starts[-1]
        for t in range(len(starts) - 2, -1, -1):
            start = jnp.where(j == t, starts[t], start)
        return pl.multiple_of(start, SUBLANES), 0
    assert all(s % SUBLANES == 0 for s in starts)
    return pl.BlockSpec((pl.Element(rows), pl.Element(k)), index)


def _project(a, wt, starts, tn_in, gain, flag, out_dtype, tm, name,
             pad_from=0, pad_to=0, group=0, count=1, aug="", seq=0, aug_table=None):
    m, k = a.shape
    n_tiles = len(starts)
    tn_out = tn_in if pad_from == pad_to else tn_in // pad_from * pad_to
    n_out = n_tiles * tn_out
    if gain is None:
        gain = jnp.ones((n_out,), F32)
        flag = jnp.zeros((n_out,), F32)
    assert not aug or seq % tm == 0
    operands = [a, wt, gain.reshape(1, n_out), flag.reshape(1, n_out)]
    in_specs = [pl.BlockSpec((tm, k), lambda j, i: (i, 0)),
                _row_window(starts, tn_in, k),
                pl.BlockSpec((1, tn_out), lambda j, i: (0, j)),
                pl.BlockSpec((1, tn_out), lambda j, i: (0, j))]
    if aug == "query":
        operands.append(aug_table)
        in_specs.append(pl.BlockSpec((2, tn_out), lambda j, i: (0, j)))
    return pl.pallas_call(
        functools.partial(_proj_kernel, pad_from=pad_from, pad_to=pad_to, group=group,
                          count=count, aug=aug, seq=seq),
        grid=(n_tiles, m // tm),
        in_specs=in_specs,
        out_specs=pl.BlockSpec((tm, tn_out), lambda j, i: (i, j)),
        out_shape=jax.ShapeDtypeStruct((m, n_out), out_dtype),
        scratch_shapes=[pltpu.VMEM((k, tn_out), BF16)],
        compiler_params=_cparams(("arbitrary", "arbitrary")),
        name=name,
    )(*operands)


def _out_norm_kernel(a_ref, w_ref, r_ref, g_ref, h_ref, hn_ref, wb_ref):
    @pl.when(pl.program_id(0) == 0)
    def _():
        _stage_weight(w_ref, wb_ref)

    h = r_ref[...] + jnp.dot(a_ref[...], wb_ref[...], preferred_element_type=F32)
    h_ref[...] = h
    ms = jnp.mean(h * h, axis=-1, keepdims=True)
    hn_ref[...] = (h * lax.rsqrt(ms + RMS_EPS) * g_ref[...]).astype(hn_ref.dtype)


def _out_proj_norm(a, w, res, gain, tm=OUT_PROJ_TM):
    m, k = a.shape
    n = w.shape[1]
    full = lambda i: (0, 0)
    row = lambda i: (i, 0)
    return pl.pallas_call(
        _out_norm_kernel,
        grid=(m // tm,),
        in_specs=[pl.BlockSpec((tm, k), row),
                  pl.BlockSpec((k, n), full, pipeline_mode=pl.Buffered(1)),
                  pl.BlockSpec((tm, n), row), pl.BlockSpec((1, n), full)],
        out_specs=[pl.BlockSpec((tm, n), row), pl.BlockSpec((tm, n), row)],
        out_shape=[jax.ShapeDtypeStruct((m, n), F32), jax.ShapeDtypeStruct((m, n), BF16)],
        scratch_shapes=[pltpu.VMEM((k, n), BF16)],
        compiler_params=_cparams(("arbitrary",)),
        name="out_proj_norm",
    )(a, w, res, gain.reshape(1, n))


def _mm_res_kernel(a_ref, w_ref, r_ref, o_ref, wb_ref):
    @pl.when(_first_m_step())
    def _():
        _stage_weight(w_ref, wb_ref)

    o_ref[...] = r_ref[...] + jnp.dot(a_ref[...], wb_ref[...], preferred_element_type=F32)


def _matmul_residual(a, w, res, tm, tn, name):
    m, k = a.shape
    n = w.shape[1]
    return pl.pallas_call(
        _mm_res_kernel,
        grid=(n // tn, m // tm),
        in_specs=[pl.BlockSpec((tm, k), lambda j, i: (i, 0)),
                  pl.BlockSpec((k, tn), lambda j, i: (0, j)),
                  pl.BlockSpec((tm, tn), lambda j, i: (i, j))],
        out_specs=pl.BlockSpec((tm, tn), lambda j, i: (i, j)),
        out_shape=jax.ShapeDtypeStruct((m, n), F32),
        scratch_shapes=[pltpu.VMEM((k, tn), BF16)],
        compiler_params=_cparams(("arbitrary", "arbitrary")),
        name=name,
    )(a, w, res)


def _merge_kernel(xn_ref, oa_ref, ob_ref, wga_ref, wgb_ref, wuf_ref, wun_ref, o_ref,
                  bga_ref, bgb_ref, buf_ref, bun_ref):
    @pl.when(_first_m_step())
    def _():
        _stage_weight(wga_ref, bga_ref, True)
        _stage_weight(wgb_ref, bgb_ref, True)
        _stage_weight(wuf_ref, buf_ref)
        _stage_weight(wun_ref, bun_ref)

    xn = xn_ref[...]
    ga = jax.nn.sigmoid(jnp.dot(xn, bga_ref[...], preferred_element_type=F32))
    ua = jnp.dot(oa_ref[...], buf_ref[...], preferred_element_type=F32)
    acc = ga * ua
    gb = jax.nn.sigmoid(jnp.dot(xn, bgb_ref[...], preferred_element_type=F32))
    ub = jnp.dot(ob_ref[...], bun_ref[...], preferred_element_type=F32)
    o_ref[...] = (acc + gb * ub).astype(o_ref.dtype)


def _merge(xn, oa, ob, wt, row_a, row_b, wuf, wun, tm=MERGE_TM, tn=MERGE_TN):
    m, d = xn.shape
    n = wuf.shape[1]
    ka = oa.shape[1]
    kb = ob.shape[1]
    row = lambda j, i: (i, 0)
    col = lambda j, i: (0, j)
    return pl.pallas_call(
        _merge_kernel,
        grid=(n // tn, m // tm),
        in_specs=[pl.BlockSpec((tm, d), row), pl.BlockSpec((tm, ka), row),
                  pl.BlockSpec((tm, kb), row),
                  _row_window([row_a + t * tn for t in range(n // tn)], tn, d),
                  _row_window([row_b + t * tn for t in range(n // tn)], tn, d),
                  pl.BlockSpec((ka, tn), col), pl.BlockSpec((kb, tn), col)],
        out_specs=pl.BlockSpec((tm, tn), lambda j, i: (i, j)),
        out_shape=jax.ShapeDtypeStruct((m, n), BF16),
        scratch_shapes=[pltpu.VMEM((d, tn), BF16), pltpu.VMEM((d, tn), BF16),
                        pltpu.VMEM((ka, tn), BF16), pltpu.VMEM((kb, tn), BF16)],
        compiler_params=_cparams(("arbitrary", "arbitrary")),
        name="gated_merge",
    )(xn, oa, ob, wt, wt, wuf, wun)


def _swiglu_kernel(a_ref, wg_ref, wu_ref, o_ref, bg_ref, bu_ref):
    @pl.when(_first_m_step())
    def _():
        _stage_weight(wg_ref, bg_ref)
        _stage_weight(wu_ref, bu_ref)

    a = a_ref[...]
    gt = jnp.dot(a, bg_ref[...], preferred_element_type=F32)
    up = jnp.dot(a, bu_ref[...], preferred_element_type=F32)
    o_ref[...] = (gt * jax.nn.sigmoid(gt) * up).astype(o_ref.dtype)


def _swiglu(a, wg, wu, tm=SWIGLU_TM, tn=SWIGLU_TN):
    m, k = a.shape
    n = wg.shape[1]
    return pl.pallas_call(
        _swiglu_kernel,
        grid=(n // tn, m // tm),
        in_specs=[pl.BlockSpec((tm, k), lambda j, i: (i, 0)),
                  pl.BlockSpec((k, tn), lambda j, i: (0, j)),
                  pl.BlockSpec((k, tn), lambda j, i: (0, j))],
        out_specs=pl.BlockSpec((tm, tn), lambda j, i: (i, j)),
        out_shape=jax.ShapeDtypeStruct((m, n), BF16),
        scratch_shapes=[pltpu.VMEM((k, tn), BF16), pltpu.VMEM((k, tn), BF16)],
        compiler_params=_cparams(("arbitrary", "arbitrary")),
        name="swiglu_up",
    )(a, wg, wu)


def _split3(x):
    hi = x.astype(BF16)
    r1 = x - hi.astype(F32)
    mid = r1.astype(BF16)
    lo = (r1 - mid.astype(F32)).astype(BF16)
    return hi, mid, lo


def _decay_kernel(z_ref, b_ref, ccol_ref, crow_ref, *, blk):
    t = z_ref.shape[0]
    r = lax.broadcasted_iota(jnp.int32, (blk, blk), 0)
    c = lax.broadcasted_iota(jnp.int32, (blk, blk), 1)
    tri = jnp.where(r >= c, 1.0, 0.0).astype(BF16)
    carry = jnp.zeros((1, LANES), F32)
    for s in range(t // blk):
        rows = slice(s * blk, (s + 1) * blk)
        z = z_ref[rows, :] + b_ref[...]
        logf = (jnp.minimum(z, 0.0) - jnp.log1p(jnp.exp(-jnp.abs(z)))) * LOG2E
        hi, mid, lo = _split3(logf)
        cb = (jnp.dot(tri, hi, preferred_element_type=F32)
              + jnp.dot(tri, mid, preferred_element_type=F32)
              + jnp.dot(tri, lo, preferred_element_type=F32)) + carry
        carry = cb[blk - 1:blk, :]
        ccol_ref[rows, :] = cb
        crow_ref[0, :, rows] = cb.T[:FOX_HEADS, :]


def _decay(p3, bias_row, batch, seq, blk=256):
    return pl.pallas_call(
        functools.partial(_decay_kernel, blk=blk),
        grid=(batch,),
        in_specs=[pl.BlockSpec((seq, LANES), lambda b: (b, 6)),
                  pl.BlockSpec((1, LANES), lambda b: (0, 0))],
        out_specs=[pl.BlockSpec((seq, LANES), lambda b: (b, 0)),
                   pl.BlockSpec((1, FOX_HEADS, seq), lambda b: (b, 0, 0))],
        out_shape=[jax.ShapeDtypeStruct((batch * seq, LANES), F32),
                   jax.ShapeDtypeStruct((batch, FOX_HEADS, seq), F32)],
        compiler_params=_cparams(("arbitrary",)),
        name="fox_decay_cumsum",
    )(p3, bias_row)


def _transpose_bf16(x):
    return x.astype(F32).T.astype(BF16)


def _normalised(acc, d):
    return acc[:d, :] * (1.0 / jnp.maximum(acc[d:d + 1, :], 1e-30))


def _fox_kernel(q_ref, k_ref, v_ref, ccol_ref, crow_ref, o_ref, vt_ref, ka_ref, *, tq, nq, nh):
    hh = pl.program_id(1)
    i = pl.program_id(2)
    dh = FOX_HEAD_DIM

    def pieces(x, lane, first):
        hi = x.astype(BF16).astype(F32)
        r1 = x - hi
        mid = r1.astype(BF16).astype(F32)
        return jnp.where(lane == first, hi,
                         jnp.where(lane == first + 1, mid,
                                   jnp.where(lane == first + 2, r1 - mid, 0.0)))

    @pl.when(i == 0)
    def _():
        cc = ccol_ref[...]
        lane = lax.broadcasted_iota(jnp.int32, cc.shape, 1)
        for h in range(nh):
            vt_ref[h, 0:dh, :] = _transpose_bf16(v_ref[:, h * dh:(h + 1) * dh])
            vt_ref[h, dh:, :] = jnp.ones((vt_ref.shape[1] - dh, vt_ref.shape[2]), BF16)
            cj = jnp.sum(jnp.where(lane == hh * nh + h, cc, 0.0), axis=-1, keepdims=True)
            extra = pieces(-cj, lane, 0) + jnp.where(lane < 3, 0.0, jnp.where(lane < 6, 1.0, 0.0))
            ka_ref[h, :, 0:dh] = k_ref[:, h * dh:(h + 1) * dh]
            ka_ref[h, :, dh:] = extra.astype(BF16)

    row = lax.broadcasted_iota(jnp.int32, (dh, tq), 0)
    qts = []
    for h in range(nh):
        ci = crow_ref[0, h]
        extra = pieces(ci, row, 3) + jnp.where(row < 3, 1.0, 0.0)
        qts.append(jnp.concatenate(
            [_transpose_bf16(q_ref[:, h * dh:(h + 1) * dh]), extra.astype(BF16)], axis=0))
    rk = lax.broadcasted_iota(jnp.int32, (tq, tq), 0)
    cq = lax.broadcasted_iota(jnp.int32, (tq, tq), 1)

    def scores(h, k0):
        return jnp.dot(ka_ref[h, k0:k0 + tq, :], qts[h], preferred_element_type=F32)

    def variant(n):
        def run():
            starts = [t * tq for t in range(n, -1, -1)]
            tiles = [[jnp.where(rk <= cq, scores(h, starts[0]), NEG)]
                     + [scores(h, k0) for k0 in starts[1:]] for h in range(nh)]
            ms = [jnp.max(tiles[h][0], axis=0, keepdims=True) for h in range(nh)]
            accs = [jnp.dot(vt_ref[h, :, starts[0]:starts[0] + tq],
                            jnp.exp2(tiles[h][0] - ms[h]).astype(BF16),
                            preferred_element_type=F32) for h in range(nh)]
            for t in range(1, n + 1):
                k0 = starts[t]
                for h in range(nh):
                    st = tiles[h][t]
                    m_new = jnp.maximum(ms[h], jnp.max(st, axis=0, keepdims=True))
                    accs[h] = jnp.exp2(ms[h] - m_new) * accs[h] + jnp.dot(
                        vt_ref[h, :, k0:k0 + tq], jnp.exp2(st - m_new).astype(BF16),
                        preferred_element_type=F32)
                    ms[h] = m_new
            for h in range(nh):
                o_ref[:, h * dh:(h + 1) * dh] = _normalised(accs[h], dh).T.astype(o_ref.dtype)
        return run

    lax.switch(i, [variant(n) for n in range(nq)])


def _fox_attention(p1, ccol, crow4, batch, seq, tq=FOX_TQ, nh=FOX_HEADS_PER_STEP):
    nq = seq // tq
    hg = FOX_HEADS // nh
    w = nh * FOX_HEAD_DIM
    return pl.pallas_call(
        functools.partial(_fox_kernel, tq=tq, nq=nq, nh=nh),
        grid=(batch, hg, nq),
        in_specs=[pl.BlockSpec((tq, w), lambda b, hh, i: (b * nq + i, hh)),
                  pl.BlockSpec((seq, w), lambda b, hh, i: (b, hg + hh)),
                  pl.BlockSpec((seq, w), lambda b, hh, i: (b, 2 * hg + hh)),
                  pl.BlockSpec((seq, LANES), lambda b, hh, i: (b, 0)),
                  pl.BlockSpec((1, nh, 1, tq), lambda b, hh, i: (b, hh, 0, i))],
        out_specs=pl.BlockSpec((tq, w), lambda b, hh, i: (b * nq + i, hh)),
        out_shape=jax.ShapeDtypeStruct((batch * seq, FOX_WIDTH), BF16),
        scratch_shapes=[pltpu.VMEM((nh, FOX_HEAD_DIM + BF16_SUBLANES, seq), BF16),
                        pltpu.VMEM((nh, seq, 2 * FOX_HEAD_DIM), BF16)],
        compiler_params=_cparams(("arbitrary", "arbitrary", "arbitrary")),
        name="fox_attention",
    )(p1, p1, p1, ccol, crow4)


def _compress_one(z_refs, pe_ref, w1_ref, w2_ref, nblk):
    half = CMP_BLOCK // 2
    first = jnp.zeros((nblk, CMP_HIDDEN), F32)
    second = jnp.zeros((nblk, CMP_HIDDEN), F32)
    for p in range(half):
        rows = pl.ds(p, nblk, stride=CMP_STRIDE)
        zp = [z_ref[rows, :] for z_ref in z_refs]
        zp = zp[0] if len(zp) == 1 else jnp.concatenate(zp, axis=1)
        first += jnp.dot((zp + pe_ref[p:p + 1, :]).astype(BF16), w1_ref[p],
                         preferred_element_type=F32)
        second += jnp.dot((zp + pe_ref[half + p:half + p + 1, :]).astype(BF16),
                          w1_ref[half + p], preferred_element_type=F32)
    hid = first + pltpu.roll(second, nblk - 1, 0)
    act = (hid * jax.nn.sigmoid(hid)).astype(BF16)
    return jnp.dot(act, w2_ref[...], preferred_element_type=F32)


def _compress_kernel(zk0_ref, zk1_ref, zv_ref, pek_ref, w1k_ref, w2k_ref, gk_ref,
                     pev_ref, w1v_ref, w2v_ref, kc_ref, vc_ref, *, nblk):
    kc = _compress_one((zk0_ref, zk1_ref), pek_ref, w1k_ref, w2k_ref, nblk)
    ms = jnp.sum(kc * kc, axis=-1, keepdims=True) * (1.0 / NSA_QK_DIM)
    kc = kc * lax.rsqrt(ms + RMS_EPS) * gk_ref[...]
    pos = CMP_STRIDE * lax.broadcasted_iota(jnp.int32, (nblk, LANES), 0) + (CMP_BLOCK - 1)
    up = NSA_QK_PAD - LANES
    kc_ref[:, :up] = kc[:, :up].astype(kc_ref.dtype)
    kc_ref[:, up:] = (kc[:, up:] + _key_aug(pos)).astype(kc_ref.dtype)
    vc = _compress_one((zv_ref,), pev_ref, w1v_ref, w2v_ref, nblk)
    vc_ref[...] = vc.T.astype(vc_ref.dtype)


def _compress(p3, pek, w1k, w2k, gk, pev, w1v, w2v, batch, seq):
    g = NSA_KV_GROUPS
    nblk = seq // CMP_STRIDE
    full2 = lambda b, gg: (0, 0)
    full3 = lambda b, gg: (0, 0, 0)
    return pl.pallas_call(
        functools.partial(_compress_kernel, nblk=nblk),
        grid=(batch, g),
        in_specs=[pl.BlockSpec((seq, LANES), lambda b, gg: (b, 2 * gg)),
                  pl.BlockSpec((seq, LANES), lambda b, gg: (b, 2 * gg + 1)),
                  pl.BlockSpec((seq, NSA_V_DIM), lambda b, gg: (b, 4 + gg)),
                  pl.BlockSpec(pek.shape, full2), pl.BlockSpec(w1k.shape, full3),
                  pl.BlockSpec(w2k.shape, full2), pl.BlockSpec(gk.shape, full2),
                  pl.BlockSpec(pev.shape, full2), pl.BlockSpec(w1v.shape, full3),
                  pl.BlockSpec(w2v.shape, full2)],
        out_specs=[pl.BlockSpec((nblk, NSA_QK_PAD), lambda b, gg: (b * g + gg, 0)),
                   pl.BlockSpec((NSA_V_DIM, nblk), lambda b, gg: (b * g + gg, 0))],
        out_shape=[jax.ShapeDtypeStruct((batch * g * nblk, NSA_QK_PAD), BF16),
                   jax.ShapeDtypeStruct((batch * g * NSA_V_DIM, nblk), BF16)],
        compiler_params=_cparams(("arbitrary", "arbitrary")),
        name="nsa_compress",
    )(p3, p3, p3, pek, w1k, w2k, gk, pev, w1v, w2v)


def _q_heads_t(q_ref):
    qb = q_ref[...]
    return jnp.concatenate(
        [_transpose_bf16(qb[:, hh * NSA_QK_PAD:(hh + 1) * NSA_QK_PAD]) for hh in range(NSA_HPG)],
        axis=1)


def _gate_rows(glt_ref, g, hh):
    base = FOX_HEADS + (g * NSA_HPG + hh) * 3
    return [glt_ref[pl.ds(base + br, 1), :] for br in range(3)]


def _nsa_select_kernel(q_ref, kc_ref, vct_ref, gl_ref, ovt_ref,
                       sel_ref, ocmp_ref, glt_ref, *, tq, n_cmp, n_sel):
    g = pl.program_id(1)
    t0 = pl.program_id(2) * tq
    hpg = NSA_HPG
    dv = NSA_V_DIM
    q4t = _q_heads_t(q_ref)
    rk = lax.broadcasted_iota(jnp.int32, (LANES, tq), 0)
    cq = lax.broadcasted_iota(jnp.int32, (LANES, tq), 1)

    s_c = jnp.dot(kc_ref[...], q4t, preferred_element_type=F32)
    dist_c = (t0 + cq) - (CMP_STRIDE * rk + (CMP_BLOCK - 1))
    mask_c = jnp.where(rk < n_cmp, dist_c, -1) >= 0
    probs = []
    p_sum = jnp.zeros((LANES, tq), F32)
    for hh in range(hpg):
        sm = jnp.where(mask_c, s_c[:, hh * tq:(hh + 1) * tq], NEG)
        m = jnp.max(sm, axis=0, keepdims=True)
        e = jnp.where(mask_c, jnp.exp2(sm - m), 0.0)
        p = e * (1.0 / jnp.maximum(jnp.sum(e, axis=0, keepdims=True), 1e-30))
        probs.append(p)
        p_sum = p_sum + p
    o_cmp = jnp.dot(vct_ref[...], jnp.concatenate(probs, axis=1).astype(BF16),
                    preferred_element_type=F32)

    ph = p_sum.astype(BF16)
    plo = (p_sum - ph.astype(F32)).astype(BF16)
    ovt = ovt_ref[...]
    imp = (jnp.dot(ovt, ph, preferred_element_type=F32)
           + jnp.dot(ovt, plo, preferred_element_type=F32))[:n_sel, :]

    rj = lax.broadcasted_iota(jnp.int32, (n_sel, tq), 0)
    tcol = t0 + lax.broadcasted_iota(jnp.int32, (n_sel, tq), 1)
    back = (tcol >> (SEL_BLOCK.bit_length() - 1)) - rj
    elig = back >= 0
    forced = jnp.where(rj == 0, 0, jnp.where(elig, back, SEL_LOCAL)) < SEL_LOCAL
    score = jnp.where(elig, jnp.where(forced, FORCE_SCORE, imp), -1.0)
    rank = jnp.zeros((n_sel, tq), F32)
    for jp in range(n_sel):
        row = score[jp:jp + 1, :]
        later = jnp.where(rj > jp, 1.0, 0.0)
        rank = rank + jnp.where(row > score, 1.0, jnp.where(row == score, later, 0.0))
    sel_ref[0:n_sel, :] = jnp.where(elig, jnp.where(rank < SEL_TOPK, 0.0, NEG), NEG)
    sel_ref[n_sel:, :] = jnp.full((LANES - n_sel, tq), NEG, F32)

    glt_ref[...] = jax.nn.sigmoid(gl_ref[...]).T
    for hh in range(hpg):
        gate = _gate_rows(glt_ref, g, hh)[0]
        ocmp_ref[:, hh * dv:(hh + 1) * dv] = (gate * o_cmp[:, hh * tq:(hh + 1) * tq]).T


def _nsa_select(pq, p3, kc, vct, ovt, batch, seq, tq=NSA_SELECT_TQ):
    g = NSA_KV_GROUPS
    nq = seq // tq
    nblk = seq // CMP_STRIDE
    n_cmp = nblk - CMP_BLOCK // CMP_STRIDE + 1
    return pl.pallas_call(
        functools.partial(_nsa_select_kernel, tq=tq, n_cmp=n_cmp, n_sel=seq // SEL_BLOCK),
        grid=(batch, g, nq),
        in_specs=[
            pl.BlockSpec((tq, NSA_HPG * NSA_QK_PAD), lambda b, gg, i: (b * nq + i, gg)),
            pl.BlockSpec((nblk, NSA_QK_PAD), lambda b, gg, i: (b * g + gg, 0)),
            pl.BlockSpec((NSA_V_DIM, nblk), lambda b, gg, i: (b * g + gg, 0)),
            pl.BlockSpec((tq, LANES), lambda b, gg, i: (b * nq + i, 6)),
            pl.BlockSpec(ovt.shape, lambda b, gg, i: (0, 0)),
        ],
        out_specs=[pl.BlockSpec((LANES, tq), lambda b, gg, i: (b * g + gg, i)),
                   pl.BlockSpec((tq, NSA_HPG * NSA_V_DIM), lambda b, gg, i: (b * nq + i, gg))],
        out_shape=[jax.ShapeDtypeStruct((batch * g * LANES, seq), F32),
                   jax.ShapeDtypeStruct((batch * seq, NSA_WIDTH), F32)],
        scratch_shapes=[pltpu.VMEM((LANES, tq), F32)],
        compiler_params=_cparams(("arbitrary", "arbitrary", "arbitrary")),
        name="nsa_select",
    )(pq, kc, vct, p3, ovt)


def _nsa_attend_kernel(q_ref, ks_ref, vs_ref, kw_ref, vw_ref, gl_ref, sel_ref,
                       ocmp_ref, o_ref, vst_ref, vwt_ref, glt_ref, *, tq, n_var):
    i = pl.program_id(1)
    t0 = i * tq
    hpg = NSA_HPG
    ng = NSA_KV_GROUPS
    dv = NSA_V_DIM
    dk = NSA_QK_PAD

    @pl.when(i == 0)
    def _():
        ones = jnp.ones((vst_ref.shape[1] - dv, vst_ref.shape[2]), BF16)
        for gg in range(ng):
            vst_ref[gg, 0:dv, :] = _transpose_bf16(vs_ref[:, gg * dv:(gg + 1) * dv])
            vst_ref[gg, dv:, :] = ones
            vwt_ref[gg, 0:dv, :] = _transpose_bf16(vw_ref[:, gg * dv:(gg + 1) * dv])
            vwt_ref[gg, dv:, :] = ones

    qb = q_ref[...]
    q4t = [jnp.concatenate(
        [_transpose_bf16(qb[:, (gg * hpg + hh) * dk:(gg * hpg + hh + 1) * dk])
         for hh in range(hpg)], axis=1) for gg in range(ng)]
    glt_ref[...] = jax.nn.sigmoid(gl_ref[...]).T

    def distance(k0, rows):
        return (lax.broadcasted_iota(jnp.int32, (rows, tq), 1)
                - lax.broadcasted_iota(jnp.int32, (rows, tq), 0)) + (t0 - k0)

    def scores(gg, k, mask_bias):
        st = jnp.dot(k, q4t[gg], preferred_element_type=F32)
        return st + jnp.concatenate([mask_bias] * hpg, axis=1)

    def selection_bias(gg, k0, rows):
        j0 = gg * LANES + k0 // SEL_BLOCK
        return jnp.concatenate(
            [jnp.broadcast_to(sel_ref[j0 + j:j0 + j + 1, :], (SEL_BLOCK, tq))
             for j in range(rows // SEL_BLOCK)], axis=0)

    def variant(n):
        def run():
            kw0 = pl.multiple_of(jnp.maximum(t0 - WINDOW, 0), LANES)
            wrows = WINDOW + tq
            dist_w = distance(kw0, wrows)
            bias_w = jnp.where(jnp.where(dist_w >= 0, dist_w, WINDOW) < WINDOW, 0.0, NEG)
            o_win = []
            for gg in range(ng):
                sw = scores(gg, kw_ref[pl.ds(kw0, wrows), gg * dk:(gg + 1) * dk], bias_w)
                pw = jnp.exp2(sw - jnp.max(sw, axis=0, keepdims=True))
                o_win.append(_normalised(
                    jnp.dot(vwt_ref[gg, :, pl.ds(kw0, wrows)], pw.astype(BF16),
                            preferred_element_type=F32), dv))

            n_full = n * tq // SLC_TILE
            first = (n + 1) * tq - n_full * SLC_TILE
            starts = [t * SLC_TILE for t in range(n_full, -1, -1)]
            sizes = [first] + [SLC_TILE] * n_full
            causal = distance(starts[0], first) >= 0
            tiles = [[] for _ in range(ng)]
            for t, (k0, rows) in enumerate(zip(starts, sizes)):
                for gg in range(ng):
                    bias = selection_bias(gg, k0, rows)
                    if t == 0:
                        bias = jnp.where(causal, bias, NEG)
                    tiles[gg].append(scores(
                        gg, ks_ref[k0:k0 + rows, gg * dk:(gg + 1) * dk], bias))
            ms = [jnp.max(tiles[gg][0], axis=0, keepdims=True) for gg in range(ng)]
            accs = [jnp.dot(vst_ref[gg, :, starts[0]:starts[0] + first],
                            jnp.exp2(tiles[gg][0] - ms[gg]).astype(BF16),
                            preferred_element_type=F32) for gg in range(ng)]
            for t in range(1, n_full + 1):
                k0 = starts[t]
                for gg in range(ng):
                    st = tiles[gg][t]
                    m_new = jnp.maximum(ms[gg], jnp.max(st, axis=0, keepdims=True))
                    accs[gg] = jnp.exp2(ms[gg] - m_new) * accs[gg] + jnp.dot(
                        vst_ref[gg, :, k0:k0 + SLC_TILE], jnp.exp2(st - m_new).astype(BF16),
                        preferred_element_type=F32)
                    ms[gg] = m_new

            for gg in range(ng):
                o_slc = _normalised(accs[gg], dv)
                for hh in range(hpg):
                    _, g_slc, g_win = _gate_rows(glt_ref, gg, hh)
                    lanes = slice(hh * tq, (hh + 1) * tq)
                    out = g_slc * o_slc[:, lanes] + g_win * o_win[gg][:, lanes]
                    cols = slice((gg * hpg + hh) * dv, (gg * hpg + hh + 1) * dv)
                    o_ref[:, cols] = (ocmp_ref[:, cols] + out.T).astype(o_ref.dtype)
        return run

    lax.switch(i, [variant(n) for n in range(n_var)])


def _nsa_attend(pq, pk, pv, p3, sel, ocmp, batch, seq):
    tq = NSA_ATTEND_TQ
    g = NSA_KV_GROUPS
    nq = seq // tq
    vrows = NSA_V_DIM + BF16_SUBLANES
    return pl.pallas_call(
        functools.partial(_nsa_attend_kernel, tq=tq, n_var=nq),
        grid=(batch, nq),
        in_specs=[
            pl.BlockSpec((tq, NSA_HEADS * NSA_QK_PAD), lambda b, i: (b * nq + i, 0)),
            pl.BlockSpec((seq, g * NSA_QK_PAD), lambda b, i: (b, 0)),
            pl.BlockSpec((seq, g * NSA_V_DIM), lambda b, i: (b, 0)),
            pl.BlockSpec((seq, g * NSA_QK_PAD), lambda b, i: (b, 1)),
            pl.BlockSpec((seq, g * NSA_V_DIM), lambda b, i: (b, 1)),
            pl.BlockSpec((tq, LANES), lambda b, i: (b * nq + i, 6)),
            pl.BlockSpec((g * LANES, tq), lambda b, i: (b, i)),
            pl.BlockSpec((tq, NSA_WIDTH), lambda b, i: (b * nq + i, 0)),
        ],
        out_specs=pl.BlockSpec((tq, NSA_WIDTH), lambda b, i: (b * nq + i, 0)),
        out_shape=jax.ShapeDtypeStruct((batch * seq, NSA_WIDTH), BF16),
        scratch_shapes=[pltpu.VMEM((g, vrows, seq), BF16), pltpu.VMEM((g, vrows, seq), BF16),
                        pltpu.VMEM((LANES, tq), F32)],
        compiler_params=_cparams(("arbitrary", "arbitrary")),
        name="nsa_attend",
    )(pq, pk, pv, pk, pv, p3, sel, ocmp)


def _pad_head_rows(wt, heads):
    k = wt.shape[1]
    wt = wt.reshape(heads, NSA_QK_DIM, k)
    wt = jnp.pad(wt, ((0, 0), (0, NSA_QK_PAD - NSA_QK_DIM), (0, 0)))
    return wt.reshape(heads * NSA_QK_PAD, k)


def _pad_gain(gain, scale=1.0):
    return jnp.pad(gain * scale, (0, NSA_QK_PAD - NSA_QK_DIM))


def _overlap_matrix(nc, ns):
    i = np.arange(nc)[:, None]
    j = np.arange(ns)[None, :]
    lo = np.maximum(i * CMP_STRIDE, j * SEL_BLOCK)
    hi = np.minimum(i * CMP_STRIDE + CMP_BLOCK, (j + 1) * SEL_BLOCK)
    return (np.maximum(hi - lo, 0) / CMP_STRIDE).astype(np.float32)


def kernel(x, norm_attn, w_in, fox_f_bias, fox_q_gain, fox_k_gain,
           nsa_q_gain, nsa_kc_gain, nsa_ks_gain, nsa_kw_gain,
           cmp_pe_k, cmp_w1_k, cmp_w2_k, cmp_pe_v, cmp_w1_v, cmp_w2_v,
           w_up_fox, w_up_nsa, w_out, norm_ffn, w_ffn_gate, w_ffn_up, w_ffn_down):
    batch, seq, d = x.shape
    m = batch * seq
    depth = w_in.shape[0]
    pts = [0] + [int(p) for p in np.cumsum(IN_SPLITS)]
    nblk = seq // CMP_STRIDE
    n_cmp = nblk - CMP_BLOCK // CMP_STRIDE + 1
    ns = seq // SEL_BLOCK

    slope = jnp.exp2(-8.0 * jnp.arange(1, NSA_HEADS + 1, dtype=F32) / NSA_HEADS) * LOG2E
    s1, s2, s3 = [p.astype(F32) for p in _split3(slope)]
    q_const = jnp.stack([256.0 * s1, 256.0 * s2, 256.0 * s3, s1, s2, s3], axis=1)
    lane0 = NSA_QK_DIM
    aug_q = jnp.zeros((2, NSA_HEADS, NSA_QK_PAD), F32)
    aug_q = aug_q.at[0, :, lane0:lane0 + 6].set(q_const)
    aug_q = aug_q.at[1, :, lane0 + 6:lane0 + 9].set(-slope[:, None])
    aug_q = aug_q.reshape(2, NSA_HEADS * NSA_QK_PAD)
    ovt_np = np.zeros((LANES, nblk), np.float32)
    ovt_np[:ns, :n_cmp] = _overlap_matrix(n_cmp, ns).T
    ovt = jnp.asarray(ovt_np, BF16)

    w_in_t = jnp.swapaxes(w_in, 1, 2)

    xf = x.reshape(m, d)
    for l in range(depth):
        wt = w_in_t[l]
        row = dict(zip(("fq", "fk", "fv", "fl", "nq", "kc", "vc", "ks", "vs", "kw", "vw", "ng",
                        "ga", "gb"), pts))

        gain1 = jnp.concatenate([jnp.tile(fox_q_gain[l] * (FOX_HEAD_DIM ** -0.5 * LOG2E), FOX_HEADS),
                                 jnp.tile(fox_k_gain[l], FOX_HEADS),
                                 jnp.ones((FOX_WIDTH,), F32)])
        flag1 = jnp.concatenate([jnp.ones((2 * FOX_WIDTH,), F32), jnp.zeros((FOX_WIDTH,), F32)])
        gain_q = jnp.tile(_pad_gain(nsa_q_gain[l], NSA_QK_DIM ** -0.5 * LOG2E), NSA_HEADS)
        gain_k = jnp.concatenate([jnp.tile(_pad_gain(nsa_ks_gain[l]), NSA_KV_GROUPS),
                                  jnp.tile(_pad_gain(nsa_kw_gain[l]), NSA_KV_GROUPS)])
        n_small = FOX_HEADS + 3 * NSA_HEADS
        w3 = jnp.concatenate([_pad_head_rows(wt[row["kc"]:row["vc"]], NSA_KV_GROUPS),
                              wt[row["vc"]:row["ks"]], wt[row["fl"]:row["nq"]],
                              wt[row["ng"]:row["ga"]],
                              jnp.zeros((LANES - n_small, d), F32)], axis=0)

        xn = _rmsnorm(xf, norm_attn[l])
        q_tile = NSA_HPG * NSA_QK_DIM
        p1 = _project(xn, wt, list(range(0, 3 * FOX_WIDTH, FOX_PROJ_TN)), FOX_PROJ_TN, gain1,
                      flag1, BF16, PROJ_TM, "proj_fox", group=FOX_HEAD_DIM, count=FOX_HEAD_DIM)
        pq = _project(xn, wt, [row["nq"], row["nq"] + q_tile], q_tile, gain_q,
                      jnp.ones_like(gain_q), BF16, PROJ_TM, "proj_nsa_q", pad_from=NSA_QK_DIM,
                      pad_to=NSA_QK_PAD, group=NSA_QK_PAD, count=NSA_QK_DIM,
                      aug="query", seq=seq, aug_table=aug_q)
        w_k = jnp.concatenate([wt[row["ks"]:row["vs"]], wt[row["kw"]:row["vw"]]], axis=0)
        w_v = jnp.concatenate([wt[row["vs"]:row["kw"]], wt[row["vw"]:row["ng"]]], axis=0)
        pk = _project(xn, w_k, [0], 2 * KV_K, gain_k, jnp.ones_like(gain_k), BF16,
                      PROJ_TM, "proj_nsa_k", pad_from=NSA_QK_DIM, pad_to=NSA_QK_PAD,
                      group=NSA_QK_PAD, count=NSA_QK_DIM, aug="key", seq=seq)
        pv = _project(xn, w_v, [0], 2 * KV_V, None, None, BF16, PROJ_TM, "proj_nsa_v")
        p3 = _project(xn, w3, [0], w3.shape[0], None, None, F32, PROJ_TM, "proj_f32")

        bias_row = jnp.pad(fox_f_bias[l], (0, LANES - FOX_HEADS)).reshape(1, LANES)
        ccol, crow = _decay(p3, bias_row, batch, seq)
        o_a = _fox_attention(p1, ccol, crow.reshape(batch, FOX_HEADS, 1, seq), batch, seq)

        pad_d = NSA_QK_PAD - NSA_QK_DIM
        pek = jnp.pad(cmp_pe_k[l], ((0, 0), (0, pad_d)))
        w1k = jnp.pad(cmp_w1_k[l].reshape(CMP_BLOCK, NSA_QK_DIM, CMP_HIDDEN),
                      ((0, 0), (0, pad_d), (0, 0))).astype(BF16)
        w2k = jnp.pad(cmp_w2_k[l], ((0, 0), (0, pad_d))).astype(BF16)
        gk = _pad_gain(nsa_kc_gain[l]).reshape(1, NSA_QK_PAD)
        w1v = cmp_w1_v[l].reshape(CMP_BLOCK, NSA_V_DIM, CMP_HIDDEN).astype(BF16)
        w2v = cmp_w2_v[l].astype(BF16)
        kc, vct = _compress(p3, pek, w1k, w2k, gk, cmp_pe_v[l], w1v, w2v, batch, seq)
        sel, ocmp = _nsa_select(pq, p3, kc, vct, ovt, batch, seq)
        o_b = _nsa_attend(pq, pk, pv, p3, sel, ocmp, batch, seq)

        merged = _merge(xn, o_a, o_b, wt, row["ga"], row["gb"], w_up_fox[l], w_up_nsa[l])
        hres, hn = _out_proj_norm(merged, w_out[l], xf, norm_ffn[l])

        act = _swiglu(hn, w_ffn_gate[l], w_ffn_up[l])
        xf = _matmul_residual(act, w_ffn_down[l], hres, FFN_DOWN_TM, FFN_DOWN_TN, "ffn_down")
    return xf.reshape(batch, seq, d)
```

```python
import functools

import numpy as np
import jax
import jax.numpy as jnp
from jax import lax
from jax.experimental import pallas as pl
from jax.experimental.pallas import tpu as pltpu

F32 = jnp.float32
BF16 = jnp.bfloat16

D_MODEL = 2048
FOX_HEADS = 8
FOX_HEAD_DIM = 128
FOX_WIDTH = FOX_HEADS * FOX_HEAD_DIM
NSA_HEADS = 8
NSA_KV_GROUPS = 2
NSA_HPG = NSA_HEADS // NSA_KV_GROUPS
NSA_QK_DIM = 192
NSA_QK_PAD = 256
NSA_V_DIM = 128
NSA_WIDTH = NSA_HEADS * NSA_V_DIM
CMP_BLOCK = 32
CMP_STRIDE = 16
CMP_HIDDEN = 256
SEL_BLOCK = 64
SEL_TOPK = 16
SEL_LOCAL = 2
FORCE_SCORE = 1.0e4
WINDOW = 512
KV_K = NSA_KV_GROUPS * NSA_QK_DIM
KV_V = NSA_KV_GROUPS * NSA_V_DIM
D_FF = -(-(8 * D_MODEL) // (3 * 256)) * 256
RMS_EPS = 1e-6
IN_SPLITS = (FOX_WIDTH, FOX_WIDTH, FOX_WIDTH, FOX_HEADS,
             NSA_HEADS * NSA_QK_DIM, KV_K, KV_V, KV_K, KV_V, KV_K, KV_V,
             3 * NSA_HEADS, D_MODEL, D_MODEL)

_NT = (((1,), (1,)), ((), ()))

LANES = 128
SUBLANES = 8
BF16_SUBLANES = 16
NEG = -1.0e30
LOG2E = 1.4426950408889634
SLC_TILE = 512
VMEM_LIMIT = 56 * 1024 * 1024

RMS_TM = 512
PROJ_TM = 1024
FOX_PROJ_TN = 1024
MERGE_TM, MERGE_TN = 512, 512
OUT_PROJ_TM = 512
SWIGLU_TM, SWIGLU_TN = 1024, 512
FFN_DOWN_TM, FFN_DOWN_TN = 512, 512
FOX_TQ = 512
FOX_HEADS_PER_STEP = 4
NSA_SELECT_TQ = 512
NSA_ATTEND_TQ = 256


def _cparams(sem):
    return pltpu.CompilerParams(dimension_semantics=sem, vmem_limit_bytes=VMEM_LIMIT)


def _rms_kernel(x_ref, g_ref, o_ref):
    x = x_ref[...]
    ms = jnp.mean(x * x, axis=-1, keepdims=True)
    o_ref[...] = (x * lax.rsqrt(ms + RMS_EPS) * g_ref[...]).astype(o_ref.dtype)


def _rmsnorm(x, gain, tm=RMS_TM):
    m, d = x.shape
    return pl.pallas_call(
        _rms_kernel,
        grid=(m // tm,),
        in_specs=[pl.BlockSpec((tm, d), lambda i: (i, 0)),
                  pl.BlockSpec((1, d), lambda i: (0, 0))],
        out_specs=pl.BlockSpec((tm, d), lambda i: (i, 0)),
        out_shape=jax.ShapeDtypeStruct((m, d), BF16),
        compiler_params=_cparams(("arbitrary",)),
        name="rmsnorm",
    )(x, gain.reshape(1, d))


STAGE_CHUNK = 512
EPILOGUE_SPLIT = 4


def _stage_weight(w_ref, wb_ref, transposed=False, pad_from=0, pad_to=0):
    if not transposed:
        wb_ref[...] = w_ref[...].astype(BF16)
        return
    n_in, k = w_ref.shape
    for c in range(k // STAGE_CHUNK):
        cols = slice(c * STAGE_CHUNK, (c + 1) * STAGE_CHUNK)
        w = w_ref[:, cols]
        if pad_from != pad_to:
            zero = jnp.zeros((pad_to - pad_from, STAGE_CHUNK), F32)
            w = jnp.concatenate(
                [piece for h in range(n_in // pad_from)
                 for piece in (w[h * pad_from:(h + 1) * pad_from, :], zero)], axis=0)
        wb_ref[cols, :] = w.T.astype(BF16)


def _first_m_step():
    return pl.program_id(1) == 0


AUG_LANE = NSA_QK_DIM - LANES


def _key_aug(pos):
    lane = lax.broadcasted_iota(jnp.int32, pos.shape, 1)
    hi = (pos >> 8).astype(F32)
    lo = (pos & 255).astype(F32)
    return jnp.where(lane < AUG_LANE, 0.0,
                     jnp.where(lane < AUG_LANE + 3, hi,
                               jnp.where(lane < AUG_LANE + 6, lo,
                                         jnp.where(lane < AUG_LANE + 9, 1.0, 0.0))))


def _proj_kernel(a_ref, w_ref, gain_ref, flag_ref, o_ref, wb_ref, *, pad_from, pad_to,
                 group, count, aug, seq):
    @pl.when(_first_m_step())
    def _():
        _stage_weight(w_ref, wb_ref, True, pad_from, pad_to)

    if group == 0:
        o_ref[...] = jnp.dot(a_ref[...], wb_ref[...],
                             preferred_element_type=F32).astype(o_ref.dtype)
        return
    tm = a_ref.shape[0]
    rows = tm // EPILOGUE_SPLIT
    ys = [jnp.dot(a_ref[r * rows:(r + 1) * rows, :], wb_ref[...], preferred_element_type=F32)
          for r in range(EPILOGUE_SPLIT)]
    for r, y in enumerate(ys):
        if aug:
            pos = ((pl.program_id(1) * tm) % seq + r * rows
                   + lax.broadcasted_iota(jnp.int32, (rows, LANES), 0))
        for c in range(y.shape[1] // group):
            sl = slice(c * group, (c + 1) * group)
            yc = y[:, sl]
            ss = jnp.sum(yc * yc, axis=-1, keepdims=True)
            rs = lax.rsqrt(ss * (1.0 / count) + RMS_EPS)
            scale = jnp.where(flag_ref[:, sl] > 0.0, rs, 1.0)
            out = yc * scale * gain_ref[:, sl]
            rsl = slice(r * rows, (r + 1) * rows)
            if not aug:
                o_ref[rsl, sl] = out.astype(o_ref.dtype)
                continue
            up = slice((c + 1) * group - LANES, (c + 1) * group)
            o_ref[rsl, c * group:(c + 1) * group - LANES] = out[:, :group - LANES].astype(o_ref.dtype)
            o_ref[rsl, up] = (out[:, group - LANES:] + _key_aug(pos)).astype(o_ref.dtype)


def _row_window(starts, rows, k):
    def index(j, i):
        start = starts[-1]
        for t in range(len(starts) - 2, -1, -1):
            start = jnp.where(j == t, starts[t], start)
        return pl.multiple_of(start, SUBLANES), 0
    assert all(s % SUBLANES == 0 for s in starts)
    return pl.BlockSpec((pl.Element(rows), pl.Element(k)), index)


def _project(a, wt, starts, tn_in, gain, flag, out_dtype, tm, name,
             pad_from=0, pad_to=0, group=0, count=1, aug="", seq=0):
    m, k = a.shape
    n_tiles = len(starts)
    tn_out = tn_in if pad_from == pad_to else tn_in // pad_from * pad_to
    n_out = n_tiles * tn_out
    if gain is None:
        gain = jnp.ones((n_out,), F32)
        flag = jnp.zeros((n_out,), F32)
    assert not aug or seq % tm == 0
    operands = [a, wt, gain.reshape(1, n_out), flag.reshape(1, n_out)]
    in_specs = [pl.BlockSpec((tm, k), lambda j, i: (i, 0)),
                _row_window(starts, tn_in, k),
                pl.BlockSpec((1, tn_out), lambda j, i: (0, j)),
                pl.BlockSpec((1, tn_out), lambda j, i: (0, j))]
    return pl.pallas_call(
        functools.partial(_proj_kernel, pad_from=pad_from, pad_to=pad_to, group=group,
                          count=count, aug=aug, seq=seq),
        grid=(n_tiles, m // tm),
        in_specs=in_specs,
        out_specs=pl.BlockSpec((tm, tn_out), lambda j, i: (i, j)),
        out_shape=jax.ShapeDtypeStruct((m, n_out), out_dtype),
        scratch_shapes=[pltpu.VMEM((k, tn_out), BF16)],
        compiler_params=_cparams(("arbitrary", "arbitrary")),
        name=name,
    )(*operands)


def _proj_qt_kernel(a_ref, w_ref, tab_ref, o_ref, wb_ref, *, head, head_pad, seq):
    @pl.when(_first_m_step())
    def _():
        wb_ref[...] = w_ref[...].astype(BF16)

    tm = a_ref.shape[0]
    cols = tm // EPILOGUE_SPLIT
    reps = cols // LANES
    spare = head_pad - head
    ys = [lax.dot_general(wb_ref[...], a_ref[r * cols:(r + 1) * cols, :], _NT,
                          preferred_element_type=F32) for r in range(EPILOGUE_SPLIT)]
    row = lax.broadcasted_iota(jnp.int32, (spare, cols), 0)
    for r, y in enumerate(ys):
        lanes = slice(r * cols, (r + 1) * cols)
        pos = ((pl.program_id(1) * tm) % seq + r * cols
               + lax.broadcasted_iota(jnp.int32, (1, cols), 1)).astype(F32)
        for h in range(y.shape[0] // head):
            yh = y[h * head:(h + 1) * head, :]
            rs = lax.rsqrt(jnp.sum(yh * yh, axis=0, keepdims=True) * (1.0 / head) + RMS_EPS)
            slot = h * head_pad
            gain = jnp.concatenate([tab_ref[slot:slot + head, :]] * reps, axis=1)
            o_ref[slot:slot + head, lanes] = (yh * rs * gain).astype(o_ref.dtype)
            tab = jnp.concatenate([tab_ref[slot + head:slot + head_pad, :]] * reps, axis=1)
            u = pos * tab
            hi = u.astype(BF16).astype(F32)
            r1 = u - hi
            mid = r1.astype(BF16).astype(F32)
            extra = jnp.where(row < 6, tab,
                              jnp.where(row == 6, hi,
                                        jnp.where(row == 7, mid,
                                                  jnp.where(row == 8, r1 - mid, 0.0))))
            o_ref[slot + head:slot + head_pad, lanes] = extra.astype(o_ref.dtype)


def _project_qt(a, wt, starts, tn_in, table, tm, seq, name):
    m, k = a.shape
    tn_out = tn_in // NSA_QK_DIM * NSA_QK_PAD
    assert seq % tm == 0
    return pl.pallas_call(
        functools.partial(_proj_qt_kernel, head=NSA_QK_DIM, head_pad=NSA_QK_PAD, seq=seq),
        grid=(len(starts), m // tm),
        in_specs=[pl.BlockSpec((tm, k), lambda j, i: (i, 0)),
                  _row_window(starts, tn_in, k),
                  pl.BlockSpec((tn_out, LANES), lambda j, i: (j, 0))],
        out_specs=pl.BlockSpec((tn_out, tm), lambda j, i: (j, i)),
        out_shape=jax.ShapeDtypeStruct((len(starts) * tn_out, m), BF16),
        scratch_shapes=[pltpu.VMEM((tn_in, k), BF16)],
        compiler_params=_cparams(("arbitrary", "arbitrary")),
        name=name,
    )(a, wt, table)


def _out_norm_kernel(a_ref, w_ref, r_ref, g_ref, h_ref, hn_ref, wb_ref):
    @pl.when(pl.program_id(0) == 0)
    def _():
        _stage_weight(w_ref, wb_ref)

    h = r_ref[...] + jnp.dot(a_ref[...], wb_ref[...], preferred_element_type=F32)
    h_ref[...] = h
    ms = jnp.mean(h * h, axis=-1, keepdims=True)
    hn_ref[...] = (h * lax.rsqrt(ms + RMS_EPS) * g_ref[...]).astype(hn_ref.dtype)


def _out_proj_norm(a, w, res, gain, tm=OUT_PROJ_TM):
    m, k = a.shape
    n = w.shape[1]
    full = lambda i: (0, 0)
    row = lambda i: (i, 0)
    return pl.pallas_call(
        _out_norm_kernel,
        grid=(m // tm,),
        in_specs=[pl.BlockSpec((tm, k), row),
                  pl.BlockSpec((k, n), full, pipeline_mode=pl.Buffered(1)),
                  pl.BlockSpec((tm, n), row), pl.BlockSpec((1, n), full)],
        out_specs=[pl.BlockSpec((tm, n), row), pl.BlockSpec((tm, n), row)],
        out_shape=[jax.ShapeDtypeStruct((m, n), F32), jax.ShapeDtypeStruct((m, n), BF16)],
        scratch_shapes=[pltpu.VMEM((k, n), BF16)],
        compiler_params=_cparams(("arbitrary",)),
        name="out_proj_norm",
    )(a, w, res, gain.reshape(1, n))


def _mm_res_kernel(a_ref, w_ref, r_ref, o_ref, wb_ref):
    @pl.when(_first_m_step())
    def _():
        _stage_weight(w_ref, wb_ref)

    o_ref[...] = r_ref[...] + jnp.dot(a_ref[...], wb_ref[...], preferred_element_type=F32)


def _matmul_residual(a, w, res, tm, tn, name):
    m, k = a.shape
    n = w.shape[1]
    return pl.pallas_call(
        _mm_res_kernel,
        grid=(n // tn, m // tm),
        in_specs=[pl.BlockSpec((tm, k), lambda j, i: (i, 0)),
                  pl.BlockSpec((k, tn), lambda j, i: (0, j)),
                  pl.BlockSpec((tm, tn), lambda j, i: (i, j))],
        out_specs=pl.BlockSpec((tm, tn), lambda j, i: (i, j)),
        out_shape=jax.ShapeDtypeStruct((m, n), F32),
        scratch_shapes=[pltpu.VMEM((k, tn), BF16)],
        compiler_params=_cparams(("arbitrary", "arbitrary")),
        name=name,
    )(a, w, res)


def _merge_kernel(xn_ref, oa_ref, ob_ref, wga_ref, wgb_ref, wuf_ref, wun_ref, o_ref,
                  bga_ref, bgb_ref, buf_ref, bun_ref):
    @pl.when(_first_m_step())
    def _():
        _stage_weight(wga_ref, bga_ref, True)
        _stage_weight(wgb_ref, bgb_ref, True)
        _stage_weight(wuf_ref, buf_ref)
        _stage_weight(wun_ref, bun_ref)

    xn = xn_ref[...]
    ga = jax.nn.sigmoid(jnp.dot(xn, bga_ref[...], preferred_element_type=F32))
    ua = jnp.dot(oa_ref[...], buf_ref[...], preferred_element_type=F32)
    acc = ga * ua
    gb = jax.nn.sigmoid(jnp.dot(xn, bgb_ref[...], preferred_element_type=F32))
    ub = jnp.dot(ob_ref[...], bun_ref[...], preferred_element_type=F32)
    o_ref[...] = (acc + gb * ub).astype(o_ref.dtype)


def _merge(xn, oa, ob, wt, row_a, row_b, wuf, wun, tm=MERGE_TM, tn=MERGE_TN):
    m, d = xn.shape
    n = wuf.shape[1]
    ka = oa.shape[1]
    kb = ob.shape[1]
    row = lambda j, i: (i, 0)
    col = lambda j, i: (0, j)
    return pl.pallas_call(
        _merge_kernel,
        grid=(n // tn, m // tm),
        in_specs=[pl.BlockSpec((tm, d), row), pl.BlockSpec((tm, ka), row),
                  pl.BlockSpec((tm, kb), row),
                  _row_window([row_a + t * tn for t in range(n // tn)], tn, d),
                  _row_window([row_b + t * tn for t in range(n // tn)], tn, d),
                  pl.BlockSpec((ka, tn), col), pl.BlockSpec((kb, tn), col)],
        out_specs=pl.BlockSpec((tm, tn), lambda j, i: (i, j)),
        out_shape=jax.ShapeDtypeStruct((m, n), BF16),
        scratch_shapes=[pltpu.VMEM((d, tn), BF16), pltpu.VMEM((d, tn), BF16),
                        pltpu.VMEM((ka, tn), BF16), pltpu.VMEM((kb, tn), BF16)],
        compiler_params=_cparams(("arbitrary", "arbitrary")),
        name="gated_merge",
    )(xn, oa, ob, wt, wt, wuf, wun)


def _swiglu_kernel(a_ref, wg_ref, wu_ref, o_ref, bg_ref, bu_ref):
    @pl.when(_first_m_step())
    def _():
        _stage_weight(wg_ref, bg_ref)
        _stage_weight(wu_ref, bu_ref)

    a = a_ref[...]
    gt = jnp.dot(a, bg_ref[...], preferred_element_type=F32)
    up = jnp.dot(a, bu_ref[...], preferred_element_type=F32)
    o_ref[...] = (gt * jax.nn.sigmoid(gt) * up).astype(o_ref.dtype)


def _swiglu(a, wg, wu, tm=SWIGLU_TM, tn=SWIGLU_TN):
    m, k = a.shape
    n = wg.shape[1]
    return pl.pallas_call(
        _swiglu_kernel,
        grid=(n // tn, m // tm),
        in_specs=[pl.BlockSpec((tm, k), lambda j, i: (i, 0)),
                  pl.BlockSpec((k, tn), lambda j, i: (0, j)),
                  pl.BlockSpec((k, tn), lambda j, i: (0, j))],
        out_specs=pl.BlockSpec((tm, tn), lambda j, i: (i, j)),
        out_shape=jax.ShapeDtypeStruct((m, n), BF16),
        scratch_shapes=[pltpu.VMEM((k, tn), BF16), pltpu.VMEM((k, tn), BF16)],
        compiler_params=_cparams(("arbitrary", "arbitrary")),
        name="swiglu_up",
    )(a, wg, wu)


def _split3(x):
    hi = x.astype(BF16)
    r1 = x - hi.astype(F32)
    mid = r1.astype(BF16)
    lo = (r1 - mid.astype(F32)).astype(BF16)
    return hi, mid, lo


def _decay_kernel(z_ref, b_ref, ccol_ref, crow_ref, *, blk):
    t = z_ref.shape[0]
    r = lax.broadcasted_iota(jnp.int32, (blk, blk), 0)
    c = lax.broadcasted_iota(jnp.int32, (blk, blk), 1)
    tri = jnp.where(r >= c, 1.0, 0.0).astype(BF16)
    carry = jnp.zeros((1, LANES), F32)
    for s in range(t // blk):
        rows = slice(s * blk, (s + 1) * blk)
        z = z_ref[rows, :] + b_ref[...]
        logf = (jnp.minimum(z, 0.0) - jnp.log1p(jnp.exp(-jnp.abs(z)))) * LOG2E
        hi, mid, lo = _split3(logf)
        cb = (jnp.dot(tri, hi, preferred_element_type=F32)
              + jnp.dot(tri, mid, preferred_element_type=F32)
              + jnp.dot(tri, lo, preferred_element_type=F32)) + carry
        carry = cb[blk - 1:blk, :]
        ccol_ref[rows, :] = cb
        crow_ref[0, :, rows] = cb.T[:FOX_HEADS, :]


def _decay(p3, bias_row, batch, seq, blk=256):
    return pl.pallas_call(
        functools.partial(_decay_kernel, blk=blk),
        grid=(batch,),
        in_specs=[pl.BlockSpec((seq, LANES), lambda b: (b, 6)),
                  pl.BlockSpec((1, LANES), lambda b: (0, 0))],
        out_specs=[pl.BlockSpec((seq, LANES), lambda b: (b, 0)),
                   pl.BlockSpec((1, FOX_HEADS, seq), lambda b: (b, 0, 0))],
        out_shape=[jax.ShapeDtypeStruct((batch * seq, LANES), F32),
                   jax.ShapeDtypeStruct((batch, FOX_HEADS, seq), F32)],
        compiler_params=_cparams(("arbitrary",)),
        name="fox_decay_cumsum",
    )(p3, bias_row)


def _transpose_bf16(x):
    return x.astype(F32).T.astype(BF16)


def _normalised(acc, d):
    return acc[:d, :] * (1.0 / jnp.maximum(acc[d:d + 1, :], 1e-30))


def _fox_kernel(q_ref, k_ref, v_ref, ccol_ref, crow_ref, o_ref, vt_ref, ka_ref, *, tq, nq, nh):
    hh = pl.program_id(1)
    i = pl.program_id(2)
    dh = FOX_HEAD_DIM

    def pieces(x, lane, first):
        hi = x.astype(BF16).astype(F32)
        r1 = x - hi
        mid = r1.astype(BF16).astype(F32)
        return jnp.where(lane == first, hi,
                         jnp.where(lane == first + 1, mid,
                                   jnp.where(lane == first + 2, r1 - mid, 0.0)))

    @pl.when(i == 0)
    def _():
        cc = ccol_ref[...]
        lane = lax.broadcasted_iota(jnp.int32, cc.shape, 1)
        for h in range(nh):
            vt_ref[h, 0:dh, :] = _transpose_bf16(v_ref[:, h * dh:(h + 1) * dh])
            vt_ref[h, dh:, :] = jnp.ones((vt_ref.shape[1] - dh, vt_ref.shape[2]), BF16)
            cj = jnp.sum(jnp.where(lane == hh * nh + h, cc, 0.0), axis=-1, keepdims=True)
            extra = pieces(-cj, lane, 0) + jnp.where(lane < 3, 0.0, jnp.where(lane < 6, 1.0, 0.0))
            ka_ref[h, :, 0:dh] = k_ref[:, h * dh:(h + 1) * dh]
            ka_ref[h, :, dh:] = extra.astype(BF16)

    row = lax.broadcasted_iota(jnp.int32, (dh, tq), 0)
    qts = []
    for h in range(nh):
        ci = crow_ref[0, h]
        extra = pieces(ci, row, 3) + jnp.where(row < 3, 1.0, 0.0)
        qts.append(jnp.concatenate(
            [_transpose_bf16(q_ref[:, h * dh:(h + 1) * dh]), extra.astype(BF16)], axis=0))
    rk = lax.broadcasted_iota(jnp.int32, (tq, tq), 0)
    cq = lax.broadcasted_iota(jnp.int32, (tq, tq), 1)

    def scores(h, k0):
        return jnp.dot(ka_ref[h, k0:k0 + tq, :], qts[h], preferred_element_type=F32)

    def variant(n):
        def run():
            starts = [t * tq for t in range(n, -1, -1)]
            tiles = [[jnp.where(rk <= cq, scores(h, starts[0]), NEG)]
                     + [scores(h, k0) for k0 in starts[1:]] for h in range(nh)]
            ms = [jnp.max(tiles[h][0], axis=0, keepdims=True) for h in range(nh)]
            accs = [jnp.dot(vt_ref[h, :, starts[0]:starts[0] + tq],
                            jnp.exp2(tiles[h][0] - ms[h]).astype(BF16),
                            preferred_element_type=F32) for h in range(nh)]
            for t in range(1, n + 1):
                k0 = starts[t]
                for h in range(nh):
                    st = tiles[h][t]
                    m_new = jnp.maximum(ms[h], jnp.max(st, axis=0, keepdims=True))
                    accs[h] = jnp.exp2(ms[h] - m_new) * accs[h] + jnp.dot(
                        vt_ref[h, :, k0:k0 + tq], jnp.exp2(st - m_new).astype(BF16),
                        preferred_element_type=F32)
                    ms[h] = m_new
            for h in range(nh):
                o_ref[:, h * dh:(h + 1) * dh] = _normalised(accs[h], dh).T.astype(o_ref.dtype)
        return run

    lax.switch(i, [variant(n) for n in range(nq)])


def _fox_attention(p1, ccol, crow4, batch, seq, tq=FOX_TQ, nh=FOX_HEADS_PER_STEP):
    nq = seq // tq
    hg = FOX_HEADS // nh
    w = nh * FOX_HEAD_DIM
    return pl.pallas_call(
        functools.partial(_fox_kernel, tq=tq, nq=nq, nh=nh),
        grid=(batch, hg, nq),
        in_specs=[pl.BlockSpec((tq, w), lambda b, hh, i: (b * nq + i, hh)),
                  pl.BlockSpec((seq, w), lambda b, hh, i: (b, hg + hh)),
                  pl.BlockSpec((seq, w), lambda b, hh, i: (b, 2 * hg + hh)),
                  pl.BlockSpec((seq, LANES), lambda b, hh, i: (b, 0)),
                  pl.BlockSpec((1, nh, 1, tq), lambda b, hh, i: (b, hh, 0, i))],
        out_specs=pl.BlockSpec((tq, w), lambda b, hh, i: (b * nq + i, hh)),
        out_shape=jax.ShapeDtypeStruct((batch * seq, FOX_WIDTH), BF16),
        scratch_shapes=[pltpu.VMEM((nh, FOX_HEAD_DIM + BF16_SUBLANES, seq), BF16),
                        pltpu.VMEM((nh, seq, 2 * FOX_HEAD_DIM), BF16)],
        compiler_params=_cparams(("arbitrary", "arbitrary", "arbitrary")),
        name="fox_attention",
    )(p1, p1, p1, ccol, crow4)


def _compress_one(z_refs, pe_ref, w1_ref, w2_ref, nblk):
    half = CMP_BLOCK // 2
    first = jnp.zeros((nblk, CMP_HIDDEN), F32)
    second = jnp.zeros((nblk, CMP_HIDDEN), F32)
    for p in range(half):
        rows = pl.ds(p, nblk, stride=CMP_STRIDE)
        zp = [z_ref[rows, :] for z_ref in z_refs]
        zp = zp[0] if len(zp) == 1 else jnp.concatenate(zp, axis=1)
        first += jnp.dot((zp + pe_ref[p:p + 1, :]).astype(BF16), w1_ref[p],
                         preferred_element_type=F32)
        second += jnp.dot((zp + pe_ref[half + p:half + p + 1, :]).astype(BF16),
                          w1_ref[half + p], preferred_element_type=F32)
    hid = first + pltpu.roll(second, nblk - 1, 0)
    act = (hid * jax.nn.sigmoid(hid)).astype(BF16)
    return jnp.dot(act, w2_ref[...], preferred_element_type=F32)


def _compress_kernel(zk0_ref, zk1_ref, zv_ref, pek_ref, w1k_ref, w2k_ref, gk_ref,
                     pev_ref, w1v_ref, w2v_ref, kc_ref, vc_ref, *, nblk):
    kc = _compress_one((zk0_ref, zk1_ref), pek_ref, w1k_ref, w2k_ref, nblk)
    ms = jnp.sum(kc * kc, axis=-1, keepdims=True) * (1.0 / NSA_QK_DIM)
    kc = kc * lax.rsqrt(ms + RMS_EPS) * gk_ref[...]
    pos = CMP_STRIDE * lax.broadcasted_iota(jnp.int32, (nblk, LANES), 0) + (CMP_BLOCK - 1)
    up = NSA_QK_PAD - LANES
    kc_ref[:, :up] = kc[:, :up].astype(kc_ref.dtype)
    kc_ref[:, up:] = (kc[:, up:] + _key_aug(pos)).astype(kc_ref.dtype)
    vc = _compress_one((zv_ref,), pev_ref, w1v_ref, w2v_ref, nblk)
    vc_ref[...] = vc.T.astype(vc_ref.dtype)


def _compress(p3, pek, w1k, w2k, gk, pev, w1v, w2v, batch, seq):
    g = NSA_KV_GROUPS
    nblk = seq // CMP_STRIDE
    full2 = lambda b, gg: (0, 0)
    full3 = lambda b, gg: (0, 0, 0)
    return pl.pallas_call(
        functools.partial(_compress_kernel, nblk=nblk),
        grid=(batch, g),
        in_specs=[pl.BlockSpec((seq, LANES), lambda b, gg: (b, 2 * gg)),
                  pl.BlockSpec((seq, LANES), lambda b, gg: (b, 2 * gg + 1)),
                  pl.BlockSpec((seq, NSA_V_DIM), lambda b, gg: (b, 4 + gg)),
                  pl.BlockSpec(pek.shape, full2), pl.BlockSpec(w1k.shape, full3),
                  pl.BlockSpec(w2k.shape, full2), pl.BlockSpec(gk.shape, full2),
                  pl.BlockSpec(pev.shape, full2), pl.BlockSpec(w1v.shape, full3),
                  pl.BlockSpec(w2v.shape, full2)],
        out_specs=[pl.BlockSpec((nblk, NSA_QK_PAD), lambda b, gg: (b * g + gg, 0)),
                   pl.BlockSpec((NSA_V_DIM, nblk), lambda b, gg: (b * g + gg, 0))],
        out_shape=[jax.ShapeDtypeStruct((batch * g * nblk, NSA_QK_PAD), BF16),
                   jax.ShapeDtypeStruct((batch * g * NSA_V_DIM, nblk), BF16)],
        compiler_params=_cparams(("arbitrary", "arbitrary")),
        name="nsa_compress",
    )(p3, p3, p3, pek, w1k, w2k, gk, pev, w1v, w2v)


def _q_heads_t(qt_ref, first_head):
    return jnp.concatenate(
        [qt_ref[(first_head + hh) * NSA_QK_PAD:(first_head + hh + 1) * NSA_QK_PAD, :]
         for hh in range(NSA_HPG)], axis=1)


def _gate_rows(glt_ref, g, hh):
    base = FOX_HEADS + (g * NSA_HPG + hh) * 3
    return [glt_ref[pl.ds(base + br, 1), :] for br in range(3)]


def _nsa_select_kernel(q_ref, kc_ref, vct_ref, gl_ref, ovt_ref,
                       sel_ref, ocmp_ref, glt_ref, *, tq, n_cmp, n_sel):
    g = pl.program_id(1)
    t0 = pl.program_id(2) * tq
    hpg = NSA_HPG
    dv = NSA_V_DIM
    q4t = _q_heads_t(q_ref, 0)
    rk = lax.broadcasted_iota(jnp.int32, (LANES, tq), 0)
    cq = lax.broadcasted_iota(jnp.int32, (LANES, tq), 1)

    s_c = jnp.dot(kc_ref[...], q4t, preferred_element_type=F32)
    dist_c = (t0 + cq) - (CMP_STRIDE * rk + (CMP_BLOCK - 1))
    mask_c = jnp.where(rk < n_cmp, dist_c, -1) >= 0
    probs = []
    p_sum = jnp.zeros((LANES, tq), F32)
    for hh in range(hpg):
        sm = jnp.where(mask_c, s_c[:, hh * tq:(hh + 1) * tq], NEG)
        m = jnp.max(sm, axis=0, keepdims=True)
        e = jnp.where(mask_c, jnp.exp2(sm - m), 0.0)
        p = e * (1.0 / jnp.maximum(jnp.sum(e, axis=0, keepdims=True), 1e-30))
        probs.append(p)
        p_sum = p_sum + p
    o_cmp = jnp.dot(vct_ref[...], jnp.concatenate(probs, axis=1).astype(BF16),
                    preferred_element_type=F32)

    ph = p_sum.astype(BF16)
    plo = (p_sum - ph.astype(F32)).astype(BF16)
    ovt = ovt_ref[...]
    imp = (jnp.dot(ovt, ph, preferred_element_type=F32)
           + jnp.dot(ovt, plo, preferred_element_type=F32))[:n_sel, :]

    rj = lax.broadcasted_iota(jnp.int32, (n_sel, tq), 0)
    tcol = t0 + lax.broadcasted_iota(jnp.int32, (n_sel, tq), 1)
    back = (tcol >> (SEL_BLOCK.bit_length() - 1)) - rj
    elig = back >= 0
    forced = jnp.where(rj == 0, 0, jnp.where(elig, back, SEL_LOCAL)) < SEL_LOCAL
    score = jnp.where(elig, jnp.where(forced, FORCE_SCORE, imp), -1.0)
    rank = jnp.zeros((n_sel, tq), F32)
    for jp in range(n_sel):
        row = score[jp:jp + 1, :]
        later = jnp.where(rj > jp, 1.0, 0.0)
        rank = rank + jnp.where(row > score, 1.0, jnp.where(row == score, later, 0.0))
    sel_ref[0:n_sel, :] = jnp.where(elig, jnp.where(rank < SEL_TOPK, 0.0, NEG), NEG)
    sel_ref[n_sel:, :] = jnp.full((LANES - n_sel, tq), NEG, F32)

    glt_ref[...] = jax.nn.sigmoid(gl_ref[...]).T
    for hh in range(hpg):
        gate = _gate_rows(glt_ref, g, hh)[0]
        ocmp_ref[:, hh * dv:(hh + 1) * dv] = (gate * o_cmp[:, hh * tq:(hh + 1) * tq]).T


def _nsa_select(pq, p3, kc, vct, ovt, batch, seq, tq=NSA_SELECT_TQ):
    g = NSA_KV_GROUPS
    nq = seq // tq
    nblk = seq // CMP_STRIDE
    n_cmp = nblk - CMP_BLOCK // CMP_STRIDE + 1
    return pl.pallas_call(
        functools.partial(_nsa_select_kernel, tq=tq, n_cmp=n_cmp, n_sel=seq // SEL_BLOCK),
        grid=(batch, g, nq),
        in_specs=[
            pl.BlockSpec((NSA_HPG * NSA_QK_PAD, tq), lambda b, gg, i: (gg, b * nq + i)),
            pl.BlockSpec((nblk, NSA_QK_PAD), lambda b, gg, i: (b * g + gg, 0)),
            pl.BlockSpec((NSA_V_DIM, nblk), lambda b, gg, i: (b * g + gg, 0)),
            pl.BlockSpec((tq, LANES), lambda b, gg, i: (b * nq + i, 6)),
            pl.BlockSpec(ovt.shape, lambda b, gg, i: (0, 0)),
        ],
        out_specs=[pl.BlockSpec((LANES, tq), lambda b, gg, i: (b * g + gg, i)),
                   pl.BlockSpec((tq, NSA_HPG * NSA_V_DIM), lambda b, gg, i: (b * nq + i, gg))],
        out_shape=[jax.ShapeDtypeStruct((batch * g * LANES, seq), F32),
                   jax.ShapeDtypeStruct((batch * seq, NSA_WIDTH), F32)],
        scratch_shapes=[pltpu.VMEM((LANES, tq), F32)],
        compiler_params=_cparams(("arbitrary", "arbitrary", "arbitrary")),
        name="nsa_select",
    )(pq, kc, vct, p3, ovt)


def _nsa_attend_kernel(q_ref, ks_ref, vs_ref, kw_ref, vw_ref, gl_ref, sel_ref,
                       ocmp_ref, o_ref, vst_ref, vwt_ref, glt_ref, *, tq, n_var):
    i = pl.program_id(1)
    t0 = i * tq
    hpg = NSA_HPG
    ng = NSA_KV_GROUPS
    dv = NSA_V_DIM
    dk = NSA_QK_PAD

    @pl.when(i == 0)
    def _():
        ones = jnp.ones((vst_ref.shape[1] - dv, vst_ref.shape[2]), BF16)
        for gg in range(ng):
            vst_ref[gg, 0:dv, :] = _transpose_bf16(vs_ref[:, gg * dv:(gg + 1) * dv])
            vst_ref[gg, dv:, :] = ones
            vwt_ref[gg, 0:dv, :] = _transpose_bf16(vw_ref[:, gg * dv:(gg + 1) * dv])
            vwt_ref[gg, dv:, :] = ones

    q4t = [_q_heads_t(q_ref, gg * hpg) for gg in range(ng)]
    glt_ref[...] = jax.nn.sigmoid(gl_ref[...]).T

    def distance(k0, rows):
        return (lax.broadcasted_iota(jnp.int32, (rows, tq), 1)
                - lax.broadcasted_iota(jnp.int32, (rows, tq), 0)) + (t0 - k0)

    def scores(gg, k, mask_bias):
        st = jnp.dot(k, q4t[gg], preferred_element_type=F32)
        return st + jnp.concatenate([mask_bias] * hpg, axis=1)

    def selection_bias(gg, k0):
        j0 = gg * LANES + k0 // SEL_BLOCK
        return jnp.concatenate(
            [jnp.broadcast_to(sel_ref[j0 + j:j0 + j + 1, :], (SEL_BLOCK, tq))
             for j in range(SLC_TILE // SEL_BLOCK)], axis=0)

    def variant(n):
        def run():
            kw0 = pl.multiple_of(jnp.maximum(t0 - WINDOW, 0), LANES)
            wrows = WINDOW + tq
            dist_w = distance(kw0, wrows)
            bias_w = jnp.where(jnp.where(dist_w >= 0, dist_w, WINDOW) < WINDOW, 0.0, NEG)
            o_win = []
            for gg in range(ng):
                sw = scores(gg, kw_ref[pl.ds(kw0, wrows), gg * dk:(gg + 1) * dk], bias_w)
                pw = jnp.exp2(sw - jnp.max(sw, axis=0, keepdims=True))
                o_win.append(_normalised(
                    jnp.dot(vwt_ref[gg, :, pl.ds(kw0, wrows)], pw.astype(BF16),
                            preferred_element_type=F32), dv))

            starts = [t * SLC_TILE for t in range(n, -1, -1)]
            causal = distance(starts[0], SLC_TILE) >= 0
            tiles = [[] for _ in range(ng)]
            for t, k0 in enumerate(starts):
                for gg in range(ng):
                    bias = selection_bias(gg, k0)
                    if t == 0:
                        bias = jnp.where(causal, bias, NEG)
                    tiles[gg].append(scores(
                        gg, ks_ref[k0:k0 + SLC_TILE, gg * dk:(gg + 1) * dk], bias))
            ms = [jnp.max(tiles[gg][0], axis=0, keepdims=True) for gg in range(ng)]
            accs = [jnp.dot(vst_ref[gg, :, starts[0]:starts[0] + SLC_TILE],
                            jnp.exp2(tiles[gg][0] - ms[gg]).astype(BF16),
                            preferred_element_type=F32) for gg in range(ng)]
            for t in range(1, n + 1):
                k0 = starts[t]
                for gg in range(ng):
                    st = tiles[gg][t]
                    m_new = jnp.maximum(ms[gg], jnp.max(st, axis=0, keepdims=True))
                    accs[gg] = jnp.exp2(ms[gg] - m_new) * accs[gg] + jnp.dot(
                        vst_ref[gg, :, k0:k0 + SLC_TILE], jnp.exp2(st - m_new).astype(BF16),
                        preferred_element_type=F32)
                    ms[gg] = m_new

            for gg in range(ng):
                o_slc = _normalised(accs[gg], dv)
                for hh in range(hpg):
                    _, g_slc, g_win = _gate_rows(glt_ref, gg, hh)
                    lanes = slice(hh * tq, (hh + 1) * tq)
                    out = g_slc * o_slc[:, lanes] + g_win * o_win[gg][:, lanes]
                    cols = slice((gg * hpg + hh) * dv, (gg * hpg + hh + 1) * dv)
                    o_ref[:, cols] = (ocmp_ref[:, cols] + out.T).astype(o_ref.dtype)
        return run

    lax.switch(t0 // SLC_TILE, [variant(n) for n in range(n_var)])


def _nsa_attend(pq, pk, pv, p3, sel, ocmp, batch, seq):
    tq = NSA_ATTEND_TQ
    g = NSA_KV_GROUPS
    nq = seq // tq
    vrows = NSA_V_DIM + BF16_SUBLANES
    return pl.pallas_call(
        functools.partial(_nsa_attend_kernel, tq=tq, n_var=seq // SLC_TILE),
        grid=(batch, nq),
        in_specs=[
            pl.BlockSpec((NSA_HEADS * NSA_QK_PAD, tq), lambda b, i: (0, b * nq + i)),
            pl.BlockSpec((seq, g * NSA_QK_PAD), lambda b, i: (b, 0)),
            pl.BlockSpec((seq, g * NSA_V_DIM), lambda b, i: (b, 0)),
            pl.BlockSpec((seq, g * NSA_QK_PAD), lambda b, i: (b, 1)),
            pl.BlockSpec((seq, g * NSA_V_DIM), lambda b, i: (b, 1)),
            pl.BlockSpec((tq, LANES), lambda b, i: (b * nq + i, 6)),
            pl.BlockSpec((g * LANES, tq), lambda b, i: (b, i)),
            pl.BlockSpec((tq, NSA_WIDTH), lambda b, i: (b * nq + i, 0)),
        ],
        out_specs=pl.BlockSpec((tq, NSA_WIDTH), lambda b, i: (b * nq + i, 0)),
        out_shape=jax.ShapeDtypeStruct((batch * seq, NSA_WIDTH), BF16),
        scratch_shapes=[pltpu.VMEM((g, vrows, seq), BF16), pltpu.VMEM((g, vrows, seq), BF16),
                        pltpu.VMEM((LANES, tq), F32)],
        compiler_params=_cparams(("arbitrary", "arbitrary")),
        name="nsa_attend",
    )(pq, pk, pv, pk, pv, p3, sel, ocmp)


def _pad_head_rows(wt, heads):
    k = wt.shape[1]
    wt = wt.reshape(heads, NSA_QK_DIM, k)
    wt = jnp.pad(wt, ((0, 0), (0, NSA_QK_PAD - NSA_QK_DIM), (0, 0)))
    return wt.reshape(heads * NSA_QK_PAD, k)


def _pad_gain(gain, scale=1.0):
    return jnp.pad(gain * scale, (0, NSA_QK_PAD - NSA_QK_DIM))


def _overlap_matrix(nc, ns):
    i = np.arange(nc)[:, None]
    j = np.arange(ns)[None, :]
    lo = np.maximum(i * CMP_STRIDE, j * SEL_BLOCK)
    hi = np.minimum(i * CMP_STRIDE + CMP_BLOCK, (j + 1) * SEL_BLOCK)
    return (np.maximum(hi - lo, 0) / CMP_STRIDE).astype(np.float32)


def kernel(x, norm_attn, w_in, fox_f_bias, fox_q_gain, fox_k_gain,
           nsa_q_gain, nsa_kc_gain, nsa_ks_gain, nsa_kw_gain,
           cmp_pe_k, cmp_w1_k, cmp_w2_k, cmp_pe_v, cmp_w1_v, cmp_w2_v,
           w_up_fox, w_up_nsa, w_out, norm_ffn, w_ffn_gate, w_ffn_up, w_ffn_down):
    batch, seq, d = x.shape
    m = batch * seq
    depth = w_in.shape[0]
    pts = [0] + [int(p) for p in np.cumsum(IN_SPLITS)]
    nblk = seq // CMP_STRIDE
    n_cmp = nblk - CMP_BLOCK // CMP_STRIDE + 1
    ns = seq // SEL_BLOCK

    slope = jnp.exp2(-8.0 * jnp.arange(1, NSA_HEADS + 1, dtype=F32) / NSA_HEADS) * LOG2E
    s1, s2, s3 = [p.astype(F32) for p in _split3(slope)]
    q_spare = jnp.stack([256.0 * s1, 256.0 * s2, 256.0 * s3, s1, s2, s3, -slope, -slope, -slope],
                        axis=1)
    q_spare = jnp.pad(q_spare, ((0, 0), (0, NSA_QK_PAD - NSA_QK_DIM - 9)))
    ovt_np = np.zeros((LANES, nblk), np.float32)
    ovt_np[:ns, :n_cmp] = _overlap_matrix(n_cmp, ns).T
    ovt = jnp.asarray(ovt_np, BF16)

    w_in_t = jnp.swapaxes(w_in, 1, 2)

    xf = x.reshape(m, d)
    for l in range(depth):
        wt = w_in_t[l]
        row = dict(zip(("fq", "fk", "fv", "fl", "nq", "kc", "vc", "ks", "vs", "kw", "vw", "ng",
                        "ga", "gb"), pts))

        gain1 = jnp.concatenate([jnp.tile(fox_q_gain[l] * (FOX_HEAD_DIM ** -0.5 * LOG2E), FOX_HEADS),
                                 jnp.tile(fox_k_gain[l], FOX_HEADS),
                                 jnp.ones((FOX_WIDTH,), F32)])
        flag1 = jnp.concatenate([jnp.ones((2 * FOX_WIDTH,), F32), jnp.zeros((FOX_WIDTH,), F32)])
        q_gain = jnp.broadcast_to(nsa_q_gain[l] * (NSA_QK_DIM ** -0.5 * LOG2E),
                                  (NSA_HEADS, NSA_QK_DIM))
        q_table = jnp.broadcast_to(
            jnp.concatenate([q_gain, q_spare], axis=1).reshape(NSA_HEADS * NSA_QK_PAD, 1),
            (NSA_HEADS * NSA_QK_PAD, LANES))
        gain_k = jnp.concatenate([jnp.tile(_pad_gain(nsa_ks_gain[l]), NSA_KV_GROUPS),
                                  jnp.tile(_pad_gain(nsa_kw_gain[l]), NSA_KV_GROUPS)])
        n_small = FOX_HEADS + 3 * NSA_HEADS
        w3 = jnp.concatenate([_pad_head_rows(wt[row["kc"]:row["vc"]], NSA_KV_GROUPS),
                              wt[row["vc"]:row["ks"]], wt[row["fl"]:row["nq"]],
                              wt[row["ng"]:row["ga"]],
                              jnp.zeros((LANES - n_small, d), F32)], axis=0)

        xn = _rmsnorm(xf, norm_attn[l])
        q_tile = NSA_HPG * NSA_QK_DIM
        p1 = _project(xn, wt, list(range(0, 3 * FOX_WIDTH, FOX_PROJ_TN)), FOX_PROJ_TN, gain1,
                      flag1, BF16, PROJ_TM, "proj_fox", group=FOX_HEAD_DIM, count=FOX_HEAD_DIM)
        pq = _project_qt(xn, wt, [row["nq"], row["nq"] + q_tile], q_tile, q_table, PROJ_TM, seq,
                         "proj_nsa_q")
        w_k = jnp.concatenate([wt[row["ks"]:row["vs"]], wt[row["kw"]:row["vw"]]], axis=0)
        w_v = jnp.concatenate([wt[row["vs"]:row["kw"]], wt[row["vw"]:row["ng"]]], axis=0)
        pk = _project(xn, w_k, [0], 2 * KV_K, gain_k, jnp.ones_like(gain_k), BF16,
                      PROJ_TM, "proj_nsa_k", pad_from=NSA_QK_DIM, pad_to=NSA_QK_PAD,
                      group=NSA_QK_PAD, count=NSA_QK_DIM, aug="key", seq=seq)
        pv = _project(xn, w_v, [0], 2 * KV_V, None, None, BF16, PROJ_TM, "proj_nsa_v")
        p3 = _project(xn, w3, [0], w3.shape[0], None, None, F32, PROJ_TM, "proj_f32")

        bias_row = jnp.pad(fox_f_bias[l], (0, LANES - FOX_HEADS)).reshape(1, LANES)
        ccol, crow = _decay(p3, bias_row, batch, seq)
        o_a = _fox_attention(p1, ccol, crow.reshape(batch, FOX_HEADS, 1, seq), batch, seq)

        pad_d = NSA_QK_PAD - NSA_QK_DIM
        pek = jnp.pad(cmp_pe_k[l], ((0, 0), (0, pad_d)))
        w1k = jnp.pad(cmp_w1_k[l].reshape(CMP_BLOCK, NSA_QK_DIM, CMP_HIDDEN),
                      ((0, 0), (0, pad_d), (0, 0))).astype(BF16)
        w2k = jnp.pad(cmp_w2_k[l], ((0, 0), (0, pad_d))).astype(BF16)
        gk = _pad_gain(nsa_kc_gain[l]).reshape(1, NSA_QK_PAD)
        w1v = cmp_w1_v[l].reshape(CMP_BLOCK, NSA_V_DIM, CMP_HIDDEN).astype(BF16)
        w2v = cmp_w2_v[l].astype(BF16)
        kc, vct = _compress(p3, pek, w1k, w2k, gk, cmp_pe_v[l], w1v, w2v, batch, seq)
        sel, ocmp = _nsa_select(pq, p3, kc, vct, ovt, batch, seq)
        o_b = _nsa_attend(pq, pk, pv, p3, sel, ocmp, batch, seq)

        merged = _merge(xn, o_a, o_b, wt, row["ga"], row["gb"], w_up_fox[l], w_up_nsa[l])
        hres, hn = _out_proj_norm(merged, w_out[l], xf, norm_ffn[l])

        act = _swiglu(hn, w_ffn_gate[l], w_ffn_up[l])
        xf = _matmul_residual(act, w_ffn_down[l], hres, FFN_DOWN_TM, FFN_DOWN_TN, "ffn_down")
    return xf.reshape(batch, seq, d)
```

```python
import functools

import numpy as np
import jax
import jax.numpy as jnp
from jax import lax
from jax.experimental import pallas as pl
from jax.experimental.pallas import tpu as pltpu

F32 = jnp.float32
BF16 = jnp.bfloat16

D_MODEL = 2048
FOX_HEADS = 8
FOX_HEAD_DIM = 128
FOX_WIDTH = FOX_HEADS * FOX_HEAD_DIM
NSA_HEADS = 8
NSA_KV_GROUPS = 2
NSA_HPG = NSA_HEADS // NSA_KV_GROUPS
NSA_QK_DIM = 192
NSA_QK_PAD = 256
NSA_V_DIM = 128
NSA_WIDTH = NSA_HEADS * NSA_V_DIM
CMP_BLOCK = 32
CMP_STRIDE = 16
CMP_HIDDEN = 256
SEL_BLOCK = 64
SEL_TOPK = 16
SEL_LOCAL = 2
FORCE_SCORE = 1.0e4
WINDOW = 512
KV_K = NSA_KV_GROUPS * NSA_QK_DIM
KV_V = NSA_KV_GROUPS * NSA_V_DIM
D_FF = -(-(8 * D_MODEL) // (3 * 256)) * 256
RMS_EPS = 1e-6
IN_SPLITS = (FOX_WIDTH, FOX_WIDTH, FOX_WIDTH, FOX_HEADS,
             NSA_HEADS * NSA_QK_DIM, KV_K, KV_V, KV_K, KV_V, KV_K, KV_V,
             3 * NSA_HEADS, D_MODEL, D_MODEL)

_NT = (((1,), (1,)), ((), ()))

LANES = 128
SUBLANES = 8
BF16_SUBLANES = 16
NEG = -1.0e30
LOG2E = 1.4426950408889634
SLC_TILE = 512
VMEM_LIMIT = 56 * 1024 * 1024

RMS_TM = 512
PROJ_TM = 1024
FOX_PROJ_TN = 1024
MERGE_TM, MERGE_TN = 512, 512
OUT_PROJ_TM = 512
SWIGLU_TM, SWIGLU_TN = 1024, 512
FFN_DOWN_TM, FFN_DOWN_TN = 512, 512
FOX_TQ = 512
FOX_HEADS_PER_STEP = 4
NSA_SELECT_TQ = 512
NSA_ATTEND_TQ = 256


def _cparams(sem):
    return pltpu.CompilerParams(dimension_semantics=sem, vmem_limit_bytes=VMEM_LIMIT)


def _rms_kernel(x_ref, g_ref, o_ref):
    x = x_ref[...]
    ms = jnp.mean(x * x, axis=-1, keepdims=True)
    o_ref[...] = (x * lax.rsqrt(ms + RMS_EPS) * g_ref[...]).astype(o_ref.dtype)


def _rmsnorm(x, gain, tm=RMS_TM):
    m, d = x.shape
    return pl.pallas_call(
        _rms_kernel,
        grid=(m // tm,),
        in_specs=[pl.BlockSpec((tm, d), lambda i: (i, 0)),
                  pl.BlockSpec((1, d), lambda i: (0, 0))],
        out_specs=pl.BlockSpec((tm, d), lambda i: (i, 0)),
        out_shape=jax.ShapeDtypeStruct((m, d), BF16),
        compiler_params=_cparams(("arbitrary",)),
        name="rmsnorm",
    )(x, gain.reshape(1, d))


STAGE_CHUNK = 512
EPILOGUE_SPLIT = 4


def _stage_weight(w_ref, wb_ref, transposed=False, pad_from=0, pad_to=0):
    if not transposed:
        wb_ref[...] = w_ref[...].astype(BF16)
        return
    n_in, k = w_ref.shape
    for c in range(k // STAGE_CHUNK):
        cols = slice(c * STAGE_CHUNK, (c + 1) * STAGE_CHUNK)
        w = w_ref[:, cols]
        if pad_from != pad_to:
            zero = jnp.zeros((pad_to - pad_from, STAGE_CHUNK), F32)
            w = jnp.concatenate(
                [piece for h in range(n_in // pad_from)
                 for piece in (w[h * pad_from:(h + 1) * pad_from, :], zero)], axis=0)
        wb_ref[cols, :] = w.T.astype(BF16)


def _first_m_step():
    return pl.program_id(1) == 0


AUG_LANE = NSA_QK_DIM - LANES


def _key_aug(pos):
    lane = lax.broadcasted_iota(jnp.int32, pos.shape, 1)
    hi = (pos >> 8).astype(F32)
    lo = (pos & 255).astype(F32)
    return jnp.where(lane < AUG_LANE, 0.0,
                     jnp.where(lane < AUG_LANE + 3, hi,
                               jnp.where(lane < AUG_LANE + 6, lo,
                                         jnp.where(lane < AUG_LANE + 9, 1.0, 0.0))))


def _proj_kernel(a_ref, w_ref, gain_ref, flag_ref, o_ref, wb_ref, *, pad_from, pad_to,
                 group, count, aug, seq):
    @pl.when(_first_m_step())
    def _():
        _stage_weight(w_ref, wb_ref, True, pad_from, pad_to)

    if group == 0:
        o_ref[...] = jnp.dot(a_ref[...], wb_ref[...],
                             preferred_element_type=F32).astype(o_ref.dtype)
        return
    tm = a_ref.shape[0]
    rows = tm // EPILOGUE_SPLIT
    ys = [jnp.dot(a_ref[r * rows:(r + 1) * rows, :], wb_ref[...], preferred_element_type=F32)
          for r in range(EPILOGUE_SPLIT)]
    for r, y in enumerate(ys):
        if aug:
            pos = ((pl.program_id(1) * tm) % seq + r * rows
                   + lax.broadcasted_iota(jnp.int32, (rows, LANES), 0))
        for c in range(y.shape[1] // group):
            sl = slice(c * group, (c + 1) * group)
            yc = y[:, sl]
            ss = jnp.sum(yc * yc, axis=-1, keepdims=True)
            rs = lax.rsqrt(ss * (1.0 / count) + RMS_EPS)
            scale = jnp.where(flag_ref[:, sl] > 0.0, rs, 1.0)
            out = yc * scale * gain_ref[:, sl]
            rsl = slice(r * rows, (r + 1) * rows)
            if not aug:
                o_ref[rsl, sl] = out.astype(o_ref.dtype)
                continue
            up = slice((c + 1) * group - LANES, (c + 1) * group)
            o_ref[rsl, c * group:(c + 1) * group - LANES] = out[:, :group - LANES].astype(o_ref.dtype)
            o_ref[rsl, up] = (out[:, group - LANES:] + _key_aug(pos)).astype(o_ref.dtype)


def _row_window(starts, rows, k):
    def index(j, i):
        start = starts[-1]
        for t in range(len(starts) - 2, -1, -1):
            start = jnp.where(j == t, starts[t], start)
        return pl.multiple_of(start, SUBLANES), 0
    assert all(s % SUBLANES == 0 for s in starts)
    return pl.BlockSpec((pl.Element(rows), pl.Element(k)), index)


def _project(a, wt, starts, tn_in, gain, flag, out_dtype, tm, name,
             pad_from=0, pad_to=0, group=0, count=1, aug="", seq=0):
    m, k = a.shape
    n_tiles = len(starts)
    tn_out = tn_in if pad_from == pad_to else tn_in // pad_from * pad_to
    n_out = n_tiles * tn_out
    if gain is None:
        gain = jnp.ones((n_out,), F32)
        flag = jnp.zeros((n_out,), F32)
    assert not aug or seq % tm == 0
    operands = [a, wt, gain.reshape(1, n_out), flag.reshape(1, n_out)]
    in_specs = [pl.BlockSpec((tm, k), lambda j, i: (i, 0)),
                _row_window(starts, tn_in, k),
                pl.BlockSpec((1, tn_out), lambda j, i: (0, j)),
                pl.BlockSpec((1, tn_out), lambda j, i: (0, j))]
    return pl.pallas_call(
        functools.partial(_proj_kernel, pad_from=pad_from, pad_to=pad_to, group=group,
                          count=count, aug=aug, seq=seq),
        grid=(n_tiles, m // tm),
        in_specs=in_specs,
        out_specs=pl.BlockSpec((tm, tn_out), lambda j, i: (i, j)),
        out_shape=jax.ShapeDtypeStruct((m, n_out), out_dtype),
        scratch_shapes=[pltpu.VMEM((k, tn_out), BF16)],
        compiler_params=_cparams(("arbitrary", "arbitrary")),
        name=name,
    )(*operands)


def _proj_qt_kernel(a_ref, w_ref, tab_ref, o_ref, wb_ref, *, head, head_pad, seq):
    @pl.when(_first_m_step())
    def _():
        wb_ref[...] = w_ref[...].astype(BF16)

    tm = a_ref.shape[0]
    cols = tm // EPILOGUE_SPLIT
    reps = cols // LANES
    spare = head_pad - head
    ys = [lax.dot_general(wb_ref[...], a_ref[r * cols:(r + 1) * cols, :], _NT,
                          preferred_element_type=F32) for r in range(EPILOGUE_SPLIT)]
    row = lax.broadcasted_iota(jnp.int32, (spare, cols), 0)
    for r, y in enumerate(ys):
        lanes = slice(r * cols, (r + 1) * cols)
        pos = ((pl.program_id(1) * tm) % seq + r * cols
               + lax.broadcasted_iota(jnp.int32, (1, cols), 1)).astype(F32)
        for h in range(y.shape[0] // head):
            yh = y[h * head:(h + 1) * head, :]
            rs = lax.rsqrt(jnp.sum(yh * yh, axis=0, keepdims=True) * (1.0 / head) + RMS_EPS)
            slot = h * head_pad
            gain = jnp.concatenate([tab_ref[slot:slot + head, :]] * reps, axis=1)
            o_ref[slot:slot + head, lanes] = (yh * rs * gain).astype(o_ref.dtype)
            if not spare:
                continue
            tab = jnp.concatenate([tab_ref[slot + head:slot + head_pad, :]] * reps, axis=1)
            u = pos * tab
            hi = u.astype(BF16).astype(F32)
            r1 = u - hi
            mid = r1.astype(BF16).astype(F32)
            extra = jnp.where(row < 6, tab,
                              jnp.where(row == 6, hi,
                                        jnp.where(row == 7, mid,
                                                  jnp.where(row == 8, r1 - mid, 0.0))))
            o_ref[slot + head:slot + head_pad, lanes] = extra.astype(o_ref.dtype)


def _project_qt(a, wt, starts, tn_in, table, head, head_pad, tm, seq, name):
    m, k = a.shape
    tn_out = tn_in // head * head_pad
    assert seq % tm == 0
    return pl.pallas_call(
        functools.partial(_proj_qt_kernel, head=head, head_pad=head_pad, seq=seq),
        grid=(len(starts), m // tm),
        in_specs=[pl.BlockSpec((tm, k), lambda j, i: (i, 0)),
                  _row_window(starts, tn_in, k),
                  pl.BlockSpec((tn_out, LANES), lambda j, i: (j, 0))],
        out_specs=pl.BlockSpec((tn_out, tm), lambda j, i: (j, i)),
        out_shape=jax.ShapeDtypeStruct((len(starts) * tn_out, m), BF16),
        scratch_shapes=[pltpu.VMEM((tn_in, k), BF16)],
        compiler_params=_cparams(("arbitrary", "arbitrary")),
        name=name,
    )(a, wt, table)


def _out_norm_kernel(a_ref, w_ref, r_ref, g_ref, h_ref, hn_ref, wb_ref):
    @pl.when(pl.program_id(0) == 0)
    def _():
        _stage_weight(w_ref, wb_ref)

    h = r_ref[...] + jnp.dot(a_ref[...], wb_ref[...], preferred_element_type=F32)
    h_ref[...] = h
    ms = jnp.mean(h * h, axis=-1, keepdims=True)
    hn_ref[...] = (h * lax.rsqrt(ms + RMS_EPS) * g_ref[...]).astype(hn_ref.dtype)


def _out_proj_norm(a, w, res, gain, tm=OUT_PROJ_TM):
    m, k = a.shape
    n = w.shape[1]
    full = lambda i: (0, 0)
    row = lambda i: (i, 0)
    return pl.pallas_call(
        _out_norm_kernel,
        grid=(m // tm,),
        in_specs=[pl.BlockSpec((tm, k), row),
                  pl.BlockSpec((k, n), full, pipeline_mode=pl.Buffered(1)),
                  pl.BlockSpec((tm, n), row), pl.BlockSpec((1, n), full)],
        out_specs=[pl.BlockSpec((tm, n), row), pl.BlockSpec((tm, n), row)],
        out_shape=[jax.ShapeDtypeStruct((m, n), F32), jax.ShapeDtypeStruct((m, n), BF16)],
        scratch_shapes=[pltpu.VMEM((k, n), BF16)],
        compiler_params=_cparams(("arbitrary",)),
        name="out_proj_norm",
    )(a, w, res, gain.reshape(1, n))


def _mm_res_kernel(a_ref, w_ref, r_ref, o_ref, wb_ref):
    @pl.when(_first_m_step())
    def _():
        _stage_weight(w_ref, wb_ref)

    o_ref[...] = r_ref[...] + jnp.dot(a_ref[...], wb_ref[...], preferred_element_type=F32)


def _matmul_residual(a, w, res, tm, tn, name):
    m, k = a.shape
    n = w.shape[1]
    return pl.pallas_call(
        _mm_res_kernel,
        grid=(n // tn, m // tm),
        in_specs=[pl.BlockSpec((tm, k), lambda j, i: (i, 0)),
                  pl.BlockSpec((k, tn), lambda j, i: (0, j)),
                  pl.BlockSpec((tm, tn), lambda j, i: (i, j))],
        out_specs=pl.BlockSpec((tm, tn), lambda j, i: (i, j)),
        out_shape=jax.ShapeDtypeStruct((m, n), F32),
        scratch_shapes=[pltpu.VMEM((k, tn), BF16)],
        compiler_params=_cparams(("arbitrary", "arbitrary")),
        name=name,
    )(a, w, res)


def _merge_kernel(xn_ref, oa_ref, ob_ref, wga_ref, wgb_ref, wuf_ref, wun_ref, o_ref,
                  bga_ref, bgb_ref, buf_ref, bun_ref):
    @pl.when(_first_m_step())
    def _():
        _stage_weight(wga_ref, bga_ref, True)
        _stage_weight(wgb_ref, bgb_ref, True)
        _stage_weight(wuf_ref, buf_ref)
        _stage_weight(wun_ref, bun_ref)

    xn = xn_ref[...]
    ga = jax.nn.sigmoid(jnp.dot(xn, bga_ref[...], preferred_element_type=F32))
    ua = jnp.dot(oa_ref[...], buf_ref[...], preferred_element_type=F32)
    acc = ga * ua
    gb = jax.nn.sigmoid(jnp.dot(xn, bgb_ref[...], preferred_element_type=F32))
    ub = jnp.dot(ob_ref[...], bun_ref[...], preferred_element_type=F32)
    o_ref[...] = (acc + gb * ub).astype(o_ref.dtype)


def _merge(xn, oa, ob, wt, row_a, row_b, wuf, wun, tm=MERGE_TM, tn=MERGE_TN):
    m, d = xn.shape
    n = wuf.shape[1]
    ka = oa.shape[1]
    kb = ob.shape[1]
    row = lambda j, i: (i, 0)
    col = lambda j, i: (0, j)
    return pl.pallas_call(
        _merge_kernel,
        grid=(n // tn, m // tm),
        in_specs=[pl.BlockSpec((tm, d), row), pl.BlockSpec((tm, ka), row),
                  pl.BlockSpec((tm, kb), row),
                  _row_window([row_a + t * tn for t in range(n // tn)], tn, d),
                  _row_window([row_b + t * tn for t in range(n // tn)], tn, d),
                  pl.BlockSpec((ka, tn), col), pl.BlockSpec((kb, tn), col)],
        out_specs=pl.BlockSpec((tm, tn), lambda j, i: (i, j)),
        out_shape=jax.ShapeDtypeStruct((m, n), BF16),
        scratch_shapes=[pltpu.VMEM((d, tn), BF16), pltpu.VMEM((d, tn), BF16),
                        pltpu.VMEM((ka, tn), BF16), pltpu.VMEM((kb, tn), BF16)],
        compiler_params=_cparams(("arbitrary", "arbitrary")),
        name="gated_merge",
    )(xn, oa, ob, wt, wt, wuf, wun)


def _swiglu_kernel(a_ref, wg_ref, wu_ref, o_ref, bg_ref, bu_ref):
    @pl.when(_first_m_step())
    def _():
        _stage_weight(wg_ref, bg_ref)
        _stage_weight(wu_ref, bu_ref)

    a = a_ref[...]
    gt = jnp.dot(a, bg_ref[...], preferred_element_type=F32)
    up = jnp.dot(a, bu_ref[...], preferred_element_type=F32)
    o_ref[...] = (gt * jax.nn.sigmoid(gt) * up).astype(o_ref.dtype)


def _swiglu(a, wg, wu, tm=SWIGLU_TM, tn=SWIGLU_TN):
    m, k = a.shape
    n = wg.shape[1]
    return pl.pallas_call(
        _swiglu_kernel,
        grid=(n // tn, m // tm),
        in_specs=[pl.BlockSpec((tm, k), lambda j, i: (i, 0)),
                  pl.BlockSpec((k, tn), lambda j, i: (0, j)),
                  pl.BlockSpec((k, tn), lambda j, i: (0, j))],
        out_specs=pl.BlockSpec((tm, tn), lambda j, i: (i, j)),
        out_shape=jax.ShapeDtypeStruct((m, n), BF16),
        scratch_shapes=[pltpu.VMEM((k, tn), BF16), pltpu.VMEM((k, tn), BF16)],
        compiler_params=_cparams(("arbitrary", "arbitrary")),
        name="swiglu_up",
    )(a, wg, wu)


def _split3(x):
    hi = x.astype(BF16)
    r1 = x - hi.astype(F32)
    mid = r1.astype(BF16)
    lo = (r1 - mid.astype(F32)).astype(BF16)
    return hi, mid, lo


def _decay_kernel(z_ref, b_ref, ccol_ref, crow_ref, *, blk):
    t = z_ref.shape[0]
    r = lax.broadcasted_iota(jnp.int32, (blk, blk), 0)
    c = lax.broadcasted_iota(jnp.int32, (blk, blk), 1)
    tri = jnp.where(r >= c, 1.0, 0.0).astype(BF16)
    carry = jnp.zeros((1, LANES), F32)
    for s in range(t // blk):
        rows = slice(s * blk, (s + 1) * blk)
        z = z_ref[rows, :] + b_ref[...]
        logf = (jnp.minimum(z, 0.0) - jnp.log1p(jnp.exp(-jnp.abs(z)))) * LOG2E
        hi, mid, lo = _split3(logf)
        cb = (jnp.dot(tri, hi, preferred_element_type=F32)
              + jnp.dot(tri, mid, preferred_element_type=F32)
              + jnp.dot(tri, lo, preferred_element_type=F32)) + carry
        carry = cb[blk - 1:blk, :]
        ccol_ref[rows, :] = cb
        crow_ref[0, :, rows] = cb.T[:FOX_HEADS, :]


def _decay(p3, bias_row, batch, seq, blk=256):
    return pl.pallas_call(
        functools.partial(_decay_kernel, blk=blk),
        grid=(batch,),
        in_specs=[pl.BlockSpec((seq, LANES), lambda b: (b, 6)),
                  pl.BlockSpec((1, LANES), lambda b: (0, 0))],
        out_specs=[pl.BlockSpec((seq, LANES), lambda b: (b, 0)),
                   pl.BlockSpec((1, FOX_HEADS, seq), lambda b: (b, 0, 0))],
        out_shape=[jax.ShapeDtypeStruct((batch * seq, LANES), F32),
                   jax.ShapeDtypeStruct((batch, FOX_HEADS, seq), F32)],
        compiler_params=_cparams(("arbitrary",)),
        name="fox_decay_cumsum",
    )(p3, bias_row)


def _transpose_bf16(x):
    return x.astype(F32).T.astype(BF16)


def _normalised(acc, d):
    return acc[:d, :] * (1.0 / jnp.maximum(acc[d:d + 1, :], 1e-30))


def _fox_kernel(q_ref, k_ref, v_ref, ccol_ref, crow_ref, o_ref, vt_ref, ka_ref, *, tq, nq, nh):
    hh = pl.program_id(1)
    i = pl.program_id(2)
    dh = FOX_HEAD_DIM

    def pieces(x, lane, first):
        hi = x.astype(BF16).astype(F32)
        r1 = x - hi
        mid = r1.astype(BF16).astype(F32)
        return jnp.where(lane == first, hi,
                         jnp.where(lane == first + 1, mid,
                                   jnp.where(lane == first + 2, r1 - mid, 0.0)))

    @pl.when(i == 0)
    def _():
        cc = ccol_ref[...]
        lane = lax.broadcasted_iota(jnp.int32, cc.shape, 1)
        for h in range(nh):
            vt_ref[h, 0:dh, :] = _transpose_bf16(v_ref[:, h * dh:(h + 1) * dh])
            vt_ref[h, dh:, :] = jnp.ones((vt_ref.shape[1] - dh, vt_ref.shape[2]), BF16)
            cj = jnp.sum(jnp.where(lane == hh * nh + h, cc, 0.0), axis=-1, keepdims=True)
            extra = pieces(-cj, lane, 0) + jnp.where(lane < 3, 0.0, jnp.where(lane < 6, 1.0, 0.0))
            ka_ref[h, :, 0:dh] = k_ref[:, h * dh:(h + 1) * dh]
            ka_ref[h, :, dh:] = extra.astype(BF16)

    row = lax.broadcasted_iota(jnp.int32, (dh, tq), 0)
    qts = []
    for h in range(nh):
        ci = crow_ref[0, h]
        extra = pieces(ci, row, 3) + jnp.where(row < 3, 1.0, 0.0)
        qts.append(jnp.concatenate([q_ref[h * dh:(h + 1) * dh, :], extra.astype(BF16)], axis=0))
    rk = lax.broadcasted_iota(jnp.int32, (tq, tq), 0)
    cq = lax.broadcasted_iota(jnp.int32, (tq, tq), 1)

    def scores(h, k0):
        return jnp.dot(ka_ref[h, k0:k0 + tq, :], qts[h], preferred_element_type=F32)

    def variant(n):
        def run():
            starts = [t * tq for t in range(n, -1, -1)]
            tiles = [[jnp.where(rk <= cq, scores(h, starts[0]), NEG)]
                     + [scores(h, k0) for k0 in starts[1:]] for h in range(nh)]
            ms = [jnp.max(tiles[h][0], axis=0, keepdims=True) for h in range(nh)]
            accs = [jnp.dot(vt_ref[h, :, starts[0]:starts[0] + tq],
                            jnp.exp2(tiles[h][0] - ms[h]).astype(BF16),
                            preferred_element_type=F32) for h in range(nh)]
            for t in range(1, n + 1):
                k0 = starts[t]
                for h in range(nh):
                    st = tiles[h][t]
                    m_new = jnp.maximum(ms[h], jnp.max(st, axis=0, keepdims=True))
                    accs[h] = jnp.exp2(ms[h] - m_new) * accs[h] + jnp.dot(
                        vt_ref[h, :, k0:k0 + tq], jnp.exp2(st - m_new).astype(BF16),
                        preferred_element_type=F32)
                    ms[h] = m_new
            for h in range(nh):
                o_ref[:, h * dh:(h + 1) * dh] = _normalised(accs[h], dh).T.astype(o_ref.dtype)
        return run

    lax.switch(i, [variant(n) for n in range(nq)])


def _fox_attention(fq, p1, ccol, crow4, batch, seq, tq=FOX_TQ, nh=FOX_HEADS_PER_STEP):
    nq = seq // tq
    hg = FOX_HEADS // nh
    w = nh * FOX_HEAD_DIM
    return pl.pallas_call(
        functools.partial(_fox_kernel, tq=tq, nq=nq, nh=nh),
        grid=(batch, hg, nq),
        in_specs=[pl.BlockSpec((w, tq), lambda b, hh, i: (hh, b * nq + i)),
                  pl.BlockSpec((seq, w), lambda b, hh, i: (b, hh)),
                  pl.BlockSpec((seq, w), lambda b, hh, i: (b, hg + hh)),
                  pl.BlockSpec((seq, LANES), lambda b, hh, i: (b, 0)),
                  pl.BlockSpec((1, nh, 1, tq), lambda b, hh, i: (b, hh, 0, i))],
        out_specs=pl.BlockSpec((tq, w), lambda b, hh, i: (b * nq + i, hh)),
        out_shape=jax.ShapeDtypeStruct((batch * seq, FOX_WIDTH), BF16),
        scratch_shapes=[pltpu.VMEM((nh, FOX_HEAD_DIM + BF16_SUBLANES, seq), BF16),
                        pltpu.VMEM((nh, seq, 2 * FOX_HEAD_DIM), BF16)],
        compiler_params=_cparams(("arbitrary", "arbitrary", "arbitrary")),
        name="fox_attention",
    )(fq, p1, p1, ccol, crow4)


def _compress_one(z_refs, pe_ref, w1_ref, w2_ref, nblk):
    half = CMP_BLOCK // 2
    first = jnp.zeros((nblk, CMP_HIDDEN), F32)
    second = jnp.zeros((nblk, CMP_HIDDEN), F32)
    for p in range(half):
        rows = pl.ds(p, nblk, stride=CMP_STRIDE)
        zp = [z_ref[rows, :] for z_ref in z_refs]
        zp = zp[0] if len(zp) == 1 else jnp.concatenate(zp, axis=1)
        first += jnp.dot((zp + pe_ref[p:p + 1, :]).astype(BF16), w1_ref[p],
                         preferred_element_type=F32)
        second += jnp.dot((zp + pe_ref[half + p:half + p + 1, :]).astype(BF16),
                          w1_ref[half + p], preferred_element_type=F32)
    hid = first + pltpu.roll(second, nblk - 1, 0)
    act = (hid * jax.nn.sigmoid(hid)).astype(BF16)
    return jnp.dot(act, w2_ref[...], preferred_element_type=F32)


def _compress_kernel(zk0_ref, zk1_ref, zv_ref, pek_ref, w1k_ref, w2k_ref, gk_ref,
                     pev_ref, w1v_ref, w2v_ref, kc_ref, vc_ref, *, nblk):
    kc = _compress_one((zk0_ref, zk1_ref), pek_ref, w1k_ref, w2k_ref, nblk)
    ms = jnp.sum(kc * kc, axis=-1, keepdims=True) * (1.0 / NSA_QK_DIM)
    kc = kc * lax.rsqrt(ms + RMS_EPS) * gk_ref[...]
    pos = CMP_STRIDE * lax.broadcasted_iota(jnp.int32, (nblk, LANES), 0) + (CMP_BLOCK - 1)
    up = NSA_QK_PAD - LANES
    kc_ref[:, :up] = kc[:, :up].astype(kc_ref.dtype)
    kc_ref[:, up:] = (kc[:, up:] + _key_aug(pos)).astype(kc_ref.dtype)
    vc = _compress_one((zv_ref,), pev_ref, w1v_ref, w2v_ref, nblk)
    vc_ref[...] = vc.T.astype(vc_ref.dtype)


def _compress(p3, pek, w1k, w2k, gk, pev, w1v, w2v, batch, seq):
    g = NSA_KV_GROUPS
    nblk = seq // CMP_STRIDE
    full2 = lambda b, gg: (0, 0)
    full3 = lambda b, gg: (0, 0, 0)
    return pl.pallas_call(
        functools.partial(_compress_kernel, nblk=nblk),
        grid=(batch, g),
        in_specs=[pl.BlockSpec((seq, LANES), lambda b, gg: (b, 2 * gg)),
                  pl.BlockSpec((seq, LANES), lambda b, gg: (b, 2 * gg + 1)),
                  pl.BlockSpec((seq, NSA_V_DIM), lambda b, gg: (b, 4 + gg)),
                  pl.BlockSpec(pek.shape, full2), pl.BlockSpec(w1k.shape, full3),
                  pl.BlockSpec(w2k.shape, full2), pl.BlockSpec(gk.shape, full2),
                  pl.BlockSpec(pev.shape, full2), pl.BlockSpec(w1v.shape, full3),
                  pl.BlockSpec(w2v.shape, full2)],
        out_specs=[pl.BlockSpec((nblk, NSA_QK_PAD), lambda b, gg: (b * g + gg, 0)),
                   pl.BlockSpec((NSA_V_DIM, nblk), lambda b, gg: (b * g + gg, 0))],
        out_shape=[jax.ShapeDtypeStruct((batch * g * nblk, NSA_QK_PAD), BF16),
                   jax.ShapeDtypeStruct((batch * g * NSA_V_DIM, nblk), BF16)],
        compiler_params=_cparams(("arbitrary", "arbitrary")),
        name="nsa_compress",
    )(p3, p3, p3, pek, w1k, w2k, gk, pev, w1v, w2v)


def _q_heads_t(qt_ref, first_head):
    return jnp.concatenate(
        [qt_ref[(first_head + hh) * NSA_QK_PAD:(first_head + hh + 1) * NSA_QK_PAD, :]
         for hh in range(NSA_HPG)], axis=1)


def _gate_rows(glt_ref, g, hh):
    base = FOX_HEADS + (g * NSA_HPG + hh) * 3
    return [glt_ref[pl.ds(base + br, 1), :] for br in range(3)]


def _nsa_select_kernel(q_ref, kc_ref, vct_ref, gl_ref, ovt_ref,
                       sel_ref, ocmp_ref, glt_ref, *, tq, n_cmp, n_sel):
    g = pl.program_id(1)
    t0 = pl.program_id(2) * tq
    hpg = NSA_HPG
    dv = NSA_V_DIM
    q4t = _q_heads_t(q_ref, 0)
    rk = lax.broadcasted_iota(jnp.int32, (LANES, tq), 0)
    cq = lax.broadcasted_iota(jnp.int32, (LANES, tq), 1)

    s_c = jnp.dot(kc_ref[...], q4t, preferred_element_type=F32)
    dist_c = (t0 + cq) - (CMP_STRIDE * rk + (CMP_BLOCK - 1))
    mask_c = jnp.where(rk < n_cmp, dist_c, -1) >= 0
    probs = []
    p_sum = jnp.zeros((LANES, tq), F32)
    for hh in range(hpg):
        sm = jnp.where(mask_c, s_c[:, hh * tq:(hh + 1) * tq], NEG)
        m = jnp.max(sm, axis=0, keepdims=True)
        e = jnp.where(mask_c, jnp.exp2(sm - m), 0.0)
        p = e * (1.0 / jnp.maximum(jnp.sum(e, axis=0, keepdims=True), 1e-30))
        probs.append(p)
        p_sum = p_sum + p
    o_cmp = jnp.dot(vct_ref[...], jnp.concatenate(probs, axis=1).astype(BF16),
                    preferred_element_type=F32)

    ph = p_sum.astype(BF16)
    plo = (p_sum - ph.astype(F32)).astype(BF16)
    ovt = ovt_ref[...]
    imp = (jnp.dot(ovt, ph, preferred_element_type=F32)
           + jnp.dot(ovt, plo, preferred_element_type=F32))[:n_sel, :]

    rj = lax.broadcasted_iota(jnp.int32, (n_sel, tq), 0)
    tcol = t0 + lax.broadcasted_iota(jnp.int32, (n_sel, tq), 1)
    back = (tcol >> (SEL_BLOCK.bit_length() - 1)) - rj
    elig = back >= 0
    forced = jnp.where(rj == 0, 0, jnp.where(elig, back, SEL_LOCAL)) < SEL_LOCAL
    score = jnp.where(elig, jnp.where(forced, FORCE_SCORE, imp), -1.0)
    rank = jnp.zeros((n_sel, tq), F32)
    for jp in range(n_sel):
        row = score[jp:jp + 1, :]
        later = jnp.where(rj > jp, 1.0, 0.0)
        rank = rank + jnp.where(row > score, 1.0, jnp.where(row == score, later, 0.0))
    sel_ref[0:n_sel, :] = jnp.where(elig, jnp.where(rank < SEL_TOPK, 0.0, NEG), NEG)
    sel_ref[n_sel:, :] = jnp.full((LANES - n_sel, tq), NEG, F32)

    glt_ref[...] = jax.nn.sigmoid(gl_ref[...]).T
    for hh in range(hpg):
        gate = _gate_rows(glt_ref, g, hh)[0]
        ocmp_ref[:, hh * dv:(hh + 1) * dv] = (gate * o_cmp[:, hh * tq:(hh + 1) * tq]).T


def _nsa_select(pq, p3, kc, vct, ovt, batch, seq, tq=NSA_SELECT_TQ):
    g = NSA_KV_GROUPS
    nq = seq // tq
    nblk = seq // CMP_STRIDE
    n_cmp = nblk - CMP_BLOCK // CMP_STRIDE + 1
    return pl.pallas_call(
        functools.partial(_nsa_select_kernel, tq=tq, n_cmp=n_cmp, n_sel=seq // SEL_BLOCK),
        grid=(batch, g, nq),
        in_specs=[
            pl.BlockSpec((NSA_HPG * NSA_QK_PAD, tq), lambda b, gg, i: (gg, b * nq + i)),
            pl.BlockSpec((nblk, NSA_QK_PAD), lambda b, gg, i: (b * g + gg, 0)),
            pl.BlockSpec((NSA_V_DIM, nblk), lambda b, gg, i: (b * g + gg, 0)),
            pl.BlockSpec((tq, LANES), lambda b, gg, i: (b * nq + i, 6)),
            pl.BlockSpec(ovt.shape, lambda b, gg, i: (0, 0)),
        ],
        out_specs=[pl.BlockSpec((LANES, tq), lambda b, gg, i: (b * g + gg, i)),
                   pl.BlockSpec((tq, NSA_HPG * NSA_V_DIM), lambda b, gg, i: (b * nq + i, gg))],
        out_shape=[jax.ShapeDtypeStruct((batch * g * LANES, seq), F32),
                   jax.ShapeDtypeStruct((batch * seq, NSA_WIDTH), F32)],
        scratch_shapes=[pltpu.VMEM((LANES, tq), F32)],
        compiler_params=_cparams(("arbitrary", "arbitrary", "arbitrary")),
        name="nsa_select",
    )(pq, kc, vct, p3, ovt)


def _nsa_attend_kernel(q_ref, ks_ref, vs_ref, kw_ref, vw_ref, gl_ref, sel_ref,
                       ocmp_ref, o_ref, vst_ref, vwt_ref, glt_ref, *, tq, n_var):
    i = pl.program_id(1)
    t0 = i * tq
    hpg = NSA_HPG
    ng = NSA_KV_GROUPS
    dv = NSA_V_DIM
    dk = NSA_QK_PAD

    @pl.when(i == 0)
    def _():
        ones = jnp.ones((vst_ref.shape[1] - dv, vst_ref.shape[2]), BF16)
        for gg in range(ng):
            vst_ref[gg, 0:dv, :] = _transpose_bf16(vs_ref[:, gg * dv:(gg + 1) * dv])
            vst_ref[gg, dv:, :] = ones
            vwt_ref[gg, 0:dv, :] = _transpose_bf16(vw_ref[:, gg * dv:(gg + 1) * dv])
            vwt_ref[gg, dv:, :] = ones

    q4t = [_q_heads_t(q_ref, gg * hpg) for gg in range(ng)]
    glt_ref[...] = jax.nn.sigmoid(gl_ref[...]).T

    def distance(k0, rows):
        return (lax.broadcasted_iota(jnp.int32, (rows, tq), 1)
                - lax.broadcasted_iota(jnp.int32, (rows, tq), 0)) + (t0 - k0)

    def scores(gg, k, mask_bias):
        st = jnp.dot(k, q4t[gg], preferred_element_type=F32)
        return st + jnp.concatenate([mask_bias] * hpg, axis=1)

    def selection_bias(gg, k0):
        j0 = gg * LANES + k0 // SEL_BLOCK
        return jnp.concatenate(
            [jnp.broadcast_to(sel_ref[j0 + j:j0 + j + 1, :], (SEL_BLOCK, tq))
             for j in range(SLC_TILE // SEL_BLOCK)], axis=0)

    def variant(n):
        def run():
            kw0 = pl.multiple_of(jnp.maximum(t0 - WINDOW, 0), LANES)
            wrows = WINDOW + tq
            dist_w = distance(kw0, wrows)
            bias_w = jnp.where(jnp.where(dist_w >= 0, dist_w, WINDOW) < WINDOW, 0.0, NEG)
            o_win = []
            for gg in range(ng):
                sw = scores(gg, kw_ref[pl.ds(kw0, wrows), gg * dk:(gg + 1) * dk], bias_w)
                pw = jnp.exp2(sw - jnp.max(sw, axis=0, keepdims=True))
                o_win.append(_normalised(
                    jnp.dot(vwt_ref[gg, :, pl.ds(kw0, wrows)], pw.astype(BF16),
                            preferred_element_type=F32), dv))

            starts = [t * SLC_TILE for t in range(n, -1, -1)]
            causal = distance(starts[0], SLC_TILE) >= 0
            tiles = [[] for _ in range(ng)]
            for t, k0 in enumerate(starts):
                for gg in range(ng):
                    bias = selection_bias(gg, k0)
                    if t == 0:
                        bias = jnp.where(causal, bias, NEG)
                    tiles[gg].append(scores(
                        gg, ks_ref[k0:k0 + SLC_TILE, gg * dk:(gg + 1) * dk], bias))
            ms = [jnp.max(tiles[gg][0], axis=0, keepdims=True) for gg in range(ng)]
            accs = [jnp.dot(vst_ref[gg, :, starts[0]:starts[0] + SLC_TILE],
                            jnp.exp2(tiles[gg][0] - ms[gg]).astype(BF16),
                            preferred_element_type=F32) for gg in range(ng)]
            for t in range(1, n + 1):
                k0 = starts[t]
                for gg in range(ng):
                    st = tiles[gg][t]
                    m_new = jnp.maximum(ms[gg], jnp.max(st, axis=0, keepdims=True))
                    accs[gg] = jnp.exp2(ms[gg] - m_new) * accs[gg] + jnp.dot(
                        vst_ref[gg, :, k0:k0 + SLC_TILE], jnp.exp2(st - m_new).astype(BF16),
                        preferred_element_type=F32)
                    ms[gg] = m_new

            for gg in range(ng):
                o_slc = _normalised(accs[gg], dv)
                for hh in range(hpg):
                    _, g_slc, g_win = _gate_rows(glt_ref, gg, hh)
                    lanes = slice(hh * tq, (hh + 1) * tq)
                    out = g_slc * o_slc[:, lanes] + g_win * o_win[gg][:, lanes]
                    cols = slice((gg * hpg + hh) * dv, (gg * hpg + hh + 1) * dv)
                    o_ref[:, cols] = (ocmp_ref[:, cols] + out.T).astype(o_ref.dtype)
        return run

    lax.switch(t0 // SLC_TILE, [variant(n) for n in range(n_var)])


def _nsa_attend(pq, pk, pv, p3, sel, ocmp, batch, seq):
    tq = NSA_ATTEND_TQ
    g = NSA_KV_GROUPS
    nq = seq // tq
    vrows = NSA_V_DIM + BF16_SUBLANES
    return pl.pallas_call(
        functools.partial(_nsa_attend_kernel, tq=tq, n_var=seq // SLC_TILE),
        grid=(batch, nq),
        in_specs=[
            pl.BlockSpec((NSA_HEADS * NSA_QK_PAD, tq), lambda b, i: (0, b * nq + i)),
            pl.BlockSpec((seq, g * NSA_QK_PAD), lambda b, i: (b, 0)),
            pl.BlockSpec((seq, g * NSA_V_DIM), lambda b, i: (b, 0)),
            pl.BlockSpec((seq, g * NSA_QK_PAD), lambda b, i: (b, 1)),
            pl.BlockSpec((seq, g * NSA_V_DIM), lambda b, i: (b, 1)),
            pl.BlockSpec((tq, LANES), lambda b, i: (b * nq + i, 6)),
            pl.BlockSpec((g * LANES, tq), lambda b, i: (b, i)),
            pl.BlockSpec((tq, NSA_WIDTH), lambda b, i: (b * nq + i, 0)),
        ],
        out_specs=pl.BlockSpec((tq, NSA_WIDTH), lambda b, i: (b * nq + i, 0)),
        out_shape=jax.ShapeDtypeStruct((batch * seq, NSA_WIDTH), BF16),
        scratch_shapes=[pltpu.VMEM((g, vrows, seq), BF16), pltpu.VMEM((g, vrows, seq), BF16),
                        pltpu.VMEM((LANES, tq), F32)],
        compiler_params=_cparams(("arbitrary", "arbitrary")),
        name="nsa_attend",
    )(pq, pk, pv, pk, pv, p3, sel, ocmp)


def _pad_head_rows(wt, heads):
    k = wt.shape[1]
    wt = wt.reshape(heads, NSA_QK_DIM, k)
    wt = jnp.pad(wt, ((0, 0), (0, NSA_QK_PAD - NSA_QK_DIM), (0, 0)))
    return wt.reshape(heads * NSA_QK_PAD, k)


def _pad_gain(gain, scale=1.0):
    return jnp.pad(gain * scale, (0, NSA_QK_PAD - NSA_QK_DIM))


def _overlap_matrix(nc, ns):
    i = np.arange(nc)[:, None]
    j = np.arange(ns)[None, :]
    lo = np.maximum(i * CMP_STRIDE, j * SEL_BLOCK)
    hi = np.minimum(i * CMP_STRIDE + CMP_BLOCK, (j + 1) * SEL_BLOCK)
    return (np.maximum(hi - lo, 0) / CMP_STRIDE).astype(np.float32)


def kernel(x, norm_attn, w_in, fox_f_bias, fox_q_gain, fox_k_gain,
           nsa_q_gain, nsa_kc_gain, nsa_ks_gain, nsa_kw_gain,
           cmp_pe_k, cmp_w1_k, cmp_w2_k, cmp_pe_v, cmp_w1_v, cmp_w2_v,
           w_up_fox, w_up_nsa, w_out, norm_ffn, w_ffn_gate, w_ffn_up, w_ffn_down):
    batch, seq, d = x.shape
    m = batch * seq
    depth = w_in.shape[0]
    pts = [0] + [int(p) for p in np.cumsum(IN_SPLITS)]
    nblk = seq // CMP_STRIDE
    n_cmp = nblk - CMP_BLOCK // CMP_STRIDE + 1
    ns = seq // SEL_BLOCK

    slope = jnp.exp2(-8.0 * jnp.arange(1, NSA_HEADS + 1, dtype=F32) / NSA_HEADS) * LOG2E
    s1, s2, s3 = [p.astype(F32) for p in _split3(slope)]
    q_spare = jnp.stack([256.0 * s1, 256.0 * s2, 256.0 * s3, s1, s2, s3, -slope, -slope, -slope],
                        axis=1)
    q_spare = jnp.pad(q_spare, ((0, 0), (0, NSA_QK_PAD - NSA_QK_DIM - 9)))
    ovt_np = np.zeros((LANES, nblk), np.float32)
    ovt_np[:ns, :n_cmp] = _overlap_matrix(n_cmp, ns).T
    ovt = jnp.asarray(ovt_np, BF16)

    w_in_t = jnp.swapaxes(w_in, 1, 2)

    xf = x.reshape(m, d)
    for l in range(depth):
        wt = w_in_t[l]
        row = dict(zip(("fq", "fk", "fv", "fl", "nq", "kc", "vc", "ks", "vs", "kw", "vw", "ng",
                        "ga", "gb"), pts))

        fq_table = jnp.broadcast_to(
            jnp.tile(fox_q_gain[l] * (FOX_HEAD_DIM ** -0.5 * LOG2E), FOX_HEADS)[:, None],
            (FOX_WIDTH, LANES))
        gain1 = jnp.concatenate([jnp.tile(fox_k_gain[l], FOX_HEADS), jnp.ones((FOX_WIDTH,), F32)])
        flag1 = jnp.concatenate([jnp.ones((FOX_WIDTH,), F32), jnp.zeros((FOX_WIDTH,), F32)])
        q_gain = jnp.broadcast_to(nsa_q_gain[l] * (NSA_QK_DIM ** -0.5 * LOG2E),
                                  (NSA_HEADS, NSA_QK_DIM))
        q_table = jnp.broadcast_to(
            jnp.concatenate([q_gain, q_spare], axis=1).reshape(NSA_HEADS * NSA_QK_PAD, 1),
            (NSA_HEADS * NSA_QK_PAD, LANES))
        gain_k = jnp.concatenate([jnp.tile(_pad_gain(nsa_ks_gain[l]), NSA_KV_GROUPS),
                                  jnp.tile(_pad_gain(nsa_kw_gain[l]), NSA_KV_GROUPS)])
        n_small = FOX_HEADS + 3 * NSA_HEADS
        w3 = jnp.concatenate([_pad_head_rows(wt[row["kc"]:row["vc"]], NSA_KV_GROUPS),
                              wt[row["vc"]:row["ks"]], wt[row["fl"]:row["nq"]],
                              wt[row["ng"]:row["ga"]],
                              jnp.zeros((LANES - n_small, d), F32)], axis=0)

        xn = _rmsnorm(xf, norm_attn[l])
        q_tile = NSA_HPG * NSA_QK_DIM
        fq = _project_qt(xn, wt, [row["fq"]], FOX_WIDTH, fq_table, FOX_HEAD_DIM, FOX_HEAD_DIM,
                         PROJ_TM, seq, "proj_fox_q")
        p1 = _project(xn, wt, list(range(row["fk"], row["fl"], FOX_PROJ_TN)), FOX_PROJ_TN, gain1,
                      flag1, BF16, PROJ_TM, "proj_fox_kv", group=FOX_HEAD_DIM, count=FOX_HEAD_DIM)
        pq = _project_qt(xn, wt, [row["nq"], row["nq"] + q_tile], q_tile, q_table, NSA_QK_DIM,
                         NSA_QK_PAD, PROJ_TM, seq, "proj_nsa_q")
        w_k = jnp.concatenate([wt[row["ks"]:row["vs"]], wt[row["kw"]:row["vw"]]], axis=0)
        w_v = jnp.concatenate([wt[row["vs"]:row["kw"]], wt[row["vw"]:row["ng"]]], axis=0)
        pk = _project(xn, w_k, [0], 2 * KV_K, gain_k, jnp.ones_like(gain_k), BF16,
                      PROJ_TM, "proj_nsa_k", pad_from=NSA_QK_DIM, pad_to=NSA_QK_PAD,
                      group=NSA_QK_PAD, count=NSA_QK_DIM, aug="key", seq=seq)
        pv = _project(xn, w_v, [0], 2 * KV_V, None, None, BF16, PROJ_TM, "proj_nsa_v")
        p3 = _project(xn, w3, [0], w3.shape[0], None, None, F32, PROJ_TM, "proj_f32")

        bias_row = jnp.pad(fox_f_bias[l], (0, LANES - FOX_HEADS)).reshape(1, LANES)
        ccol, crow = _decay(p3, bias_row, batch, seq)
        o_a = _fox_attention(fq, p1, ccol, crow.reshape(batch, FOX_HEADS, 1, seq), batch, seq)

        pad_d = NSA_QK_PAD - NSA_QK_DIM
        pek = jnp.pad(cmp_pe_k[l], ((0, 0), (0, pad_d)))
        w1k = jnp.pad(cmp_w1_k[l].reshape(CMP_BLOCK, NSA_QK_DIM, CMP_HIDDEN),
                      ((0, 0), (0, pad_d), (0, 0))).astype(BF16)
        w2k = jnp.pad(cmp_w2_k[l], ((0, 0), (0, pad_d))).astype(BF16)
        gk = _pad_gain(nsa_kc_gain[l]).reshape(1, NSA_QK_PAD)
        w1v = cmp_w1_v[l].reshape(CMP_BLOCK, NSA_V_DIM, CMP_HIDDEN).astype(BF16)
        w2v = cmp_w2_v[l].astype(BF16)
        kc, vct = _compress(p3, pek, w1k, w2k, gk, cmp_pe_v[l], w1v, w2v, batch, seq)
        sel, ocmp = _nsa_select(pq, p3, kc, vct, ovt, batch, seq)
        o_b = _nsa_attend(pq, pk, pv, p3, sel, ocmp, batch, seq)

        merged = _merge(xn, o_a, o_b, wt, row["ga"], row["gb"], w_up_fox[l], w_up_nsa[l])
        hres, hn = _out_proj_norm(merged, w_out[l], xf, norm_ffn[l])

        act = _swiglu(hn, w_ffn_gate[l], w_ffn_up[l])
        xf = _matmul_residual(act, w_ffn_down[l], hres, FFN_DOWN_TM, FFN_DOWN_TN, "ffn_down")
    return xf.reshape(batch, seq, d)
```

```python
import functools

import numpy as np
import jax
import jax.numpy as jnp
from jax import lax
from jax.experimental import pallas as pl
from jax.experimental.pallas import tpu as pltpu

F32 = jnp.float32
BF16 = jnp.bfloat16

D_MODEL = 2048
FOX_HEADS = 8
FOX_HEAD_DIM = 128
FOX_WIDTH = FOX_HEADS * FOX_HEAD_DIM
NSA_HEADS = 8
NSA_KV_GROUPS = 2
NSA_HPG = NSA_HEADS // NSA_KV_GROUPS
NSA_QK_DIM = 192
NSA_QK_PAD = 256
NSA_V_DIM = 128
NSA_WIDTH = NSA_HEADS * NSA_V_DIM
CMP_BLOCK = 32
CMP_STRIDE = 16
CMP_HIDDEN = 256
SEL_BLOCK = 64
SEL_TOPK = 16
SEL_LOCAL = 2
FORCE_SCORE = 1.0e4
WINDOW = 512
KV_K = NSA_KV_GROUPS * NSA_QK_DIM
KV_V = NSA_KV_GROUPS * NSA_V_DIM
D_FF = -(-(8 * D_MODEL) // (3 * 256)) * 256
RMS_EPS = 1e-6
IN_SPLITS = (FOX_WIDTH, FOX_WIDTH, FOX_WIDTH, FOX_HEADS,
             NSA_HEADS * NSA_QK_DIM, KV_K, KV_V, KV_K, KV_V, KV_K, KV_V,
             3 * NSA_HEADS, D_MODEL, D_MODEL)

_NT = (((1,), (1,)), ((), ()))

LANES = 128
SUBLANES = 8
BF16_SUBLANES = 16
NEG = -1.0e30
LOG2E = 1.4426950408889634
SLC_TILE = 512
VMEM_LIMIT = 56 * 1024 * 1024

RMS_TM = 512
PROJ_TM = 1024
FOX_PROJ_TN = 1024
MERGE_TM, MERGE_TN = 512, 512
OUT_PROJ_TM = 512
SWIGLU_TM, SWIGLU_TN = 1024, 512
FFN_DOWN_TM, FFN_DOWN_TN = 512, 512
FOX_TQ = 512
FOX_HEADS_PER_STEP = 4
NSA_SELECT_TQ = 512
NSA_ATTEND_TQ = 256


def _cparams(sem):
    return pltpu.CompilerParams(dimension_semantics=sem, vmem_limit_bytes=VMEM_LIMIT)


def _rms_proj_kernel(x_ref, g_ref, w_ref, xn_ref, p_ref, wb_ref):
    @pl.when(pl.program_id(0) == 0)
    def _():
        _stage_weight(w_ref, wb_ref, True)

    rows = x_ref.shape[0] // EPILOGUE_SPLIT
    for r in range(EPILOGUE_SPLIT):
        sl = slice(r * rows, (r + 1) * rows)
        x = x_ref[sl, :]
        ms = jnp.mean(x * x, axis=-1, keepdims=True)
        xn = (x * lax.rsqrt(ms + RMS_EPS) * g_ref[...]).astype(BF16)
        xn_ref[sl, :] = xn
        p_ref[sl, :] = jnp.dot(xn, wb_ref[...], preferred_element_type=F32)


def _rmsnorm_project(x, gain, wt, tm=RMS_TM):
    m, d = x.shape
    n = wt.shape[0]
    return pl.pallas_call(
        _rms_proj_kernel,
        grid=(m // tm,),
        in_specs=[pl.BlockSpec((tm, d), lambda i: (i, 0)),
                  pl.BlockSpec((1, d), lambda i: (0, 0)),
                  pl.BlockSpec((n, d), lambda i: (0, 0), pipeline_mode=pl.Buffered(1))],
        out_specs=[pl.BlockSpec((tm, d), lambda i: (i, 0)),
                   pl.BlockSpec((tm, n), lambda i: (i, 0))],
        out_shape=[jax.ShapeDtypeStruct((m, d), BF16), jax.ShapeDtypeStruct((m, n), F32)],
        scratch_shapes=[pltpu.VMEM((d, n), BF16)],
        compiler_params=_cparams(("arbitrary",)),
        name="rmsnorm_proj_f32",
    )(x, gain.reshape(1, d), wt)


STAGE_CHUNK = 512
EPILOGUE_SPLIT = 4


def _stage_weight(w_ref, wb_ref, transposed=False, pad_from=0, pad_to=0):
    if not transposed:
        wb_ref[...] = w_ref[...].astype(BF16)
        return
    n_in, k = w_ref.shape
    for c in range(k // STAGE_CHUNK):
        cols = slice(c * STAGE_CHUNK, (c + 1) * STAGE_CHUNK)
        w = w_ref[:, cols]
        if pad_from != pad_to:
            zero = jnp.zeros((pad_to - pad_from, STAGE_CHUNK), F32)
            w = jnp.concatenate(
                [piece for h in range(n_in // pad_from)
                 for piece in (w[h * pad_from:(h + 1) * pad_from, :], zero)], axis=0)
        wb_ref[cols, :] = w.T.astype(BF16)


def _first_m_step():
    return pl.program_id(1) == 0


AUG_LANE = NSA_QK_DIM - LANES


def _key_aug(pos):
    lane = lax.broadcasted_iota(jnp.int32, pos.shape, 1)
    hi = (pos >> 8).astype(F32)
    lo = (pos & 255).astype(F32)
    return jnp.where(lane < AUG_LANE, 0.0,
                     jnp.where(lane < AUG_LANE + 3, hi,
                               jnp.where(lane < AUG_LANE + 6, lo,
                                         jnp.where(lane < AUG_LANE + 9, 1.0, 0.0))))


def _proj_kernel(a_ref, w_ref, gain_ref, flag_ref, o_ref, wb_ref, *, pad_from, pad_to,
                 group, count, aug, seq):
    @pl.when(_first_m_step())
    def _():
        _stage_weight(w_ref, wb_ref, True, pad_from, pad_to)

    if group == 0:
        o_ref[...] = jnp.dot(a_ref[...], wb_ref[...],
                             preferred_element_type=F32).astype(o_ref.dtype)
        return
    tm = a_ref.shape[0]
    rows = tm // EPILOGUE_SPLIT
    ys = [jnp.dot(a_ref[r * rows:(r + 1) * rows, :], wb_ref[...], preferred_element_type=F32)
          for r in range(EPILOGUE_SPLIT)]
    for r, y in enumerate(ys):
        if aug:
            pos = ((pl.program_id(1) * tm) % seq + r * rows
                   + lax.broadcasted_iota(jnp.int32, (rows, LANES), 0))
        for c in range(y.shape[1] // group):
            sl = slice(c * group, (c + 1) * group)
            yc = y[:, sl]
            ss = jnp.sum(yc * yc, axis=-1, keepdims=True)
            rs = lax.rsqrt(ss * (1.0 / count) + RMS_EPS)
            scale = jnp.where(flag_ref[:, sl] > 0.0, rs, 1.0)
            out = yc * scale * gain_ref[:, sl]
            rsl = slice(r * rows, (r + 1) * rows)
            if not aug:
                o_ref[rsl, sl] = out.astype(o_ref.dtype)
                continue
            up = slice((c + 1) * group - LANES, (c + 1) * group)
            o_ref[rsl, c * group:(c + 1) * group - LANES] = out[:, :group - LANES].astype(o_ref.dtype)
            o_ref[rsl, up] = (out[:, group - LANES:] + _key_aug(pos)).astype(o_ref.dtype)


def _row_window(starts, rows, k):
    def index(j, i):
        start = starts[-1]
        for t in range(len(starts) - 2, -1, -1):
            start = jnp.where(j == t, starts[t], start)
        return pl.multiple_of(start, SUBLANES), 0
    assert all(s % SUBLANES == 0 for s in starts)
    return pl.BlockSpec((pl.Element(rows), pl.Element(k)), index)


def _project(a, wt, starts, tn_in, gain, flag, out_dtype, tm, name,
             pad_from=0, pad_to=0, group=0, count=1, aug="", seq=0):
    m, k = a.shape
    n_tiles = len(starts)
    tn_out = tn_in if pad_from == pad_to else tn_in // pad_from * pad_to
    n_out = n_tiles * tn_out
    if gain is None:
        gain = jnp.ones((n_out,), F32)
        flag = jnp.zeros((n_out,), F32)
    assert not aug or seq % tm == 0
    operands = [a, wt, gain.reshape(1, n_out), flag.reshape(1, n_out)]
    in_specs = [pl.BlockSpec((tm, k), lambda j, i: (i, 0)),
                _row_window(starts, tn_in, k),
                pl.BlockSpec((1, tn_out), lambda j, i: (0, j)),
                pl.BlockSpec((1, tn_out), lambda j, i: (0, j))]
    return pl.pallas_call(
        functools.partial(_proj_kernel, pad_from=pad_from, pad_to=pad_to, group=group,
                          count=count, aug=aug, seq=seq),
        grid=(n_tiles, m // tm),
        in_specs=in_specs,
        out_specs=pl.BlockSpec((tm, tn_out), lambda j, i: (i, j)),
        out_shape=jax.ShapeDtypeStruct((m, n_out), out_dtype),
        scratch_shapes=[pltpu.VMEM((k, tn_out), BF16)],
        compiler_params=_cparams(("arbitrary", "arbitrary")),
        name=name,
    )(*operands)


def _proj_qt_kernel(a_ref, w_ref, tab_ref, o_ref, wb_ref, *, head, head_pad, seq):
    @pl.when(_first_m_step())
    def _():
        wb_ref[...] = w_ref[...].astype(BF16)

    tm = a_ref.shape[0]
    cols = tm // EPILOGUE_SPLIT
    reps = cols // LANES
    spare = head_pad - head
    ys = [lax.dot_general(wb_ref[...], a_ref[r * cols:(r + 1) * cols, :], _NT,
                          preferred_element_type=F32) for r in range(EPILOGUE_SPLIT)]
    row = lax.broadcasted_iota(jnp.int32, (spare, cols), 0)
    for r, y in enumerate(ys):
        lanes = slice(r * cols, (r + 1) * cols)
        pos = ((pl.program_id(1) * tm) % seq + r * cols
               + lax.broadcasted_iota(jnp.int32, (1, cols), 1)).astype(F32)
        for h in range(y.shape[0] // head):
            yh = y[h * head:(h + 1) * head, :]
            rs = lax.rsqrt(jnp.sum(yh * yh, axis=0, keepdims=True) * (1.0 / head) + RMS_EPS)
            slot = h * head_pad
            gain = jnp.concatenate([tab_ref[slot:slot + head, :]] * reps, axis=1)
            o_ref[slot:slot + head, lanes] = (yh * rs * gain).astype(o_ref.dtype)
            tab = jnp.concatenate([tab_ref[slot + head:slot + head_pad, :]] * reps, axis=1)
            u = pos * tab
            hi = u.astype(BF16).astype(F32)
            r1 = u - hi
            mid = r1.astype(BF16).astype(F32)
            extra = jnp.where(row < 6, tab,
                              jnp.where(row == 6, hi,
                                        jnp.where(row == 7, mid,
                                                  jnp.where(row == 8, r1 - mid, 0.0))))
            o_ref[slot + head:slot + head_pad, lanes] = extra.astype(o_ref.dtype)


def _project_qt(a, wt, starts, tn_in, table, tm, seq, name):
    m, k = a.shape
    tn_out = tn_in // NSA_QK_DIM * NSA_QK_PAD
    assert seq % tm == 0
    return pl.pallas_call(
        functools.partial(_proj_qt_kernel, head=NSA_QK_DIM, head_pad=NSA_QK_PAD, seq=seq),
        grid=(len(starts), m // tm),
        in_specs=[pl.BlockSpec((tm, k), lambda j, i: (i, 0)),
                  _row_window(starts, tn_in, k),
                  pl.BlockSpec((tn_out, LANES), lambda j, i: (j, 0))],
        out_specs=pl.BlockSpec((tn_out, tm), lambda j, i: (j, i)),
        out_shape=jax.ShapeDtypeStruct((len(starts) * tn_out, m), BF16),
        scratch_shapes=[pltpu.VMEM((tn_in, k), BF16)],
        compiler_params=_cparams(("arbitrary", "arbitrary")),
        name=name,
    )(a, wt, table)


def _out_norm_kernel(a_ref, w_ref, r_ref, g_ref, h_ref, hn_ref, wb_ref):
    @pl.when(pl.program_id(0) == 0)
    def _():
        _stage_weight(w_ref, wb_ref)

    h = r_ref[...] + jnp.dot(a_ref[...], wb_ref[...], preferred_element_type=F32)
    h_ref[...] = h
    ms = jnp.mean(h * h, axis=-1, keepdims=True)
    hn_ref[...] = (h * lax.rsqrt(ms + RMS_EPS) * g_ref[...]).astype(hn_ref.dtype)


def _out_proj_norm(a, w, res, gain, tm=OUT_PROJ_TM):
    m, k = a.shape
    n = w.shape[1]
    full = lambda i: (0, 0)
    row = lambda i: (i, 0)
    return pl.pallas_call(
        _out_norm_kernel,
        grid=(m // tm,),
        in_specs=[pl.BlockSpec((tm, k), row),
                  pl.BlockSpec((k, n), full, pipeline_mode=pl.Buffered(1)),
                  pl.BlockSpec((tm, n), row), pl.BlockSpec((1, n), full)],
        out_specs=[pl.BlockSpec((tm, n), row), pl.BlockSpec((tm, n), row)],
        out_shape=[jax.ShapeDtypeStruct((m, n), F32), jax.ShapeDtypeStruct((m, n), BF16)],
        scratch_shapes=[pltpu.VMEM((k, n), BF16)],
        compiler_params=_cparams(("arbitrary",)),
        name="out_proj_norm",
    )(a, w, res, gain.reshape(1, n))


def _mm_res_kernel(a_ref, w_ref, r_ref, o_ref, wb_ref):
    @pl.when(_first_m_step())
    def _():
        _stage_weight(w_ref, wb_ref)

    o_ref[...] = r_ref[...] + jnp.dot(a_ref[...], wb_ref[...], preferred_element_type=F32)


def _matmul_residual(a, w, res, tm, tn, name):
    m, k = a.shape
    n = w.shape[1]
    return pl.pallas_call(
        _mm_res_kernel,
        grid=(n // tn, m // tm),
        in_specs=[pl.BlockSpec((tm, k), lambda j, i: (i, 0)),
                  pl.BlockSpec((k, tn), lambda j, i: (0, j)),
                  pl.BlockSpec((tm, tn), lambda j, i: (i, j))],
        out_specs=pl.BlockSpec((tm, tn), lambda j, i: (i, j)),
        out_shape=jax.ShapeDtypeStruct((m, n), F32),
        scratch_shapes=[pltpu.VMEM((k, tn), BF16)],
        compiler_params=_cparams(("arbitrary", "arbitrary")),
        name=name,
    )(a, w, res)


def _merge_kernel(xn_ref, oa_ref, ob_ref, wga_ref, wgb_ref, wuf_ref, wun_ref, o_ref,
                  bga_ref, bgb_ref, buf_ref, bun_ref):
    @pl.when(_first_m_step())
    def _():
        _stage_weight(wga_ref, bga_ref, True)
        _stage_weight(wgb_ref, bgb_ref, True)
        _stage_weight(wuf_ref, buf_ref)
        _stage_weight(wun_ref, bun_ref)

    xn = xn_ref[...]
    ga = jax.nn.sigmoid(jnp.dot(xn, bga_ref[...], preferred_element_type=F32))
    ua = jnp.dot(oa_ref[...], buf_ref[...], preferred_element_type=F32)
    acc = ga * ua
    gb = jax.nn.sigmoid(jnp.dot(xn, bgb_ref[...], preferred_element_type=F32))
    ub = jnp.dot(ob_ref[...], bun_ref[...], preferred_element_type=F32)
    o_ref[...] = (acc + gb * ub).astype(o_ref.dtype)


def _merge(xn, oa, ob, wt, row_a, row_b, wuf, wun, tm=MERGE_TM, tn=MERGE_TN):
    m, d = xn.shape
    n = wuf.shape[1]
    ka = oa.shape[1]
    kb = ob.shape[1]
    row = lambda j, i: (i, 0)
    col = lambda j, i: (0, j)
    return pl.pallas_call(
        _merge_kernel,
        grid=(n // tn, m // tm),
        in_specs=[pl.BlockSpec((tm, d), row), pl.BlockSpec((tm, ka), row),
                  pl.BlockSpec((tm, kb), row),
                  _row_window([row_a + t * tn for t in range(n // tn)], tn, d),
                  _row_window([row_b + t * tn for t in range(n // tn)], tn, d),
                  pl.BlockSpec((ka, tn), col), pl.BlockSpec((kb, tn), col)],
        out_specs=pl.BlockSpec((tm, tn), lambda j, i: (i, j)),
        out_shape=jax.ShapeDtypeStruct((m, n), BF16),
        scratch_shapes=[pltpu.VMEM((d, tn), BF16), pltpu.VMEM((d, tn), BF16),
                        pltpu.VMEM((ka, tn), BF16), pltpu.VMEM((kb, tn), BF16)],
        compiler_params=_cparams(("arbitrary", "arbitrary")),
        name="gated_merge",
    )(xn, oa, ob, wt, wt, wuf, wun)


def _swiglu_kernel(a_ref, wg_ref, wu_ref, o_ref, bg_ref, bu_ref):
    @pl.when(_first_m_step())
    def _():
        _stage_weight(wg_ref, bg_ref)
        _stage_weight(wu_ref, bu_ref)

    a = a_ref[...]
    gt = jnp.dot(a, bg_ref[...], preferred_element_type=F32)
    up = jnp.dot(a, bu_ref[...], preferred_element_type=F32)
    o_ref[...] = (gt * jax.nn.sigmoid(gt) * up).astype(o_ref.dtype)


def _swiglu(a, wg, wu, tm=SWIGLU_TM, tn=SWIGLU_TN):
    m, k = a.shape
    n = wg.shape[1]
    return pl.pallas_call(
        _swiglu_kernel,
        grid=(n // tn, m // tm),
        in_specs=[pl.BlockSpec((tm, k), lambda j, i: (i, 0)),
                  pl.BlockSpec((k, tn), lambda j, i: (0, j)),
                  pl.BlockSpec((k, tn), lambda j, i: (0, j))],
        out_specs=pl.BlockSpec((tm, tn), lambda j, i: (i, j)),
        out_shape=jax.ShapeDtypeStruct((m, n), BF16),
        scratch_shapes=[pltpu.VMEM((k, tn), BF16), pltpu.VMEM((k, tn), BF16)],
        compiler_params=_cparams(("arbitrary", "arbitrary")),
        name="swiglu_up",
    )(a, wg, wu)


def _split3(x):
    hi = x.astype(BF16)
    r1 = x - hi.astype(F32)
    mid = r1.astype(BF16)
    lo = (r1 - mid.astype(F32)).astype(BF16)
    return hi, mid, lo


def _decay_kernel(z_ref, b_ref, ccol_ref, crow_ref, *, blk):
    t = z_ref.shape[0]
    r = lax.broadcasted_iota(jnp.int32, (blk, blk), 0)
    c = lax.broadcasted_iota(jnp.int32, (blk, blk), 1)
    tri = jnp.where(r >= c, 1.0, 0.0).astype(BF16)
    carry = jnp.zeros((1, LANES), F32)
    for s in range(t // blk):
        rows = slice(s * blk, (s + 1) * blk)
        z = z_ref[rows, :] + b_ref[...]
        logf = (jnp.minimum(z, 0.0) - jnp.log1p(jnp.exp(-jnp.abs(z)))) * LOG2E
        hi, mid, lo = _split3(logf)
        cb = (jnp.dot(tri, hi, preferred_element_type=F32)
              + jnp.dot(tri, mid, preferred_element_type=F32)
              + jnp.dot(tri, lo, preferred_element_type=F32)) + carry
        carry = cb[blk - 1:blk, :]
        ccol_ref[rows, :] = cb
        crow_ref[0, :, rows] = cb.T[:FOX_HEADS, :]


def _decay(p3, bias_row, batch, seq, blk=256):
    return pl.pallas_call(
        functools.partial(_decay_kernel, blk=blk),
        grid=(batch,),
        in_specs=[pl.BlockSpec((seq, LANES), lambda b: (b, 6)),
                  pl.BlockSpec((1, LANES), lambda b: (0, 0))],
        out_specs=[pl.BlockSpec((seq, LANES), lambda b: (b, 0)),
                   pl.BlockSpec((1, FOX_HEADS, seq), lambda b: (b, 0, 0))],
        out_shape=[jax.ShapeDtypeStruct((batch * seq, LANES), F32),
                   jax.ShapeDtypeStruct((batch, FOX_HEADS, seq), F32)],
        compiler_params=_cparams(("arbitrary",)),
        name="fox_decay_cumsum",
    )(p3, bias_row)


def _transpose_bf16(x):
    return x.astype(F32).T.astype(BF16)


def _normalised(acc, d):
    return acc[:d, :] * (1.0 / jnp.maximum(acc[d:d + 1, :], 1e-30))


def _fox_kernel(q_ref, k_ref, v_ref, ccol_ref, crow_ref, o_ref, vt_ref, ka_ref, *, tq, nq, nh):
    hh = pl.program_id(1)
    i = pl.program_id(2)
    dh = FOX_HEAD_DIM

    def pieces(x, lane, first):
        hi = x.astype(BF16).astype(F32)
        r1 = x - hi
        mid = r1.astype(BF16).astype(F32)
        return jnp.where(lane == first, hi,
                         jnp.where(lane == first + 1, mid,
                                   jnp.where(lane == first + 2, r1 - mid, 0.0)))

    @pl.when(i == 0)
    def _():
        cc = ccol_ref[...]
        lane = lax.broadcasted_iota(jnp.int32, cc.shape, 1)
        for h in range(nh):
            vt_ref[h, 0:dh, :] = _transpose_bf16(v_ref[:, h * dh:(h + 1) * dh])
            vt_ref[h, dh:, :] = jnp.ones((vt_ref.shape[1] - dh, vt_ref.shape[2]), BF16)
            cj = jnp.sum(jnp.where(lane == hh * nh + h, cc, 0.0), axis=-1, keepdims=True)
            extra = pieces(-cj, lane, 0) + jnp.where(lane < 3, 0.0, jnp.where(lane < 6, 1.0, 0.0))
            ka_ref[h, :, 0:dh] = k_ref[:, h * dh:(h + 1) * dh]
            ka_ref[h, :, dh:] = extra.astype(BF16)

    row = lax.broadcasted_iota(jnp.int32, (dh, tq), 0)
    qts = []
    for h in range(nh):
        ci = crow_ref[0, h]
        extra = pieces(ci, row, 3) + jnp.where(row < 3, 1.0, 0.0)
        qts.append(jnp.concatenate(
            [_transpose_bf16(q_ref[:, h * dh:(h + 1) * dh]), extra.astype(BF16)], axis=0))
    rk = lax.broadcasted_iota(jnp.int32, (tq, tq), 0)
    cq = lax.broadcasted_iota(jnp.int32, (tq, tq), 1)

    def scores(h, k0):
        return jnp.dot(ka_ref[h, k0:k0 + tq, :], qts[h], preferred_element_type=F32)

    def variant(n):
        def run():
            starts = [t * tq for t in range(n, -1, -1)]
            tiles = [[jnp.where(rk <= cq, scores(h, starts[0]), NEG)]
                     + [scores(h, k0) for k0 in starts[1:]] for h in range(nh)]
            ms = [jnp.max(tiles[h][0], axis=0, keepdims=True) for h in range(nh)]
            accs = [jnp.dot(vt_ref[h, :, starts[0]:starts[0] + tq],
                            jnp.exp2(tiles[h][0] - ms[h]).astype(BF16),
                            preferred_element_type=F32) for h in range(nh)]
            for t in range(1, n + 1):
                k0 = starts[t]
                for h in range(nh):
                    st = tiles[h][t]
                    m_new = jnp.maximum(ms[h], jnp.max(st, axis=0, keepdims=True))
                    accs[h] = jnp.exp2(ms[h] - m_new) * accs[h] + jnp.dot(
                        vt_ref[h, :, k0:k0 + tq], jnp.exp2(st - m_new).astype(BF16),
                        preferred_element_type=F32)
                    ms[h] = m_new
            for h in range(nh):
                o_ref[:, h * dh:(h + 1) * dh] = _normalised(accs[h], dh).T.astype(o_ref.dtype)
        return run

    lax.switch(i, [variant(n) for n in range(nq)])


def _fox_attention(p1, ccol, crow4, batch, seq, tq=FOX_TQ, nh=FOX_HEADS_PER_STEP):
    nq = seq // tq
    hg = FOX_HEADS // nh
    w = nh * FOX_HEAD_DIM
    return pl.pallas_call(
        functools.partial(_fox_kernel, tq=tq, nq=nq, nh=nh),
        grid=(batch, hg, nq),
        in_specs=[pl.BlockSpec((tq, w), lambda b, hh, i: (b * nq + i, hh)),
                  pl.BlockSpec((seq, w), lambda b, hh, i: (b, hg + hh)),
                  pl.BlockSpec((seq, w), lambda b, hh, i: (b, 2 * hg + hh)),
                  pl.BlockSpec((seq, LANES), lambda b, hh, i: (b, 0)),
                  pl.BlockSpec((1, nh, 1, tq), lambda b, hh, i: (b, hh, 0, i))],
        out_specs=pl.BlockSpec((tq, w), lambda b, hh, i: (b * nq + i, hh)),
        out_shape=jax.ShapeDtypeStruct((batch * seq, FOX_WIDTH), BF16),
        scratch_shapes=[pltpu.VMEM((nh, FOX_HEAD_DIM + BF16_SUBLANES, seq), BF16),
                        pltpu.VMEM((nh, seq, 2 * FOX_HEAD_DIM), BF16)],
        compiler_params=_cparams(("arbitrary", "arbitrary", "arbitrary")),
        name="fox_attention",
    )(p1, p1, p1, ccol, crow4)


def _compress_one(z_refs, pe_ref, w1_ref, w2_ref, nblk):
    half = CMP_BLOCK // 2
    first = jnp.zeros((nblk, CMP_HIDDEN), F32)
    second = jnp.zeros((nblk, CMP_HIDDEN), F32)
    for p in range(half):
        rows = pl.ds(p, nblk, stride=CMP_STRIDE)
        zp = [z_ref[rows, :] for z_ref in z_refs]
        zp = zp[0] if len(zp) == 1 else jnp.concatenate(zp, axis=1)
        first += jnp.dot((zp + pe_ref[p:p + 1, :]).astype(BF16), w1_ref[p],
                         preferred_element_type=F32)
        second += jnp.dot((zp + pe_ref[half + p:half + p + 1, :]).astype(BF16),
                          w1_ref[half + p], preferred_element_type=F32)
    hid = first + pltpu.roll(second, nblk - 1, 0)
    act = (hid * jax.nn.sigmoid(hid)).astype(BF16)
    return jnp.dot(act, w2_ref[...], preferred_element_type=F32)


def _compress_kernel(zk0_ref, zk1_ref, zv_ref, pek_ref, w1k_ref, w2k_ref, gk_ref,
                     pev_ref, w1v_ref, w2v_ref, kc_ref, vc_ref, *, nblk):
    kc = _compress_one((zk0_ref, zk1_ref), pek_ref, w1k_ref, w2k_ref, nblk)
    ms = jnp.sum(kc * kc, axis=-1, keepdims=True) * (1.0 / NSA_QK_DIM)
    kc = kc * lax.rsqrt(ms + RMS_EPS) * gk_ref[...]
    pos = CMP_STRIDE * lax.broadcasted_iota(jnp.int32, (nblk, LANES), 0) + (CMP_BLOCK - 1)
    up = NSA_QK_PAD - LANES
    kc_ref[:, :up] = kc[:, :up].astype(kc_ref.dtype)
    kc_ref[:, up:] = (kc[:, up:] + _key_aug(pos)).astype(kc_ref.dtype)
    vc = _compress_one((zv_ref,), pev_ref, w1v_ref, w2v_ref, nblk)
    vc_ref[...] = vc.T.astype(vc_ref.dtype)


def _compress(p3, pek, w1k, w2k, gk, pev, w1v, w2v, batch, seq):
    g = NSA_KV_GROUPS
    nblk = seq // CMP_STRIDE
    full2 = lambda b, gg: (0, 0)
    full3 = lambda b, gg: (0, 0, 0)
    return pl.pallas_call(
        functools.partial(_compress_kernel, nblk=nblk),
        grid=(batch, g),
        in_specs=[pl.BlockSpec((seq, LANES), lambda b, gg: (b, 2 * gg)),
                  pl.BlockSpec((seq, LANES), lambda b, gg: (b, 2 * gg + 1)),
                  pl.BlockSpec((seq, NSA_V_DIM), lambda b, gg: (b, 4 + gg)),
                  pl.BlockSpec(pek.shape, full2), pl.BlockSpec(w1k.shape, full3),
                  pl.BlockSpec(w2k.shape, full2), pl.BlockSpec(gk.shape, full2),
                  pl.BlockSpec(pev.shape, full2), pl.BlockSpec(w1v.shape, full3),
                  pl.BlockSpec(w2v.shape, full2)],
        out_specs=[pl.BlockSpec((nblk, NSA_QK_PAD), lambda b, gg: (b * g + gg, 0)),
                   pl.BlockSpec((NSA_V_DIM, nblk), lambda b, gg: (b * g + gg, 0))],
        out_shape=[jax.ShapeDtypeStruct((batch * g * nblk, NSA_QK_PAD), BF16),
                   jax.ShapeDtypeStruct((batch * g * NSA_V_DIM, nblk), BF16)],
        compiler_params=_cparams(("arbitrary", "arbitrary")),
        name="nsa_compress",
    )(p3, p3, p3, pek, w1k, w2k, gk, pev, w1v, w2v)


def _q_heads_t(qt_ref, first_head):
    return jnp.concatenate(
        [qt_ref[(first_head + hh) * NSA_QK_PAD:(first_head + hh + 1) * NSA_QK_PAD, :]
         for hh in range(NSA_HPG)], axis=1)


def _gate_rows(glt_ref, g, hh):
    base = FOX_HEADS + (g * NSA_HPG + hh) * 3
    return [glt_ref[pl.ds(base + br, 1), :] for br in range(3)]


def _nsa_select_kernel(q_ref, kc_ref, vct_ref, gl_ref, ovt_ref,
                       sel_ref, ocmp_ref, glt_ref, *, tq, n_cmp, n_sel):
    g = pl.program_id(1)
    t0 = pl.program_id(2) * tq
    hpg = NSA_HPG
    dv = NSA_V_DIM
    q4t = _q_heads_t(q_ref, 0)
    rk = lax.broadcasted_iota(jnp.int32, (LANES, tq), 0)
    cq = lax.broadcasted_iota(jnp.int32, (LANES, tq), 1)

    s_c = jnp.dot(kc_ref[...], q4t, preferred_element_type=F32)
    dist_c = (t0 + cq) - (CMP_STRIDE * rk + (CMP_BLOCK - 1))
    mask_c = jnp.where(rk < n_cmp, dist_c, -1) >= 0
    probs = []
    p_sum = jnp.zeros((LANES, tq), F32)
    for hh in range(hpg):
        sm = jnp.where(mask_c, s_c[:, hh * tq:(hh + 1) * tq], NEG)
        m = jnp.max(sm, axis=0, keepdims=True)
        e = jnp.where(mask_c, jnp.exp2(sm - m), 0.0)
        p = e * (1.0 / jnp.maximum(jnp.sum(e, axis=0, keepdims=True), 1e-30))
        probs.append(p)
        p_sum = p_sum + p
    o_cmp = jnp.dot(vct_ref[...], jnp.concatenate(probs, axis=1).astype(BF16),
                    preferred_element_type=F32)

    ph = p_sum.astype(BF16)
    plo = (p_sum - ph.astype(F32)).astype(BF16)
    ovt = ovt_ref[...]
    imp = (jnp.dot(ovt, ph, preferred_element_type=F32)
           + jnp.dot(ovt, plo, preferred_element_type=F32))[:n_sel, :]

    rj = lax.broadcasted_iota(jnp.int32, (n_sel, tq), 0)
    tcol = t0 + lax.broadcasted_iota(jnp.int32, (n_sel, tq), 1)
    back = (tcol >> (SEL_BLOCK.bit_length() - 1)) - rj
    elig = back >= 0
    forced = jnp.where(rj == 0, 0, jnp.where(elig, back, SEL_LOCAL)) < SEL_LOCAL
    score = jnp.where(elig, jnp.where(forced, FORCE_SCORE, imp), -1.0)
    rank = jnp.zeros((n_sel, tq), F32)
    for jp in range(n_sel):
        row = score[jp:jp + 1, :]
        later = jnp.where(rj > jp, 1.0, 0.0)
        rank = rank + jnp.where(row > score, 1.0, jnp.where(row == score, later, 0.0))
    sel_ref[0:n_sel, :] = jnp.where(elig, jnp.where(rank < SEL_TOPK, 0.0, NEG), NEG)
    sel_ref[n_sel:, :] = jnp.full((LANES - n_sel, tq), NEG, F32)

    glt_ref[...] = jax.nn.sigmoid(gl_ref[...]).T
    for hh in range(hpg):
        gate = _gate_rows(glt_ref, g, hh)[0]
        ocmp_ref[:, hh * dv:(hh + 1) * dv] = (gate * o_cmp[:, hh * tq:(hh + 1) * tq]).T


def _nsa_select(pq, p3, kc, vct, ovt, batch, seq, tq=NSA_SELECT_TQ):
    g = NSA_KV_GROUPS
    nq = seq // tq
    nblk = seq // CMP_STRIDE
    n_cmp = nblk - CMP_BLOCK // CMP_STRIDE + 1
    return pl.pallas_call(
        functools.partial(_nsa_select_kernel, tq=tq, n_cmp=n_cmp, n_sel=seq // SEL_BLOCK),
        grid=(batch, g, nq),
        in_specs=[
            pl.BlockSpec((NSA_HPG * NSA_QK_PAD, tq), lambda b, gg, i: (gg, b * nq + i)),
            pl.BlockSpec((nblk, NSA_QK_PAD), lambda b, gg, i: (b * g + gg, 0)),
            pl.BlockSpec((NSA_V_DIM, nblk), lambda b, gg, i: (b * g + gg, 0)),
            pl.BlockSpec((tq, LANES), lambda b, gg, i: (b * nq + i, 6)),
            pl.BlockSpec(ovt.shape, lambda b, gg, i: (0, 0)),
        ],
        out_specs=[pl.BlockSpec((LANES, tq), lambda b, gg, i: (b * g + gg, i)),
                   pl.BlockSpec((tq, NSA_HPG * NSA_V_DIM), lambda b, gg, i: (b * nq + i, gg))],
        out_shape=[jax.ShapeDtypeStruct((batch * g * LANES, seq), F32),
                   jax.ShapeDtypeStruct((batch * seq, NSA_WIDTH), F32)],
        scratch_shapes=[pltpu.VMEM((LANES, tq), F32)],
        compiler_params=_cparams(("arbitrary", "arbitrary", "arbitrary")),
        name="nsa_select",
    )(pq, kc, vct, p3, ovt)


def _nsa_attend_kernel(q_ref, ks_ref, vs_ref, kw_ref, vw_ref, gl_ref, sel_ref,
                       ocmp_ref, o_ref, vst_ref, vwt_ref, glt_ref, *, tq, n_var):
    i = pl.program_id(1)
    t0 = i * tq
    hpg = NSA_HPG
    ng = NSA_KV_GROUPS
    dv = NSA_V_DIM
    dk = NSA_QK_PAD

    @pl.when(i == 0)
    def _():
        ones = jnp.ones((vst_ref.shape[1] - dv, vst_ref.shape[2]), BF16)
        for gg in range(ng):
            vst_ref[gg, 0:dv, :] = _transpose_bf16(vs_ref[:, gg * dv:(gg + 1) * dv])
            vst_ref[gg, dv:, :] = ones
            vwt_ref[gg, 0:dv, :] = _transpose_bf16(vw_ref[:, gg * dv:(gg + 1) * dv])
            vwt_ref[gg, dv:, :] = ones

    q4t = [_q_heads_t(q_ref, gg * hpg) for gg in range(ng)]
    glt_ref[...] = jax.nn.sigmoid(gl_ref[...]).T

    def distance(k0, rows):
        return (lax.broadcasted_iota(jnp.int32, (rows, tq), 1)
                - lax.broadcasted_iota(jnp.int32, (rows, tq), 0)) + (t0 - k0)

    def scores(gg, k, mask_bias):
        st = jnp.dot(k, q4t[gg], preferred_element_type=F32)
        return st + jnp.concatenate([mask_bias] * hpg, axis=1)

    def selection_bias(gg, k0):
        j0 = gg * LANES + k0 // SEL_BLOCK
        return jnp.concatenate(
            [jnp.broadcast_to(sel_ref[j0 + j:j0 + j + 1, :], (SEL_BLOCK, tq))
             for j in range(SLC_TILE // SEL_BLOCK)], axis=0)

    def variant(n):
        def run():
            kw0 = pl.multiple_of(jnp.maximum(t0 - WINDOW, 0), LANES)
            wrows = WINDOW + tq
            dist_w = distance(kw0, wrows)
            bias_w = jnp.where(jnp.where(dist_w >= 0, dist_w, WINDOW) < WINDOW, 0.0, NEG)
            o_win = []
            for gg in range(ng):
                sw = scores(gg, kw_ref[pl.ds(kw0, wrows), gg * dk:(gg + 1) * dk], bias_w)
                pw = jnp.exp2(sw - jnp.max(sw, axis=0, keepdims=True))
                o_win.append(_normalised(
                    jnp.dot(vwt_ref[gg, :, pl.ds(kw0, wrows)], pw.astype(BF16),
                            preferred_element_type=F32), dv))

            starts = [t * SLC_TILE for t in range(n, -1, -1)]
            causal = distance(starts[0], SLC_TILE) >= 0
            tiles = [[] for _ in range(ng)]
            for t, k0 in enumerate(starts):
                for gg in range(ng):
                    bias = selection_bias(gg, k0)
                    if t == 0:
                        bias = jnp.where(causal, bias, NEG)
                    tiles[gg].append(scores(
                        gg, ks_ref[k0:k0 + SLC_TILE, gg * dk:(gg + 1) * dk], bias))
            ms = [jnp.max(tiles[gg][0], axis=0, keepdims=True) for gg in range(ng)]
            accs = [jnp.dot(vst_ref[gg, :, starts[0]:starts[0] + SLC_TILE],
                            jnp.exp2(tiles[gg][0] - ms[gg]).astype(BF16),
                            preferred_element_type=F32) for gg in range(ng)]
            for t in range(1, n + 1):
                k0 = starts[t]
                for gg in range(ng):
                    st = tiles[gg][t]
                    m_new = jnp.maximum(ms[gg], jnp.max(st, axis=0, keepdims=True))
                    accs[gg] = jnp.exp2(ms[gg] - m_new) * accs[gg] + jnp.dot(
                        vst_ref[gg, :, k0:k0 + SLC_TILE], jnp.exp2(st - m_new).astype(BF16),
                        preferred_element_type=F32)
                    ms[gg] = m_new

            for gg in range(ng):
                o_slc = _normalised(accs[gg], dv)
                for hh in range(hpg):
                    _, g_slc, g_win = _gate_rows(glt_ref, gg, hh)
                    lanes = slice(hh * tq, (hh + 1) * tq)
                    out = g_slc * o_slc[:, lanes] + g_win * o_win[gg][:, lanes]
                    cols = slice((gg * hpg + hh) * dv, (gg * hpg + hh + 1) * dv)
                    o_ref[:, cols] = (ocmp_ref[:, cols] + out.T).astype(o_ref.dtype)
        return run

    lax.switch(t0 // SLC_TILE, [variant(n) for n in range(n_var)])


def _nsa_attend(pq, pk, pv, p3, sel, ocmp, batch, seq):
    tq = NSA_ATTEND_TQ
    g = NSA_KV_GROUPS
    nq = seq // tq
    vrows = NSA_V_DIM + BF16_SUBLANES
    return pl.pallas_call(
        functools.partial(_nsa_attend_kernel, tq=tq, n_var=seq // SLC_TILE),
        grid=(batch, nq),
        in_specs=[
            pl.BlockSpec((NSA_HEADS * NSA_QK_PAD, tq), lambda b, i: (0, b * nq + i)),
            pl.BlockSpec((seq, g * NSA_QK_PAD), lambda b, i: (b, 0)),
            pl.BlockSpec((seq, g * NSA_V_DIM), lambda b, i: (b, 0)),
            pl.BlockSpec((seq, g * NSA_QK_PAD), lambda b, i: (b, 1)),
            pl.BlockSpec((seq, g * NSA_V_DIM), lambda b, i: (b, 1)),
            pl.BlockSpec((tq, LANES), lambda b, i: (b * nq + i, 6)),
            pl.BlockSpec((g * LANES, tq), lambda b, i: (b, i)),
            pl.BlockSpec((tq, NSA_WIDTH), lambda b, i: (b * nq + i, 0)),
        ],
        out_specs=pl.BlockSpec((tq, NSA_WIDTH), lambda b, i: (b * nq + i, 0)),
        out_shape=jax.ShapeDtypeStruct((batch * seq, NSA_WIDTH), BF16),
        scratch_shapes=[pltpu.VMEM((g, vrows, seq), BF16), pltpu.VMEM((g, vrows, seq), BF16),
                        pltpu.VMEM((LANES, tq), F32)],
        compiler_params=_cparams(("arbitrary", "arbitrary")),
        name="nsa_attend",
    )(pq, pk, pv, pk, pv, p3, sel, ocmp)


def _pad_head_rows(wt, heads):
    k = wt.shape[1]
    wt = wt.reshape(heads, NSA_QK_DIM, k)
    wt = jnp.pad(wt, ((0, 0), (0, NSA_QK_PAD - NSA_QK_DIM), (0, 0)))
    return wt.reshape(heads * NSA_QK_PAD, k)


def _pad_gain(gain, scale=1.0):
    return jnp.pad(gain * scale, (0, NSA_QK_PAD - NSA_QK_DIM))


def _overlap_matrix(nc, ns):
    i = np.arange(nc)[:, None]
    j = np.arange(ns)[None, :]
    lo = np.maximum(i * CMP_STRIDE, j * SEL_BLOCK)
    hi = np.minimum(i * CMP_STRIDE + CMP_BLOCK, (j + 1) * SEL_BLOCK)
    return (np.maximum(hi - lo, 0) / CMP_STRIDE).astype(np.float32)


def kernel(x, norm_attn, w_in, fox_f_bias, fox_q_gain, fox_k_gain,
           nsa_q_gain, nsa_kc_gain, nsa_ks_gain, nsa_kw_gain,
           cmp_pe_k, cmp_w1_k, cmp_w2_k, cmp_pe_v, cmp_w1_v, cmp_w2_v,
           w_up_fox, w_up_nsa, w_out, norm_ffn, w_ffn_gate, w_ffn_up, w_ffn_down):
    batch, seq, d = x.shape
    m = batch * seq
    depth = w_in.shape[0]
    pts = [0] + [int(p) for p in np.cumsum(IN_SPLITS)]
    nblk = seq // CMP_STRIDE
    n_cmp = nblk - CMP_BLOCK // CMP_STRIDE + 1
    ns = seq // SEL_BLOCK

    slope = jnp.exp2(-8.0 * jnp.arange(1, NSA_HEADS + 1, dtype=F32) / NSA_HEADS) * LOG2E
    s1, s2, s3 = [p.astype(F32) for p in _split3(slope)]
    q_spare = jnp.stack([256.0 * s1, 256.0 * s2, 256.0 * s3, s1, s2, s3, -slope, -slope, -slope],
                        axis=1)
    q_spare = jnp.pad(q_spare, ((0, 0), (0, NSA_QK_PAD - NSA_QK_DIM - 9)))
    ovt_np = np.zeros((LANES, nblk), np.float32)
    ovt_np[:ns, :n_cmp] = _overlap_matrix(n_cmp, ns).T
    ovt = jnp.asarray(ovt_np, BF16)

    w_in_t = jnp.swapaxes(w_in, 1, 2)

    xf = x.reshape(m, d)
    for l in range(depth):
        wt = w_in_t[l]
        row = dict(zip(("fq", "fk", "fv", "fl", "nq", "kc", "vc", "ks", "vs", "kw", "vw", "ng",
                        "ga", "gb"), pts))

        gain1 = jnp.concatenate([jnp.tile(fox_q_gain[l] * (FOX_HEAD_DIM ** -0.5 * LOG2E), FOX_HEADS),
                                 jnp.tile(fox_k_gain[l], FOX_HEADS),
                                 jnp.ones((FOX_WIDTH,), F32)])
        flag1 = jnp.concatenate([jnp.ones((2 * FOX_WIDTH,), F32), jnp.zeros((FOX_WIDTH,), F32)])
        q_gain = jnp.broadcast_to(nsa_q_gain[l] * (NSA_QK_DIM ** -0.5 * LOG2E),
                                  (NSA_HEADS, NSA_QK_DIM))
        q_table = jnp.broadcast_to(
            jnp.concatenate([q_gain, q_spare], axis=1).reshape(NSA_HEADS * NSA_QK_PAD, 1),
            (NSA_HEADS * NSA_QK_PAD, LANES))
        gain_k = jnp.concatenate([jnp.tile(_pad_gain(nsa_ks_gain[l]), NSA_KV_GROUPS),
                                  jnp.tile(_pad_gain(nsa_kw_gain[l]), NSA_KV_GROUPS)])
        n_small = FOX_HEADS + 3 * NSA_HEADS
        w3 = jnp.concatenate([_pad_head_rows(wt[row["kc"]:row["vc"]], NSA_KV_GROUPS),
                              wt[row["vc"]:row["ks"]], wt[row["fl"]:row["nq"]],
                              wt[row["ng"]:row["ga"]],
                              jnp.zeros((LANES - n_small, d), F32)], axis=0)

        xn, p3 = _rmsnorm_project(xf, norm_attn[l], w3)
        q_tile = NSA_HPG * NSA_QK_DIM
        p1 = _project(xn, wt, list(range(0, 3 * FOX_WIDTH, FOX_PROJ_TN)), FOX_PROJ_TN, gain1,
                      flag1, BF16, PROJ_TM, "proj_fox", group=FOX_HEAD_DIM, count=FOX_HEAD_DIM)
        pq = _project_qt(xn, wt, [row["nq"], row["nq"] + q_tile], q_tile, q_table, PROJ_TM, seq,
                         "proj_nsa_q")
        w_k = jnp.concatenate([wt[row["ks"]:row["vs"]], wt[row["kw"]:row["vw"]]], axis=0)
        w_v = jnp.concatenate([wt[row["vs"]:row["kw"]], wt[row["vw"]:row["ng"]]], axis=0)
        pk = _project(xn, w_k, [0], 2 * KV_K, gain_k, jnp.ones_like(gain_k), BF16,
                      PROJ_TM, "proj_nsa_k", pad_from=NSA_QK_DIM, pad_to=NSA_QK_PAD,
                      group=NSA_QK_PAD, count=NSA_QK_DIM, aug="key", seq=seq)
        pv = _project(xn, w_v, [0], 2 * KV_V, None, None, BF16, PROJ_TM, "proj_nsa_v")

        bias_row = jnp.pad(fox_f_bias[l], (0, LANES - FOX_HEADS)).reshape(1, LANES)
        ccol, crow = _decay(p3, bias_row, batch, seq)
        o_a = _fox_attention(p1, ccol, crow.reshape(batch, FOX_HEADS, 1, seq), batch, seq)

        pad_d = NSA_QK_PAD - NSA_QK_DIM
        pek = jnp.pad(cmp_pe_k[l], ((0, 0), (0, pad_d)))
        w1k = jnp.pad(cmp_w1_k[l].reshape(CMP_BLOCK, NSA_QK_DIM, CMP_HIDDEN),
                      ((0, 0), (0, pad_d), (0, 0))).astype(BF16)
        w2k = jnp.pad(cmp_w2_k[l], ((0, 0), (0, pad_d))).astype(BF16)
        gk = _pad_gain(nsa_kc_gain[l]).reshape(1, NSA_QK_PAD)
        w1v = cmp_w1_v[l].reshape(CMP_BLOCK, NSA_V_DIM, CMP_HIDDEN).astype(BF16)
        w2v = cmp_w2_v[l].astype(BF16)
        kc, vct = _compress(p3, pek, w1k, w2k, gk, cmp_pe_v[l], w1v, w2v, batch, seq)
        sel, ocmp = _nsa_select(pq, p3, kc, vct, ovt, batch, seq)
        o_b = _nsa_attend(pq, pk, pv, p3, sel, ocmp, batch, seq)

        merged = _merge(xn, o_a, o_b, wt, row["ga"], row["gb"], w_up_fox[l], w_up_nsa[l])
        hres, hn = _out_proj_norm(merged, w_out[l], xf, norm_ffn[l])

        act = _swiglu(hn, w_ffn_gate[l], w_ffn_up[l])
        xf = _matmul_residual(act, w_ffn_down[l], hres, FFN_DOWN_TM, FFN_DOWN_TN, "ffn_down")
    return xf.reshape(batch, seq, d)
```

```python
import functools

import numpy as np
import jax
import jax.numpy as jnp
from jax import lax
from jax.experimental import pallas as pl
from jax.experimental.pallas import tpu as pltpu

F32 = jnp.float32
BF16 = jnp.bfloat16

D_MODEL = 2048
FOX_HEADS = 8
FOX_HEAD_DIM = 128
FOX_WIDTH = FOX_HEADS * FOX_HEAD_DIM
NSA_HEADS = 8
NSA_KV_GROUPS = 2
NSA_HPG = NSA_HEADS // NSA_KV_GROUPS
NSA_QK_DIM = 192
NSA_QK_PAD = 256
NSA_V_DIM = 128
NSA_WIDTH = NSA_HEADS * NSA_V_DIM
CMP_BLOCK = 32
CMP_STRIDE = 16
CMP_HIDDEN = 256
SEL_BLOCK = 64
SEL_TOPK = 16
SEL_LOCAL = 2
FORCE_SCORE = 1.0e4
WINDOW = 512
KV_K = NSA_KV_GROUPS * NSA_QK_DIM
KV_V = NSA_KV_GROUPS * NSA_V_DIM
D_FF = -(-(8 * D_MODEL) // (3 * 256)) * 256
RMS_EPS = 1e-6
IN_SPLITS = (FOX_WIDTH, FOX_WIDTH, FOX_WIDTH, FOX_HEADS,
             NSA_HEADS * NSA_QK_DIM, KV_K, KV_V, KV_K, KV_V, KV_K, KV_V,
             3 * NSA_HEADS, D_MODEL, D_MODEL)

_NT = (((1,), (1,)), ((), ()))

LANES = 128
SUBLANES = 8
BF16_SUBLANES = 16
NEG = -1.0e30
LOG2E = 1.4426950408889634
SLC_TILE = 512
VMEM_LIMIT = 56 * 1024 * 1024

RMS_TM = 512
PROJ_TM = 1024
FOX_PROJ_TN = 1024
MERGE_TM, MERGE_TN = 512, 512
OUT_PROJ_TM = 512
SWIGLU_TM, SWIGLU_TN = 1024, 512
FFN_DOWN_TM, FFN_DOWN_TN = 512, 512
FOX_TQ = 512
FOX_HEADS_PER_STEP = 4
NSA_SELECT_TQ = 512
NSA_ATTEND_TQ = 256


def _cparams(sem):
    return pltpu.CompilerParams(dimension_semantics=sem, vmem_limit_bytes=VMEM_LIMIT)


def _rms_proj_kernel(x_ref, g_ref, w_ref, xn_ref, p_ref, wb_ref):
    @pl.when(pl.program_id(0) == 0)
    def _():
        _stage_weight(w_ref, wb_ref, True)

    rows = x_ref.shape[0] // EPILOGUE_SPLIT
    for r in range(EPILOGUE_SPLIT):
        sl = slice(r * rows, (r + 1) * rows)
        x = x_ref[sl, :]
        ms = jnp.mean(x * x, axis=-1, keepdims=True)
        xn = (x * lax.rsqrt(ms + RMS_EPS) * g_ref[...]).astype(BF16)
        xn_ref[sl, :] = xn
        p_ref[sl, :] = jnp.dot(xn, wb_ref[...], preferred_element_type=F32)


def _rmsnorm_project(x, gain, wt, tm=RMS_TM):
    m, d = x.shape
    n = wt.shape[0]
    return pl.pallas_call(
        _rms_proj_kernel,
        grid=(m // tm,),
        in_specs=[pl.BlockSpec((tm, d), lambda i: (i, 0)),
                  pl.BlockSpec((1, d), lambda i: (0, 0)),
                  pl.BlockSpec((n, d), lambda i: (0, 0), pipeline_mode=pl.Buffered(1))],
        out_specs=[pl.BlockSpec((tm, d), lambda i: (i, 0)),
                   pl.BlockSpec((tm, n), lambda i: (i, 0))],
        out_shape=[jax.ShapeDtypeStruct((m, d), BF16), jax.ShapeDtypeStruct((m, n), F32)],
        scratch_shapes=[pltpu.VMEM((d, n), BF16)],
        compiler_params=_cparams(("arbitrary",)),
        name="rmsnorm_proj_f32",
    )(x, gain.reshape(1, d), wt)


STAGE_CHUNK = 512
EPILOGUE_SPLIT = 4


def _stage_weight(w_ref, wb_ref, transposed=False, pad_from=0, pad_to=0):
    if not transposed:
        wb_ref[...] = w_ref[...].astype(BF16)
        return
    n_in, k = w_ref.shape
    for c in range(k // STAGE_CHUNK):
        cols = slice(c * STAGE_CHUNK, (c + 1) * STAGE_CHUNK)
        w = w_ref[:, cols]
        if pad_from != pad_to:
            zero = jnp.zeros((pad_to - pad_from, STAGE_CHUNK), F32)
            w = jnp.concatenate(
                [piece for h in range(n_in // pad_from)
                 for piece in (w[h * pad_from:(h + 1) * pad_from, :], zero)], axis=0)
        wb_ref[cols, :] = w.T.astype(BF16)


def _first_m_step():
    return pl.program_id(1) == 0


AUG_LANE = NSA_QK_DIM - LANES


def _key_aug(pos):
    lane = lax.broadcasted_iota(jnp.int32, pos.shape, 1)
    hi = (pos >> 8).astype(F32)
    lo = (pos & 255).astype(F32)
    return jnp.where(lane < AUG_LANE, 0.0,
                     jnp.where(lane < AUG_LANE + 3, hi,
                               jnp.where(lane < AUG_LANE + 6, lo,
                                         jnp.where(lane < AUG_LANE + 9, 1.0, 0.0))))


def _proj_kernel(a_ref, w_ref, gain_ref, flag_ref, o_ref, wb_ref, *, group):
    @pl.when(_first_m_step())
    def _():
        _stage_weight(w_ref, wb_ref, True)

    rows = a_ref.shape[0] // EPILOGUE_SPLIT
    ys = [jnp.dot(a_ref[r * rows:(r + 1) * rows, :], wb_ref[...], preferred_element_type=F32)
          for r in range(EPILOGUE_SPLIT)]
    for r, y in enumerate(ys):
        for c in range(y.shape[1] // group):
            sl = slice(c * group, (c + 1) * group)
            yc = y[:, sl]
            ss = jnp.sum(yc * yc, axis=-1, keepdims=True)
            rs = lax.rsqrt(ss * (1.0 / group) + RMS_EPS)
            scale = jnp.where(flag_ref[:, sl] > 0.0, rs, 1.0)
            o_ref[r * rows:(r + 1) * rows, sl] = (
                yc * scale * gain_ref[:, sl]).astype(o_ref.dtype)


def _row_window(starts, rows, k, grid_arg=0):
    def index(*grid):
        j = grid[grid_arg]
        start = starts[-1]
        for t in range(len(starts) - 2, -1, -1):
            start = jnp.where(j == t, starts[t], start)
        return pl.multiple_of(start, SUBLANES), 0
    assert all(s % SUBLANES == 0 for s in starts)
    return pl.BlockSpec((pl.Element(rows), pl.Element(k)), index)


def _project(a, wt, starts, tn, gain, flag, tm, name, group):
    m, k = a.shape
    n_out = len(starts) * tn
    return pl.pallas_call(
        functools.partial(_proj_kernel, group=group),
        grid=(len(starts), m // tm),
        in_specs=[pl.BlockSpec((tm, k), lambda j, i: (i, 0)),
                  _row_window(starts, tn, k),
                  pl.BlockSpec((1, tn), lambda j, i: (0, j)),
                  pl.BlockSpec((1, tn), lambda j, i: (0, j))],
        out_specs=pl.BlockSpec((tm, tn), lambda j, i: (i, j)),
        out_shape=jax.ShapeDtypeStruct((m, n_out), BF16),
        scratch_shapes=[pltpu.VMEM((k, tn), BF16)],
        compiler_params=_cparams(("arbitrary", "arbitrary")),
        name=name,
    )(a, wt, gain.reshape(1, n_out), flag.reshape(1, n_out))


def _proj_qt_kernel(a_ref, w_ref, tab_ref, o_ref, wb_ref, *, head, head_pad, seq):
    @pl.when(_first_m_step())
    def _():
        wb_ref[...] = w_ref[...].astype(BF16)

    tm = a_ref.shape[0]
    cols = tm // EPILOGUE_SPLIT
    reps = cols // LANES
    spare = head_pad - head
    ys = [lax.dot_general(wb_ref[...], a_ref[r * cols:(r + 1) * cols, :], _NT,
                          preferred_element_type=F32) for r in range(EPILOGUE_SPLIT)]
    row = lax.broadcasted_iota(jnp.int32, (spare, cols), 0)
    for r, y in enumerate(ys):
        lanes = slice(r * cols, (r + 1) * cols)
        pos = ((pl.program_id(1) * tm) % seq + r * cols
               + lax.broadcasted_iota(jnp.int32, (1, cols), 1)).astype(F32)
        for h in range(y.shape[0] // head):
            yh = y[h * head:(h + 1) * head, :]
            rs = lax.rsqrt(jnp.sum(yh * yh, axis=0, keepdims=True) * (1.0 / head) + RMS_EPS)
            slot = h * head_pad
            gain = jnp.concatenate([tab_ref[slot:slot + head, :]] * reps, axis=1)
            o_ref[slot:slot + head, lanes] = (yh * rs * gain).astype(o_ref.dtype)
            tab = jnp.concatenate([tab_ref[slot + head:slot + head_pad, :]] * reps, axis=1)
            u = pos * tab
            hi = u.astype(BF16).astype(F32)
            r1 = u - hi
            mid = r1.astype(BF16).astype(F32)
            extra = jnp.where(row < 6, tab,
                              jnp.where(row == 6, hi,
                                        jnp.where(row == 7, mid,
                                                  jnp.where(row == 8, r1 - mid, 0.0))))
            o_ref[slot + head:slot + head_pad, lanes] = extra.astype(o_ref.dtype)


def _project_qt(a, wt, starts, tn_in, table, tm, seq, name):
    m, k = a.shape
    tn_out = tn_in // NSA_QK_DIM * NSA_QK_PAD
    assert seq % tm == 0
    return pl.pallas_call(
        functools.partial(_proj_qt_kernel, head=NSA_QK_DIM, head_pad=NSA_QK_PAD, seq=seq),
        grid=(len(starts), m // tm),
        in_specs=[pl.BlockSpec((tm, k), lambda j, i: (i, 0)),
                  _row_window(starts, tn_in, k),
                  pl.BlockSpec((tn_out, LANES), lambda j, i: (j, 0))],
        out_specs=pl.BlockSpec((tn_out, tm), lambda j, i: (j, i)),
        out_shape=jax.ShapeDtypeStruct((len(starts) * tn_out, m), BF16),
        scratch_shapes=[pltpu.VMEM((tn_in, k), BF16)],
        compiler_params=_cparams(("arbitrary", "arbitrary")),
        name=name,
    )(a, wt, table)


def _out_norm_kernel(a_ref, w_ref, r_ref, g_ref, h_ref, hn_ref, wb_ref):
    @pl.when(pl.program_id(0) == 0)
    def _():
        _stage_weight(w_ref, wb_ref)

    h = r_ref[...] + jnp.dot(a_ref[...], wb_ref[...], preferred_element_type=F32)
    h_ref[...] = h
    ms = jnp.mean(h * h, axis=-1, keepdims=True)
    hn_ref[...] = (h * lax.rsqrt(ms + RMS_EPS) * g_ref[...]).astype(hn_ref.dtype)


def _out_proj_norm(a, w, res, gain, tm=OUT_PROJ_TM):
    m, k = a.shape
    n = w.shape[1]
    full = lambda i: (0, 0)
    row = lambda i: (i, 0)
    return pl.pallas_call(
        _out_norm_kernel,
        grid=(m // tm,),
        in_specs=[pl.BlockSpec((tm, k), row),
                  pl.BlockSpec((k, n), full, pipeline_mode=pl.Buffered(1)),
                  pl.BlockSpec((tm, n), row), pl.BlockSpec((1, n), full)],
        out_specs=[pl.BlockSpec((tm, n), row), pl.BlockSpec((tm, n), row)],
        out_shape=[jax.ShapeDtypeStruct((m, n), F32), jax.ShapeDtypeStruct((m, n), BF16)],
        scratch_shapes=[pltpu.VMEM((k, n), BF16)],
        compiler_params=_cparams(("arbitrary",)),
        name="out_proj_norm",
    )(a, w, res, gain.reshape(1, n))


def _mm_res_kernel(a_ref, w_ref, r_ref, o_ref, wb_ref):
    @pl.when(_first_m_step())
    def _():
        _stage_weight(w_ref, wb_ref)

    o_ref[...] = r_ref[...] + jnp.dot(a_ref[...], wb_ref[...], preferred_element_type=F32)


def _matmul_residual(a, w, res, tm, tn, name):
    m, k = a.shape
    n = w.shape[1]
    return pl.pallas_call(
        _mm_res_kernel,
        grid=(n // tn, m // tm),
        in_specs=[pl.BlockSpec((tm, k), lambda j, i: (i, 0)),
                  pl.BlockSpec((k, tn), lambda j, i: (0, j)),
                  pl.BlockSpec((tm, tn), lambda j, i: (i, j))],
        out_specs=pl.BlockSpec((tm, tn), lambda j, i: (i, j)),
        out_shape=jax.ShapeDtypeStruct((m, n), F32),
        scratch_shapes=[pltpu.VMEM((k, tn), BF16)],
        compiler_params=_cparams(("arbitrary", "arbitrary")),
        name=name,
    )(a, w, res)


def _merge_kernel(xn_ref, oa_ref, ob_ref, wga_ref, wgb_ref, wuf_ref, wun_ref, o_ref,
                  bga_ref, bgb_ref, buf_ref, bun_ref):
    @pl.when(_first_m_step())
    def _():
        _stage_weight(wga_ref, bga_ref, True)
        _stage_weight(wgb_ref, bgb_ref, True)
        _stage_weight(wuf_ref, buf_ref)
        _stage_weight(wun_ref, bun_ref)

    xn = xn_ref[...]
    ga = jax.nn.sigmoid(jnp.dot(xn, bga_ref[...], preferred_element_type=F32))
    ua = jnp.dot(oa_ref[...], buf_ref[...], preferred_element_type=F32)
    acc = ga * ua
    gb = jax.nn.sigmoid(jnp.dot(xn, bgb_ref[...], preferred_element_type=F32))
    ub = jnp.dot(ob_ref[...], bun_ref[...], preferred_element_type=F32)
    o_ref[...] = (acc + gb * ub).astype(o_ref.dtype)


def _merge(xn, oa, ob, wt, row_a, row_b, wuf, wun, tm=MERGE_TM, tn=MERGE_TN):
    m, d = xn.shape
    n = wuf.shape[1]
    ka = oa.shape[1]
    kb = ob.shape[1]
    row = lambda j, i: (i, 0)
    col = lambda j, i: (0, j)
    return pl.pallas_call(
        _merge_kernel,
        grid=(n // tn, m // tm),
        in_specs=[pl.BlockSpec((tm, d), row), pl.BlockSpec((tm, ka), row),
                  pl.BlockSpec((tm, kb), row),
                  _row_window([row_a + t * tn for t in range(n // tn)], tn, d),
                  _row_window([row_b + t * tn for t in range(n // tn)], tn, d),
                  pl.BlockSpec((ka, tn), col), pl.BlockSpec((kb, tn), col)],
        out_specs=pl.BlockSpec((tm, tn), lambda j, i: (i, j)),
        out_shape=jax.ShapeDtypeStruct((m, n), BF16),
        scratch_shapes=[pltpu.VMEM((d, tn), BF16), pltpu.VMEM((d, tn), BF16),
                        pltpu.VMEM((ka, tn), BF16), pltpu.VMEM((kb, tn), BF16)],
        compiler_params=_cparams(("arbitrary", "arbitrary")),
        name="gated_merge",
    )(xn, oa, ob, wt, wt, wuf, wun)


def _swiglu_kernel(a_ref, wg_ref, wu_ref, o_ref, bg_ref, bu_ref):
    @pl.when(_first_m_step())
    def _():
        _stage_weight(wg_ref, bg_ref)
        _stage_weight(wu_ref, bu_ref)

    a = a_ref[...]
    gt = jnp.dot(a, bg_ref[...], preferred_element_type=F32)
    up = jnp.dot(a, bu_ref[...], preferred_element_type=F32)
    o_ref[...] = (gt * jax.nn.sigmoid(gt) * up).astype(o_ref.dtype)


def _swiglu(a, wg, wu, tm=SWIGLU_TM, tn=SWIGLU_TN):
    m, k = a.shape
    n = wg.shape[1]
    return pl.pallas_call(
        _swiglu_kernel,
        grid=(n // tn, m // tm),
        in_specs=[pl.BlockSpec((tm, k), lambda j, i: (i, 0)),
                  pl.BlockSpec((k, tn), lambda j, i: (0, j)),
                  pl.BlockSpec((k, tn), lambda j, i: (0, j))],
        out_specs=pl.BlockSpec((tm, tn), lambda j, i: (i, j)),
        out_shape=jax.ShapeDtypeStruct((m, n), BF16),
        scratch_shapes=[pltpu.VMEM((k, tn), BF16), pltpu.VMEM((k, tn), BF16)],
        compiler_params=_cparams(("arbitrary", "arbitrary")),
        name="swiglu_up",
    )(a, wg, wu)


def _split3(x):
    hi = x.astype(BF16)
    r1 = x - hi.astype(F32)
    mid = r1.astype(BF16)
    lo = (r1 - mid.astype(F32)).astype(BF16)
    return hi, mid, lo


def _decay_kernel(z_ref, b_ref, ccol_ref, crow_ref, *, blk):
    t = z_ref.shape[0]
    r = lax.broadcasted_iota(jnp.int32, (blk, blk), 0)
    c = lax.broadcasted_iota(jnp.int32, (blk, blk), 1)
    tri = jnp.where(r >= c, 1.0, 0.0).astype(BF16)
    carry = jnp.zeros((1, LANES), F32)
    for s in range(t // blk):
        rows = slice(s * blk, (s + 1) * blk)
        z = z_ref[rows, :] + b_ref[...]
        logf = (jnp.minimum(z, 0.0) - jnp.log1p(jnp.exp(-jnp.abs(z)))) * LOG2E
        hi, mid, lo = _split3(logf)
        cb = (jnp.dot(tri, hi, preferred_element_type=F32)
              + jnp.dot(tri, mid, preferred_element_type=F32)
              + jnp.dot(tri, lo, preferred_element_type=F32)) + carry
        carry = cb[blk - 1:blk, :]
        ccol_ref[rows, :] = cb
        crow_ref[0, :, rows] = cb.T[:FOX_HEADS, :]


def _decay(p3, bias_row, batch, seq, blk=256):
    return pl.pallas_call(
        functools.partial(_decay_kernel, blk=blk),
        grid=(batch,),
        in_specs=[pl.BlockSpec((seq, LANES), lambda b: (b, 6)),
                  pl.BlockSpec((1, LANES), lambda b: (0, 0))],
        out_specs=[pl.BlockSpec((seq, LANES), lambda b: (b, 0)),
                   pl.BlockSpec((1, FOX_HEADS, seq), lambda b: (b, 0, 0))],
        out_shape=[jax.ShapeDtypeStruct((batch * seq, LANES), F32),
                   jax.ShapeDtypeStruct((batch, FOX_HEADS, seq), F32)],
        compiler_params=_cparams(("arbitrary",)),
        name="fox_decay_cumsum",
    )(p3, bias_row)


def _transpose_bf16(x):
    return x.astype(F32).T.astype(BF16)


def _normalised(acc, d):
    return acc[:d, :] * (1.0 / jnp.maximum(acc[d:d + 1, :], 1e-30))


def _fox_kernel(q_ref, k_ref, v_ref, ccol_ref, crow_ref, o_ref, vt_ref, ka_ref, *, tq, nq, nh):
    hh = pl.program_id(1)
    i = pl.program_id(2)
    dh = FOX_HEAD_DIM

    def pieces(x, lane, first):
        hi = x.astype(BF16).astype(F32)
        r1 = x - hi
        mid = r1.astype(BF16).astype(F32)
        return jnp.where(lane == first, hi,
                         jnp.where(lane == first + 1, mid,
                                   jnp.where(lane == first + 2, r1 - mid, 0.0)))

    @pl.when(i == 0)
    def _():
        cc = ccol_ref[...]
        lane = lax.broadcasted_iota(jnp.int32, cc.shape, 1)
        for h in range(nh):
            vt_ref[h, 0:dh, :] = _transpose_bf16(v_ref[:, h * dh:(h + 1) * dh])
            vt_ref[h, dh:, :] = jnp.ones((vt_ref.shape[1] - dh, vt_ref.shape[2]), BF16)
            cj = jnp.sum(jnp.where(lane == hh * nh + h, cc, 0.0), axis=-1, keepdims=True)
            extra = pieces(-cj, lane, 0) + jnp.where(lane < 3, 0.0, jnp.where(lane < 6, 1.0, 0.0))
            ka_ref[h, :, 0:dh] = k_ref[:, h * dh:(h + 1) * dh]
            ka_ref[h, :, dh:] = extra.astype(BF16)

    row = lax.broadcasted_iota(jnp.int32, (dh, tq), 0)
    qts = []
    for h in range(nh):
        ci = crow_ref[0, h]
        extra = pieces(ci, row, 3) + jnp.where(row < 3, 1.0, 0.0)
        qts.append(jnp.concatenate(
            [_transpose_bf16(q_ref[:, h * dh:(h + 1) * dh]), extra.astype(BF16)], axis=0))
    rk = lax.broadcasted_iota(jnp.int32, (tq, tq), 0)
    cq = lax.broadcasted_iota(jnp.int32, (tq, tq), 1)

    def scores(h, k0):
        return jnp.dot(ka_ref[h, k0:k0 + tq, :], qts[h], preferred_element_type=F32)

    def variant(n):
        def run():
            starts = [t * tq for t in range(n, -1, -1)]
            tiles = [[jnp.where(rk <= cq, scores(h, starts[0]), NEG)]
                     + [scores(h, k0) for k0 in starts[1:]] for h in range(nh)]
            ms = [jnp.max(tiles[h][0], axis=0, keepdims=True) for h in range(nh)]
            accs = [jnp.dot(vt_ref[h, :, starts[0]:starts[0] + tq],
                            jnp.exp2(tiles[h][0] - ms[h]).astype(BF16),
                            preferred_element_type=F32) for h in range(nh)]
            for t in range(1, n + 1):
                k0 = starts[t]
                for h in range(nh):
                    st = tiles[h][t]
                    m_new = jnp.maximum(ms[h], jnp.max(st, axis=0, keepdims=True))
                    accs[h] = jnp.exp2(ms[h] - m_new) * accs[h] + jnp.dot(
                        vt_ref[h, :, k0:k0 + tq], jnp.exp2(st - m_new).astype(BF16),
                        preferred_element_type=F32)
                    ms[h] = m_new
            for h in range(nh):
                o_ref[:, h * dh:(h + 1) * dh] = _normalised(accs[h], dh).T.astype(o_ref.dtype)
        return run

    lax.switch(i, [variant(n) for n in range(nq)])


def _fox_attention(p1, ccol, crow4, batch, seq, tq=FOX_TQ, nh=FOX_HEADS_PER_STEP):
    nq = seq // tq
    hg = FOX_HEADS // nh
    w = nh * FOX_HEAD_DIM
    return pl.pallas_call(
        functools.partial(_fox_kernel, tq=tq, nq=nq, nh=nh),
        grid=(batch, hg, nq),
        in_specs=[pl.BlockSpec((tq, w), lambda b, hh, i: (b * nq + i, hh)),
                  pl.BlockSpec((seq, w), lambda b, hh, i: (b, hg + hh)),
                  pl.BlockSpec((seq, w), lambda b, hh, i: (b, 2 * hg + hh)),
                  pl.BlockSpec((seq, LANES), lambda b, hh, i: (b, 0)),
                  pl.BlockSpec((1, nh, 1, tq), lambda b, hh, i: (b, hh, 0, i))],
        out_specs=pl.BlockSpec((tq, w), lambda b, hh, i: (b * nq + i, hh)),
        out_shape=jax.ShapeDtypeStruct((batch * seq, FOX_WIDTH), BF16),
        scratch_shapes=[pltpu.VMEM((nh, FOX_HEAD_DIM + BF16_SUBLANES, seq), BF16),
                        pltpu.VMEM((nh, seq, 2 * FOX_HEAD_DIM), BF16)],
        compiler_params=_cparams(("arbitrary", "arbitrary", "arbitrary")),
        name="fox_attention",
    )(p1, p1, p1, ccol, crow4)


def _compress_one(z_refs, pe_ref, w1_ref, w2_ref, nblk):
    half = CMP_BLOCK // 2
    first = jnp.zeros((nblk, CMP_HIDDEN), F32)
    second = jnp.zeros((nblk, CMP_HIDDEN), F32)
    for p in range(half):
        rows = pl.ds(p, nblk, stride=CMP_STRIDE)
        zp = [z_ref[rows, :] for z_ref in z_refs]
        zp = zp[0] if len(zp) == 1 else jnp.concatenate(zp, axis=1)
        first += jnp.dot((zp + pe_ref[p:p + 1, :]).astype(BF16), w1_ref[p],
                         preferred_element_type=F32)
        second += jnp.dot((zp + pe_ref[half + p:half + p + 1, :]).astype(BF16),
                          w1_ref[half + p], preferred_element_type=F32)
    hid = first + pltpu.roll(second, nblk - 1, 0)
    act = (hid * jax.nn.sigmoid(hid)).astype(BF16)
    return jnp.dot(act, w2_ref[...], preferred_element_type=F32)


def _compress_kernel(zk0_ref, zk1_ref, zv_ref, pek_ref, w1k_ref, w2k_ref, gk_ref,
                     pev_ref, w1v_ref, w2v_ref, kc_ref, vc_ref, *, nblk):
    kc = _compress_one((zk0_ref, zk1_ref), pek_ref, w1k_ref, w2k_ref, nblk)
    ms = jnp.sum(kc * kc, axis=-1, keepdims=True) * (1.0 / NSA_QK_DIM)
    kc = kc * lax.rsqrt(ms + RMS_EPS) * gk_ref[...]
    pos = CMP_STRIDE * lax.broadcasted_iota(jnp.int32, (nblk, LANES), 0) + (CMP_BLOCK - 1)
    up = NSA_QK_PAD - LANES
    kc_ref[:, :up] = kc[:, :up].astype(kc_ref.dtype)
    kc_ref[:, up:] = (kc[:, up:] + _key_aug(pos)).astype(kc_ref.dtype)
    vc = _compress_one((zv_ref,), pev_ref, w1v_ref, w2v_ref, nblk)
    vc_ref[...] = vc.T.astype(vc_ref.dtype)


def _compress(p3, pek, w1k, w2k, gk, pev, w1v, w2v, batch, seq):
    g = NSA_KV_GROUPS
    nblk = seq // CMP_STRIDE
    full2 = lambda b, gg: (0, 0)
    full3 = lambda b, gg: (0, 0, 0)
    return pl.pallas_call(
        functools.partial(_compress_kernel, nblk=nblk),
        grid=(batch, g),
        in_specs=[pl.BlockSpec((seq, LANES), lambda b, gg: (b, 2 * gg)),
                  pl.BlockSpec((seq, LANES), lambda b, gg: (b, 2 * gg + 1)),
                  pl.BlockSpec((seq, NSA_V_DIM), lambda b, gg: (b, 4 + gg)),
                  pl.BlockSpec(pek.shape, full2), pl.BlockSpec(w1k.shape, full3),
                  pl.BlockSpec(w2k.shape, full2), pl.BlockSpec(gk.shape, full2),
                  pl.BlockSpec(pev.shape, full2), pl.BlockSpec(w1v.shape, full3),
                  pl.BlockSpec(w2v.shape, full2)],
        out_specs=[pl.BlockSpec((nblk, NSA_QK_PAD), lambda b, gg: (b * g + gg, 0)),
                   pl.BlockSpec((NSA_V_DIM, nblk), lambda b, gg: (b * g + gg, 0))],
        out_shape=[jax.ShapeDtypeStruct((batch * g * nblk, NSA_QK_PAD), BF16),
                   jax.ShapeDtypeStruct((batch * g * NSA_V_DIM, nblk), BF16)],
        compiler_params=_cparams(("arbitrary", "arbitrary")),
        name="nsa_compress",
    )(p3, p3, p3, pek, w1k, w2k, gk, pev, w1v, w2v)


def _q_heads_t(qt_ref, first_head):
    return jnp.concatenate(
        [qt_ref[(first_head + hh) * NSA_QK_PAD:(first_head + hh + 1) * NSA_QK_PAD, :]
         for hh in range(NSA_HPG)], axis=1)


def _gate_rows(glt_ref, g, hh):
    base = FOX_HEADS + (g * NSA_HPG + hh) * 3
    return [glt_ref[pl.ds(base + br, 1), :] for br in range(3)]


def _nsa_select_kernel(q_ref, kc_ref, vct_ref, gl_ref, ovt_ref, xn_ref, wk_ref, wv_ref, gk_ref,
                       sel_ref, ocmp_ref, pk_ref, pv_ref, glt_ref, wkb_ref, wvb_ref,
                       *, tq, n_cmp, n_sel):
    g = pl.program_id(1)
    t0 = pl.program_id(2) * tq
    hpg = NSA_HPG
    dv = NSA_V_DIM

    @pl.when(pl.program_id(2) == 0)
    def _():
        _stage_weight(wk_ref, wkb_ref, True, NSA_QK_DIM, NSA_QK_PAD)
        _stage_weight(wv_ref, wvb_ref, True)

    xn = xn_ref[...]
    yk = jnp.dot(xn, wkb_ref[...], preferred_element_type=F32)
    pv_ref[...] = jnp.dot(xn, wvb_ref[...], preferred_element_type=F32).astype(pv_ref.dtype)
    pos = t0 + lax.broadcasted_iota(jnp.int32, (tq, LANES), 0)
    for c in range(NSA_KV_GROUPS):
        sl = slice(c * NSA_QK_PAD, (c + 1) * NSA_QK_PAD)
        yc = yk[:, sl]
        rs = lax.rsqrt(jnp.sum(yc * yc, axis=-1, keepdims=True) * (1.0 / NSA_QK_DIM) + RMS_EPS)
        out = yc * rs * gk_ref[:, sl]
        up = slice((c + 1) * NSA_QK_PAD - LANES, (c + 1) * NSA_QK_PAD)
        pk_ref[:, c * NSA_QK_PAD:(c + 1) * NSA_QK_PAD - LANES] = (
            out[:, :NSA_QK_PAD - LANES].astype(pk_ref.dtype))
        pk_ref[:, up] = (out[:, NSA_QK_PAD - LANES:] + _key_aug(pos)).astype(pk_ref.dtype)

    q4t = _q_heads_t(q_ref, 0)
    rk = lax.broadcasted_iota(jnp.int32, (LANES, tq), 0)
    cq = lax.broadcasted_iota(jnp.int32, (LANES, tq), 1)

    s_c = jnp.dot(kc_ref[...], q4t, preferred_element_type=F32)
    dist_c = (t0 + cq) - (CMP_STRIDE * rk + (CMP_BLOCK - 1))
    mask_c = jnp.where(rk < n_cmp, dist_c, -1) >= 0
    probs = []
    p_sum = jnp.zeros((LANES, tq), F32)
    for hh in range(hpg):
        sm = jnp.where(mask_c, s_c[:, hh * tq:(hh + 1) * tq], NEG)
        m = jnp.max(sm, axis=0, keepdims=True)
        e = jnp.where(mask_c, jnp.exp2(sm - m), 0.0)
        p = e * (1.0 / jnp.maximum(jnp.sum(e, axis=0, keepdims=True), 1e-30))
        probs.append(p)
        p_sum = p_sum + p
    o_cmp = jnp.dot(vct_ref[...], jnp.concatenate(probs, axis=1).astype(BF16),
                    preferred_element_type=F32)

    ph = p_sum.astype(BF16)
    plo = (p_sum - ph.astype(F32)).astype(BF16)
    ovt = ovt_ref[...]
    imp = (jnp.dot(ovt, ph, preferred_element_type=F32)
           + jnp.dot(ovt, plo, preferred_element_type=F32))[:n_sel, :]

    rj = lax.broadcasted_iota(jnp.int32, (n_sel, tq), 0)
    tcol = t0 + lax.broadcasted_iota(jnp.int32, (n_sel, tq), 1)
    back = (tcol >> (SEL_BLOCK.bit_length() - 1)) - rj
    elig = back >= 0
    forced = jnp.where(rj == 0, 0, jnp.where(elig, back, SEL_LOCAL)) < SEL_LOCAL
    score = jnp.where(elig, jnp.where(forced, FORCE_SCORE, imp), -1.0)
    rank = jnp.zeros((n_sel, tq), F32)
    for jp in range(n_sel):
        row = score[jp:jp + 1, :]
        later = jnp.where(rj > jp, 1.0, 0.0)
        rank = rank + jnp.where(row > score, 1.0, jnp.where(row == score, later, 0.0))
    sel_ref[0:n_sel, :] = jnp.where(elig, jnp.where(rank < SEL_TOPK, 0.0, NEG), NEG)
    sel_ref[n_sel:, :] = jnp.full((LANES - n_sel, tq), NEG, F32)

    glt_ref[...] = jax.nn.sigmoid(gl_ref[...]).T
    for hh in range(hpg):
        gate = _gate_rows(glt_ref, g, hh)[0]
        ocmp_ref[:, hh * dv:(hh + 1) * dv] = (gate * o_cmp[:, hh * tq:(hh + 1) * tq]).T


def _nsa_select(pq, p3, kc, vct, ovt, xn, wt, k_rows, v_rows, gain_k, batch, seq,
                tq=NSA_SELECT_TQ):
    g = NSA_KV_GROUPS
    nq = seq // tq
    nblk = seq // CMP_STRIDE
    n_cmp = nblk - CMP_BLOCK // CMP_STRIDE + 1
    d = xn.shape[1]
    kw_out = g * NSA_QK_PAD
    return pl.pallas_call(
        functools.partial(_nsa_select_kernel, tq=tq, n_cmp=n_cmp, n_sel=seq // SEL_BLOCK),
        grid=(batch, g, nq),
        in_specs=[
            pl.BlockSpec((NSA_HPG * NSA_QK_PAD, tq), lambda b, gg, i: (gg, b * nq + i)),
            pl.BlockSpec((nblk, NSA_QK_PAD), lambda b, gg, i: (b * g + gg, 0)),
            pl.BlockSpec((NSA_V_DIM, nblk), lambda b, gg, i: (b * g + gg, 0)),
            pl.BlockSpec((tq, LANES), lambda b, gg, i: (b * nq + i, 6)),
            pl.BlockSpec(ovt.shape, lambda b, gg, i: (0, 0)),
            pl.BlockSpec((tq, d), lambda b, gg, i: (b * nq + i, 0)),
            _row_window(k_rows, KV_K, d, grid_arg=1),
            _row_window(v_rows, KV_V, d, grid_arg=1),
            pl.BlockSpec((1, kw_out), lambda b, gg, i: (0, gg)),
        ],
        out_specs=[pl.BlockSpec((LANES, tq), lambda b, gg, i: (b * g + gg, i)),
                   pl.BlockSpec((tq, NSA_HPG * NSA_V_DIM), lambda b, gg, i: (b * nq + i, gg)),
                   pl.BlockSpec((tq, kw_out), lambda b, gg, i: (b * nq + i, gg)),
                   pl.BlockSpec((tq, KV_V), lambda b, gg, i: (b * nq + i, gg))],
        out_shape=[jax.ShapeDtypeStruct((batch * g * LANES, seq), F32),
                   jax.ShapeDtypeStruct((batch * seq, NSA_WIDTH), F32),
                   jax.ShapeDtypeStruct((batch * seq, 2 * kw_out), BF16),
                   jax.ShapeDtypeStruct((batch * seq, 2 * KV_V), BF16)],
        scratch_shapes=[pltpu.VMEM((LANES, tq), F32), pltpu.VMEM((d, kw_out), BF16),
                        pltpu.VMEM((d, KV_V), BF16)],
        compiler_params=_cparams(("arbitrary", "arbitrary", "arbitrary")),
        name="nsa_select",
    )(pq, kc, vct, p3, ovt, xn, wt, wt, gain_k.reshape(1, 2 * kw_out))


def _nsa_attend_kernel(q_ref, ks_ref, vs_ref, kw_ref, vw_ref, gl_ref, sel_ref,
                       ocmp_ref, o_ref, vst_ref, vwt_ref, glt_ref, *, tq, n_var):
    i = pl.program_id(1)
    t0 = i * tq
    hpg = NSA_HPG
    ng = NSA_KV_GROUPS
    dv = NSA_V_DIM
    dk = NSA_QK_PAD

    @pl.when(i == 0)
    def _():
        ones = jnp.ones((vst_ref.shape[1] - dv, vst_ref.shape[2]), BF16)
        for gg in range(ng):
            vst_ref[gg, 0:dv, :] = _transpose_bf16(vs_ref[:, gg * dv:(gg + 1) * dv])
            vst_ref[gg, dv:, :] = ones
            vwt_ref[gg, 0:dv, :] = _transpose_bf16(vw_ref[:, gg * dv:(gg + 1) * dv])
            vwt_ref[gg, dv:, :] = ones

    q4t = [_q_heads_t(q_ref, gg * hpg) for gg in range(ng)]
    glt_ref[...] = jax.nn.sigmoid(gl_ref[...]).T

    def distance(k0, rows):
        return (lax.broadcasted_iota(jnp.int32, (rows, tq), 1)
                - lax.broadcasted_iota(jnp.int32, (rows, tq), 0)) + (t0 - k0)

    def scores(gg, k, mask_bias):
        st = jnp.dot(k, q4t[gg], preferred_element_type=F32)
        return st + jnp.concatenate([mask_bias] * hpg, axis=1)

    def selection_bias(gg, k0):
        j0 = gg * LANES + k0 // SEL_BLOCK
        return jnp.concatenate(
            [jnp.broadcast_to(sel_ref[j0 + j:j0 + j + 1, :], (SEL_BLOCK, tq))
             for j in range(SLC_TILE // SEL_BLOCK)], axis=0)

    def variant(n):
        def run():
            kw0 = pl.multiple_of(jnp.maximum(t0 - WINDOW, 0), LANES)
            wrows = WINDOW + tq
            dist_w = distance(kw0, wrows)
            bias_w = jnp.where(jnp.where(dist_w >= 0, dist_w, WINDOW) < WINDOW, 0.0, NEG)
            o_win = []
            for gg in range(ng):
                sw = scores(gg, kw_ref[pl.ds(kw0, wrows), gg * dk:(gg + 1) * dk], bias_w)
                pw = jnp.exp2(sw - jnp.max(sw, axis=0, keepdims=True))
                o_win.append(_normalised(
                    jnp.dot(vwt_ref[gg, :, pl.ds(kw0, wrows)], pw.astype(BF16),
                            preferred_element_type=F32), dv))

            starts = [t * SLC_TILE for t in range(n, -1, -1)]
            causal = distance(starts[0], SLC_TILE) >= 0
            tiles = [[] for _ in range(ng)]
            for t, k0 in enumerate(starts):
                for gg in range(ng):
                    bias = selection_bias(gg, k0)
                    if t == 0:
                        bias = jnp.where(causal, bias, NEG)
                    tiles[gg].append(scores(
                        gg, ks_ref[k0:k0 + SLC_TILE, gg * dk:(gg + 1) * dk], bias))
            ms = [jnp.max(tiles[gg][0], axis=0, keepdims=True) for gg in range(ng)]
            accs = [jnp.dot(vst_ref[gg, :, starts[0]:starts[0] + SLC_TILE],
                            jnp.exp2(tiles[gg][0] - ms[gg]).astype(BF16),
                            preferred_element_type=F32) for gg in range(ng)]
            for t in range(1, n + 1):
                k0 = starts[t]
                for gg in range(ng):
                    st = tiles[gg][t]
                    m_new = jnp.maximum(ms[gg], jnp.max(st, axis=0, keepdims=True))
                    accs[gg] = jnp.exp2(ms[gg] - m_new) * accs[gg] + jnp.dot(
                        vst_ref[gg, :, k0:k0 + SLC_TILE], jnp.exp2(st - m_new).astype(BF16),
                        preferred_element_type=F32)
                    ms[gg] = m_new

            for gg in range(ng):
                o_slc = _normalised(accs[gg], dv)
                for hh in range(hpg):
                    _, g_slc, g_win = _gate_rows(glt_ref, gg, hh)
                    lanes = slice(hh * tq, (hh + 1) * tq)
                    out = g_slc * o_slc[:, lanes] + g_win * o_win[gg][:, lanes]
                    cols = slice((gg * hpg + hh) * dv, (gg * hpg + hh + 1) * dv)
                    o_ref[:, cols] = (ocmp_ref[:, cols] + out.T).astype(o_ref.dtype)
        return run

    lax.switch(t0 // SLC_TILE, [variant(n) for n in range(n_var)])


def _nsa_attend(pq, pk, pv, p3, sel, ocmp, batch, seq):
    tq = NSA_ATTEND_TQ
    g = NSA_KV_GROUPS
    nq = seq // tq
    vrows = NSA_V_DIM + BF16_SUBLANES
    return pl.pallas_call(
        functools.partial(_nsa_attend_kernel, tq=tq, n_var=seq // SLC_TILE),
        grid=(batch, nq),
        in_specs=[
            pl.BlockSpec((NSA_HEADS * NSA_QK_PAD, tq), lambda b, i: (0, b * nq + i)),
            pl.BlockSpec((seq, g * NSA_QK_PAD), lambda b, i: (b, 0)),
            pl.BlockSpec((seq, g * NSA_V_DIM), lambda b, i: (b, 0)),
            pl.BlockSpec((seq, g * NSA_QK_PAD), lambda b, i: (b, 1)),
            pl.BlockSpec((seq, g * NSA_V_DIM), lambda b, i: (b, 1)),
            pl.BlockSpec((tq, LANES), lambda b, i: (b * nq + i, 6)),
            pl.BlockSpec((g * LANES, tq), lambda b, i: (b, i)),
            pl.BlockSpec((tq, NSA_WIDTH), lambda b, i: (b * nq + i, 0)),
        ],
        out_specs=pl.BlockSpec((tq, NSA_WIDTH), lambda b, i: (b * nq + i, 0)),
        out_shape=jax.ShapeDtypeStruct((batch * seq, NSA_WIDTH), BF16),
        scratch_shapes=[pltpu.VMEM((g, vrows, seq), BF16), pltpu.VMEM((g, vrows, seq), BF16),
                        pltpu.VMEM((LANES, tq), F32)],
        compiler_params=_cparams(("arbitrary", "arbitrary")),
        name="nsa_attend",
    )(pq, pk, pv, pk, pv, p3, sel, ocmp)


def _pad_head_rows(wt, heads):
    k = wt.shape[1]
    wt = wt.reshape(heads, NSA_QK_DIM, k)
    wt = jnp.pad(wt, ((0, 0), (0, NSA_QK_PAD - NSA_QK_DIM), (0, 0)))
    return wt.reshape(heads * NSA_QK_PAD, k)


def _pad_gain(gain, scale=1.0):
    return jnp.pad(gain * scale, (0, NSA_QK_PAD - NSA_QK_DIM))


def _overlap_matrix(nc, ns):
    i = np.arange(nc)[:, None]
    j = np.arange(ns)[None, :]
    lo = np.maximum(i * CMP_STRIDE, j * SEL_BLOCK)
    hi = np.minimum(i * CMP_STRIDE + CMP_BLOCK, (j + 1) * SEL_BLOCK)
    return (np.maximum(hi - lo, 0) / CMP_STRIDE).astype(np.float32)


def kernel(x, norm_attn, w_in, fox_f_bias, fox_q_gain, fox_k_gain,
           nsa_q_gain, nsa_kc_gain, nsa_ks_gain, nsa_kw_gain,
           cmp_pe_k, cmp_w1_k, cmp_w2_k, cmp_pe_v, cmp_w1_v, cmp_w2_v,
           w_up_fox, w_up_nsa, w_out, norm_ffn, w_ffn_gate, w_ffn_up, w_ffn_down):
    batch, seq, d = x.shape
    m = batch * seq
    depth = w_in.shape[0]
    pts = [0] + [int(p) for p in np.cumsum(IN_SPLITS)]
    nblk = seq // CMP_STRIDE
    n_cmp = nblk - CMP_BLOCK // CMP_STRIDE + 1
    ns = seq // SEL_BLOCK

    slope = jnp.exp2(-8.0 * jnp.arange(1, NSA_HEADS + 1, dtype=F32) / NSA_HEADS) * LOG2E
    s1, s2, s3 = [p.astype(F32) for p in _split3(slope)]
    q_spare = jnp.stack([256.0 * s1, 256.0 * s2, 256.0 * s3, s1, s2, s3, -slope, -slope, -slope],
                        axis=1)
    q_spare = jnp.pad(q_spare, ((0, 0), (0, NSA_QK_PAD - NSA_QK_DIM - 9)))
    ovt_np = np.zeros((LANES, nblk), np.float32)
    ovt_np[:ns, :n_cmp] = _overlap_matrix(n_cmp, ns).T
    ovt = jnp.asarray(ovt_np, BF16)

    w_in_t = jnp.swapaxes(w_in, 1, 2)

    xf = x.reshape(m, d)
    for l in range(depth):
        wt = w_in_t[l]
        row = dict(zip(("fq", "fk", "fv", "fl", "nq", "kc", "vc", "ks", "vs", "kw", "vw", "ng",
                        "ga", "gb"), pts))

        gain1 = jnp.concatenate([jnp.tile(fox_q_gain[l] * (FOX_HEAD_DIM ** -0.5 * LOG2E), FOX_HEADS),
                                 jnp.tile(fox_k_gain[l], FOX_HEADS),
                                 jnp.ones((FOX_WIDTH,), F32)])
        flag1 = jnp.concatenate([jnp.ones((2 * FOX_WIDTH,), F32), jnp.zeros((FOX_WIDTH,), F32)])
        q_gain = jnp.broadcast_to(nsa_q_gain[l] * (NSA_QK_DIM ** -0.5 * LOG2E),
                                  (NSA_HEADS, NSA_QK_DIM))
        q_table = jnp.broadcast_to(
            jnp.concatenate([q_gain, q_spare], axis=1).reshape(NSA_HEADS * NSA_QK_PAD, 1),
            (NSA_HEADS * NSA_QK_PAD, LANES))
        gain_k = jnp.concatenate([jnp.tile(_pad_gain(nsa_ks_gain[l]), NSA_KV_GROUPS),
                                  jnp.tile(_pad_gain(nsa_kw_gain[l]), NSA_KV_GROUPS)])
        n_small = FOX_HEADS + 3 * NSA_HEADS
        w3 = jnp.concatenate([_pad_head_rows(wt[row["kc"]:row["vc"]], NSA_KV_GROUPS),
                              wt[row["vc"]:row["ks"]], wt[row["fl"]:row["nq"]],
                              wt[row["ng"]:row["ga"]],
                              jnp.zeros((LANES - n_small, d), F32)], axis=0)

        xn, p3 = _rmsnorm_project(xf, norm_attn[l], w3)
        q_tile = NSA_HPG * NSA_QK_DIM
        p1 = _project(xn, wt, list(range(0, 3 * FOX_WIDTH, FOX_PROJ_TN)), FOX_PROJ_TN, gain1,
                      flag1, PROJ_TM, "proj_fox", group=FOX_HEAD_DIM)
        pq = _project_qt(xn, wt, [row["nq"], row["nq"] + q_tile], q_tile, q_table, PROJ_TM, seq,
                         "proj_nsa_q")

        bias_row = jnp.pad(fox_f_bias[l], (0, LANES - FOX_HEADS)).reshape(1, LANES)
        ccol, crow = _decay(p3, bias_row, batch, seq)
        o_a = _fox_attention(p1, ccol, crow.reshape(batch, FOX_HEADS, 1, seq), batch, seq)

        pad_d = NSA_QK_PAD - NSA_QK_DIM
        pek = jnp.pad(cmp_pe_k[l], ((0, 0), (0, pad_d)))
        w1k = jnp.pad(cmp_w1_k[l].reshape(CMP_BLOCK, NSA_QK_DIM, CMP_HIDDEN),
                      ((0, 0), (0, pad_d), (0, 0))).astype(BF16)
        w2k = jnp.pad(cmp_w2_k[l], ((0, 0), (0, pad_d))).astype(BF16)
        gk = _pad_gain(nsa_kc_gain[l]).reshape(1, NSA_QK_PAD)
        w1v = cmp_w1_v[l].reshape(CMP_BLOCK, NSA_V_DIM, CMP_HIDDEN).astype(BF16)
        w2v = cmp_w2_v[l].astype(BF16)
        kc, vct = _compress(p3, pek, w1k, w2k, gk, cmp_pe_v[l], w1v, w2v, batch, seq)
        sel, ocmp, pk, pv = _nsa_select(pq, p3, kc, vct, ovt, xn, wt, [row["ks"], row["kw"]],
                                        [row["vs"], row["vw"]], gain_k, batch, seq)
        o_b = _nsa_attend(pq, pk, pv, p3, sel, ocmp, batch, seq)

        merged = _merge(xn, o_a, o_b, wt, row["ga"], row["gb"], w_up_fox[l], w_up_nsa[l])
        hres, hn = _out_proj_norm(merged, w_out[l], xf, norm_ffn[l])

        act = _swiglu(hn, w_ffn_gate[l], w_ffn_up[l])
        xf = _matmul_residual(act, w_ffn_down[l], hres, FFN_DOWN_TM, FFN_DOWN_TN, "ffn_down")
    return xf.reshape(batch, seq, d)
```

```python
import functools

import numpy as np
import jax
import jax.numpy as jnp
from jax import lax
from jax.experimental import pallas as pl
from jax.experimental.pallas import tpu as pltpu

F32 = jnp.float32
BF16 = jnp.bfloat16

D_MODEL = 2048
FOX_HEADS = 8
FOX_HEAD_DIM = 128
FOX_WIDTH = FOX_HEADS * FOX_HEAD_DIM
NSA_HEADS = 8
NSA_KV_GROUPS = 2
NSA_HPG = NSA_HEADS // NSA_KV_GROUPS
NSA_QK_DIM = 192
NSA_QK_PAD = 256
NSA_V_DIM = 128
NSA_WIDTH = NSA_HEADS * NSA_V_DIM
CMP_BLOCK = 32
CMP_STRIDE = 16
CMP_HIDDEN = 256
SEL_BLOCK = 64
SEL_TOPK = 16
SEL_LOCAL = 2
FORCE_SCORE = 1.0e4
WINDOW = 512
KV_K = NSA_KV_GROUPS * NSA_QK_DIM
KV_V = NSA_KV_GROUPS * NSA_V_DIM
D_FF = -(-(8 * D_MODEL) // (3 * 256)) * 256
RMS_EPS = 1e-6
IN_SPLITS = (FOX_WIDTH, FOX_WIDTH, FOX_WIDTH, FOX_HEADS,
             NSA_HEADS * NSA_QK_DIM, KV_K, KV_V, KV_K, KV_V, KV_K, KV_V,
             3 * NSA_HEADS, D_MODEL, D_MODEL)

_NT = (((1,), (1,)), ((), ()))

LANES = 128
SUBLANES = 8
BF16_SUBLANES = 16
NEG = -1.0e30
LOG2E = 1.4426950408889634
SLC_TILE = 512
LOGIT_BLOCK = 1
LOGIT_LANE = NSA_QK_DIM - LANES
VMEM_LIMIT = 56 * 1024 * 1024

RMS_TM = 512
PROJ_TM = 1024
FOX_PROJ_TN = 1024
MERGE_TM, MERGE_TN = 512, 512
OUT_PROJ_TM = 512
SWIGLU_TM, SWIGLU_TN = 1024, 512
FFN_DOWN_TM, FFN_DOWN_TN = 512, 512
FOX_TQ = 512
FOX_HEADS_PER_STEP = 4
NSA_SELECT_TQ = 512
NSA_ATTEND_TQ = 256


def _cparams(sem):
    return pltpu.CompilerParams(dimension_semantics=sem, vmem_limit_bytes=VMEM_LIMIT)


def _rms_proj_kernel(x_ref, g_ref, w_ref, xn_ref, p_ref, wb_ref):
    @pl.when(pl.program_id(0) == 0)
    def _():
        _stage_weight(w_ref, wb_ref, True)

    rows = x_ref.shape[0] // EPILOGUE_SPLIT
    for r in range(EPILOGUE_SPLIT):
        sl = slice(r * rows, (r + 1) * rows)
        x = x_ref[sl, :]
        ms = jnp.mean(x * x, axis=-1, keepdims=True)
        xn = (x * lax.rsqrt(ms + RMS_EPS) * g_ref[...]).astype(BF16)
        xn_ref[sl, :] = xn
        p_ref[sl, :] = jnp.dot(xn, wb_ref[...], preferred_element_type=F32)


def _rmsnorm_project(x, gain, wt, tm=RMS_TM):
    m, d = x.shape
    n = wt.shape[0]
    return pl.pallas_call(
        _rms_proj_kernel,
        grid=(m // tm,),
        in_specs=[pl.BlockSpec((tm, d), lambda i: (i, 0)),
                  pl.BlockSpec((1, d), lambda i: (0, 0)),
                  pl.BlockSpec((n, d), lambda i: (0, 0), pipeline_mode=pl.Buffered(1))],
        out_specs=[pl.BlockSpec((tm, d), lambda i: (i, 0)),
                   pl.BlockSpec((tm, n), lambda i: (i, 0))],
        out_shape=[jax.ShapeDtypeStruct((m, d), BF16), jax.ShapeDtypeStruct((m, n), F32)],
        scratch_shapes=[pltpu.VMEM((d, n), BF16)],
        compiler_params=_cparams(("arbitrary",)),
        name="rmsnorm_proj_f32",
    )(x, gain.reshape(1, d), wt)


STAGE_CHUNK = 512
EPILOGUE_SPLIT = 4


def _stage_weight(w_ref, wb_ref, transposed=False, pad_from=0, pad_to=0):
    if not transposed:
        wb_ref[...] = w_ref[...].astype(BF16)
        return
    n_in, k = w_ref.shape
    for c in range(k // STAGE_CHUNK):
        cols = slice(c * STAGE_CHUNK, (c + 1) * STAGE_CHUNK)
        w = w_ref[:, cols]
        if pad_from != pad_to:
            zero = jnp.zeros((pad_to - pad_from, STAGE_CHUNK), F32)
            w = jnp.concatenate(
                [piece for h in range(n_in // pad_from)
                 for piece in (w[h * pad_from:(h + 1) * pad_from, :], zero)], axis=0)
        wb_ref[cols, :] = w.T.astype(BF16)


def _first_m_step():
    return pl.program_id(1) == 0


AUG_LANE = NSA_QK_DIM - LANES


def _key_aug(pos):
    lane = lax.broadcasted_iota(jnp.int32, pos.shape, 1)
    hi = (pos >> 8).astype(F32)
    lo = (pos & 255).astype(F32)
    return jnp.where(lane < AUG_LANE, 0.0,
                     jnp.where(lane < AUG_LANE + 3, hi,
                               jnp.where(lane < AUG_LANE + 6, lo,
                                         jnp.where(lane < AUG_LANE + 9, 1.0, 0.0))))


def _proj_kernel(a_ref, w_ref, gain_ref, flag_ref, o_ref, wb_ref, *, group):
    @pl.when(_first_m_step())
    def _():
        _stage_weight(w_ref, wb_ref, True)

    rows = a_ref.shape[0] // EPILOGUE_SPLIT
    ys = [jnp.dot(a_ref[r * rows:(r + 1) * rows, :], wb_ref[...], preferred_element_type=F32)
          for r in range(EPILOGUE_SPLIT)]
    for r, y in enumerate(ys):
        for c in range(y.shape[1] // group):
            sl = slice(c * group, (c + 1) * group)
            yc = y[:, sl]
            ss = jnp.sum(yc * yc, axis=-1, keepdims=True)
            rs = lax.rsqrt(ss * (1.0 / group) + RMS_EPS)
            scale = jnp.where(flag_ref[:, sl] > 0.0, rs, 1.0)
            o_ref[r * rows:(r + 1) * rows, sl] = (
                yc * scale * gain_ref[:, sl]).astype(o_ref.dtype)


def _row_window(starts, rows, k, grid_arg=0):
    def index(*grid):
        j = grid[grid_arg]
        start = starts[-1]
        for t in range(len(starts) - 2, -1, -1):
            start = jnp.where(j == t, starts[t], start)
        return pl.multiple_of(start, SUBLANES), 0
    assert all(s % SUBLANES == 0 for s in starts)
    return pl.BlockSpec((pl.Element(rows), pl.Element(k)), index)


def _project(a, wt, starts, tn, gain, flag, tm, name, group):
    m, k = a.shape
    n_out = len(starts) * tn
    return pl.pallas_call(
        functools.partial(_proj_kernel, group=group),
        grid=(len(starts), m // tm),
        in_specs=[pl.BlockSpec((tm, k), lambda j, i: (i, 0)),
                  _row_window(starts, tn, k),
                  pl.BlockSpec((1, tn), lambda j, i: (0, j)),
                  pl.BlockSpec((1, tn), lambda j, i: (0, j))],
        out_specs=pl.BlockSpec((tm, tn), lambda j, i: (i, j)),
        out_shape=jax.ShapeDtypeStruct((m, n_out), BF16),
        scratch_shapes=[pltpu.VMEM((k, tn), BF16)],
        compiler_params=_cparams(("arbitrary", "arbitrary")),
        name=name,
    )(a, wt, gain.reshape(1, n_out), flag.reshape(1, n_out))


def _proj_qt_kernel(a_ref, w_ref, tab_ref, o_ref, wb_ref, *, head, head_pad, seq):
    @pl.when(_first_m_step())
    def _():
        wb_ref[...] = w_ref[...].astype(BF16)

    tm = a_ref.shape[0]
    cols = tm // EPILOGUE_SPLIT
    reps = cols // LANES
    spare = head_pad - head
    ys = [lax.dot_general(wb_ref[...], a_ref[r * cols:(r + 1) * cols, :], _NT,
                          preferred_element_type=F32) for r in range(EPILOGUE_SPLIT)]
    row = lax.broadcasted_iota(jnp.int32, (spare, cols), 0)
    for r, y in enumerate(ys):
        lanes = slice(r * cols, (r + 1) * cols)
        pos = ((pl.program_id(1) * tm) % seq + r * cols
               + lax.broadcasted_iota(jnp.int32, (1, cols), 1)).astype(F32)
        for h in range(y.shape[0] // head):
            yh = y[h * head:(h + 1) * head, :]
            rs = lax.rsqrt(jnp.sum(yh * yh, axis=0, keepdims=True) * (1.0 / head) + RMS_EPS)
            slot = h * head_pad
            gain = jnp.concatenate([tab_ref[slot:slot + head, :]] * reps, axis=1)
            o_ref[slot:slot + head, lanes] = (yh * rs * gain).astype(o_ref.dtype)
            tab = jnp.concatenate([tab_ref[slot + head:slot + head_pad, :]] * reps, axis=1)
            u = pos * tab
            hi = u.astype(BF16).astype(F32)
            r1 = u - hi
            mid = r1.astype(BF16).astype(F32)
            extra = jnp.where(row < 6, tab,
                              jnp.where(row == 6, hi,
                                        jnp.where(row == 7, mid,
                                                  jnp.where(row == 8, r1 - mid, 0.0))))
            o_ref[slot + head:slot + head_pad, lanes] = extra.astype(o_ref.dtype)


def _project_qt(a, wt, starts, tn_in, table, tm, seq, name):
    m, k = a.shape
    tn_out = tn_in // NSA_QK_DIM * NSA_QK_PAD
    assert seq % tm == 0
    return pl.pallas_call(
        functools.partial(_proj_qt_kernel, head=NSA_QK_DIM, head_pad=NSA_QK_PAD, seq=seq),
        grid=(len(starts), m // tm),
        in_specs=[pl.BlockSpec((tm, k), lambda j, i: (i, 0)),
                  _row_window(starts, tn_in, k),
                  pl.BlockSpec((tn_out, LANES), lambda j, i: (j, 0))],
        out_specs=pl.BlockSpec((tn_out, tm), lambda j, i: (j, i)),
        out_shape=jax.ShapeDtypeStruct((len(starts) * tn_out, m), BF16),
        scratch_shapes=[pltpu.VMEM((tn_in, k), BF16)],
        compiler_params=_cparams(("arbitrary", "arbitrary")),
        name=name,
    )(a, wt, table)


def _out_norm_kernel(a_ref, w_ref, r_ref, g_ref, h_ref, hn_ref, wb_ref):
    @pl.when(pl.program_id(0) == 0)
    def _():
        _stage_weight(w_ref, wb_ref)

    h = r_ref[...] + jnp.dot(a_ref[...], wb_ref[...], preferred_element_type=F32)
    h_ref[...] = h
    ms = jnp.mean(h * h, axis=-1, keepdims=True)
    hn_ref[...] = (h * lax.rsqrt(ms + RMS_EPS) * g_ref[...]).astype(hn_ref.dtype)


def _out_proj_norm(a, w, res, gain, tm=OUT_PROJ_TM):
    m, k = a.shape
    n = w.shape[1]
    full = lambda i: (0, 0)
    row = lambda i: (i, 0)
    return pl.pallas_call(
        _out_norm_kernel,
        grid=(m // tm,),
        in_specs=[pl.BlockSpec((tm, k), row),
                  pl.BlockSpec((k, n), full, pipeline_mode=pl.Buffered(1)),
                  pl.BlockSpec((tm, n), row), pl.BlockSpec((1, n), full)],
        out_specs=[pl.BlockSpec((tm, n), row), pl.BlockSpec((tm, n), row)],
        out_shape=[jax.ShapeDtypeStruct((m, n), F32), jax.ShapeDtypeStruct((m, n), BF16)],
        scratch_shapes=[pltpu.VMEM((k, n), BF16)],
        compiler_params=_cparams(("arbitrary",)),
        name="out_proj_norm",
    )(a, w, res, gain.reshape(1, n))


def _mm_res_kernel(a_ref, w_ref, r_ref, o_ref, wb_ref):
    @pl.when(_first_m_step())
    def _():
        _stage_weight(w_ref, wb_ref)

    o_ref[...] = r_ref[...] + jnp.dot(a_ref[...], wb_ref[...], preferred_element_type=F32)


def _matmul_residual(a, w, res, tm, tn, name):
    m, k = a.shape
    n = w.shape[1]
    return pl.pallas_call(
        _mm_res_kernel,
        grid=(n // tn, m // tm),
        in_specs=[pl.BlockSpec((tm, k), lambda j, i: (i, 0)),
                  pl.BlockSpec((k, tn), lambda j, i: (0, j)),
                  pl.BlockSpec((tm, tn), lambda j, i: (i, j))],
        out_specs=pl.BlockSpec((tm, tn), lambda j, i: (i, j)),
        out_shape=jax.ShapeDtypeStruct((m, n), F32),
        scratch_shapes=[pltpu.VMEM((k, tn), BF16)],
        compiler_params=_cparams(("arbitrary", "arbitrary")),
        name=name,
    )(a, w, res)


def _merge_kernel(xn_ref, oa_ref, ob_ref, wga_ref, wgb_ref, wuf_ref, wun_ref, o_ref,
                  bga_ref, bgb_ref, buf_ref, bun_ref):
    @pl.when(_first_m_step())
    def _():
        _stage_weight(wga_ref, bga_ref, True)
        _stage_weight(wgb_ref, bgb_ref, True)
        _stage_weight(wuf_ref, buf_ref)
        _stage_weight(wun_ref, bun_ref)

    xn = xn_ref[...]
    ga = jax.nn.sigmoid(jnp.dot(xn, bga_ref[...], preferred_element_type=F32))
    ua = jnp.dot(oa_ref[...], buf_ref[...], preferred_element_type=F32)
    acc = ga * ua
    gb = jax.nn.sigmoid(jnp.dot(xn, bgb_ref[...], preferred_element_type=F32))
    ub = jnp.dot(ob_ref[...], bun_ref[...], preferred_element_type=F32)
    o_ref[...] = (acc + gb * ub).astype(o_ref.dtype)


def _merge(xn, oa, ob, wt, row_a, row_b, wuf, wun, tm=MERGE_TM, tn=MERGE_TN):
    m, d = xn.shape
    n = wuf.shape[1]
    ka = oa.shape[1]
    kb = ob.shape[1]
    row = lambda j, i: (i, 0)
    col = lambda j, i: (0, j)
    return pl.pallas_call(
        _merge_kernel,
        grid=(n // tn, m // tm),
        in_specs=[pl.BlockSpec((tm, d), row), pl.BlockSpec((tm, ka), row),
                  pl.BlockSpec((tm, kb), row),
                  _row_window([row_a + t * tn for t in range(n // tn)], tn, d),
                  _row_window([row_b + t * tn for t in range(n // tn)], tn, d),
                  pl.BlockSpec((ka, tn), col), pl.BlockSpec((kb, tn), col)],
        out_specs=pl.BlockSpec((tm, tn), lambda j, i: (i, j)),
        out_shape=jax.ShapeDtypeStruct((m, n), BF16),
        scratch_shapes=[pltpu.VMEM((d, tn), BF16), pltpu.VMEM((d, tn), BF16),
                        pltpu.VMEM((ka, tn), BF16), pltpu.VMEM((kb, tn), BF16)],
        compiler_params=_cparams(("arbitrary", "arbitrary")),
        name="gated_merge",
    )(xn, oa, ob, wt, wt, wuf, wun)


def _swiglu_kernel(a_ref, wg_ref, wu_ref, o_ref, bg_ref, bu_ref):
    @pl.when(_first_m_step())
    def _():
        _stage_weight(wg_ref, bg_ref)
        _stage_weight(wu_ref, bu_ref)

    a = a_ref[...]
    gt = jnp.dot(a, bg_ref[...], preferred_element_type=F32)
    up = jnp.dot(a, bu_ref[...], preferred_element_type=F32)
    o_ref[...] = (gt * jax.nn.sigmoid(gt) * up).astype(o_ref.dtype)


def _swiglu(a, wg, wu, tm=SWIGLU_TM, tn=SWIGLU_TN):
    m, k = a.shape
    n = wg.shape[1]
    return pl.pallas_call(
        _swiglu_kernel,
        grid=(n // tn, m // tm),
        in_specs=[pl.BlockSpec((tm, k), lambda j, i: (i, 0)),
                  pl.BlockSpec((k, tn), lambda j, i: (0, j)),
                  pl.BlockSpec((k, tn), lambda j, i: (0, j))],
        out_specs=pl.BlockSpec((tm, tn), lambda j, i: (i, j)),
        out_shape=jax.ShapeDtypeStruct((m, n), BF16),
        scratch_shapes=[pltpu.VMEM((k, tn), BF16), pltpu.VMEM((k, tn), BF16)],
        compiler_params=_cparams(("arbitrary", "arbitrary")),
        name="swiglu_up",
    )(a, wg, wu)


def _split3(x):
    hi = x.astype(BF16)
    r1 = x - hi.astype(F32)
    mid = r1.astype(BF16)
    lo = (r1 - mid.astype(F32)).astype(BF16)
    return hi, mid, lo


def _decay_kernel(z_ref, b_ref, ccol_ref, crow_ref, *, blk):
    t = z_ref.shape[0]
    r = lax.broadcasted_iota(jnp.int32, (blk, blk), 0)
    c = lax.broadcasted_iota(jnp.int32, (blk, blk), 1)
    tri = jnp.where(r >= c, 1.0, 0.0).astype(BF16)
    carry = jnp.zeros((1, LANES), F32)
    for s in range(t // blk):
        rows = slice(s * blk, (s + 1) * blk)
        z = z_ref[rows, :] + b_ref[...]
        logf = (jnp.minimum(z, 0.0) - jnp.log1p(jnp.exp(-jnp.abs(z)))) * LOG2E
        hi, mid, lo = _split3(logf)
        cb = (jnp.dot(tri, hi, preferred_element_type=F32)
              + jnp.dot(tri, mid, preferred_element_type=F32)
              + jnp.dot(tri, lo, preferred_element_type=F32)) + carry
        carry = cb[blk - 1:blk, :]
        ccol_ref[rows, :] = cb
        crow_ref[0, :, rows] = cb.T[LOGIT_LANE:LOGIT_LANE + FOX_HEADS, :]


def _decay(p3, bias_row, batch, seq, blk=256):
    return pl.pallas_call(
        functools.partial(_decay_kernel, blk=blk),
        grid=(batch,),
        in_specs=[pl.BlockSpec((seq, LANES), lambda b: (b, LOGIT_BLOCK)),
                  pl.BlockSpec((1, LANES), lambda b: (0, 0))],
        out_specs=[pl.BlockSpec((seq, LANES), lambda b: (b, 0)),
                   pl.BlockSpec((1, FOX_HEADS, seq), lambda b: (b, 0, 0))],
        out_shape=[jax.ShapeDtypeStruct((batch * seq, LANES), F32),
                   jax.ShapeDtypeStruct((batch, FOX_HEADS, seq), F32)],
        compiler_params=_cparams(("arbitrary",)),
        name="fox_decay_cumsum",
    )(p3, bias_row)


def _transpose_bf16(x):
    return x.astype(F32).T.astype(BF16)


def _normalised(acc, d):
    return acc[:d, :] * (1.0 / jnp.maximum(acc[d:d + 1, :], 1e-30))


def _fox_kernel(q_ref, k_ref, v_ref, ccol_ref, crow_ref, o_ref, vt_ref, ka_ref, *, tq, nq, nh):
    hh = pl.program_id(1)
    i = pl.program_id(2)
    dh = FOX_HEAD_DIM

    def pieces(x, lane, first):
        hi = x.astype(BF16).astype(F32)
        r1 = x - hi
        mid = r1.astype(BF16).astype(F32)
        return jnp.where(lane == first, hi,
                         jnp.where(lane == first + 1, mid,
                                   jnp.where(lane == first + 2, r1 - mid, 0.0)))

    @pl.when(i == 0)
    def _():
        cc = ccol_ref[...]
        lane = lax.broadcasted_iota(jnp.int32, cc.shape, 1)
        for h in range(nh):
            vt_ref[h, 0:dh, :] = _transpose_bf16(v_ref[:, h * dh:(h + 1) * dh])
            vt_ref[h, dh:, :] = jnp.ones((vt_ref.shape[1] - dh, vt_ref.shape[2]), BF16)
            cj = jnp.sum(jnp.where(lane == LOGIT_LANE + hh * nh + h, cc, 0.0), axis=-1,
                         keepdims=True)
            extra = pieces(-cj, lane, 0) + jnp.where(lane < 3, 0.0, jnp.where(lane < 6, 1.0, 0.0))
            ka_ref[h, :, 0:dh] = k_ref[:, h * dh:(h + 1) * dh]
            ka_ref[h, :, dh:] = extra.astype(BF16)

    row = lax.broadcasted_iota(jnp.int32, (dh, tq), 0)
    qts = []
    for h in range(nh):
        ci = crow_ref[0, h]
        extra = pieces(ci, row, 3) + jnp.where(row < 3, 1.0, 0.0)
        qts.append(jnp.concatenate(
            [_transpose_bf16(q_ref[:, h * dh:(h + 1) * dh]), extra.astype(BF16)], axis=0))
    rk = lax.broadcasted_iota(jnp.int32, (tq, tq), 0)
    cq = lax.broadcasted_iota(jnp.int32, (tq, tq), 1)

    def scores(h, k0):
        return jnp.dot(ka_ref[h, k0:k0 + tq, :], qts[h], preferred_element_type=F32)

    def variant(n):
        def run():
            starts = [t * tq for t in range(n, -1, -1)]
            tiles = [[jnp.where(rk <= cq, scores(h, starts[0]), NEG)]
                     + [scores(h, k0) for k0 in starts[1:]] for h in range(nh)]
            ms = [jnp.max(tiles[h][0], axis=0, keepdims=True) for h in range(nh)]
            accs = [jnp.dot(vt_ref[h, :, starts[0]:starts[0] + tq],
                            jnp.exp2(tiles[h][0] - ms[h]).astype(BF16),
                            preferred_element_type=F32) for h in range(nh)]
            for t in range(1, n + 1):
                k0 = starts[t]
                for h in range(nh):
                    st = tiles[h][t]
                    m_new = jnp.maximum(ms[h], jnp.max(st, axis=0, keepdims=True))
                    accs[h] = jnp.exp2(ms[h] - m_new) * accs[h] + jnp.dot(
                        vt_ref[h, :, k0:k0 + tq], jnp.exp2(st - m_new).astype(BF16),
                        preferred_element_type=F32)
                    ms[h] = m_new
            for h in range(nh):
                o_ref[:, h * dh:(h + 1) * dh] = _normalised(accs[h], dh).T.astype(o_ref.dtype)
        return run

    lax.switch(i, [variant(n) for n in range(nq)])


def _fox_attention(p1, ccol, crow4, batch, seq, tq=FOX_TQ, nh=FOX_HEADS_PER_STEP):
    nq = seq // tq
    hg = FOX_HEADS // nh
    w = nh * FOX_HEAD_DIM
    return pl.pallas_call(
        functools.partial(_fox_kernel, tq=tq, nq=nq, nh=nh),
        grid=(batch, hg, nq),
        in_specs=[pl.BlockSpec((tq, w), lambda b, hh, i: (b * nq + i, hh)),
                  pl.BlockSpec((seq, w), lambda b, hh, i: (b, hg + hh)),
                  pl.BlockSpec((seq, w), lambda b, hh, i: (b, 2 * hg + hh)),
                  pl.BlockSpec((seq, LANES), lambda b, hh, i: (b, 0)),
                  pl.BlockSpec((1, nh, 1, tq), lambda b, hh, i: (b, hh, 0, i))],
        out_specs=pl.BlockSpec((tq, w), lambda b, hh, i: (b * nq + i, hh)),
        out_shape=jax.ShapeDtypeStruct((batch * seq, FOX_WIDTH), BF16),
        scratch_shapes=[pltpu.VMEM((nh, FOX_HEAD_DIM + BF16_SUBLANES, seq), BF16),
                        pltpu.VMEM((nh, seq, 2 * FOX_HEAD_DIM), BF16)],
        compiler_params=_cparams(("arbitrary", "arbitrary", "arbitrary")),
        name="fox_attention",
    )(p1, p1, p1, ccol, crow4)


def _compress_one(z_refs, pe_ref, w1_ref, w2_ref, nblk):
    half = CMP_BLOCK // 2
    first = jnp.zeros((nblk, CMP_HIDDEN), F32)
    second = jnp.zeros((nblk, CMP_HIDDEN), F32)
    for p in range(half):
        rows = pl.ds(p, nblk, stride=CMP_STRIDE)
        zp = [z_ref[rows, :] for z_ref in z_refs]
        zp = zp[0] if len(zp) == 1 else jnp.concatenate(zp, axis=1)
        first += jnp.dot((zp + pe_ref[p:p + 1, :]).astype(BF16), w1_ref[p],
                         preferred_element_type=F32)
        second += jnp.dot((zp + pe_ref[half + p:half + p + 1, :]).astype(BF16),
                          w1_ref[half + p], preferred_element_type=F32)
    hid = first + pltpu.roll(second, nblk - 1, 0)
    act = (hid * jax.nn.sigmoid(hid)).astype(BF16)
    return jnp.dot(act, w2_ref[...], preferred_element_type=F32)


def _compress_kernel(zk0_ref, zk1_ref, zv_ref, pek_ref, w1k_ref, w2k_ref, gk_ref,
                     pev_ref, w1v_ref, w2v_ref, kc_ref, vc_ref, *, nblk):
    kc = _compress_one((zk0_ref, zk1_ref), pek_ref, w1k_ref, w2k_ref, nblk)
    ms = jnp.sum(kc * kc, axis=-1, keepdims=True) * (1.0 / NSA_QK_DIM)
    kc = kc * lax.rsqrt(ms + RMS_EPS) * gk_ref[...]
    pos = CMP_STRIDE * lax.broadcasted_iota(jnp.int32, (nblk, LANES), 0) + (CMP_BLOCK - 1)
    up = NSA_QK_PAD - LANES
    kc_ref[:, :up] = kc[:, :up].astype(kc_ref.dtype)
    kc_ref[:, up:] = (kc[:, up:] + _key_aug(pos)).astype(kc_ref.dtype)
    vc = _compress_one((zv_ref,), pev_ref, w1v_ref, w2v_ref, nblk)
    vc_ref[...] = vc.T.astype(vc_ref.dtype)


def _compress(p3, pek, w1k, w2k, gk, pev, w1v, w2v, batch, seq):
    g = NSA_KV_GROUPS
    nblk = seq // CMP_STRIDE
    full2 = lambda b, gg: (0, 0)
    full3 = lambda b, gg: (0, 0, 0)
    return pl.pallas_call(
        functools.partial(_compress_kernel, nblk=nblk),
        grid=(batch, g),
        in_specs=[pl.BlockSpec((seq, LANES), lambda b, gg: (b, 2 * gg)),
                  pl.BlockSpec((seq, LANES), lambda b, gg: (b, 2 * gg + 1)),
                  pl.BlockSpec((seq, NSA_V_DIM), lambda b, gg: (b, 4 + gg)),
                  pl.BlockSpec(pek.shape, full2), pl.BlockSpec(w1k.shape, full3),
                  pl.BlockSpec(w2k.shape, full2), pl.BlockSpec(gk.shape, full2),
                  pl.BlockSpec(pev.shape, full2), pl.BlockSpec(w1v.shape, full3),
                  pl.BlockSpec(w2v.shape, full2)],
        out_specs=[pl.BlockSpec((nblk, NSA_QK_PAD), lambda b, gg: (b * g + gg, 0)),
                   pl.BlockSpec((NSA_V_DIM, nblk), lambda b, gg: (b * g + gg, 0))],
        out_shape=[jax.ShapeDtypeStruct((batch * g * nblk, NSA_QK_PAD), BF16),
                   jax.ShapeDtypeStruct((batch * g * NSA_V_DIM, nblk), BF16)],
        compiler_params=_cparams(("arbitrary", "arbitrary")),
        name="nsa_compress",
    )(p3, p3, p3, pek, w1k, w2k, gk, pev, w1v, w2v)


def _q_heads_t(qt_ref, first_head):
    return jnp.concatenate(
        [qt_ref[(first_head + hh) * NSA_QK_PAD:(first_head + hh + 1) * NSA_QK_PAD, :]
         for hh in range(NSA_HPG)], axis=1)


def _gate_rows(glt_ref, g, hh):
    base = LOGIT_LANE + FOX_HEADS + (g * NSA_HPG + hh) * 3
    return [glt_ref[pl.ds(base + br, 1), :] for br in range(3)]


def _nsa_select_kernel(q_ref, kc_ref, vct_ref, gl_ref, ovt_ref, xn_ref, wk_ref, wv_ref, gk_ref,
                       sel_ref, ocmp_ref, pk_ref, pv_ref, glt_ref, wkb_ref, wvb_ref,
                       *, tq, n_cmp, n_sel):
    g = pl.program_id(1)
    t0 = pl.program_id(2) * tq
    hpg = NSA_HPG
    dv = NSA_V_DIM

    @pl.when(pl.program_id(2) == 0)
    def _():
        _stage_weight(wk_ref, wkb_ref, True, NSA_QK_DIM, NSA_QK_PAD)
        _stage_weight(wv_ref, wvb_ref, True)

    xn = xn_ref[...]
    yk = jnp.dot(xn, wkb_ref[...], preferred_element_type=F32)
    pv_ref[...] = jnp.dot(xn, wvb_ref[...], preferred_element_type=F32).astype(pv_ref.dtype)
    pos = t0 + lax.broadcasted_iota(jnp.int32, (tq, LANES), 0)
    for c in range(NSA_KV_GROUPS):
        sl = slice(c * NSA_QK_PAD, (c + 1) * NSA_QK_PAD)
        yc = yk[:, sl]
        rs = lax.rsqrt(jnp.sum(yc * yc, axis=-1, keepdims=True) * (1.0 / NSA_QK_DIM) + RMS_EPS)
        out = yc * rs * gk_ref[:, sl]
        up = slice((c + 1) * NSA_QK_PAD - LANES, (c + 1) * NSA_QK_PAD)
        pk_ref[:, c * NSA_QK_PAD:(c + 1) * NSA_QK_PAD - LANES] = (
            out[:, :NSA_QK_PAD - LANES].astype(pk_ref.dtype))
        pk_ref[:, up] = (out[:, NSA_QK_PAD - LANES:] + _key_aug(pos)).astype(pk_ref.dtype)

    q4t = _q_heads_t(q_ref, 0)
    rk = lax.broadcasted_iota(jnp.int32, (LANES, tq), 0)
    cq = lax.broadcasted_iota(jnp.int32, (LANES, tq), 1)

    s_c = jnp.dot(kc_ref[...], q4t, preferred_element_type=F32)
    dist_c = (t0 + cq) - (CMP_STRIDE * rk + (CMP_BLOCK - 1))
    mask_c = jnp.where(rk < n_cmp, dist_c, -1) >= 0
    probs = []
    p_sum = jnp.zeros((LANES, tq), F32)
    for hh in range(hpg):
        sm = jnp.where(mask_c, s_c[:, hh * tq:(hh + 1) * tq], NEG)
        m = jnp.max(sm, axis=0, keepdims=True)
        e = jnp.where(mask_c, jnp.exp2(sm - m), 0.0)
        p = e * (1.0 / jnp.maximum(jnp.sum(e, axis=0, keepdims=True), 1e-30))
        probs.append(p)
        p_sum = p_sum + p
    o_cmp = jnp.dot(vct_ref[...], jnp.concatenate(probs, axis=1).astype(BF16),
                    preferred_element_type=F32)

    ph = p_sum.astype(BF16)
    plo = (p_sum - ph.astype(F32)).astype(BF16)
    ovt = ovt_ref[...]
    imp = (jnp.dot(ovt, ph, preferred_element_type=F32)
           + jnp.dot(ovt, plo, preferred_element_type=F32))[:n_sel, :]

    rj = lax.broadcasted_iota(jnp.int32, (n_sel, tq), 0)
    tcol = t0 + lax.broadcasted_iota(jnp.int32, (n_sel, tq), 1)
    back = (tcol >> (SEL_BLOCK.bit_length() - 1)) - rj
    elig = back >= 0
    forced = jnp.where(rj == 0, 0, jnp.where(elig, back, SEL_LOCAL)) < SEL_LOCAL
    score = jnp.where(elig, jnp.where(forced, FORCE_SCORE, imp), -1.0)
    rank = jnp.zeros((n_sel, tq), F32)
    for jp in range(n_sel):
        row = score[jp:jp + 1, :]
        later = jnp.where(rj > jp, 1.0, 0.0)
        rank = rank + jnp.where(row > score, 1.0, jnp.where(row == score, later, 0.0))
    sel_ref[0:n_sel, :] = jnp.where(elig, jnp.where(rank < SEL_TOPK, 0.0, NEG), NEG)
    sel_ref[n_sel:, :] = jnp.full((LANES - n_sel, tq), NEG, F32)

    glt_ref[...] = jax.nn.sigmoid(gl_ref[...]).T
    for hh in range(hpg):
        gate = _gate_rows(glt_ref, g, hh)[0]
        ocmp_ref[:, hh * dv:(hh + 1) * dv] = (gate * o_cmp[:, hh * tq:(hh + 1) * tq]).T


def _nsa_select(pq, p3, kc, vct, ovt, xn, wt, k_rows, v_rows, gain_k, batch, seq,
                tq=NSA_SELECT_TQ):
    g = NSA_KV_GROUPS
    nq = seq // tq
    nblk = seq // CMP_STRIDE
    n_cmp = nblk - CMP_BLOCK // CMP_STRIDE + 1
    d = xn.shape[1]
    kw_out = g * NSA_QK_PAD
    return pl.pallas_call(
        functools.partial(_nsa_select_kernel, tq=tq, n_cmp=n_cmp, n_sel=seq // SEL_BLOCK),
        grid=(batch, g, nq),
        in_specs=[
            pl.BlockSpec((NSA_HPG * NSA_QK_PAD, tq), lambda b, gg, i: (gg, b * nq + i)),
            pl.BlockSpec((nblk, NSA_QK_PAD), lambda b, gg, i: (b * g + gg, 0)),
            pl.BlockSpec((NSA_V_DIM, nblk), lambda b, gg, i: (b * g + gg, 0)),
            pl.BlockSpec((tq, LANES), lambda b, gg, i: (b * nq + i, LOGIT_BLOCK)),
            pl.BlockSpec(ovt.shape, lambda b, gg, i: (0, 0)),
            pl.BlockSpec((tq, d), lambda b, gg, i: (b * nq + i, 0)),
            _row_window(k_rows, KV_K, d, grid_arg=1),
            _row_window(v_rows, KV_V, d, grid_arg=1),
            pl.BlockSpec((1, kw_out), lambda b, gg, i: (0, gg)),
        ],
        out_specs=[pl.BlockSpec((LANES, tq), lambda b, gg, i: (b * g + gg, i)),
                   pl.BlockSpec((tq, NSA_HPG * NSA_V_DIM), lambda b, gg, i: (b * nq + i, gg)),
                   pl.BlockSpec((tq, kw_out), lambda b, gg, i: (b * nq + i, gg)),
                   pl.BlockSpec((tq, KV_V), lambda b, gg, i: (b * nq + i, gg))],
        out_shape=[jax.ShapeDtypeStruct((batch * g * LANES, seq), F32),
                   jax.ShapeDtypeStruct((batch * seq, NSA_WIDTH), F32),
                   jax.ShapeDtypeStruct((batch * seq, 2 * kw_out), BF16),
                   jax.ShapeDtypeStruct((batch * seq, 2 * KV_V), BF16)],
        scratch_shapes=[pltpu.VMEM((LANES, tq), F32), pltpu.VMEM((d, kw_out), BF16),
                        pltpu.VMEM((d, KV_V), BF16)],
        compiler_params=_cparams(("arbitrary", "arbitrary", "arbitrary")),
        name="nsa_select",
    )(pq, kc, vct, p3, ovt, xn, wt, wt, gain_k.reshape(1, 2 * kw_out))


def _nsa_attend_kernel(q_ref, ks_ref, vs_ref, kw_ref, vw_ref, gl_ref, sel_ref,
                       ocmp_ref, o_ref, vst_ref, vwt_ref, glt_ref, *, tq, n_var):
    i = pl.program_id(1)
    t0 = i * tq
    hpg = NSA_HPG
    ng = NSA_KV_GROUPS
    dv = NSA_V_DIM
    dk = NSA_QK_PAD

    @pl.when(i == 0)
    def _():
        ones = jnp.ones((vst_ref.shape[1] - dv, vst_ref.shape[2]), BF16)
        for gg in range(ng):
            vst_ref[gg, 0:dv, :] = _transpose_bf16(vs_ref[:, gg * dv:(gg + 1) * dv])
            vst_ref[gg, dv:, :] = ones
            vwt_ref[gg, 0:dv, :] = _transpose_bf16(vw_ref[:, gg * dv:(gg + 1) * dv])
            vwt_ref[gg, dv:, :] = ones

    q4t = [_q_heads_t(q_ref, gg * hpg) for gg in range(ng)]
    glt_ref[...] = jax.nn.sigmoid(gl_ref[...]).T

    def distance(k0, rows):
        return (lax.broadcasted_iota(jnp.int32, (rows, tq), 1)
                - lax.broadcasted_iota(jnp.int32, (rows, tq), 0)) + (t0 - k0)

    def scores(gg, k, mask_bias):
        st = jnp.dot(k, q4t[gg], preferred_element_type=F32)
        return st + jnp.concatenate([mask_bias] * hpg, axis=1)

    def selection_bias(gg, k0):
        j0 = gg * LANES + k0 // SEL_BLOCK
        return jnp.concatenate(
            [jnp.broadcast_to(sel_ref[j0 + j:j0 + j + 1, :], (SEL_BLOCK, tq))
             for j in range(SLC_TILE // SEL_BLOCK)], axis=0)

    def variant(n):
        def run():
            kw0 = pl.multiple_of(jnp.maximum(t0 - WINDOW, 0), LANES)
            wrows = WINDOW + tq
            dist_w = distance(kw0, wrows)
            bias_w = jnp.where(jnp.where(dist_w >= 0, dist_w, WINDOW) < WINDOW, 0.0, NEG)
            o_win = []
            for gg in range(ng):
                sw = scores(gg, kw_ref[pl.ds(kw0, wrows), gg * dk:(gg + 1) * dk], bias_w)
                pw = jnp.exp2(sw - jnp.max(sw, axis=0, keepdims=True))
                o_win.append(_normalised(
                    jnp.dot(vwt_ref[gg, :, pl.ds(kw0, wrows)], pw.astype(BF16),
                            preferred_element_type=F32), dv))

            starts = [t * SLC_TILE for t in range(n, -1, -1)]
            causal = distance(starts[0], SLC_TILE) >= 0
            tiles = [[] for _ in range(ng)]
            for t, k0 in enumerate(starts):
                for gg in range(ng):
                    bias = selection_bias(gg, k0)
                    if t == 0:
                        bias = jnp.where(causal, bias, NEG)
                    tiles[gg].append(scores(
                        gg, ks_ref[k0:k0 + SLC_TILE, gg * dk:(gg + 1) * dk], bias))
            ms = [jnp.max(tiles[gg][0], axis=0, keepdims=True) for gg in range(ng)]
            accs = [jnp.dot(vst_ref[gg, :, starts[0]:starts[0] + SLC_TILE],
                            jnp.exp2(tiles[gg][0] - ms[gg]).astype(BF16),
                            preferred_element_type=F32) for gg in range(ng)]
            for t in range(1, n + 1):
                k0 = starts[t]
                for gg in range(ng):
                    st = tiles[gg][t]
                    m_new = jnp.maximum(ms[gg], jnp.max(st, axis=0, keepdims=True))
                    accs[gg] = jnp.exp2(ms[gg] - m_new) * accs[gg] + jnp.dot(
                        vst_ref[gg, :, k0:k0 + SLC_TILE], jnp.exp2(st - m_new).astype(BF16),
                        preferred_element_type=F32)
                    ms[gg] = m_new

            for gg in range(ng):
                o_slc = _normalised(accs[gg], dv)
                for hh in range(hpg):
                    _, g_slc, g_win = _gate_rows(glt_ref, gg, hh)
                    lanes = slice(hh * tq, (hh + 1) * tq)
                    out = g_slc * o_slc[:, lanes] + g_win * o_win[gg][:, lanes]
                    cols = slice((gg * hpg + hh) * dv, (gg * hpg + hh + 1) * dv)
                    o_ref[:, cols] = (ocmp_ref[:, cols] + out.T).astype(o_ref.dtype)
        return run

    lax.switch(t0 // SLC_TILE, [variant(n) for n in range(n_var)])


def _nsa_attend(pq, pk, pv, p3, sel, ocmp, batch, seq):
    tq = NSA_ATTEND_TQ
    g = NSA_KV_GROUPS
    nq = seq // tq
    vrows = NSA_V_DIM + BF16_SUBLANES
    return pl.pallas_call(
        functools.partial(_nsa_attend_kernel, tq=tq, n_var=seq // SLC_TILE),
        grid=(batch, nq),
        in_specs=[
            pl.BlockSpec((NSA_HEADS * NSA_QK_PAD, tq), lambda b, i: (0, b * nq + i)),
            pl.BlockSpec((seq, g * NSA_QK_PAD), lambda b, i: (b, 0)),
            pl.BlockSpec((seq, g * NSA_V_DIM), lambda b, i: (b, 0)),
            pl.BlockSpec((seq, g * NSA_QK_PAD), lambda b, i: (b, 1)),
            pl.BlockSpec((seq, g * NSA_V_DIM), lambda b, i: (b, 1)),
            pl.BlockSpec((tq, LANES), lambda b, i: (b * nq + i, LOGIT_BLOCK)),
            pl.BlockSpec((g * LANES, tq), lambda b, i: (b, i)),
            pl.BlockSpec((tq, NSA_WIDTH), lambda b, i: (b * nq + i, 0)),
        ],
        out_specs=pl.BlockSpec((tq, NSA_WIDTH), lambda b, i: (b * nq + i, 0)),
        out_shape=jax.ShapeDtypeStruct((batch * seq, NSA_WIDTH), BF16),
        scratch_shapes=[pltpu.VMEM((g, vrows, seq), BF16), pltpu.VMEM((g, vrows, seq), BF16),
                        pltpu.VMEM((LANES, tq), F32)],
        compiler_params=_cparams(("arbitrary", "arbitrary")),
        name="nsa_attend",
    )(pq, pk, pv, pk, pv, p3, sel, ocmp)


def _pad_gain(gain, scale=1.0):
    return jnp.pad(gain * scale, (0, NSA_QK_PAD - NSA_QK_DIM))


def _overlap_matrix(nc, ns):
    i = np.arange(nc)[:, None]
    j = np.arange(ns)[None, :]
    lo = np.maximum(i * CMP_STRIDE, j * SEL_BLOCK)
    hi = np.minimum(i * CMP_STRIDE + CMP_BLOCK, (j + 1) * SEL_BLOCK)
    return (np.maximum(hi - lo, 0) / CMP_STRIDE).astype(np.float32)


def kernel(x, norm_attn, w_in, fox_f_bias, fox_q_gain, fox_k_gain,
           nsa_q_gain, nsa_kc_gain, nsa_ks_gain, nsa_kw_gain,
           cmp_pe_k, cmp_w1_k, cmp_w2_k, cmp_pe_v, cmp_w1_v, cmp_w2_v,
           w_up_fox, w_up_nsa, w_out, norm_ffn, w_ffn_gate, w_ffn_up, w_ffn_down):
    batch, seq, d = x.shape
    m = batch * seq
    depth = w_in.shape[0]
    pts = [0] + [int(p) for p in np.cumsum(IN_SPLITS)]
    nblk = seq // CMP_STRIDE
    n_cmp = nblk - CMP_BLOCK // CMP_STRIDE + 1
    ns = seq // SEL_BLOCK

    slope = jnp.exp2(-8.0 * jnp.arange(1, NSA_HEADS + 1, dtype=F32) / NSA_HEADS) * LOG2E
    s1, s2, s3 = [p.astype(F32) for p in _split3(slope)]
    q_spare = jnp.stack([256.0 * s1, 256.0 * s2, 256.0 * s3, s1, s2, s3, -slope, -slope, -slope],
                        axis=1)
    q_spare = jnp.pad(q_spare, ((0, 0), (0, NSA_QK_PAD - NSA_QK_DIM - 9)))
    ovt_np = np.zeros((LANES, nblk), np.float32)
    ovt_np[:ns, :n_cmp] = _overlap_matrix(n_cmp, ns).T
    ovt = jnp.asarray(ovt_np, BF16)

    w_in_t = jnp.swapaxes(w_in, 1, 2)

    xf = x.reshape(m, d)
    for l in range(depth):
        wt = w_in_t[l]
        row = dict(zip(("fq", "fk", "fv", "fl", "nq", "kc", "vc", "ks", "vs", "kw", "vw", "ng",
                        "ga", "gb"), pts))

        gain1 = jnp.concatenate([jnp.tile(fox_q_gain[l] * (FOX_HEAD_DIM ** -0.5 * LOG2E), FOX_HEADS),
                                 jnp.tile(fox_k_gain[l], FOX_HEADS),
                                 jnp.ones((FOX_WIDTH,), F32)])
        flag1 = jnp.concatenate([jnp.ones((2 * FOX_WIDTH,), F32), jnp.zeros((FOX_WIDTH,), F32)])
        q_gain = jnp.broadcast_to(nsa_q_gain[l] * (NSA_QK_DIM ** -0.5 * LOG2E),
                                  (NSA_HEADS, NSA_QK_DIM))
        q_table = jnp.broadcast_to(
            jnp.concatenate([q_gain, q_spare], axis=1).reshape(NSA_HEADS * NSA_QK_PAD, 1),
            (NSA_HEADS * NSA_QK_PAD, LANES))
        gain_k = jnp.concatenate([jnp.tile(_pad_gain(nsa_ks_gain[l]), NSA_KV_GROUPS),
                                  jnp.tile(_pad_gain(nsa_kw_gain[l]), NSA_KV_GROUPS)])
        pad_d = NSA_QK_PAD - NSA_QK_DIM
        n_small = FOX_HEADS + 3 * NSA_HEADS
        w3 = jnp.concatenate([wt[row["kc"]:row["kc"] + NSA_QK_DIM],
                              wt[row["fl"]:row["nq"]], wt[row["ng"]:row["ga"]],
                              jnp.zeros((pad_d - n_small, d), F32),
                              wt[row["kc"] + NSA_QK_DIM:row["vc"]], jnp.zeros((pad_d, d), F32),
                              wt[row["vc"]:row["ks"]]], axis=0)

        xn, p3 = _rmsnorm_project(xf, norm_attn[l], w3)
        q_tile = NSA_HPG * NSA_QK_DIM
        p1 = _project(xn, wt, list(range(0, 3 * FOX_WIDTH, FOX_PROJ_TN)), FOX_PROJ_TN, gain1,
                      flag1, PROJ_TM, "proj_fox", group=FOX_HEAD_DIM)
        pq = _project_qt(xn, wt, [row["nq"], row["nq"] + q_tile], q_tile, q_table, PROJ_TM, seq,
                         "proj_nsa_q")

        bias_row = jnp.pad(fox_f_bias[l], (LOGIT_LANE, LANES - LOGIT_LANE - FOX_HEADS)).reshape(1, LANES)
        ccol, crow = _decay(p3, bias_row, batch, seq)
        o_a = _fox_attention(p1, ccol, crow.reshape(batch, FOX_HEADS, 1, seq), batch, seq)

        pek = jnp.pad(cmp_pe_k[l], ((0, 0), (0, pad_d)))
        w1k = jnp.pad(cmp_w1_k[l].reshape(CMP_BLOCK, NSA_QK_DIM, CMP_HIDDEN),
                      ((0, 0), (0, pad_d), (0, 0))).astype(BF16)
        w2k = jnp.pad(cmp_w2_k[l], ((0, 0), (0, pad_d))).astype(BF16)
        gk = _pad_gain(nsa_kc_gain[l]).reshape(1, NSA_QK_PAD)
        w1v = cmp_w1_v[l].reshape(CMP_BLOCK, NSA_V_DIM, CMP_HIDDEN).astype(BF16)
        w2v = cmp_w2_v[l].astype(BF16)
        kc, vct = _compress(p3, pek, w1k, w2k, gk, cmp_pe_v[l], w1v, w2v, batch, seq)
        sel, ocmp, pk, pv = _nsa_select(pq, p3, kc, vct, ovt, xn, wt, [row["ks"], row["kw"]],
                                        [row["vs"], row["vw"]], gain_k, batch, seq)
        o_b = _nsa_attend(pq, pk, pv, p3, sel, ocmp, batch, seq)

        merged = _merge(xn, o_a, o_b, wt, row["ga"], row["gb"], w_up_fox[l], w_up_nsa[l])
        hres, hn = _out_proj_norm(merged, w_out[l], xf, norm_ffn[l])

        act = _swiglu(hn, w_ffn_gate[l], w_ffn_up[l])
        xf = _matmul_residual(act, w_ffn_down[l], hres, FFN_DOWN_TM, FFN_DOWN_TN, "ffn_down")
    return xf.reshape(batch, seq, d)
```

```python
import functools

import numpy as np
import jax
import jax.numpy as jnp
from jax import lax
from jax.experimental import pallas as pl
from jax.experimental.pallas import tpu as pltpu

F32 = jnp.float32
BF16 = jnp.bfloat16

D_MODEL = 2048
FOX_HEADS = 8
FOX_HEAD_DIM = 128
FOX_WIDTH = FOX_HEADS * FOX_HEAD_DIM
NSA_HEADS = 8
NSA_KV_GROUPS = 2
NSA_HPG = NSA_HEADS // NSA_KV_GROUPS
NSA_QK_DIM = 192
NSA_QK_PAD = 256
NSA_V_DIM = 128
NSA_WIDTH = NSA_HEADS * NSA_V_DIM
CMP_BLOCK = 32
CMP_STRIDE = 16
CMP_HIDDEN = 256
SEL_BLOCK = 64
SEL_TOPK = 16
SEL_LOCAL = 2
FORCE_SCORE = 1.0e4
WINDOW = 512
KV_K = NSA_KV_GROUPS * NSA_QK_DIM
KV_V = NSA_KV_GROUPS * NSA_V_DIM
D_FF = -(-(8 * D_MODEL) // (3 * 256)) * 256
RMS_EPS = 1e-6
IN_SPLITS = (FOX_WIDTH, FOX_WIDTH, FOX_WIDTH, FOX_HEADS,
             NSA_HEADS * NSA_QK_DIM, KV_K, KV_V, KV_K, KV_V, KV_K, KV_V,
             3 * NSA_HEADS, D_MODEL, D_MODEL)

_NT = (((1,), (1,)), ((), ()))

LANES = 128
SUBLANES = 8
BF16_SUBLANES = 16
NEG = -1.0e30
LOG2E = 1.4426950408889634
SLC_TILE = 512
LOGIT_BLOCK = 1
LOGIT_LANE = NSA_QK_DIM - LANES
VMEM_LIMIT = 56 * 1024 * 1024

RMS_TM = 512
PROJ_TM = 1024
FOX_PROJ_TN = 1024
MERGE_TM, MERGE_TN = 512, 512
OUT_PROJ_TM = 512
SWIGLU_TM, SWIGLU_TN = 1024, 512
FFN_DOWN_TM, FFN_DOWN_TN = 512, 512
FOX_TQ = 512
FOX_HEADS_PER_STEP = 4
NSA_SELECT_TQ = 512
NSA_ATTEND_TQ = 256


def _cparams(sem):
    return pltpu.CompilerParams(dimension_semantics=sem, vmem_limit_bytes=VMEM_LIMIT)


def _rms_proj_kernel(x_ref, g_ref, w_ref, xn_ref, p_ref, wb_ref):
    @pl.when(pl.program_id(0) == 0)
    def _():
        _stage_weight(w_ref, wb_ref, True)

    rows = x_ref.shape[0] // EPILOGUE_SPLIT
    for r in range(EPILOGUE_SPLIT):
        sl = slice(r * rows, (r + 1) * rows)
        x = x_ref[sl, :]
        ms = jnp.mean(x * x, axis=-1, keepdims=True)
        xn = (x * lax.rsqrt(ms + RMS_EPS) * g_ref[...]).astype(BF16)
        xn_ref[sl, :] = xn
        p_ref[sl, :] = jnp.dot(xn, wb_ref[...], preferred_element_type=F32)


def _rmsnorm_project(x, gain, wt, tm=RMS_TM):
    m, d = x.shape
    n = wt.shape[0]
    return pl.pallas_call(
        _rms_proj_kernel,
        grid=(m // tm,),
        in_specs=[pl.BlockSpec((tm, d), lambda i: (i, 0)),
                  pl.BlockSpec((1, d), lambda i: (0, 0)),
                  pl.BlockSpec((n, d), lambda i: (0, 0), pipeline_mode=pl.Buffered(1))],
        out_specs=[pl.BlockSpec((tm, d), lambda i: (i, 0)),
                   pl.BlockSpec((tm, n), lambda i: (i, 0))],
        out_shape=[jax.ShapeDtypeStruct((m, d), BF16), jax.ShapeDtypeStruct((m, n), F32)],
        scratch_shapes=[pltpu.VMEM((d, n), BF16)],
        compiler_params=_cparams(("arbitrary",)),
        name="rmsnorm_proj_f32",
    )(x, gain.reshape(1, d), wt)


STAGE_CHUNK = 512
EPILOGUE_SPLIT = 4


def _stage_weight(w_ref, wb_ref, transposed=False, pad_from=0, pad_to=0):
    if not transposed:
        wb_ref[...] = w_ref[...].astype(BF16)
        return
    n_in, k = w_ref.shape
    for c in range(k // STAGE_CHUNK):
        cols = slice(c * STAGE_CHUNK, (c + 1) * STAGE_CHUNK)
        w = w_ref[:, cols]
        if pad_from != pad_to:
            zero = jnp.zeros((pad_to - pad_from, STAGE_CHUNK), F32)
            w = jnp.concatenate(
                [piece for h in range(n_in // pad_from)
                 for piece in (w[h * pad_from:(h + 1) * pad_from, :], zero)], axis=0)
        wb_ref[cols, :] = w.T.astype(BF16)


def _first_m_step():
    return pl.program_id(1) == 0


AUG_LANE = NSA_QK_DIM - LANES


def _key_aug(pos):
    lane = lax.broadcasted_iota(jnp.int32, pos.shape, 1)
    hi = (pos >> 8).astype(F32)
    lo = (pos & 255).astype(F32)
    return jnp.where(lane < AUG_LANE, 0.0,
                     jnp.where(lane < AUG_LANE + 3, hi,
                               jnp.where(lane < AUG_LANE + 6, lo,
                                         jnp.where(lane < AUG_LANE + 9, 1.0, 0.0))))


def _proj_kernel(a_ref, w_ref, gain_ref, flag_ref, o_ref, wb_ref, *, group):
    @pl.when(_first_m_step())
    def _():
        _stage_weight(w_ref, wb_ref, True)

    rows = a_ref.shape[0] // EPILOGUE_SPLIT
    ys = [jnp.dot(a_ref[r * rows:(r + 1) * rows, :], wb_ref[...], preferred_element_type=F32)
          for r in range(EPILOGUE_SPLIT)]
    for r, y in enumerate(ys):
        for c in range(y.shape[1] // group):
            sl = slice(c * group, (c + 1) * group)
            yc = y[:, sl]
            ss = jnp.sum(yc * yc, axis=-1, keepdims=True)
            rs = lax.rsqrt(ss * (1.0 / group) + RMS_EPS)
            scale = jnp.where(flag_ref[:, sl] > 0.0, rs, 1.0)
            o_ref[r * rows:(r + 1) * rows, sl] = (
                yc * scale * gain_ref[:, sl]).astype(o_ref.dtype)


def _row_window(starts, rows, k, grid_arg=0):
    def index(*grid):
        j = grid[grid_arg]
        start = starts[-1]
        for t in range(len(starts) - 2, -1, -1):
            start = jnp.where(j == t, starts[t], start)
        return pl.multiple_of(start, SUBLANES), 0
    assert all(s % SUBLANES == 0 for s in starts)
    return pl.BlockSpec((pl.Element(rows), pl.Element(k)), index)


def _project(a, wt, starts, tn, gain, flag, tm, name, group):
    m, k = a.shape
    n_out = len(starts) * tn
    return pl.pallas_call(
        functools.partial(_proj_kernel, group=group),
        grid=(len(starts), m // tm),
        in_specs=[pl.BlockSpec((tm, k), lambda j, i: (i, 0)),
                  _row_window(starts, tn, k),
                  pl.BlockSpec((1, tn), lambda j, i: (0, j)),
                  pl.BlockSpec((1, tn), lambda j, i: (0, j))],
        out_specs=pl.BlockSpec((tm, tn), lambda j, i: (i, j)),
        out_shape=jax.ShapeDtypeStruct((m, n_out), BF16),
        scratch_shapes=[pltpu.VMEM((k, tn), BF16)],
        compiler_params=_cparams(("arbitrary", "arbitrary")),
        name=name,
    )(a, wt, gain.reshape(1, n_out), flag.reshape(1, n_out))


def _proj_qt_kernel(a_ref, w_ref, tab_ref, o_ref, wb_ref, *, head, head_pad, seq):
    @pl.when(_first_m_step())
    def _():
        wb_ref[...] = w_ref[...].astype(BF16)

    tm = a_ref.shape[0]
    cols = tm // EPILOGUE_SPLIT
    reps = cols // LANES
    spare = head_pad - head
    ys = [lax.dot_general(wb_ref[...], a_ref[r * cols:(r + 1) * cols, :], _NT,
                          preferred_element_type=F32) for r in range(EPILOGUE_SPLIT)]
    row = lax.broadcasted_iota(jnp.int32, (spare, cols), 0)
    for r, y in enumerate(ys):
        lanes = slice(r * cols, (r + 1) * cols)
        pos = ((pl.program_id(1) * tm) % seq + r * cols
               + lax.broadcasted_iota(jnp.int32, (1, cols), 1)).astype(F32)
        for h in range(y.shape[0] // head):
            yh = y[h * head:(h + 1) * head, :]
            rs = lax.rsqrt(jnp.sum(yh * yh, axis=0, keepdims=True) * (1.0 / head) + RMS_EPS)
            slot = h * head_pad
            gain = jnp.concatenate([tab_ref[slot:slot + head, :]] * reps, axis=1)
            o_ref[slot:slot + head, lanes] = (yh * rs * gain).astype(o_ref.dtype)
            tab = jnp.concatenate([tab_ref[slot + head:slot + head_pad, :]] * reps, axis=1)
            u = pos * tab
            hi = u.astype(BF16).astype(F32)
            r1 = u - hi
            mid = r1.astype(BF16).astype(F32)
            extra = jnp.where(row < 6, tab,
                              jnp.where(row == 6, hi,
                                        jnp.where(row == 7, mid,
                                                  jnp.where(row == 8, r1 - mid, 0.0))))
            o_ref[slot + head:slot + head_pad, lanes] = extra.astype(o_ref.dtype)


def _project_qt(a, wt, starts, tn_in, table, tm, seq, name):
    m, k = a.shape
    tn_out = tn_in // NSA_QK_DIM * NSA_QK_PAD
    assert seq % tm == 0
    return pl.pallas_call(
        functools.partial(_proj_qt_kernel, head=NSA_QK_DIM, head_pad=NSA_QK_PAD, seq=seq),
        grid=(len(starts), m // tm),
        in_specs=[pl.BlockSpec((tm, k), lambda j, i: (i, 0)),
                  _row_window(starts, tn_in, k),
                  pl.BlockSpec((tn_out, LANES), lambda j, i: (j, 0))],
        out_specs=pl.BlockSpec((tn_out, tm), lambda j, i: (j, i)),
        out_shape=jax.ShapeDtypeStruct((len(starts) * tn_out, m), BF16),
        scratch_shapes=[pltpu.VMEM((tn_in, k), BF16)],
        compiler_params=_cparams(("arbitrary", "arbitrary")),
        name=name,
    )(a, wt, table)


def _out_norm_kernel(a_ref, w_ref, r_ref, g_ref, h_ref, hn_ref, wb_ref):
    @pl.when(pl.program_id(0) == 0)
    def _():
        _stage_weight(w_ref, wb_ref)

    h = r_ref[...] + jnp.dot(a_ref[...], wb_ref[...], preferred_element_type=F32)
    h_ref[...] = h
    ms = jnp.mean(h * h, axis=-1, keepdims=True)
    hn_ref[...] = (h * lax.rsqrt(ms + RMS_EPS) * g_ref[...]).astype(hn_ref.dtype)


def _out_proj_norm(a, w, res, gain, tm=OUT_PROJ_TM):
    m, k = a.shape
    n = w.shape[1]
    full = lambda i: (0, 0)
    row = lambda i: (i, 0)
    return pl.pallas_call(
        _out_norm_kernel,
        grid=(m // tm,),
        in_specs=[pl.BlockSpec((tm, k), row),
                  pl.BlockSpec((k, n), full, pipeline_mode=pl.Buffered(1)),
                  pl.BlockSpec((tm, n), row), pl.BlockSpec((1, n), full)],
        out_specs=[pl.BlockSpec((tm, n), row), pl.BlockSpec((tm, n), row)],
        out_shape=[jax.ShapeDtypeStruct((m, n), F32), jax.ShapeDtypeStruct((m, n), BF16)],
        scratch_shapes=[pltpu.VMEM((k, n), BF16)],
        compiler_params=_cparams(("arbitrary",)),
        name="out_proj_norm",
    )(a, w, res, gain.reshape(1, n))


RING_SLOTS = 3


def _mm_res_kernel(a_hbm, w_ref, r_ref, o_ref, wb_ref, abuf_ref, sem_ref, *, tm, n_m, n_steps):
    s = pl.program_id(0) * n_m + pl.program_id(1)
    ahead = RING_SLOTS - 1

    def tile_copy(step):
        row = pl.multiple_of((step % n_m) * tm, tm)
        slot = step % RING_SLOTS
        return pltpu.make_async_copy(a_hbm.at[pl.ds(row, tm), :], abuf_ref.at[slot],
                                     sem_ref.at[slot])

    @pl.when(s == 0)
    def _():
        for first in range(ahead):
            tile_copy(first).start()

    @pl.when(s + ahead < n_steps)
    def _():
        tile_copy(s + ahead).start()

    @pl.when(_first_m_step())
    def _():
        _stage_weight(w_ref, wb_ref)

    tile_copy(s).wait()
    o_ref[...] = r_ref[...] + jnp.dot(abuf_ref[s % RING_SLOTS], wb_ref[...],
                                      preferred_element_type=F32)


def _matmul_residual(a, w, res, tm, tn, name):
    m, k = a.shape
    n = w.shape[1]
    n_m = m // tm
    n_steps = (n // tn) * n_m
    assert n_steps >= RING_SLOTS - 1
    return pl.pallas_call(
        functools.partial(_mm_res_kernel, tm=tm, n_m=n_m, n_steps=n_steps),
        grid=(n // tn, n_m),
        in_specs=[pl.BlockSpec(memory_space=pl.ANY),
                  pl.BlockSpec((k, tn), lambda j, i: (0, j)),
                  pl.BlockSpec((tm, tn), lambda j, i: (i, j))],
        out_specs=pl.BlockSpec((tm, tn), lambda j, i: (i, j)),
        out_shape=jax.ShapeDtypeStruct((m, n), F32),
        scratch_shapes=[pltpu.VMEM((k, tn), BF16), pltpu.VMEM((RING_SLOTS, tm, k), BF16),
                        pltpu.SemaphoreType.DMA((RING_SLOTS,))],
        compiler_params=_cparams(("arbitrary", "arbitrary")),
        name=name,
    )(a, w, res)


def _merge_kernel(xn_ref, oa_ref, ob_ref, wga_ref, wgb_ref, wuf_ref, wun_ref, o_ref,
                  bga_ref, bgb_ref, buf_ref, bun_ref):
    @pl.when(_first_m_step())
    def _():
        _stage_weight(wga_ref, bga_ref, True)
        _stage_weight(wgb_ref, bgb_ref, True)
        _stage_weight(wuf_ref, buf_ref)
        _stage_weight(wun_ref, bun_ref)

    xn = xn_ref[...]
    ga = jax.nn.sigmoid(jnp.dot(xn, bga_ref[...], preferred_element_type=F32))
    ua = jnp.dot(oa_ref[...], buf_ref[...], preferred_element_type=F32)
    acc = ga * ua
    gb = jax.nn.sigmoid(jnp.dot(xn, bgb_ref[...], preferred_element_type=F32))
    ub = jnp.dot(ob_ref[...], bun_ref[...], preferred_element_type=F32)
    o_ref[...] = (acc + gb * ub).astype(o_ref.dtype)


def _merge(xn, oa, ob, wt, row_a, row_b, wuf, wun, tm=MERGE_TM, tn=MERGE_TN):
    m, d = xn.shape
    n = wuf.shape[1]
    ka = oa.shape[1]
    kb = ob.shape[1]
    row = lambda j, i: (i, 0)
    col = lambda j, i: (0, j)
    return pl.pallas_call(
        _merge_kernel,
        grid=(n // tn, m // tm),
        in_specs=[pl.BlockSpec((tm, d), row), pl.BlockSpec((tm, ka), row),
                  pl.BlockSpec((tm, kb), row),
                  _row_window([row_a + t * tn for t in range(n // tn)], tn, d),
                  _row_window([row_b + t * tn for t in range(n // tn)], tn, d),
                  pl.BlockSpec((ka, tn), col), pl.BlockSpec((kb, tn), col)],
        out_specs=pl.BlockSpec((tm, tn), lambda j, i: (i, j)),
        out_shape=jax.ShapeDtypeStruct((m, n), BF16),
        scratch_shapes=[pltpu.VMEM((d, tn), BF16), pltpu.VMEM((d, tn), BF16),
                        pltpu.VMEM((ka, tn), BF16), pltpu.VMEM((kb, tn), BF16)],
        compiler_params=_cparams(("arbitrary", "arbitrary")),
        name="gated_merge",
    )(xn, oa, ob, wt, wt, wuf, wun)


def _swiglu_kernel(a_ref, wg_ref, wu_ref, o_ref, bg_ref, bu_ref):
    @pl.when(_first_m_step())
    def _():
        _stage_weight(wg_ref, bg_ref)
        _stage_weight(wu_ref, bu_ref)

    a = a_ref[...]
    gt = jnp.dot(a, bg_ref[...], preferred_element_type=F32)
    up = jnp.dot(a, bu_ref[...], preferred_element_type=F32)
    o_ref[...] = (gt * jax.nn.sigmoid(gt) * up).astype(o_ref.dtype)


def _swiglu(a, wg, wu, tm=SWIGLU_TM, tn=SWIGLU_TN):
    m, k = a.shape
    n = wg.shape[1]
    return pl.pallas_call(
        _swiglu_kernel,
        grid=(n // tn, m // tm),
        in_specs=[pl.BlockSpec((tm, k), lambda j, i: (i, 0)),
                  pl.BlockSpec((k, tn), lambda j, i: (0, j)),
                  pl.BlockSpec((k, tn), lambda j, i: (0, j))],
        out_specs=pl.BlockSpec((tm, tn), lambda j, i: (i, j)),
        out_shape=jax.ShapeDtypeStruct((m, n), BF16),
        scratch_shapes=[pltpu.VMEM((k, tn), BF16), pltpu.VMEM((k, tn), BF16)],
        compiler_params=_cparams(("arbitrary", "arbitrary")),
        name="swiglu_up",
    )(a, wg, wu)


def _split3(x):
    hi = x.astype(BF16)
    r1 = x - hi.astype(F32)
    mid = r1.astype(BF16)
    lo = (r1 - mid.astype(F32)).astype(BF16)
    return hi, mid, lo


def _decay_kernel(z_ref, b_ref, ccol_ref, crow_ref, *, blk):
    t = z_ref.shape[0]
    r = lax.broadcasted_iota(jnp.int32, (blk, blk), 0)
    c = lax.broadcasted_iota(jnp.int32, (blk, blk), 1)
    tri = jnp.where(r >= c, 1.0, 0.0).astype(BF16)
    carry = jnp.zeros((1, LANES), F32)
    for s in range(t // blk):
        rows = slice(s * blk, (s + 1) * blk)
        z = z_ref[rows, :] + b_ref[...]
        logf = (jnp.minimum(z, 0.0) - jnp.log1p(jnp.exp(-jnp.abs(z)))) * LOG2E
        hi, mid, lo = _split3(logf)
        cb = (jnp.dot(tri, hi, preferred_element_type=F32)
              + jnp.dot(tri, mid, preferred_element_type=F32)
              + jnp.dot(tri, lo, preferred_element_type=F32)) + carry
        carry = cb[blk - 1:blk, :]
        ccol_ref[rows, :] = cb
        crow_ref[0, :, rows] = cb.T[LOGIT_LANE:LOGIT_LANE + FOX_HEADS, :]


def _decay(p3, bias_row, batch, seq, blk=256):
    return pl.pallas_call(
        functools.partial(_decay_kernel, blk=blk),
        grid=(batch,),
        in_specs=[pl.BlockSpec((seq, LANES), lambda b: (b, LOGIT_BLOCK)),
                  pl.BlockSpec((1, LANES), lambda b: (0, 0))],
        out_specs=[pl.BlockSpec((seq, LANES), lambda b: (b, 0)),
                   pl.BlockSpec((1, FOX_HEADS, seq), lambda b: (b, 0, 0))],
        out_shape=[jax.ShapeDtypeStruct((batch * seq, LANES), F32),
                   jax.ShapeDtypeStruct((batch, FOX_HEADS, seq), F32)],
        compiler_params=_cparams(("arbitrary",)),
        name="fox_decay_cumsum",
    )(p3, bias_row)


def _transpose_bf16(x):
    return x.astype(F32).T.astype(BF16)


def _normalised(acc, d):
    return acc[:d, :] * (1.0 / jnp.maximum(acc[d:d + 1, :], 1e-30))


def _fox_kernel(q_ref, k_ref, v_ref, ccol_ref, crow_ref, o_ref, vt_ref, ka_ref, *, tq, nq, nh):
    hh = pl.program_id(1)
    i = pl.program_id(2)
    dh = FOX_HEAD_DIM

    def pieces(x, lane, first):
        hi = x.astype(BF16).astype(F32)
        r1 = x - hi
        mid = r1.astype(BF16).astype(F32)
        return jnp.where(lane == first, hi,
                         jnp.where(lane == first + 1, mid,
                                   jnp.where(lane == first + 2, r1 - mid, 0.0)))

    @pl.when(i == 0)
    def _():
        cc = ccol_ref[...]
        lane = lax.broadcasted_iota(jnp.int32, cc.shape, 1)
        for h in range(nh):
            vt_ref[h, 0:dh, :] = _transpose_bf16(v_ref[:, h * dh:(h + 1) * dh])
            vt_ref[h, dh:, :] = jnp.ones((vt_ref.shape[1] - dh, vt_ref.shape[2]), BF16)
            cj = jnp.sum(jnp.where(lane == LOGIT_LANE + hh * nh + h, cc, 0.0), axis=-1,
                         keepdims=True)
            extra = pieces(-cj, lane, 0) + jnp.where(lane < 3, 0.0, jnp.where(lane < 6, 1.0, 0.0))
            ka_ref[h, :, 0:dh] = k_ref[:, h * dh:(h + 1) * dh]
            ka_ref[h, :, dh:] = extra.astype(BF16)

    row = lax.broadcasted_iota(jnp.int32, (dh, tq), 0)
    qts = []
    for h in range(nh):
        ci = crow_ref[0, h]
        extra = pieces(ci, row, 3) + jnp.where(row < 3, 1.0, 0.0)
        qts.append(jnp.concatenate(
            [_transpose_bf16(q_ref[:, h * dh:(h + 1) * dh]), extra.astype(BF16)], axis=0))
    rk = lax.broadcasted_iota(jnp.int32, (tq, tq), 0)
    cq = lax.broadcasted_iota(jnp.int32, (tq, tq), 1)

    def scores(h, k0):
        return jnp.dot(ka_ref[h, k0:k0 + tq, :], qts[h], preferred_element_type=F32)

    def variant(n):
        def run():
            starts = [t * tq for t in range(n, -1, -1)]
            tiles = [[jnp.where(rk <= cq, scores(h, starts[0]), NEG)]
                     + [scores(h, k0) for k0 in starts[1:]] for h in range(nh)]
            ms = [jnp.max(tiles[h][0], axis=0, keepdims=True) for h in range(nh)]
            accs = [jnp.dot(vt_ref[h, :, starts[0]:starts[0] + tq],
                            jnp.exp2(tiles[h][0] - ms[h]).astype(BF16),
                            preferred_element_type=F32) for h in range(nh)]
            for t in range(1, n + 1):
                k0 = starts[t]
                for h in range(nh):
                    st = tiles[h][t]
                    m_new = jnp.maximum(ms[h], jnp.max(st, axis=0, keepdims=True))
                    accs[h] = jnp.exp2(ms[h] - m_new) * accs[h] + jnp.dot(
                        vt_ref[h, :, k0:k0 + tq], jnp.exp2(st - m_new).astype(BF16),
                        preferred_element_type=F32)
                    ms[h] = m_new
            for h in range(nh):
                o_ref[:, h * dh:(h + 1) * dh] = _normalised(accs[h], dh).T.astype(o_ref.dtype)
        return run

    lax.switch(i, [variant(n) for n in range(nq)])


def _fox_attention(p1, ccol, crow4, batch, seq, tq=FOX_TQ, nh=FOX_HEADS_PER_STEP):
    nq = seq // tq
    hg = FOX_HEADS // nh
    w = nh * FOX_HEAD_DIM
    return pl.pallas_call(
        functools.partial(_fox_kernel, tq=tq, nq=nq, nh=nh),
        grid=(batch, hg, nq),
        in_specs=[pl.BlockSpec((tq, w), lambda b, hh, i: (b * nq + i, hh)),
                  pl.BlockSpec((seq, w), lambda b, hh, i: (b, hg + hh)),
                  pl.BlockSpec((seq, w), lambda b, hh, i: (b, 2 * hg + hh)),
                  pl.BlockSpec((seq, LANES), lambda b, hh, i: (b, 0)),
                  pl.BlockSpec((1, nh, 1, tq), lambda b, hh, i: (b, hh, 0, i))],
        out_specs=pl.BlockSpec((tq, w), lambda b, hh, i: (b * nq + i, hh)),
        out_shape=jax.ShapeDtypeStruct((batch * seq, FOX_WIDTH), BF16),
        scratch_shapes=[pltpu.VMEM((nh, FOX_HEAD_DIM + BF16_SUBLANES, seq), BF16),
                        pltpu.VMEM((nh, seq, 2 * FOX_HEAD_DIM), BF16)],
        compiler_params=_cparams(("arbitrary", "arbitrary", "arbitrary")),
        name="fox_attention",
    )(p1, p1, p1, ccol, crow4)


def _compress_one(z_refs, pe_ref, w1_ref, w2_ref, nblk):
    half = CMP_BLOCK // 2
    first = jnp.zeros((nblk, CMP_HIDDEN), F32)
    second = jnp.zeros((nblk, CMP_HIDDEN), F32)
    for p in range(half):
        rows = pl.ds(p, nblk, stride=CMP_STRIDE)
        zp = [z_ref[rows, :] for z_ref in z_refs]
        zp = zp[0] if len(zp) == 1 else jnp.concatenate(zp, axis=1)
        first += jnp.dot((zp + pe_ref[p:p + 1, :]).astype(BF16), w1_ref[p],
                         preferred_element_type=F32)
        second += jnp.dot((zp + pe_ref[half + p:half + p + 1, :]).astype(BF16),
                          w1_ref[half + p], preferred_element_type=F32)
    hid = first + pltpu.roll(second, nblk - 1, 0)
    act = (hid * jax.nn.sigmoid(hid)).astype(BF16)
    return jnp.dot(act, w2_ref[...], preferred_element_type=F32)


def _compress_kernel(zk0_ref, zk1_ref, zv_ref, pek_ref, w1k_ref, w2k_ref, gk_ref,
                     pev_ref, w1v_ref, w2v_ref, kc_ref, vc_ref, *, nblk):
    kc = _compress_one((zk0_ref, zk1_ref), pek_ref, w1k_ref, w2k_ref, nblk)
    ms = jnp.sum(kc * kc, axis=-1, keepdims=True) * (1.0 / NSA_QK_DIM)
    kc = kc * lax.rsqrt(ms + RMS_EPS) * gk_ref[...]
    pos = CMP_STRIDE * lax.broadcasted_iota(jnp.int32, (nblk, LANES), 0) + (CMP_BLOCK - 1)
    up = NSA_QK_PAD - LANES
    kc_ref[:, :up] = kc[:, :up].astype(kc_ref.dtype)
    kc_ref[:, up:] = (kc[:, up:] + _key_aug(pos)).astype(kc_ref.dtype)
    vc = _compress_one((zv_ref,), pev_ref, w1v_ref, w2v_ref, nblk)
    vc_ref[...] = vc.T.astype(vc_ref.dtype)


def _compress(p3, pek, w1k, w2k, gk, pev, w1v, w2v, batch, seq):
    g = NSA_KV_GROUPS
    nblk = seq // CMP_STRIDE
    full2 = lambda b, gg: (0, 0)
    full3 = lambda b, gg: (0, 0, 0)
    return pl.pallas_call(
        functools.partial(_compress_kernel, nblk=nblk),
        grid=(batch, g),
        in_specs=[pl.BlockSpec((seq, LANES), lambda b, gg: (b, 2 * gg)),
                  pl.BlockSpec((seq, LANES), lambda b, gg: (b, 2 * gg + 1)),
                  pl.BlockSpec((seq, NSA_V_DIM), lambda b, gg: (b, 4 + gg)),
                  pl.BlockSpec(pek.shape, full2), pl.BlockSpec(w1k.shape, full3),
                  pl.BlockSpec(w2k.shape, full2), pl.BlockSpec(gk.shape, full2),
                  pl.BlockSpec(pev.shape, full2), pl.BlockSpec(w1v.shape, full3),
                  pl.BlockSpec(w2v.shape, full2)],
        out_specs=[pl.BlockSpec((nblk, NSA_QK_PAD), lambda b, gg: (b * g + gg, 0)),
                   pl.BlockSpec((NSA_V_DIM, nblk), lambda b, gg: (b * g + gg, 0))],
        out_shape=[jax.ShapeDtypeStruct((batch * g * nblk, NSA_QK_PAD), BF16),
                   jax.ShapeDtypeStruct((batch * g * NSA_V_DIM, nblk), BF16)],
        compiler_params=_cparams(("arbitrary", "arbitrary")),
        name="nsa_compress",
    )(p3, p3, p3, pek, w1k, w2k, gk, pev, w1v, w2v)


def _q_heads_t(qt_ref, first_head):
    return jnp.concatenate(
        [qt_ref[(first_head + hh) * NSA_QK_PAD:(first_head + hh + 1) * NSA_QK_PAD, :]
         for hh in range(NSA_HPG)], axis=1)


def _gate_rows(glt_ref, g, hh):
    base = LOGIT_LANE + FOX_HEADS + (g * NSA_HPG + hh) * 3
    return [glt_ref[pl.ds(base + br, 1), :] for br in range(3)]


def _nsa_select_kernel(q_ref, kc_ref, vct_ref, gl_ref, ovt_ref, xn_ref, wk_ref, wv_ref, gk_ref,
                       sel_ref, ocmp_ref, pk_ref, pv_ref, glt_ref, wkb_ref, wvb_ref,
                       *, tq, n_cmp, n_sel):
    g = pl.program_id(1)
    t0 = pl.program_id(2) * tq
    hpg = NSA_HPG
    dv = NSA_V_DIM

    @pl.when(pl.program_id(2) == 0)
    def _():
        _stage_weight(wk_ref, wkb_ref, True, NSA_QK_DIM, NSA_QK_PAD)
        _stage_weight(wv_ref, wvb_ref, True)

    xn = xn_ref[...]
    yk = jnp.dot(xn, wkb_ref[...], preferred_element_type=F32)
    pv_ref[...] = jnp.dot(xn, wvb_ref[...], preferred_element_type=F32).astype(pv_ref.dtype)
    pos = t0 + lax.broadcasted_iota(jnp.int32, (tq, LANES), 0)
    for c in range(NSA_KV_GROUPS):
        sl = slice(c * NSA_QK_PAD, (c + 1) * NSA_QK_PAD)
        yc = yk[:, sl]
        rs = lax.rsqrt(jnp.sum(yc * yc, axis=-1, keepdims=True) * (1.0 / NSA_QK_DIM) + RMS_EPS)
        out = yc * rs * gk_ref[:, sl]
        up = slice((c + 1) * NSA_QK_PAD - LANES, (c + 1) * NSA_QK_PAD)
        pk_ref[:, c * NSA_QK_PAD:(c + 1) * NSA_QK_PAD - LANES] = (
            out[:, :NSA_QK_PAD - LANES].astype(pk_ref.dtype))
        pk_ref[:, up] = (out[:, NSA_QK_PAD - LANES:] + _key_aug(pos)).astype(pk_ref.dtype)

    q4t = _q_heads_t(q_ref, 0)
    rk = lax.broadcasted_iota(jnp.int32, (LANES, tq), 0)
    cq = lax.broadcasted_iota(jnp.int32, (LANES, tq), 1)

    s_c = jnp.dot(kc_ref[...], q4t, preferred_element_type=F32)
    dist_c = (t0 + cq) - (CMP_STRIDE * rk + (CMP_BLOCK - 1))
    mask_c = jnp.where(rk < n_cmp, dist_c, -1) >= 0
    probs = []
    p_sum = jnp.zeros((LANES, tq), F32)
    for hh in range(hpg):
        sm = jnp.where(mask_c, s_c[:, hh * tq:(hh + 1) * tq], NEG)
        m = jnp.max(sm, axis=0, keepdims=True)
        e = jnp.where(mask_c, jnp.exp2(sm - m), 0.0)
        p = e * (1.0 / jnp.maximum(jnp.sum(e, axis=0, keepdims=True), 1e-30))
        probs.append(p)
        p_sum = p_sum + p
    o_cmp = jnp.dot(vct_ref[...], jnp.concatenate(probs, axis=1).astype(BF16),
                    preferred_element_type=F32)

    ph = p_sum.astype(BF16)
    plo = (p_sum - ph.astype(F32)).astype(BF16)
    ovt = ovt_ref[...]
    imp = (jnp.dot(ovt, ph, preferred_element_type=F32)
           + jnp.dot(ovt, plo, preferred_element_type=F32))[:n_sel, :]

    rj = lax.broadcasted_iota(jnp.int32, (n_sel, tq), 0)
    tcol = t0 + lax.broadcasted_iota(jnp.int32, (n_sel, tq), 1)
    back = (tcol >> (SEL_BLOCK.bit_length() - 1)) - rj
    elig = back >= 0
    forced = jnp.where(rj == 0, 0, jnp.where(elig, back, SEL_LOCAL)) < SEL_LOCAL
    score = jnp.where(elig, jnp.where(forced, FORCE_SCORE, imp), -1.0)
    rank = jnp.zeros((n_sel, tq), F32)
    for jp in range(n_sel):
        row = score[jp:jp + 1, :]
        later = jnp.where(rj > jp, 1.0, 0.0)
        rank = rank + jnp.where(row > score, 1.0, jnp.where(row == score, later, 0.0))
    sel_ref[0:n_sel, :] = jnp.where(elig, jnp.where(rank < SEL_TOPK, 0.0, NEG), NEG)
    sel_ref[n_sel:, :] = jnp.full((LANES - n_sel, tq), NEG, F32)

    glt_ref[...] = jax.nn.sigmoid(gl_ref[...]).T
    for hh in range(hpg):
        gate = _gate_rows(glt_ref, g, hh)[0]
        ocmp_ref[:, hh * dv:(hh + 1) * dv] = (gate * o_cmp[:, hh * tq:(hh + 1) * tq]).T


def _nsa_select(pq, p3, kc, vct, ovt, xn, wt, k_rows, v_rows, gain_k, batch, seq,
                tq=NSA_SELECT_TQ):
    g = NSA_KV_GROUPS
    nq = seq // tq
    nblk = seq // CMP_STRIDE
    n_cmp = nblk - CMP_BLOCK // CMP_STRIDE + 1
    d = xn.shape[1]
    kw_out = g * NSA_QK_PAD
    return pl.pallas_call(
        functools.partial(_nsa_select_kernel, tq=tq, n_cmp=n_cmp, n_sel=seq // SEL_BLOCK),
        grid=(batch, g, nq),
        in_specs=[
            pl.BlockSpec((NSA_HPG * NSA_QK_PAD, tq), lambda b, gg, i: (gg, b * nq + i)),
            pl.BlockSpec((nblk, NSA_QK_PAD), lambda b, gg, i: (b * g + gg, 0)),
            pl.BlockSpec((NSA_V_DIM, nblk), lambda b, gg, i: (b * g + gg, 0)),
            pl.BlockSpec((tq, LANES), lambda b, gg, i: (b * nq + i, LOGIT_BLOCK)),
            pl.BlockSpec(ovt.shape, lambda b, gg, i: (0, 0)),
            pl.BlockSpec((tq, d), lambda b, gg, i: (b * nq + i, 0)),
            _row_window(k_rows, KV_K, d, grid_arg=1),
            _row_window(v_rows, KV_V, d, grid_arg=1),
            pl.BlockSpec((1, kw_out), lambda b, gg, i: (0, gg)),
        ],
        out_specs=[pl.BlockSpec((LANES, tq), lambda b, gg, i: (b * g + gg, i)),
                   pl.BlockSpec((tq, NSA_HPG * NSA_V_DIM), lambda b, gg, i: (b * nq + i, gg)),
                   pl.BlockSpec((tq, kw_out), lambda b, gg, i: (b * nq + i, gg)),
                   pl.BlockSpec((tq, KV_V), lambda b, gg, i: (b * nq + i, gg))],
        out_shape=[jax.ShapeDtypeStruct((batch * g * LANES, seq), F32),
                   jax.ShapeDtypeStruct((batch * seq, NSA_WIDTH), F32),
                   jax.ShapeDtypeStruct((batch * seq, 2 * kw_out), BF16),
                   jax.ShapeDtypeStruct((batch * seq, 2 * KV_V), BF16)],
        scratch_shapes=[pltpu.VMEM((LANES, tq), F32), pltpu.VMEM((d, kw_out), BF16),
                        pltpu.VMEM((d, KV_V), BF16)],
        compiler_params=_cparams(("arbitrary", "arbitrary", "arbitrary")),
        name="nsa_select",
    )(pq, kc, vct, p3, ovt, xn, wt, wt, gain_k.reshape(1, 2 * kw_out))


def _nsa_attend_kernel(q_ref, ks_ref, vs_ref, kw_ref, vw_ref, gl_ref, sel_ref,
                       ocmp_ref, o_ref, vst_ref, vwt_ref, glt_ref, *, tq, n_var):
    i = pl.program_id(1)
    t0 = i * tq
    hpg = NSA_HPG
    ng = NSA_KV_GROUPS
    dv = NSA_V_DIM
    dk = NSA_QK_PAD

    @pl.when(i == 0)
    def _():
        ones = jnp.ones((vst_ref.shape[1] - dv, vst_ref.shape[2]), BF16)
        for gg in range(ng):
            vst_ref[gg, 0:dv, :] = _transpose_bf16(vs_ref[:, gg * dv:(gg + 1) * dv])
            vst_ref[gg, dv:, :] = ones
            vwt_ref[gg, 0:dv, :] = _transpose_bf16(vw_ref[:, gg * dv:(gg + 1) * dv])
            vwt_ref[gg, dv:, :] = ones

    q4t = [_q_heads_t(q_ref, gg * hpg) for gg in range(ng)]
    glt_ref[...] = jax.nn.sigmoid(gl_ref[...]).T

    def distance(k0, rows):
        return (lax.broadcasted_iota(jnp.int32, (rows, tq), 1)
                - lax.broadcasted_iota(jnp.int32, (rows, tq), 0)) + (t0 - k0)

    def scores(gg, k, mask_bias):
        st = jnp.dot(k, q4t[gg], preferred_element_type=F32)
        return st + jnp.concatenate([mask_bias] * hpg, axis=1)

    def selection_bias(gg, k0):
        j0 = gg * LANES + k0 // SEL_BLOCK
        return jnp.concatenate(
            [jnp.broadcast_to(sel_ref[j0 + j:j0 + j + 1, :], (SEL_BLOCK, tq))
             for j in range(SLC_TILE // SEL_BLOCK)], axis=0)

    def variant(n):
        def run():
            kw0 = pl.multiple_of(jnp.maximum(t0 - WINDOW, 0), LANES)
            wrows = WINDOW + tq
            dist_w = distance(kw0, wrows)
            bias_w = jnp.where(jnp.where(dist_w >= 0, dist_w, WINDOW) < WINDOW, 0.0, NEG)
            o_win = []
            for gg in range(ng):
                sw = scores(gg, kw_ref[pl.ds(kw0, wrows), gg * dk:(gg + 1) * dk], bias_w)
                pw = jnp.exp2(sw - jnp.max(sw, axis=0, keepdims=True))
                o_win.append(_normalised(
                    jnp.dot(vwt_ref[gg, :, pl.ds(kw0, wrows)], pw.astype(BF16),
                            preferred_element_type=F32), dv))

            starts = [t * SLC_TILE for t in range(n, -1, -1)]
            causal = distance(starts[0], SLC_TILE) >= 0
            tiles = [[] for _ in range(ng)]
            for t, k0 in enumerate(starts):
                for gg in range(ng):
                    bias = selection_bias(gg, k0)
                    if t == 0:
                        bias = jnp.where(causal, bias, NEG)
                    tiles[gg].append(scores(
                        gg, ks_ref[k0:k0 + SLC_TILE, gg * dk:(gg + 1) * dk], bias))
            ms = [jnp.max(tiles[gg][0], axis=0, keepdims=True) for gg in range(ng)]
            accs = [jnp.dot(vst_ref[gg, :, starts[0]:starts[0] + SLC_TILE],
                            jnp.exp2(tiles[gg][0] - ms[gg]).astype(BF16),
                            preferred_element_type=F32) for gg in range(ng)]
            for t in range(1, n + 1):
                k0 = starts[t]
                for gg in range(ng):
                    st = tiles[gg][t]
                    m_new = jnp.maximum(ms[gg], jnp.max(st, axis=0, keepdims=True))
                    accs[gg] = jnp.exp2(ms[gg] - m_new) * accs[gg] + jnp.dot(
                        vst_ref[gg, :, k0:k0 + SLC_TILE], jnp.exp2(st - m_new).astype(BF16),
                        preferred_element_type=F32)
                    ms[gg] = m_new

            for gg in range(ng):
                o_slc = _normalised(accs[gg], dv)
                for hh in range(hpg):
                    _, g_slc, g_win = _gate_rows(glt_ref, gg, hh)
                    lanes = slice(hh * tq, (hh + 1) * tq)
                    out = g_slc * o_slc[:, lanes] + g_win * o_win[gg][:, lanes]
                    cols = slice((gg * hpg + hh) * dv, (gg * hpg + hh + 1) * dv)
                    o_ref[:, cols] = (ocmp_ref[:, cols] + out.T).astype(o_ref.dtype)
        return run

    lax.switch(t0 // SLC_TILE, [variant(n) for n in range(n_var)])


def _nsa_attend(pq, pk, pv, p3, sel, ocmp, batch, seq):
    tq = NSA_ATTEND_TQ
    g = NSA_KV_GROUPS
    nq = seq // tq
    vrows = NSA_V_DIM + BF16_SUBLANES
    return pl.pallas_call(
        functools.partial(_nsa_attend_kernel, tq=tq, n_var=seq // SLC_TILE),
        grid=(batch, nq),
        in_specs=[
            pl.BlockSpec((NSA_HEADS * NSA_QK_PAD, tq), lambda b, i: (0, b * nq + i)),
            pl.BlockSpec((seq, g * NSA_QK_PAD), lambda b, i: (b, 0)),
            pl.BlockSpec((seq, g * NSA_V_DIM), lambda b, i: (b, 0)),
            pl.BlockSpec((seq, g * NSA_QK_PAD), lambda b, i: (b, 1)),
            pl.BlockSpec((seq, g * NSA_V_DIM), lambda b, i: (b, 1)),
            pl.BlockSpec((tq, LANES), lambda b, i: (b * nq + i, LOGIT_BLOCK)),
            pl.BlockSpec((g * LANES, tq), lambda b, i: (b, i)),
            pl.BlockSpec((tq, NSA_WIDTH), lambda b, i: (b * nq + i, 0)),
        ],
        out_specs=pl.BlockSpec((tq, NSA_WIDTH), lambda b, i: (b * nq + i, 0)),
        out_shape=jax.ShapeDtypeStruct((batch * seq, NSA_WIDTH), BF16),
        scratch_shapes=[pltpu.VMEM((g, vrows, seq), BF16), pltpu.VMEM((g, vrows, seq), BF16),
                        pltpu.VMEM((LANES, tq), F32)],
        compiler_params=_cparams(("arbitrary", "arbitrary")),
        name="nsa_attend",
    )(pq, pk, pv, pk, pv, p3, sel, ocmp)


def _pad_gain(gain, scale=1.0):
    return jnp.pad(gain * scale, (0, NSA_QK_PAD - NSA_QK_DIM))


def _overlap_matrix(nc, ns):
    i = np.arange(nc)[:, None]
    j = np.arange(ns)[None, :]
    lo = np.maximum(i * CMP_STRIDE, j * SEL_BLOCK)
    hi = np.minimum(i * CMP_STRIDE + CMP_BLOCK, (j + 1) * SEL_BLOCK)
    return (np.maximum(hi - lo, 0) / CMP_STRIDE).astype(np.float32)


def kernel(x, norm_attn, w_in, fox_f_bias, fox_q_gain, fox_k_gain,
           nsa_q_gain, nsa_kc_gain, nsa_ks_gain, nsa_kw_gain,
           cmp_pe_k, cmp_w1_k, cmp_w2_k, cmp_pe_v, cmp_w1_v, cmp_w2_v,
           w_up_fox, w_up_nsa, w_out, norm_ffn, w_ffn_gate, w_ffn_up, w_ffn_down):
    batch, seq, d = x.shape
    m = batch * seq
    depth = w_in.shape[0]
    pts = [0] + [int(p) for p in np.cumsum(IN_SPLITS)]
    nblk = seq // CMP_STRIDE
    n_cmp = nblk - CMP_BLOCK // CMP_STRIDE + 1
    ns = seq // SEL_BLOCK

    slope = jnp.exp2(-8.0 * jnp.arange(1, NSA_HEADS + 1, dtype=F32) / NSA_HEADS) * LOG2E
    s1, s2, s3 = [p.astype(F32) for p in _split3(slope)]
    q_spare = jnp.stack([256.0 * s1, 256.0 * s2, 256.0 * s3, s1, s2, s3, -slope, -slope, -slope],
                        axis=1)
    q_spare = jnp.pad(q_spare, ((0, 0), (0, NSA_QK_PAD - NSA_QK_DIM - 9)))
    ovt_np = np.zeros((LANES, nblk), np.float32)
    ovt_np[:ns, :n_cmp] = _overlap_matrix(n_cmp, ns).T
    ovt = jnp.asarray(ovt_np, BF16)

    w_in_t = jnp.swapaxes(w_in, 1, 2)

    xf = x.reshape(m, d)
    for l in range(depth):
        wt = w_in_t[l]
        row = dict(zip(("fq", "fk", "fv", "fl", "nq", "kc", "vc", "ks", "vs", "kw", "vw", "ng",
                        "ga", "gb"), pts))

        gain1 = jnp.concatenate([jnp.tile(fox_q_gain[l] * (FOX_HEAD_DIM ** -0.5 * LOG2E), FOX_HEADS),
                                 jnp.tile(fox_k_gain[l], FOX_HEADS),
                                 jnp.ones((FOX_WIDTH,), F32)])
        flag1 = jnp.concatenate([jnp.ones((2 * FOX_WIDTH,), F32), jnp.zeros((FOX_WIDTH,), F32)])
        q_gain = jnp.broadcast_to(nsa_q_gain[l] * (NSA_QK_DIM ** -0.5 * LOG2E),
                                  (NSA_HEADS, NSA_QK_DIM))
        q_table = jnp.broadcast_to(
            jnp.concatenate([q_gain, q_spare], axis=1).reshape(NSA_HEADS * NSA_QK_PAD, 1),
            (NSA_HEADS * NSA_QK_PAD, LANES))
        gain_k = jnp.concatenate([jnp.tile(_pad_gain(nsa_ks_gain[l]), NSA_KV_GROUPS),
                                  jnp.tile(_pad_gain(nsa_kw_gain[l]), NSA_KV_GROUPS)])
        pad_d = NSA_QK_PAD - NSA_QK_DIM
        n_small = FOX_HEADS + 3 * NSA_HEADS
        w3 = jnp.concatenate([wt[row["kc"]:row["kc"] + NSA_QK_DIM],
                              wt[row["fl"]:row["nq"]], wt[row["ng"]:row["ga"]],
                              jnp.zeros((pad_d - n_small, d), F32),
                              wt[row["kc"] + NSA_QK_DIM:row["vc"]], jnp.zeros((pad_d, d), F32),
                              wt[row["vc"]:row["ks"]]], axis=0)

        xn, p3 = _rmsnorm_project(xf, norm_attn[l], w3)
        q_tile = NSA_HPG * NSA_QK_DIM
        p1 = _project(xn, wt, list(range(0, 3 * FOX_WIDTH, FOX_PROJ_TN)), FOX_PROJ_TN, gain1,
                      flag1, PROJ_TM, "proj_fox", group=FOX_HEAD_DIM)
        pq = _project_qt(xn, wt, [row["nq"], row["nq"] + q_tile], q_tile, q_table, PROJ_TM, seq,
                         "proj_nsa_q")

        bias_row = jnp.pad(fox_f_bias[l], (LOGIT_LANE, LANES - LOGIT_LANE - FOX_HEADS)).reshape(1, LANES)
        ccol, crow = _decay(p3, bias_row, batch, seq)
        o_a = _fox_attention(p1, ccol, crow.reshape(batch, FOX_HEADS, 1, seq), batch, seq)

        pek = jnp.pad(cmp_pe_k[l], ((0, 0), (0, pad_d)))
        w1k = jnp.pad(cmp_w1_k[l].reshape(CMP_BLOCK, NSA_QK_DIM, CMP_HIDDEN),
                      ((0, 0), (0, pad_d), (0, 0))).astype(BF16)
        w2k = jnp.pad(cmp_w2_k[l], ((0, 0), (0, pad_d))).astype(BF16)
        gk = _pad_gain(nsa_kc_gain[l]).reshape(1, NSA_QK_PAD)
        w1v = cmp_w1_v[l].reshape(CMP_BLOCK, NSA_V_DIM, CMP_HIDDEN).astype(BF16)
        w2v = cmp_w2_v[l].astype(BF16)
        kc, vct = _compress(p3, pek, w1k, w2k, gk, cmp_pe_v[l], w1v, w2v, batch, seq)
        sel, ocmp, pk, pv = _nsa_select(pq, p3, kc, vct, ovt, xn, wt, [row["ks"], row["kw"]],
                                        [row["vs"], row["vw"]], gain_k, batch, seq)
        o_b = _nsa_attend(pq, pk, pv, p3, sel, ocmp, batch, seq)

        merged = _merge(xn, o_a, o_b, wt, row["ga"], row["gb"], w_up_fox[l], w_up_nsa[l])
        hres, hn = _out_proj_norm(merged, w_out[l], xf, norm_ffn[l])

        act = _swiglu(hn, w_ffn_gate[l], w_ffn_up[l])
        xf = _matmul_residual(act, w_ffn_down[l], hres, FFN_DOWN_TM, FFN_DOWN_TN, "ffn_down")
    return xf.reshape(batch, seq, d)
```

```python
import functools

import numpy as np
import jax
import jax.numpy as jnp
from jax import lax
from jax.experimental import pallas as pl
from jax.experimental.pallas import tpu as pltpu

F32 = jnp.float32
BF16 = jnp.bfloat16

D_MODEL = 2048
FOX_HEADS = 8
FOX_HEAD_DIM = 128
FOX_WIDTH = FOX_HEADS * FOX_HEAD_DIM
NSA_HEADS = 8
NSA_KV_GROUPS = 2
NSA_HPG = NSA_HEADS // NSA_KV_GROUPS
NSA_QK_DIM = 192
NSA_QK_PAD = 256
NSA_V_DIM = 128
NSA_WIDTH = NSA_HEADS * NSA_V_DIM
CMP_BLOCK = 32
CMP_STRIDE = 16
CMP_HIDDEN = 256
SEL_BLOCK = 64
SEL_TOPK = 16
SEL_LOCAL = 2
FORCE_SCORE = 1.0e4
WINDOW = 512
KV_K = NSA_KV_GROUPS * NSA_QK_DIM
KV_V = NSA_KV_GROUPS * NSA_V_DIM
D_FF = -(-(8 * D_MODEL) // (3 * 256)) * 256
RMS_EPS = 1e-6
IN_SPLITS = (FOX_WIDTH, FOX_WIDTH, FOX_WIDTH, FOX_HEADS,
             NSA_HEADS * NSA_QK_DIM, KV_K, KV_V, KV_K, KV_V, KV_K, KV_V,
             3 * NSA_HEADS, D_MODEL, D_MODEL)

_NT = (((1,), (1,)), ((), ()))

LANES = 128
SUBLANES = 8
BF16_SUBLANES = 16
NEG = -1.0e30
LOG2E = 1.4426950408889634
SLC_TILE = 512
LOGIT_BLOCK = 1
LOGIT_LANE = NSA_QK_DIM - LANES
VMEM_LIMIT = 56 * 1024 * 1024

RMS_TM = 512
PROJ_TM = 1024
FOX_PROJ_TN = 1024
MERGE_TM, MERGE_TN = 512, 512
OUT_PROJ_TM = 512
SWIGLU_TM, SWIGLU_TN = 1024, 512
FFN_DOWN_TM, FFN_DOWN_TN = 512, 512
FOX_TQ = 512
FOX_HEADS_PER_STEP = 4
NSA_SELECT_TQ = 512
NSA_ATTEND_TQ = 256


def _cparams(sem):
    return pltpu.CompilerParams(dimension_semantics=sem, vmem_limit_bytes=VMEM_LIMIT)


def _rms_proj_kernel(x_ref, g_ref, w_ref, xn_ref, p_ref, wb_ref):
    @pl.when(pl.program_id(0) == 0)
    def _():
        _stage_weight(w_ref, wb_ref, True)

    rows = x_ref.shape[0] // EPILOGUE_SPLIT
    for r in range(EPILOGUE_SPLIT):
        sl = slice(r * rows, (r + 1) * rows)
        x = x_ref[sl, :]
        ms = jnp.mean(x * x, axis=-1, keepdims=True)
        xn = (x * lax.rsqrt(ms + RMS_EPS) * g_ref[...]).astype(BF16)
        xn_ref[sl, :] = xn
        p_ref[sl, :] = jnp.dot(xn, wb_ref[...], preferred_element_type=F32)


def _rmsnorm_project(x, gain, wt, tm=RMS_TM):
    m, d = x.shape
    n = wt.shape[0]
    return pl.pallas_call(
        _rms_proj_kernel,
        grid=(m // tm,),
        in_specs=[pl.BlockSpec((tm, d), lambda i: (i, 0)),
                  pl.BlockSpec((1, d), lambda i: (0, 0)),
                  pl.BlockSpec((n, d), lambda i: (0, 0), pipeline_mode=pl.Buffered(1))],
        out_specs=[pl.BlockSpec((tm, d), lambda i: (i, 0)),
                   pl.BlockSpec((tm, n), lambda i: (i, 0))],
        out_shape=[jax.ShapeDtypeStruct((m, d), BF16), jax.ShapeDtypeStruct((m, n), F32)],
        scratch_shapes=[pltpu.VMEM((d, n), BF16)],
        compiler_params=_cparams(("arbitrary",)),
        name="rmsnorm_proj_f32",
    )(x, gain.reshape(1, d), wt)


STAGE_CHUNK = 512
EPILOGUE_SPLIT = 4


def _stage_weight(w_ref, wb_ref, transposed=False, pad_from=0, pad_to=0):
    if not transposed:
        wb_ref[...] = w_ref[...].astype(BF16)
        return
    n_in, k = w_ref.shape
    for c in range(k // STAGE_CHUNK):
        cols = slice(c * STAGE_CHUNK, (c + 1) * STAGE_CHUNK)
        w = w_ref[:, cols]
        if pad_from != pad_to:
            zero = jnp.zeros((pad_to - pad_from, STAGE_CHUNK), F32)
            w = jnp.concatenate(
                [piece for h in range(n_in // pad_from)
                 for piece in (w[h * pad_from:(h + 1) * pad_from, :], zero)], axis=0)
        wb_ref[cols, :] = w.T.astype(BF16)


def _first_m_step():
    return pl.program_id(1) == 0


AUG_LANE = NSA_QK_DIM - LANES


def _key_aug(pos):
    lane = lax.broadcasted_iota(jnp.int32, pos.shape, 1)
    hi = (pos >> 8).astype(F32)
    lo = (pos & 255).astype(F32)
    return jnp.where(lane < AUG_LANE, 0.0,
                     jnp.where(lane < AUG_LANE + 3, hi,
                               jnp.where(lane < AUG_LANE + 6, lo,
                                         jnp.where(lane < AUG_LANE + 9, 1.0, 0.0))))


def _proj_kernel(a_ref, w_ref, gain_ref, flag_ref, o_ref, wb_ref, *, group):
    @pl.when(_first_m_step())
    def _():
        _stage_weight(w_ref, wb_ref, True)

    rows = a_ref.shape[0] // EPILOGUE_SPLIT
    ys = [jnp.dot(a_ref[r * rows:(r + 1) * rows, :], wb_ref[...], preferred_element_type=F32)
          for r in range(EPILOGUE_SPLIT)]
    for r, y in enumerate(ys):
        for c in range(y.shape[1] // group):
            sl = slice(c * group, (c + 1) * group)
            yc = y[:, sl]
            ss = jnp.sum(yc * yc, axis=-1, keepdims=True)
            rs = lax.rsqrt(ss * (1.0 / group) + RMS_EPS)
            scale = jnp.where(flag_ref[:, sl] > 0.0, rs, 1.0)
            o_ref[r * rows:(r + 1) * rows, sl] = (
                yc * scale * gain_ref[:, sl]).astype(o_ref.dtype)


def _row_window(starts, rows, k, grid_arg=0):
    def index(*grid):
        j = grid[grid_arg]
        start = starts[-1]
        for t in range(len(starts) - 2, -1, -1):
            start = jnp.where(j == t, starts[t], start)
        return pl.multiple_of(start, SUBLANES), 0
    assert all(s % SUBLANES == 0 for s in starts)
    return pl.BlockSpec((pl.Element(rows), pl.Element(k)), index)


def _project(a, wt, starts, tn, gain, flag, tm, name, group):
    m, k = a.shape
    n_out = len(starts) * tn
    return pl.pallas_call(
        functools.partial(_proj_kernel, group=group),
        grid=(len(starts), m // tm),
        in_specs=[pl.BlockSpec((tm, k), lambda j, i: (i, 0)),
                  _row_window(starts, tn, k),
                  pl.BlockSpec((1, tn), lambda j, i: (0, j)),
                  pl.BlockSpec((1, tn), lambda j, i: (0, j))],
        out_specs=pl.BlockSpec((tm, tn), lambda j, i: (i, j)),
        out_shape=jax.ShapeDtypeStruct((m, n_out), BF16),
        scratch_shapes=[pltpu.VMEM((k, tn), BF16)],
        compiler_params=_cparams(("arbitrary", "arbitrary")),
        name=name,
    )(a, wt, gain.reshape(1, n_out), flag.reshape(1, n_out))


def _proj_qt_kernel(a_ref, w_ref, tab_ref, o_ref, wb_ref, *, head, head_pad, seq):
    @pl.when(_first_m_step())
    def _():
        wb_ref[...] = w_ref[...].astype(BF16)

    tm = a_ref.shape[0]
    cols = tm // EPILOGUE_SPLIT
    reps = cols // LANES
    spare = head_pad - head
    ys = [lax.dot_general(wb_ref[...], a_ref[r * cols:(r + 1) * cols, :], _NT,
                          preferred_element_type=F32) for r in range(EPILOGUE_SPLIT)]
    row = lax.broadcasted_iota(jnp.int32, (spare, cols), 0)
    for r, y in enumerate(ys):
        lanes = slice(r * cols, (r + 1) * cols)
        pos = ((pl.program_id(1) * tm) % seq + r * cols
               + lax.broadcasted_iota(jnp.int32, (1, cols), 1)).astype(F32)
        for h in range(y.shape[0] // head):
            yh = y[h * head:(h + 1) * head, :]
            rs = lax.rsqrt(jnp.sum(yh * yh, axis=0, keepdims=True) * (1.0 / head) + RMS_EPS)
            slot = h * head_pad
            gain = jnp.concatenate([tab_ref[slot:slot + head, :]] * reps, axis=1)
            o_ref[slot:slot + head, lanes] = (yh * rs * gain).astype(o_ref.dtype)
            tab = jnp.concatenate([tab_ref[slot + head:slot + head_pad, :]] * reps, axis=1)
            u = pos * tab
            hi = u.astype(BF16).astype(F32)
            r1 = u - hi
            mid = r1.astype(BF16).astype(F32)
            extra = jnp.where(row < 6, tab,
                              jnp.where(row == 6, hi,
                                        jnp.where(row == 7, mid,
                                                  jnp.where(row == 8, r1 - mid, 0.0))))
            o_ref[slot + head:slot + head_pad, lanes] = extra.astype(o_ref.dtype)


def _project_qt(a, wt, starts, tn_in, table, tm, seq, name):
    m, k = a.shape
    tn_out = tn_in // NSA_QK_DIM * NSA_QK_PAD
    assert seq % tm == 0
    return pl.pallas_call(
        functools.partial(_proj_qt_kernel, head=NSA_QK_DIM, head_pad=NSA_QK_PAD, seq=seq),
        grid=(len(starts), m // tm),
        in_specs=[pl.BlockSpec((tm, k), lambda j, i: (i, 0)),
                  _row_window(starts, tn_in, k),
                  pl.BlockSpec((tn_out, LANES), lambda j, i: (j, 0))],
        out_specs=pl.BlockSpec((tn_out, tm), lambda j, i: (j, i)),
        out_shape=jax.ShapeDtypeStruct((len(starts) * tn_out, m), BF16),
        scratch_shapes=[pltpu.VMEM((tn_in, k), BF16)],
        compiler_params=_cparams(("arbitrary", "arbitrary")),
        name=name,
    )(a, wt, table)


def _out_norm_kernel(a_ref, w_ref, r_ref, g_ref, h_ref, hn_ref, wb_ref):
    @pl.when(pl.program_id(0) == 0)
    def _():
        _stage_weight(w_ref, wb_ref)

    h = r_ref[...] + jnp.dot(a_ref[...], wb_ref[...], preferred_element_type=F32)
    h_ref[...] = h
    ms = jnp.mean(h * h, axis=-1, keepdims=True)
    hn_ref[...] = (h * lax.rsqrt(ms + RMS_EPS) * g_ref[...]).astype(hn_ref.dtype)


def _out_proj_norm(a, w, res, gain, tm=OUT_PROJ_TM):
    m, k = a.shape
    n = w.shape[1]
    full = lambda i: (0, 0)
    row = lambda i: (i, 0)
    return pl.pallas_call(
        _out_norm_kernel,
        grid=(m // tm,),
        in_specs=[pl.BlockSpec((tm, k), row),
                  pl.BlockSpec((k, n), full, pipeline_mode=pl.Buffered(1)),
                  pl.BlockSpec((tm, n), row), pl.BlockSpec((1, n), full)],
        out_specs=[pl.BlockSpec((tm, n), row), pl.BlockSpec((tm, n), row)],
        out_shape=[jax.ShapeDtypeStruct((m, n), F32), jax.ShapeDtypeStruct((m, n), BF16)],
        scratch_shapes=[pltpu.VMEM((k, n), BF16)],
        compiler_params=_cparams(("arbitrary",)),
        name="out_proj_norm",
    )(a, w, res, gain.reshape(1, n))


def _mm_res_kernel(a_ref, w_ref, r_ref, o_ref, wb_ref):
    @pl.when(_first_m_step())
    def _():
        _stage_weight(w_ref, wb_ref)

    o_ref[...] = r_ref[...] + jnp.dot(a_ref[...], wb_ref[...], preferred_element_type=F32)


def _matmul_residual(a, w, res, tm, tn, name):
    m, k = a.shape
    n = w.shape[1]
    return pl.pallas_call(
        _mm_res_kernel,
        grid=(n // tn, m // tm),
        in_specs=[pl.BlockSpec((tm, k), lambda j, i: (i, 0)),
                  pl.BlockSpec((k, tn), lambda j, i: (0, j)),
                  pl.BlockSpec((tm, tn), lambda j, i: (i, j))],
        out_specs=pl.BlockSpec((tm, tn), lambda j, i: (i, j)),
        out_shape=jax.ShapeDtypeStruct((m, n), F32),
        scratch_shapes=[pltpu.VMEM((k, tn), BF16)],
        compiler_params=_cparams(("arbitrary", "arbitrary")),
        name=name,
    )(a, w, res)


def _merge_kernel(xn_ref, oa_ref, ob_ref, wga_ref, wgb_ref, wuf_ref, wun_ref, o_ref,
                  bga_ref, bgb_ref, buf_ref, bun_ref):
    @pl.when(_first_m_step())
    def _():
        _stage_weight(wga_ref, bga_ref, True)
        _stage_weight(wgb_ref, bgb_ref, True)
        _stage_weight(wuf_ref, buf_ref)
        _stage_weight(wun_ref, bun_ref)

    xn = xn_ref[...]
    ga = jax.nn.sigmoid(jnp.dot(xn, bga_ref[...], preferred_element_type=F32))
    ua = jnp.dot(oa_ref[...], buf_ref[...], preferred_element_type=F32)
    acc = ga * ua
    gb = jax.nn.sigmoid(jnp.dot(xn, bgb_ref[...], preferred_element_type=F32))
    ub = jnp.dot(ob_ref[...], bun_ref[...], preferred_element_type=F32)
    o_ref[...] = (acc + gb * ub).astype(o_ref.dtype)


def _merge(xn, oa, ob, wt, row_a, row_b, wuf, wun, tm=MERGE_TM, tn=MERGE_TN):
    m, d = xn.shape
    n = wuf.shape[1]
    ka = oa.shape[1]
    kb = ob.shape[1]
    row = lambda j, i: (i, 0)
    col = lambda j, i: (0, j)
    return pl.pallas_call(
        _merge_kernel,
        grid=(n // tn, m // tm),
        in_specs=[pl.BlockSpec((tm, d), row), pl.BlockSpec((tm, ka), row),
                  pl.BlockSpec((tm, kb), row),
                  _row_window([row_a + t * tn for t in range(n // tn)], tn, d),
                  _row_window([row_b + t * tn for t in range(n // tn)], tn, d),
                  pl.BlockSpec((ka, tn), col), pl.BlockSpec((kb, tn), col)],
        out_specs=pl.BlockSpec((tm, tn), lambda j, i: (i, j)),
        out_shape=jax.ShapeDtypeStruct((m, n), BF16),
        scratch_shapes=[pltpu.VMEM((d, tn), BF16), pltpu.VMEM((d, tn), BF16),
                        pltpu.VMEM((ka, tn), BF16), pltpu.VMEM((kb, tn), BF16)],
        compiler_params=_cparams(("arbitrary", "arbitrary")),
        name="gated_merge",
    )(xn, oa, ob, wt, wt, wuf, wun)


def _swiglu_kernel(a_ref, wg_ref, wu_ref, o_ref, bg_ref, bu_ref):
    @pl.when(_first_m_step())
    def _():
        _stage_weight(wg_ref, bg_ref)
        _stage_weight(wu_ref, bu_ref)

    a = a_ref[...]
    gt = jnp.dot(a, bg_ref[...], preferred_element_type=F32)
    up = jnp.dot(a, bu_ref[...], preferred_element_type=F32)
    o_ref[...] = (gt * jax.nn.sigmoid(gt) * up).astype(o_ref.dtype)


def _swiglu(a, wg, wu, tm=SWIGLU_TM, tn=SWIGLU_TN):
    m, k = a.shape
    n = wg.shape[1]
    return pl.pallas_call(
        _swiglu_kernel,
        grid=(n // tn, m // tm),
        in_specs=[pl.BlockSpec((tm, k), lambda j, i: (i, 0)),
                  pl.BlockSpec((k, tn), lambda j, i: (0, j)),
                  pl.BlockSpec((k, tn), lambda j, i: (0, j))],
        out_specs=pl.BlockSpec((tm, tn), lambda j, i: (i, j)),
        out_shape=jax.ShapeDtypeStruct((m, n), BF16),
        scratch_shapes=[pltpu.VMEM((k, tn), BF16), pltpu.VMEM((k, tn), BF16)],
        compiler_params=_cparams(("arbitrary", "arbitrary")),
        name="swiglu_up",
    )(a, wg, wu)


def _split3(x):
    hi = x.astype(BF16)
    r1 = x - hi.astype(F32)
    mid = r1.astype(BF16)
    lo = (r1 - mid.astype(F32)).astype(BF16)
    return hi, mid, lo


def _decay_kernel(z_ref, b_ref, ckey_ref, crow_ref, *, blk):
    t = z_ref.shape[0]
    r = lax.broadcasted_iota(jnp.int32, (blk, blk), 0)
    c = lax.broadcasted_iota(jnp.int32, (blk, blk), 1)
    tri = jnp.where(r >= c, 1.0, 0.0).astype(BF16)
    lane = lax.broadcasted_iota(jnp.int32, (blk, LANES), 1)
    seg = (lane - LOGIT_LANE) // FOX_HEADS
    carry = jnp.zeros((1, LANES), F32)
    for s in range(t // blk):
        rows = slice(s * blk, (s + 1) * blk)
        z = z_ref[rows, :] + b_ref[...]
        logf = (jnp.minimum(z, 0.0) - jnp.log1p(jnp.exp(-jnp.abs(z)))) * LOG2E
        hi, mid, lo = _split3(logf)
        cb = (jnp.dot(tri, hi, preferred_element_type=F32)
              + jnp.dot(tri, mid, preferred_element_type=F32)
              + jnp.dot(tri, lo, preferred_element_type=F32)) + carry
        carry = cb[blk - 1:blk, :]
        khi = (-cb).astype(BF16).astype(F32)
        kr1 = -cb - khi
        kmid = kr1.astype(BF16).astype(F32)
        klo = kr1 - kmid
        ckey_ref[rows, :] = jnp.where(
            seg == 0, khi,
            jnp.where(seg == 1, pltpu.roll(kmid, FOX_HEADS, 1),
                      jnp.where(seg == 2, pltpu.roll(klo, 2 * FOX_HEADS, 1),
                                jnp.where(lane == LANES - 1, 1.0, 0.0)))).astype(BF16)
        crow_ref[0, :, rows] = cb.T[LOGIT_LANE:LOGIT_LANE + FOX_HEADS, :]


def _decay(p3, bias_row, batch, seq, blk=256):
    return pl.pallas_call(
        functools.partial(_decay_kernel, blk=blk),
        grid=(batch,),
        in_specs=[pl.BlockSpec((seq, LANES), lambda b: (b, LOGIT_BLOCK)),
                  pl.BlockSpec((1, LANES), lambda b: (0, 0))],
        out_specs=[pl.BlockSpec((seq, LANES), lambda b: (b, 0)),
                   pl.BlockSpec((1, FOX_HEADS, seq), lambda b: (b, 0, 0))],
        out_shape=[jax.ShapeDtypeStruct((batch * seq, LANES), BF16),
                   jax.ShapeDtypeStruct((batch, FOX_HEADS, seq), F32)],
        compiler_params=_cparams(("arbitrary",)),
        name="fox_decay_cumsum",
    )(p3, bias_row)


def _transpose_bf16(x):
    return x.astype(F32).T.astype(BF16)


def _normalised(acc, d):
    return acc[:d, :] * (1.0 / jnp.maximum(acc[d:d + 1, :], 1e-30))


def _fox_kernel(q_ref, k_ref, v_ref, ckey_ref, crow_ref, o_ref, vt_ref, ka_ref, *, tq, nq, nh):
    hh = pl.program_id(1)
    i = pl.program_id(2)
    dh = FOX_HEAD_DIM

    def pieces(x, lane, first):
        hi = x.astype(BF16).astype(F32)
        r1 = x - hi
        mid = r1.astype(BF16).astype(F32)
        return jnp.where(lane == first, hi,
                         jnp.where(lane == first + 1, mid,
                                   jnp.where(lane == first + 2, r1 - mid, 0.0)))

    @pl.when(i == 0)
    def _():
        ck = ckey_ref[...]
        src = lax.broadcasted_iota(jnp.int32, (LANES, LANES), 0)
        dst = lax.broadcasted_iota(jnp.int32, (LANES, LANES), 1)
        for h in range(nh):
            vt_ref[h, 0:dh, :] = _transpose_bf16(v_ref[:, h * dh:(h + 1) * dh])
            vt_ref[h, dh:, :] = jnp.ones((vt_ref.shape[1] - dh, vt_ref.shape[2]), BF16)
            first = LOGIT_LANE + hh * nh + h
            pick = (((dst < 3) & (src == first + dst * FOX_HEADS))
                    | ((dst >= 3) & (dst < 6) & (src == LANES - 1)))
            extra = jnp.dot(ck, jnp.where(pick, 1.0, 0.0).astype(BF16),
                            preferred_element_type=F32)
            ka_ref[h, :, 0:dh] = k_ref[:, h * dh:(h + 1) * dh]
            ka_ref[h, :, dh:] = extra.astype(BF16)

    row = lax.broadcasted_iota(jnp.int32, (dh, tq), 0)
    qts = []
    for h in range(nh):
        ci = crow_ref[0, h]
        extra = pieces(ci, row, 3) + jnp.where(row < 3, 1.0, 0.0)
        qts.append(jnp.concatenate(
            [_transpose_bf16(q_ref[:, h * dh:(h + 1) * dh]), extra.astype(BF16)], axis=0))
    rk = lax.broadcasted_iota(jnp.int32, (tq, tq), 0)
    cq = lax.broadcasted_iota(jnp.int32, (tq, tq), 1)

    def scores(h, k0):
        return jnp.dot(ka_ref[h, k0:k0 + tq, :], qts[h], preferred_element_type=F32)

    def variant(n):
        def run():
            starts = [t * tq for t in range(n, -1, -1)]
            tiles = [[jnp.where(rk <= cq, scores(h, starts[0]), NEG)]
                     + [scores(h, k0) for k0 in starts[1:]] for h in range(nh)]
            ms = [jnp.max(tiles[h][0], axis=0, keepdims=True) for h in range(nh)]
            accs = [jnp.dot(vt_ref[h, :, starts[0]:starts[0] + tq],
                            jnp.exp2(tiles[h][0] - ms[h]).astype(BF16),
                            preferred_element_type=F32) for h in range(nh)]
            for t in range(1, n + 1):
                k0 = starts[t]
                for h in range(nh):
                    st = tiles[h][t]
                    m_new = jnp.maximum(ms[h], jnp.max(st, axis=0, keepdims=True))
                    accs[h] = jnp.exp2(ms[h] - m_new) * accs[h] + jnp.dot(
                        vt_ref[h, :, k0:k0 + tq], jnp.exp2(st - m_new).astype(BF16),
                        preferred_element_type=F32)
                    ms[h] = m_new
            for h in range(nh):
                o_ref[:, h * dh:(h + 1) * dh] = _normalised(accs[h], dh).T.astype(o_ref.dtype)
        return run

    lax.switch(i, [variant(n) for n in range(nq)])


def _fox_attention(p1, ccol, crow4, batch, seq, tq=FOX_TQ, nh=FOX_HEADS_PER_STEP):
    nq = seq // tq
    hg = FOX_HEADS // nh
    w = nh * FOX_HEAD_DIM
    return pl.pallas_call(
        functools.partial(_fox_kernel, tq=tq, nq=nq, nh=nh),
        grid=(batch, hg, nq),
        in_specs=[pl.BlockSpec((tq, w), lambda b, hh, i: (b * nq + i, hh)),
                  pl.BlockSpec((seq, w), lambda b, hh, i: (b, hg + hh)),
                  pl.BlockSpec((seq, w), lambda b, hh, i: (b, 2 * hg + hh)),
                  pl.BlockSpec((seq, LANES), lambda b, hh, i: (b, 0)),
                  pl.BlockSpec((1, nh, 1, tq), lambda b, hh, i: (b, hh, 0, i))],
        out_specs=pl.BlockSpec((tq, w), lambda b, hh, i: (b * nq + i, hh)),
        out_shape=jax.ShapeDtypeStruct((batch * seq, FOX_WIDTH), BF16),
        scratch_shapes=[pltpu.VMEM((nh, FOX_HEAD_DIM + BF16_SUBLANES, seq), BF16),
                        pltpu.VMEM((nh, seq, 2 * FOX_HEAD_DIM), BF16)],
        compiler_params=_cparams(("arbitrary", "arbitrary", "arbitrary")),
        name="fox_attention",
    )(p1, p1, p1, ccol, crow4)


def _compress_one(z_refs, pe_ref, w1_ref, w2_ref, nblk):
    half = CMP_BLOCK // 2
    first = jnp.zeros((nblk, CMP_HIDDEN), F32)
    second = jnp.zeros((nblk, CMP_HIDDEN), F32)
    for p in range(half):
        rows = pl.ds(p, nblk, stride=CMP_STRIDE)
        zp = [z_ref[rows, :] for z_ref in z_refs]
        zp = zp[0] if len(zp) == 1 else jnp.concatenate(zp, axis=1)
        first += jnp.dot((zp + pe_ref[p:p + 1, :]).astype(BF16), w1_ref[p],
                         preferred_element_type=F32)
        second += jnp.dot((zp + pe_ref[half + p:half + p + 1, :]).astype(BF16),
                          w1_ref[half + p], preferred_element_type=F32)
    hid = first + pltpu.roll(second, nblk - 1, 0)
    act = (hid * jax.nn.sigmoid(hid)).astype(BF16)
    return jnp.dot(act, w2_ref[...], preferred_element_type=F32)


def _compress_kernel(zk0_ref, zk1_ref, zv_ref, pek_ref, w1k_ref, w2k_ref, gk_ref,
                     pev_ref, w1v_ref, w2v_ref, kc_ref, vc_ref, *, nblk):
    kc = _compress_one((zk0_ref, zk1_ref), pek_ref, w1k_ref, w2k_ref, nblk)
    ms = jnp.sum(kc * kc, axis=-1, keepdims=True) * (1.0 / NSA_QK_DIM)
    kc = kc * lax.rsqrt(ms + RMS_EPS) * gk_ref[...]
    pos = CMP_STRIDE * lax.broadcasted_iota(jnp.int32, (nblk, LANES), 0) + (CMP_BLOCK - 1)
    up = NSA_QK_PAD - LANES
    kc_ref[:, :up] = kc[:, :up].astype(kc_ref.dtype)
    kc_ref[:, up:] = (kc[:, up:] + _key_aug(pos)).astype(kc_ref.dtype)
    vc = _compress_one((zv_ref,), pev_ref, w1v_ref, w2v_ref, nblk)
    vc_ref[...] = vc.T.astype(vc_ref.dtype)


def _compress(p3, pek, w1k, w2k, gk, pev, w1v, w2v, batch, seq):
    g = NSA_KV_GROUPS
    nblk = seq // CMP_STRIDE
    full2 = lambda b, gg: (0, 0)
    full3 = lambda b, gg: (0, 0, 0)
    return pl.pallas_call(
        functools.partial(_compress_kernel, nblk=nblk),
        grid=(batch, g),
        in_specs=[pl.BlockSpec((seq, LANES), lambda b, gg: (b, 2 * gg)),
                  pl.BlockSpec((seq, LANES), lambda b, gg: (b, 2 * gg + 1)),
                  pl.BlockSpec((seq, NSA_V_DIM), lambda b, gg: (b, 4 + gg)),
                  pl.BlockSpec(pek.shape, full2), pl.BlockSpec(w1k.shape, full3),
                  pl.BlockSpec(w2k.shape, full2), pl.BlockSpec(gk.shape, full2),
                  pl.BlockSpec(pev.shape, full2), pl.BlockSpec(w1v.shape, full3),
                  pl.BlockSpec(w2v.shape, full2)],
        out_specs=[pl.BlockSpec((nblk, NSA_QK_PAD), lambda b, gg: (b * g + gg, 0)),
                   pl.BlockSpec((NSA_V_DIM, nblk), lambda b, gg: (b * g + gg, 0))],
        out_shape=[jax.ShapeDtypeStruct((batch * g * nblk, NSA_QK_PAD), BF16),
                   jax.ShapeDtypeStruct((batch * g * NSA_V_DIM, nblk), BF16)],
        compiler_params=_cparams(("arbitrary", "arbitrary")),
        name="nsa_compress",
    )(p3, p3, p3, pek, w1k, w2k, gk, pev, w1v, w2v)


def _q_heads_t(qt_ref, first_head):
    return jnp.concatenate(
        [qt_ref[(first_head + hh) * NSA_QK_PAD:(first_head + hh + 1) * NSA_QK_PAD, :]
         for hh in range(NSA_HPG)], axis=1)


def _gate_rows(glt_ref, g, hh):
    base = LOGIT_LANE + FOX_HEADS + (g * NSA_HPG + hh) * 3
    return [glt_ref[pl.ds(base + br, 1), :] for br in range(3)]


def _nsa_select_kernel(q_ref, kc_ref, vct_ref, gl_ref, ovt_ref, xn_ref, wk_ref, wv_ref, gk_ref,
                       sel_ref, ocmp_ref, pk_ref, pv_ref, glt_ref, wkb_ref, wvb_ref,
                       *, tq, n_cmp, n_sel):
    g = pl.program_id(1)
    t0 = pl.program_id(2) * tq
    hpg = NSA_HPG
    dv = NSA_V_DIM

    @pl.when(pl.program_id(2) == 0)
    def _():
        _stage_weight(wk_ref, wkb_ref, True, NSA_QK_DIM, NSA_QK_PAD)
        _stage_weight(wv_ref, wvb_ref, True)

    xn = xn_ref[...]
    yk = jnp.dot(xn, wkb_ref[...], preferred_element_type=F32)
    pv_ref[...] = jnp.dot(xn, wvb_ref[...], preferred_element_type=F32).astype(pv_ref.dtype)
    pos = t0 + lax.broadcasted_iota(jnp.int32, (tq, LANES), 0)
    for c in range(NSA_KV_GROUPS):
        sl = slice(c * NSA_QK_PAD, (c + 1) * NSA_QK_PAD)
        yc = yk[:, sl]
        rs = lax.rsqrt(jnp.sum(yc * yc, axis=-1, keepdims=True) * (1.0 / NSA_QK_DIM) + RMS_EPS)
        out = yc * rs * gk_ref[:, sl]
        up = slice((c + 1) * NSA_QK_PAD - LANES, (c + 1) * NSA_QK_PAD)
        pk_ref[:, c * NSA_QK_PAD:(c + 1) * NSA_QK_PAD - LANES] = (
            out[:, :NSA_QK_PAD - LANES].astype(pk_ref.dtype))
        pk_ref[:, up] = (out[:, NSA_QK_PAD - LANES:] + _key_aug(pos)).astype(pk_ref.dtype)

    q4t = _q_heads_t(q_ref, 0)
    rk = lax.broadcasted_iota(jnp.int32, (LANES, tq), 0)
    cq = lax.broadcasted_iota(jnp.int32, (LANES, tq), 1)

    s_c = jnp.dot(kc_ref[...], q4t, preferred_element_type=F32)
    dist_c = (t0 + cq) - (CMP_STRIDE * rk + (CMP_BLOCK - 1))
    mask_c = jnp.where(rk < n_cmp, dist_c, -1) >= 0
    probs = []
    p_sum = jnp.zeros((LANES, tq), F32)
    for hh in range(hpg):
        sm = jnp.where(mask_c, s_c[:, hh * tq:(hh + 1) * tq], NEG)
        m = jnp.max(sm, axis=0, keepdims=True)
        e = jnp.where(mask_c, jnp.exp2(sm - m), 0.0)
        p = e * (1.0 / jnp.maximum(jnp.sum(e, axis=0, keepdims=True), 1e-30))
        probs.append(p)
        p_sum = p_sum + p
    o_cmp = jnp.dot(vct_ref[...], jnp.concatenate(probs, axis=1).astype(BF16),
                    preferred_element_type=F32)

    ph = p_sum.astype(BF16)
    plo = (p_sum - ph.astype(F32)).astype(BF16)
    ovt = ovt_ref[...]
    imp = (jnp.dot(ovt, ph, preferred_element_type=F32)
           + jnp.dot(ovt, plo, preferred_element_type=F32))[:n_sel, :]

    rj = lax.broadcasted_iota(jnp.int32, (n_sel, tq), 0)
    tcol = t0 + lax.broadcasted_iota(jnp.int32, (n_sel, tq), 1)
    back = (tcol >> (SEL_BLOCK.bit_length() - 1)) - rj
    elig = back >= 0
    forced = jnp.where(rj == 0, 0, jnp.where(elig, back, SEL_LOCAL)) < SEL_LOCAL
    score = jnp.where(elig, jnp.where(forced, FORCE_SCORE, imp), -1.0)
    rank = jnp.zeros((n_sel, tq), F32)
    for jp in range(n_sel):
        row = score[jp:jp + 1, :]
        later = jnp.where(rj > jp, 1.0, 0.0)
        rank = rank + jnp.where(row > score, 1.0, jnp.where(row == score, later, 0.0))
    sel_ref[0:n_sel, :] = jnp.where(elig, jnp.where(rank < SEL_TOPK, 0.0, NEG), NEG)
    sel_ref[n_sel:, :] = jnp.full((LANES - n_sel, tq), NEG, F32)

    glt_ref[...] = jax.nn.sigmoid(gl_ref[...]).T
    for hh in range(hpg):
        gate = _gate_rows(glt_ref, g, hh)[0]
        ocmp_ref[:, hh * dv:(hh + 1) * dv] = (gate * o_cmp[:, hh * tq:(hh + 1) * tq]).T


def _nsa_select(pq, p3, kc, vct, ovt, xn, wt, k_rows, v_rows, gain_k, batch, seq,
                tq=NSA_SELECT_TQ):
    g = NSA_KV_GROUPS
    nq = seq // tq
    nblk = seq // CMP_STRIDE
    n_cmp = nblk - CMP_BLOCK // CMP_STRIDE + 1
    d = xn.shape[1]
    kw_out = g * NSA_QK_PAD
    return pl.pallas_call(
        functools.partial(_nsa_select_kernel, tq=tq, n_cmp=n_cmp, n_sel=seq // SEL_BLOCK),
        grid=(batch, g, nq),
        in_specs=[
            pl.BlockSpec((NSA_HPG * NSA_QK_PAD, tq), lambda b, gg, i: (gg, b * nq + i)),
            pl.BlockSpec((nblk, NSA_QK_PAD), lambda b, gg, i: (b * g + gg, 0)),
            pl.BlockSpec((NSA_V_DIM, nblk), lambda b, gg, i: (b * g + gg, 0)),
            pl.BlockSpec((tq, LANES), lambda b, gg, i: (b * nq + i, LOGIT_BLOCK)),
            pl.BlockSpec(ovt.shape, lambda b, gg, i: (0, 0)),
            pl.BlockSpec((tq, d), lambda b, gg, i: (b * nq + i, 0)),
            _row_window(k_rows, KV_K, d, grid_arg=1),
            _row_window(v_rows, KV_V, d, grid_arg=1),
            pl.BlockSpec((1, kw_out), lambda b, gg, i: (0, gg)),
        ],
        out_specs=[pl.BlockSpec((LANES, tq), lambda b, gg, i: (b * g + gg, i)),
                   pl.BlockSpec((tq, NSA_HPG * NSA_V_DIM), lambda b, gg, i: (b * nq + i, gg)),
                   pl.BlockSpec((tq, kw_out), lambda b, gg, i: (b * nq + i, gg)),
                   pl.BlockSpec((tq, KV_V), lambda b, gg, i: (b * nq + i, gg))],
        out_shape=[jax.ShapeDtypeStruct((batch * g * LANES, seq), F32),
                   jax.ShapeDtypeStruct((batch * seq, NSA_WIDTH), F32),
                   jax.ShapeDtypeStruct((batch * seq, 2 * kw_out), BF16),
                   jax.ShapeDtypeStruct((batch * seq, 2 * KV_V), BF16)],
        scratch_shapes=[pltpu.VMEM((LANES, tq), F32), pltpu.VMEM((d, kw_out), BF16),
                        pltpu.VMEM((d, KV_V), BF16)],
        compiler_params=_cparams(("arbitrary", "arbitrary", "arbitrary")),
        name="nsa_select",
    )(pq, kc, vct, p3, ovt, xn, wt, wt, gain_k.reshape(1, 2 * kw_out))


def _nsa_attend_kernel(q_ref, ks_ref, vs_ref, kw_ref, vw_ref, gl_ref, sel_ref,
                       ocmp_ref, o_ref, vst_ref, vwt_ref, glt_ref, *, tq, n_var):
    i = pl.program_id(1)
    t0 = i * tq
    hpg = NSA_HPG
    ng = NSA_KV_GROUPS
    dv = NSA_V_DIM
    dk = NSA_QK_PAD

    @pl.when(i == 0)
    def _():
        ones = jnp.ones((vst_ref.shape[1] - dv, vst_ref.shape[2]), BF16)
        for gg in range(ng):
            vst_ref[gg, 0:dv, :] = _transpose_bf16(vs_ref[:, gg * dv:(gg + 1) * dv])
            vst_ref[gg, dv:, :] = ones
            vwt_ref[gg, 0:dv, :] = _transpose_bf16(vw_ref[:, gg * dv:(gg + 1) * dv])
            vwt_ref[gg, dv:, :] = ones

    q4t = [_q_heads_t(q_ref, gg * hpg) for gg in range(ng)]
    glt_ref[...] = jax.nn.sigmoid(gl_ref[...]).T

    def distance(k0, rows):
        return (lax.broadcasted_iota(jnp.int32, (rows, tq), 1)
                - lax.broadcasted_iota(jnp.int32, (rows, tq), 0)) + (t0 - k0)

    def scores(gg, k, mask_bias):
        st = jnp.dot(k, q4t[gg], preferred_element_type=F32)
        return st + jnp.concatenate([mask_bias] * hpg, axis=1)

    def selection_bias(gg, k0):
        j0 = gg * LANES + k0 // SEL_BLOCK
        return jnp.concatenate(
            [jnp.broadcast_to(sel_ref[j0 + j:j0 + j + 1, :], (SEL_BLOCK, tq))
             for j in range(SLC_TILE // SEL_BLOCK)], axis=0)

    def variant(n):
        def run():
            kw0 = pl.multiple_of(jnp.maximum(t0 - WINDOW, 0), LANES)
            wrows = WINDOW + tq
            dist_w = distance(kw0, wrows)
            bias_w = jnp.where(jnp.where(dist_w >= 0, dist_w, WINDOW) < WINDOW, 0.0, NEG)
            o_win = []
            for gg in range(ng):
                sw = scores(gg, kw_ref[pl.ds(kw0, wrows), gg * dk:(gg + 1) * dk], bias_w)
                pw = jnp.exp2(sw - jnp.max(sw, axis=0, keepdims=True))
                o_win.append(_normalised(
                    jnp.dot(vwt_ref[gg, :, pl.ds(kw0, wrows)], pw.astype(BF16),
                            preferred_element_type=F32), dv))

            starts = [t * SLC_TILE for t in range(n, -1, -1)]
            causal = distance(starts[0], SLC_TILE) >= 0
            tiles = [[] for _ in range(ng)]
            for t, k0 in enumerate(starts):
                for gg in range(ng):
                    bias = selection_bias(gg, k0)
                    if t == 0:
                        bias = jnp.where(causal, bias, NEG)
                    tiles[gg].append(scores(
                        gg, ks_ref[k0:k0 + SLC_TILE, gg * dk:(gg + 1) * dk], bias))
            ms = [jnp.max(tiles[gg][0], axis=0, keepdims=True) for gg in range(ng)]
            accs = [jnp.dot(vst_ref[gg, :, starts[0]:starts[0] + SLC_TILE],
                            jnp.exp2(tiles[gg][0] - ms[gg]).astype(BF16),
                            preferred_element_type=F32) for gg in range(ng)]
            for t in range(1, n + 1):
                k0 = starts[t]
                for gg in range(ng):
                    st = tiles[gg][t]
                    m_new = jnp.maximum(ms[gg], jnp.max(st, axis=0, keepdims=True))
                    accs[gg] = jnp.exp2(ms[gg] - m_new) * accs[gg] + jnp.dot(
                        vst_ref[gg, :, k0:k0 + SLC_TILE], jnp.exp2(st - m_new).astype(BF16),
                        preferred_element_type=F32)
                    ms[gg] = m_new

            for gg in range(ng):
                o_slc = _normalised(accs[gg], dv)
                for hh in range(hpg):
                    _, g_slc, g_win = _gate_rows(glt_ref, gg, hh)
                    lanes = slice(hh * tq, (hh + 1) * tq)
                    out = g_slc * o_slc[:, lanes] + g_win * o_win[gg][:, lanes]
                    cols = slice((gg * hpg + hh) * dv, (gg * hpg + hh + 1) * dv)
                    o_ref[:, cols] = (ocmp_ref[:, cols] + out.T).astype(o_ref.dtype)
        return run

    lax.switch(t0 // SLC_TILE, [variant(n) for n in range(n_var)])


def _nsa_attend(pq, pk, pv, p3, sel, ocmp, batch, seq):
    tq = NSA_ATTEND_TQ
    g = NSA_KV_GROUPS
    nq = seq // tq
    vrows = NSA_V_DIM + BF16_SUBLANES
    return pl.pallas_call(
        functools.partial(_nsa_attend_kernel, tq=tq, n_var=seq // SLC_TILE),
        grid=(batch, nq),
        in_specs=[
            pl.BlockSpec((NSA_HEADS * NSA_QK_PAD, tq), lambda b, i: (0, b * nq + i)),
            pl.BlockSpec((seq, g * NSA_QK_PAD), lambda b, i: (b, 0)),
            pl.BlockSpec((seq, g * NSA_V_DIM), lambda b, i: (b, 0)),
            pl.BlockSpec((seq, g * NSA_QK_PAD), lambda b, i: (b, 1)),
            pl.BlockSpec((seq, g * NSA_V_DIM), lambda b, i: (b, 1)),
            pl.BlockSpec((tq, LANES), lambda b, i: (b * nq + i, LOGIT_BLOCK)),
            pl.BlockSpec((g * LANES, tq), lambda b, i: (b, i)),
            pl.BlockSpec((tq, NSA_WIDTH), lambda b, i: (b * nq + i, 0)),
        ],
        out_specs=pl.BlockSpec((tq, NSA_WIDTH), lambda b, i: (b * nq + i, 0)),
        out_shape=jax.ShapeDtypeStruct((batch * seq, NSA_WIDTH), BF16),
        scratch_shapes=[pltpu.VMEM((g, vrows, seq), BF16), pltpu.VMEM((g, vrows, seq), BF16),
                        pltpu.VMEM((LANES, tq), F32)],
        compiler_params=_cparams(("arbitrary", "arbitrary")),
        name="nsa_attend",
    )(pq, pk, pv, pk, pv, p3, sel, ocmp)


def _pad_gain(gain, scale=1.0):
    return jnp.pad(gain * scale, (0, NSA_QK_PAD - NSA_QK_DIM))


def _overlap_matrix(nc, ns):
    i = np.arange(nc)[:, None]
    j = np.arange(ns)[None, :]
    lo = np.maximum(i * CMP_STRIDE, j * SEL_BLOCK)
    hi = np.minimum(i * CMP_STRIDE + CMP_BLOCK, (j + 1) * SEL_BLOCK)
    return (np.maximum(hi - lo, 0) / CMP_STRIDE).astype(np.float32)


def kernel(x, norm_attn, w_in, fox_f_bias, fox_q_gain, fox_k_gain,
           nsa_q_gain, nsa_kc_gain, nsa_ks_gain, nsa_kw_gain,
           cmp_pe_k, cmp_w1_k, cmp_w2_k, cmp_pe_v, cmp_w1_v, cmp_w2_v,
           w_up_fox, w_up_nsa, w_out, norm_ffn, w_ffn_gate, w_ffn_up, w_ffn_down):
    batch, seq, d = x.shape
    m = batch * seq
    depth = w_in.shape[0]
    pts = [0] + [int(p) for p in np.cumsum(IN_SPLITS)]
    nblk = seq // CMP_STRIDE
    n_cmp = nblk - CMP_BLOCK // CMP_STRIDE + 1
    ns = seq // SEL_BLOCK

    slope = jnp.exp2(-8.0 * jnp.arange(1, NSA_HEADS + 1, dtype=F32) / NSA_HEADS) * LOG2E
    s1, s2, s3 = [p.astype(F32) for p in _split3(slope)]
    q_spare = jnp.stack([256.0 * s1, 256.0 * s2, 256.0 * s3, s1, s2, s3, -slope, -slope, -slope],
                        axis=1)
    q_spare = jnp.pad(q_spare, ((0, 0), (0, NSA_QK_PAD - NSA_QK_DIM - 9)))
    ovt_np = np.zeros((LANES, nblk), np.float32)
    ovt_np[:ns, :n_cmp] = _overlap_matrix(n_cmp, ns).T
    ovt = jnp.asarray(ovt_np, BF16)

    w_in_t = jnp.swapaxes(w_in, 1, 2)

    xf = x.reshape(m, d)
    for l in range(depth):
        wt = w_in_t[l]
        row = dict(zip(("fq", "fk", "fv", "fl", "nq", "kc", "vc", "ks", "vs", "kw", "vw", "ng",
                        "ga", "gb"), pts))

        gain1 = jnp.concatenate([jnp.tile(fox_q_gain[l] * (FOX_HEAD_DIM ** -0.5 * LOG2E), FOX_HEADS),
                                 jnp.tile(fox_k_gain[l], FOX_HEADS),
                                 jnp.ones((FOX_WIDTH,), F32)])
        flag1 = jnp.concatenate([jnp.ones((2 * FOX_WIDTH,), F32), jnp.zeros((FOX_WIDTH,), F32)])
        q_gain = jnp.broadcast_to(nsa_q_gain[l] * (NSA_QK_DIM ** -0.5 * LOG2E),
                                  (NSA_HEADS, NSA_QK_DIM))
        q_table = jnp.broadcast_to(
            jnp.concatenate([q_gain, q_spare], axis=1).reshape(NSA_HEADS * NSA_QK_PAD, 1),
            (NSA_HEADS * NSA_QK_PAD, LANES))
        gain_k = jnp.concatenate([jnp.tile(_pad_gain(nsa_ks_gain[l]), NSA_KV_GROUPS),
                                  jnp.tile(_pad_gain(nsa_kw_gain[l]), NSA_KV_GROUPS)])
        pad_d = NSA_QK_PAD - NSA_QK_DIM
        n_small = FOX_HEADS + 3 * NSA_HEADS
        w3 = jnp.concatenate([wt[row["kc"]:row["kc"] + NSA_QK_DIM],
                              wt[row["fl"]:row["nq"]], wt[row["ng"]:row["ga"]],
                              jnp.zeros((pad_d - n_small, d), F32),
                              wt[row["kc"] + NSA_QK_DIM:row["vc"]], jnp.zeros((pad_d, d), F32),
                              wt[row["vc"]:row["ks"]]], axis=0)

        xn, p3 = _rmsnorm_project(xf, norm_attn[l], w3)
        q_tile = NSA_HPG * NSA_QK_DIM
        p1 = _project(xn, wt, list(range(0, 3 * FOX_WIDTH, FOX_PROJ_TN)), FOX_PROJ_TN, gain1,
                      flag1, PROJ_TM, "proj_fox", group=FOX_HEAD_DIM)
        pq = _project_qt(xn, wt, [row["nq"], row["nq"] + q_tile], q_tile, q_table, PROJ_TM, seq,
                         "proj_nsa_q")

        bias_row = jnp.pad(fox_f_bias[l], (LOGIT_LANE, LANES - LOGIT_LANE - FOX_HEADS)).reshape(1, LANES)
        ccol, crow = _decay(p3, bias_row, batch, seq)
        o_a = _fox_attention(p1, ccol, crow.reshape(batch, FOX_HEADS, 1, seq), batch, seq)

        pek = jnp.pad(cmp_pe_k[l], ((0, 0), (0, pad_d)))
        w1k = jnp.pad(cmp_w1_k[l].reshape(CMP_BLOCK, NSA_QK_DIM, CMP_HIDDEN),
                      ((0, 0), (0, pad_d), (0, 0))).astype(BF16)
        w2k = jnp.pad(cmp_w2_k[l], ((0, 0), (0, pad_d))).astype(BF16)
        gk = _pad_gain(nsa_kc_gain[l]).reshape(1, NSA_QK_PAD)
        w1v = cmp_w1_v[l].reshape(CMP_BLOCK, NSA_V_DIM, CMP_HIDDEN).astype(BF16)
        w2v = cmp_w2_v[l].astype(BF16)
        kc, vct = _compress(p3, pek, w1k, w2k, gk, cmp_pe_v[l], w1v, w2v, batch, seq)
        sel, ocmp, pk, pv = _nsa_select(pq, p3, kc, vct, ovt, xn, wt, [row["ks"], row["kw"]],
                                        [row["vs"], row["vw"]], gain_k, batch, seq)
        o_b = _nsa_attend(pq, pk, pv, p3, sel, ocmp, batch, seq)

        merged = _merge(xn, o_a, o_b, wt, row["ga"], row["gb"], w_up_fox[l], w_up_nsa[l])
        hres, hn = _out_proj_norm(merged, w_out[l], xf, norm_ffn[l])

        act = _swiglu(hn, w_ffn_gate[l], w_ffn_up[l])
        xf = _matmul_residual(act, w_ffn_down[l], hres, FFN_DOWN_TM, FFN_DOWN_TN, "ffn_down")
    return xf.reshape(batch, seq, d)
```

```python
import functools

import numpy as np
import jax
import jax.numpy as jnp
from jax import lax
from jax.experimental import pallas as pl
from jax.experimental.pallas import tpu as pltpu

F32 = jnp.float32
BF16 = jnp.bfloat16

D_MODEL = 2048
FOX_HEADS = 8
FOX_HEAD_DIM = 128
FOX_WIDTH = FOX_HEADS * FOX_HEAD_DIM
NSA_HEADS = 8
NSA_KV_GROUPS = 2
NSA_HPG = NSA_HEADS // NSA_KV_GROUPS
NSA_QK_DIM = 192
NSA_QK_PAD = 256
NSA_V_DIM = 128
NSA_WIDTH = NSA_HEADS * NSA_V_DIM
CMP_BLOCK = 32
CMP_STRIDE = 16
CMP_HIDDEN = 256
SEL_BLOCK = 64
SEL_TOPK = 16
SEL_LOCAL = 2
FORCE_SCORE = 1.0e4
WINDOW = 512
KV_K = NSA_KV_GROUPS * NSA_QK_DIM
KV_V = NSA_KV_GROUPS * NSA_V_DIM
D_FF = -(-(8 * D_MODEL) // (3 * 256)) * 256
RMS_EPS = 1e-6
IN_SPLITS = (FOX_WIDTH, FOX_WIDTH, FOX_WIDTH, FOX_HEADS,
             NSA_HEADS * NSA_QK_DIM, KV_K, KV_V, KV_K, KV_V, KV_K, KV_V,
             3 * NSA_HEADS, D_MODEL, D_MODEL)

_NT = (((1,), (1,)), ((), ()))

LANES = 128
SUBLANES = 8
BF16_SUBLANES = 16
NEG = -1.0e30
LOG2E = 1.4426950408889634
SLC_TILE = 512
LOGIT_BLOCK = 1
LOGIT_LANE = NSA_QK_DIM - LANES
VMEM_LIMIT = 56 * 1024 * 1024

RMS_TM = 512
PROJ_TM = 1024
FOX_PROJ_TN = 1024
MERGE_TM, MERGE_TN = 512, 512
OUT_PROJ_TM = 512
SWIGLU_TM, SWIGLU_TN = 1024, 512
FFN_DOWN_TM, FFN_DOWN_TN = 512, 512
FOX_TQ = 512
FOX_HEADS_PER_STEP = 4
NSA_SELECT_TQ = 512
NSA_ATTEND_TQ = 256


def _cparams(sem):
    return pltpu.CompilerParams(dimension_semantics=sem, vmem_limit_bytes=VMEM_LIMIT)


def _rms_proj_kernel(x_ref, g_ref, w_ref, xn_ref, p_ref, wb_ref):
    @pl.when(pl.program_id(0) == 0)
    def _():
        _stage_weight(w_ref, wb_ref, True)

    rows = x_ref.shape[0] // EPILOGUE_SPLIT
    for r in range(EPILOGUE_SPLIT):
        sl = slice(r * rows, (r + 1) * rows)
        x = x_ref[sl, :]
        ms = jnp.mean(x * x, axis=-1, keepdims=True)
        xn = (x * lax.rsqrt(ms + RMS_EPS) * g_ref[...]).astype(BF16)
        xn_ref[sl, :] = xn
        p_ref[sl, :] = jnp.dot(xn, wb_ref[...], preferred_element_type=F32)


def _rmsnorm_project(x, gain, wt, tm=RMS_TM):
    m, d = x.shape
    n = wt.shape[0]
    return pl.pallas_call(
        _rms_proj_kernel,
        grid=(m // tm,),
        in_specs=[pl.BlockSpec((tm, d), lambda i: (i, 0)),
                  pl.BlockSpec((1, d), lambda i: (0, 0)),
                  pl.BlockSpec((n, d), lambda i: (0, 0), pipeline_mode=pl.Buffered(1))],
        out_specs=[pl.BlockSpec((tm, d), lambda i: (i, 0)),
                   pl.BlockSpec((tm, n), lambda i: (i, 0))],
        out_shape=[jax.ShapeDtypeStruct((m, d), BF16), jax.ShapeDtypeStruct((m, n), F32)],
        scratch_shapes=[pltpu.VMEM((d, n), BF16)],
        compiler_params=_cparams(("arbitrary",)),
        name="rmsnorm_proj_f32",
    )(x, gain.reshape(1, d), wt)


STAGE_CHUNK = 512
EPILOGUE_SPLIT = 4


def _stage_weight(w_ref, wb_ref, transposed=False, pad_from=0, pad_to=0):
    if not transposed:
        wb_ref[...] = w_ref[...].astype(BF16)
        return
    n_in, k = w_ref.shape
    for c in range(k // STAGE_CHUNK):
        cols = slice(c * STAGE_CHUNK, (c + 1) * STAGE_CHUNK)
        w = w_ref[:, cols]
        if pad_from != pad_to:
            zero = jnp.zeros((pad_to - pad_from, STAGE_CHUNK), F32)
            w = jnp.concatenate(
                [piece for h in range(n_in // pad_from)
                 for piece in (w[h * pad_from:(h + 1) * pad_from, :], zero)], axis=0)
        wb_ref[cols, :] = w.T.astype(BF16)


def _first_m_step():
    return pl.program_id(1) == 0


AUG_LANE = NSA_QK_DIM - LANES


def _key_aug(pos):
    lane = lax.broadcasted_iota(jnp.int32, pos.shape, 1)
    hi = (pos >> 8).astype(F32)
    lo = (pos & 255).astype(F32)
    return jnp.where(lane < AUG_LANE, 0.0,
                     jnp.where(lane < AUG_LANE + 3, hi,
                               jnp.where(lane < AUG_LANE + 6, lo,
                                         jnp.where(lane < AUG_LANE + 9, 1.0, 0.0))))


def _proj_kernel(a_ref, w_ref, gain_ref, flag_ref, o_ref, wb_ref, *, group):
    @pl.when(_first_m_step())
    def _():
        _stage_weight(w_ref, wb_ref, True)

    rows = a_ref.shape[0] // EPILOGUE_SPLIT
    ys = [jnp.dot(a_ref[r * rows:(r + 1) * rows, :], wb_ref[...], preferred_element_type=F32)
          for r in range(EPILOGUE_SPLIT)]
    for r, y in enumerate(ys):
        for c in range(y.shape[1] // group):
            sl = slice(c * group, (c + 1) * group)
            yc = y[:, sl]
            ss = jnp.sum(yc * yc, axis=-1, keepdims=True)
            rs = lax.rsqrt(ss * (1.0 / group) + RMS_EPS)
            scale = jnp.where(flag_ref[:, sl] > 0.0, rs, 1.0)
            o_ref[r * rows:(r + 1) * rows, sl] = (
                yc * scale * gain_ref[:, sl]).astype(o_ref.dtype)


def _row_window(starts, rows, k, grid_arg=0):
    def index(*grid):
        j = grid[grid_arg]
        start = starts[-1]
        for t in range(len(starts) - 2, -1, -1):
            start = jnp.where(j == t, starts[t], start)
        return pl.multiple_of(start, SUBLANES), 0
    assert all(s % SUBLANES == 0 for s in starts)
    return pl.BlockSpec((pl.Element(rows), pl.Element(k)), index)


def _project(a, wt, starts, tn, gain, flag, tm, name, group):
    m, k = a.shape
    n_out = len(starts) * tn
    return pl.pallas_call(
        functools.partial(_proj_kernel, group=group),
        grid=(len(starts), m // tm),
        in_specs=[pl.BlockSpec((tm, k), lambda j, i: (i, 0)),
                  _row_window(starts, tn, k),
                  pl.BlockSpec((1, tn), lambda j, i: (0, j)),
                  pl.BlockSpec((1, tn), lambda j, i: (0, j))],
        out_specs=pl.BlockSpec((tm, tn), lambda j, i: (i, j)),
        out_shape=jax.ShapeDtypeStruct((m, n_out), BF16),
        scratch_shapes=[pltpu.VMEM((k, tn), BF16)],
        compiler_params=_cparams(("arbitrary", "arbitrary")),
        name=name,
    )(a, wt, gain.reshape(1, n_out), flag.reshape(1, n_out))


def _proj_qt_kernel(a_ref, w_ref, tab_ref, o_ref, wb_ref, *, head, head_pad, seq):
    @pl.when(_first_m_step())
    def _():
        wb_ref[...] = w_ref[...].astype(BF16)

    tm = a_ref.shape[0]
    cols = tm // EPILOGUE_SPLIT
    reps = cols // LANES
    spare = head_pad - head
    ys = [lax.dot_general(wb_ref[...], a_ref[r * cols:(r + 1) * cols, :], _NT,
                          preferred_element_type=F32) for r in range(EPILOGUE_SPLIT)]
    row = lax.broadcasted_iota(jnp.int32, (spare, cols), 0)
    for r, y in enumerate(ys):
        lanes = slice(r * cols, (r + 1) * cols)
        pos = ((pl.program_id(1) * tm) % seq + r * cols
               + lax.broadcasted_iota(jnp.int32, (1, cols), 1)).astype(F32)
        for h in range(y.shape[0] // head):
            yh = y[h * head:(h + 1) * head, :]
            rs = lax.rsqrt(jnp.sum(yh * yh, axis=0, keepdims=True) * (1.0 / head) + RMS_EPS)
            slot = h * head_pad
            gain = jnp.concatenate([tab_ref[slot:slot + head, :]] * reps, axis=1)
            o_ref[slot:slot + head, lanes] = (yh * rs * gain).astype(o_ref.dtype)
            tab = jnp.concatenate([tab_ref[slot + head:slot + head_pad, :]] * reps, axis=1)
            u = pos * tab
            hi = u.astype(BF16).astype(F32)
            r1 = u - hi
            mid = r1.astype(BF16).astype(F32)
            extra = jnp.where(row < 6, tab,
                              jnp.where(row == 6, hi,
                                        jnp.where(row == 7, mid,
                                                  jnp.where(row == 8, r1 - mid, 0.0))))
            o_ref[slot + head:slot + head_pad, lanes] = extra.astype(o_ref.dtype)


def _project_qt(a, wt, starts, tn_in, table, tm, seq, name):
    m, k = a.shape
    tn_out = tn_in // NSA_QK_DIM * NSA_QK_PAD
    assert seq % tm == 0
    return pl.pallas_call(
        functools.partial(_proj_qt_kernel, head=NSA_QK_DIM, head_pad=NSA_QK_PAD, seq=seq),
        grid=(len(starts), m // tm),
        in_specs=[pl.BlockSpec((tm, k), lambda j, i: (i, 0)),
                  _row_window(starts, tn_in, k),
                  pl.BlockSpec((tn_out, LANES), lambda j, i: (j, 0))],
        out_specs=pl.BlockSpec((tn_out, tm), lambda j, i: (j, i)),
        out_shape=jax.ShapeDtypeStruct((len(starts) * tn_out, m), BF16),
        scratch_shapes=[pltpu.VMEM((tn_in, k), BF16)],
        compiler_params=_cparams(("arbitrary", "arbitrary")),
        name=name,
    )(a, wt, table)


def _out_norm_kernel(a_ref, w_ref, r_ref, g_ref, h_ref, hn_ref, wb_ref):
    @pl.when(pl.program_id(0) == 0)
    def _():
        _stage_weight(w_ref, wb_ref)

    h = r_ref[...] + jnp.dot(a_ref[...], wb_ref[...], preferred_element_type=F32)
    h_ref[...] = h
    ms = jnp.mean(h * h, axis=-1, keepdims=True)
    hn_ref[...] = (h * lax.rsqrt(ms + RMS_EPS) * g_ref[...]).astype(hn_ref.dtype)


def _out_proj_norm(a, w, res, gain, tm=OUT_PROJ_TM):
    m, k = a.shape
    n = w.shape[1]
    full = lambda i: (0, 0)
    row = lambda i: (i, 0)
    return pl.pallas_call(
        _out_norm_kernel,
        grid=(m // tm,),
        in_specs=[pl.BlockSpec((tm, k), row),
                  pl.BlockSpec((k, n), full, pipeline_mode=pl.Buffered(1)),
                  pl.BlockSpec((tm, n), row), pl.BlockSpec((1, n), full)],
        out_specs=[pl.BlockSpec((tm, n), row), pl.BlockSpec((tm, n), row)],
        out_shape=[jax.ShapeDtypeStruct((m, n), F32), jax.ShapeDtypeStruct((m, n), BF16)],
        scratch_shapes=[pltpu.VMEM((k, n), BF16)],
        compiler_params=_cparams(("arbitrary",)),
        name="out_proj_norm",
    )(a, w, res, gain.reshape(1, n))


def _mm_res_kernel(a_ref, w_ref, r_ref, o_ref, wb_ref):
    @pl.when(_first_m_step())
    def _():
        _stage_weight(w_ref, wb_ref)

    o_ref[...] = r_ref[...] + jnp.dot(a_ref[...], wb_ref[...], preferred_element_type=F32)


def _matmul_residual(a, w, res, tm, tn, name):
    m, k = a.shape
    n = w.shape[1]
    return pl.pallas_call(
        _mm_res_kernel,
        grid=(n // tn, m // tm),
        in_specs=[pl.BlockSpec((tm, k), lambda j, i: (i, 0)),
                  pl.BlockSpec((k, tn), lambda j, i: (0, j)),
                  pl.BlockSpec((tm, tn), lambda j, i: (i, j))],
        out_specs=pl.BlockSpec((tm, tn), lambda j, i: (i, j)),
        out_shape=jax.ShapeDtypeStruct((m, n), F32),
        scratch_shapes=[pltpu.VMEM((k, tn), BF16)],
        compiler_params=_cparams(("arbitrary", "arbitrary")),
        name=name,
    )(a, w, res)


def _merge_kernel(xn_ref, oa_ref, ob_ref, wga_ref, wgb_ref, wuf_ref, wun_ref, o_ref,
                  bga_ref, bgb_ref, buf_ref, bun_ref):
    @pl.when(_first_m_step())
    def _():
        _stage_weight(wga_ref, bga_ref, True)
        _stage_weight(wgb_ref, bgb_ref, True)
        _stage_weight(wuf_ref, buf_ref)
        _stage_weight(wun_ref, bun_ref)

    xn = xn_ref[...]
    ga = jax.nn.sigmoid(jnp.dot(xn, bga_ref[...], preferred_element_type=F32))
    ua = jnp.dot(oa_ref[...], buf_ref[...], preferred_element_type=F32)
    acc = ga * ua
    gb = jax.nn.sigmoid(jnp.dot(xn, bgb_ref[...], preferred_element_type=F32))
    ub = jnp.dot(ob_ref[...], bun_ref[...], preferred_element_type=F32)
    o_ref[...] = (acc + gb * ub).astype(o_ref.dtype)


def _merge(xn, oa, ob, wt, row_a, row_b, wuf, wun, tm=MERGE_TM, tn=MERGE_TN):
    m, d = xn.shape
    n = wuf.shape[1]
    ka = oa.shape[1]
    kb = ob.shape[1]
    row = lambda j, i: (i, 0)
    col = lambda j, i: (0, j)
    return pl.pallas_call(
        _merge_kernel,
        grid=(n // tn, m // tm),
        in_specs=[pl.BlockSpec((tm, d), row), pl.BlockSpec((tm, ka), row),
                  pl.BlockSpec((tm, kb), row),
                  _row_window([row_a + t * tn for t in range(n // tn)], tn, d),
                  _row_window([row_b + t * tn for t in range(n // tn)], tn, d),
                  pl.BlockSpec((ka, tn), col), pl.BlockSpec((kb, tn), col)],
        out_specs=pl.BlockSpec((tm, tn), lambda j, i: (i, j)),
        out_shape=jax.ShapeDtypeStruct((m, n), BF16),
        scratch_shapes=[pltpu.VMEM((d, tn), BF16), pltpu.VMEM((d, tn), BF16),
                        pltpu.VMEM((ka, tn), BF16), pltpu.VMEM((kb, tn), BF16)],
        compiler_params=_cparams(("arbitrary", "arbitrary")),
        name="gated_merge",
    )(xn, oa, ob, wt, wt, wuf, wun)


def _swiglu_kernel(a_ref, wg_ref, wu_ref, o_ref, bg_ref, bu_ref):
    @pl.when(_first_m_step())
    def _():
        _stage_weight(wg_ref, bg_ref)
        _stage_weight(wu_ref, bu_ref)

    a = a_ref[...]
    gt = jnp.dot(a, bg_ref[...], preferred_element_type=F32)
    up = jnp.dot(a, bu_ref[...], preferred_element_type=F32)
    o_ref[...] = (gt * jax.nn.sigmoid(gt) * up).astype(o_ref.dtype)


def _swiglu(a, wg, wu, tm=SWIGLU_TM, tn=SWIGLU_TN):
    m, k = a.shape
    n = wg.shape[1]
    return pl.pallas_call(
        _swiglu_kernel,
        grid=(n // tn, m // tm),
        in_specs=[pl.BlockSpec((tm, k), lambda j, i: (i, 0)),
                  pl.BlockSpec((k, tn), lambda j, i: (0, j)),
                  pl.BlockSpec((k, tn), lambda j, i: (0, j))],
        out_specs=pl.BlockSpec((tm, tn), lambda j, i: (i, j)),
        out_shape=jax.ShapeDtypeStruct((m, n), BF16),
        scratch_shapes=[pltpu.VMEM((k, tn), BF16), pltpu.VMEM((k, tn), BF16)],
        compiler_params=_cparams(("arbitrary", "arbitrary")),
        name="swiglu_up",
    )(a, wg, wu)


def _split3(x):
    hi = x.astype(BF16)
    r1 = x - hi.astype(F32)
    mid = r1.astype(BF16)
    lo = (r1 - mid.astype(F32)).astype(BF16)
    return hi, mid, lo


def _decay_kernel(z_ref, b_ref, ckey_ref, *, blk):
    t = z_ref.shape[0]
    r = lax.broadcasted_iota(jnp.int32, (blk, blk), 0)
    c = lax.broadcasted_iota(jnp.int32, (blk, blk), 1)
    tri = jnp.where(r >= c, 1.0, 0.0).astype(BF16)
    lane = lax.broadcasted_iota(jnp.int32, (blk, LANES), 1)
    seg = (lane - LOGIT_LANE) // FOX_HEADS
    carry = jnp.zeros((1, LANES), F32)
    for s in range(t // blk):
        rows = slice(s * blk, (s + 1) * blk)
        z = z_ref[rows, :] + b_ref[...]
        logf = (jnp.minimum(z, 0.0) - jnp.log1p(jnp.exp(-jnp.abs(z)))) * LOG2E
        hi, mid, lo = _split3(logf)
        cb = (jnp.dot(tri, hi, preferred_element_type=F32)
              + jnp.dot(tri, mid, preferred_element_type=F32)
              + jnp.dot(tri, lo, preferred_element_type=F32)) + carry
        carry = cb[blk - 1:blk, :]
        khi = (-cb).astype(BF16).astype(F32)
        kr1 = -cb - khi
        kmid = kr1.astype(BF16).astype(F32)
        klo = kr1 - kmid
        ckey_ref[rows, :] = jnp.where(
            seg == 0, khi,
            jnp.where(seg == 1, pltpu.roll(kmid, FOX_HEADS, 1),
                      jnp.where(seg == 2, pltpu.roll(klo, 2 * FOX_HEADS, 1),
                                jnp.where(lane == LANES - 1, 1.0, 0.0)))).astype(BF16)


def _decay(p3, bias_row, batch, seq, blk=256):
    return pl.pallas_call(
        functools.partial(_decay_kernel, blk=blk),
        grid=(batch,),
        in_specs=[pl.BlockSpec((seq, LANES), lambda b: (b, LOGIT_BLOCK)),
                  pl.BlockSpec((1, LANES), lambda b: (0, 0))],
        out_specs=pl.BlockSpec((seq, LANES), lambda b: (b, 0)),
        out_shape=jax.ShapeDtypeStruct((batch * seq, LANES), BF16),
        compiler_params=_cparams(("arbitrary",)),
        name="fox_decay_cumsum",
    )(p3, bias_row)


def _transpose_bf16(x):
    return x.astype(F32).T.astype(BF16)


def _normalised(acc, d):
    return acc[:d, :] * (1.0 / jnp.maximum(acc[d:d + 1, :], 1e-30))


def _fox_kernel(q_ref, k_ref, v_ref, ckey_ref, o_ref, vt_ref, ka_ref, *, tq, nq, nh):
    hh = pl.program_id(1)
    i = pl.program_id(2)
    dh = FOX_HEAD_DIM

    @pl.when(i == 0)
    def _():
        ck = ckey_ref[...]
        src = lax.broadcasted_iota(jnp.int32, (LANES, LANES), 0)
        dst = lax.broadcasted_iota(jnp.int32, (LANES, LANES), 1)
        for h in range(nh):
            vt_ref[h, 0:dh, :] = _transpose_bf16(v_ref[:, h * dh:(h + 1) * dh])
            vt_ref[h, dh:, :] = jnp.ones((vt_ref.shape[1] - dh, vt_ref.shape[2]), BF16)
            first = LOGIT_LANE + hh * nh + h
            pick = (((dst < 3) & (src == first + dst * FOX_HEADS))
                    | ((dst >= 3) & (dst < 6) & (src == LANES - 1)))
            extra = jnp.dot(ck, jnp.where(pick, 1.0, 0.0).astype(BF16),
                            preferred_element_type=F32)
            ka_ref[h, :, 0:dh] = k_ref[:, h * dh:(h + 1) * dh]
            ka_ref[h, :, dh:] = extra.astype(BF16)

    ckq = ckey_ref[pl.ds(pl.multiple_of(i * tq, tq), tq), :]
    out_r = lax.broadcasted_iota(jnp.int32, (dh, LANES), 0)
    src_l = lax.broadcasted_iota(jnp.int32, (dh, LANES), 1)
    qts = []
    for h in range(nh):
        first = LOGIT_LANE + hh * nh + h
        sel = (jnp.where((out_r < 3) & (src_l == LANES - 1), 1.0, 0.0)
               - jnp.where((out_r >= 3) & (out_r < 6)
                           & (src_l == first + (out_r - 3) * FOX_HEADS), 1.0, 0.0))
        extra = lax.dot_general(sel.astype(BF16), ckq, (((1,), (1,)), ((), ())),
                                preferred_element_type=F32)
        qts.append(jnp.concatenate(
            [_transpose_bf16(q_ref[:, h * dh:(h + 1) * dh]), extra.astype(BF16)], axis=0))
    rk = lax.broadcasted_iota(jnp.int32, (tq, tq), 0)
    cq = lax.broadcasted_iota(jnp.int32, (tq, tq), 1)

    def scores(h, k0):
        return jnp.dot(ka_ref[h, k0:k0 + tq, :], qts[h], preferred_element_type=F32)

    def variant(n):
        def run():
            starts = [t * tq for t in range(n, -1, -1)]
            tiles = [[jnp.where(rk <= cq, scores(h, starts[0]), NEG)]
                     + [scores(h, k0) for k0 in starts[1:]] for h in range(nh)]
            ms = [jnp.max(tiles[h][0], axis=0, keepdims=True) for h in range(nh)]
            accs = [jnp.dot(vt_ref[h, :, starts[0]:starts[0] + tq],
                            jnp.exp2(tiles[h][0] - ms[h]).astype(BF16),
                            preferred_element_type=F32) for h in range(nh)]
            for t in range(1, n + 1):
                k0 = starts[t]
                for h in range(nh):
                    st = tiles[h][t]
                    m_new = jnp.maximum(ms[h], jnp.max(st, axis=0, keepdims=True))
                    accs[h] = jnp.exp2(ms[h] - m_new) * accs[h] + jnp.dot(
                        vt_ref[h, :, k0:k0 + tq], jnp.exp2(st - m_new).astype(BF16),
                        preferred_element_type=F32)
                    ms[h] = m_new
            for h in range(nh):
                o_ref[:, h * dh:(h + 1) * dh] = _normalised(accs[h], dh).T.astype(o_ref.dtype)
        return run

    lax.switch(i, [variant(n) for n in range(nq)])


def _fox_attention(p1, ckey, batch, seq, tq=FOX_TQ, nh=FOX_HEADS_PER_STEP):
    nq = seq // tq
    hg = FOX_HEADS // nh
    w = nh * FOX_HEAD_DIM
    return pl.pallas_call(
        functools.partial(_fox_kernel, tq=tq, nq=nq, nh=nh),
        grid=(batch, hg, nq),
        in_specs=[pl.BlockSpec((tq, w), lambda b, hh, i: (b * nq + i, hh)),
                  pl.BlockSpec((seq, w), lambda b, hh, i: (b, hg + hh)),
                  pl.BlockSpec((seq, w), lambda b, hh, i: (b, 2 * hg + hh)),
                  pl.BlockSpec((seq, LANES), lambda b, hh, i: (b, 0))],
        out_specs=pl.BlockSpec((tq, w), lambda b, hh, i: (b * nq + i, hh)),
        out_shape=jax.ShapeDtypeStruct((batch * seq, FOX_WIDTH), BF16),
        scratch_shapes=[pltpu.VMEM((nh, FOX_HEAD_DIM + BF16_SUBLANES, seq), BF16),
                        pltpu.VMEM((nh, seq, 2 * FOX_HEAD_DIM), BF16)],
        compiler_params=_cparams(("arbitrary", "arbitrary", "arbitrary")),
        name="fox_attention",
    )(p1, p1, p1, ckey)


def _compress_one(z_refs, pe_ref, w1_ref, w2_ref, nblk):
    half = CMP_BLOCK // 2
    first = jnp.zeros((nblk, CMP_HIDDEN), F32)
    second = jnp.zeros((nblk, CMP_HIDDEN), F32)
    for p in range(half):
        rows = pl.ds(p, nblk, stride=CMP_STRIDE)
        zp = [z_ref[rows, :] for z_ref in z_refs]
        zp = zp[0] if len(zp) == 1 else jnp.concatenate(zp, axis=1)
        first += jnp.dot((zp + pe_ref[p:p + 1, :]).astype(BF16), w1_ref[p],
                         preferred_element_type=F32)
        second += jnp.dot((zp + pe_ref[half + p:half + p + 1, :]).astype(BF16),
                          w1_ref[half + p], preferred_element_type=F32)
    hid = first + pltpu.roll(second, nblk - 1, 0)
    act = (hid * jax.nn.sigmoid(hid)).astype(BF16)
    return jnp.dot(act, w2_ref[...], preferred_element_type=F32)


def _compress_kernel(zk0_ref, zk1_ref, zv_ref, pek_ref, w1k_ref, w2k_ref, gk_ref,
                     pev_ref, w1v_ref, w2v_ref, kc_ref, vc_ref, *, nblk):
    kc = _compress_one((zk0_ref, zk1_ref), pek_ref, w1k_ref, w2k_ref, nblk)
    ms = jnp.sum(kc * kc, axis=-1, keepdims=True) * (1.0 / NSA_QK_DIM)
    kc = kc * lax.rsqrt(ms + RMS_EPS) * gk_ref[...]
    pos = CMP_STRIDE * lax.broadcasted_iota(jnp.int32, (nblk, LANES), 0) + (CMP_BLOCK - 1)
    up = NSA_QK_PAD - LANES
    kc_ref[:, :up] = kc[:, :up].astype(kc_ref.dtype)
    kc_ref[:, up:] = (kc[:, up:] + _key_aug(pos)).astype(kc_ref.dtype)
    vc = _compress_one((zv_ref,), pev_ref, w1v_ref, w2v_ref, nblk)
    vc_ref[...] = vc.T.astype(vc_ref.dtype)


def _compress(p3, pek, w1k, w2k, gk, pev, w1v, w2v, batch, seq):
    g = NSA_KV_GROUPS
    nblk = seq // CMP_STRIDE
    full2 = lambda b, gg: (0, 0)
    full3 = lambda b, gg: (0, 0, 0)
    return pl.pallas_call(
        functools.partial(_compress_kernel, nblk=nblk),
        grid=(batch, g),
        in_specs=[pl.BlockSpec((seq, LANES), lambda b, gg: (b, 2 * gg)),
                  pl.BlockSpec((seq, LANES), lambda b, gg: (b, 2 * gg + 1)),
                  pl.BlockSpec((seq, NSA_V_DIM), lambda b, gg: (b, 4 + gg)),
                  pl.BlockSpec(pek.shape, full2), pl.BlockSpec(w1k.shape, full3),
                  pl.BlockSpec(w2k.shape, full2), pl.BlockSpec(gk.shape, full2),
                  pl.BlockSpec(pev.shape, full2), pl.BlockSpec(w1v.shape, full3),
                  pl.BlockSpec(w2v.shape, full2)],
        out_specs=[pl.BlockSpec((nblk, NSA_QK_PAD), lambda b, gg: (b * g + gg, 0)),
                   pl.BlockSpec((NSA_V_DIM, nblk), lambda b, gg: (b * g + gg, 0))],
        out_shape=[jax.ShapeDtypeStruct((batch * g * nblk, NSA_QK_PAD), BF16),
                   jax.ShapeDtypeStruct((batch * g * NSA_V_DIM, nblk), BF16)],
        compiler_params=_cparams(("arbitrary", "arbitrary")),
        name="nsa_compress",
    )(p3, p3, p3, pek, w1k, w2k, gk, pev, w1v, w2v)


def _q_heads_t(qt_ref, first_head):
    return jnp.concatenate(
        [qt_ref[(first_head + hh) * NSA_QK_PAD:(first_head + hh + 1) * NSA_QK_PAD, :]
         for hh in range(NSA_HPG)], axis=1)


def _gate_rows(glt_ref, g, hh):
    base = LOGIT_LANE + FOX_HEADS + (g * NSA_HPG + hh) * 3
    return [glt_ref[pl.ds(base + br, 1), :] for br in range(3)]


def _nsa_select_kernel(q_ref, kc_ref, vct_ref, gl_ref, ovt_ref, xn_ref, wk_ref, wv_ref, gk_ref,
                       sel_ref, ocmp_ref, pk_ref, pv_ref, glt_ref, wkb_ref, wvb_ref,
                       *, tq, n_cmp, n_sel):
    g = pl.program_id(1)
    t0 = pl.program_id(2) * tq
    hpg = NSA_HPG
    dv = NSA_V_DIM

    @pl.when(pl.program_id(2) == 0)
    def _():
        _stage_weight(wk_ref, wkb_ref, True, NSA_QK_DIM, NSA_QK_PAD)
        _stage_weight(wv_ref, wvb_ref, True)

    xn = xn_ref[...]
    yk = jnp.dot(xn, wkb_ref[...], preferred_element_type=F32)
    pv_ref[...] = jnp.dot(xn, wvb_ref[...], preferred_element_type=F32).astype(pv_ref.dtype)
    pos = t0 + lax.broadcasted_iota(jnp.int32, (tq, LANES), 0)
    for c in range(NSA_KV_GROUPS):
        sl = slice(c * NSA_QK_PAD, (c + 1) * NSA_QK_PAD)
        yc = yk[:, sl]
        rs = lax.rsqrt(jnp.sum(yc * yc, axis=-1, keepdims=True) * (1.0 / NSA_QK_DIM) + RMS_EPS)
        out = yc * rs * gk_ref[:, sl]
        up = slice((c + 1) * NSA_QK_PAD - LANES, (c + 1) * NSA_QK_PAD)
        pk_ref[:, c * NSA_QK_PAD:(c + 1) * NSA_QK_PAD - LANES] = (
            out[:, :NSA_QK_PAD - LANES].astype(pk_ref.dtype))
        pk_ref[:, up] = (out[:, NSA_QK_PAD - LANES:] + _key_aug(pos)).astype(pk_ref.dtype)

    q4t = _q_heads_t(q_ref, 0)
    rk = lax.broadcasted_iota(jnp.int32, (LANES, tq), 0)
    cq = lax.broadcasted_iota(jnp.int32, (LANES, tq), 1)

    s_c = jnp.dot(kc_ref[...], q4t, preferred_element_type=F32)
    dist_c = (t0 + cq) - (CMP_STRIDE * rk + (CMP_BLOCK - 1))
    mask_c = jnp.where(rk < n_cmp, dist_c, -1) >= 0
    probs = []
    p_sum = jnp.zeros((LANES, tq), F32)
    for hh in range(hpg):
        sm = jnp.where(mask_c, s_c[:, hh * tq:(hh + 1) * tq], NEG)
        m = jnp.max(sm, axis=0, keepdims=True)
        e = jnp.where(mask_c, jnp.exp2(sm - m), 0.0)
        p = e * (1.0 / jnp.maximum(jnp.sum(e, axis=0, keepdims=True), 1e-30))
        probs.append(p)
        p_sum = p_sum + p
    o_cmp = jnp.dot(vct_ref[...], jnp.concatenate(probs, axis=1).astype(BF16),
                    preferred_element_type=F32)

    ph = p_sum.astype(BF16)
    plo = (p_sum - ph.astype(F32)).astype(BF16)
    ovt = ovt_ref[...]
    imp = (jnp.dot(ovt, ph, preferred_element_type=F32)
           + jnp.dot(ovt, plo, preferred_element_type=F32))[:n_sel, :]

    rj = lax.broadcasted_iota(jnp.int32, (n_sel, tq), 0)
    tcol = t0 + lax.broadcasted_iota(jnp.int32, (n_sel, tq), 1)
    back = (tcol >> (SEL_BLOCK.bit_length() - 1)) - rj
    elig = back >= 0
    forced = jnp.where(rj == 0, 0, jnp.where(elig, back, SEL_LOCAL)) < SEL_LOCAL
    score = jnp.where(elig, jnp.where(forced, FORCE_SCORE, imp), -1.0)
    rank = jnp.zeros((n_sel, tq), F32)
    for jp in range(n_sel):
        row = score[jp:jp + 1, :]
        later = jnp.where(rj > jp, 1.0, 0.0)
        rank = rank + jnp.where(row > score, 1.0, jnp.where(row == score, later, 0.0))
    sel_ref[0:n_sel, :] = jnp.where(elig, jnp.where(rank < SEL_TOPK, 0.0, NEG), NEG)
    sel_ref[n_sel:, :] = jnp.full((LANES - n_sel, tq), NEG, F32)

    glt_ref[...] = jax.nn.sigmoid(gl_ref[...]).T
    for hh in range(hpg):
        gate = _gate_rows(glt_ref, g, hh)[0]
        ocmp_ref[:, hh * dv:(hh + 1) * dv] = (gate * o_cmp[:, hh * tq:(hh + 1) * tq]).T


def _nsa_select(pq, p3, kc, vct, ovt, xn, wt, k_rows, v_rows, gain_k, batch, seq,
                tq=NSA_SELECT_TQ):
    g = NSA_KV_GROUPS
    nq = seq // tq
    nblk = seq // CMP_STRIDE
    n_cmp = nblk - CMP_BLOCK // CMP_STRIDE + 1
    d = xn.shape[1]
    kw_out = g * NSA_QK_PAD
    return pl.pallas_call(
        functools.partial(_nsa_select_kernel, tq=tq, n_cmp=n_cmp, n_sel=seq // SEL_BLOCK),
        grid=(batch, g, nq),
        in_specs=[
            pl.BlockSpec((NSA_HPG * NSA_QK_PAD, tq), lambda b, gg, i: (gg, b * nq + i)),
            pl.BlockSpec((nblk, NSA_QK_PAD), lambda b, gg, i: (b * g + gg, 0)),
            pl.BlockSpec((NSA_V_DIM, nblk), lambda b, gg, i: (b * g + gg, 0)),
            pl.BlockSpec((tq, LANES), lambda b, gg, i: (b * nq + i, LOGIT_BLOCK)),
            pl.BlockSpec(ovt.shape, lambda b, gg, i: (0, 0)),
            pl.BlockSpec((tq, d), lambda b, gg, i: (b * nq + i, 0)),
            _row_window(k_rows, KV_K, d, grid_arg=1),
            _row_window(v_rows, KV_V, d, grid_arg=1),
            pl.BlockSpec((1, kw_out), lambda b, gg, i: (0, gg)),
        ],
        out_specs=[pl.BlockSpec((LANES, tq), lambda b, gg, i: (b * g + gg, i)),
                   pl.BlockSpec((tq, NSA_HPG * NSA_V_DIM), lambda b, gg, i: (b * nq + i, gg)),
                   pl.BlockSpec((tq, kw_out), lambda b, gg, i: (b * nq + i, gg)),
                   pl.BlockSpec((tq, KV_V), lambda b, gg, i: (b * nq + i, gg))],
        out_shape=[jax.ShapeDtypeStruct((batch * g * LANES, seq), F32),
                   jax.ShapeDtypeStruct((batch * seq, NSA_WIDTH), F32),
                   jax.ShapeDtypeStruct((batch * seq, 2 * kw_out), BF16),
                   jax.ShapeDtypeStruct((batch * seq, 2 * KV_V), BF16)],
        scratch_shapes=[pltpu.VMEM((LANES, tq), F32), pltpu.VMEM((d, kw_out), BF16),
                        pltpu.VMEM((d, KV_V), BF16)],
        compiler_params=_cparams(("arbitrary", "arbitrary", "arbitrary")),
        name="nsa_select",
    )(pq, kc, vct, p3, ovt, xn, wt, wt, gain_k.reshape(1, 2 * kw_out))


def _nsa_attend_kernel(q_ref, ks_ref, vs_ref, kw_ref, vw_ref, gl_ref, sel_ref,
                       ocmp_ref, o_ref, vst_ref, vwt_ref, glt_ref, *, tq, n_var):
    i = pl.program_id(1)
    t0 = i * tq
    hpg = NSA_HPG
    ng = NSA_KV_GROUPS
    dv = NSA_V_DIM
    dk = NSA_QK_PAD

    @pl.when(i == 0)
    def _():
        ones = jnp.ones((vst_ref.shape[1] - dv, vst_ref.shape[2]), BF16)
        for gg in range(ng):
            vst_ref[gg, 0:dv, :] = _transpose_bf16(vs_ref[:, gg * dv:(gg + 1) * dv])
            vst_ref[gg, dv:, :] = ones
            vwt_ref[gg, 0:dv, :] = _transpose_bf16(vw_ref[:, gg * dv:(gg + 1) * dv])
            vwt_ref[gg, dv:, :] = ones

    q4t = [_q_heads_t(q_ref, gg * hpg) for gg in range(ng)]
    glt_ref[...] = jax.nn.sigmoid(gl_ref[...]).T

    def distance(k0, rows):
        return (lax.broadcasted_iota(jnp.int32, (rows, tq), 1)
                - lax.broadcasted_iota(jnp.int32, (rows, tq), 0)) + (t0 - k0)

    def scores(gg, k, mask_bias):
        st = jnp.dot(k, q4t[gg], preferred_element_type=F32)
        return st + jnp.concatenate([mask_bias] * hpg, axis=1)

    def selection_bias(gg, k0):
        j0 = gg * LANES + k0 // SEL_BLOCK
        return jnp.concatenate(
            [jnp.broadcast_to(sel_ref[j0 + j:j0 + j + 1, :], (SEL_BLOCK, tq))
             for j in range(SLC_TILE // SEL_BLOCK)], axis=0)

    def variant(n):
        def run():
            kw0 = pl.multiple_of(jnp.maximum(t0 - WINDOW, 0), LANES)
            wrows = WINDOW + tq
            dist_w = distance(kw0, wrows)
            bias_w = jnp.where(jnp.where(dist_w >= 0, dist_w, WINDOW) < WINDOW, 0.0, NEG)
            o_win = []
            for gg in range(ng):
                sw = scores(gg, kw_ref[pl.ds(kw0, wrows), gg * dk:(gg + 1) * dk], bias_w)
                pw = jnp.exp2(sw - jnp.max(sw, axis=0, keepdims=True))
                o_win.append(_normalised(
                    jnp.dot(vwt_ref[gg, :, pl.ds(kw0, wrows)], pw.astype(BF16),
                            preferred_element_type=F32), dv))

            starts = [t * SLC_TILE for t in range(n, -1, -1)]
            causal = distance(starts[0], SLC_TILE) >= 0
            tiles = [[] for _ in range(ng)]
            for t, k0 in enumerate(starts):
                for gg in range(ng):
                    bias = selection_bias(gg, k0)
                    if t == 0:
                        bias = jnp.where(causal, bias, NEG)
                    tiles[gg].append(scores(
                        gg, ks_ref[k0:k0 + SLC_TILE, gg * dk:(gg + 1) * dk], bias))
            ms = [jnp.max(tiles[gg][0], axis=0, keepdims=True) for gg in range(ng)]
            accs = [jnp.dot(vst_ref[gg, :, starts[0]:starts[0] + SLC_TILE],
                            jnp.exp2(tiles[gg][0] - ms[gg]).astype(BF16),
                            preferred_element_type=F32) for gg in range(ng)]
            for t in range(1, n + 1):
                k0 = starts[t]
                for gg in range(ng):
                    st = tiles[gg][t]
                    m_new = jnp.maximum(ms[gg], jnp.max(st, axis=0, keepdims=True))
                    accs[gg] = jnp.exp2(ms[gg] - m_new) * accs[gg] + jnp.dot(
                        vst_ref[gg, :, k0:k0 + SLC_TILE], jnp.exp2(st - m_new).astype(BF16),
                        preferred_element_type=F32)
                    ms[gg] = m_new

            for gg in range(ng):
                o_slc = _normalised(accs[gg], dv)
                for hh in range(hpg):
                    _, g_slc, g_win = _gate_rows(glt_ref, gg, hh)
                    lanes = slice(hh * tq, (hh + 1) * tq)
                    out = g_slc * o_slc[:, lanes] + g_win * o_win[gg][:, lanes]
                    cols = slice((gg * hpg + hh) * dv, (gg * hpg + hh + 1) * dv)
                    o_ref[:, cols] = (ocmp_ref[:, cols] + out.T).astype(o_ref.dtype)
        return run

    lax.switch(t0 // SLC_TILE, [variant(n) for n in range(n_var)])


def _nsa_attend(pq, pk, pv, p3, sel, ocmp, batch, seq):
    tq = NSA_ATTEND_TQ
    g = NSA_KV_GROUPS
    nq = seq // tq
    vrows = NSA_V_DIM + BF16_SUBLANES
    return pl.pallas_call(
        functools.partial(_nsa_attend_kernel, tq=tq, n_var=seq // SLC_TILE),
        grid=(batch, nq),
        in_specs=[
            pl.BlockSpec((NSA_HEADS * NSA_QK_PAD, tq), lambda b, i: (0, b * nq + i)),
            pl.BlockSpec((seq, g * NSA_QK_PAD), lambda b, i: (b, 0)),
            pl.BlockSpec((seq, g * NSA_V_DIM), lambda b, i: (b, 0)),
            pl.BlockSpec((seq, g * NSA_QK_PAD), lambda b, i: (b, 1)),
            pl.BlockSpec((seq, g * NSA_V_DIM), lambda b, i: (b, 1)),
            pl.BlockSpec((tq, LANES), lambda b, i: (b * nq + i, LOGIT_BLOCK)),
            pl.BlockSpec((g * LANES, tq), lambda b, i: (b, i)),
            pl.BlockSpec((tq, NSA_WIDTH), lambda b, i: (b * nq + i, 0)),
        ],
        out_specs=pl.BlockSpec((tq, NSA_WIDTH), lambda b, i: (b * nq + i, 0)),
        out_shape=jax.ShapeDtypeStruct((batch * seq, NSA_WIDTH), BF16),
        scratch_shapes=[pltpu.VMEM((g, vrows, seq), BF16), pltpu.VMEM((g, vrows, seq), BF16),
                        pltpu.VMEM((LANES, tq), F32)],
        compiler_params=_cparams(("arbitrary", "arbitrary")),
        name="nsa_attend",
    )(pq, pk, pv, pk, pv, p3, sel, ocmp)


def _pad_gain(gain, scale=1.0):
    return jnp.pad(gain * scale, (0, NSA_QK_PAD - NSA_QK_DIM))


def _overlap_matrix(nc, ns):
    i = np.arange(nc)[:, None]
    j = np.arange(ns)[None, :]
    lo = np.maximum(i * CMP_STRIDE, j * SEL_BLOCK)
    hi = np.minimum(i * CMP_STRIDE + CMP_BLOCK, (j + 1) * SEL_BLOCK)
    return (np.maximum(hi - lo, 0) / CMP_STRIDE).astype(np.float32)


def kernel(x, norm_attn, w_in, fox_f_bias, fox_q_gain, fox_k_gain,
           nsa_q_gain, nsa_kc_gain, nsa_ks_gain, nsa_kw_gain,
           cmp_pe_k, cmp_w1_k, cmp_w2_k, cmp_pe_v, cmp_w1_v, cmp_w2_v,
           w_up_fox, w_up_nsa, w_out, norm_ffn, w_ffn_gate, w_ffn_up, w_ffn_down):
    batch, seq, d = x.shape
    m = batch * seq
    depth = w_in.shape[0]
    pts = [0] + [int(p) for p in np.cumsum(IN_SPLITS)]
    nblk = seq // CMP_STRIDE
    n_cmp = nblk - CMP_BLOCK // CMP_STRIDE + 1
    ns = seq // SEL_BLOCK

    slope = jnp.exp2(-8.0 * jnp.arange(1, NSA_HEADS + 1, dtype=F32) / NSA_HEADS) * LOG2E
    s1, s2, s3 = [p.astype(F32) for p in _split3(slope)]
    q_spare = jnp.stack([256.0 * s1, 256.0 * s2, 256.0 * s3, s1, s2, s3, -slope, -slope, -slope],
                        axis=1)
    q_spare = jnp.pad(q_spare, ((0, 0), (0, NSA_QK_PAD - NSA_QK_DIM - 9)))
    ovt_np = np.zeros((LANES, nblk), np.float32)
    ovt_np[:ns, :n_cmp] = _overlap_matrix(n_cmp, ns).T
    ovt = jnp.asarray(ovt_np, BF16)

    w_in_t = jnp.swapaxes(w_in, 1, 2)

    xf = x.reshape(m, d)
    for l in range(depth):
        wt = w_in_t[l]
        row = dict(zip(("fq", "fk", "fv", "fl", "nq", "kc", "vc", "ks", "vs", "kw", "vw", "ng",
                        "ga", "gb"), pts))

        gain1 = jnp.concatenate([jnp.tile(fox_q_gain[l] * (FOX_HEAD_DIM ** -0.5 * LOG2E), FOX_HEADS),
                                 jnp.tile(fox_k_gain[l], FOX_HEADS),
                                 jnp.ones((FOX_WIDTH,), F32)])
        flag1 = jnp.concatenate([jnp.ones((2 * FOX_WIDTH,), F32), jnp.zeros((FOX_WIDTH,), F32)])
        q_gain = jnp.broadcast_to(nsa_q_gain[l] * (NSA_QK_DIM ** -0.5 * LOG2E),
                                  (NSA_HEADS, NSA_QK_DIM))
        q_table = jnp.broadcast_to(
            jnp.concatenate([q_gain, q_spare], axis=1).reshape(NSA_HEADS * NSA_QK_PAD, 1),
            (NSA_HEADS * NSA_QK_PAD, LANES))
        gain_k = jnp.concatenate([jnp.tile(_pad_gain(nsa_ks_gain[l]), NSA_KV_GROUPS),
                                  jnp.tile(_pad_gain(nsa_kw_gain[l]), NSA_KV_GROUPS)])
        pad_d = NSA_QK_PAD - NSA_QK_DIM
        n_small = FOX_HEADS + 3 * NSA_HEADS
        w3 = jnp.concatenate([wt[row["kc"]:row["kc"] + NSA_QK_DIM],
                              wt[row["fl"]:row["nq"]], wt[row["ng"]:row["ga"]],
                              jnp.zeros((pad_d - n_small, d), F32),
                              wt[row["kc"] + NSA_QK_DIM:row["vc"]], jnp.zeros((pad_d, d), F32),
                              wt[row["vc"]:row["ks"]]], axis=0)

        xn, p3 = _rmsnorm_project(xf, norm_attn[l], w3)
        q_tile = NSA_HPG * NSA_QK_DIM
        p1 = _project(xn, wt, list(range(0, 3 * FOX_WIDTH, FOX_PROJ_TN)), FOX_PROJ_TN, gain1,
                      flag1, PROJ_TM, "proj_fox", group=FOX_HEAD_DIM)
        pq = _project_qt(xn, wt, [row["nq"], row["nq"] + q_tile], q_tile, q_table, PROJ_TM, seq,
                         "proj_nsa_q")

        bias_row = jnp.pad(fox_f_bias[l], (LOGIT_LANE, LANES - LOGIT_LANE - FOX_HEADS)).reshape(1, LANES)
        ckey = _decay(p3, bias_row, batch, seq)
        o_a = _fox_attention(p1, ckey, batch, seq)

        pek = jnp.pad(cmp_pe_k[l], ((0, 0), (0, pad_d)))
        w1k = jnp.pad(cmp_w1_k[l].reshape(CMP_BLOCK, NSA_QK_DIM, CMP_HIDDEN),
                      ((0, 0), (0, pad_d), (0, 0))).astype(BF16)
        w2k = jnp.pad(cmp_w2_k[l], ((0, 0), (0, pad_d))).astype(BF16)
        gk = _pad_gain(nsa_kc_gain[l]).reshape(1, NSA_QK_PAD)
        w1v = cmp_w1_v[l].reshape(CMP_BLOCK, NSA_V_DIM, CMP_HIDDEN).astype(BF16)
        w2v = cmp_w2_v[l].astype(BF16)
        kc, vct = _compress(p3, pek, w1k, w2k, gk, cmp_pe_v[l], w1v, w2v, batch, seq)
        sel, ocmp, pk, pv = _nsa_select(pq, p3, kc, vct, ovt, xn, wt, [row["ks"], row["kw"]],
                                        [row["vs"], row["vw"]], gain_k, batch, seq)
        o_b = _nsa_attend(pq, pk, pv, p3, sel, ocmp, batch, seq)

        merged = _merge(xn, o_a, o_b, wt, row["ga"], row["gb"], w_up_fox[l], w_up_nsa[l])
        hres, hn = _out_proj_norm(merged, w_out[l], xf, norm_ffn[l])

        act = _swiglu(hn, w_ffn_gate[l], w_ffn_up[l])
        xf = _matmul_residual(act, w_ffn_down[l], hres, FFN_DOWN_TM, FFN_DOWN_TN, "ffn_down")
    return xf.reshape(batch, seq, d)
```

```python
import functools

import numpy as np
import jax
import jax.numpy as jnp
from jax import lax
from jax.experimental import pallas as pl
from jax.experimental.pallas import tpu as pltpu

F32 = jnp.float32
BF16 = jnp.bfloat16

D_MODEL = 2048
FOX_HEADS = 8
FOX_HEAD_DIM = 128
FOX_WIDTH = FOX_HEADS * FOX_HEAD_DIM
NSA_HEADS = 8
NSA_KV_GROUPS = 2
NSA_HPG = NSA_HEADS // NSA_KV_GROUPS
NSA_QK_DIM = 192
NSA_QK_PAD = 256
NSA_V_DIM = 128
NSA_WIDTH = NSA_HEADS * NSA_V_DIM
CMP_BLOCK = 32
CMP_STRIDE = 16
CMP_HIDDEN = 256
SEL_BLOCK = 64
SEL_TOPK = 16
SEL_LOCAL = 2
FORCE_SCORE = 1.0e4
WINDOW = 512
KV_K = NSA_KV_GROUPS * NSA_QK_DIM
KV_V = NSA_KV_GROUPS * NSA_V_DIM
D_FF = -(-(8 * D_MODEL) // (3 * 256)) * 256
RMS_EPS = 1e-6
IN_SPLITS = (FOX_WIDTH, FOX_WIDTH, FOX_WIDTH, FOX_HEADS,
             NSA_HEADS * NSA_QK_DIM, KV_K, KV_V, KV_K, KV_V, KV_K, KV_V,
             3 * NSA_HEADS, D_MODEL, D_MODEL)

_NT = (((1,), (1,)), ((), ()))

LANES = 128
SUBLANES = 8
BF16_SUBLANES = 16
NEG = -1.0e30
LOG2E = 1.4426950408889634
SLC_TILE = 512
LOGIT_BLOCK = 1
LOGIT_LANE = NSA_QK_DIM - LANES
VMEM_LIMIT = 56 * 1024 * 1024

RMS_TM = 512
PROJ_TM = 1024
FOX_PROJ_TN = 1024
MERGE_TM, MERGE_TN = 512, 512
OUT_PROJ_TM = 512
SWIGLU_TM, SWIGLU_TN = 1024, 512
FFN_DOWN_TM, FFN_DOWN_TN = 512, 512
FOX_TQ = 512
FOX_HEADS_PER_STEP = 4
NSA_SELECT_TQ = 512
NSA_ATTEND_TQ = 256


def _cparams(sem):
    return pltpu.CompilerParams(dimension_semantics=sem, vmem_limit_bytes=VMEM_LIMIT)


def _rms_proj_kernel(x_ref, g_ref, w_ref, xn_ref, p_ref, wb_ref):
    @pl.when(pl.program_id(0) == 0)
    def _():
        _stage_weight(w_ref, wb_ref, True)

    rows = x_ref.shape[0] // EPILOGUE_SPLIT
    for r in range(EPILOGUE_SPLIT):
        sl = slice(r * rows, (r + 1) * rows)
        x = x_ref[sl, :]
        ms = jnp.mean(x * x, axis=-1, keepdims=True)
        xn = (x * lax.rsqrt(ms + RMS_EPS) * g_ref[...]).astype(BF16)
        xn_ref[sl, :] = xn
        p_ref[sl, :] = jnp.dot(xn, wb_ref[...], preferred_element_type=F32)


def _rmsnorm_project(x, gain, wt, tm=RMS_TM):
    m, d = x.shape
    n = wt.shape[0]
    return pl.pallas_call(
        _rms_proj_kernel,
        grid=(m // tm,),
        in_specs=[pl.BlockSpec((tm, d), lambda i: (i, 0)),
                  pl.BlockSpec((1, d), lambda i: (0, 0)),
                  pl.BlockSpec((n, d), lambda i: (0, 0), pipeline_mode=pl.Buffered(1))],
        out_specs=[pl.BlockSpec((tm, d), lambda i: (i, 0)),
                   pl.BlockSpec((tm, n), lambda i: (i, 0))],
        out_shape=[jax.ShapeDtypeStruct((m, d), BF16), jax.ShapeDtypeStruct((m, n), F32)],
        scratch_shapes=[pltpu.VMEM((d, n), BF16)],
        compiler_params=_cparams(("arbitrary",)),
        name="rmsnorm_proj_f32",
    )(x, gain.reshape(1, d), wt)


STAGE_CHUNK = 512
EPILOGUE_SPLIT = 4


def _stage_weight(w_ref, wb_ref, transposed=False, pad_from=0, pad_to=0):
    if not transposed:
        wb_ref[...] = w_ref[...].astype(BF16)
        return
    n_in, k = w_ref.shape
    for c in range(k // STAGE_CHUNK):
        cols = slice(c * STAGE_CHUNK, (c + 1) * STAGE_CHUNK)
        w = w_ref[:, cols]
        if pad_from != pad_to:
            zero = jnp.zeros((pad_to - pad_from, STAGE_CHUNK), F32)
            w = jnp.concatenate(
                [piece for h in range(n_in // pad_from)
                 for piece in (w[h * pad_from:(h + 1) * pad_from, :], zero)], axis=0)
        wb_ref[cols, :] = w.T.astype(BF16)


def _first_m_step():
    return pl.program_id(1) == 0


AUG_LANE = NSA_QK_DIM - LANES


def _key_aug(pos):
    lane = lax.broadcasted_iota(jnp.int32, pos.shape, 1)
    hi = (pos >> 8).astype(F32)
    lo = (pos & 255).astype(F32)
    return jnp.where(lane < AUG_LANE, 0.0,
                     jnp.where(lane < AUG_LANE + 3, hi,
                               jnp.where(lane < AUG_LANE + 6, lo,
                                         jnp.where(lane < AUG_LANE + 9, 1.0, 0.0))))


def _proj_kernel(a_ref, w_ref, gain_ref, flag_ref, o_ref, wb_ref, *, group):
    @pl.when(_first_m_step())
    def _():
        _stage_weight(w_ref, wb_ref, True)

    rows = a_ref.shape[0] // EPILOGUE_SPLIT
    ys = [jnp.dot(a_ref[r * rows:(r + 1) * rows, :], wb_ref[...], preferred_element_type=F32)
          for r in range(EPILOGUE_SPLIT)]
    for r, y in enumerate(ys):
        for c in range(y.shape[1] // group):
            sl = slice(c * group, (c + 1) * group)
            yc = y[:, sl]
            ss = jnp.sum(yc * yc, axis=-1, keepdims=True)
            rs = lax.rsqrt(ss * (1.0 / group) + RMS_EPS)
            scale = jnp.where(flag_ref[:, sl] > 0.0, rs, 1.0)
            o_ref[r * rows:(r + 1) * rows, sl] = (
                yc * scale * gain_ref[:, sl]).astype(o_ref.dtype)


def _row_window(starts, rows, k, grid_arg=0):
    def index(*grid):
        j = grid[grid_arg]
        start = starts[-1]
        for t in range(len(starts) - 2, -1, -1):
            start = jnp.where(j == t, starts[t], start)
        return pl.multiple_of(start, SUBLANES), 0
    assert all(s % SUBLANES == 0 for s in starts)
    return pl.BlockSpec((pl.Element(rows), pl.Element(k)), index)


def _project(a, wt, starts, tn, gain, flag, tm, name, group):
    m, k = a.shape
    n_out = len(starts) * tn
    return pl.pallas_call(
        functools.partial(_proj_kernel, group=group),
        grid=(len(starts), m // tm),
        in_specs=[pl.BlockSpec((tm, k), lambda j, i: (i, 0)),
                  _row_window(starts, tn, k),
                  pl.BlockSpec((1, tn), lambda j, i: (0, j)),
                  pl.BlockSpec((1, tn), lambda j, i: (0, j))],
        out_specs=pl.BlockSpec((tm, tn), lambda j, i: (i, j)),
        out_shape=jax.ShapeDtypeStruct((m, n_out), BF16),
        scratch_shapes=[pltpu.VMEM((k, tn), BF16)],
        compiler_params=_cparams(("arbitrary", "arbitrary")),
        name=name,
    )(a, wt, gain.reshape(1, n_out), flag.reshape(1, n_out))


def _proj_qt_kernel(a_ref, w_ref, tab_ref, o_ref, wb_ref, *, head, head_pad, seq):
    @pl.when(_first_m_step())
    def _():
        wb_ref[...] = w_ref[...].astype(BF16)

    tm = a_ref.shape[0]
    cols = tm // EPILOGUE_SPLIT
    reps = cols // LANES
    spare = head_pad - head
    ys = [lax.dot_general(wb_ref[...], a_ref[r * cols:(r + 1) * cols, :], _NT,
                          preferred_element_type=F32) for r in range(EPILOGUE_SPLIT)]
    row = lax.broadcasted_iota(jnp.int32, (spare, cols), 0)
    for r, y in enumerate(ys):
        lanes = slice(r * cols, (r + 1) * cols)
        pos = ((pl.program_id(1) * tm) % seq + r * cols
               + lax.broadcasted_iota(jnp.int32, (1, cols), 1)).astype(F32)
        for h in range(y.shape[0] // head):
            yh = y[h * head:(h + 1) * head, :]
            rs = lax.rsqrt(jnp.sum(yh * yh, axis=0, keepdims=True) * (1.0 / head) + RMS_EPS)
            slot = h * head_pad
            gain = jnp.concatenate([tab_ref[slot:slot + head, :]] * reps, axis=1)
            o_ref[slot:slot + head, lanes] = (yh * rs * gain).astype(o_ref.dtype)
            tab = jnp.concatenate([tab_ref[slot + head:slot + head_pad, :]] * reps, axis=1)
            u = pos * tab
            hi = u.astype(BF16).astype(F32)
            r1 = u - hi
            mid = r1.astype(BF16).astype(F32)
            extra = jnp.where(row < 6, tab,
                              jnp.where(row == 6, hi,
                                        jnp.where(row == 7, mid,
                                                  jnp.where(row == 8, r1 - mid, 0.0))))
            o_ref[slot + head:slot + head_pad, lanes] = extra.astype(o_ref.dtype)


def _project_qt(a, wt, starts, tn_in, table, tm, seq, name):
    m, k = a.shape
    tn_out = tn_in // NSA_QK_DIM * NSA_QK_PAD
    assert seq % tm == 0
    return pl.pallas_call(
        functools.partial(_proj_qt_kernel, head=NSA_QK_DIM, head_pad=NSA_QK_PAD, seq=seq),
        grid=(len(starts), m // tm),
        in_specs=[pl.BlockSpec((tm, k), lambda j, i: (i, 0)),
                  _row_window(starts, tn_in, k),
                  pl.BlockSpec((tn_out, LANES), lambda j, i: (j, 0))],
        out_specs=pl.BlockSpec((tn_out, tm), lambda j, i: (j, i)),
        out_shape=jax.ShapeDtypeStruct((len(starts) * tn_out, m), BF16),
        scratch_shapes=[pltpu.VMEM((tn_in, k), BF16)],
        compiler_params=_cparams(("arbitrary", "arbitrary")),
        name=name,
    )(a, wt, table)


def _out_norm_kernel(a_ref, w_ref, r_ref, g_ref, h_ref, hn_ref, wb_ref):
    @pl.when(pl.program_id(0) == 0)
    def _():
        _stage_weight(w_ref, wb_ref)

    h = r_ref[...] + jnp.dot(a_ref[...], wb_ref[...], preferred_element_type=F32)
    h_ref[...] = h
    ms = jnp.mean(h * h, axis=-1, keepdims=True)
    hn_ref[...] = (h * lax.rsqrt(ms + RMS_EPS) * g_ref[...]).astype(hn_ref.dtype)


def _out_proj_norm(a, w, res, gain, tm=OUT_PROJ_TM):
    m, k = a.shape
    n = w.shape[1]
    full = lambda i: (0, 0)
    row = lambda i: (i, 0)
    return pl.pallas_call(
        _out_norm_kernel,
        grid=(m // tm,),
        in_specs=[pl.BlockSpec((tm, k), row),
                  pl.BlockSpec((k, n), full, pipeline_mode=pl.Buffered(1)),
                  pl.BlockSpec((tm, n), row), pl.BlockSpec((1, n), full)],
        out_specs=[pl.BlockSpec((tm, n), row), pl.BlockSpec((tm, n), row)],
        out_shape=[jax.ShapeDtypeStruct((m, n), F32), jax.ShapeDtypeStruct((m, n), BF16)],
        scratch_shapes=[pltpu.VMEM((k, n), BF16)],
        compiler_params=_cparams(("arbitrary",)),
        name="out_proj_norm",
    )(a, w, res, gain.reshape(1, n))


def _mm_res_kernel(a_ref, w_ref, r_ref, o_ref, wb_ref):
    @pl.when(_first_m_step())
    def _():
        _stage_weight(w_ref, wb_ref)

    o_ref[...] = r_ref[...] + jnp.dot(a_ref[...], wb_ref[...], preferred_element_type=F32)


def _matmul_residual(a, w, res, tm, tn, name):
    m, k = a.shape
    n = w.shape[1]
    return pl.pallas_call(
        _mm_res_kernel,
        grid=(n // tn, m // tm),
        in_specs=[pl.BlockSpec((tm, k), lambda j, i: (i, 0)),
                  pl.BlockSpec((k, tn), lambda j, i: (0, j)),
                  pl.BlockSpec((tm, tn), lambda j, i: (i, j))],
        out_specs=pl.BlockSpec((tm, tn), lambda j, i: (i, j)),
        out_shape=jax.ShapeDtypeStruct((m, n), F32),
        scratch_shapes=[pltpu.VMEM((k, tn), BF16)],
        compiler_params=_cparams(("arbitrary", "arbitrary")),
        name=name,
    )(a, w, res)


def _merge_kernel(xn_ref, oa_ref, ob_ref, wga_ref, wgb_ref, wuf_ref, wun_ref, o_ref,
                  bga_ref, bgb_ref, buf_ref, bun_ref):
    @pl.when(_first_m_step())
    def _():
        _stage_weight(wga_ref, bga_ref, True)
        _stage_weight(wgb_ref, bgb_ref, True)
        _stage_weight(wuf_ref, buf_ref)
        _stage_weight(wun_ref, bun_ref)

    xn = xn_ref[...]
    ga = jax.nn.sigmoid(jnp.dot(xn, bga_ref[...], preferred_element_type=F32))
    ua = jnp.dot(oa_ref[...], buf_ref[...], preferred_element_type=F32)
    acc = ga * ua
    gb = jax.nn.sigmoid(jnp.dot(xn, bgb_ref[...], preferred_element_type=F32))
    ub = jnp.dot(ob_ref[...], bun_ref[...], preferred_element_type=F32)
    o_ref[...] = (acc + gb * ub).astype(o_ref.dtype)


def _merge(xn, oa, ob, wt, row_a, row_b, wuf, wun, tm=MERGE_TM, tn=MERGE_TN):
    m, d = xn.shape
    n = wuf.shape[1]
    ka = oa.shape[1]
    kb = ob.shape[1]
    row = lambda j, i: (i, 0)
    col = lambda j, i: (0, j)
    return pl.pallas_call(
        _merge_kernel,
        grid=(n // tn, m // tm),
        in_specs=[pl.BlockSpec((tm, d), row), pl.BlockSpec((tm, ka), row),
                  pl.BlockSpec((tm, kb), row),
                  _row_window([row_a + t * tn for t in range(n // tn)], tn, d),
                  _row_window([row_b + t * tn for t in range(n // tn)], tn, d),
                  pl.BlockSpec((ka, tn), col), pl.BlockSpec((kb, tn), col)],
        out_specs=pl.BlockSpec((tm, tn), lambda j, i: (i, j)),
        out_shape=jax.ShapeDtypeStruct((m, n), BF16),
        scratch_shapes=[pltpu.VMEM((d, tn), BF16), pltpu.VMEM((d, tn), BF16),
                        pltpu.VMEM((ka, tn), BF16), pltpu.VMEM((kb, tn), BF16)],
        compiler_params=_cparams(("arbitrary", "arbitrary")),
        name="gated_merge",
    )(xn, oa, ob, wt, wt, wuf, wun)


def _swiglu_kernel(a_ref, wg_ref, wu_ref, o_ref, bg_ref, bu_ref):
    @pl.when(_first_m_step())
    def _():
        _stage_weight(wg_ref, bg_ref)
        _stage_weight(wu_ref, bu_ref)

    a = a_ref[...]
    gt = jnp.dot(a, bg_ref[...], preferred_element_type=F32)
    up = jnp.dot(a, bu_ref[...], preferred_element_type=F32)
    o_ref[...] = (gt * jax.nn.sigmoid(gt) * up).astype(o_ref.dtype)


def _swiglu(a, wg, wu, tm=SWIGLU_TM, tn=SWIGLU_TN):
    m, k = a.shape
    n = wg.shape[1]
    return pl.pallas_call(
        _swiglu_kernel,
        grid=(n // tn, m // tm),
        in_specs=[pl.BlockSpec((tm, k), lambda j, i: (i, 0)),
                  pl.BlockSpec((k, tn), lambda j, i: (0, j)),
                  pl.BlockSpec((k, tn), lambda j, i: (0, j))],
        out_specs=pl.BlockSpec((tm, tn), lambda j, i: (i, j)),
        out_shape=jax.ShapeDtypeStruct((m, n), BF16),
        scratch_shapes=[pltpu.VMEM((k, tn), BF16), pltpu.VMEM((k, tn), BF16)],
        compiler_params=_cparams(("arbitrary", "arbitrary")),
        name="swiglu_up",
    )(a, wg, wu)


def _split3(x):
    hi = x.astype(BF16)
    r1 = x - hi.astype(F32)
    mid = r1.astype(BF16)
    lo = (r1 - mid.astype(F32)).astype(BF16)
    return hi, mid, lo


def _decay_kernel(z_ref, b_ref, ckey_ref, *, blk):
    t = z_ref.shape[0]
    r = lax.broadcasted_iota(jnp.int32, (blk, blk), 0)
    c = lax.broadcasted_iota(jnp.int32, (blk, blk), 1)
    tri = jnp.where(r >= c, 1.0, 0.0).astype(BF16)
    lane = lax.broadcasted_iota(jnp.int32, (blk, LANES), 1)
    seg = (lane - LOGIT_LANE) // FOX_HEADS
    carry = jnp.zeros((1, LANES), F32)
    for s in range(t // blk):
        rows = slice(s * blk, (s + 1) * blk)
        z = z_ref[rows, :] + b_ref[...]
        logf = (jnp.minimum(z, 0.0) - jnp.log1p(jnp.exp(-jnp.abs(z)))) * LOG2E
        hi, mid, lo = _split3(logf)
        cb = (jnp.dot(tri, hi, preferred_element_type=F32)
              + jnp.dot(tri, mid, preferred_element_type=F32)
              + jnp.dot(tri, lo, preferred_element_type=F32)) + carry
        carry = cb[blk - 1:blk, :]
        khi = (-cb).astype(BF16).astype(F32)
        kr1 = -cb - khi
        kmid = kr1.astype(BF16).astype(F32)
        klo = kr1 - kmid
        ckey_ref[rows, :] = jnp.where(
            seg == 0, khi,
            jnp.where(seg == 1, pltpu.roll(kmid, FOX_HEADS, 1),
                      jnp.where(seg == 2, pltpu.roll(klo, 2 * FOX_HEADS, 1),
                                jnp.where(lane == LANES - 1, 1.0, 0.0)))).astype(BF16)


def _decay(p3, bias_row, batch, seq, blk=256):
    return pl.pallas_call(
        functools.partial(_decay_kernel, blk=blk),
        grid=(batch,),
        in_specs=[pl.BlockSpec((seq, LANES), lambda b: (b, LOGIT_BLOCK)),
                  pl.BlockSpec((1, LANES), lambda b: (0, 0))],
        out_specs=pl.BlockSpec((seq, LANES), lambda b: (b, 0)),
        out_shape=jax.ShapeDtypeStruct((batch * seq, LANES), BF16),
        compiler_params=_cparams(("arbitrary",)),
        name="fox_decay_cumsum",
    )(p3, bias_row)


def _transpose_bf16(x):
    return x.astype(F32).T.astype(BF16)


def _normalised(acc, d):
    return acc[:d, :] * (1.0 / jnp.maximum(acc[d:d + 1, :], 1e-30))


def _fox_kernel(q_ref, k_ref, v_ref, ckey_ref, o_ref, vt_ref, ka_ref, qa_ref, *, tq, nq, nh):
    hh = pl.program_id(1)
    i = pl.program_id(2)
    dh = FOX_HEAD_DIM

    @pl.when(i == 0)
    def _():
        ck = ckey_ref[...]
        src = lax.broadcasted_iota(jnp.int32, (LANES, LANES), 0)
        dst = lax.broadcasted_iota(jnp.int32, (LANES, LANES), 1)
        for h in range(nh):
            vt_ref[h, 0:dh, :] = _transpose_bf16(v_ref[:, h * dh:(h + 1) * dh])
            vt_ref[h, dh:, :] = jnp.ones((vt_ref.shape[1] - dh, vt_ref.shape[2]), BF16)
            first = LOGIT_LANE + hh * nh + h
            pick = (((dst < 3) & (src == first + dst * FOX_HEADS))
                    | ((dst >= 3) & (dst < 6) & (src == LANES - 1)))
            extra = jnp.dot(ck, jnp.where(pick, 1.0, 0.0).astype(BF16),
                            preferred_element_type=F32)
            ka_ref[h, :, 0:dh] = k_ref[:, h * dh:(h + 1) * dh]
            ka_ref[h, :, dh:] = extra.astype(BF16)
            sel = (jnp.where((src < 3) & (dst == LANES - 1), 1.0, 0.0)
                   - jnp.where((src >= 3) & (src < 6)
                               & (dst == first + (src - 3) * FOX_HEADS), 1.0, 0.0))
            qa_ref[h] = lax.dot_general(sel.astype(BF16), ck, (((1,), (1,)), ((), ())),
                                        preferred_element_type=F32).astype(BF16)

    qtr = [_transpose_bf16(q_ref[:, h * dh:(h + 1) * dh]) for h in range(nh)]
    rk = lax.broadcasted_iota(jnp.int32, (tq, tq), 0)
    cq = lax.broadcasted_iota(jnp.int32, (tq, tq), 1)

    def variant(n):
        def run():
            qts = [jnp.concatenate([qtr[h], qa_ref[h, :, n * tq:(n + 1) * tq]], axis=0)
                   for h in range(nh)]

            def scores(h, k0):
                return jnp.dot(ka_ref[h, k0:k0 + tq, :], qts[h], preferred_element_type=F32)

            starts = [t * tq for t in range(n, -1, -1)]
            tiles = [[jnp.where(rk <= cq, scores(h, starts[0]), NEG)]
                     + [scores(h, k0) for k0 in starts[1:]] for h in range(nh)]
            ms = [jnp.max(tiles[h][0], axis=0, keepdims=True) for h in range(nh)]
            accs = [jnp.dot(vt_ref[h, :, starts[0]:starts[0] + tq],
                            jnp.exp2(tiles[h][0] - ms[h]).astype(BF16),
                            preferred_element_type=F32) for h in range(nh)]
            for t in range(1, n + 1):
                k0 = starts[t]
                for h in range(nh):
                    st = tiles[h][t]
                    m_new = jnp.maximum(ms[h], jnp.max(st, axis=0, keepdims=True))
                    accs[h] = jnp.exp2(ms[h] - m_new) * accs[h] + jnp.dot(
                        vt_ref[h, :, k0:k0 + tq], jnp.exp2(st - m_new).astype(BF16),
                        preferred_element_type=F32)
                    ms[h] = m_new
            for h in range(nh):
                o_ref[:, h * dh:(h + 1) * dh] = _normalised(accs[h], dh).T.astype(o_ref.dtype)
        return run

    lax.switch(i, [variant(n) for n in range(nq)])


def _fox_attention(p1, ckey, batch, seq, tq=FOX_TQ, nh=FOX_HEADS_PER_STEP):
    nq = seq // tq
    hg = FOX_HEADS // nh
    w = nh * FOX_HEAD_DIM
    return pl.pallas_call(
        functools.partial(_fox_kernel, tq=tq, nq=nq, nh=nh),
        grid=(batch, hg, nq),
        in_specs=[pl.BlockSpec((tq, w), lambda b, hh, i: (b * nq + i, hh)),
                  pl.BlockSpec((seq, w), lambda b, hh, i: (b, hg + hh)),
                  pl.BlockSpec((seq, w), lambda b, hh, i: (b, 2 * hg + hh)),
                  pl.BlockSpec((seq, LANES), lambda b, hh, i: (b, 0))],
        out_specs=pl.BlockSpec((tq, w), lambda b, hh, i: (b * nq + i, hh)),
        out_shape=jax.ShapeDtypeStruct((batch * seq, FOX_WIDTH), BF16),
        scratch_shapes=[pltpu.VMEM((nh, FOX_HEAD_DIM + BF16_SUBLANES, seq), BF16),
                        pltpu.VMEM((nh, seq, 2 * FOX_HEAD_DIM), BF16),
                        pltpu.VMEM((nh, FOX_HEAD_DIM, seq), BF16)],
        compiler_params=_cparams(("arbitrary", "arbitrary", "arbitrary")),
        name="fox_attention",
    )(p1, p1, p1, ckey)


def _compress_one(z_refs, pe_ref, w1_ref, w2_ref, nblk):
    half = CMP_BLOCK // 2
    first = jnp.zeros((nblk, CMP_HIDDEN), F32)
    second = jnp.zeros((nblk, CMP_HIDDEN), F32)
    for p in range(half):
        rows = pl.ds(p, nblk, stride=CMP_STRIDE)
        zp = [z_ref[rows, :] for z_ref in z_refs]
        zp = zp[0] if len(zp) == 1 else jnp.concatenate(zp, axis=1)
        first += jnp.dot((zp + pe_ref[p:p + 1, :]).astype(BF16), w1_ref[p],
                         preferred_element_type=F32)
        second += jnp.dot((zp + pe_ref[half + p:half + p + 1, :]).astype(BF16),
                          w1_ref[half + p], preferred_element_type=F32)
    hid = first + pltpu.roll(second, nblk - 1, 0)
    act = (hid * jax.nn.sigmoid(hid)).astype(BF16)
    return jnp.dot(act, w2_ref[...], preferred_element_type=F32)


def _compress_kernel(zk0_ref, zk1_ref, zv_ref, pek_ref, w1k_ref, w2k_ref, gk_ref,
                     pev_ref, w1v_ref, w2v_ref, kc_ref, vc_ref, *, nblk):
    kc = _compress_one((zk0_ref, zk1_ref), pek_ref, w1k_ref, w2k_ref, nblk)
    ms = jnp.sum(kc * kc, axis=-1, keepdims=True) * (1.0 / NSA_QK_DIM)
    kc = kc * lax.rsqrt(ms + RMS_EPS) * gk_ref[...]
    pos = CMP_STRIDE * lax.broadcasted_iota(jnp.int32, (nblk, LANES), 0) + (CMP_BLOCK - 1)
    up = NSA_QK_PAD - LANES
    kc_ref[:, :up] = kc[:, :up].astype(kc_ref.dtype)
    kc_ref[:, up:] = (kc[:, up:] + _key_aug(pos)).astype(kc_ref.dtype)
    vc = _compress_one((zv_ref,), pev_ref, w1v_ref, w2v_ref, nblk)
    vc_ref[...] = vc.T.astype(vc_ref.dtype)


def _compress(p3, pek, w1k, w2k, gk, pev, w1v, w2v, batch, seq):
    g = NSA_KV_GROUPS
    nblk = seq // CMP_STRIDE
    full2 = lambda b, gg: (0, 0)
    full3 = lambda b, gg: (0, 0, 0)
    return pl.pallas_call(
        functools.partial(_compress_kernel, nblk=nblk),
        grid=(batch, g),
        in_specs=[pl.BlockSpec((seq, LANES), lambda b, gg: (b, 2 * gg)),
                  pl.BlockSpec((seq, LANES), lambda b, gg: (b, 2 * gg + 1)),
                  pl.BlockSpec((seq, NSA_V_DIM), lambda b, gg: (b, 4 + gg)),
                  pl.BlockSpec(pek.shape, full2), pl.BlockSpec(w1k.shape, full3),
                  pl.BlockSpec(w2k.shape, full2), pl.BlockSpec(gk.shape, full2),
                  pl.BlockSpec(pev.shape, full2), pl.BlockSpec(w1v.shape, full3),
                  pl.BlockSpec(w2v.shape, full2)],
        out_specs=[pl.BlockSpec((nblk, NSA_QK_PAD), lambda b, gg: (b * g + gg, 0)),
                   pl.BlockSpec((NSA_V_DIM, nblk), lambda b, gg: (b * g + gg, 0))],
        out_shape=[jax.ShapeDtypeStruct((batch * g * nblk, NSA_QK_PAD), BF16),
                   jax.ShapeDtypeStruct((batch * g * NSA_V_DIM, nblk), BF16)],
        compiler_params=_cparams(("arbitrary", "arbitrary")),
        name="nsa_compress",
    )(p3, p3, p3, pek, w1k, w2k, gk, pev, w1v, w2v)


def _q_heads_t(qt_ref, first_head):
    return jnp.concatenate(
        [qt_ref[(first_head + hh) * NSA_QK_PAD:(first_head + hh + 1) * NSA_QK_PAD, :]
         for hh in range(NSA_HPG)], axis=1)


def _gate_rows(glt_ref, g, hh):
    base = LOGIT_LANE + FOX_HEADS + (g * NSA_HPG + hh) * 3
    return [glt_ref[pl.ds(base + br, 1), :] for br in range(3)]


def _nsa_select_kernel(q_ref, kc_ref, vct_ref, gl_ref, ovt_ref, xn_ref, wk_ref, wv_ref, gk_ref,
                       sel_ref, ocmp_ref, pk_ref, pv_ref, glt_ref, wkb_ref, wvb_ref,
                       *, tq, n_cmp, n_sel):
    g = pl.program_id(1)
    t0 = pl.program_id(2) * tq
    hpg = NSA_HPG
    dv = NSA_V_DIM

    @pl.when(pl.program_id(2) == 0)
    def _():
        _stage_weight(wk_ref, wkb_ref, True, NSA_QK_DIM, NSA_QK_PAD)
        _stage_weight(wv_ref, wvb_ref, True)

    xn = xn_ref[...]
    yk = jnp.dot(xn, wkb_ref[...], preferred_element_type=F32)
    pv_ref[...] = jnp.dot(xn, wvb_ref[...], preferred_element_type=F32).astype(pv_ref.dtype)
    pos = t0 + lax.broadcasted_iota(jnp.int32, (tq, LANES), 0)
    for c in range(NSA_KV_GROUPS):
        sl = slice(c * NSA_QK_PAD, (c + 1) * NSA_QK_PAD)
        yc = yk[:, sl]
        rs = lax.rsqrt(jnp.sum(yc * yc, axis=-1, keepdims=True) * (1.0 / NSA_QK_DIM) + RMS_EPS)
        out = yc * rs * gk_ref[:, sl]
        up = slice((c + 1) * NSA_QK_PAD - LANES, (c + 1) * NSA_QK_PAD)
        pk_ref[:, c * NSA_QK_PAD:(c + 1) * NSA_QK_PAD - LANES] = (
            out[:, :NSA_QK_PAD - LANES].astype(pk_ref.dtype))
        pk_ref[:, up] = (out[:, NSA_QK_PAD - LANES:] + _key_aug(pos)).astype(pk_ref.dtype)

    q4t = _q_heads_t(q_ref, 0)
    rk = lax.broadcasted_iota(jnp.int32, (LANES, tq), 0)
    cq = lax.broadcasted_iota(jnp.int32, (LANES, tq), 1)

    s_c = jnp.dot(kc_ref[...], q4t, preferred_element_type=F32)
    dist_c = (t0 + cq) - (CMP_STRIDE * rk + (CMP_BLOCK - 1))
    mask_c = jnp.where(rk < n_cmp, dist_c, -1) >= 0
    probs = []
    p_sum = jnp.zeros((LANES, tq), F32)
    for hh in range(hpg):
        sm = jnp.where(mask_c, s_c[:, hh * tq:(hh + 1) * tq], NEG)
        m = jnp.max(sm, axis=0, keepdims=True)
        e = jnp.where(mask_c, jnp.exp2(sm - m), 0.0)
        p = e * (1.0 / jnp.maximum(jnp.sum(e, axis=0, keepdims=True), 1e-30))
        probs.append(p)
        p_sum = p_sum + p
    o_cmp = jnp.dot(vct_ref[...], jnp.concatenate(probs, axis=1).astype(BF16),
                    preferred_element_type=F32)

    ph = p_sum.astype(BF16)
    plo = (p_sum - ph.astype(F32)).astype(BF16)
    ovt = ovt_ref[...]
    imp = (jnp.dot(ovt, ph, preferred_element_type=F32)
           + jnp.dot(ovt, plo, preferred_element_type=F32))[:n_sel, :]

    rj = lax.broadcasted_iota(jnp.int32, (n_sel, tq), 0)
    tcol = t0 + lax.broadcasted_iota(jnp.int32, (n_sel, tq), 1)
    back = (tcol >> (SEL_BLOCK.bit_length() - 1)) - rj
    elig = back >= 0
    forced = jnp.where(rj == 0, 0, jnp.where(elig, back, SEL_LOCAL)) < SEL_LOCAL
    score = jnp.where(elig, jnp.where(forced, FORCE_SCORE, imp), -1.0)
    rank = jnp.zeros((n_sel, tq), F32)
    for jp in range(n_sel):
        row = score[jp:jp + 1, :]
        later = jnp.where(rj > jp, 1.0, 0.0)
        rank = rank + jnp.where(row > score, 1.0, jnp.where(row == score, later, 0.0))
    sel_ref[0:n_sel, :] = jnp.where(elig, jnp.where(rank < SEL_TOPK, 0.0, NEG), NEG)
    sel_ref[n_sel:, :] = jnp.full((LANES - n_sel, tq), NEG, F32)

    glt_ref[...] = jax.nn.sigmoid(gl_ref[...]).T
    for hh in range(hpg):
        gate = _gate_rows(glt_ref, g, hh)[0]
        ocmp_ref[:, hh * dv:(hh + 1) * dv] = (gate * o_cmp[:, hh * tq:(hh + 1) * tq]).T


def _nsa_select(pq, p3, kc, vct, ovt, xn, wt, k_rows, v_rows, gain_k, batch, seq,
                tq=NSA_SELECT_TQ):
    g = NSA_KV_GROUPS
    nq = seq // tq
    nblk = seq // CMP_STRIDE
    n_cmp = nblk - CMP_BLOCK // CMP_STRIDE + 1
    d = xn.shape[1]
    kw_out = g * NSA_QK_PAD
    return pl.pallas_call(
        functools.partial(_nsa_select_kernel, tq=tq, n_cmp=n_cmp, n_sel=seq // SEL_BLOCK),
        grid=(batch, g, nq),
        in_specs=[
            pl.BlockSpec((NSA_HPG * NSA_QK_PAD, tq), lambda b, gg, i: (gg, b * nq + i)),
            pl.BlockSpec((nblk, NSA_QK_PAD), lambda b, gg, i: (b * g + gg, 0)),
            pl.BlockSpec((NSA_V_DIM, nblk), lambda b, gg, i: (b * g + gg, 0)),
            pl.BlockSpec((tq, LANES), lambda b, gg, i: (b * nq + i, LOGIT_BLOCK)),
            pl.BlockSpec(ovt.shape, lambda b, gg, i: (0, 0)),
            pl.BlockSpec((tq, d), lambda b, gg, i: (b * nq + i, 0)),
            _row_window(k_rows, KV_K, d, grid_arg=1),
            _row_window(v_rows, KV_V, d, grid_arg=1),
            pl.BlockSpec((1, kw_out), lambda b, gg, i: (0, gg)),
        ],
        out_specs=[pl.BlockSpec((LANES, tq), lambda b, gg, i: (b * g + gg, i)),
                   pl.BlockSpec((tq, NSA_HPG * NSA_V_DIM), lambda b, gg, i: (b * nq + i, gg)),
                   pl.BlockSpec((tq, kw_out), lambda b, gg, i: (b * nq + i, gg)),
                   pl.BlockSpec((tq, KV_V), lambda b, gg, i: (b * nq + i, gg))],
        out_shape=[jax.ShapeDtypeStruct((batch * g * LANES, seq), F32),
                   jax.ShapeDtypeStruct((batch * seq, NSA_WIDTH), F32),
                   jax.ShapeDtypeStruct((batch * seq, 2 * kw_out), BF16),
                   jax.ShapeDtypeStruct((batch * seq, 2 * KV_V), BF16)],
        scratch_shapes=[pltpu.VMEM((LANES, tq), F32), pltpu.VMEM((d, kw_out), BF16),
                        pltpu.VMEM((d, KV_V), BF16)],
        compiler_params=_cparams(("arbitrary", "arbitrary", "arbitrary")),
        name="nsa_select",
    )(pq, kc, vct, p3, ovt, xn, wt, wt, gain_k.reshape(1, 2 * kw_out))


def _nsa_attend_kernel(q_ref, ks_ref, vs_ref, kw_ref, vw_ref, gl_ref, sel_ref,
                       ocmp_ref, o_ref, vst_ref, vwt_ref, glt_ref, *, tq, n_var):
    i = pl.program_id(1)
    t0 = i * tq
    hpg = NSA_HPG
    ng = NSA_KV_GROUPS
    dv = NSA_V_DIM
    dk = NSA_QK_PAD

    @pl.when(i == 0)
    def _():
        ones = jnp.ones((vst_ref.shape[1] - dv, vst_ref.shape[2]), BF16)
        for gg in range(ng):
            vst_ref[gg, 0:dv, :] = _transpose_bf16(vs_ref[:, gg * dv:(gg + 1) * dv])
            vst_ref[gg, dv:, :] = ones
            vwt_ref[gg, 0:dv, :] = _transpose_bf16(vw_ref[:, gg * dv:(gg + 1) * dv])
            vwt_ref[gg, dv:, :] = ones

    q4t = [_q_heads_t(q_ref, gg * hpg) for gg in range(ng)]
    glt_ref[...] = jax.nn.sigmoid(gl_ref[...]).T

    def distance(k0, rows):
        return (lax.broadcasted_iota(jnp.int32, (rows, tq), 1)
                - lax.broadcasted_iota(jnp.int32, (rows, tq), 0)) + (t0 - k0)

    def scores(gg, k, mask_bias):
        st = jnp.dot(k, q4t[gg], preferred_element_type=F32)
        return st + jnp.concatenate([mask_bias] * hpg, axis=1)

    def selection_bias(gg, k0):
        j0 = gg * LANES + k0 // SEL_BLOCK
        return jnp.concatenate(
            [jnp.broadcast_to(sel_ref[j0 + j:j0 + j + 1, :], (SEL_BLOCK, tq))
             for j in range(SLC_TILE // SEL_BLOCK)], axis=0)

    def variant(n):
        def run():
            kw0 = pl.multiple_of(jnp.maximum(t0 - WINDOW, 0), LANES)
            wrows = WINDOW + tq
            dist_w = distance(kw0, wrows)
            bias_w = jnp.where(jnp.where(dist_w >= 0, dist_w, WINDOW) < WINDOW, 0.0, NEG)
            o_win = []
            for gg in range(ng):
                sw = scores(gg, kw_ref[pl.ds(kw0, wrows), gg * dk:(gg + 1) * dk], bias_w)
                pw = jnp.exp2(sw - jnp.max(sw, axis=0, keepdims=True))
                o_win.append(_normalised(
                    jnp.dot(vwt_ref[gg, :, pl.ds(kw0, wrows)], pw.astype(BF16),
                            preferred_element_type=F32), dv))

            starts = [t * SLC_TILE for t in range(n, -1, -1)]
            causal = distance(starts[0], SLC_TILE) >= 0
            tiles = [[] for _ in range(ng)]
            for t, k0 in enumerate(starts):
                for gg in range(ng):
                    bias = selection_bias(gg, k0)
                    if t == 0:
                        bias = jnp.where(causal, bias, NEG)
                    tiles[gg].append(scores(
                        gg, ks_ref[k0:k0 + SLC_TILE, gg * dk:(gg + 1) * dk], bias))
            ms = [jnp.max(tiles[gg][0], axis=0, keepdims=True) for gg in range(ng)]
            accs = [jnp.dot(vst_ref[gg, :, starts[0]:starts[0] + SLC_TILE],
                            jnp.exp2(tiles[gg][0] - ms[gg]).astype(BF16),
                            preferred_element_type=F32) for gg in range(ng)]
            for t in range(1, n + 1):
                k0 = starts[t]
                for gg in range(ng):
                    st = tiles[gg][t]
                    m_new = jnp.maximum(ms[gg], jnp.max(st, axis=0, keepdims=True))
                    accs[gg] = jnp.exp2(ms[gg] - m_new) * accs[gg] + jnp.dot(
                        vst_ref[gg, :, k0:k0 + SLC_TILE], jnp.exp2(st - m_new).astype(BF16),
                        preferred_element_type=F32)
                    ms[gg] = m_new

            for gg in range(ng):
                o_slc = _normalised(accs[gg], dv)
                for hh in range(hpg):
                    _, g_slc, g_win = _gate_rows(glt_ref, gg, hh)
                    lanes = slice(hh * tq, (hh + 1) * tq)
                    out = g_slc * o_slc[:, lanes] + g_win * o_win[gg][:, lanes]
                    cols = slice((gg * hpg + hh) * dv, (gg * hpg + hh + 1) * dv)
                    o_ref[:, cols] = (ocmp_ref[:, cols] + out.T).astype(o_ref.dtype)
        return run

    lax.switch(t0 // SLC_TILE, [variant(n) for n in range(n_var)])


def _nsa_attend(pq, pk, pv, p3, sel, ocmp, batch, seq):
    tq = NSA_ATTEND_TQ
    g = NSA_KV_GROUPS
    nq = seq // tq
    vrows = NSA_V_DIM + BF16_SUBLANES
    return pl.pallas_call(
        functools.partial(_nsa_attend_kernel, tq=tq, n_var=seq // SLC_TILE),
        grid=(batch, nq),
        in_specs=[
            pl.BlockSpec((NSA_HEADS * NSA_QK_PAD, tq), lambda b, i: (0, b * nq + i)),
            pl.BlockSpec((seq, g * NSA_QK_PAD), lambda b, i: (b, 0)),
            pl.BlockSpec((seq, g * NSA_V_DIM), lambda b, i: (b, 0)),
            pl.BlockSpec((seq, g * NSA_QK_PAD), lambda b, i: (b, 1)),
            pl.BlockSpec((seq, g * NSA_V_DIM), lambda b, i: (b, 1)),
            pl.BlockSpec((tq, LANES), lambda b, i: (b * nq + i, LOGIT_BLOCK)),
            pl.BlockSpec((g * LANES, tq), lambda b, i: (b, i)),
            pl.BlockSpec((tq, NSA_WIDTH), lambda b, i: (b * nq + i, 0)),
        ],
        out_specs=pl.BlockSpec((tq, NSA_WIDTH), lambda b, i: (b * nq + i, 0)),
        out_shape=jax.ShapeDtypeStruct((batch * seq, NSA_WIDTH), BF16),
        scratch_shapes=[pltpu.VMEM((g, vrows, seq), BF16), pltpu.VMEM((g, vrows, seq), BF16),
                        pltpu.VMEM((LANES, tq), F32)],
        compiler_params=_cparams(("arbitrary", "arbitrary")),
        name="nsa_attend",
    )(pq, pk, pv, pk, pv, p3, sel, ocmp)


def _pad_gain(gain, scale=1.0):
    return jnp.pad(gain * scale, (0, NSA_QK_PAD - NSA_QK_DIM))


def _overlap_matrix(nc, ns):
    i = np.arange(nc)[:, None]
    j = np.arange(ns)[None, :]
    lo = np.maximum(i * CMP_STRIDE, j * SEL_BLOCK)
    hi = np.minimum(i * CMP_STRIDE + CMP_BLOCK, (j + 1) * SEL_BLOCK)
    return (np.maximum(hi - lo, 0) / CMP_STRIDE).astype(np.float32)


def kernel(x, norm_attn, w_in, fox_f_bias, fox_q_gain, fox_k_gain,
           nsa_q_gain, nsa_kc_gain, nsa_ks_gain, nsa_kw_gain,
           cmp_pe_k, cmp_w1_k, cmp_w2_k, cmp_pe_v, cmp_w1_v, cmp_w2_v,
           w_up_fox, w_up_nsa, w_out, norm_ffn, w_ffn_gate, w_ffn_up, w_ffn_down):
    batch, seq, d = x.shape
    m = batch * seq
    depth = w_in.shape[0]
    pts = [0] + [int(p) for p in np.cumsum(IN_SPLITS)]
    nblk = seq // CMP_STRIDE
    n_cmp = nblk - CMP_BLOCK // CMP_STRIDE + 1
    ns = seq // SEL_BLOCK

    slope = jnp.exp2(-8.0 * jnp.arange(1, NSA_HEADS + 1, dtype=F32) / NSA_HEADS) * LOG2E
    s1, s2, s3 = [p.astype(F32) for p in _split3(slope)]
    q_spare = jnp.stack([256.0 * s1, 256.0 * s2, 256.0 * s3, s1, s2, s3, -slope, -slope, -slope],
                        axis=1)
    q_spare = jnp.pad(q_spare, ((0, 0), (0, NSA_QK_PAD - NSA_QK_DIM - 9)))
    ovt_np = np.zeros((LANES, nblk), np.float32)
    ovt_np[:ns, :n_cmp] = _overlap_matrix(n_cmp, ns).T
    ovt = jnp.asarray(ovt_np, BF16)

    w_in_t = jnp.swapaxes(w_in, 1, 2)

    xf = x.reshape(m, d)
    for l in range(depth):
        wt = w_in_t[l]
        row = dict(zip(("fq", "fk", "fv", "fl", "nq", "kc", "vc", "ks", "vs", "kw", "vw", "ng",
                        "ga", "gb"), pts))

        gain1 = jnp.concatenate([jnp.tile(fox_q_gain[l] * (FOX_HEAD_DIM ** -0.5 * LOG2E), FOX_HEADS),
                                 jnp.tile(fox_k_gain[l], FOX_HEADS),
                                 jnp.ones((FOX_WIDTH,), F32)])
        flag1 = jnp.concatenate([jnp.ones((2 * FOX_WIDTH,), F32), jnp.zeros((FOX_WIDTH,), F32)])
        q_gain = jnp.broadcast_to(nsa_q_gain[l] * (NSA_QK_DIM ** -0.5 * LOG2E),
                                  (NSA_HEADS, NSA_QK_DIM))
        q_table = jnp.broadcast_to(
            jnp.concatenate([q_gain, q_spare], axis=1).reshape(NSA_HEADS * NSA_QK_PAD, 1),
            (NSA_HEADS * NSA_QK_PAD, LANES))
        gain_k = jnp.concatenate([jnp.tile(_pad_gain(nsa_ks_gain[l]), NSA_KV_GROUPS),
                                  jnp.tile(_pad_gain(nsa_kw_gain[l]), NSA_KV_GROUPS)])
        pad_d = NSA_QK_PAD - NSA_QK_DIM
        n_small = FOX_HEADS + 3 * NSA_HEADS
        w3 = jnp.concatenate([wt[row["kc"]:row["kc"] + NSA_QK_DIM],
                              wt[row["fl"]:row["nq"]], wt[row["ng"]:row["ga"]],
                              jnp.zeros((pad_d - n_small, d), F32),
                              wt[row["kc"] + NSA_QK_DIM:row["vc"]], jnp.zeros((pad_d, d), F32),
                              wt[row["vc"]:row["ks"]]], axis=0)

        xn, p3 = _rmsnorm_project(xf, norm_attn[l], w3)
        q_tile = NSA_HPG * NSA_QK_DIM
        p1 = _project(xn, wt, list(range(0, 3 * FOX_WIDTH, FOX_PROJ_TN)), FOX_PROJ_TN, gain1,
                      flag1, PROJ_TM, "proj_fox", group=FOX_HEAD_DIM)
        pq = _project_qt(xn, wt, [row["nq"], row["nq"] + q_tile], q_tile, q_table, PROJ_TM, seq,
                         "proj_nsa_q")

        bias_row = jnp.pad(fox_f_bias[l], (LOGIT_LANE, LANES - LOGIT_LANE - FOX_HEADS)).reshape(1, LANES)
        ckey = _decay(p3, bias_row, batch, seq)
        o_a = _fox_attention(p1, ckey, batch, seq)

        pek = jnp.pad(cmp_pe_k[l], ((0, 0), (0, pad_d)))
        w1k = jnp.pad(cmp_w1_k[l].reshape(CMP_BLOCK, NSA_QK_DIM, CMP_HIDDEN),
                      ((0, 0), (0, pad_d), (0, 0))).astype(BF16)
        w2k = jnp.pad(cmp_w2_k[l], ((0, 0), (0, pad_d))).astype(BF16)
        gk = _pad_gain(nsa_kc_gain[l]).reshape(1, NSA_QK_PAD)
        w1v = cmp_w1_v[l].reshape(CMP_BLOCK, NSA_V_DIM, CMP_HIDDEN).astype(BF16)
        w2v = cmp_w2_v[l].astype(BF16)
        kc, vct = _compress(p3, pek, w1k, w2k, gk, cmp_pe_v[l], w1v, w2v, batch, seq)
        sel, ocmp, pk, pv = _nsa_select(pq, p3, kc, vct, ovt, xn, wt, [row["ks"], row["kw"]],
                                        [row["vs"], row["vw"]], gain_k, batch, seq)
        o_b = _nsa_attend(pq, pk, pv, p3, sel, ocmp, batch, seq)

        merged = _merge(xn, o_a, o_b, wt, row["ga"], row["gb"], w_up_fox[l], w_up_nsa[l])
        hres, hn = _out_proj_norm(merged, w_out[l], xf, norm_ffn[l])

        act = _swiglu(hn, w_ffn_gate[l], w_ffn_up[l])
        xf = _matmul_residual(act, w_ffn_down[l], hres, FFN_DOWN_TM, FFN_DOWN_TN, "ffn_down")
    return xf.reshape(batch, seq, d)
```
